```python
import jax
import jax.numpy as jnp
from jax import lax
import numpy as np

D_MODEL = 1024
BATCH = 32
SEQ = 256
DEPTH = 2
DEC_BATCH = 4
DEC_SEQ = 2048
PAST_LEN = 256

GRID_W = 64
N_MIXERS = 2
N_NA_LAYERS = (DEPTH + 1) // 2
N_MLA_LAYERS = DEPTH // 2
NA_HEADS = 16
NA_HEAD_DIM = D_MODEL // NA_HEADS
NA_KH = 8
NA_KW = 16
NA_QB = NA_KW
NA_KB = 2 * NA_KW
MLA_HEADS = 16
MLA_Q_RANK = 384
MLA_KV_RANK = 256
MLA_NOPE_DIM = 64
MLA_ROPE_DIM = 32
MLA_V_DIM = 64
MLA_QK_DIM = MLA_NOPE_DIM + MLA_ROPE_DIM
ROPE_THETA = 10000.0
Q_BLOCK = 128
N_GROUPS = 4
EXPERTS_PER_GROUP = 4
N_EXPERTS = N_GROUPS * EXPERTS_PER_GROUP
TOP_K_IN_GROUP = 2
D_FF_EXPERT = 256
N_MOD = 6
EPS = 1e-6
NEG_INF = -1e30

kernel_name = "hybrid_natten_mla_hmoe_diffusion_step"


def rms_norm(x, g):
    xf = x.astype(jnp.float32)
    y = xf * lax.rsqrt(jnp.mean(xf * xf, axis=-1, keepdims=True) + EPS)
    return (y * g.astype(jnp.float32)).astype(x.dtype)


def ada_params(cond, w, b):
    m = (jax.nn.silu(cond) @ w + b)[..., None, :]
    return jnp.split(m, N_MOD, axis=-1)


def modulate(h, shift, scale):
    return h * (1 + scale) + shift


def axial_rope_tables(n_tokens):
    n_freq = MLA_ROPE_DIM // 4
    inv = 1.0 / (ROPE_THETA ** (jnp.arange(n_freq, dtype=jnp.float32) / n_freq))
    t = jnp.arange(n_tokens)
    row = (t // GRID_W).astype(jnp.float32)
    col = (t % GRID_W).astype(jnp.float32)
    ang = jnp.concatenate([row[:, None] * inv, col[:, None] * inv], axis=-1)
    return jnp.cos(ang), jnp.sin(ang)


def apply_rope(x, cos, sin):
    x1, x2 = jnp.split(x, 2, axis=-1)
    return jnp.concatenate([x1 * cos - x2 * sin, x2 * cos + x1 * sin], axis=-1).astype(x.dtype)


def block_attention(q, k, v, scale):
    b, tq, h, dk = q.shape
    nb = tq // Q_BLOCK
    qb = jnp.moveaxis(q.reshape(b, nb, Q_BLOCK, h, dk), 1, 0)

    def one(qi):
        s = jnp.einsum('bqhd,bkhd->bhqk', qi, k, preferred_element_type=jnp.float32) * scale
        p = jax.nn.softmax(s, axis=-1).astype(v.dtype)
        return jnp.einsum('bhqk,bkhd->bqhd', p, v)

    o = lax.map(one, qb)
    return jnp.moveaxis(o, 0, 1).reshape(b, tq, h, v.shape[-1])


def na_window_indices(rows):
    kh = min(NA_KH, rows)
    n_cb = GRID_W // NA_QB
    r = np.arange(rows)
    row_idx = np.clip(r - kh // 2, 0, rows - kh)[:, None] + np.arange(kh)[None, :]
    q_col = np.arange(n_cb)[:, None] * NA_QB + np.arange(NA_QB)[None, :]
    cb_start = np.clip(np.arange(n_cb) * NA_QB - NA_KW // 2, 0, GRID_W - NA_KB)
    col_idx = cb_start[:, None] + np.arange(NA_KB)[None, :]
    col_start = np.clip(q_col - NA_KW // 2, 0, GRID_W - NA_KW)
    kc = col_idx[:, None, :]
    col_valid = (kc >= col_start[:, :, None]) & (kc < col_start[:, :, None] + NA_KW)
    dy_idx = row_idx - r[:, None] + (NA_KH - 1)
    dx_idx = np.clip(kc - q_col[:, :, None] + (NA_KW - 1), 0, 2 * NA_KW - 2)
    return row_idx, col_idx, col_valid, dy_idx, dx_idx


def na_context(h, w_qkv, w_o):
    b, t, _ = h.shape
    q, k, v = [a.reshape(b, t, NA_HEADS, NA_HEAD_DIM) for a in jnp.split(h @ w_qkv, 3, axis=-1)]
    o = block_attention(q, k, v, NA_HEAD_DIM ** -0.5)
    return o.reshape(b, t, D_MODEL) @ w_o, k, v


def na_latent(h, k_ctx, v_ctx, w_qkv, rpb, w_o, idx):
    row_idx, col_idx, col_valid, dy_idx, dx_idx = idx
    b, t, _ = h.shape
    rows = t // GRID_W
    n_cb = GRID_W // NA_QB
    scale = NA_HEAD_DIM ** -0.5
    q, k, v = jnp.split(h @ w_qkv, 3, axis=-1)
    q = q.reshape(b, rows, n_cb, NA_QB, NA_HEADS, NA_HEAD_DIM)
    k = k.reshape(b, rows, GRID_W, NA_HEADS, NA_HEAD_DIM)
    v = v.reshape(b, rows, GRID_W, NA_HEADS, NA_HEAD_DIM)
    ri = row_idx[:, None, :, None]
    ci = col_idx[None, :, None, :]
    k_win = k[:, ri, ci]
    v_win = v[:, ri, ci]
    s_win = jnp.einsum('brnqhd,brnykhd->bhrnqyk', q, k_win, preferred_element_type=jnp.float32) * scale
    bias = rpb[:, dy_idx[:, None, None, :, None], dx_idx[None, :, :, None, :]]
    s_win = jnp.where(col_valid[:, :, None, :], s_win + bias.astype(jnp.float32), NEG_INF)
    s_ctx = jnp.einsum('brnqhd,bchd->bhrnqc', q, k_ctx, preferred_element_type=jnp.float32) * scale
    n_win = row_idx.shape[1] * NA_KB
    s = jnp.concatenate([s_win.reshape(*s_win.shape[:5], n_win), s_ctx], axis=-1)
    p = jax.nn.softmax(s, axis=-1).astype(v.dtype)
    p_win = p[..., :n_win].reshape(s_win.shape)
    p_ctx = p[..., n_win:]
    o = (jnp.einsum('bhrnqyk,brnykhd->brnqhd', p_win, v_win)
         + jnp.einsum('bhrnqc,bchd->brnqhd', p_ctx, v_ctx))
    return o.reshape(b, t, D_MODEL) @ w_o


def mla_project(h, w_in, q_norm_g, kv_norm_g, w_uq):
    b, t, _ = h.shape
    c_q, c_kv, k_rope = jnp.split(h @ w_in, [MLA_Q_RANK, MLA_Q_RANK + MLA_KV_RANK], axis=-1)
    q = (rms_norm(c_q, q_norm_g) @ w_uq).reshape(b, t, MLA_HEADS, MLA_QK_DIM)
    return q, rms_norm(c_kv, kv_norm_g), k_rope


def mla_keys_values(c_kv, k_rope, w_uk, w_uv):
    b, t, _ = c_kv.shape
    k_nope = (c_kv @ w_uk).reshape(b, t, MLA_HEADS, MLA_NOPE_DIM)
    k_r = jnp.broadcast_to(k_rope[:, :, None, :], (b, t, MLA_HEADS, MLA_ROPE_DIM))
    v = (c_kv @ w_uv).reshape(b, t, MLA_HEADS, MLA_V_DIM)
    return jnp.concatenate([k_nope, k_r], axis=-1), v


def mla_context(h, w_in, q_norm_g, kv_norm_g, w_uq, w_uk, w_uv, w_o):
    b, t, _ = h.shape
    q, c_kv, k_rope = mla_project(h, w_in, q_norm_g, kv_norm_g, w_uq)
    k, v = mla_keys_values(c_kv, k_rope, w_uk, w_uv)
    o = block_attention(q, k, v, MLA_QK_DIM ** -0.5)
    return o.reshape(b, t, MLA_HEADS * MLA_V_DIM) @ w_o, c_kv, k_rope


def mla_latent(h, ckv_ctx, kr_ctx, cos, sin, w_in, q_norm_g, kv_norm_g, w_uq, w_uk, w_uv, w_o):
    b, t, _ = h.shape
    q, c_kv, k_rope = mla_project(h, w_in, q_norm_g, kv_norm_g, w_uq)
    q = jnp.concatenate([q[..., :MLA_NOPE_DIM],
                         apply_rope(q[..., MLA_NOPE_DIM:], cos[:, None, :], sin[:, None, :])], axis=-1)
    k_lat, v_lat = mla_keys_values(c_kv, apply_rope(k_rope, cos, sin), w_uk, w_uv)
    k_ctx, v_ctx = mla_keys_values(ckv_ctx, kr_ctx, w_uk, w_uv)
    k = jnp.concatenate([k_lat, k_ctx], axis=1)
    v = jnp.concatenate([v_lat, v_ctx], axis=1)
    o = block_attention(q, k, v, MLA_QK_DIM ** -0.5)
    return o.reshape(b, t, MLA_HEADS * MLA_V_DIM) @ w_o


def hier_moe(h, w_group, b_group, w_expert, b_expert, w_gate, w_up, w_down):
    shp = h.shape
    x = h.reshape(-1, D_MODEL)
    n = x.shape[0]
    g_logits = jnp.dot(x, w_group, preferred_element_type=jnp.float32) + b_group.astype(jnp.float32)
    g_sel = jnp.argmax(g_logits, axis=-1)
    p_group = jnp.take_along_axis(jax.nn.softmax(g_logits, axis=-1), g_sel[:, None], axis=-1)
    e_logits = (jnp.dot(x, w_expert, preferred_element_type=jnp.float32)
                + b_expert.astype(jnp.float32)).reshape(n, N_GROUPS, EXPERTS_PER_GROUP)
    e_in = jnp.take_along_axis(e_logits, g_sel[:, None, None], axis=1)[:, 0]
    top_val, top_idx = lax.top_k(e_in, TOP_K_IN_GROUP)
    w_top = jax.nn.softmax(top_val, axis=-1) * p_group
    ids = g_sel[:, None] * EXPERTS_PER_GROUP + top_idx
    gates = jnp.einsum('nk,nke->ne', w_top, jax.nn.one_hot(ids, N_EXPERTS, dtype=jnp.float32)).astype(x.dtype)
    hid = jax.nn.silu(jnp.einsum('nd,edf->nef', x, w_gate)) * jnp.einsum('nd,edf->nef', x, w_up)
    y = jnp.einsum('nef,efd->nd', hid * gates[:, :, None], w_down)
    return y.reshape(shp)


def setup_inputs(seed: int = 0) -> dict:
    key = jax.random.key(seed)
    ks = jax.random.split(key, 32)

    def nrm(i, shape, scale):
        return jax.random.normal(ks[i], shape, jnp.float32) * scale

    d = D_MODEL
    return {
        'x_prompt': nrm(0, (BATCH, SEQ, d), 1.0),
        'x_sample': nrm(1, (DEC_BATCH, DEC_SEQ, d), 1.0),
        'state_na_k': nrm(2, (DEC_BATCH, N_NA_LAYERS, PAST_LEN, NA_HEADS, NA_HEAD_DIM), 1.0),
        'state_na_v': nrm(3, (DEC_BATCH, N_NA_LAYERS, PAST_LEN, NA_HEADS, NA_HEAD_DIM), 1.0),
        'state_mla_ckv': nrm(4, (DEC_BATCH, N_MLA_LAYERS, PAST_LEN, MLA_KV_RANK), 1.0),
        'state_mla_kr': nrm(5, (DEC_BATCH, N_MLA_LAYERS, PAST_LEN, MLA_ROPE_DIM), 1.0),
        'c': nrm(6, (DEC_BATCH, d), 1.0),
        'c_ctx': nrm(7, (d,), 1.0),
        'ada_w': nrm(8, (DEPTH, d, N_MOD * d), d ** -0.5),
        'ada_b': nrm(9, (DEPTH, N_MOD * d), 0.02),
        'norm_mix_g': 1.0 + nrm(10, (DEPTH, d), 0.02),
        'norm_ffn_g': 1.0 + nrm(11, (DEPTH, d), 0.02),
        'final_norm_g': 1.0 + nrm(12, (d,), 0.02),
        'na_w_qkv': nrm(13, (N_NA_LAYERS, d, 3 * d), d ** -0.5),
        'na_w_o': nrm(14, (N_NA_LAYERS, d, d), d ** -0.5),
        'na_rpb': nrm(15, (N_NA_LAYERS, NA_HEADS, 2 * NA_KH - 1, 2 * NA_KW - 1), 0.1),
        'mla_w_in': nrm(16, (N_MLA_LAYERS, d, MLA_Q_RANK + MLA_KV_RANK + MLA_ROPE_DIM), d ** -0.5),
        'mla_q_norm_g': 1.0 + nrm(17, (N_MLA_LAYERS, MLA_Q_RANK), 0.02),
        'mla_kv_norm_g': 1.0 + nrm(18, (N_MLA_LAYERS, MLA_KV_RANK), 0.02),
        'mla_w_uq': nrm(19, (N_MLA_LAYERS, MLA_Q_RANK, MLA_HEADS * MLA_QK_DIM), MLA_Q_RANK ** -0.5),
        'mla_w_uk': nrm(20, (N_MLA_LAYERS, MLA_KV_RANK, MLA_HEADS * MLA_NOPE_DIM), MLA_KV_RANK ** -0.5),
        'mla_w_uv': nrm(21, (N_MLA_LAYERS, MLA_KV_RANK, MLA_HEADS * MLA_V_DIM), MLA_KV_RANK ** -0.5),
        'mla_w_o': nrm(22, (N_MLA_LAYERS, MLA_HEADS * MLA_V_DIM, d), (MLA_HEADS * MLA_V_DIM) ** -0.5),
        'moe_w_group': nrm(23, (DEPTH, d, N_GROUPS), d ** -0.5),
        'moe_b_group': nrm(24, (DEPTH, N_GROUPS), 0.01),
        'moe_w_expert': nrm(25, (DEPTH, d, N_EXPERTS), d ** -0.5),
        'moe_b_expert': nrm(26, (DEPTH, N_EXPERTS), 0.01),
        'moe_w_gate': nrm(27, (DEPTH, N_EXPERTS, d, D_FF_EXPERT), d ** -0.5),
        'moe_w_up': nrm(28, (DEPTH, N_EXPERTS, d, D_FF_EXPERT), d ** -0.5),
        'moe_w_down': nrm(29, (DEPTH, N_EXPERTS, D_FF_EXPERT, d), D_FF_EXPERT ** -0.5),
    }


def reference(x_prompt, x_sample, state_na_k, state_na_v, state_mla_ckv, state_mla_kr, c, c_ctx,
              ada_w, ada_b, norm_mix_g, norm_ffn_g, final_norm_g,
              na_w_qkv, na_w_o, na_rpb,
              mla_w_in, mla_q_norm_g, mla_kv_norm_g, mla_w_uq, mla_w_uk, mla_w_uv, mla_w_o,
              moe_w_group, moe_b_group, moe_w_expert, moe_b_expert, moe_w_gate, moe_w_up, moe_w_down):
    n_lat = x_sample.shape[1]
    rows = n_lat // GRID_W
    cos, sin = axial_rope_tables(n_lat)
    na_idx = na_window_indices(rows)
    xp, xs = x_prompt, x_sample
    na_k_out, na_v_out, ckv_out, kr_out = [], [], [], []
    for i in range(DEPTH):
        mod_p = ada_params(c_ctx, ada_w[i], ada_b[i])
        mod_s = ada_params(c, ada_w[i], ada_b[i])
        hp = modulate(rms_norm(xp, norm_mix_g[i]), mod_p[0], mod_p[1])
        hs = modulate(rms_norm(xs, norm_mix_g[i]), mod_s[0], mod_s[1])
        j = i // N_MIXERS
        if i % N_MIXERS == 0:
            yp, k_ctx, v_ctx = na_context(hp, na_w_qkv[j], na_w_o[j])
            ys = na_latent(hs, state_na_k[:, j], state_na_v[:, j], na_w_qkv[j], na_rpb[j], na_w_o[j], na_idx)
            na_k_out.append(k_ctx)
            na_v_out.append(v_ctx)
        else:
            mla_w = (mla_w_in[j], mla_q_norm_g[j], mla_kv_norm_g[j], mla_w_uq[j],
                     mla_w_uk[j], mla_w_uv[j], mla_w_o[j])
            yp, ckv_ctx, kr_ctx = mla_context(hp, *mla_w)
            ys = mla_latent(hs, state_mla_ckv[:, j], state_mla_kr[:, j], cos, sin, *mla_w)
            ckv_out.append(ckv_ctx)
            kr_out.append(kr_ctx)
        xp = xp + mod_p[2] * yp
        xs = xs + mod_s[2] * ys
        moe_w = (moe_w_group[i], moe_b_group[i], moe_w_expert[i], moe_b_expert[i],
                 moe_w_gate[i], moe_w_up[i], moe_w_down[i])
        hp = modulate(rms_norm(xp, norm_ffn_g[i]), mod_p[3], mod_p[4])
        hs = modulate(rms_norm(xs, norm_ffn_g[i]), mod_s[3], mod_s[4])
        xp = xp + mod_p[5] * hier_moe(hp, *moe_w)
        xs = xs + mod_s[5] * hier_moe(hs, *moe_w)
    y_prompt = rms_norm(xp, final_norm_g)
    y_sample = rms_norm(xs, final_norm_g)
    new_na_k = jnp.stack(na_k_out, axis=1)
    new_na_v = jnp.stack(na_v_out, axis=1)
    new_mla_ckv = jnp.stack(ckv_out, axis=1)
    new_mla_kr = jnp.stack(kr_out, axis=1)
    return (y_prompt, y_sample, new_na_k, new_na_v, new_mla_ckv, new_mla_kr)
```

```python
import functools

import numpy as np
import jax
import jax.numpy as jnp
from jax import lax
from jax.experimental import pallas as pl
from jax.experimental.pallas import tpu as pltpu

F32 = jnp.float32
BF16 = jnp.bfloat16

D_MODEL = 1024
DEPTH = 2
GRID_W = 64
N_MOD = 6
EPS = 1e-6
NEG_INF = -1e30
NA_HEADS = 16
NA_HEAD_DIM = 64
NA_KH = 8
NA_KW = 16
NA_GROUP_ROWS = 4
NA_WIN_ROWS = 12
MLA_HEADS = 16
MLA_Q_RANK = 384
MLA_KV_RANK = 256
MLA_NOPE_DIM = 64
MLA_ROPE_DIM = 32
MLA_V_DIM = 64
MLA_QK_DIM = MLA_NOPE_DIM + MLA_ROPE_DIM
MLA_IN_PAD = 768
ROPE_THETA = 10000.0
N_GROUPS = 4
EXPERTS_PER_GROUP = 4
N_EXPERTS = 16
D_FF_EXPERT = 256

LANES = 128
MOD_ROWS = 8
VMEM_LIMIT = 56 * 1024 * 1024
TM = 512


def _cparams(sem):
    return pltpu.CompilerParams(dimension_semantics=sem, vmem_limit_bytes=VMEM_LIMIT)


def _const_spec(shape):
    nd = len(shape)
    return pl.BlockSpec(shape, lambda *_: (0,) * nd)


def _norm_mod(x, g, shift, scale):
    ms = jnp.mean(x * x, axis=-1, keepdims=True)
    y = x * lax.rsqrt(ms + EPS) * g
    return y * (1.0 + scale) + shift


def _rms(x, g):
    ms = jnp.mean(x * x, axis=-1, keepdims=True)
    return x * lax.rsqrt(ms + EPS) * g


def _silu(x):
    return x / (1.0 + jnp.exp(-x))


def _ada_kernel(cond_ref, w_ref, b_ref, o_ref):
    s = _silu(cond_ref[...]).astype(BF16)
    o_ref[0] = jnp.dot(s, w_ref[0].astype(BF16), preferred_element_type=F32) + b_ref[0]


def _ada_mod(cond8, ada_w, ada_b):
    tn = 1024
    n = N_MOD * D_MODEL
    out = pl.pallas_call(
        _ada_kernel,
        grid=(DEPTH, n // tn),
        in_specs=[
            _const_spec((MOD_ROWS, D_MODEL)),
            pl.BlockSpec((1, D_MODEL, tn), lambda l, j: (l, 0, j)),
            pl.BlockSpec((1, 1, tn), lambda l, j: (l, 0, j)),
        ],
        out_specs=pl.BlockSpec((1, MOD_ROWS, tn), lambda l, j: (l, 0, j)),
        out_shape=jax.ShapeDtypeStruct((DEPTH, MOD_ROWS, n), F32),
        compiler_params=_cparams(("arbitrary", "arbitrary")),
        name="ada_mod",
    )(cond8, ada_w, ada_b.reshape(DEPTH, 1, n))
    return out.reshape(DEPTH * MOD_ROWS * N_MOD, 1, D_MODEL)


def _mod_spec(layer, k, row0, tokens_per_row):
    def imap(i):
        row = row0 + (i * TM) // tokens_per_row
        return ((layer * MOD_ROWS + row) * N_MOD + k, 0, 0)
    return pl.BlockSpec((1, 1, D_MODEL), imap)


def _qkv_kernel(x_ref, g_ref, sh_ref, sc_ref, w_ref, q_ref, k_ref, v_ref):
    h = _norm_mod(x_ref[...], g_ref[...], sh_ref[0], sc_ref[0]).astype(BF16)
    y = jnp.dot(h, w_ref[...], preferred_element_type=F32)
    q_ref[...] = (y[:, :D_MODEL] * (NA_HEAD_DIM ** -0.5)).astype(q_ref.dtype)
    k_ref[...] = y[:, D_MODEL:2 * D_MODEL].astype(k_ref.dtype)
    v_ref[...] = y[:, 2 * D_MODEL:].astype(v_ref.dtype)


def _na_qkv(x, g, mod, layer, row0, tokens_per_row, w_qkv, kv_dtype):
    t = x.shape[0]
    tok = pl.BlockSpec((TM, D_MODEL), lambda i: (i, 0))
    return pl.pallas_call(
        _qkv_kernel,
        grid=(t // TM,),
        in_specs=[tok, _const_spec((1, D_MODEL)),
                  _mod_spec(layer, 0, row0, tokens_per_row), _mod_spec(layer, 1, row0, tokens_per_row),
                  _const_spec((D_MODEL, 3 * D_MODEL))],
        out_specs=[tok, tok, tok],
        out_shape=[jax.ShapeDtypeStruct((t, D_MODEL), BF16),
                   jax.ShapeDtypeStruct((t, D_MODEL), kv_dtype),
                   jax.ShapeDtypeStruct((t, D_MODEL), kv_dtype)],
        compiler_params=_cparams(("arbitrary",)),
        name="na_qkv",
    )(x, g, mod, mod, w_qkv)


def _softmax_pv(s_list, v_list):
    m = s_list[0].max(axis=-1, keepdims=True)
    for s in s_list[1:]:
        m = jnp.maximum(m, s.max(axis=-1, keepdims=True))
    l = None
    acc = None
    for s, v in zip(s_list, v_list):
        p = jnp.exp(s - m)
        ls = p.sum(axis=-1, keepdims=True)
        a = jnp.dot(p.astype(BF16), v, preferred_element_type=F32)
        l = ls if l is None else l + ls
        acc = a if acc is None else acc + a
    return acc * (1.0 / l)


def _qk(q, k):
    return lax.dot_general(q, k, (((1,), (1,)), ((), ())), preferred_element_type=F32)


def _head_masks():
    lane = lax.broadcasted_iota(jnp.int32, (1, LANES), 1)
    lo = lane < (LANES // 2)
    return lo, jnp.where(lo, 1.0, 0.0).astype(BF16), jnp.where(lo, 0.0, 1.0).astype(BF16)


def _attn_kernel(*refs, n_src, wide_qk):
    q_ref, o_ref = refs[0], refs[-1]
    kv = refs[1:-1]
    lo, m0, m1 = _head_masks()
    q = q_ref[...]
    vs = [kv[2 * t + 1][...].astype(BF16) for t in range(n_src)]
    outs = []
    for j in range(2):
        s_list = []
        for t in range(n_src):
            k = kv[2 * t][...]
            if wide_qk:
                s_list.append(_qk(q[:, j * LANES:(j + 1) * LANES], k[:, j * LANES:(j + 1) * LANES].astype(BF16)))
            else:
                s_list.append(_qk(q * (m0 if j == 0 else m1), k.astype(BF16)))
        outs.append(_softmax_pv(s_list, vs))
    o_ref[...] = jnp.where(lo, outs[0], outs[1]).astype(o_ref.dtype)


def _attention(q, kv_srcs, n_batch, tq, wide_qk):
    wq = 2 * LANES if wide_qk else LANES
    t_q = q.shape[0] // n_batch
    nq = t_q // tq
    n_pairs = D_MODEL // LANES
    in_specs = [pl.BlockSpec((tq, wq), lambda b, p, i: (b * nq + i, p))]
    args = [q]
    for k, v in kv_srcs:
        t_k = k.shape[0] // n_batch
        in_specs.append(pl.BlockSpec((t_k, wq), lambda b, p, i: (b, p)))
        in_specs.append(pl.BlockSpec((t_k, LANES), lambda b, p, i: (b, p)))
        args += [k, v]
    return pl.pallas_call(
        functools.partial(_attn_kernel, n_src=len(kv_srcs), wide_qk=wide_qk),
        grid=(n_batch, n_pairs, nq),
        in_specs=in_specs,
        out_specs=pl.BlockSpec((tq, LANES), lambda b, p, i: (b * nq + i, p)),
        out_shape=jax.ShapeDtypeStruct((q.shape[0], D_MODEL), BF16),
        compiler_params=_cparams(("arbitrary", "arbitrary", "arbitrary")),
        name="attention",
    )(*args)


def _na_group_layout(rows):
    n_groups = rows // NA_GROUP_ROWS
    kh = min(NA_KH, rows)
    bases, types, keys = [], [], {}
    for gi in range(n_groups):
        r0 = gi * NA_GROUP_ROWS
        base = int(np.clip(r0 - kh // 2, 0, rows - NA_WIN_ROWS))
        starts = tuple(int(np.clip(r - kh // 2, 0, rows - kh)) - base for r in range(r0, r0 + NA_GROUP_ROWS))
        key = (starts, r0 - base)
        if key not in keys:
            keys[key] = len(keys)
        bases.append(base)
        types.append(keys[key])
    return bases, types, list(keys.keys())


def _na_bias_tables(rpb, rows):
    kh = min(NA_KH, rows)
    _, _, type_keys = _na_group_layout(rows)
    qc = np.arange(GRID_W)
    kc = np.arange(GRID_W)
    col_start = np.clip(qc - NA_KW // 2, 0, GRID_W - NA_KW)
    col_valid = (kc[None, :] >= col_start[:, None]) & (kc[None, :] < col_start[:, None] + NA_KW)
    dx = np.clip(kc[None, :] - qc[:, None] + (NA_KW - 1), 0, 2 * NA_KW - 2)
    tables = []
    for starts, r_off in type_keys:
        ry = np.arange(NA_WIN_ROWS)
        valid_rows = np.stack([(ry >= s) & (ry < s + kh) for s in starts])
        dy = ry[None, :] - (np.arange(NA_GROUP_ROWS)[:, None] + r_off) + (NA_KH - 1)
        dy = np.clip(dy, 0, 2 * NA_KH - 2)
        valid = valid_rows[:, None, :, None] & col_valid[None, :, None, :]
        dy_b = np.broadcast_to(dy[:, None, :, None], valid.shape)
        dx_b = np.broadcast_to(dx[None, :, None, :], valid.shape)
        bias = rpb[:, dy_b, dx_b].astype(F32)
        bias = jnp.where(valid[None], bias, NEG_INF)
        tables.append(bias.reshape(NA_HEADS, NA_GROUP_ROWS * GRID_W, NA_WIN_ROWS * GRID_W))
    return jnp.stack(tables, axis=1)


def _na_latent_kernel(q_ref, k_ref, v_ref, kc_ref, vc_ref, tab_ref, o_ref, *, bases, types):
    lo, m0, m1 = _head_masks()
    kctx = kc_ref[...].astype(BF16)
    vctx = vc_ref[...].astype(BF16)
    gq = NA_GROUP_ROWS * GRID_W
    gk = NA_WIN_ROWS * GRID_W
    for gi, (base, ty) in enumerate(zip(bases, types)):
        q = q_ref[gi * gq:(gi + 1) * gq, :]
        kwin = k_ref[base * GRID_W:base * GRID_W + gk, :]
        vwin = v_ref[base * GRID_W:base * GRID_W + gk, :]
        outs = []
        for j in range(2):
            qj = q * (m0 if j == 0 else m1)
            s_win = _qk(qj, kwin) + tab_ref[j, ty]
            s_ctx = _qk(qj, kctx)
            outs.append(_softmax_pv([s_win, s_ctx], [vwin, vctx]))
        o_ref[gi * gq:(gi + 1) * gq, :] = jnp.where(lo, outs[0], outs[1]).astype(o_ref.dtype)


def _na_latent(q, k, v, k_ctx, v_ctx, tables, n_batch):
    t = q.shape[0] // n_batch
    rows = t // GRID_W
    t_ctx = k_ctx.shape[0] // n_batch
    bases, types, type_keys = _na_group_layout(rows)
    n_types = len(type_keys)
    gq = NA_GROUP_ROWS * GRID_W
    gk = NA_WIN_ROWS * GRID_W
    tok = pl.BlockSpec((t, LANES), lambda p, b: (b, p))
    ctx = pl.BlockSpec((t_ctx, LANES), lambda p, b: (b, p))
    return pl.pallas_call(
        functools.partial(_na_latent_kernel, bases=bases, types=types),
        grid=(D_MODEL // LANES, n_batch),
        in_specs=[tok, tok, tok, ctx, ctx,
                  pl.BlockSpec((2, n_types, gq, gk), lambda p, b: (p, 0, 0, 0))],
        out_specs=tok,
        out_shape=jax.ShapeDtypeStruct(q.shape, BF16),
        compiler_params=_cparams(("arbitrary", "arbitrary")),
        name="na_latent",
    )(q, k, v, k_ctx, v_ctx, tables)


def _rope_lane_tables(n_tokens):
    n_freq = MLA_ROPE_DIM // 4
    inv = 1.0 / (ROPE_THETA ** (jnp.arange(n_freq, dtype=F32) / n_freq))
    tt = jnp.arange(n_tokens)
    row = (tt // GRID_W).astype(F32)
    col = (tt % GRID_W).astype(F32)
    ang = jnp.concatenate([row[:, None] * inv, col[:, None] * inv], axis=-1)
    cos, sin = jnp.cos(ang), jnp.sin(ang)
    ones = jnp.ones((n_tokens, MLA_NOPE_DIM), F32)
    z16 = jnp.zeros((n_tokens, MLA_ROPE_DIM // 2), F32)
    z32 = jnp.zeros((n_tokens, LANES - MLA_QK_DIM), F32)
    z64 = jnp.zeros((n_tokens, MLA_NOPE_DIM), F32)
    c_tab = jnp.concatenate([ones, cos, cos, z32], axis=-1)
    s_dn = jnp.concatenate([z64, -sin, z16, z32], axis=-1)
    s_up = jnp.concatenate([z64, z16, sin, z32], axis=-1)
    return c_tab, s_dn, s_up


def _rope_slot(x, c_tab, s_dn, s_up):
    half = MLA_ROPE_DIM // 2
    return x * c_tab + pltpu.roll(x, LANES - half, 1) * s_dn + pltpu.roll(x, half, 1) * s_up


def _mla_kv(ckv_n, kr_slot, wuk_ref, wuv_ref, k_ref, v_ref):
    c = ckv_n.astype(BF16)
    k_nope = jnp.dot(c, wuk_ref[...], preferred_element_type=F32)
    for h in range(MLA_HEADS):
        k_ref[:, h * LANES:(h + 1) * LANES] = (k_nope[:, h * LANES:(h + 1) * LANES] + kr_slot).astype(k_ref.dtype)
    v_ref[...] = jnp.dot(c, wuv_ref[...], preferred_element_type=F32).astype(v_ref.dtype)


def _mla_proj_kernel(*refs, rope, emit_state):
    (x_ref, g_ref, sh_ref, sc_ref, win_ref, qg_ref, kvg_ref, wuq_ref, wuk_ref, wuv_ref) = refs[:10]
    pos = 10
    if rope:
        ct_ref, sd_ref, su_ref = refs[pos:pos + 3]
        pos += 3
    q_ref, k_ref, v_ref = refs[pos:pos + 3]
    pos += 3
    h = _norm_mod(x_ref[...], g_ref[...], sh_ref[0], sc_ref[0]).astype(BF16)
    c = jnp.dot(h, win_ref[...], preferred_element_type=F32)
    cq = _rms(c[:, :MLA_Q_RANK], qg_ref[...]).astype(BF16)
    ckv = _rms(c[:, MLA_Q_RANK:MLA_Q_RANK + MLA_KV_RANK], kvg_ref[...])
    kr_chunk = c[:, MLA_Q_RANK + MLA_KV_RANK:]
    if emit_state:
        ckv_out, kr_out = refs[pos:pos + 2]
        ckv_out[...] = ckv
        kr_out[...] = kr_chunk[:, :MLA_ROPE_DIM]
    kr_slot = pltpu.roll(kr_chunk, MLA_NOPE_DIM, 1)
    q = jnp.dot(cq, wuq_ref[...], preferred_element_type=F32) * (MLA_QK_DIM ** -0.5)
    if rope:
        ct, sd, su = ct_ref[...], sd_ref[...], su_ref[...]
        kr_slot = _rope_slot(kr_slot, ct, sd, su)
    for hh in range(MLA_HEADS):
        qh = q[:, hh * LANES:(hh + 1) * LANES]
        if rope:
            qh = _rope_slot(qh, ct, sd, su)
        q_ref[:, hh * LANES:(hh + 1) * LANES] = qh.astype(q_ref.dtype)
    _mla_kv(ckv, kr_slot, wuk_ref, wuv_ref, k_ref, v_ref)


def _mla_proj(x, g, mod, layer, row0, tokens_per_row, w, rope_tabs, emit_state):
    t = x.shape[0]
    win, qg, kvg, wuq, wuk, wuv = w
    tok = lambda n: pl.BlockSpec((TM, n), lambda i: (i, 0))
    wq = MLA_HEADS * LANES
    in_specs = [tok(D_MODEL), _const_spec((1, D_MODEL)),
                _mod_spec(layer, 0, row0, tokens_per_row), _mod_spec(layer, 1, row0, tokens_per_row),
                _const_spec(win.shape), _const_spec(qg.shape), _const_spec(kvg.shape),
                _const_spec(wuq.shape), _const_spec(wuk.shape), _const_spec(wuv.shape)]
    args = [x, g, mod, mod, win, qg, kvg, wuq, wuk, wuv]
    rope = rope_tabs is not None
    if rope:
        n_tab = rope_tabs[0].shape[0] // TM
        in_specs += [pl.BlockSpec((TM, LANES), lambda i: (i % n_tab, 0)) for _ in range(3)]
        args += list(rope_tabs)
    out_specs = [tok(wq), tok(wq), tok(D_MODEL)]
    out_shape = [jax.ShapeDtypeStruct((t, wq), BF16), jax.ShapeDtypeStruct((t, wq), BF16),
                 jax.ShapeDtypeStruct((t, D_MODEL), BF16)]
    if emit_state:
        out_specs += [tok(MLA_KV_RANK), tok(MLA_ROPE_DIM)]
        out_shape += [jax.ShapeDtypeStruct((t, MLA_KV_RANK), F32), jax.ShapeDtypeStruct((t, MLA_ROPE_DIM), F32)]
    return pl.pallas_call(
        functools.partial(_mla_proj_kernel, rope=rope, emit_state=emit_state),
        grid=(t // TM,),
        in_specs=in_specs, out_specs=out_specs, out_shape=out_shape,
        compiler_params=_cparams(("arbitrary",)),
        name="mla_proj",
    )(*args)


def _mla_state_kv_kernel(ckv_ref, kr_ref, wuk_ref, wuv_ref, k_ref, v_ref):
    kr_slot = pltpu.roll(kr_ref[...], MLA_NOPE_DIM, 1)
    _mla_kv(ckv_ref[...], kr_slot, wuk_ref, wuv_ref, k_ref, v_ref)


def _mla_state_kv(ckv, kr_pad, wuk, wuv):
    t = ckv.shape[0]
    wq = MLA_HEADS * LANES
    tok = lambda n: pl.BlockSpec((TM, n), lambda i: (i, 0))
    return pl.pallas_call(
        _mla_state_kv_kernel,
        grid=(t // TM,),
        in_specs=[tok(MLA_KV_RANK), tok(LANES), _const_spec(wuk.shape), _const_spec(wuv.shape)],
        out_specs=[tok(wq), tok(D_MODEL)],
        out_shape=[jax.ShapeDtypeStruct((t, wq), BF16), jax.ShapeDtypeStruct((t, D_MODEL), BF16)],
        compiler_params=_cparams(("arbitrary",)),
        name="mla_state_kv",
    )(ckv, kr_pad, wuk, wuv)


def _proj_res_kernel(o_ref, x_ref, w_ref, gate_ref, g_ref, sh_ref, sc_ref, wr_ref, br_ref, xn_ref, h_ref, lg_ref):
    y = jnp.dot(o_ref[...], w_ref[...], preferred_element_type=F32)
    xn = x_ref[...] + gate_ref[0] * y
    xn_ref[...] = xn
    h = _norm_mod(xn, g_ref[...], sh_ref[0], sc_ref[0])
    h_ref[...] = h.astype(BF16)
    lg_ref[...] = jnp.dot(h, wr_ref[...], preferred_element_type=F32, precision=lax.Precision.HIGHEST) + br_ref[...]


def _proj_res(o, x, w_o, g_ffn, mod, layer, row0, tokens_per_row, w_router, b_router):
    t = x.shape[0]
    tok = lambda n: pl.BlockSpec((TM, n), lambda i: (i, 0))
    ms = lambda k: _mod_spec(layer, k, row0, tokens_per_row)
    return pl.pallas_call(
        _proj_res_kernel,
        grid=(t // TM,),
        in_specs=[tok(D_MODEL), tok(D_MODEL), _const_spec(w_o.shape), ms(2), _const_spec((1, D_MODEL)), ms(3), ms(4),
                  _const_spec(w_router.shape), _const_spec(b_router.shape)],
        out_specs=[tok(D_MODEL), tok(D_MODEL), tok(LANES)],
        out_shape=[jax.ShapeDtypeStruct((t, D_MODEL), F32), jax.ShapeDtypeStruct((t, D_MODEL), BF16),
                   jax.ShapeDtypeStruct((t, LANES), F32)],
        compiler_params=_cparams(("arbitrary",)),
        name="proj_res",
    )(o, x, w_o, mod, g_ffn, mod, mod, w_router, b_router)


def _lane_argmax(v, lane):
    top = v.max(axis=-1, keepdims=True)
    idx = jnp.where(v == top, lane, float(LANES)).min(axis=-1, keepdims=True)
    return top, idx


def _route(logits):
    lane = lax.broadcasted_iota(jnp.int32, logits.shape, 1).astype(F32)
    is_g = lane < N_GROUPS
    gl = jnp.where(is_g, logits, -jnp.inf)
    gmax, gsel = _lane_argmax(gl, lane)
    p_group = 1.0 / jnp.where(is_g, jnp.exp(logits - gmax), 0.0).sum(axis=-1, keepdims=True)
    first = N_GROUPS + gsel * EXPERTS_PER_GROUP
    el = jnp.where((lane >= first) & (lane < first + EXPERTS_PER_GROUP), logits, -jnp.inf)
    t1, i1 = _lane_argmax(el, lane)
    t2, i2 = _lane_argmax(jnp.where(lane == i1, -jnp.inf, el), lane)
    e2 = jnp.exp(t2 - t1)
    w1 = 1.0 / (1.0 + e2)
    return jnp.where(lane == i1, w1 * p_group, jnp.where(lane == i2, e2 * w1 * p_group, 0.0))


def _moe_kernel(*refs, final_norm):
    h_ref, lg_ref, x_ref, gate_ref, wgu_ref, wd_ref = refs[:6]
    o_ref, hid_ref = refs[-2], refs[-1]
    gates = _route(lg_ref[...])
    h = h_ref[...]
    f = D_FF_EXPERT
    for e in range(N_EXPERTS):
        gu = jnp.dot(h, wgu_ref[e], preferred_element_type=F32)
        hid = _silu(gu[:, :f]) * gu[:, f:] * gates[:, N_GROUPS + e:N_GROUPS + e + 1]
        hid_ref[:, e * f:(e + 1) * f] = hid.astype(BF16)
    y = jnp.dot(hid_ref[...], wd_ref[...], preferred_element_type=F32)
    out = x_ref[...] + gate_ref[0] * y
    if final_norm:
        out = _rms(out, refs[6][...])
    o_ref[...] = out


def _moe(h, logits, x, mod, layer, row0, tokens_per_row, wgu, wd, final_g):
    t = x.shape[0]
    tok = lambda n: pl.BlockSpec((TM, n), lambda i: (i, 0))
    in_specs = [tok(D_MODEL), tok(LANES), tok(D_MODEL), _mod_spec(layer, 5, row0, tokens_per_row),
                pl.BlockSpec(wgu.shape, lambda i: (0, 0, 0), pipeline_mode=pl.Buffered(1)),
                pl.BlockSpec(wd.shape, lambda i: (0, 0), pipeline_mode=pl.Buffered(1))]
    args = [h, logits, x, mod, wgu, wd]
    if final_g is not None:
        in_specs.append(_const_spec((1, D_MODEL)))
        args.append(final_g)
    return pl.pallas_call(
        functools.partial(_moe_kernel, final_norm=final_g is not None),
        grid=(t // TM,),
        in_specs=in_specs,
        out_specs=tok(D_MODEL),
        out_shape=jax.ShapeDtypeStruct((t, D_MODEL), F32),
        scratch_shapes=[pltpu.VMEM((TM, N_EXPERTS * D_FF_EXPERT), BF16)],
        compiler_params=_cparams(("arbitrary",)),
        name="moe",
    )(*args)


def _pad_heads(w, used):
    k = w.shape[0]
    w = w.reshape(k, MLA_HEADS, used)
    return jnp.pad(w, ((0, 0), (0, 0), (0, LANES - used))).reshape(k, MLA_HEADS * LANES)


def kernel(x_prompt, x_sample, state_na_k, state_na_v, state_mla_ckv, state_mla_kr, c, c_ctx, ada_w, ada_b, norm_mix_g, norm_ffn_g, final_norm_g, na_w_qkv, na_w_o, na_rpb, mla_w_in, mla_q_norm_g, mla_kv_norm_g, mla_w_uq, mla_w_uk, mla_w_uv, mla_w_o, moe_w_group, moe_b_group, moe_w_expert, moe_b_expert, moe_w_gate, moe_w_up, moe_w_down):
    n_b, seq, d = x_prompt.shape
    dec_b, n_lat, _ = x_sample.shape
    past = state_na_k.shape[2]
    rows = n_lat // GRID_W
    assert d == D_MODEL and dec_b + 1 <= MOD_ROWS and n_lat % TM == 0 and (n_b * seq) % TM == 0

    cond8 = jnp.concatenate([c_ctx[None], c, jnp.zeros((MOD_ROWS - 1 - dec_b, d), F32)], axis=0)
    mod = _ada_mod(cond8, ada_w, ada_b)

    xp = x_prompt.reshape(n_b * seq, d)
    xs = x_sample.reshape(dec_b * n_lat, d)
    p_rows = (0, n_b * seq)
    s_rows = (1, n_lat)
    outs = {}

    for i in range(DEPTH):
        j = i // 2
        g_mix = norm_mix_g[i][None]
        g_ffn = norm_ffn_g[i][None]
        if i % 2 == 0:
            w_qkv = na_w_qkv[j].astype(BF16)
            qp, kp, vp = _na_qkv(xp, g_mix, mod, i, *p_rows, w_qkv, F32)
            qs, ks, vs = _na_qkv(xs, g_mix, mod, i, *s_rows, w_qkv, BF16)
            outs['na_k'] = kp.reshape(n_b, 1, seq, NA_HEADS, NA_HEAD_DIM)
            outs['na_v'] = vp.reshape(n_b, 1, seq, NA_HEADS, NA_HEAD_DIM)
            op = _attention(qp, [(kp, vp)], n_b, seq, wide_qk=False)
            tables = _na_bias_tables(na_rpb[j], rows)
            os_ = _na_latent(qs, ks, vs, state_na_k[:, j].reshape(dec_b * past, d),
                             state_na_v[:, j].reshape(dec_b * past, d), tables, dec_b)
            w_o = na_w_o[j].astype(BF16)
        else:
            pad_in = MLA_IN_PAD - mla_w_in.shape[-1]
            w_mla = (jnp.pad(mla_w_in[j], ((0, 0), (0, pad_in))).astype(BF16),
                     mla_q_norm_g[j][None], mla_kv_norm_g[j][None],
                     _pad_heads(mla_w_uq[j], MLA_QK_DIM).astype(BF16),
                     _pad_heads(mla_w_uk[j], MLA_NOPE_DIM).astype(BF16),
                     mla_w_uv[j].astype(BF16))
            qp, kp, vp, ckv_p, kr_p = _mla_proj(xp, g_mix, mod, i, *p_rows, w_mla, None, True)
            qs, ks, vs = _mla_proj(xs, g_mix, mod, i, *s_rows, w_mla, _rope_lane_tables(n_lat), False)
            outs['ckv'] = ckv_p.reshape(n_b, 1, seq, MLA_KV_RANK)
            outs['kr'] = kr_p.reshape(n_b, 1, seq, MLA_ROPE_DIM)
            kr_state = jnp.pad(state_mla_kr[:, j].reshape(dec_b * past, MLA_ROPE_DIM),
                               ((0, 0), (0, LANES - MLA_ROPE_DIM)))
            kc, vc = _mla_state_kv(state_mla_ckv[:, j].reshape(dec_b * past, MLA_KV_RANK), kr_state,
                                   w_mla[4], w_mla[5])
            op = _attention(qp, [(kp, vp)], n_b, seq, wide_qk=True)
            os_ = _attention(qs, [(ks, vs), (kc, vc)], dec_b, 512, wide_qk=True)
            w_o = mla_w_o[j].astype(BF16)

        w_router = jnp.pad(jnp.concatenate([moe_w_group[i], moe_w_expert[i]], axis=1),
                           ((0, 0), (0, LANES - N_GROUPS - N_EXPERTS)))
        b_router = jnp.pad(jnp.concatenate([moe_b_group[i], moe_b_expert[i]]),
                           (0, LANES - N_GROUPS - N_EXPERTS))[None]
        wgu = jnp.concatenate([moe_w_gate[i], moe_w_up[i]], axis=-1).astype(BF16)
        wd = moe_w_down[i].reshape(N_EXPERTS * D_FF_EXPERT, d).astype(BF16)
        final_g = final_norm_g[None] if i == DEPTH - 1 else None
        xp, hp, lgp = _proj_res(op, xp, w_o, g_ffn, mod, i, *p_rows, w_router, b_router)
        xs, hs, lgs = _proj_res(os_, xs, w_o, g_ffn, mod, i, *s_rows, w_router, b_router)
        xp = _moe(hp, lgp, xp, mod, i, *p_rows, wgu, wd, final_g)
        xs = _moe(hs, lgs, xs, mod, i, *s_rows, wgu, wd, final_g)

    return (xp.reshape(n_b, seq, d), xs.reshape(dec_b, n_lat, d),
            outs['na_k'], outs['na_v'], outs['ckv'], outs['kr'])
```

```python
import functools

import numpy as np
import jax
import jax.numpy as jnp
from jax import lax
from jax.experimental import pallas as pl
from jax.experimental.pallas import tpu as pltpu

F32 = jnp.float32
BF16 = jnp.bfloat16

D_MODEL = 1024
DEPTH = 2
GRID_W = 64
N_MOD = 6
EPS = 1e-6
NEG_INF = -1e30
NA_HEADS = 16
NA_HEAD_DIM = 64
NA_KH = 8
NA_KW = 16
NA_GROUP_ROWS = 4
NA_WIN_ROWS = 12
MLA_HEADS = 16
MLA_Q_RANK = 384
MLA_KV_RANK = 256
MLA_NOPE_DIM = 64
MLA_ROPE_DIM = 32
MLA_V_DIM = 64
MLA_QK_DIM = MLA_NOPE_DIM + MLA_ROPE_DIM
MLA_IN_PAD = 768
ROPE_THETA = 10000.0
N_GROUPS = 4
EXPERTS_PER_GROUP = 4
N_EXPERTS = 16
D_FF_EXPERT = 256

LANES = 128
MOD_ROWS = 8
VMEM_LIMIT = 56 * 1024 * 1024
TM = 512


def _cparams(sem):
    return pltpu.CompilerParams(dimension_semantics=sem, vmem_limit_bytes=VMEM_LIMIT)


def _const_spec(shape):
    nd = len(shape)
    return pl.BlockSpec(shape, lambda *_: (0,) * nd)


def _norm_mod(x, g, shift, scale):
    ms = jnp.mean(x * x, axis=-1, keepdims=True)
    y = x * lax.rsqrt(ms + EPS) * g
    return y * (1.0 + scale) + shift


def _rms(x, g):
    ms = jnp.mean(x * x, axis=-1, keepdims=True)
    return x * lax.rsqrt(ms + EPS) * g


def _silu(x):
    return x / (1.0 + jnp.exp(-x))


def _ada_kernel(cond_ref, w_ref, b_ref, o_ref):
    s = _silu(cond_ref[...]).astype(BF16)
    o_ref[0] = jnp.dot(s, w_ref[0].astype(BF16), preferred_element_type=F32) + b_ref[0]


def _ada_mod(cond8, ada_w, ada_b):
    tn = 1024
    n = N_MOD * D_MODEL
    out = pl.pallas_call(
        _ada_kernel,
        grid=(DEPTH, n // tn),
        in_specs=[
            _const_spec((MOD_ROWS, D_MODEL)),
            pl.BlockSpec((1, D_MODEL, tn), lambda l, j: (l, 0, j)),
            pl.BlockSpec((1, 1, tn), lambda l, j: (l, 0, j)),
        ],
        out_specs=pl.BlockSpec((1, MOD_ROWS, tn), lambda l, j: (l, 0, j)),
        out_shape=jax.ShapeDtypeStruct((DEPTH, MOD_ROWS, n), F32),
        compiler_params=_cparams(("arbitrary", "arbitrary")),
        name="ada_mod",
    )(cond8, ada_w, ada_b.reshape(DEPTH, 1, n))
    return out.reshape(DEPTH * MOD_ROWS * N_MOD, 1, D_MODEL)


def _mod_spec(layer, k, row0, tokens_per_row):
    def imap(i):
        row = row0 + (i * TM) // tokens_per_row
        return ((layer * MOD_ROWS + row) * N_MOD + k, 0, 0)
    return pl.BlockSpec((1, 1, D_MODEL), imap)


def _qkv_kernel(x_ref, g_ref, sh_ref, sc_ref, w_ref, q_ref, k_ref, v_ref):
    h = _norm_mod(x_ref[...], g_ref[...], sh_ref[0], sc_ref[0]).astype(BF16)
    y = jnp.dot(h, w_ref[...], preferred_element_type=F32)
    q_ref[...] = (y[:, :D_MODEL] * (NA_HEAD_DIM ** -0.5)).astype(q_ref.dtype)
    k_ref[...] = y[:, D_MODEL:2 * D_MODEL].astype(k_ref.dtype)
    v_ref[...] = y[:, 2 * D_MODEL:].astype(v_ref.dtype)


def _na_qkv(x, g, mod, layer, row0, tokens_per_row, w_qkv, kv_dtype):
    t = x.shape[0]
    tok = pl.BlockSpec((TM, D_MODEL), lambda i: (i, 0))
    return pl.pallas_call(
        _qkv_kernel,
        grid=(t // TM,),
        in_specs=[tok, _const_spec((1, D_MODEL)),
                  _mod_spec(layer, 0, row0, tokens_per_row), _mod_spec(layer, 1, row0, tokens_per_row),
                  _const_spec((D_MODEL, 3 * D_MODEL))],
        out_specs=[tok, tok, tok],
        out_shape=[jax.ShapeDtypeStruct((t, D_MODEL), BF16),
                   jax.ShapeDtypeStruct((t, D_MODEL), kv_dtype),
                   jax.ShapeDtypeStruct((t, D_MODEL), kv_dtype)],
        compiler_params=_cparams(("arbitrary",)),
        name="na_qkv",
    )(x, g, mod, mod, w_qkv)


def _softmax_pv(s_list, v_list):
    m = s_list[0].max(axis=-1, keepdims=True)
    for s in s_list[1:]:
        m = jnp.maximum(m, s.max(axis=-1, keepdims=True))
    l = None
    acc = None
    for s, v in zip(s_list, v_list):
        p = jnp.exp(s - m)
        ls = p.sum(axis=-1, keepdims=True)
        a = jnp.dot(p.astype(BF16), v, preferred_element_type=F32)
        l = ls if l is None else l + ls
        acc = a if acc is None else acc + a
    return acc * (1.0 / l)


def _qk(q, k):
    return lax.dot_general(q, k, (((1,), (1,)), ((), ())), preferred_element_type=F32)


def _head_masks():
    lane = lax.broadcasted_iota(jnp.int32, (1, LANES), 1)
    lo = lane < (LANES // 2)
    return lo, jnp.where(lo, 1.0, 0.0).astype(BF16), jnp.where(lo, 0.0, 1.0).astype(BF16)


def _attn_kernel(*refs, n_src, wide_qk, pairs):
    q_ref, o_ref = refs[0], refs[-1]
    kv = refs[1:-1]
    lo, m0, m1 = _head_masks()
    wq = 2 * LANES if wide_qk else LANES
    for p in range(pairs):
        q = q_ref[:, p * wq:(p + 1) * wq]
        ks = [kv[2 * t][:, p * wq:(p + 1) * wq].astype(BF16) for t in range(n_src)]
        vs = [kv[2 * t + 1][:, p * LANES:(p + 1) * LANES].astype(BF16) for t in range(n_src)]
        outs = []
        for j in range(2):
            if wide_qk:
                s_list = [_qk(q[:, j * LANES:(j + 1) * LANES], k[:, j * LANES:(j + 1) * LANES]) for k in ks]
            else:
                s_list = [_qk(q * (m0 if j == 0 else m1), k) for k in ks]
            outs.append(_softmax_pv(s_list, vs))
        o_ref[:, p * LANES:(p + 1) * LANES] = jnp.where(lo, outs[0], outs[1]).astype(o_ref.dtype)


def _attention(q, kv_srcs, n_batch, tq, wide_qk, pairs):
    wq = (2 * LANES if wide_qk else LANES) * pairs
    wv = LANES * pairs
    t_q = q.shape[0] // n_batch
    nq = t_q // tq
    n_steps = D_MODEL // wv
    in_specs = [pl.BlockSpec((tq, wq), lambda b, p, i: (b * nq + i, p))]
    args = [q]
    for k, v in kv_srcs:
        t_k = k.shape[0] // n_batch
        in_specs.append(pl.BlockSpec((t_k, wq), lambda b, p, i: (b, p)))
        in_specs.append(pl.BlockSpec((t_k, wv), lambda b, p, i: (b, p)))
        args += [k, v]
    return pl.pallas_call(
        functools.partial(_attn_kernel, n_src=len(kv_srcs), wide_qk=wide_qk, pairs=pairs),
        grid=(n_batch, n_steps, nq),
        in_specs=in_specs,
        out_specs=pl.BlockSpec((tq, wv), lambda b, p, i: (b * nq + i, p)),
        out_shape=jax.ShapeDtypeStruct((q.shape[0], D_MODEL), BF16),
        compiler_params=_cparams(("arbitrary", "arbitrary", "arbitrary")),
        name="attention",
    )(*args)


def _na_group_layout(rows):
    n_groups = rows // NA_GROUP_ROWS
    kh = min(NA_KH, rows)
    bases, types, keys = [], [], {}
    for gi in range(n_groups):
        r0 = gi * NA_GROUP_ROWS
        base = int(np.clip(r0 - kh // 2, 0, rows - NA_WIN_ROWS))
        starts = tuple(int(np.clip(r - kh // 2, 0, rows - kh)) - base for r in range(r0, r0 + NA_GROUP_ROWS))
        key = (starts, r0 - base)
        if key not in keys:
            keys[key] = len(keys)
        bases.append(base)
        types.append(keys[key])
    return bases, types, list(keys.keys())


def _na_bias_tables(rpb, rows):
    kh = min(NA_KH, rows)
    _, _, type_keys = _na_group_layout(rows)
    qc = np.arange(GRID_W)
    kc = np.arange(GRID_W)
    col_start = np.clip(qc - NA_KW // 2, 0, GRID_W - NA_KW)
    col_valid = (kc[None, :] >= col_start[:, None]) & (kc[None, :] < col_start[:, None] + NA_KW)
    dx = kc[None, :] - qc[:, None] + (NA_KW - 1)
    onehot = ((dx[None] == np.arange(2 * NA_KW - 1)[:, None, None]) & col_valid[None]).astype(np.float32)
    band = jnp.einsum('hyd,dqk->hyqk', rpb.astype(F32), onehot, precision=lax.Precision.HIGHEST)
    band = jnp.where(col_valid, band, NEG_INF)
    neg = jnp.full((NA_HEADS, GRID_W, GRID_W), NEG_INF, F32)
    tables = []
    for starts, r_off in type_keys:
        q_rows = []
        for rq in range(NA_GROUP_ROWS):
            blocks = []
            for ry in range(NA_WIN_ROWS):
                dy = ry - (rq + r_off) + (NA_KH - 1)
                inside = starts[rq] <= ry < starts[rq] + kh
                blocks.append(band[:, dy] if inside else neg)
            q_rows.append(jnp.concatenate(blocks, axis=-1))
        tables.append(jnp.concatenate(q_rows, axis=1))
    return jnp.stack(tables, axis=1)


def _na_latent_kernel(q_ref, k_ref, v_ref, kc_ref, vc_ref, tab_ref, o_ref, *, bases, types):
    lo, m0, m1 = _head_masks()
    kctx = kc_ref[...].astype(BF16)
    vctx = vc_ref[...].astype(BF16)
    gq = NA_GROUP_ROWS * GRID_W
    gk = NA_WIN_ROWS * GRID_W
    for gi, (base, ty) in enumerate(zip(bases, types)):
        q = q_ref[gi * gq:(gi + 1) * gq, :]
        kwin = k_ref[base * GRID_W:base * GRID_W + gk, :]
        vwin = v_ref[base * GRID_W:base * GRID_W + gk, :]
        outs = []
        for j in range(2):
            qj = q * (m0 if j == 0 else m1)
            s_win = _qk(qj, kwin) + tab_ref[j, ty]
            s_ctx = _qk(qj, kctx)
            outs.append(_softmax_pv([s_win, s_ctx], [vwin, vctx]))
        o_ref[gi * gq:(gi + 1) * gq, :] = jnp.where(lo, outs[0], outs[1]).astype(o_ref.dtype)


def _na_latent(q, k, v, k_ctx, v_ctx, tables, n_batch):
    t = q.shape[0] // n_batch
    rows = t // GRID_W
    t_ctx = k_ctx.shape[0] // n_batch
    bases, types, type_keys = _na_group_layout(rows)
    n_types = len(type_keys)
    gq = NA_GROUP_ROWS * GRID_W
    gk = NA_WIN_ROWS * GRID_W
    tok = pl.BlockSpec((t, LANES), lambda p, b: (b, p))
    ctx = pl.BlockSpec((t_ctx, LANES), lambda p, b: (b, p))
    return pl.pallas_call(
        functools.partial(_na_latent_kernel, bases=bases, types=types),
        grid=(D_MODEL // LANES, n_batch),
        in_specs=[tok, tok, tok, ctx, ctx,
                  pl.BlockSpec((2, n_types, gq, gk), lambda p, b: (p, 0, 0, 0))],
        out_specs=tok,
        out_shape=jax.ShapeDtypeStruct(q.shape, BF16),
        compiler_params=_cparams(("arbitrary", "arbitrary")),
        name="na_latent",
    )(q, k, v, k_ctx, v_ctx, tables)


def _rope_lane_tables(n_tokens):
    n_freq = MLA_ROPE_DIM // 4
    inv = 1.0 / (ROPE_THETA ** (jnp.arange(n_freq, dtype=F32) / n_freq))
    tt = jnp.arange(n_tokens)
    row = (tt // GRID_W).astype(F32)
    col = (tt % GRID_W).astype(F32)
    ang = jnp.concatenate([row[:, None] * inv, col[:, None] * inv], axis=-1)
    cos, sin = jnp.cos(ang), jnp.sin(ang)
    ones = jnp.ones((n_tokens, MLA_NOPE_DIM), F32)
    z16 = jnp.zeros((n_tokens, MLA_ROPE_DIM // 2), F32)
    z32 = jnp.zeros((n_tokens, LANES - MLA_QK_DIM), F32)
    z64 = jnp.zeros((n_tokens, MLA_NOPE_DIM), F32)
    c_tab = jnp.concatenate([ones, cos, cos, z32], axis=-1)
    s_dn = jnp.concatenate([z64, -sin, z16, z32], axis=-1)
    s_up = jnp.concatenate([z64, z16, sin, z32], axis=-1)
    return c_tab, s_dn, s_up


def _rope_slot(x, c_tab, s_dn, s_up):
    half = MLA_ROPE_DIM // 2
    return x * c_tab + pltpu.roll(x, LANES - half, 1) * s_dn + pltpu.roll(x, half, 1) * s_up


def _mla_kv(ckv_n, kr_slot, wuk_ref, wuv_ref, k_ref, v_ref):
    c = ckv_n.astype(BF16)
    k_nope = jnp.dot(c, wuk_ref[...], preferred_element_type=F32)
    for h in range(MLA_HEADS):
        k_ref[:, h * LANES:(h + 1) * LANES] = (k_nope[:, h * LANES:(h + 1) * LANES] + kr_slot).astype(k_ref.dtype)
    v_ref[...] = jnp.dot(c, wuv_ref[...], preferred_element_type=F32).astype(v_ref.dtype)


def _mla_proj_kernel(*refs, rope, emit_state):
    (x_ref, g_ref, sh_ref, sc_ref, win_ref, qg_ref, kvg_ref, wuq_ref, wuk_ref, wuv_ref) = refs[:10]
    pos = 10
    if rope:
        ct_ref, sd_ref, su_ref = refs[pos:pos + 3]
        pos += 3
    q_ref, k_ref, v_ref = refs[pos:pos + 3]
    pos += 3
    h = _norm_mod(x_ref[...], g_ref[...], sh_ref[0], sc_ref[0]).astype(BF16)
    c = jnp.dot(h, win_ref[...], preferred_element_type=F32)
    cq = _rms(c[:, :MLA_Q_RANK], qg_ref[...]).astype(BF16)
    ckv = _rms(c[:, MLA_Q_RANK:MLA_Q_RANK + MLA_KV_RANK], kvg_ref[...])
    kr_chunk = c[:, MLA_Q_RANK + MLA_KV_RANK:]
    if emit_state:
        ckv_out, kr_out = refs[pos:pos + 2]
        ckv_out[...] = ckv
        kr_out[...] = kr_chunk[:, :MLA_ROPE_DIM]
    kr_slot = pltpu.roll(kr_chunk, MLA_NOPE_DIM, 1)
    q = jnp.dot(cq, wuq_ref[...], preferred_element_type=F32) * (MLA_QK_DIM ** -0.5)
    if rope:
        ct, sd, su = ct_ref[...], sd_ref[...], su_ref[...]
        kr_slot = _rope_slot(kr_slot, ct, sd, su)
    for hh in range(MLA_HEADS):
        qh = q[:, hh * LANES:(hh + 1) * LANES]
        if rope:
            qh = _rope_slot(qh, ct, sd, su)
        q_ref[:, hh * LANES:(hh + 1) * LANES] = qh.astype(q_ref.dtype)
    _mla_kv(ckv, kr_slot, wuk_ref, wuv_ref, k_ref, v_ref)


def _mla_proj(x, g, mod, layer, row0, tokens_per_row, w, rope_tabs, emit_state):
    t = x.shape[0]
    win, qg, kvg, wuq, wuk, wuv = w
    tok = lambda n: pl.BlockSpec((TM, n), lambda i: (i, 0))
    wq = MLA_HEADS * LANES
    in_specs = [tok(D_MODEL), _const_spec((1, D_MODEL)),
                _mod_spec(layer, 0, row0, tokens_per_row), _mod_spec(layer, 1, row0, tokens_per_row),
                _const_spec(win.shape), _const_spec(qg.shape), _const_spec(kvg.shape),
                _const_spec(wuq.shape), _const_spec(wuk.shape), _const_spec(wuv.shape)]
    args = [x, g, mod, mod, win, qg, kvg, wuq, wuk, wuv]
    rope = rope_tabs is not None
    if rope:
        n_tab = rope_tabs[0].shape[0] // TM
        in_specs += [pl.BlockSpec((TM, LANES), lambda i: (i % n_tab, 0)) for _ in range(3)]
        args += list(rope_tabs)
    out_specs = [tok(wq), tok(wq), tok(D_MODEL)]
    out_shape = [jax.ShapeDtypeStruct((t, wq), BF16), jax.ShapeDtypeStruct((t, wq), BF16),
                 jax.ShapeDtypeStruct((t, D_MODEL), BF16)]
    if emit_state:
        out_specs += [tok(MLA_KV_RANK), tok(MLA_ROPE_DIM)]
        out_shape += [jax.ShapeDtypeStruct((t, MLA_KV_RANK), F32), jax.ShapeDtypeStruct((t, MLA_ROPE_DIM), F32)]
    return pl.pallas_call(
        functools.partial(_mla_proj_kernel, rope=rope, emit_state=emit_state),
        grid=(t // TM,),
        in_specs=in_specs, out_specs=out_specs, out_shape=out_shape,
        compiler_params=_cparams(("arbitrary",)),
        name="mla_proj",
    )(*args)


def _mla_state_kv_kernel(ckv_ref, kr_ref, wuk_ref, wuv_ref, k_ref, v_ref):
    kr_slot = pltpu.roll(kr_ref[...], MLA_NOPE_DIM, 1)
    _mla_kv(ckv_ref[...], kr_slot, wuk_ref, wuv_ref, k_ref, v_ref)


def _mla_state_kv(ckv, kr_pad, wuk, wuv):
    t = ckv.shape[0]
    wq = MLA_HEADS * LANES
    tok = lambda n: pl.BlockSpec((TM, n), lambda i: (i, 0))
    return pl.pallas_call(
        _mla_state_kv_kernel,
        grid=(t // TM,),
        in_specs=[tok(MLA_KV_RANK), tok(LANES), _const_spec(wuk.shape), _const_spec(wuv.shape)],
        out_specs=[tok(wq), tok(D_MODEL)],
        out_shape=[jax.ShapeDtypeStruct((t, wq), BF16), jax.ShapeDtypeStruct((t, D_MODEL), BF16)],
        compiler_params=_cparams(("arbitrary",)),
        name="mla_state_kv",
    )(ckv, kr_pad, wuk, wuv)


def _proj_res_kernel(o_ref, x_ref, w_ref, gate_ref, g_ref, sh_ref, sc_ref, wr_ref, br_ref, xn_ref, h_ref, lg_ref):
    y = jnp.dot(o_ref[...], w_ref[...], preferred_element_type=F32)
    xn = x_ref[...] + gate_ref[0] * y
    xn_ref[...] = xn
    h = _norm_mod(xn, g_ref[...], sh_ref[0], sc_ref[0])
    h_ref[...] = h.astype(BF16)
    lg_ref[...] = jnp.dot(h, wr_ref[...], preferred_element_type=F32, precision=lax.Precision.HIGHEST) + br_ref[...]


def _proj_res(o, x, w_o, g_ffn, mod, layer, row0, tokens_per_row, w_router, b_router):
    t = x.shape[0]
    tok = lambda n: pl.BlockSpec((TM, n), lambda i: (i, 0))
    ms = lambda k: _mod_spec(layer, k, row0, tokens_per_row)
    return pl.pallas_call(
        _proj_res_kernel,
        grid=(t // TM,),
        in_specs=[tok(D_MODEL), tok(D_MODEL), _const_spec(w_o.shape), ms(2), _const_spec((1, D_MODEL)), ms(3), ms(4),
                  _const_spec(w_router.shape), _const_spec(b_router.shape)],
        out_specs=[tok(D_MODEL), tok(D_MODEL), tok(LANES)],
        out_shape=[jax.ShapeDtypeStruct((t, D_MODEL), F32), jax.ShapeDtypeStruct((t, D_MODEL), BF16),
                   jax.ShapeDtypeStruct((t, LANES), F32)],
        compiler_params=_cparams(("arbitrary",)),
        name="proj_res",
    )(o, x, w_o, mod, g_ffn, mod, mod, w_router, b_router)


def _lane_argmax(v, lane):
    top = v.max(axis=-1, keepdims=True)
    idx = jnp.where(v == top, lane, float(LANES)).min(axis=-1, keepdims=True)
    return top, idx


def _route(logits):
    lane = lax.broadcasted_iota(jnp.int32, logits.shape, 1).astype(F32)
    is_g = lane < N_GROUPS
    gl = jnp.where(is_g, logits, -jnp.inf)
    gmax, gsel = _lane_argmax(gl, lane)
    p_group = 1.0 / jnp.where(is_g, jnp.exp(logits - gmax), 0.0).sum(axis=-1, keepdims=True)
    first = N_GROUPS + gsel * EXPERTS_PER_GROUP
    el = jnp.where((lane >= first) & (lane < first + EXPERTS_PER_GROUP), logits, -jnp.inf)
    t1, i1 = _lane_argmax(el, lane)
    t2, i2 = _lane_argmax(jnp.where(lane == i1, -jnp.inf, el), lane)
    e2 = jnp.exp(t2 - t1)
    w1 = 1.0 / (1.0 + e2)
    return jnp.where(lane == i1, w1 * p_group, jnp.where(lane == i2, e2 * w1 * p_group, 0.0))


def _moe_kernel(*refs, final_norm):
    h_ref, lg_ref, x_ref, gate_ref, wgu_ref, wd_ref = refs[:6]
    o_ref, hid_ref = refs[-2], refs[-1]
    gates = _route(lg_ref[...])
    h = h_ref[...]
    f = D_FF_EXPERT
    for e in range(N_EXPERTS):
        gu = jnp.dot(h, wgu_ref[e], preferred_element_type=F32)
        hid = _silu(gu[:, :f]) * gu[:, f:] * gates[:, N_GROUPS + e:N_GROUPS + e + 1]
        hid_ref[:, e * f:(e + 1) * f] = hid.astype(BF16)
    y = jnp.dot(hid_ref[...], wd_ref[...], preferred_element_type=F32)
    out = x_ref[...] + gate_ref[0] * y
    if final_norm:
        out = _rms(out, refs[6][...])
    o_ref[...] = out


def _moe(h, logits, x, mod, layer, row0, tokens_per_row, wgu, wd, final_g):
    t = x.shape[0]
    tok = lambda n: pl.BlockSpec((TM, n), lambda i: (i, 0))
    in_specs = [tok(D_MODEL), tok(LANES), tok(D_MODEL), _mod_spec(layer, 5, row0, tokens_per_row),
                pl.BlockSpec(wgu.shape, lambda i: (0, 0, 0), pipeline_mode=pl.Buffered(1)),
                pl.BlockSpec(wd.shape, lambda i: (0, 0), pipeline_mode=pl.Buffered(1))]
    args = [h, logits, x, mod, wgu, wd]
    if final_g is not None:
        in_specs.append(_const_spec((1, D_MODEL)))
        args.append(final_g)
    return pl.pallas_call(
        functools.partial(_moe_kernel, final_norm=final_g is not None),
        grid=(t // TM,),
        in_specs=in_specs,
        out_specs=tok(D_MODEL),
        out_shape=jax.ShapeDtypeStruct((t, D_MODEL), F32),
        scratch_shapes=[pltpu.VMEM((TM, N_EXPERTS * D_FF_EXPERT), BF16)],
        compiler_params=_cparams(("arbitrary",)),
        name="moe",
    )(*args)


def _pad_heads(w, used):
    k = w.shape[0]
    w = w.reshape(k, MLA_HEADS, used)
    return jnp.pad(w, ((0, 0), (0, 0), (0, LANES - used))).reshape(k, MLA_HEADS * LANES)


def kernel(x_prompt, x_sample, state_na_k, state_na_v, state_mla_ckv, state_mla_kr, c, c_ctx, ada_w, ada_b, norm_mix_g, norm_ffn_g, final_norm_g, na_w_qkv, na_w_o, na_rpb, mla_w_in, mla_q_norm_g, mla_kv_norm_g, mla_w_uq, mla_w_uk, mla_w_uv, mla_w_o, moe_w_group, moe_b_group, moe_w_expert, moe_b_expert, moe_w_gate, moe_w_up, moe_w_down):
    n_b, seq, d = x_prompt.shape
    dec_b, n_lat, _ = x_sample.shape
    past = state_na_k.shape[2]
    rows = n_lat // GRID_W
    assert d == D_MODEL and dec_b + 1 <= MOD_ROWS and n_lat % TM == 0 and (n_b * seq) % TM == 0

    cond8 = jnp.concatenate([c_ctx[None], c, jnp.zeros((MOD_ROWS - 1 - dec_b, d), F32)], axis=0)
    mod = _ada_mod(cond8, ada_w, ada_b)

    xp = x_prompt.reshape(n_b * seq, d)
    xs = x_sample.reshape(dec_b * n_lat, d)
    p_rows = (0, n_b * seq)
    s_rows = (1, n_lat)
    outs = {}

    for i in range(DEPTH):
        j = i // 2
        g_mix = norm_mix_g[i][None]
        g_ffn = norm_ffn_g[i][None]
        if i % 2 == 0:
            w_qkv = na_w_qkv[j].astype(BF16)
            qp, kp, vp = _na_qkv(xp, g_mix, mod, i, *p_rows, w_qkv, F32)
            qs, ks, vs = _na_qkv(xs, g_mix, mod, i, *s_rows, w_qkv, BF16)
            outs['na_k'] = kp.reshape(n_b, 1, seq, NA_HEADS, NA_HEAD_DIM)
            outs['na_v'] = vp.reshape(n_b, 1, seq, NA_HEADS, NA_HEAD_DIM)
            op = _attention(qp, [(kp, vp)], n_b, seq, wide_qk=False, pairs=8)
            tables = _na_bias_tables(na_rpb[j], rows)
            os_ = _na_latent(qs, ks, vs, state_na_k[:, j].reshape(dec_b * past, d),
                             state_na_v[:, j].reshape(dec_b * past, d), tables, dec_b)
            w_o = na_w_o[j].astype(BF16)
        else:
            pad_in = MLA_IN_PAD - mla_w_in.shape[-1]
            w_mla = (jnp.pad(mla_w_in[j], ((0, 0), (0, pad_in))).astype(BF16),
                     mla_q_norm_g[j][None], mla_kv_norm_g[j][None],
                     _pad_heads(mla_w_uq[j], MLA_QK_DIM).astype(BF16),
                     _pad_heads(mla_w_uk[j], MLA_NOPE_DIM).astype(BF16),
                     mla_w_uv[j].astype(BF16))
            qp, kp, vp, ckv_p, kr_p = _mla_proj(xp, g_mix, mod, i, *p_rows, w_mla, None, True)
            qs, ks, vs = _mla_proj(xs, g_mix, mod, i, *s_rows, w_mla, _rope_lane_tables(n_lat), False)
            outs['ckv'] = ckv_p.reshape(n_b, 1, seq, MLA_KV_RANK)
            outs['kr'] = kr_p.reshape(n_b, 1, seq, MLA_ROPE_DIM)
            kr_state = jnp.pad(state_mla_kr[:, j].reshape(dec_b * past, MLA_ROPE_DIM),
                               ((0, 0), (0, LANES - MLA_ROPE_DIM)))
            kc, vc = _mla_state_kv(state_mla_ckv[:, j].reshape(dec_b * past, MLA_KV_RANK), kr_state,
                                   w_mla[4], w_mla[5])
            op = _attention(qp, [(kp, vp)], n_b, seq, wide_qk=True, pairs=8)
            os_ = _attention(qs, [(ks, vs), (kc, vc)], dec_b, 512, wide_qk=True, pairs=1)
            w_o = mla_w_o[j].astype(BF16)

        w_router = jnp.pad(jnp.concatenate([moe_w_group[i], moe_w_expert[i]], axis=1),
                           ((0, 0), (0, LANES - N_GROUPS - N_EXPERTS)))
        b_router = jnp.pad(jnp.concatenate([moe_b_group[i], moe_b_expert[i]]),
                           (0, LANES - N_GROUPS - N_EXPERTS))[None]
        wgu = jnp.concatenate([moe_w_gate[i], moe_w_up[i]], axis=-1).astype(BF16)
        wd = moe_w_down[i].reshape(N_EXPERTS * D_FF_EXPERT, d).astype(BF16)
        final_g = final_norm_g[None] if i == DEPTH - 1 else None
        xp, hp, lgp = _proj_res(op, xp, w_o, g_ffn, mod, i, *p_rows, w_router, b_router)
        xs, hs, lgs = _proj_res(os_, xs, w_o, g_ffn, mod, i, *s_rows, w_router, b_router)
        xp = _moe(hp, lgp, xp, mod, i, *p_rows, wgu, wd, final_g)
        xs = _moe(hs, lgs, xs, mod, i, *s_rows, wgu, wd, final_g)

    return (xp.reshape(n_b, seq, d), xs.reshape(dec_b, n_lat, d),
            outs['na_k'], outs['na_v'], outs['ckv'], outs['kr'])
```

```python
import functools

import numpy as np
import jax
import jax.numpy as jnp
from jax import lax
from jax.experimental import pallas as pl
from jax.experimental.pallas import tpu as pltpu

F32 = jnp.float32
BF16 = jnp.bfloat16

D_MODEL = 1024
DEPTH = 2
GRID_W = 64
N_MOD = 6
EPS = 1e-6
NEG_INF = -1e30
NA_HEADS = 16
NA_HEAD_DIM = 64
NA_KH = 8
NA_KW = 16
NA_GROUP_ROWS = 4
NA_WIN_ROWS = 12
MLA_HEADS = 16
MLA_Q_RANK = 384
MLA_KV_RANK = 256
MLA_NOPE_DIM = 64
MLA_ROPE_DIM = 32
MLA_V_DIM = 64
MLA_QK_DIM = MLA_NOPE_DIM + MLA_ROPE_DIM
MLA_IN_PAD = 768
ROPE_THETA = 10000.0
N_GROUPS = 4
EXPERTS_PER_GROUP = 4
N_EXPERTS = 16
D_FF_EXPERT = 256

LANES = 128
MOD_ROWS = 8
VMEM_LIMIT = 56 * 1024 * 1024
TM = 512


def _cparams(sem):
    return pltpu.CompilerParams(dimension_semantics=sem, vmem_limit_bytes=VMEM_LIMIT)


def _const_spec(shape):
    nd = len(shape)
    return pl.BlockSpec(shape, lambda *_: (0,) * nd)


def _norm_mod(x, g, shift, scale):
    ms = jnp.mean(x * x, axis=-1, keepdims=True)
    y = x * lax.rsqrt(ms + EPS) * g
    return y * (1.0 + scale) + shift


def _rms(x, g):
    ms = jnp.mean(x * x, axis=-1, keepdims=True)
    return x * lax.rsqrt(ms + EPS) * g


def _silu(x):
    return x / (1.0 + jnp.exp(-x))


def _ada_kernel(cond_ref, w_ref, b_ref, o_ref):
    s = _silu(cond_ref[...]).astype(BF16)
    o_ref[0] = jnp.dot(s, w_ref[0].astype(BF16), preferred_element_type=F32) + b_ref[0]


def _ada_mod(cond8, ada_w, ada_b):
    tn = 1024
    n = N_MOD * D_MODEL
    out = pl.pallas_call(
        _ada_kernel,
        grid=(DEPTH, n // tn),
        in_specs=[
            _const_spec((MOD_ROWS, D_MODEL)),
            pl.BlockSpec((1, D_MODEL, tn), lambda l, j: (l, 0, j)),
            pl.BlockSpec((1, 1, tn), lambda l, j: (l, 0, j)),
        ],
        out_specs=pl.BlockSpec((1, MOD_ROWS, tn), lambda l, j: (l, 0, j)),
        out_shape=jax.ShapeDtypeStruct((DEPTH, MOD_ROWS, n), F32),
        compiler_params=_cparams(("arbitrary", "arbitrary")),
        name="ada_mod",
    )(cond8, ada_w, ada_b.reshape(DEPTH, 1, n))
    return out.reshape(DEPTH * MOD_ROWS * N_MOD, 1, D_MODEL)


def _mod_spec(layer, k, row0, tokens_per_row):
    def imap(i):
        row = row0 + (i * TM) // tokens_per_row
        return ((layer * MOD_ROWS + row) * N_MOD + k, 0, 0)
    return pl.BlockSpec((1, 1, D_MODEL), imap)


def _qkv_kernel(x_ref, g_ref, sh_ref, sc_ref, w_ref, q_ref, k_ref, v_ref):
    h = _norm_mod(x_ref[...], g_ref[...], sh_ref[0], sc_ref[0]).astype(BF16)
    y = jnp.dot(h, w_ref[...], preferred_element_type=F32)
    q_ref[...] = (y[:, :D_MODEL] * (NA_HEAD_DIM ** -0.5)).astype(q_ref.dtype)
    k_ref[...] = y[:, D_MODEL:2 * D_MODEL].astype(k_ref.dtype)
    v_ref[...] = y[:, 2 * D_MODEL:].astype(v_ref.dtype)


def _na_qkv(x, g, mod, layer, row0, tokens_per_row, w_qkv, kv_dtype):
    t = x.shape[0]
    tok = pl.BlockSpec((TM, D_MODEL), lambda i: (i, 0))
    return pl.pallas_call(
        _qkv_kernel,
        grid=(t // TM,),
        in_specs=[tok, _const_spec((1, D_MODEL)),
                  _mod_spec(layer, 0, row0, tokens_per_row), _mod_spec(layer, 1, row0, tokens_per_row),
                  _const_spec((D_MODEL, 3 * D_MODEL))],
        out_specs=[tok, tok, tok],
        out_shape=[jax.ShapeDtypeStruct((t, D_MODEL), BF16),
                   jax.ShapeDtypeStruct((t, D_MODEL), kv_dtype),
                   jax.ShapeDtypeStruct((t, D_MODEL), kv_dtype)],
        compiler_params=_cparams(("arbitrary",)),
        name="na_qkv",
    )(x, g, mod, mod, w_qkv)


def _softmax_pv(s_list, v_list):
    m = s_list[0].max(axis=-1, keepdims=True)
    for s in s_list[1:]:
        m = jnp.maximum(m, s.max(axis=-1, keepdims=True))
    l = None
    acc = None
    for s, v in zip(s_list, v_list):
        p = jnp.exp(s - m)
        ls = p.sum(axis=-1, keepdims=True)
        a = jnp.dot(p.astype(BF16), v, preferred_element_type=F32)
        l = ls if l is None else l + ls
        acc = a if acc is None else acc + a
    return acc * (1.0 / l)


def _qk(q, k):
    return lax.dot_general(q, k, (((1,), (1,)), ((), ())), preferred_element_type=F32)


def _head_masks():
    lane = lax.broadcasted_iota(jnp.int32, (1, LANES), 1)
    lo = lane < (LANES // 2)
    return lo, jnp.where(lo, 1.0, 0.0).astype(BF16), jnp.where(lo, 0.0, 1.0).astype(BF16)


def _attn_kernel(*refs, n_src, wide_qk, pairs):
    q_ref, o_ref = refs[0], refs[-1]
    kv = refs[1:-1]
    lo, m0, m1 = _head_masks()
    wq = 2 * LANES if wide_qk else LANES
    for p in range(pairs):
        q = q_ref[:, p * wq:(p + 1) * wq]
        ks = [kv[2 * t][:, p * wq:(p + 1) * wq].astype(BF16) for t in range(n_src)]
        vs = [kv[2 * t + 1][:, p * LANES:(p + 1) * LANES].astype(BF16) for t in range(n_src)]
        outs = []
        for j in range(2):
            if wide_qk:
                s_list = [_qk(q[:, j * LANES:(j + 1) * LANES], k[:, j * LANES:(j + 1) * LANES]) for k in ks]
            else:
                s_list = [_qk(q * (m0 if j == 0 else m1), k) for k in ks]
            outs.append(_softmax_pv(s_list, vs))
        o_ref[:, p * LANES:(p + 1) * LANES] = jnp.where(lo, outs[0], outs[1]).astype(o_ref.dtype)


def _attention(q, kv_srcs, n_batch, tq, wide_qk, pairs):
    wq = (2 * LANES if wide_qk else LANES) * pairs
    wv = LANES * pairs
    t_q = q.shape[0] // n_batch
    nq = t_q // tq
    n_steps = D_MODEL // wv
    in_specs = [pl.BlockSpec((tq, wq), lambda b, p, i: (b * nq + i, p))]
    args = [q]
    for k, v in kv_srcs:
        t_k = k.shape[0] // n_batch
        in_specs.append(pl.BlockSpec((t_k, wq), lambda b, p, i: (b, p)))
        in_specs.append(pl.BlockSpec((t_k, wv), lambda b, p, i: (b, p)))
        args += [k, v]
    return pl.pallas_call(
        functools.partial(_attn_kernel, n_src=len(kv_srcs), wide_qk=wide_qk, pairs=pairs),
        grid=(n_batch, n_steps, nq),
        in_specs=in_specs,
        out_specs=pl.BlockSpec((tq, wv), lambda b, p, i: (b * nq + i, p)),
        out_shape=jax.ShapeDtypeStruct((q.shape[0], D_MODEL), BF16),
        compiler_params=_cparams(("arbitrary", "arbitrary", "arbitrary")),
        name="attention",
    )(*args)


def _na_group_layout(rows):
    n_groups = rows // NA_GROUP_ROWS
    kh = min(NA_KH, rows)
    bases, types, keys = [], [], {}
    for gi in range(n_groups):
        r0 = gi * NA_GROUP_ROWS
        base = int(np.clip(r0 - kh // 2, 0, rows - NA_WIN_ROWS))
        starts = tuple(int(np.clip(r - kh // 2, 0, rows - kh)) - base for r in range(r0, r0 + NA_GROUP_ROWS))
        key = (starts, r0 - base)
        if key not in keys:
            keys[key] = len(keys)
        bases.append(base)
        types.append(keys[key])
    return bases, types, list(keys.keys())


def _na_bias_tables(rpb, rows):
    kh = min(NA_KH, rows)
    _, _, type_keys = _na_group_layout(rows)
    qc = np.arange(GRID_W)
    kc = np.arange(GRID_W)
    col_start = np.clip(qc - NA_KW // 2, 0, GRID_W - NA_KW)
    col_valid = (kc[None, :] >= col_start[:, None]) & (kc[None, :] < col_start[:, None] + NA_KW)
    dx = kc[None, :] - qc[:, None] + (NA_KW - 1)
    onehot = ((dx[None] == np.arange(2 * NA_KW - 1)[:, None, None]) & col_valid[None]).astype(np.float32)
    band = jnp.einsum('hyd,dqk->hyqk', rpb.astype(F32), onehot, precision=lax.Precision.HIGHEST)
    band = jnp.where(col_valid, band, NEG_INF)
    neg = jnp.full((NA_HEADS, GRID_W, GRID_W), NEG_INF, F32)
    tables = []
    for starts, r_off in type_keys:
        q_rows = []
        for rq in range(NA_GROUP_ROWS):
            blocks = []
            for ry in range(NA_WIN_ROWS):
                dy = ry - (rq + r_off) + (NA_KH - 1)
                inside = starts[rq] <= ry < starts[rq] + kh
                blocks.append(band[:, dy] if inside else neg)
            q_rows.append(jnp.concatenate(blocks, axis=-1))
        tables.append(jnp.concatenate(q_rows, axis=1))
    return jnp.stack(tables, axis=1)


def _na_latent_kernel(q_ref, k_ref, v_ref, kc_ref, vc_ref, tab_ref, o_ref, *, bases, types):
    lo, m0, m1 = _head_masks()
    kctx = kc_ref[...].astype(BF16)
    vctx = vc_ref[...].astype(BF16)
    gq = NA_GROUP_ROWS * GRID_W
    gk = NA_WIN_ROWS * GRID_W
    for gi, (base, ty) in enumerate(zip(bases, types)):
        q = q_ref[gi * gq:(gi + 1) * gq, :]
        kwin = k_ref[base * GRID_W:base * GRID_W + gk, :]
        vwin = v_ref[base * GRID_W:base * GRID_W + gk, :]
        outs = []
        for j in range(2):
            qj = q * (m0 if j == 0 else m1)
            s_win = _qk(qj, kwin) + tab_ref[j, ty]
            s_ctx = _qk(qj, kctx)
            outs.append(_softmax_pv([s_win, s_ctx], [vwin, vctx]))
        o_ref[gi * gq:(gi + 1) * gq, :] = jnp.where(lo, outs[0], outs[1]).astype(o_ref.dtype)


def _na_latent(q, k, v, k_ctx, v_ctx, tables, n_batch):
    t = q.shape[0] // n_batch
    rows = t // GRID_W
    t_ctx = k_ctx.shape[0] // n_batch
    bases, types, type_keys = _na_group_layout(rows)
    n_types = len(type_keys)
    gq = NA_GROUP_ROWS * GRID_W
    gk = NA_WIN_ROWS * GRID_W
    tok = pl.BlockSpec((t, LANES), lambda p, b: (b, p))
    ctx = pl.BlockSpec((t_ctx, LANES), lambda p, b: (b, p))
    return pl.pallas_call(
        functools.partial(_na_latent_kernel, bases=bases, types=types),
        grid=(D_MODEL // LANES, n_batch),
        in_specs=[tok, tok, tok, ctx, ctx,
                  pl.BlockSpec((2, n_types, gq, gk), lambda p, b: (p, 0, 0, 0))],
        out_specs=tok,
        out_shape=jax.ShapeDtypeStruct(q.shape, BF16),
        compiler_params=_cparams(("arbitrary", "arbitrary")),
        name="na_latent",
    )(q, k, v, k_ctx, v_ctx, tables)


def _rope_lane_tables(n_tokens):
    n_freq = MLA_ROPE_DIM // 4
    inv = 1.0 / (ROPE_THETA ** (jnp.arange(n_freq, dtype=F32) / n_freq))
    tt = jnp.arange(n_tokens)
    row = (tt // GRID_W).astype(F32)
    col = (tt % GRID_W).astype(F32)
    ang = jnp.concatenate([row[:, None] * inv, col[:, None] * inv], axis=-1)
    cos, sin = jnp.cos(ang), jnp.sin(ang)
    ones = jnp.ones((n_tokens, MLA_NOPE_DIM), F32)
    z16 = jnp.zeros((n_tokens, MLA_ROPE_DIM // 2), F32)
    z32 = jnp.zeros((n_tokens, LANES - MLA_QK_DIM), F32)
    z64 = jnp.zeros((n_tokens, MLA_NOPE_DIM), F32)
    c_tab = jnp.concatenate([ones, cos, cos, z32], axis=-1)
    s_dn = jnp.concatenate([z64, -sin, z16, z32], axis=-1)
    s_up = jnp.concatenate([z64, z16, sin, z32], axis=-1)
    return c_tab, s_dn, s_up


def _rope_slot(x, c_tab, s_dn, s_up):
    half = MLA_ROPE_DIM // 2
    return x * c_tab + pltpu.roll(x, LANES - half, 1) * s_dn + pltpu.roll(x, half, 1) * s_up


def _mla_kv(ckv_n, kr_slot, wuk_ref, wuv_ref, k_ref, v_ref):
    c = ckv_n.astype(BF16)
    k_nope = jnp.dot(c, wuk_ref[...], preferred_element_type=F32)
    for h in range(MLA_HEADS):
        k_ref[:, h * LANES:(h + 1) * LANES] = (k_nope[:, h * LANES:(h + 1) * LANES] + kr_slot).astype(k_ref.dtype)
    v_ref[...] = jnp.dot(c, wuv_ref[...], preferred_element_type=F32).astype(v_ref.dtype)


def _mla_proj_kernel(*refs, rope, emit_state):
    (x_ref, g_ref, sh_ref, sc_ref, win_ref, qg_ref, kvg_ref, wuq_ref, wuk_ref, wuv_ref) = refs[:10]
    pos = 10
    if rope:
        ct_ref, sd_ref, su_ref = refs[pos:pos + 3]
        pos += 3
    q_ref, k_ref, v_ref = refs[pos:pos + 3]
    pos += 3
    h = _norm_mod(x_ref[...], g_ref[...], sh_ref[0], sc_ref[0]).astype(BF16)
    c = jnp.dot(h, win_ref[...], preferred_element_type=F32)
    cq = _rms(c[:, :MLA_Q_RANK], qg_ref[...]).astype(BF16)
    ckv = _rms(c[:, MLA_Q_RANK:MLA_Q_RANK + MLA_KV_RANK], kvg_ref[...])
    kr_chunk = c[:, MLA_Q_RANK + MLA_KV_RANK:]
    if emit_state:
        ckv_out, kr_out = refs[pos:pos + 2]
        ckv_out[...] = ckv
        kr_out[...] = kr_chunk[:, :MLA_ROPE_DIM]
    kr_slot = pltpu.roll(kr_chunk, MLA_NOPE_DIM, 1)
    q = jnp.dot(cq, wuq_ref[...], preferred_element_type=F32) * (MLA_QK_DIM ** -0.5)
    if rope:
        ct, sd, su = ct_ref[...], sd_ref[...], su_ref[...]
        kr_slot = _rope_slot(kr_slot, ct, sd, su)
    for hh in range(MLA_HEADS):
        qh = q[:, hh * LANES:(hh + 1) * LANES]
        if rope:
            qh = _rope_slot(qh, ct, sd, su)
        q_ref[:, hh * LANES:(hh + 1) * LANES] = qh.astype(q_ref.dtype)
    _mla_kv(ckv, kr_slot, wuk_ref, wuv_ref, k_ref, v_ref)


def _mla_proj(x, g, mod, layer, row0, tokens_per_row, w, rope_tabs, emit_state):
    t = x.shape[0]
    win, qg, kvg, wuq, wuk, wuv = w
    tok = lambda n: pl.BlockSpec((TM, n), lambda i: (i, 0))
    wq = MLA_HEADS * LANES
    in_specs = [tok(D_MODEL), _const_spec((1, D_MODEL)),
                _mod_spec(layer, 0, row0, tokens_per_row), _mod_spec(layer, 1, row0, tokens_per_row),
                _const_spec(win.shape), _const_spec(qg.shape), _const_spec(kvg.shape),
                _const_spec(wuq.shape), _const_spec(wuk.shape), _const_spec(wuv.shape)]
    args = [x, g, mod, mod, win, qg, kvg, wuq, wuk, wuv]
    rope = rope_tabs is not None
    if rope:
        n_tab = rope_tabs[0].shape[0] // TM
        in_specs += [pl.BlockSpec((TM, LANES), lambda i: (i % n_tab, 0)) for _ in range(3)]
        args += list(rope_tabs)
    out_specs = [tok(wq), tok(wq), tok(D_MODEL)]
    out_shape = [jax.ShapeDtypeStruct((t, wq), BF16), jax.ShapeDtypeStruct((t, wq), BF16),
                 jax.ShapeDtypeStruct((t, D_MODEL), BF16)]
    if emit_state:
        out_specs += [tok(MLA_KV_RANK), tok(MLA_ROPE_DIM)]
        out_shape += [jax.ShapeDtypeStruct((t, MLA_KV_RANK), F32), jax.ShapeDtypeStruct((t, MLA_ROPE_DIM), F32)]
    return pl.pallas_call(
        functools.partial(_mla_proj_kernel, rope=rope, emit_state=emit_state),
        grid=(t // TM,),
        in_specs=in_specs, out_specs=out_specs, out_shape=out_shape,
        compiler_params=_cparams(("arbitrary",)),
        name="mla_proj",
    )(*args)


def _mla_state_kv_kernel(ckv_ref, kr_ref, wuk_ref, wuv_ref, k_ref, v_ref):
    kr_slot = pltpu.roll(kr_ref[...], MLA_NOPE_DIM, 1)
    _mla_kv(ckv_ref[...], kr_slot, wuk_ref, wuv_ref, k_ref, v_ref)


def _mla_state_kv(ckv, kr_pad, wuk, wuv):
    t = ckv.shape[0]
    wq = MLA_HEADS * LANES
    tok = lambda n: pl.BlockSpec((TM, n), lambda i: (i, 0))
    return pl.pallas_call(
        _mla_state_kv_kernel,
        grid=(t // TM,),
        in_specs=[tok(MLA_KV_RANK), tok(LANES), _const_spec(wuk.shape), _const_spec(wuv.shape)],
        out_specs=[tok(wq), tok(D_MODEL)],
        out_shape=[jax.ShapeDtypeStruct((t, wq), BF16), jax.ShapeDtypeStruct((t, D_MODEL), BF16)],
        compiler_params=_cparams(("arbitrary",)),
        name="mla_state_kv",
    )(ckv, kr_pad, wuk, wuv)


def _lane_argmax(v, lane):
    top = v.max(axis=-1, keepdims=True)
    idx = jnp.where(v == top, lane, float(LANES)).min(axis=-1, keepdims=True)
    return top, idx


def _route(logits, lane):
    is_g = lane < N_GROUPS
    gl = jnp.where(is_g, logits, -jnp.inf)
    gmax, gsel = _lane_argmax(gl, lane)
    p_group = 1.0 / jnp.where(is_g, jnp.exp(logits - gmax), 0.0).sum(axis=-1, keepdims=True)
    first = N_GROUPS + gsel * EXPERTS_PER_GROUP
    el = jnp.where((lane >= first) & (lane < first + EXPERTS_PER_GROUP), logits, -jnp.inf)
    t1, i1 = _lane_argmax(el, lane)
    t2, i2 = _lane_argmax(jnp.where(lane == i1, -jnp.inf, el), lane)
    e2 = jnp.exp(t2 - t1)
    w1 = 1.0 / (1.0 + e2)
    gates = jnp.where(lane == i1, w1 * p_group, jnp.where(lane == i2, e2 * w1 * p_group, 0.0))
    return gates, gsel


def _pack_halves(x):
    half = x.shape[1] // 2
    lo = lax.bitcast_convert_type(x[:, :half].astype(BF16).astype(F32), jnp.uint32)
    hi = lax.bitcast_convert_type(x[:, half:].astype(BF16).astype(F32), jnp.uint32)
    return (lo >> 16) | hi


def _unpack_halves(p):
    lo = lax.bitcast_convert_type(p << 16, F32)
    hi = lax.bitcast_convert_type(p & jnp.uint32(0xFFFF0000), F32)
    return lo, hi


META_GROUP_LANE = EXPERTS_PER_GROUP
META_RANK_LANE = EXPERTS_PER_GROUP + 1


def _proj_res_kernel(o_ref, x_ref, w_ref, gate_ref, g_ref, sh_ref, sc_ref, wr_ref, br_ref, tri_ref,
                     xn_ref, hpk_ref, meta_ref, cnt_ref):
    @pl.when(pl.program_id(0) == 0)
    def _():
        cnt_ref[...] = jnp.zeros(cnt_ref.shape, cnt_ref.dtype)

    y = jnp.dot(o_ref[...], w_ref[...], preferred_element_type=F32)
    xn = x_ref[...] + gate_ref[0] * y
    xn_ref[...] = xn
    h = _norm_mod(xn, g_ref[...], sh_ref[0], sc_ref[0])
    hpk_ref[...] = _pack_halves(h)
    logits = jnp.dot(h.astype(BF16), wr_ref[...], preferred_element_type=F32) + br_ref[...]
    lane = lax.broadcasted_iota(jnp.int32, logits.shape, 1).astype(F32)
    gates, gsel = _route(logits, lane)
    rel = pltpu.roll(gates, LANES - N_GROUPS, 1)
    for g in range(1, N_GROUPS):
        rel = rel + pltpu.roll(gates, LANES - N_GROUPS - g * EXPERTS_PER_GROUP, 1)
    sel = lane == gsel
    onehot = jnp.where(sel, 1.0, 0.0)
    prefix = jnp.dot(tri_ref[...], onehot.astype(BF16), preferred_element_type=F32)
    rank = jnp.where(sel, prefix + cnt_ref[...] - 1.0, 0.0).sum(axis=-1, keepdims=True)
    cnt_ref[...] = cnt_ref[...] + onehot.sum(axis=0, keepdims=True)
    meta_ref[...] = jnp.where(lane < EXPERTS_PER_GROUP, rel,
                              jnp.where(lane == META_GROUP_LANE, gsel,
                                        jnp.where(lane == META_RANK_LANE, rank, 0.0)))


def _proj_res(o, x, w_o, g_ffn, mod, layer, row0, tokens_per_row, w_router, b_router):
    t = x.shape[0]
    tok = lambda n: pl.BlockSpec((TM, n), lambda i: (i, 0))
    ms = lambda k: _mod_spec(layer, k, row0, tokens_per_row)
    tri = jnp.asarray(np.tril(np.ones((TM, TM), np.float32)), BF16)
    return pl.pallas_call(
        _proj_res_kernel,
        grid=(t // TM,),
        in_specs=[tok(D_MODEL), tok(D_MODEL), _const_spec(w_o.shape), ms(2), _const_spec((1, D_MODEL)), ms(3), ms(4),
                  _const_spec(w_router.shape), _const_spec(b_router.shape), _const_spec((TM, TM))],
        out_specs=[tok(D_MODEL), tok(D_MODEL // 2), tok(LANES), _const_spec((1, LANES))],
        out_shape=[jax.ShapeDtypeStruct((t, D_MODEL), F32), jax.ShapeDtypeStruct((t, D_MODEL // 2), jnp.uint32),
                   jax.ShapeDtypeStruct((t, LANES), F32), jax.ShapeDtypeStruct((1, LANES), F32)],
        compiler_params=_cparams(("arbitrary",)),
        name="proj_res",
    )(o, x, w_o, mod, g_ffn, mod, mod, w_router, b_router, tri)


def _moe_plan(meta, cnt):
    t = meta.shape[0]
    n_tiles = t // TM + N_GROUPS
    group = meta[:, META_GROUP_LANE].astype(jnp.int32)
    rank = meta[:, META_RANK_LANE].astype(jnp.int32)
    counts = cnt[0, :N_GROUPS].astype(jnp.int32)
    tiles = (counts + TM - 1) // TM
    tile_end = jnp.cumsum(tiles)
    tile_start = tile_end - tiles
    pos = rank + TM * jnp.sum(jnp.where(group[:, None] == jnp.arange(N_GROUPS)[None], tile_start[None], 0), axis=1)
    tile_group = jnp.minimum(jnp.sum(jnp.arange(n_tiles)[:, None] >= tile_end[None], axis=1), N_GROUPS - 1)
    return pos.astype(jnp.int32), tile_group.astype(jnp.int32), tile_end[-1:].astype(jnp.int32)


def _gather_rows(src_ref, idx_ref, base, dst_ref):
    def body(r, carry):
        dst_ref[pl.ds(r, 1), :] = src_ref[pl.ds(idx_ref[base + r], 1), :]
        return carry
    lax.fori_loop(0, dst_ref.shape[0], body, 0, unroll=8)


def _moe_kernel(pos_ref, tg_ref, nt_ref, hpk_ref, meta_ref, wgu_ref, wd_ref, ypk_ref, src_ref, hbuf, gbuf, hid_ref):
    i = pl.program_id(0)
    n_tok = hpk_ref.shape[0]

    @pl.when(i == 0)
    def _():
        def clear(p, c):
            src_ref[p] = 0
            return c
        lax.fori_loop(0, src_ref.shape[0], clear, 0, unroll=8)

        def fill(t, c):
            src_ref[pos_ref[t]] = t
            return c
        lax.fori_loop(0, n_tok, fill, 0, unroll=8)

    @pl.when(i < nt_ref[0])
    def _():
        _gather_rows(hpk_ref, src_ref, i * TM, hbuf)
        _gather_rows(meta_ref, src_ref, i * TM, gbuf)
        lo, hi = _unpack_halves(hbuf[...])
        h = jnp.concatenate([lo.astype(BF16), hi.astype(BF16)], axis=1)
        gates = gbuf[...]
        f = D_FF_EXPERT
        for j in range(EXPERTS_PER_GROUP):
            gu = jnp.dot(h, wgu_ref[0, j], preferred_element_type=F32)
            hid = _silu(gu[:, :f]) * gu[:, f:] * gates[:, j:j + 1]
            hid_ref[:, j * f:(j + 1) * f] = hid.astype(BF16)
        ypk_ref[...] = _pack_halves(jnp.dot(hid_ref[...], wd_ref[0], preferred_element_type=F32))

    @pl.when(i >= nt_ref[0])
    def _():
        ypk_ref[...] = jnp.zeros(ypk_ref.shape, ypk_ref.dtype)


def _moe(hpk, meta, pos, tile_group, n_used, wgu, wd):
    t = hpk.shape[0]
    n_tiles = t // TM + N_GROUPS
    f = D_FF_EXPERT
    grid_spec = pltpu.PrefetchScalarGridSpec(
        num_scalar_prefetch=3,
        grid=(n_tiles,),
        in_specs=[
            pl.BlockSpec(hpk.shape, lambda i, *_: (0, 0), pipeline_mode=pl.Buffered(1)),
            pl.BlockSpec(meta.shape, lambda i, *_: (0, 0), pipeline_mode=pl.Buffered(1)),
            pl.BlockSpec((1, EXPERTS_PER_GROUP, D_MODEL, 2 * f), lambda i, pos, tg, nt: (tg[i], 0, 0, 0)),
            pl.BlockSpec((1, EXPERTS_PER_GROUP * f, D_MODEL), lambda i, pos, tg, nt: (tg[i], 0, 0)),
        ],
        out_specs=pl.BlockSpec((TM, D_MODEL // 2), lambda i, *_: (i, 0)),
        scratch_shapes=[pltpu.SMEM((n_tiles * TM,), jnp.int32),
                        pltpu.VMEM((TM, D_MODEL // 2), jnp.uint32),
                        pltpu.VMEM((TM, LANES), F32),
                        pltpu.VMEM((TM, EXPERTS_PER_GROUP * f), BF16)],
    )
    return pl.pallas_call(
        _moe_kernel,
        grid_spec=grid_spec,
        out_shape=jax.ShapeDtypeStruct((n_tiles * TM, D_MODEL // 2), jnp.uint32),
        compiler_params=_cparams(("arbitrary",)),
        name="moe",
    )(pos, tile_group, n_used, hpk, meta, wgu, wd)


def _combine_kernel(*refs, final_norm):
    pos_ref, ypk_ref, x_ref, gate_ref = refs[:4]
    o_ref, ybuf = refs[-2], refs[-1]
    _gather_rows(ypk_ref, pos_ref, pl.program_id(0) * TM, ybuf)
    lo, hi = _unpack_halves(ybuf[...])
    out = x_ref[...] + gate_ref[0] * jnp.concatenate([lo, hi], axis=1)
    if final_norm:
        out = _rms(out, refs[4][...])
    o_ref[...] = out


def _combine(ypk, pos, x, mod, layer, row0, tokens_per_row, final_g):
    t = x.shape[0]

    def mod_map(i, pos):
        row = row0 + (i * TM) // tokens_per_row
        return ((layer * MOD_ROWS + row) * N_MOD + 5, 0, 0)

    in_specs = [pl.BlockSpec(ypk.shape, lambda i, pos: (0, 0), pipeline_mode=pl.Buffered(1)),
                pl.BlockSpec((TM, D_MODEL), lambda i, pos: (i, 0)),
                pl.BlockSpec((1, 1, D_MODEL), mod_map)]
    args = [ypk, x, mod]
    if final_g is not None:
        in_specs.append(pl.BlockSpec((1, D_MODEL), lambda i, pos: (0, 0)))
        args.append(final_g)
    grid_spec = pltpu.PrefetchScalarGridSpec(
        num_scalar_prefetch=1,
        grid=(t // TM,),
        in_specs=in_specs,
        out_specs=pl.BlockSpec((TM, D_MODEL), lambda i, pos: (i, 0)),
        scratch_shapes=[pltpu.VMEM((TM, D_MODEL // 2), jnp.uint32)],
    )
    return pl.pallas_call(
        functools.partial(_combine_kernel, final_norm=final_g is not None),
        grid_spec=grid_spec,
        out_shape=jax.ShapeDtypeStruct((t, D_MODEL), F32),
        compiler_params=_cparams(("arbitrary",)),
        name="moe_combine",
    )(pos, *args)


def _pad_heads(w, used):
    k = w.shape[0]
    w = w.reshape(k, MLA_HEADS, used)
    return jnp.pad(w, ((0, 0), (0, 0), (0, LANES - used))).reshape(k, MLA_HEADS * LANES)


def kernel(x_prompt, x_sample, state_na_k, state_na_v, state_mla_ckv, state_mla_kr, c, c_ctx, ada_w, ada_b, norm_mix_g, norm_ffn_g, final_norm_g, na_w_qkv, na_w_o, na_rpb, mla_w_in, mla_q_norm_g, mla_kv_norm_g, mla_w_uq, mla_w_uk, mla_w_uv, mla_w_o, moe_w_group, moe_b_group, moe_w_expert, moe_b_expert, moe_w_gate, moe_w_up, moe_w_down):
    n_b, seq, d = x_prompt.shape
    dec_b, n_lat, _ = x_sample.shape
    past = state_na_k.shape[2]
    rows = n_lat // GRID_W
    assert d == D_MODEL and dec_b + 1 <= MOD_ROWS and n_lat % TM == 0 and (n_b * seq) % TM == 0

    cond8 = jnp.concatenate([c_ctx[None], c, jnp.zeros((MOD_ROWS - 1 - dec_b, d), F32)], axis=0)
    mod = _ada_mod(cond8, ada_w, ada_b)

    xp = x_prompt.reshape(n_b * seq, d)
    xs = x_sample.reshape(dec_b * n_lat, d)
    p_rows = (0, n_b * seq)
    s_rows = (1, n_lat)
    outs = {}

    for i in range(DEPTH):
        j = i // 2
        g_mix = norm_mix_g[i][None]
        g_ffn = norm_ffn_g[i][None]
        if i % 2 == 0:
            w_qkv = na_w_qkv[j].astype(BF16)
            qp, kp, vp = _na_qkv(xp, g_mix, mod, i, *p_rows, w_qkv, F32)
            qs, ks, vs = _na_qkv(xs, g_mix, mod, i, *s_rows, w_qkv, BF16)
            outs['na_k'] = kp.reshape(n_b, 1, seq, NA_HEADS, NA_HEAD_DIM)
            outs['na_v'] = vp.reshape(n_b, 1, seq, NA_HEADS, NA_HEAD_DIM)
            op = _attention(qp, [(kp, vp)], n_b, seq, wide_qk=False, pairs=8)
            tables = _na_bias_tables(na_rpb[j], rows)
            os_ = _na_latent(qs, ks, vs, state_na_k[:, j].reshape(dec_b * past, d),
                             state_na_v[:, j].reshape(dec_b * past, d), tables, dec_b)
            w_o = na_w_o[j].astype(BF16)
        else:
            pad_in = MLA_IN_PAD - mla_w_in.shape[-1]
            w_mla = (jnp.pad(mla_w_in[j], ((0, 0), (0, pad_in))).astype(BF16),
                     mla_q_norm_g[j][None], mla_kv_norm_g[j][None],
                     _pad_heads(mla_w_uq[j], MLA_QK_DIM).astype(BF16),
                     _pad_heads(mla_w_uk[j], MLA_NOPE_DIM).astype(BF16),
                     mla_w_uv[j].astype(BF16))
            qp, kp, vp, ckv_p, kr_p = _mla_proj(xp, g_mix, mod, i, *p_rows, w_mla, None, True)
            qs, ks, vs = _mla_proj(xs, g_mix, mod, i, *s_rows, w_mla, _rope_lane_tables(n_lat), False)
            outs['ckv'] = ckv_p.reshape(n_b, 1, seq, MLA_KV_RANK)
            outs['kr'] = kr_p.reshape(n_b, 1, seq, MLA_ROPE_DIM)
            kr_state = jnp.pad(state_mla_kr[:, j].reshape(dec_b * past, MLA_ROPE_DIM),
                               ((0, 0), (0, LANES - MLA_ROPE_DIM)))
            kc, vc = _mla_state_kv(state_mla_ckv[:, j].reshape(dec_b * past, MLA_KV_RANK), kr_state,
                                   w_mla[4], w_mla[5])
            op = _attention(qp, [(kp, vp)], n_b, seq, wide_qk=True, pairs=8)
            os_ = _attention(qs, [(ks, vs), (kc, vc)], dec_b, 512, wide_qk=True, pairs=1)
            w_o = mla_w_o[j].astype(BF16)

        w_router = jnp.pad(jnp.concatenate([moe_w_group[i], moe_w_expert[i]], axis=1),
                           ((0, 0), (0, LANES - N_GROUPS - N_EXPERTS))).astype(BF16)
        b_router = jnp.pad(jnp.concatenate([moe_b_group[i], moe_b_expert[i]]),
                           (0, LANES - N_GROUPS - N_EXPERTS))[None]
        f = D_FF_EXPERT
        wgu = jnp.concatenate([moe_w_gate[i], moe_w_up[i]], axis=-1).astype(BF16)
        wgu = wgu.reshape(N_GROUPS, EXPERTS_PER_GROUP, d, 2 * f)
        wd = moe_w_down[i].astype(BF16).reshape(N_GROUPS, EXPERTS_PER_GROUP * f, d)
        final_g = final_norm_g[None] if i == DEPTH - 1 else None
        new_x = []
        for o_att, x, rows_ in ((op, xp, p_rows), (os_, xs, s_rows)):
            xn, hpk, meta, cnt = _proj_res(o_att, x, w_o, g_ffn, mod, i, *rows_, w_router, b_router)
            pos, tile_group, n_used = _moe_plan(meta, cnt)
            ypk = _moe(hpk, meta, pos, tile_group, n_used, wgu, wd)
            new_x.append(_combine(ypk, pos, xn, mod, i, *rows_, final_g))
        xp, xs = new_x

    return (xp.reshape(n_b, seq, d), xs.reshape(dec_b, n_lat, d),
            outs['na_k'], outs['na_v'], outs['ckv'], outs['kr'])
```

```python
import functools

import numpy as np
import jax
import jax.numpy as jnp
from jax import lax
from jax.experimental import pallas as pl
from jax.experimental.pallas import tpu as pltpu

F32 = jnp.float32
BF16 = jnp.bfloat16

D_MODEL = 1024
DEPTH = 2
GRID_W = 64
N_MOD = 6
EPS = 1e-6
NEG_INF = -1e30
NA_HEADS = 16
NA_HEAD_DIM = 64
NA_KH = 8
NA_KW = 16
NA_GROUP_ROWS = 4
NA_WIN_ROWS = 12
MLA_HEADS = 16
MLA_Q_RANK = 384
MLA_KV_RANK = 256
MLA_NOPE_DIM = 64
MLA_ROPE_DIM = 32
MLA_V_DIM = 64
MLA_QK_DIM = MLA_NOPE_DIM + MLA_ROPE_DIM
MLA_IN_PAD = 768
ROPE_THETA = 10000.0
LOG2_E = 1.4426950408889634
N_GROUPS = 4
EXPERTS_PER_GROUP = 4
N_EXPERTS = 16
D_FF_EXPERT = 256

LANES = 128
MOD_ROWS = 8
VMEM_LIMIT = 56 * 1024 * 1024
TM = 512


def _cparams(sem):
    return pltpu.CompilerParams(dimension_semantics=sem, vmem_limit_bytes=VMEM_LIMIT)


def _const_spec(shape):
    nd = len(shape)
    return pl.BlockSpec(shape, lambda *_: (0,) * nd)


def _norm_mod(x, g, shift, scale):
    ms = jnp.mean(x * x, axis=-1, keepdims=True)
    y = x * lax.rsqrt(ms + EPS) * g
    return y * (1.0 + scale) + shift


def _rms(x, g):
    ms = jnp.mean(x * x, axis=-1, keepdims=True)
    return x * lax.rsqrt(ms + EPS) * g


def _silu(x):
    return x / (1.0 + jnp.exp(-x))


def _ada_kernel(cond_ref, w_ref, b_ref, o_ref):
    s = _silu(cond_ref[...]).astype(BF16)
    o_ref[0] = jnp.dot(s, w_ref[0].astype(BF16), preferred_element_type=F32) + b_ref[0]


def _ada_mod(cond8, ada_w, ada_b):
    tn = 1024
    n = N_MOD * D_MODEL
    out = pl.pallas_call(
        _ada_kernel,
        grid=(DEPTH, n // tn),
        in_specs=[
            _const_spec((MOD_ROWS, D_MODEL)),
            pl.BlockSpec((1, D_MODEL, tn), lambda l, j: (l, 0, j)),
            pl.BlockSpec((1, 1, tn), lambda l, j: (l, 0, j)),
        ],
        out_specs=pl.BlockSpec((1, MOD_ROWS, tn), lambda l, j: (l, 0, j)),
        out_shape=jax.ShapeDtypeStruct((DEPTH, MOD_ROWS, n), F32),
        compiler_params=_cparams(("arbitrary", "arbitrary")),
        name="ada_mod",
    )(cond8, ada_w, ada_b.reshape(DEPTH, 1, n))
    return out.reshape(DEPTH * MOD_ROWS * N_MOD, 1, D_MODEL)


def _mod_spec(layer, k, row0, tokens_per_row):
    def imap(i):
        row = row0 + (i * TM) // tokens_per_row
        return ((layer * MOD_ROWS + row) * N_MOD + k, 0, 0)
    return pl.BlockSpec((1, 1, D_MODEL), imap)


def _qkv_kernel(x_ref, g_ref, sh_ref, sc_ref, w_ref, q_ref, k_ref, v_ref):
    h = _norm_mod(x_ref[...], g_ref[...], sh_ref[0], sc_ref[0]).astype(BF16)
    y = jnp.dot(h, w_ref[...], preferred_element_type=F32)
    q_ref[...] = (y[:, :D_MODEL] * (NA_HEAD_DIM ** -0.5)).astype(q_ref.dtype)
    k_ref[...] = y[:, D_MODEL:2 * D_MODEL].astype(k_ref.dtype)
    v_ref[...] = y[:, 2 * D_MODEL:].astype(v_ref.dtype)


def _na_qkv(x, g, mod, layer, row0, tokens_per_row, w_qkv, kv_dtype):
    t = x.shape[0]
    tok = pl.BlockSpec((TM, D_MODEL), lambda i: (i, 0))
    return pl.pallas_call(
        _qkv_kernel,
        grid=(t // TM,),
        in_specs=[tok, _const_spec((1, D_MODEL)),
                  _mod_spec(layer, 0, row0, tokens_per_row), _mod_spec(layer, 1, row0, tokens_per_row),
                  _const_spec((D_MODEL, 3 * D_MODEL))],
        out_specs=[tok, tok, tok],
        out_shape=[jax.ShapeDtypeStruct((t, D_MODEL), BF16),
                   jax.ShapeDtypeStruct((t, D_MODEL), kv_dtype),
                   jax.ShapeDtypeStruct((t, D_MODEL), kv_dtype)],
        compiler_params=_cparams(("arbitrary",)),
        name="na_qkv",
    )(x, g, mod, mod, w_qkv)


def _softmax_pv(s_list, v_list):
    m = s_list[0].max(axis=-1, keepdims=True)
    for s in s_list[1:]:
        m = jnp.maximum(m, s.max(axis=-1, keepdims=True))
    l = None
    acc = None
    for s, v in zip(s_list, v_list):
        p = jnp.exp(s - m)
        ls = p.sum(axis=-1, keepdims=True)
        a = jnp.dot(p.astype(BF16), v, preferred_element_type=F32)
        l = ls if l is None else l + ls
        acc = a if acc is None else acc + a
    return acc * (1.0 / l)


def _qk(q, k):
    return lax.dot_general(q, k, (((1,), (1,)), ((), ())), preferred_element_type=F32)


def _head_masks():
    lane = lax.broadcasted_iota(jnp.int32, (1, LANES), 1)
    lo = lane < (LANES // 2)
    return lo, jnp.where(lo, 1.0, 0.0).astype(BF16), jnp.where(lo, 0.0, 1.0).astype(BF16)


def _attn_kernel(*refs, n_src, wide_qk, pairs):
    q_ref, o_ref = refs[0], refs[-1]
    kv = refs[1:-1]
    lo, m0, m1 = _head_masks()
    wq = 2 * LANES if wide_qk else LANES
    for p in range(pairs):
        q = q_ref[:, p * wq:(p + 1) * wq]
        ks = [kv[2 * t][:, p * wq:(p + 1) * wq].astype(BF16) for t in range(n_src)]
        vs = [kv[2 * t + 1][:, p * LANES:(p + 1) * LANES].astype(BF16) for t in range(n_src)]
        outs = []
        for j in range(2):
            if wide_qk:
                s_list = [_qk(q[:, j * LANES:(j + 1) * LANES], k[:, j * LANES:(j + 1) * LANES]) for k in ks]
            else:
                s_list = [_qk(q * (m0 if j == 0 else m1), k) for k in ks]
            outs.append(_softmax_pv(s_list, vs))
        o_ref[:, p * LANES:(p + 1) * LANES] = jnp.where(lo, outs[0], outs[1]).astype(o_ref.dtype)


def _attention(q, kv_srcs, n_batch, tq, wide_qk, pairs):
    wq = (2 * LANES if wide_qk else LANES) * pairs
    wv = LANES * pairs
    t_q = q.shape[0] // n_batch
    nq = t_q // tq
    n_steps = D_MODEL // wv
    in_specs = [pl.BlockSpec((tq, wq), lambda b, p, i: (b * nq + i, p))]
    args = [q]
    for k, v in kv_srcs:
        t_k = k.shape[0] // n_batch
        in_specs.append(pl.BlockSpec((t_k, wq), lambda b, p, i: (b, p)))
        in_specs.append(pl.BlockSpec((t_k, wv), lambda b, p, i: (b, p)))
        args += [k, v]
    return pl.pallas_call(
        functools.partial(_attn_kernel, n_src=len(kv_srcs), wide_qk=wide_qk, pairs=pairs),
        grid=(n_batch, n_steps, nq),
        in_specs=in_specs,
        out_specs=pl.BlockSpec((tq, wv), lambda b, p, i: (b * nq + i, p)),
        out_shape=jax.ShapeDtypeStruct((q.shape[0], D_MODEL), BF16),
        compiler_params=_cparams(("arbitrary", "arbitrary", "arbitrary")),
        name="attention",
    )(*args)


def _col_max(s, chunk=64):
    acc = s[:chunk]
    for r in range(chunk, s.shape[0], chunk):
        acc = jnp.maximum(acc, s[r:r + chunk])
    return acc.max(axis=0, keepdims=True)


def _attn_t_kernel(q_ref, k1_ref, k2_ref, v1_ref, v2_ref, o_ref):
    n_heads = q_ref.shape[1] // LANES
    slots = [slice(j * LANES, (j + 1) * LANES) for j in range(n_heads)]
    scores = [(_qk(k1_ref[:, sl], q_ref[:, sl]), _qk(k2_ref[:, sl], q_ref[:, sl])) for sl in slots]
    outs = []
    for sl, (s1, s2) in zip(slots, scores):
        m = jnp.maximum(_col_max(s1), _col_max(s2))
        p1 = jnp.exp2(s1 - m).astype(BF16)
        p2 = jnp.exp2(s2 - m).astype(BF16)
        ot = (jnp.dot(v1_ref[sl, :], p1, preferred_element_type=F32)
              + jnp.dot(v2_ref[sl, :], p2, preferred_element_type=F32))
        outs.append(ot[:MLA_V_DIM] * (1.0 / ot[MLA_V_DIM:MLA_V_DIM + 1]))
    for j in range(0, n_heads, 2):
        pair = jnp.concatenate(outs[j:j + 2], axis=0).T
        o_ref[:, (j // 2) * LANES:(j // 2 + 1) * LANES] = pair.astype(o_ref.dtype)


def _attention_t(q, k1, v1t, k2, v2t, n_batch, tq, heads):
    wq = heads * LANES
    wo = heads * MLA_V_DIM
    nq = q.shape[0] // n_batch // tq
    t1 = k1.shape[0] // n_batch
    t2 = k2.shape[0] // n_batch
    return pl.pallas_call(
        _attn_t_kernel,
        grid=(n_batch, MLA_HEADS // heads, nq),
        in_specs=[pl.BlockSpec((tq, wq), lambda b, p, i: (b * nq + i, p)),
                  pl.BlockSpec((t1, wq), lambda b, p, i: (b, p)),
                  pl.BlockSpec((t2, wq), lambda b, p, i: (b, p)),
                  pl.BlockSpec((wq, t1), lambda b, p, i: (p, b)),
                  pl.BlockSpec((wq, t2), lambda b, p, i: (p, b))],
        out_specs=pl.BlockSpec((tq, wo), lambda b, p, i: (b * nq + i, p)),
        out_shape=jax.ShapeDtypeStruct((q.shape[0], D_MODEL), BF16),
        compiler_params=_cparams(("arbitrary", "arbitrary", "arbitrary")),
        name="attention_t",
    )(q, k1, k2, v1t, v2t)


def _na_group_layout(rows):
    n_groups = rows // NA_GROUP_ROWS
    kh = min(NA_KH, rows)
    bases, types, keys = [], [], {}
    for gi in range(n_groups):
        r0 = gi * NA_GROUP_ROWS
        base = int(np.clip(r0 - kh // 2, 0, rows - NA_WIN_ROWS))
        starts = tuple(int(np.clip(r - kh // 2, 0, rows - kh)) - base for r in range(r0, r0 + NA_GROUP_ROWS))
        key = (starts, r0 - base)
        if key not in keys:
            keys[key] = len(keys)
        bases.append(base)
        types.append(keys[key])
    return bases, types, list(keys.keys())


def _na_bias_tables(rpb, rows):
    kh = min(NA_KH, rows)
    _, _, type_keys = _na_group_layout(rows)
    qc = np.arange(GRID_W)
    kc = np.arange(GRID_W)
    col_start = np.clip(qc - NA_KW // 2, 0, GRID_W - NA_KW)
    col_valid = (kc[None, :] >= col_start[:, None]) & (kc[None, :] < col_start[:, None] + NA_KW)
    dx = kc[None, :] - qc[:, None] + (NA_KW - 1)
    onehot = ((dx[None] == np.arange(2 * NA_KW - 1)[:, None, None]) & col_valid[None]).astype(np.float32)
    band = jnp.einsum('hyd,dqk->hyqk', rpb.astype(F32), onehot, precision=lax.Precision.HIGHEST)
    band = jnp.where(col_valid, band, NEG_INF)
    neg = jnp.full((NA_HEADS, GRID_W, GRID_W), NEG_INF, F32)
    tables = []
    for starts, r_off in type_keys:
        q_rows = []
        for rq in range(NA_GROUP_ROWS):
            blocks = []
            for ry in range(NA_WIN_ROWS):
                dy = ry - (rq + r_off) + (NA_KH - 1)
                inside = starts[rq] <= ry < starts[rq] + kh
                blocks.append(band[:, dy] if inside else neg)
            q_rows.append(jnp.concatenate(blocks, axis=-1))
        tables.append(jnp.concatenate(q_rows, axis=1))
    return jnp.stack(tables, axis=1)


def _na_latent_kernel(q_ref, k_ref, v_ref, kc_ref, vc_ref, tab_ref, o_ref, *, bases, types):
    lo, m0, m1 = _head_masks()
    kctx = kc_ref[...].astype(BF16)
    vctx = vc_ref[...].astype(BF16)
    gq = NA_GROUP_ROWS * GRID_W
    gk = NA_WIN_ROWS * GRID_W
    for gi, (base, ty) in enumerate(zip(bases, types)):
        q = q_ref[gi * gq:(gi + 1) * gq, :]
        kwin = k_ref[base * GRID_W:base * GRID_W + gk, :]
        vwin = v_ref[base * GRID_W:base * GRID_W + gk, :]
        outs = []
        for j in range(2):
            qj = q * (m0 if j == 0 else m1)
            s_win = _qk(qj, kwin) + tab_ref[j, ty]
            s_ctx = _qk(qj, kctx)
            outs.append(_softmax_pv([s_win, s_ctx], [vwin, vctx]))
        o_ref[gi * gq:(gi + 1) * gq, :] = jnp.where(lo, outs[0], outs[1]).astype(o_ref.dtype)


def _na_latent(q, k, v, k_ctx, v_ctx, tables, n_batch):
    t = q.shape[0] // n_batch
    rows = t // GRID_W
    t_ctx = k_ctx.shape[0] // n_batch
    bases, types, type_keys = _na_group_layout(rows)
    n_types = len(type_keys)
    gq = NA_GROUP_ROWS * GRID_W
    gk = NA_WIN_ROWS * GRID_W
    tok = pl.BlockSpec((t, LANES), lambda p, b: (b, p))
    ctx = pl.BlockSpec((t_ctx, LANES), lambda p, b: (b, p))
    return pl.pallas_call(
        functools.partial(_na_latent_kernel, bases=bases, types=types),
        grid=(D_MODEL // LANES, n_batch),
        in_specs=[tok, tok, tok, ctx, ctx,
                  pl.BlockSpec((2, n_types, gq, gk), lambda p, b: (p, 0, 0, 0))],
        out_specs=tok,
        out_shape=jax.ShapeDtypeStruct(q.shape, BF16),
        compiler_params=_cparams(("arbitrary", "arbitrary")),
        name="na_latent",
    )(q, k, v, k_ctx, v_ctx, tables)


def _rope_lane_tables(n_tokens):
    n_freq = MLA_ROPE_DIM // 4
    inv = 1.0 / (ROPE_THETA ** (jnp.arange(n_freq, dtype=F32) / n_freq))
    tt = jnp.arange(n_tokens)
    row = (tt // GRID_W).astype(F32)
    col = (tt % GRID_W).astype(F32)
    ang = jnp.concatenate([row[:, None] * inv, col[:, None] * inv], axis=-1)
    cos, sin = jnp.cos(ang), jnp.sin(ang)
    ones = jnp.ones((n_tokens, MLA_NOPE_DIM), F32)
    z16 = jnp.zeros((n_tokens, MLA_ROPE_DIM // 2), F32)
    z32 = jnp.zeros((n_tokens, LANES - MLA_QK_DIM), F32)
    z64 = jnp.zeros((n_tokens, MLA_NOPE_DIM), F32)
    c_tab = jnp.concatenate([ones, cos, cos, z32], axis=-1)
    s_dn = jnp.concatenate([z64, -sin, z16, z32], axis=-1)
    s_up = jnp.concatenate([z64, z16, sin, z32], axis=-1)
    return c_tab, s_dn, s_up


def _rope_slot(x, c_tab, s_dn, s_up):
    half = MLA_ROPE_DIM // 2
    return x * c_tab + pltpu.roll(x, LANES - half, 1) * s_dn + pltpu.roll(x, half, 1) * s_up


def _mla_kv(ckv_n, kr_slot, wuk_ref, wuv_ref, k_ref, v_ref, v_transposed):
    c = ckv_n.astype(BF16)
    k_nope = jnp.dot(c, wuk_ref[...], preferred_element_type=F32)
    for h in range(MLA_HEADS):
        k_ref[:, h * LANES:(h + 1) * LANES] = (k_nope[:, h * LANES:(h + 1) * LANES] + kr_slot).astype(k_ref.dtype)
    if v_transposed:
        vt = _qk(wuv_ref[...], c)
        row = lax.broadcasted_iota(jnp.int32, vt.shape, 0)
        v_ref[...] = jnp.where((row & (LANES - 1)) == MLA_V_DIM, 1.0, vt).astype(v_ref.dtype)
    else:
        v_ref[...] = jnp.dot(c, wuv_ref[...], preferred_element_type=F32).astype(v_ref.dtype)


def _mla_proj_kernel(*refs, rope, emit_state, q_scale):
    (x_ref, g_ref, sh_ref, sc_ref, win_ref, qg_ref, kvg_ref, wuq_ref, wuk_ref, wuv_ref) = refs[:10]
    pos = 10
    if rope:
        ct_ref, sd_ref, su_ref = refs[pos:pos + 3]
        pos += 3
    q_ref, k_ref, v_ref = refs[pos:pos + 3]
    pos += 3
    h = _norm_mod(x_ref[...], g_ref[...], sh_ref[0], sc_ref[0]).astype(BF16)
    c = jnp.dot(h, win_ref[...], preferred_element_type=F32)
    cq = _rms(c[:, :MLA_Q_RANK], qg_ref[...]).astype(BF16)
    ckv = _rms(c[:, MLA_Q_RANK:MLA_Q_RANK + MLA_KV_RANK], kvg_ref[...])
    kr_chunk = c[:, MLA_Q_RANK + MLA_KV_RANK:]
    if emit_state:
        ckv_out, kr_out = refs[pos:pos + 2]
        ckv_out[...] = ckv
        kr_out[...] = kr_chunk[:, :MLA_ROPE_DIM]
    kr_slot = pltpu.roll(kr_chunk, MLA_NOPE_DIM, 1)
    q = jnp.dot(cq, wuq_ref[...], preferred_element_type=F32) * q_scale
    if rope:
        ct, sd, su = ct_ref[...], sd_ref[...], su_ref[...]
        kr_slot = _rope_slot(kr_slot, ct, sd, su)
    for hh in range(MLA_HEADS):
        qh = q[:, hh * LANES:(hh + 1) * LANES]
        if rope:
            qh = _rope_slot(qh, ct, sd, su)
        q_ref[:, hh * LANES:(hh + 1) * LANES] = qh.astype(q_ref.dtype)
    _mla_kv(ckv, kr_slot, wuk_ref, wuv_ref, k_ref, v_ref, v_transposed=rope)


def _mla_proj(x, g, mod, layer, row0, tokens_per_row, w, rope_tabs, emit_state):
    t = x.shape[0]
    win, qg, kvg, wuq, wuk, wuv = w
    tok = lambda n: pl.BlockSpec((TM, n), lambda i: (i, 0))
    wq = MLA_HEADS * LANES
    in_specs = [tok(D_MODEL), _const_spec((1, D_MODEL)),
                _mod_spec(layer, 0, row0, tokens_per_row), _mod_spec(layer, 1, row0, tokens_per_row),
                _const_spec(win.shape), _const_spec(qg.shape), _const_spec(kvg.shape),
                _const_spec(wuq.shape), _const_spec(wuk.shape), _const_spec(wuv.shape)]
    args = [x, g, mod, mod, win, qg, kvg, wuq, wuk, wuv]
    rope = rope_tabs is not None
    if rope:
        n_tab = rope_tabs[0].shape[0] // TM
        in_specs += [pl.BlockSpec((TM, LANES), lambda i: (i % n_tab, 0)) for _ in range(3)]
        args += list(rope_tabs)
    if rope:
        v_spec, v_shape = pl.BlockSpec((wq, TM), lambda i: (0, i)), jax.ShapeDtypeStruct((wq, t), BF16)
        q_scale = MLA_QK_DIM ** -0.5 * LOG2_E
    else:
        v_spec, v_shape = tok(D_MODEL), jax.ShapeDtypeStruct((t, D_MODEL), BF16)
        q_scale = MLA_QK_DIM ** -0.5
    out_specs = [tok(wq), tok(wq), v_spec]
    out_shape = [jax.ShapeDtypeStruct((t, wq), BF16), jax.ShapeDtypeStruct((t, wq), BF16), v_shape]
    if emit_state:
        out_specs += [tok(MLA_KV_RANK), tok(MLA_ROPE_DIM)]
        out_shape += [jax.ShapeDtypeStruct((t, MLA_KV_RANK), F32), jax.ShapeDtypeStruct((t, MLA_ROPE_DIM), F32)]
    return pl.pallas_call(
        functools.partial(_mla_proj_kernel, rope=rope, emit_state=emit_state, q_scale=q_scale),
        grid=(t // TM,),
        in_specs=in_specs, out_specs=out_specs, out_shape=out_shape,
        compiler_params=_cparams(("arbitrary",)),
        name="mla_proj",
    )(*args)


def _mla_state_kv_kernel(ckv_ref, kr_ref, wuk_ref, wuv_ref, k_ref, v_ref):
    kr_slot = pltpu.roll(kr_ref[...], MLA_NOPE_DIM, 1)
    _mla_kv(ckv_ref[...], kr_slot, wuk_ref, wuv_ref, k_ref, v_ref, v_transposed=True)


def _mla_state_kv(ckv, kr_pad, wuk, wuv_t):
    t = ckv.shape[0]
    wq = MLA_HEADS * LANES
    tok = lambda n: pl.BlockSpec((TM, n), lambda i: (i, 0))
    return pl.pallas_call(
        _mla_state_kv_kernel,
        grid=(t // TM,),
        in_specs=[tok(MLA_KV_RANK), tok(LANES), _const_spec(wuk.shape), _const_spec(wuv_t.shape)],
        out_specs=[tok(wq), pl.BlockSpec((wq, TM), lambda i: (0, i))],
        out_shape=[jax.ShapeDtypeStruct((t, wq), BF16), jax.ShapeDtypeStruct((wq, t), BF16)],
        compiler_params=_cparams(("arbitrary",)),
        name="mla_state_kv",
    )(ckv, kr_pad, wuk, wuv_t)


def _lane_argmax(v, lane):
    top = v.max(axis=-1, keepdims=True)
    idx = jnp.where(v == top, lane, float(LANES)).min(axis=-1, keepdims=True)
    return top, idx


def _route(logits, lane):
    is_g = lane < N_GROUPS
    gl = jnp.where(is_g, logits, -jnp.inf)
    gmax, gsel = _lane_argmax(gl, lane)
    p_group = 1.0 / jnp.where(is_g, jnp.exp(logits - gmax), 0.0).sum(axis=-1, keepdims=True)
    first = N_GROUPS + gsel * EXPERTS_PER_GROUP
    el = jnp.where((lane >= first) & (lane < first + EXPERTS_PER_GROUP), logits, -jnp.inf)
    t1, i1 = _lane_argmax(el, lane)
    t2, i2 = _lane_argmax(jnp.where(lane == i1, -jnp.inf, el), lane)
    e2 = jnp.exp(t2 - t1)
    w1 = 1.0 / (1.0 + e2)
    gates = jnp.where(lane == i1, w1 * p_group, jnp.where(lane == i2, e2 * w1 * p_group, 0.0))
    return gates, gsel


def _pack_halves(x):
    half = x.shape[1] // 2
    lo = lax.bitcast_convert_type(x[:, :half].astype(BF16).astype(F32), jnp.uint32)
    hi = lax.bitcast_convert_type(x[:, half:].astype(BF16).astype(F32), jnp.uint32)
    return (lo >> 16) | hi


def _unpack_halves(p):
    lo = lax.bitcast_convert_type(p << 16, F32)
    hi = lax.bitcast_convert_type(p & jnp.uint32(0xFFFF0000), F32)
    return lo, hi


ROW_WORDS = D_MODEL // 2 + LANES
META_GROUP_LANE = EXPERTS_PER_GROUP
META_RANK_LANE = EXPERTS_PER_GROUP + 1


def _proj_res_kernel(o_ref, x_ref, w_ref, gate_ref, g_ref, sh_ref, sc_ref, wr_ref, br_ref, tri_ref,
                     xn_ref, hpk_ref, meta_ref, cnt_ref):
    @pl.when(pl.program_id(0) == 0)
    def _():
        cnt_ref[...] = jnp.zeros(cnt_ref.shape, cnt_ref.dtype)

    y = jnp.dot(o_ref[...], w_ref[...], preferred_element_type=F32)
    xn = x_ref[...] + gate_ref[0] * y
    xn_ref[...] = xn
    h = _norm_mod(xn, g_ref[...], sh_ref[0], sc_ref[0])
    logits = jnp.dot(h.astype(BF16), wr_ref[...], preferred_element_type=F32) + br_ref[...]
    lane = lax.broadcasted_iota(jnp.int32, logits.shape, 1).astype(F32)
    gates, gsel = _route(logits, lane)
    rel = pltpu.roll(gates, LANES - N_GROUPS, 1)
    for g in range(1, N_GROUPS):
        rel = rel + pltpu.roll(gates, LANES - N_GROUPS - g * EXPERTS_PER_GROUP, 1)
    sel = lane == gsel
    onehot = jnp.where(sel, 1.0, 0.0)
    prefix = jnp.dot(tri_ref[...], onehot.astype(BF16), preferred_element_type=F32)
    rank = jnp.where(sel, prefix + cnt_ref[...] - 1.0, 0.0).sum(axis=-1, keepdims=True)
    cnt_ref[...] = cnt_ref[...] + onehot.sum(axis=0, keepdims=True)
    meta = jnp.where(lane < EXPERTS_PER_GROUP, rel,
                     jnp.where(lane == META_GROUP_LANE, gsel, jnp.where(lane == META_RANK_LANE, rank, 0.0)))
    meta_ref[...] = meta
    hpk_ref[...] = jnp.concatenate([_pack_halves(h), lax.bitcast_convert_type(meta, jnp.uint32)], axis=1)


def _proj_res(o, x, w_o, g_ffn, mod, layer, row0, tokens_per_row, w_router, b_router):
    t = x.shape[0]
    tok = lambda n: pl.BlockSpec((TM, n), lambda i: (i, 0))
    ms = lambda k: _mod_spec(layer, k, row0, tokens_per_row)
    tri = jnp.asarray(np.tril(np.ones((TM, TM), np.float32)), BF16)
    return pl.pallas_call(
        _proj_res_kernel,
        grid=(t // TM,),
        in_specs=[tok(D_MODEL), tok(D_MODEL), _const_spec(w_o.shape), ms(2), _const_spec((1, D_MODEL)), ms(3), ms(4),
                  _const_spec(w_router.shape), _const_spec(b_router.shape), _const_spec((TM, TM))],
        out_specs=[tok(D_MODEL), tok(ROW_WORDS), tok(LANES), _const_spec((1, LANES))],
        out_shape=[jax.ShapeDtypeStruct((t, D_MODEL), F32), jax.ShapeDtypeStruct((t, ROW_WORDS), jnp.uint32),
                   jax.ShapeDtypeStruct((t, LANES), F32), jax.ShapeDtypeStruct((1, LANES), F32)],
        compiler_params=_cparams(("arbitrary",)),
        name="proj_res",
    )(o, x, w_o, mod, g_ffn, mod, mod, w_router, b_router, tri)


def _moe_plan(meta, cnt):
    t = meta.shape[0]
    n_tiles = t // TM + N_GROUPS
    group = meta[:, META_GROUP_LANE].astype(jnp.int32)
    rank = meta[:, META_RANK_LANE].astype(jnp.int32)
    counts = cnt[0, :N_GROUPS].astype(jnp.int32)
    tiles = (counts + TM - 1) // TM
    tile_end = jnp.cumsum(tiles)
    tile_start = tile_end - tiles
    pos = rank + TM * jnp.sum(jnp.where(group[:, None] == jnp.arange(N_GROUPS)[None], tile_start[None], 0), axis=1)
    tile_group = jnp.minimum(jnp.sum(jnp.arange(n_tiles)[:, None] >= tile_end[None], axis=1), N_GROUPS - 1)
    return pos.astype(jnp.int32), tile_group.astype(jnp.int32), tile_end[-1:].astype(jnp.int32)


def _gather_rows(srcs, idx_ref, base, dsts, n_rows):
    for r in range(n_rows):
        s = idx_ref[base + r]
        for src_ref, dst_ref in zip(srcs, dsts):
            dst_ref[r:r + 1, :] = src_ref[pl.ds(s, 1), :]


def _moe_kernel(pos_ref, tg_ref, nt_ref, hpk_ref, wgu_ref, wd_ref, ypk_ref, src_ref, hbuf, hid_ref):
    i = pl.program_id(0)
    n_tok = hpk_ref.shape[0]
    n_tiles = pl.num_programs(0)

    @pl.when(i == 0)
    def _():
        def clear(p, c):
            src_ref[p] = 0
            return c
        lax.fori_loop(0, src_ref.shape[0], clear, 0, unroll=8)

        def fill(t, c):
            src_ref[pos_ref[t]] = t
            return c
        lax.fori_loop(0, n_tok, fill, 0, unroll=8)
        _gather_rows((hpk_ref,), src_ref, 0, (hbuf.at[0],), TM)

    @pl.when(i < nt_ref[0])
    def _():
        slot = i % 2
        row = hbuf[slot]
        lo, hi = _unpack_halves(row[:, :D_MODEL // 2])
        h = jnp.concatenate([lo.astype(BF16), hi.astype(BF16)], axis=1)
        gates = lax.bitcast_convert_type(row[:, D_MODEL // 2:], F32)
        nxt = jnp.minimum(i + 1, n_tiles - 1) * TM
        _gather_rows((hpk_ref,), src_ref, nxt, (hbuf.at[1 - slot],), TM)
        f = D_FF_EXPERT
        for j in range(EXPERTS_PER_GROUP):
            gu = jnp.dot(h, wgu_ref[0, j], preferred_element_type=F32)
            hid = _silu(gu[:, :f]) * gu[:, f:] * gates[:, j:j + 1]
            hid_ref[:, j * f:(j + 1) * f] = hid.astype(BF16)
        ypk_ref[...] = _pack_halves(jnp.dot(hid_ref[...], wd_ref[0], preferred_element_type=F32))

    @pl.when(i >= nt_ref[0])
    def _():
        ypk_ref[...] = jnp.zeros(ypk_ref.shape, ypk_ref.dtype)


def _moe(hpk, pos, tile_group, n_used, wgu, wd):
    t = hpk.shape[0]
    n_tiles = t // TM + N_GROUPS
    f = D_FF_EXPERT
    grid_spec = pltpu.PrefetchScalarGridSpec(
        num_scalar_prefetch=3,
        grid=(n_tiles,),
        in_specs=[
            pl.BlockSpec(hpk.shape, lambda i, *_: (0, 0), pipeline_mode=pl.Buffered(1)),
            pl.BlockSpec((1, EXPERTS_PER_GROUP, D_MODEL, 2 * f), lambda i, pos, tg, nt: (tg[i], 0, 0, 0)),
            pl.BlockSpec((1, EXPERTS_PER_GROUP * f, D_MODEL), lambda i, pos, tg, nt: (tg[i], 0, 0)),
        ],
        out_specs=pl.BlockSpec((TM, D_MODEL // 2), lambda i, *_: (i, 0)),
        scratch_shapes=[pltpu.SMEM((n_tiles * TM,), jnp.int32),
                        pltpu.VMEM((2, TM, ROW_WORDS), jnp.uint32),
                        pltpu.VMEM((TM, EXPERTS_PER_GROUP * f), BF16)],
    )
    return pl.pallas_call(
        _moe_kernel,
        grid_spec=grid_spec,
        out_shape=jax.ShapeDtypeStruct((n_tiles * TM, D_MODEL // 2), jnp.uint32),
        compiler_params=_cparams(("arbitrary",)),
        name="moe",
    )(pos, tile_group, n_used, hpk, wgu, wd)


def _combine_kernel(*refs, final_norm):
    pos_ref, ypk_ref, x_ref, gate_ref = refs[:4]
    o_ref, ybuf = refs[-2], refs[-1]
    _gather_rows((ypk_ref,), pos_ref, pl.program_id(0) * TM, (ybuf,), TM)
    lo, hi = _unpack_halves(ybuf[...])
    out = x_ref[...] + gate_ref[0] * jnp.concatenate([lo, hi], axis=1)
    if final_norm:
        out = _rms(out, refs[4][...])
    o_ref[...] = out


def _combine(ypk, pos, x, mod, layer, row0, tokens_per_row, final_g):
    t = x.shape[0]

    def mod_map(i, pos):
        row = row0 + (i * TM) // tokens_per_row
        return ((layer * MOD_ROWS + row) * N_MOD + 5, 0, 0)

    in_specs = [pl.BlockSpec(ypk.shape, lambda i, pos: (0, 0), pipeline_mode=pl.Buffered(1)),
                pl.BlockSpec((TM, D_MODEL), lambda i, pos: (i, 0)),
                pl.BlockSpec((1, 1, D_MODEL), mod_map)]
    args = [ypk, x, mod]
    if final_g is not None:
        in_specs.append(pl.BlockSpec((1, D_MODEL), lambda i, pos: (0, 0)))
        args.append(final_g)
    grid_spec = pltpu.PrefetchScalarGridSpec(
        num_scalar_prefetch=1,
        grid=(t // TM,),
        in_specs=in_specs,
        out_specs=pl.BlockSpec((TM, D_MODEL), lambda i, pos: (i, 0)),
        scratch_shapes=[pltpu.VMEM((TM, D_MODEL // 2), jnp.uint32)],
    )
    return pl.pallas_call(
        functools.partial(_combine_kernel, final_norm=final_g is not None),
        grid_spec=grid_spec,
        out_shape=jax.ShapeDtypeStruct((t, D_MODEL), F32),
        compiler_params=_cparams(("arbitrary",)),
        name="moe_combine",
    )(pos, *args)


def _pad_heads(w, used):
    k = w.shape[0]
    w = w.reshape(k, MLA_HEADS, used)
    return jnp.pad(w, ((0, 0), (0, 0), (0, LANES - used))).reshape(k, MLA_HEADS * LANES)


def kernel(x_prompt, x_sample, state_na_k, state_na_v, state_mla_ckv, state_mla_kr, c, c_ctx, ada_w, ada_b, norm_mix_g, norm_ffn_g, final_norm_g, na_w_qkv, na_w_o, na_rpb, mla_w_in, mla_q_norm_g, mla_kv_norm_g, mla_w_uq, mla_w_uk, mla_w_uv, mla_w_o, moe_w_group, moe_b_group, moe_w_expert, moe_b_expert, moe_w_gate, moe_w_up, moe_w_down):
    n_b, seq, d = x_prompt.shape
    dec_b, n_lat, _ = x_sample.shape
    past = state_na_k.shape[2]
    rows = n_lat // GRID_W
    assert d == D_MODEL and dec_b + 1 <= MOD_ROWS and n_lat % TM == 0 and (n_b * seq) % TM == 0

    cond8 = jnp.concatenate([c_ctx[None], c, jnp.zeros((MOD_ROWS - 1 - dec_b, d), F32)], axis=0)
    mod = _ada_mod(cond8, ada_w, ada_b)

    xp = x_prompt.reshape(n_b * seq, d)
    xs = x_sample.reshape(dec_b * n_lat, d)
    p_rows = (0, n_b * seq)
    s_rows = (1, n_lat)
    outs = {}

    for i in range(DEPTH):
        j = i // 2
        g_mix = norm_mix_g[i][None]
        g_ffn = norm_ffn_g[i][None]
        if i % 2 == 0:
            w_qkv = na_w_qkv[j].astype(BF16)
            qp, kp, vp = _na_qkv(xp, g_mix, mod, i, *p_rows, w_qkv, F32)
            qs, ks, vs = _na_qkv(xs, g_mix, mod, i, *s_rows, w_qkv, BF16)
            outs['na_k'] = kp.reshape(n_b, 1, seq, NA_HEADS, NA_HEAD_DIM)
            outs['na_v'] = vp.reshape(n_b, 1, seq, NA_HEADS, NA_HEAD_DIM)
            op = _attention(qp, [(kp, vp)], n_b, seq, wide_qk=False, pairs=8)
            tables = _na_bias_tables(na_rpb[j], rows)
            os_ = _na_latent(qs, ks, vs, state_na_k[:, j].reshape(dec_b * past, d),
                             state_na_v[:, j].reshape(dec_b * past, d), tables, dec_b)
            w_o = na_w_o[j].astype(BF16)
        else:
            pad_in = MLA_IN_PAD - mla_w_in.shape[-1]
            w_mla = (jnp.pad(mla_w_in[j], ((0, 0), (0, pad_in))).astype(BF16),
                     mla_q_norm_g[j][None], mla_kv_norm_g[j][None],
                     _pad_heads(mla_w_uq[j], MLA_QK_DIM).astype(BF16),
                     _pad_heads(mla_w_uk[j], MLA_NOPE_DIM).astype(BF16),
                     mla_w_uv[j].astype(BF16))
            wuv_t = _pad_heads(mla_w_uv[j], MLA_V_DIM).T.astype(BF16)
            w_mla_t = w_mla[:5] + (wuv_t,)
            qp, kp, vp, ckv_p, kr_p = _mla_proj(xp, g_mix, mod, i, *p_rows, w_mla, None, True)
            qs, ks, vs_t = _mla_proj(xs, g_mix, mod, i, *s_rows, w_mla_t, _rope_lane_tables(n_lat), False)
            outs['ckv'] = ckv_p.reshape(n_b, 1, seq, MLA_KV_RANK)
            outs['kr'] = kr_p.reshape(n_b, 1, seq, MLA_ROPE_DIM)
            kr_state = jnp.pad(state_mla_kr[:, j].reshape(dec_b * past, MLA_ROPE_DIM),
                               ((0, 0), (0, LANES - MLA_ROPE_DIM)))
            kc, vc_t = _mla_state_kv(state_mla_ckv[:, j].reshape(dec_b * past, MLA_KV_RANK), kr_state,
                                     w_mla[4], wuv_t)
            op = _attention(qp, [(kp, vp)], n_b, seq, wide_qk=True, pairs=8)
            os_ = _attention_t(qs, ks, vs_t, kc, vc_t, dec_b, 512, heads=4)
            w_o = mla_w_o[j].astype(BF16)

        w_router = jnp.pad(jnp.concatenate([moe_w_group[i], moe_w_expert[i]], axis=1),
                           ((0, 0), (0, LANES - N_GROUPS - N_EXPERTS))).astype(BF16)
        b_router = jnp.pad(jnp.concatenate([moe_b_group[i], moe_b_expert[i]]),
                           (0, LANES - N_GROUPS - N_EXPERTS))[None]
        f = D_FF_EXPERT
        wgu = jnp.concatenate([moe_w_gate[i], moe_w_up[i]], axis=-1).astype(BF16)
        wgu = wgu.reshape(N_GROUPS, EXPERTS_PER_GROUP, d, 2 * f)
        wd = moe_w_down[i].astype(BF16).reshape(N_GROUPS, EXPERTS_PER_GROUP * f, d)
        final_g = final_norm_g[None] if i == DEPTH - 1 else None
        new_x = []
        for o_att, x, rows_ in ((op, xp, p_rows), (os_, xs, s_rows)):
            xn, hpk, meta, cnt = _proj_res(o_att, x, w_o, g_ffn, mod, i, *rows_, w_router, b_router)
            pos, tile_group, n_used = _moe_plan(meta, cnt)
            ypk = _moe(hpk, pos, tile_group, n_used, wgu, wd)
            new_x.append(_combine(ypk, pos, xn, mod, i, *rows_, final_g))
        xp, xs = new_x

    return (xp.reshape(n_b, seq, d), xs.reshape(dec_b, n_lat, d),
            outs['na_k'], outs['na_v'], outs['ckv'], outs['kr'])
```

```python
import functools

import numpy as np
import jax
import jax.numpy as jnp
from jax import lax
from jax.experimental import pallas as pl
from jax.experimental.pallas import tpu as pltpu

F32 = jnp.float32
BF16 = jnp.bfloat16

D_MODEL = 1024
DEPTH = 2
GRID_W = 64
N_MOD = 6
EPS = 1e-6
NEG_INF = -1e30
NA_HEADS = 16
NA_HEAD_DIM = 64
NA_KH = 8
NA_KW = 16
NA_GROUP_ROWS = 4
NA_WIN_ROWS = 12
MLA_HEADS = 16
MLA_Q_RANK = 384
MLA_KV_RANK = 256
MLA_NOPE_DIM = 64
MLA_ROPE_DIM = 32
MLA_V_DIM = 64
MLA_QK_DIM = MLA_NOPE_DIM + MLA_ROPE_DIM
MLA_IN_PAD = 768
ROPE_THETA = 10000.0
LOG2_E = 1.4426950408889634
N_GROUPS = 4
EXPERTS_PER_GROUP = 4
N_EXPERTS = 16
D_FF_EXPERT = 256

LANES = 128
MOD_ROWS = 8
VMEM_LIMIT = 56 * 1024 * 1024
TM = 512
TM_PROJ = 1024


def _cparams(sem):
    return pltpu.CompilerParams(dimension_semantics=sem, vmem_limit_bytes=VMEM_LIMIT)


def _const_spec(shape):
    nd = len(shape)
    return pl.BlockSpec(shape, lambda *_: (0,) * nd)


def _norm_mod(x, g, shift, scale):
    ms = jnp.mean(x * x, axis=-1, keepdims=True)
    y = x * lax.rsqrt(ms + EPS) * g
    return y * (1.0 + scale) + shift


def _rms(x, g):
    ms = jnp.mean(x * x, axis=-1, keepdims=True)
    return x * lax.rsqrt(ms + EPS) * g


def _silu(x):
    return x / (1.0 + jnp.exp(-x))


def _ada_kernel(cond_ref, w_ref, b_ref, o_ref):
    s = _silu(cond_ref[...]).astype(BF16)
    o_ref[0] = jnp.dot(s, w_ref[0].astype(BF16), preferred_element_type=F32) + b_ref[0]


def _ada_mod(cond8, ada_w, ada_b):
    tn = 1024
    n = N_MOD * D_MODEL
    out = pl.pallas_call(
        _ada_kernel,
        grid=(DEPTH, n // tn),
        in_specs=[
            _const_spec((MOD_ROWS, D_MODEL)),
            pl.BlockSpec((1, D_MODEL, tn), lambda l, j: (l, 0, j)),
            pl.BlockSpec((1, 1, tn), lambda l, j: (l, 0, j)),
        ],
        out_specs=pl.BlockSpec((1, MOD_ROWS, tn), lambda l, j: (l, 0, j)),
        out_shape=jax.ShapeDtypeStruct((DEPTH, MOD_ROWS, n), F32),
        compiler_params=_cparams(("arbitrary", "arbitrary")),
        name="ada_mod",
    )(cond8, ada_w, ada_b.reshape(DEPTH, 1, n))
    return out.reshape(DEPTH * MOD_ROWS * N_MOD, 1, D_MODEL)


def _mod_spec(layer, k, row0, tokens_per_row, tm=TM):
    def imap(i):
        row = row0 + (i * tm) // tokens_per_row
        return ((layer * MOD_ROWS + row) * N_MOD + k, 0, 0)
    return pl.BlockSpec((1, 1, D_MODEL), imap)


def _qkv_kernel(x_ref, g_ref, sh_ref, sc_ref, w_ref, q_ref, k_ref, v_ref):
    h = _norm_mod(x_ref[...], g_ref[...], sh_ref[0], sc_ref[0]).astype(BF16)
    y = jnp.dot(h, w_ref[...], preferred_element_type=F32)
    q_ref[...] = (y[:, :D_MODEL] * (NA_HEAD_DIM ** -0.5)).astype(q_ref.dtype)
    k_ref[...] = y[:, D_MODEL:2 * D_MODEL].astype(k_ref.dtype)
    v_ref[...] = y[:, 2 * D_MODEL:].astype(v_ref.dtype)


def _na_qkv(x, g, mod, layer, row0, tokens_per_row, w_qkv, kv_dtype):
    t = x.shape[0]
    tok = pl.BlockSpec((TM, D_MODEL), lambda i: (i, 0))
    return pl.pallas_call(
        _qkv_kernel,
        grid=(t // TM,),
        in_specs=[tok, _const_spec((1, D_MODEL)),
                  _mod_spec(layer, 0, row0, tokens_per_row), _mod_spec(layer, 1, row0, tokens_per_row),
                  _const_spec((D_MODEL, 3 * D_MODEL))],
        out_specs=[tok, tok, tok],
        out_shape=[jax.ShapeDtypeStruct((t, D_MODEL), BF16),
                   jax.ShapeDtypeStruct((t, D_MODEL), kv_dtype),
                   jax.ShapeDtypeStruct((t, D_MODEL), kv_dtype)],
        compiler_params=_cparams(("arbitrary",)),
        name="na_qkv",
    )(x, g, mod, mod, w_qkv)


def _softmax_pv(s_list, v_list):
    m = s_list[0].max(axis=-1, keepdims=True)
    for s in s_list[1:]:
        m = jnp.maximum(m, s.max(axis=-1, keepdims=True))
    l = None
    acc = None
    for s, v in zip(s_list, v_list):
        p = jnp.exp(s - m)
        ls = p.sum(axis=-1, keepdims=True)
        a = jnp.dot(p.astype(BF16), v, preferred_element_type=F32)
        l = ls if l is None else l + ls
        acc = a if acc is None else acc + a
    return acc * (1.0 / l)


def _qk(q, k):
    return lax.dot_general(q, k, (((1,), (1,)), ((), ())), preferred_element_type=F32)


def _head_masks():
    lane = lax.broadcasted_iota(jnp.int32, (1, LANES), 1)
    lo = lane < (LANES // 2)
    return lo, jnp.where(lo, 1.0, 0.0).astype(BF16), jnp.where(lo, 0.0, 1.0).astype(BF16)


def _attn_kernel(*refs, n_src, wide_qk, pairs):
    q_ref, o_ref = refs[0], refs[-1]
    kv = refs[1:-1]
    lo, m0, m1 = _head_masks()
    wq = 2 * LANES if wide_qk else LANES
    for p in range(pairs):
        q = q_ref[:, p * wq:(p + 1) * wq]
        ks = [kv[2 * t][:, p * wq:(p + 1) * wq].astype(BF16) for t in range(n_src)]
        vs = [kv[2 * t + 1][:, p * LANES:(p + 1) * LANES].astype(BF16) for t in range(n_src)]
        outs = []
        for j in range(2):
            if wide_qk:
                s_list = [_qk(q[:, j * LANES:(j + 1) * LANES], k[:, j * LANES:(j + 1) * LANES]) for k in ks]
            else:
                s_list = [_qk(q * (m0 if j == 0 else m1), k) for k in ks]
            outs.append(_softmax_pv(s_list, vs))
        o_ref[:, p * LANES:(p + 1) * LANES] = jnp.where(lo, outs[0], outs[1]).astype(o_ref.dtype)


def _attention(q, kv_srcs, n_batch, tq, wide_qk, pairs):
    wq = (2 * LANES if wide_qk else LANES) * pairs
    wv = LANES * pairs
    t_q = q.shape[0] // n_batch
    nq = t_q // tq
    n_steps = D_MODEL // wv
    in_specs = [pl.BlockSpec((tq, wq), lambda b, p, i: (b * nq + i, p))]
    args = [q]
    for k, v in kv_srcs:
        t_k = k.shape[0] // n_batch
        in_specs.append(pl.BlockSpec((t_k, wq), lambda b, p, i: (b, p)))
        in_specs.append(pl.BlockSpec((t_k, wv), lambda b, p, i: (b, p)))
        args += [k, v]
    return pl.pallas_call(
        functools.partial(_attn_kernel, n_src=len(kv_srcs), wide_qk=wide_qk, pairs=pairs),
        grid=(n_batch, n_steps, nq),
        in_specs=in_specs,
        out_specs=pl.BlockSpec((tq, wv), lambda b, p, i: (b * nq + i, p)),
        out_shape=jax.ShapeDtypeStruct((q.shape[0], D_MODEL), BF16),
        compiler_params=_cparams(("arbitrary", "arbitrary", "arbitrary")),
        name="attention",
    )(*args)


def _col_max(s, chunk=64):
    acc = s[:chunk]
    for r in range(chunk, s.shape[0], chunk):
        acc = jnp.maximum(acc, s[r:r + chunk])
    return acc.max(axis=0, keepdims=True)


def _attn_t_kernel(q_ref, k1_ref, k2_ref, v1_ref, v2_ref, o_ref):
    n_heads = q_ref.shape[1] // LANES
    slots = [slice(j * LANES, (j + 1) * LANES) for j in range(n_heads)]
    scores = [(_qk(k1_ref[:, sl], q_ref[:, sl]), _qk(k2_ref[:, sl], q_ref[:, sl])) for sl in slots]
    outs = []
    for sl, (s1, s2) in zip(slots, scores):
        m = jnp.maximum(_col_max(s1), _col_max(s2))
        p1 = jnp.exp2(s1 - m).astype(BF16)
        p2 = jnp.exp2(s2 - m).astype(BF16)
        ot = (jnp.dot(v1_ref[sl, :], p1, preferred_element_type=F32)
              + jnp.dot(v2_ref[sl, :], p2, preferred_element_type=F32))
        outs.append(ot[:MLA_V_DIM] * (1.0 / ot[MLA_V_DIM:MLA_V_DIM + 1]))
    for j in range(0, n_heads, 2):
        pair = jnp.concatenate(outs[j:j + 2], axis=0).T
        o_ref[:, (j // 2) * LANES:(j // 2 + 1) * LANES] = pair.astype(o_ref.dtype)


def _attention_t(q, k1, v1t, k2, v2t, n_batch, tq, heads):
    wq = heads * LANES
    wo = heads * MLA_V_DIM
    nq = q.shape[0] // n_batch // tq
    t1 = k1.shape[0] // n_batch
    t2 = k2.shape[0] // n_batch
    return pl.pallas_call(
        _attn_t_kernel,
        grid=(n_batch, MLA_HEADS // heads, nq),
        in_specs=[pl.BlockSpec((tq, wq), lambda b, p, i: (b * nq + i, p)),
                  pl.BlockSpec((t1, wq), lambda b, p, i: (b, p)),
                  pl.BlockSpec((t2, wq), lambda b, p, i: (b, p)),
                  pl.BlockSpec((wq, t1), lambda b, p, i: (p, b)),
                  pl.BlockSpec((wq, t2), lambda b, p, i: (p, b))],
        out_specs=pl.BlockSpec((tq, wo), lambda b, p, i: (b * nq + i, p)),
        out_shape=jax.ShapeDtypeStruct((q.shape[0], D_MODEL), BF16),
        compiler_params=_cparams(("arbitrary", "arbitrary", "arbitrary")),
        name="attention_t",
    )(q, k1, k2, v1t, v2t)


def _na_group_layout(rows):
    n_groups = rows // NA_GROUP_ROWS
    kh = min(NA_KH, rows)
    bases, types, keys = [], [], {}
    for gi in range(n_groups):
        r0 = gi * NA_GROUP_ROWS
        base = int(np.clip(r0 - kh // 2, 0, rows - NA_WIN_ROWS))
        starts = tuple(int(np.clip(r - kh // 2, 0, rows - kh)) - base for r in range(r0, r0 + NA_GROUP_ROWS))
        key = (starts, r0 - base)
        if key not in keys:
            keys[key] = len(keys)
        bases.append(base)
        types.append(keys[key])
    return bases, types, list(keys.keys())


def _na_bias_tables(rpb, rows):
    kh = min(NA_KH, rows)
    _, _, type_keys = _na_group_layout(rows)
    qc = np.arange(GRID_W)
    kc = np.arange(GRID_W)
    col_start = np.clip(qc - NA_KW // 2, 0, GRID_W - NA_KW)
    col_valid = (kc[None, :] >= col_start[:, None]) & (kc[None, :] < col_start[:, None] + NA_KW)
    dx = kc[None, :] - qc[:, None] + (NA_KW - 1)
    onehot = ((dx[None] == np.arange(2 * NA_KW - 1)[:, None, None]) & col_valid[None]).astype(np.float32)
    band = jnp.einsum('hyd,dqk->hyqk', rpb.astype(F32), onehot, precision=lax.Precision.HIGHEST)
    band = jnp.where(col_valid, band, NEG_INF)
    strip = band.transpose(0, 2, 1, 3).reshape(NA_HEADS, GRID_W, (2 * NA_KH - 1) * GRID_W)
    tables = []
    for starts, r_off in type_keys:
        q_rows = []
        for rq in range(NA_GROUP_ROWS):
            dy0 = starts[rq] - (rq + r_off) + (NA_KH - 1)
            piece = strip[:, :, dy0 * GRID_W:(dy0 + kh) * GRID_W]
            pad = ((0, 0), (0, 0), (starts[rq] * GRID_W, (NA_WIN_ROWS - kh - starts[rq]) * GRID_W))
            q_rows.append(jnp.pad(piece, pad, constant_values=NEG_INF))
        tables.append(jnp.concatenate(q_rows, axis=1))
    return jnp.stack(tables, axis=1)


def _heads_to_lanes_kernel(x_ref, o_ref):
    for h in range(NA_HEADS):
        o_ref[:, h * NA_HEAD_DIM:(h + 1) * NA_HEAD_DIM] = x_ref[0, 0, :, h, :].astype(o_ref.dtype)


def _heads_to_lanes(x, layer):
    b, _, t, h, dh = x.shape
    return pl.pallas_call(
        _heads_to_lanes_kernel,
        grid=(b,),
        in_specs=[pl.BlockSpec((1, 1, t, h, dh), lambda i: (i, layer, 0, 0, 0))],
        out_specs=pl.BlockSpec((t, h * dh), lambda i: (i, 0)),
        out_shape=jax.ShapeDtypeStruct((b * t, h * dh), BF16),
        compiler_params=_cparams(("arbitrary",)),
        name="heads_to_lanes",
    )(x)


def _na_latent_kernel(q_ref, k_ref, v_ref, kc_ref, vc_ref, tab_ref, o_ref, *, bases, types):
    lo, m0, m1 = _head_masks()
    kctx = kc_ref[...].astype(BF16)
    vctx = vc_ref[...].astype(BF16)
    gq = NA_GROUP_ROWS * GRID_W
    gk = NA_WIN_ROWS * GRID_W
    for gi, (base, ty) in enumerate(zip(bases, types)):
        q = q_ref[gi * gq:(gi + 1) * gq, :]
        kwin = k_ref[base * GRID_W:base * GRID_W + gk, :]
        vwin = v_ref[base * GRID_W:base * GRID_W + gk, :]
        outs = []
        for j in range(2):
            qj = q * (m0 if j == 0 else m1)
            s_win = _qk(qj, kwin) + tab_ref[j, ty]
            s_ctx = _qk(qj, kctx)
            outs.append(_softmax_pv([s_win, s_ctx], [vwin, vctx]))
        o_ref[gi * gq:(gi + 1) * gq, :] = jnp.where(lo, outs[0], outs[1]).astype(o_ref.dtype)


def _na_latent(q, k, v, k_ctx, v_ctx, tables, n_batch):
    t = q.shape[0] // n_batch
    rows = t // GRID_W
    t_ctx = k_ctx.shape[0] // n_batch
    bases, types, type_keys = _na_group_layout(rows)
    n_types = len(type_keys)
    gq = NA_GROUP_ROWS * GRID_W
    gk = NA_WIN_ROWS * GRID_W
    tok = pl.BlockSpec((t, LANES), lambda p, b: (b, p))
    ctx = pl.BlockSpec((t_ctx, LANES), lambda p, b: (b, p))
    return pl.pallas_call(
        functools.partial(_na_latent_kernel, bases=bases, types=types),
        grid=(D_MODEL // LANES, n_batch),
        in_specs=[tok, tok, tok, ctx, ctx,
                  pl.BlockSpec((2, n_types, gq, gk), lambda p, b: (p, 0, 0, 0))],
        out_specs=tok,
        out_shape=jax.ShapeDtypeStruct(q.shape, BF16),
        compiler_params=_cparams(("arbitrary", "arbitrary")),
        name="na_latent",
    )(q, k, v, k_ctx, v_ctx, tables)


def _rope_lane_tables(n_tokens):
    n_freq = MLA_ROPE_DIM // 4
    inv = 1.0 / (ROPE_THETA ** (jnp.arange(n_freq, dtype=F32) / n_freq))
    tt = jnp.arange(n_tokens)
    row = (tt // GRID_W).astype(F32)
    col = (tt % GRID_W).astype(F32)
    ang = jnp.concatenate([row[:, None] * inv, col[:, None] * inv], axis=-1)
    cos, sin = jnp.cos(ang), jnp.sin(ang)
    ones = jnp.ones((n_tokens, MLA_NOPE_DIM), F32)
    z16 = jnp.zeros((n_tokens, MLA_ROPE_DIM // 2), F32)
    z32 = jnp.zeros((n_tokens, LANES - MLA_QK_DIM), F32)
    z64 = jnp.zeros((n_tokens, MLA_NOPE_DIM), F32)
    c_tab = jnp.concatenate([ones, cos, cos, z32], axis=-1)
    s_dn = jnp.concatenate([z64, -sin, z16, z32], axis=-1)
    s_up = jnp.concatenate([z64, z16, sin, z32], axis=-1)
    return c_tab, s_dn, s_up


def _rope_slot(x, c_tab, s_dn, s_up):
    half = MLA_ROPE_DIM // 2
    return x * c_tab + pltpu.roll(x, LANES - half, 1) * s_dn + pltpu.roll(x, half, 1) * s_up


def _mla_kv(ckv_n, kr_slot, wuk_ref, wuv_ref, k_ref, v_ref, v_transposed):
    c = ckv_n.astype(BF16)
    k_nope = jnp.dot(c, wuk_ref[...], preferred_element_type=F32)
    for h in range(MLA_HEADS):
        k_ref[:, h * LANES:(h + 1) * LANES] = (k_nope[:, h * LANES:(h + 1) * LANES] + kr_slot).astype(k_ref.dtype)
    if v_transposed:
        vt = _qk(wuv_ref[...], c)
        row = lax.broadcasted_iota(jnp.int32, vt.shape, 0)
        v_ref[...] = jnp.where((row & (LANES - 1)) == MLA_V_DIM, 1.0, vt).astype(v_ref.dtype)
    else:
        v_ref[...] = jnp.dot(c, wuv_ref[...], preferred_element_type=F32).astype(v_ref.dtype)


def _mla_proj_kernel(*refs, rope, emit_state, q_scale):
    (x_ref, g_ref, sh_ref, sc_ref, win_ref, qg_ref, kvg_ref, wuq_ref, wuk_ref, wuv_ref) = refs[:10]
    pos = 10
    if rope:
        ct_ref, sd_ref, su_ref = refs[pos:pos + 3]
        pos += 3
    q_ref, k_ref, v_ref = refs[pos:pos + 3]
    pos += 3
    h = _norm_mod(x_ref[...], g_ref[...], sh_ref[0], sc_ref[0]).astype(BF16)
    c = jnp.dot(h, win_ref[...], preferred_element_type=F32)
    cq = _rms(c[:, :MLA_Q_RANK], qg_ref[...]).astype(BF16)
    ckv = _rms(c[:, MLA_Q_RANK:MLA_Q_RANK + MLA_KV_RANK], kvg_ref[...])
    kr_chunk = c[:, MLA_Q_RANK + MLA_KV_RANK:]
    if emit_state:
        ckv_out, kr_out = refs[pos:pos + 2]
        ckv_out[...] = ckv
        kr_out[...] = kr_chunk[:, :MLA_ROPE_DIM]
    kr_slot = pltpu.roll(kr_chunk, MLA_NOPE_DIM, 1)
    q = jnp.dot(cq, wuq_ref[...], preferred_element_type=F32) * q_scale
    if rope:
        ct, sd, su = ct_ref[...], sd_ref[...], su_ref[...]
        kr_slot = _rope_slot(kr_slot, ct, sd, su)
    for hh in range(MLA_HEADS):
        qh = q[:, hh * LANES:(hh + 1) * LANES]
        if rope:
            qh = _rope_slot(qh, ct, sd, su)
        q_ref[:, hh * LANES:(hh + 1) * LANES] = qh.astype(q_ref.dtype)
    _mla_kv(ckv, kr_slot, wuk_ref, wuv_ref, k_ref, v_ref, v_transposed=rope)


def _mla_proj(x, g, mod, layer, row0, tokens_per_row, w, rope_tabs, emit_state):
    t = x.shape[0]
    win, qg, kvg, wuq, wuk, wuv = w
    tok = lambda n: pl.BlockSpec((TM, n), lambda i: (i, 0))
    wq = MLA_HEADS * LANES
    in_specs = [tok(D_MODEL), _const_spec((1, D_MODEL)),
                _mod_spec(layer, 0, row0, tokens_per_row), _mod_spec(layer, 1, row0, tokens_per_row),
                _const_spec(win.shape), _const_spec(qg.shape), _const_spec(kvg.shape),
                _const_spec(wuq.shape), _const_spec(wuk.shape), _const_spec(wuv.shape)]
    args = [x, g, mod, mod, win, qg, kvg, wuq, wuk, wuv]
    rope = rope_tabs is not None
    if rope:
        n_tab = rope_tabs[0].shape[0] // TM
        in_specs += [pl.BlockSpec((TM, LANES), lambda i: (i % n_tab, 0)) for _ in range(3)]
        args += list(rope_tabs)
    if rope:
        v_spec, v_shape = pl.BlockSpec((wq, TM), lambda i: (0, i)), jax.ShapeDtypeStruct((wq, t), BF16)
        q_scale = MLA_QK_DIM ** -0.5 * LOG2_E
    else:
        v_spec, v_shape = tok(D_MODEL), jax.ShapeDtypeStruct((t, D_MODEL), BF16)
        q_scale = MLA_QK_DIM ** -0.5
    out_specs = [tok(wq), tok(wq), v_spec]
    out_shape = [jax.ShapeDtypeStruct((t, wq), BF16), jax.ShapeDtypeStruct((t, wq), BF16), v_shape]
    if emit_state:
        out_specs += [tok(MLA_KV_RANK), tok(MLA_ROPE_DIM)]
        out_shape += [jax.ShapeDtypeStruct((t, MLA_KV_RANK), F32), jax.ShapeDtypeStruct((t, MLA_ROPE_DIM), F32)]
    return pl.pallas_call(
        functools.partial(_mla_proj_kernel, rope=rope, emit_state=emit_state, q_scale=q_scale),
        grid=(t // TM,),
        in_specs=in_specs, out_specs=out_specs, out_shape=out_shape,
        compiler_params=_cparams(("arbitrary",)),
        name="mla_proj",
    )(*args)


def _mla_state_kv_kernel(ckv_ref, kr_ref, wuk_ref, wuv_ref, k_ref, v_ref):
    kr_slot = pltpu.roll(kr_ref[...], MLA_NOPE_DIM, 1)
    _mla_kv(ckv_ref[...], kr_slot, wuk_ref, wuv_ref, k_ref, v_ref, v_transposed=True)


def _mla_state_kv(ckv, kr_pad, wuk, wuv_t):
    t = ckv.shape[0]
    wq = MLA_HEADS * LANES
    tok = lambda n: pl.BlockSpec((TM, n), lambda i: (i, 0))
    return pl.pallas_call(
        _mla_state_kv_kernel,
        grid=(t // TM,),
        in_specs=[tok(MLA_KV_RANK), tok(LANES), _const_spec(wuk.shape), _const_spec(wuv_t.shape)],
        out_specs=[tok(wq), pl.BlockSpec((wq, TM), lambda i: (0, i))],
        out_shape=[jax.ShapeDtypeStruct((t, wq), BF16), jax.ShapeDtypeStruct((wq, t), BF16)],
        compiler_params=_cparams(("arbitrary",)),
        name="mla_state_kv",
    )(ckv, kr_pad, wuk, wuv_t)


def _lane_argmax(v, lane):
    top = v.max(axis=-1, keepdims=True)
    idx = jnp.where(v == top, lane, float(LANES)).min(axis=-1, keepdims=True)
    return top, idx


def _route(logits, lane):
    is_g = lane < N_GROUPS
    gl = jnp.where(is_g, logits, -jnp.inf)
    gmax, gsel = _lane_argmax(gl, lane)
    p_group = 1.0 / jnp.where(is_g, jnp.exp(logits - gmax), 0.0).sum(axis=-1, keepdims=True)
    first = N_GROUPS + gsel * EXPERTS_PER_GROUP
    el = jnp.where((lane >= first) & (lane < first + EXPERTS_PER_GROUP), logits, -jnp.inf)
    t1, i1 = _lane_argmax(el, lane)
    t2, i2 = _lane_argmax(jnp.where(lane == i1, -jnp.inf, el), lane)
    e2 = jnp.exp(t2 - t1)
    w1 = 1.0 / (1.0 + e2)
    gates = jnp.where(lane == i1, w1 * p_group, jnp.where(lane == i2, e2 * w1 * p_group, 0.0))
    return gates, gsel


def _pack_halves(x):
    half = x.shape[1] // 2
    lo = lax.bitcast_convert_type(x[:, :half].astype(BF16).astype(F32), jnp.uint32)
    hi = lax.bitcast_convert_type(x[:, half:].astype(BF16).astype(F32), jnp.uint32)
    return (lo >> 16) | hi


def _unpack_halves(p):
    lo = lax.bitcast_convert_type(p << 16, F32)
    hi = lax.bitcast_convert_type(p & jnp.uint32(0xFFFF0000), F32)
    return lo, hi


ROW_WORDS = D_MODEL // 2 + LANES
META_GROUP_LANE = EXPERTS_PER_GROUP
META_RANK_LANE = EXPERTS_PER_GROUP + 1


def _proj_res_kernel(o_ref, x_ref, w_ref, gate_ref, g_ref, sh_ref, sc_ref, wr_ref, br_ref, tri_ref,
                     xn_ref, hpk_ref, meta_ref, cnt_ref):
    @pl.when(pl.program_id(0) == 0)
    def _():
        cnt_ref[...] = jnp.zeros(cnt_ref.shape, cnt_ref.dtype)

    y = jnp.dot(o_ref[...], w_ref[...], preferred_element_type=F32)
    xn = x_ref[...] + gate_ref[0] * y
    xn_ref[...] = xn
    h = _norm_mod(xn, g_ref[...], sh_ref[0], sc_ref[0])
    logits = jnp.dot(h.astype(BF16), wr_ref[...], preferred_element_type=F32) + br_ref[...]
    lane = lax.broadcasted_iota(jnp.int32, logits.shape, 1).astype(F32)
    gates, gsel = _route(logits, lane)
    rel = pltpu.roll(gates, LANES - N_GROUPS, 1)
    for g in range(1, N_GROUPS):
        rel = rel + pltpu.roll(gates, LANES - N_GROUPS - g * EXPERTS_PER_GROUP, 1)
    sel = lane == gsel
    onehot = jnp.where(sel, 1.0, 0.0)
    prefix = jnp.dot(tri_ref[...], onehot.astype(BF16), preferred_element_type=F32)
    rank = jnp.where(sel, prefix + cnt_ref[...] - 1.0, 0.0).sum(axis=-1, keepdims=True)
    cnt_ref[...] = cnt_ref[...] + onehot.sum(axis=0, keepdims=True)
    meta = jnp.where(lane < EXPERTS_PER_GROUP, rel,
                     jnp.where(lane == META_GROUP_LANE, gsel, jnp.where(lane == META_RANK_LANE, rank, 0.0)))
    meta_ref[...] = meta
    hpk_ref[...] = jnp.concatenate([_pack_halves(h), lax.bitcast_convert_type(meta, jnp.uint32)], axis=1)


def _proj_res(o, x, w_o, g_ffn, mod, layer, row0, tokens_per_row, w_router, b_router):
    t = x.shape[0]
    tm = TM_PROJ
    tok = lambda n: pl.BlockSpec((tm, n), lambda i: (i, 0))
    ms = lambda k: _mod_spec(layer, k, row0, tokens_per_row, tm)
    tri = jnp.asarray(np.tril(np.ones((tm, tm), np.float32)), BF16)
    return pl.pallas_call(
        _proj_res_kernel,
        grid=(t // tm,),
        in_specs=[tok(D_MODEL), tok(D_MODEL), _const_spec(w_o.shape), ms(2), _const_spec((1, D_MODEL)), ms(3), ms(4),
                  _const_spec(w_router.shape), _const_spec(b_router.shape), _const_spec((tm, tm))],
        out_specs=[tok(D_MODEL), tok(ROW_WORDS), tok(LANES), _const_spec((1, LANES))],
        out_shape=[jax.ShapeDtypeStruct((t, D_MODEL), F32), jax.ShapeDtypeStruct((t, ROW_WORDS), jnp.uint32),
                   jax.ShapeDtypeStruct((t, LANES), F32), jax.ShapeDtypeStruct((1, LANES), F32)],
        compiler_params=_cparams(("arbitrary",)),
        name="proj_res",
    )(o, x, w_o, mod, g_ffn, mod, mod, w_router, b_router, tri)


def _moe_plan(meta, cnt):
    t = meta.shape[0]
    n_tiles = t // TM + N_GROUPS
    group = meta[:, META_GROUP_LANE].astype(jnp.int32)
    rank = meta[:, META_RANK_LANE].astype(jnp.int32)
    counts = cnt[0, :N_GROUPS].astype(jnp.int32)
    tiles = (counts + TM - 1) // TM
    tile_end = jnp.cumsum(tiles)
    tile_start = tile_end - tiles
    pos = rank + TM * jnp.sum(jnp.where(group[:, None] == jnp.arange(N_GROUPS)[None], tile_start[None], 0), axis=1)
    tile_group = jnp.minimum(jnp.sum(jnp.arange(n_tiles)[:, None] >= tile_end[None], axis=1), N_GROUPS - 1)
    pad_bounds = jnp.stack([tile_start * TM + counts, tile_end * TM], axis=1).reshape(-1)
    return (pos.astype(jnp.int32), tile_group.astype(jnp.int32), tile_end[-1:].astype(jnp.int32),
            pad_bounds.astype(jnp.int32))


def _gather_rows(srcs, idx_ref, base, dsts, n_rows):
    for r in range(n_rows):
        s = idx_ref[base + r]
        for src_ref, dst_ref in zip(srcs, dsts):
            dst_ref[r:r + 1, :] = src_ref[pl.ds(s, 1), :]


def _moe_kernel(pos_ref, tg_ref, nt_ref, pad_ref, hpk_ref, wgu_ref, wd_ref, ypk_ref, src_ref, hbuf, hid_ref):
    i = pl.program_id(0)
    n_tok = hpk_ref.shape[0]

    @pl.when(i == 0)
    def _():
        def clear(p, c):
            src_ref[p] = 0
            return c
        for g in range(N_GROUPS):
            lax.fori_loop(pad_ref[2 * g], pad_ref[2 * g + 1], clear, 0)

        def fill(t, c):
            src_ref[pos_ref[t]] = t
            return c
        lax.fori_loop(0, n_tok, fill, 0, unroll=8)
        _gather_rows((hpk_ref,), src_ref, 0, (hbuf.at[0],), TM)

    @pl.when(i < nt_ref[0])
    def _():
        slot = i % 2
        row = hbuf[slot]
        lo, hi = _unpack_halves(row[:, :D_MODEL // 2])
        h = jnp.concatenate([lo.astype(BF16), hi.astype(BF16)], axis=1)
        gates = lax.bitcast_convert_type(row[:, D_MODEL // 2:], F32)
        nxt = jnp.minimum(i + 1, nt_ref[0] - 1) * TM
        _gather_rows((hpk_ref,), src_ref, nxt, (hbuf.at[1 - slot],), TM)
        f = D_FF_EXPERT
        for j in range(EXPERTS_PER_GROUP):
            gu = jnp.dot(h, wgu_ref[0, j], preferred_element_type=F32)
            hid = _silu(gu[:, :f]) * gu[:, f:] * gates[:, j:j + 1]
            hid_ref[:, j * f:(j + 1) * f] = hid.astype(BF16)
        ypk_ref[...] = _pack_halves(jnp.dot(hid_ref[...], wd_ref[0], preferred_element_type=F32))

    @pl.when(i >= nt_ref[0])
    def _():
        ypk_ref[...] = jnp.zeros(ypk_ref.shape, ypk_ref.dtype)


def _moe(hpk, pos, tile_group, n_used, pad_bounds, wgu, wd):
    t = hpk.shape[0]
    n_tiles = t // TM + N_GROUPS
    f = D_FF_EXPERT
    grid_spec = pltpu.PrefetchScalarGridSpec(
        num_scalar_prefetch=4,
        grid=(n_tiles,),
        in_specs=[
            pl.BlockSpec(hpk.shape, lambda i, *_: (0, 0), pipeline_mode=pl.Buffered(1)),
            pl.BlockSpec((1, EXPERTS_PER_GROUP, D_MODEL, 2 * f), lambda i, pos, tg, nt, pad: (tg[i], 0, 0, 0)),
            pl.BlockSpec((1, EXPERTS_PER_GROUP * f, D_MODEL), lambda i, pos, tg, nt, pad: (tg[i], 0, 0)),
        ],
        out_specs=pl.BlockSpec((TM, D_MODEL // 2), lambda i, *_: (i, 0)),
        scratch_shapes=[pltpu.SMEM((n_tiles * TM,), jnp.int32),
                        pltpu.VMEM((2, TM, ROW_WORDS), jnp.uint32),
                        pltpu.VMEM((TM, EXPERTS_PER_GROUP * f), BF16)],
    )
    return pl.pallas_call(
        _moe_kernel,
        grid_spec=grid_spec,
        out_shape=jax.ShapeDtypeStruct((n_tiles * TM, D_MODEL // 2), jnp.uint32),
        compiler_params=_cparams(("arbitrary",)),
        name="moe",
    )(pos, tile_group, n_used, pad_bounds, hpk, wgu, wd)


def _combine_kernel(*refs, final_norm):
    pos_ref, ypk_ref, x_ref, gate_ref = refs[:4]
    o_ref, ybuf = refs[-2], refs[-1]
    _gather_rows((ypk_ref,), pos_ref, pl.program_id(0) * TM, (ybuf,), TM)
    lo, hi = _unpack_halves(ybuf[...])
    out = x_ref[...] + gate_ref[0] * jnp.concatenate([lo, hi], axis=1)
    if final_norm:
        out = _rms(out, refs[4][...])
    o_ref[...] = out


def _combine(ypk, pos, x, mod, layer, row0, tokens_per_row, final_g):
    t = x.shape[0]

    def mod_map(i, pos):
        row = row0 + (i * TM) // tokens_per_row
        return ((layer * MOD_ROWS + row) * N_MOD + 5, 0, 0)

    in_specs = [pl.BlockSpec(ypk.shape, lambda i, pos: (0, 0), pipeline_mode=pl.Buffered(1)),
                pl.BlockSpec((TM, D_MODEL), lambda i, pos: (i, 0)),
                pl.BlockSpec((1, 1, D_MODEL), mod_map)]
    args = [ypk, x, mod]
    if final_g is not None:
        in_specs.append(pl.BlockSpec((1, D_MODEL), lambda i, pos: (0, 0)))
        args.append(final_g)
    grid_spec = pltpu.PrefetchScalarGridSpec(
        num_scalar_prefetch=1,
        grid=(t // TM,),
        in_specs=in_specs,
        out_specs=pl.BlockSpec((TM, D_MODEL), lambda i, pos: (i, 0)),
        scratch_shapes=[pltpu.VMEM((TM, D_MODEL // 2), jnp.uint32)],
    )
    return pl.pallas_call(
        functools.partial(_combine_kernel, final_norm=final_g is not None),
        grid_spec=grid_spec,
        out_shape=jax.ShapeDtypeStruct((t, D_MODEL), F32),
        compiler_params=_cparams(("arbitrary",)),
        name="moe_combine",
    )(pos, *args)


def _pad_heads(w, used):
    k = w.shape[0]
    w = w.reshape(k, MLA_HEADS, used)
    return jnp.pad(w, ((0, 0), (0, 0), (0, LANES - used))).reshape(k, MLA_HEADS * LANES)


def kernel(x_prompt, x_sample, state_na_k, state_na_v, state_mla_ckv, state_mla_kr, c, c_ctx, ada_w, ada_b, norm_mix_g, norm_ffn_g, final_norm_g, na_w_qkv, na_w_o, na_rpb, mla_w_in, mla_q_norm_g, mla_kv_norm_g, mla_w_uq, mla_w_uk, mla_w_uv, mla_w_o, moe_w_group, moe_b_group, moe_w_expert, moe_b_expert, moe_w_gate, moe_w_up, moe_w_down):
    n_b, seq, d = x_prompt.shape
    dec_b, n_lat, _ = x_sample.shape
    past = state_na_k.shape[2]
    rows = n_lat // GRID_W
    assert d == D_MODEL and dec_b + 1 <= MOD_ROWS and n_lat % TM_PROJ == 0 and (n_b * seq) % TM_PROJ == 0

    cond8 = jnp.concatenate([c_ctx[None], c, jnp.zeros((MOD_ROWS - 1 - dec_b, d), F32)], axis=0)
    mod = _ada_mod(cond8, ada_w, ada_b)

    xp = x_prompt.reshape(n_b * seq, d)
    xs = x_sample.reshape(dec_b * n_lat, d)
    p_rows = (0, n_b * seq)
    s_rows = (1, n_lat)
    outs = {}

    for i in range(DEPTH):
        j = i // 2
        g_mix = norm_mix_g[i][None]
        g_ffn = norm_ffn_g[i][None]
        if i % 2 == 0:
            w_qkv = na_w_qkv[j].astype(BF16)
            qp, kp, vp = _na_qkv(xp, g_mix, mod, i, *p_rows, w_qkv, F32)
            qs, ks, vs = _na_qkv(xs, g_mix, mod, i, *s_rows, w_qkv, BF16)
            outs['na_k'] = kp.reshape(n_b, 1, seq, NA_HEADS, NA_HEAD_DIM)
            outs['na_v'] = vp.reshape(n_b, 1, seq, NA_HEADS, NA_HEAD_DIM)
            op = _attention(qp, [(kp, vp)], n_b, seq, wide_qk=False, pairs=8)
            tables = _na_bias_tables(na_rpb[j], rows)
            os_ = _na_latent(qs, ks, vs, _heads_to_lanes(state_na_k, j), _heads_to_lanes(state_na_v, j),
                             tables, dec_b)
            w_o = na_w_o[j].astype(BF16)
        else:
            pad_in = MLA_IN_PAD - mla_w_in.shape[-1]
            w_mla = (jnp.pad(mla_w_in[j], ((0, 0), (0, pad_in))).astype(BF16),
                     mla_q_norm_g[j][None], mla_kv_norm_g[j][None],
                     _pad_heads(mla_w_uq[j], MLA_QK_DIM).astype(BF16),
                     _pad_heads(mla_w_uk[j], MLA_NOPE_DIM).astype(BF16),
                     mla_w_uv[j].astype(BF16))
            wuv_t = _pad_heads(mla_w_uv[j], MLA_V_DIM).T.astype(BF16)
            w_mla_t = w_mla[:5] + (wuv_t,)
            qp, kp, vp, ckv_p, kr_p = _mla_proj(xp, g_mix, mod, i, *p_rows, w_mla, None, True)
            qs, ks, vs_t = _mla_proj(xs, g_mix, mod, i, *s_rows, w_mla_t, _rope_lane_tables(n_lat), False)
            outs['ckv'] = ckv_p.reshape(n_b, 1, seq, MLA_KV_RANK)
            outs['kr'] = kr_p.reshape(n_b, 1, seq, MLA_ROPE_DIM)
            kr_state = jnp.pad(state_mla_kr[:, j].reshape(dec_b * past, MLA_ROPE_DIM),
                               ((0, 0), (0, LANES - MLA_ROPE_DIM)))
            kc, vc_t = _mla_state_kv(state_mla_ckv[:, j].reshape(dec_b * past, MLA_KV_RANK), kr_state,
                                     w_mla[4], wuv_t)
            op = _attention(qp, [(kp, vp)], n_b, seq, wide_qk=True, pairs=8)
            os_ = _attention_t(qs, ks, vs_t, kc, vc_t, dec_b, 512, heads=4)
            w_o = mla_w_o[j].astype(BF16)

        w_router = jnp.pad(jnp.concatenate([moe_w_group[i], moe_w_expert[i]], axis=1),
                           ((0, 0), (0, LANES - N_GROUPS - N_EXPERTS))).astype(BF16)
        b_router = jnp.pad(jnp.concatenate([moe_b_group[i], moe_b_expert[i]]),
                           (0, LANES - N_GROUPS - N_EXPERTS))[None]
        f = D_FF_EXPERT
        wgu = jnp.concatenate([moe_w_gate[i], moe_w_up[i]], axis=-1).astype(BF16)
        wgu = wgu.reshape(N_GROUPS, EXPERTS_PER_GROUP, d, 2 * f)
        wd = moe_w_down[i].astype(BF16).reshape(N_GROUPS, EXPERTS_PER_GROUP * f, d)
        final_g = final_norm_g[None] if i == DEPTH - 1 else None
        new_x = []
        for o_att, x, rows_ in ((op, xp, p_rows), (os_, xs, s_rows)):
            xn, hpk, meta, cnt = _proj_res(o_att, x, w_o, g_ffn, mod, i, *rows_, w_router, b_router)
            pos, tile_group, n_used, pad_bounds = _moe_plan(meta, cnt)
            ypk = _moe(hpk, pos, tile_group, n_used, pad_bounds, wgu, wd)
            new_x.append(_combine(ypk, pos, xn, mod, i, *rows_, final_g))
        xp, xs = new_x

    return (xp.reshape(n_b, seq, d), xs.reshape(dec_b, n_lat, d),
            outs['na_k'], outs['na_v'], outs['ckv'], outs['kr'])
```

```python
import functools

import numpy as np
import jax
import jax.numpy as jnp
from jax import lax
from jax.experimental import pallas as pl
from jax.experimental.pallas import tpu as pltpu

F32 = jnp.float32
BF16 = jnp.bfloat16

D_MODEL = 1024
DEPTH = 2
GRID_W = 64
N_MOD = 6
EPS = 1e-6
NEG_INF = -1e30
NA_HEADS = 16
NA_HEAD_DIM = 64
NA_KH = 8
NA_KW = 16
NA_GROUP_ROWS = 4
NA_WIN_ROWS = 12
MLA_HEADS = 16
MLA_Q_RANK = 384
MLA_KV_RANK = 256
MLA_NOPE_DIM = 64
MLA_ROPE_DIM = 32
MLA_V_DIM = 64
MLA_QK_DIM = MLA_NOPE_DIM + MLA_ROPE_DIM
MLA_IN_PAD = 768
ROPE_THETA = 10000.0
LOG2_E = 1.4426950408889634
N_GROUPS = 4
EXPERTS_PER_GROUP = 4
N_EXPERTS = 16
D_FF_EXPERT = 256

LANES = 128
MOD_ROWS = 8
VMEM_LIMIT = 56 * 1024 * 1024
TM = 512
TM_PROJ = 1024


def _cparams(sem):
    return pltpu.CompilerParams(dimension_semantics=sem, vmem_limit_bytes=VMEM_LIMIT)


def _const_spec(shape):
    nd = len(shape)
    return pl.BlockSpec(shape, lambda *_: (0,) * nd)


def _norm_mod(x, g, shift, scale):
    ms = jnp.mean(x * x, axis=-1, keepdims=True)
    y = x * lax.rsqrt(ms + EPS) * g
    return y * (1.0 + scale) + shift


def _rms(x, g):
    ms = jnp.mean(x * x, axis=-1, keepdims=True)
    return x * lax.rsqrt(ms + EPS) * g


def _silu(x):
    return x / (1.0 + jnp.exp(-x))


def _ada_kernel(cond_ref, w_ref, b_ref, o_ref):
    s = _silu(cond_ref[...]).astype(BF16)
    o_ref[0] = jnp.dot(s, w_ref[0].astype(BF16), preferred_element_type=F32) + b_ref[0]


def _ada_mod(cond8, ada_w, ada_b):
    tn = 1024
    n = N_MOD * D_MODEL
    out = pl.pallas_call(
        _ada_kernel,
        grid=(DEPTH, n // tn),
        in_specs=[
            _const_spec((MOD_ROWS, D_MODEL)),
            pl.BlockSpec((1, D_MODEL, tn), lambda l, j: (l, 0, j)),
            pl.BlockSpec((1, 1, tn), lambda l, j: (l, 0, j)),
        ],
        out_specs=pl.BlockSpec((1, MOD_ROWS, tn), lambda l, j: (l, 0, j)),
        out_shape=jax.ShapeDtypeStruct((DEPTH, MOD_ROWS, n), F32),
        compiler_params=_cparams(("arbitrary", "arbitrary")),
        name="ada_mod",
    )(cond8, ada_w, ada_b.reshape(DEPTH, 1, n))
    return out.reshape(DEPTH * MOD_ROWS * N_MOD, 1, D_MODEL)


def _mod_spec(layer, k, row0, tokens_per_row, tm=TM):
    def imap(i):
        row = row0 + (i * tm) // tokens_per_row
        return ((layer * MOD_ROWS + row) * N_MOD + k, 0, 0)
    return pl.BlockSpec((1, 1, D_MODEL), imap)


def _qkv_kernel(x_ref, g_ref, sh_ref, sc_ref, w_ref, q_ref, k_ref, v_ref, ks_ref, vs_ref):
    h = _norm_mod(x_ref[...], g_ref[...], sh_ref[0], sc_ref[0]).astype(BF16)
    y = jnp.dot(h, w_ref[...], preferred_element_type=F32)
    q_ref[...] = (y[:, :D_MODEL] * (NA_HEAD_DIM ** -0.5)).astype(q_ref.dtype)
    k = y[:, D_MODEL:2 * D_MODEL]
    v = y[:, 2 * D_MODEL:]
    k_ref[...] = k.astype(k_ref.dtype)
    v_ref[...] = v.astype(v_ref.dtype)
    ks_ref[...] = pltpu.einshape("t(hd)->thd", k, h=NA_HEADS).reshape(ks_ref.shape)
    vs_ref[...] = pltpu.einshape("t(hd)->thd", v, h=NA_HEADS).reshape(vs_ref.shape)


def _na_qkv(x, g, mod, layer, row0, tokens_per_row, w_qkv, n_batch):
    t = x.shape[0]
    seq = t // n_batch
    nb = TM // seq
    tok = pl.BlockSpec((TM, D_MODEL), lambda i: (i, 0))
    state = pl.BlockSpec((nb, 1, seq, NA_HEADS, NA_HEAD_DIM), lambda i: (i, 0, 0, 0, 0))
    state_shape = jax.ShapeDtypeStruct((n_batch, 1, seq, NA_HEADS, NA_HEAD_DIM), F32)
    return pl.pallas_call(
        _qkv_kernel,
        grid=(t // TM,),
        in_specs=[tok, _const_spec((1, D_MODEL)),
                  _mod_spec(layer, 0, row0, tokens_per_row), _mod_spec(layer, 1, row0, tokens_per_row),
                  _const_spec((D_MODEL, 3 * D_MODEL))],
        out_specs=[tok, tok, tok, state, state],
        out_shape=[jax.ShapeDtypeStruct((t, D_MODEL), BF16)] * 3 + [state_shape, state_shape],
        compiler_params=_cparams(("arbitrary",)),
        name="na_qkv",
    )(x, g, mod, mod, w_qkv)


def _ones_row_slots(vt):
    row = lax.broadcasted_iota(jnp.int32, vt.shape, 0)
    return jnp.where((row & (LANES - 1)) == NA_HEAD_DIM, 1.0, vt)


def _qkv_t_kernel(x_ref, g_ref, sh_ref, sc_ref, wqk_ref, wvt_ref, q_ref, k_ref, vt_ref):
    h = _norm_mod(x_ref[...], g_ref[...], sh_ref[0], sc_ref[0]).astype(BF16)
    y = jnp.dot(h, wqk_ref[...], preferred_element_type=F32)
    q_ref[...] = (y[:, :D_MODEL] * (NA_HEAD_DIM ** -0.5 * LOG2_E)).astype(q_ref.dtype)
    k_ref[...] = y[:, D_MODEL:].astype(k_ref.dtype)
    vt_ref[...] = _ones_row_slots(_qk(wvt_ref[...], h)).astype(vt_ref.dtype)


def _na_qkv_t(x, g, mod, layer, row0, tokens_per_row, w_qk, w_vt):
    t = x.shape[0]
    tok = pl.BlockSpec((TM, D_MODEL), lambda i: (i, 0))
    slots = NA_HEADS * LANES
    return pl.pallas_call(
        _qkv_t_kernel,
        grid=(t // TM,),
        in_specs=[tok, _const_spec((1, D_MODEL)),
                  _mod_spec(layer, 0, row0, tokens_per_row), _mod_spec(layer, 1, row0, tokens_per_row),
                  _const_spec(w_qk.shape), _const_spec(w_vt.shape)],
        out_specs=[tok, tok, pl.BlockSpec((slots, TM), lambda i: (0, i))],
        out_shape=[jax.ShapeDtypeStruct((t, D_MODEL), BF16), jax.ShapeDtypeStruct((t, D_MODEL), BF16),
                   jax.ShapeDtypeStruct((slots, t), BF16)],
        compiler_params=_cparams(("arbitrary",)),
        name="na_qkv_t",
    )(x, g, mod, mod, w_qk, w_vt)


def _softmax_pv(s_list, v_list):
    m = s_list[0].max(axis=-1, keepdims=True)
    for s in s_list[1:]:
        m = jnp.maximum(m, s.max(axis=-1, keepdims=True))
    l = None
    acc = None
    for s, v in zip(s_list, v_list):
        p = jnp.exp(s - m)
        ls = p.sum(axis=-1, keepdims=True)
        a = jnp.dot(p.astype(BF16), v, preferred_element_type=F32)
        l = ls if l is None else l + ls
        acc = a if acc is None else acc + a
    return acc * (1.0 / l)


def _qk(q, k):
    return lax.dot_general(q, k, (((1,), (1,)), ((), ())), preferred_element_type=F32)


def _head_masks():
    lane = lax.broadcasted_iota(jnp.int32, (1, LANES), 1)
    lo = lane < (LANES // 2)
    return lo, jnp.where(lo, 1.0, 0.0).astype(BF16), jnp.where(lo, 0.0, 1.0).astype(BF16)


def _attn_kernel(*refs, n_src, wide_qk, pairs):
    q_ref, o_ref = refs[0], refs[-1]
    kv = refs[1:-1]
    lo, m0, m1 = _head_masks()
    wq = 2 * LANES if wide_qk else LANES
    for p in range(pairs):
        q = q_ref[:, p * wq:(p + 1) * wq]
        ks = [kv[2 * t][:, p * wq:(p + 1) * wq].astype(BF16) for t in range(n_src)]
        vs = [kv[2 * t + 1][:, p * LANES:(p + 1) * LANES].astype(BF16) for t in range(n_src)]
        outs = []
        for j in range(2):
            if wide_qk:
                s_list = [_qk(q[:, j * LANES:(j + 1) * LANES], k[:, j * LANES:(j + 1) * LANES]) for k in ks]
            else:
                s_list = [_qk(q * (m0 if j == 0 else m1), k) for k in ks]
            outs.append(_softmax_pv(s_list, vs))
        o_ref[:, p * LANES:(p + 1) * LANES] = jnp.where(lo, outs[0], outs[1]).astype(o_ref.dtype)


def _attention(q, kv_srcs, n_batch, tq, wide_qk, pairs):
    wq = (2 * LANES if wide_qk else LANES) * pairs
    wv = LANES * pairs
    t_q = q.shape[0] // n_batch
    nq = t_q // tq
    n_steps = D_MODEL // wv
    in_specs = [pl.BlockSpec((tq, wq), lambda b, p, i: (b * nq + i, p))]
    args = [q]
    for k, v in kv_srcs:
        t_k = k.shape[0] // n_batch
        in_specs.append(pl.BlockSpec((t_k, wq), lambda b, p, i: (b, p)))
        in_specs.append(pl.BlockSpec((t_k, wv), lambda b, p, i: (b, p)))
        args += [k, v]
    return pl.pallas_call(
        functools.partial(_attn_kernel, n_src=len(kv_srcs), wide_qk=wide_qk, pairs=pairs),
        grid=(n_batch, n_steps, nq),
        in_specs=in_specs,
        out_specs=pl.BlockSpec((tq, wv), lambda b, p, i: (b * nq + i, p)),
        out_shape=jax.ShapeDtypeStruct((q.shape[0], D_MODEL), BF16),
        compiler_params=_cparams(("arbitrary", "arbitrary", "arbitrary")),
        name="attention",
    )(*args)


def _col_max(s, chunk=64):
    acc = s[:chunk]
    for r in range(chunk, s.shape[0], chunk):
        acc = jnp.maximum(acc, s[r:r + chunk])
    return acc.max(axis=0, keepdims=True)


def _attn_t_kernel(q_ref, k1_ref, k2_ref, v1_ref, v2_ref, o_ref):
    n_heads = q_ref.shape[1] // LANES
    slots = [slice(j * LANES, (j + 1) * LANES) for j in range(n_heads)]
    scores = [(_qk(k1_ref[:, sl], q_ref[:, sl]), _qk(k2_ref[:, sl], q_ref[:, sl])) for sl in slots]
    outs = []
    for sl, (s1, s2) in zip(slots, scores):
        m = jnp.maximum(_col_max(s1), _col_max(s2))
        p1 = jnp.exp2(s1 - m).astype(BF16)
        p2 = jnp.exp2(s2 - m).astype(BF16)
        ot = (jnp.dot(v1_ref[sl, :], p1, preferred_element_type=F32)
              + jnp.dot(v2_ref[sl, :], p2, preferred_element_type=F32))
        outs.append(ot[:MLA_V_DIM] * (1.0 / ot[MLA_V_DIM:MLA_V_DIM + 1]))
    for j in range(0, n_heads, 2):
        pair = jnp.concatenate(outs[j:j + 2], axis=0).T
        o_ref[:, (j // 2) * LANES:(j // 2 + 1) * LANES] = pair.astype(o_ref.dtype)


def _attention_t(q, k1, v1t, k2, v2t, n_batch, tq, heads):
    wq = heads * LANES
    wo = heads * MLA_V_DIM
    nq = q.shape[0] // n_batch // tq
    t1 = k1.shape[0] // n_batch
    t2 = k2.shape[0] // n_batch
    return pl.pallas_call(
        _attn_t_kernel,
        grid=(n_batch, MLA_HEADS // heads, nq),
        in_specs=[pl.BlockSpec((tq, wq), lambda b, p, i: (b * nq + i, p)),
                  pl.BlockSpec((t1, wq), lambda b, p, i: (b, p)),
                  pl.BlockSpec((t2, wq), lambda b, p, i: (b, p)),
                  pl.BlockSpec((wq, t1), lambda b, p, i: (p, b)),
                  pl.BlockSpec((wq, t2), lambda b, p, i: (p, b))],
        out_specs=pl.BlockSpec((tq, wo), lambda b, p, i: (b * nq + i, p)),
        out_shape=jax.ShapeDtypeStruct((q.shape[0], D_MODEL), BF16),
        compiler_params=_cparams(("arbitrary", "arbitrary", "arbitrary")),
        name="attention_t",
    )(q, k1, k2, v1t, v2t)


def _na_group_layout(rows):
    n_groups = rows // NA_GROUP_ROWS
    kh = min(NA_KH, rows)
    bases, types, keys = [], [], {}
    for gi in range(n_groups):
        r0 = gi * NA_GROUP_ROWS
        base = int(np.clip(r0 - kh // 2, 0, rows - NA_WIN_ROWS))
        starts = tuple(int(np.clip(r - kh // 2, 0, rows - kh)) - base for r in range(r0, r0 + NA_GROUP_ROWS))
        key = (starts, r0 - base)
        if key not in keys:
            keys[key] = len(keys)
        bases.append(base)
        types.append(keys[key])
    return bases, types, list(keys.keys())


def _na_bias_tables(rpb, rows):
    kh = min(NA_KH, rows)
    _, _, type_keys = _na_group_layout(rows)
    qc = np.arange(GRID_W)
    kc = np.arange(GRID_W)
    col_start = np.clip(qc - NA_KW // 2, 0, GRID_W - NA_KW)
    col_valid = (kc[None, :] >= col_start[:, None]) & (kc[None, :] < col_start[:, None] + NA_KW)
    dx = kc[None, :] - qc[:, None] + (NA_KW - 1)
    onehot = ((dx[None] == np.arange(2 * NA_KW - 1)[:, None, None]) & col_valid[None]).astype(np.float32)
    band = jnp.einsum('hyd,dqk->hykq', rpb.astype(F32), onehot, precision=lax.Precision.HIGHEST)
    band = jnp.where(col_valid.T, band, NEG_INF)
    strip = band.reshape(NA_HEADS, (2 * NA_KH - 1) * GRID_W, GRID_W)
    tables = []
    for starts, r_off in type_keys:
        q_cols = []
        for rq in range(NA_GROUP_ROWS):
            dy0 = starts[rq] - (rq + r_off) + (NA_KH - 1)
            piece = strip[:, dy0 * GRID_W:(dy0 + kh) * GRID_W, :]
            pad = ((0, 0), (starts[rq] * GRID_W, (NA_WIN_ROWS - kh - starts[rq]) * GRID_W), (0, 0))
            q_cols.append(jnp.pad(piece, pad, constant_values=NEG_INF))
        tables.append(jnp.concatenate(q_cols, axis=2))
    return jnp.stack(tables, axis=1) * LOG2_E


def _state_k_kernel(x_ref, o_ref):
    for h in range(NA_HEADS):
        o_ref[:, h * NA_HEAD_DIM:(h + 1) * NA_HEAD_DIM] = x_ref[0, 0, :, h, :].astype(o_ref.dtype)


def _state_vt_kernel(x_ref, o_ref, flat_ref):
    for h in range(NA_HEADS):
        flat_ref[:, h * NA_HEAD_DIM:(h + 1) * NA_HEAD_DIM] = x_ref[0, 0, :, h, :]
    xt = flat_ref[...].T
    t = xt.shape[1]
    tail = jnp.where(lax.broadcasted_iota(jnp.int32, (LANES - NA_HEAD_DIM, t), 0) == 0, 1.0, 0.0)
    for h in range(NA_HEADS):
        o_ref[h * LANES:h * LANES + NA_HEAD_DIM, :] = xt[h * NA_HEAD_DIM:(h + 1) * NA_HEAD_DIM, :].astype(o_ref.dtype)
        o_ref[h * LANES + NA_HEAD_DIM:(h + 1) * LANES, :] = tail.astype(o_ref.dtype)


def _state_k(x, layer):
    b, _, t, h, dh = x.shape
    return pl.pallas_call(
        _state_k_kernel,
        grid=(b,),
        in_specs=[pl.BlockSpec((1, 1, t, h, dh), lambda i: (i, layer, 0, 0, 0))],
        out_specs=pl.BlockSpec((t, h * dh), lambda i: (i, 0)),
        out_shape=jax.ShapeDtypeStruct((b * t, h * dh), BF16),
        compiler_params=_cparams(("arbitrary",)),
        name="state_k",
    )(x)


def _state_vt(x, layer):
    b, _, t, h, dh = x.shape
    return pl.pallas_call(
        _state_vt_kernel,
        grid=(b,),
        in_specs=[pl.BlockSpec((1, 1, t, h, dh), lambda i: (i, layer, 0, 0, 0))],
        out_specs=pl.BlockSpec((h * LANES, t), lambda i: (0, i)),
        out_shape=jax.ShapeDtypeStruct((h * LANES, b * t), BF16),
        scratch_shapes=[pltpu.VMEM((t, h * dh), F32)],
        compiler_params=_cparams(("arbitrary",)),
        name="state_vt",
    )(x)


def _na_latent_kernel(q_ref, k_ref, vt_ref, kc_ref, vct_ref, tab_ref, o_ref, *, bases, types):
    _, m0, m1 = _head_masks()
    kctx = kc_ref[...]
    gq = NA_GROUP_ROWS * GRID_W
    gk = NA_WIN_ROWS * GRID_W
    blocks = [(gi, base, ty, j) for gi, (base, ty) in enumerate(zip(bases, types)) for j in range(2)]

    def scores(gi, base, ty, j):
        qj = q_ref[gi * gq:(gi + 1) * gq, :] * (m0 if j == 0 else m1)
        kwin = k_ref[base * GRID_W:base * GRID_W + gk, :]
        return _qk(kwin, qj) + tab_ref[j, ty], _qk(kctx, qj)

    nxt = scores(*blocks[0])
    outs = []
    for n, (gi, base, ty, j) in enumerate(blocks):
        s_win, s_ctx = nxt
        if n + 1 < len(blocks):
            nxt = scores(*blocks[n + 1])
        m = jnp.maximum(_col_max(s_win), _col_max(s_ctx))
        p_win = jnp.exp2(s_win - m).astype(BF16)
        p_ctx = jnp.exp2(s_ctx - m).astype(BF16)
        sl = slice(j * LANES, (j + 1) * LANES)
        ot = (jnp.dot(vt_ref[sl, base * GRID_W:base * GRID_W + gk], p_win, preferred_element_type=F32)
              + jnp.dot(vct_ref[sl, :], p_ctx, preferred_element_type=F32))
        outs.append(ot[:NA_HEAD_DIM] * (1.0 / ot[NA_HEAD_DIM:NA_HEAD_DIM + 1]))
        if j == 1:
            o_ref[gi * gq:(gi + 1) * gq, :] = jnp.concatenate(outs, axis=0).T.astype(o_ref.dtype)
            outs = []


def _na_latent(q, k, vt, k_ctx, vt_ctx, tables, n_batch):
    t = q.shape[0] // n_batch
    rows = t // GRID_W
    t_ctx = k_ctx.shape[0] // n_batch
    bases, types, type_keys = _na_group_layout(rows)
    n_types = len(type_keys)
    gq = NA_GROUP_ROWS * GRID_W
    gk = NA_WIN_ROWS * GRID_W
    tok = pl.BlockSpec((t, LANES), lambda p, b: (b, p))
    ctx = pl.BlockSpec((t_ctx, LANES), lambda p, b: (b, p))
    return pl.pallas_call(
        functools.partial(_na_latent_kernel, bases=bases, types=types),
        grid=(D_MODEL // LANES, n_batch),
        in_specs=[tok, tok, pl.BlockSpec((2 * LANES, t), lambda p, b: (p, b)),
                  ctx, pl.BlockSpec((2 * LANES, t_ctx), lambda p, b: (p, b)),
                  pl.BlockSpec((2, n_types, gk, gq), lambda p, b: (p, 0, 0, 0))],
        out_specs=tok,
        out_shape=jax.ShapeDtypeStruct(q.shape, BF16),
        compiler_params=_cparams(("arbitrary", "arbitrary")),
        name="na_latent",
    )(q, k, vt, k_ctx, vt_ctx, tables)


def _rope_lane_tables(n_tokens):
    n_freq = MLA_ROPE_DIM // 4
    inv = 1.0 / (ROPE_THETA ** (jnp.arange(n_freq, dtype=F32) / n_freq))
    tt = jnp.arange(n_tokens)
    row = (tt // GRID_W).astype(F32)
    col = (tt % GRID_W).astype(F32)
    ang = jnp.concatenate([row[:, None] * inv, col[:, None] * inv], axis=-1)
    cos, sin = jnp.cos(ang), jnp.sin(ang)
    ones = jnp.ones((n_tokens, MLA_NOPE_DIM), F32)
    z16 = jnp.zeros((n_tokens, MLA_ROPE_DIM // 2), F32)
    z32 = jnp.zeros((n_tokens, LANES - MLA_QK_DIM), F32)
    z64 = jnp.zeros((n_tokens, MLA_NOPE_DIM), F32)
    c_tab = jnp.concatenate([ones, cos, cos, z32], axis=-1)
    s_dn = jnp.concatenate([z64, -sin, z16, z32], axis=-1)
    s_up = jnp.concatenate([z64, z16, sin, z32], axis=-1)
    return c_tab, s_dn, s_up


def _rope_slot(x, c_tab, s_dn, s_up):
    half = MLA_ROPE_DIM // 2
    return x * c_tab + pltpu.roll(x, LANES - half, 1) * s_dn + pltpu.roll(x, half, 1) * s_up


def _mla_kv(ckv_n, kr_slot, wuk_ref, wuv_ref, k_ref, v_ref, v_transposed):
    c = ckv_n.astype(BF16)
    k_nope = jnp.dot(c, wuk_ref[...], preferred_element_type=F32)
    for h in range(MLA_HEADS):
        k_ref[:, h * LANES:(h + 1) * LANES] = (k_nope[:, h * LANES:(h + 1) * LANES] + kr_slot).astype(k_ref.dtype)
    if v_transposed:
        vt = _qk(wuv_ref[...], c)
        row = lax.broadcasted_iota(jnp.int32, vt.shape, 0)
        v_ref[...] = jnp.where((row & (LANES - 1)) == MLA_V_DIM, 1.0, vt).astype(v_ref.dtype)
    else:
        v_ref[...] = jnp.dot(c, wuv_ref[...], preferred_element_type=F32).astype(v_ref.dtype)


def _mla_proj_kernel(*refs, rope, emit_state, q_scale):
    (x_ref, g_ref, sh_ref, sc_ref, win_ref, qg_ref, kvg_ref, wuq_ref, wuk_ref, wuv_ref) = refs[:10]
    pos = 10
    if rope:
        ct_ref, sd_ref, su_ref = refs[pos:pos + 3]
        pos += 3
    q_ref, k_ref, v_ref = refs[pos:pos + 3]
    pos += 3
    h = _norm_mod(x_ref[...], g_ref[...], sh_ref[0], sc_ref[0]).astype(BF16)
    c = jnp.dot(h, win_ref[...], preferred_element_type=F32)
    cq = _rms(c[:, :MLA_Q_RANK], qg_ref[...]).astype(BF16)
    ckv = _rms(c[:, MLA_Q_RANK:MLA_Q_RANK + MLA_KV_RANK], kvg_ref[...])
    kr_chunk = c[:, MLA_Q_RANK + MLA_KV_RANK:]
    if emit_state:
        ckv_out, kr_out = refs[pos:pos + 2]
        ckv_out[...] = ckv
        kr_out[...] = kr_chunk[:, :MLA_ROPE_DIM]
    kr_slot = pltpu.roll(kr_chunk, MLA_NOPE_DIM, 1)
    q = jnp.dot(cq, wuq_ref[...], preferred_element_type=F32) * q_scale
    if rope:
        ct, sd, su = ct_ref[...], sd_ref[...], su_ref[...]
        kr_slot = _rope_slot(kr_slot, ct, sd, su)
    for hh in range(MLA_HEADS):
        qh = q[:, hh * LANES:(hh + 1) * LANES]
        if rope:
            qh = _rope_slot(qh, ct, sd, su)
        q_ref[:, hh * LANES:(hh + 1) * LANES] = qh.astype(q_ref.dtype)
    _mla_kv(ckv, kr_slot, wuk_ref, wuv_ref, k_ref, v_ref, v_transposed=rope)


def _mla_proj(x, g, mod, layer, row0, tokens_per_row, w, rope_tabs, emit_state):
    t = x.shape[0]
    win, qg, kvg, wuq, wuk, wuv = w
    tok = lambda n: pl.BlockSpec((TM, n), lambda i: (i, 0))
    wq = MLA_HEADS * LANES
    in_specs = [tok(D_MODEL), _const_spec((1, D_MODEL)),
                _mod_spec(layer, 0, row0, tokens_per_row), _mod_spec(layer, 1, row0, tokens_per_row),
                _const_spec(win.shape), _const_spec(qg.shape), _const_spec(kvg.shape),
                _const_spec(wuq.shape), _const_spec(wuk.shape), _const_spec(wuv.shape)]
    args = [x, g, mod, mod, win, qg, kvg, wuq, wuk, wuv]
    rope = rope_tabs is not None
    if rope:
        n_tab = rope_tabs[0].shape[0] // TM
        in_specs += [pl.BlockSpec((TM, LANES), lambda i: (i % n_tab, 0)) for _ in range(3)]
        args += list(rope_tabs)
    if rope:
        v_spec, v_shape = pl.BlockSpec((wq, TM), lambda i: (0, i)), jax.ShapeDtypeStruct((wq, t), BF16)
        q_scale = MLA_QK_DIM ** -0.5 * LOG2_E
    else:
        v_spec, v_shape = tok(D_MODEL), jax.ShapeDtypeStruct((t, D_MODEL), BF16)
        q_scale = MLA_QK_DIM ** -0.5
    out_specs = [tok(wq), tok(wq), v_spec]
    out_shape = [jax.ShapeDtypeStruct((t, wq), BF16), jax.ShapeDtypeStruct((t, wq), BF16), v_shape]
    if emit_state:
        out_specs += [tok(MLA_KV_RANK), tok(MLA_ROPE_DIM)]
        out_shape += [jax.ShapeDtypeStruct((t, MLA_KV_RANK), F32), jax.ShapeDtypeStruct((t, MLA_ROPE_DIM), F32)]
    return pl.pallas_call(
        functools.partial(_mla_proj_kernel, rope=rope, emit_state=emit_state, q_scale=q_scale),
        grid=(t // TM,),
        in_specs=in_specs, out_specs=out_specs, out_shape=out_shape,
        compiler_params=_cparams(("arbitrary",)),
        name="mla_proj",
    )(*args)


def _mla_state_kv_kernel(ckv_ref, kr_ref, wuk_ref, wuv_ref, k_ref, v_ref):
    kr_slot = pltpu.roll(kr_ref[...], MLA_NOPE_DIM, 1)
    _mla_kv(ckv_ref[...], kr_slot, wuk_ref, wuv_ref, k_ref, v_ref, v_transposed=True)


def _mla_state_kv(ckv, kr_pad, wuk, wuv_t):
    t = ckv.shape[0]
    wq = MLA_HEADS * LANES
    tok = lambda n: pl.BlockSpec((TM, n), lambda i: (i, 0))
    return pl.pallas_call(
        _mla_state_kv_kernel,
        grid=(t // TM,),
        in_specs=[tok(MLA_KV_RANK), tok(LANES), _const_spec(wuk.shape), _const_spec(wuv_t.shape)],
        out_specs=[tok(wq), pl.BlockSpec((wq, TM), lambda i: (0, i))],
        out_shape=[jax.ShapeDtypeStruct((t, wq), BF16), jax.ShapeDtypeStruct((wq, t), BF16)],
        compiler_params=_cparams(("arbitrary",)),
        name="mla_state_kv",
    )(ckv, kr_pad, wuk, wuv_t)


def _lane_argmax(v, lane):
    top = v.max(axis=-1, keepdims=True)
    idx = jnp.where(v == top, lane, float(LANES)).min(axis=-1, keepdims=True)
    return top, idx


def _route(logits, lane):
    is_g = lane < N_GROUPS
    gl = jnp.where(is_g, logits, -jnp.inf)
    gmax, gsel = _lane_argmax(gl, lane)
    p_group = 1.0 / jnp.where(is_g, jnp.exp(logits - gmax), 0.0).sum(axis=-1, keepdims=True)
    first = N_GROUPS + gsel * EXPERTS_PER_GROUP
    el = jnp.where((lane >= first) & (lane < first + EXPERTS_PER_GROUP), logits, -jnp.inf)
    t1, i1 = _lane_argmax(el, lane)
    t2, i2 = _lane_argmax(jnp.where(lane == i1, -jnp.inf, el), lane)
    e2 = jnp.exp(t2 - t1)
    w1 = 1.0 / (1.0 + e2)
    gates = jnp.where(lane == i1, w1 * p_group, jnp.where(lane == i2, e2 * w1 * p_group, 0.0))
    return gates, gsel


def _pack_halves(x):
    half = x.shape[1] // 2
    lo = lax.bitcast_convert_type(x[:, :half].astype(BF16).astype(F32), jnp.uint32)
    hi = lax.bitcast_convert_type(x[:, half:].astype(BF16).astype(F32), jnp.uint32)
    return (lo >> 16) | hi


def _unpack_halves(p):
    lo = lax.bitcast_convert_type(p << 16, F32)
    hi = lax.bitcast_convert_type(p & jnp.uint32(0xFFFF0000), F32)
    return lo, hi


ROW_WORDS = D_MODEL // 2 + LANES
META_GROUP_LANE = EXPERTS_PER_GROUP
META_RANK_LANE = EXPERTS_PER_GROUP + 1


def _proj_res_kernel(o_ref, x_ref, w_ref, gate_ref, g_ref, sh_ref, sc_ref, wr_ref, br_ref, tri_ref,
                     xn_ref, hpk_ref, meta_ref, cnt_ref):
    @pl.when(pl.program_id(0) == 0)
    def _():
        cnt_ref[...] = jnp.zeros(cnt_ref.shape, cnt_ref.dtype)

    y = jnp.dot(o_ref[...], w_ref[...], preferred_element_type=F32)
    xn = x_ref[...] + gate_ref[0] * y
    xn_ref[...] = xn
    h = _norm_mod(xn, g_ref[...], sh_ref[0], sc_ref[0])
    logits = jnp.dot(h.astype(BF16), wr_ref[...], preferred_element_type=F32) + br_ref[...]
    lane = lax.broadcasted_iota(jnp.int32, logits.shape, 1).astype(F32)
    gates, gsel = _route(logits, lane)
    rel = pltpu.roll(gates, LANES - N_GROUPS, 1)
    for g in range(1, N_GROUPS):
        rel = rel + pltpu.roll(gates, LANES - N_GROUPS - g * EXPERTS_PER_GROUP, 1)
    sel = lane == gsel
    onehot = jnp.where(sel, 1.0, 0.0)
    prefix = jnp.dot(tri_ref[...], onehot.astype(BF16), preferred_element_type=F32)
    rank = jnp.where(sel, prefix + cnt_ref[...] - 1.0, 0.0).sum(axis=-1, keepdims=True)
    cnt_ref[...] = cnt_ref[...] + onehot.sum(axis=0, keepdims=True)
    meta = jnp.where(lane < EXPERTS_PER_GROUP, rel,
                     jnp.where(lane == META_GROUP_LANE, gsel, jnp.where(lane == META_RANK_LANE, rank, 0.0)))
    meta_ref[...] = meta
    hpk_ref[...] = jnp.concatenate([_pack_halves(h), lax.bitcast_convert_type(meta, jnp.uint32)], axis=1)


def _proj_res(o, x, w_o, g_ffn, mod, layer, row0, tokens_per_row, w_router, b_router):
    t = x.shape[0]
    tm = TM_PROJ
    tok = lambda n: pl.BlockSpec((tm, n), lambda i: (i, 0))
    ms = lambda k: _mod_spec(layer, k, row0, tokens_per_row, tm)
    tri = jnp.asarray(np.tril(np.ones((tm, tm), np.float32)), BF16)
    return pl.pallas_call(
        _proj_res_kernel,
        grid=(t // tm,),
        in_specs=[tok(D_MODEL), tok(D_MODEL), _const_spec(w_o.shape), ms(2), _const_spec((1, D_MODEL)), ms(3), ms(4),
                  _const_spec(w_router.shape), _const_spec(b_router.shape), _const_spec((tm, tm))],
        out_specs=[tok(D_MODEL), tok(ROW_WORDS), tok(LANES), _const_spec((1, LANES))],
        out_shape=[jax.ShapeDtypeStruct((t, D_MODEL), F32), jax.ShapeDtypeStruct((t, ROW_WORDS), jnp.uint32),
                   jax.ShapeDtypeStruct((t, LANES), F32), jax.ShapeDtypeStruct((1, LANES), F32)],
        compiler_params=_cparams(("arbitrary",)),
        name="proj_res",
    )(o, x, w_o, mod, g_ffn, mod, mod, w_router, b_router, tri)


def _moe_plan(meta, cnt):
    t = meta.shape[0]
    n_tiles = t // TM + N_GROUPS
    group = meta[:, META_GROUP_LANE].astype(jnp.int32)
    rank = meta[:, META_RANK_LANE].astype(jnp.int32)
    counts = cnt[0, :N_GROUPS].astype(jnp.int32)
    tiles = (counts + TM - 1) // TM
    tile_end = jnp.cumsum(tiles)
    tile_start = tile_end - tiles
    pos = rank + TM * jnp.sum(jnp.where(group[:, None] == jnp.arange(N_GROUPS)[None], tile_start[None], 0), axis=1)
    tile_group = jnp.minimum(jnp.sum(jnp.arange(n_tiles)[:, None] >= tile_end[None], axis=1), N_GROUPS - 1)
    pad_bounds = jnp.stack([tile_start * TM + counts, tile_end * TM], axis=1).reshape(-1)
    return (pos.astype(jnp.int32), tile_group.astype(jnp.int32), tile_end[-1:].astype(jnp.int32),
            pad_bounds.astype(jnp.int32))


def _gather_rows(srcs, idx_ref, base, dsts, n_rows):
    for r in range(n_rows):
        s = idx_ref[base + r]
        for src_ref, dst_ref in zip(srcs, dsts):
            dst_ref[r:r + 1, :] = src_ref[pl.ds(s, 1), :]


def _moe_kernel(pos_ref, tg_ref, nt_ref, pad_ref, hpk_ref, wgu_ref, wd_ref, ypk_ref, src_ref, hbuf, hid_ref):
    i = pl.program_id(0)
    n_tok = hpk_ref.shape[0]

    @pl.when(i == 0)
    def _():
        def clear(p, c):
            src_ref[p] = 0
            return c
        for g in range(N_GROUPS):
            lax.fori_loop(pad_ref[2 * g], pad_ref[2 * g + 1], clear, 0)

        def fill(t, c):
            src_ref[pos_ref[t]] = t
            return c
        lax.fori_loop(0, n_tok, fill, 0, unroll=8)
        _gather_rows((hpk_ref,), src_ref, 0, (hbuf.at[0],), TM)

    @pl.when(i < nt_ref[0])
    def _():
        slot = i % 2
        row = hbuf[slot]
        lo, hi = _unpack_halves(row[:, :D_MODEL // 2])
        h = jnp.concatenate([lo.astype(BF16), hi.astype(BF16)], axis=1)
        gates = lax.bitcast_convert_type(row[:, D_MODEL // 2:], F32)
        nxt = jnp.minimum(i + 1, nt_ref[0] - 1) * TM
        _gather_rows((hpk_ref,), src_ref, nxt, (hbuf.at[1 - slot],), TM)
        f = D_FF_EXPERT
        for j in range(EXPERTS_PER_GROUP):
            gu = jnp.dot(h, wgu_ref[0, j], preferred_element_type=F32)
            hid = _silu(gu[:, :f]) * gu[:, f:] * gates[:, j:j + 1]
            hid_ref[:, j * f:(j + 1) * f] = hid.astype(BF16)
        ypk_ref[...] = _pack_halves(jnp.dot(hid_ref[...], wd_ref[0], preferred_element_type=F32))

    @pl.when(i >= nt_ref[0])
    def _():
        ypk_ref[...] = jnp.zeros(ypk_ref.shape, ypk_ref.dtype)


def _moe(hpk, pos, tile_group, n_used, pad_bounds, wgu, wd):
    t = hpk.shape[0]
    n_tiles = t // TM + N_GROUPS
    f = D_FF_EXPERT
    grid_spec = pltpu.PrefetchScalarGridSpec(
        num_scalar_prefetch=4,
        grid=(n_tiles,),
        in_specs=[
            pl.BlockSpec(hpk.shape, lambda i, *_: (0, 0), pipeline_mode=pl.Buffered(1)),
            pl.BlockSpec((1, EXPERTS_PER_GROUP, D_MODEL, 2 * f), lambda i, pos, tg, nt, pad: (tg[i], 0, 0, 0)),
            pl.BlockSpec((1, EXPERTS_PER_GROUP * f, D_MODEL), lambda i, pos, tg, nt, pad: (tg[i], 0, 0)),
        ],
        out_specs=pl.BlockSpec((TM, D_MODEL // 2), lambda i, *_: (i, 0)),
        scratch_shapes=[pltpu.SMEM((n_tiles * TM,), jnp.int32),
                        pltpu.VMEM((2, TM, ROW_WORDS), jnp.uint32),
                        pltpu.VMEM((TM, EXPERTS_PER_GROUP * f), BF16)],
    )
    return pl.pallas_call(
        _moe_kernel,
        grid_spec=grid_spec,
        out_shape=jax.ShapeDtypeStruct((n_tiles * TM, D_MODEL // 2), jnp.uint32),
        compiler_params=_cparams(("arbitrary",)),
        name="moe",
    )(pos, tile_group, n_used, pad_bounds, hpk, wgu, wd)


def _combine_kernel(*refs, final_norm):
    pos_ref, ypk_ref, x_ref, gate_ref = refs[:4]
    o_ref, ybuf = refs[-2], refs[-1]
    _gather_rows((ypk_ref,), pos_ref, pl.program_id(0) * TM, (ybuf,), TM)
    lo, hi = _unpack_halves(ybuf[...])
    out = x_ref[...] + gate_ref[0] * jnp.concatenate([lo, hi], axis=1)
    if final_norm:
        out = _rms(out, refs[4][...])
    o_ref[...] = out


def _combine(ypk, pos, x, mod, layer, row0, tokens_per_row, final_g):
    t = x.shape[0]

    def mod_map(i, pos):
        row = row0 + (i * TM) // tokens_per_row
        return ((layer * MOD_ROWS + row) * N_MOD + 5, 0, 0)

    in_specs = [pl.BlockSpec(ypk.shape, lambda i, pos: (0, 0), pipeline_mode=pl.Buffered(1)),
                pl.BlockSpec((TM, D_MODEL), lambda i, pos: (i, 0)),
                pl.BlockSpec((1, 1, D_MODEL), mod_map)]
    args = [ypk, x, mod]
    if final_g is not None:
        in_specs.append(pl.BlockSpec((1, D_MODEL), lambda i, pos: (0, 0)))
        args.append(final_g)
    grid_spec = pltpu.PrefetchScalarGridSpec(
        num_scalar_prefetch=1,
        grid=(t // TM,),
        in_specs=in_specs,
        out_specs=pl.BlockSpec((TM, D_MODEL), lambda i, pos: (i, 0)),
        scratch_shapes=[pltpu.VMEM((TM, D_MODEL // 2), jnp.uint32)],
    )
    return pl.pallas_call(
        functools.partial(_combine_kernel, final_norm=final_g is not None),
        grid_spec=grid_spec,
        out_shape=jax.ShapeDtypeStruct((t, D_MODEL), F32),
        compiler_params=_cparams(("arbitrary",)),
        name="moe_combine",
    )(pos, *args)


def _pad_heads(w, used):
    k = w.shape[0]
    w = w.reshape(k, MLA_HEADS, used)
    return jnp.pad(w, ((0, 0), (0, 0), (0, LANES - used))).reshape(k, MLA_HEADS * LANES)


def kernel(x_prompt, x_sample, state_na_k, state_na_v, state_mla_ckv, state_mla_kr, c, c_ctx, ada_w, ada_b, norm_mix_g, norm_ffn_g, final_norm_g, na_w_qkv, na_w_o, na_rpb, mla_w_in, mla_q_norm_g, mla_kv_norm_g, mla_w_uq, mla_w_uk, mla_w_uv, mla_w_o, moe_w_group, moe_b_group, moe_w_expert, moe_b_expert, moe_w_gate, moe_w_up, moe_w_down):
    n_b, seq, d = x_prompt.shape
    dec_b, n_lat, _ = x_sample.shape
    past = state_na_k.shape[2]
    rows = n_lat // GRID_W
    assert d == D_MODEL and dec_b + 1 <= MOD_ROWS and n_lat % TM_PROJ == 0 and (n_b * seq) % TM_PROJ == 0

    cond8 = jnp.concatenate([c_ctx[None], c, jnp.zeros((MOD_ROWS - 1 - dec_b, d), F32)], axis=0)
    mod = _ada_mod(cond8, ada_w, ada_b)

    xp = x_prompt.reshape(n_b * seq, d)
    xs = x_sample.reshape(dec_b * n_lat, d)
    p_rows = (0, n_b * seq)
    s_rows = (1, n_lat)
    outs = {}

    for i in range(DEPTH):
        j = i // 2
        g_mix = norm_mix_g[i][None]
        g_ffn = norm_ffn_g[i][None]
        if i % 2 == 0:
            w_qkv = na_w_qkv[j].astype(BF16)
            w_vt = jnp.pad(na_w_qkv[j][:, 2 * d:].T.reshape(NA_HEADS, NA_HEAD_DIM, d),
                           ((0, 0), (0, LANES - NA_HEAD_DIM), (0, 0))).reshape(NA_HEADS * LANES, d).astype(BF16)
            qp, kp, vp, outs['na_k'], outs['na_v'] = _na_qkv(xp, g_mix, mod, i, *p_rows, w_qkv, n_b)
            qs, ks, vs_t = _na_qkv_t(xs, g_mix, mod, i, *s_rows, w_qkv[:, :2 * d], w_vt)
            op = _attention(qp, [(kp, vp)], n_b, seq, wide_qk=False, pairs=8)
            tables = _na_bias_tables(na_rpb[j], rows)
            os_ = _na_latent(qs, ks, vs_t, _state_k(state_na_k, j), _state_vt(state_na_v, j), tables, dec_b)
            w_o = na_w_o[j].astype(BF16)
        else:
            pad_in = MLA_IN_PAD - mla_w_in.shape[-1]
            w_mla = (jnp.pad(mla_w_in[j], ((0, 0), (0, pad_in))).astype(BF16),
                     mla_q_norm_g[j][None], mla_kv_norm_g[j][None],
                     _pad_heads(mla_w_uq[j], MLA_QK_DIM).astype(BF16),
                     _pad_heads(mla_w_uk[j], MLA_NOPE_DIM).astype(BF16),
                     mla_w_uv[j].astype(BF16))
            wuv_t = _pad_heads(mla_w_uv[j], MLA_V_DIM).T.astype(BF16)
            w_mla_t = w_mla[:5] + (wuv_t,)
            qp, kp, vp, ckv_p, kr_p = _mla_proj(xp, g_mix, mod, i, *p_rows, w_mla, None, True)
            qs, ks, vs_t = _mla_proj(xs, g_mix, mod, i, *s_rows, w_mla_t, _rope_lane_tables(n_lat), False)
            outs['ckv'] = ckv_p.reshape(n_b, 1, seq, MLA_KV_RANK)
            outs['kr'] = kr_p.reshape(n_b, 1, seq, MLA_ROPE_DIM)
            kr_state = jnp.pad(state_mla_kr[:, j].reshape(dec_b * past, MLA_ROPE_DIM),
                               ((0, 0), (0, LANES - MLA_ROPE_DIM)))
            kc, vc_t = _mla_state_kv(state_mla_ckv[:, j].reshape(dec_b * past, MLA_KV_RANK), kr_state,
                                     w_mla[4], wuv_t)
            op = _attention(qp, [(kp, vp)], n_b, seq, wide_qk=True, pairs=8)
            os_ = _attention_t(qs, ks, vs_t, kc, vc_t, dec_b, 512, heads=4)
            w_o = mla_w_o[j].astype(BF16)

        w_router = jnp.pad(jnp.concatenate([moe_w_group[i], moe_w_expert[i]], axis=1),
                           ((0, 0), (0, LANES - N_GROUPS - N_EXPERTS))).astype(BF16)
        b_router = jnp.pad(jnp.concatenate([moe_b_group[i], moe_b_expert[i]]),
                           (0, LANES - N_GROUPS - N_EXPERTS))[None]
        f = D_FF_EXPERT
        wgu = jnp.concatenate([moe_w_gate[i], moe_w_up[i]], axis=-1).astype(BF16)
        wgu = wgu.reshape(N_GROUPS, EXPERTS_PER_GROUP, d, 2 * f)
        wd = moe_w_down[i].astype(BF16).reshape(N_GROUPS, EXPERTS_PER_GROUP * f, d)
        final_g = final_norm_g[None] if i == DEPTH - 1 else None
        new_x = []
        for o_att, x, rows_ in ((op, xp, p_rows), (os_, xs, s_rows)):
            xn, hpk, meta, cnt = _proj_res(o_att, x, w_o, g_ffn, mod, i, *rows_, w_router, b_router)
            pos, tile_group, n_used, pad_bounds = _moe_plan(meta, cnt)
            ypk = _moe(hpk, pos, tile_group, n_used, pad_bounds, wgu, wd)
            new_x.append(_combine(ypk, pos, xn, mod, i, *rows_, final_g))
        xp, xs = new_x

    return (xp.reshape(n_b, seq, d), xs.reshape(dec_b, n_lat, d),
            outs['na_k'], outs['na_v'], outs['ckv'], outs['kr'])
```

```python
import functools

import numpy as np
import jax
import jax.numpy as jnp
from jax import lax
from jax.experimental import pallas as pl
from jax.experimental.pallas import tpu as pltpu

F32 = jnp.float32
BF16 = jnp.bfloat16

D_MODEL = 1024
DEPTH = 2
GRID_W = 64
N_MOD = 6
EPS = 1e-6
NEG_INF = -1e30
NA_HEADS = 16
NA_HEAD_DIM = 64
NA_KH = 8
NA_KW = 16
NA_GROUP_ROWS = 4
NA_WIN_ROWS = 12
MLA_HEADS = 16
MLA_Q_RANK = 384
MLA_KV_RANK = 256
MLA_NOPE_DIM = 64
MLA_ROPE_DIM = 32
MLA_V_DIM = 64
MLA_QK_DIM = MLA_NOPE_DIM + MLA_ROPE_DIM
MLA_IN_PAD = 768
ROPE_THETA = 10000.0
LOG2_E = 1.4426950408889634
N_GROUPS = 4
EXPERTS_PER_GROUP = 4
N_EXPERTS = 16
D_FF_EXPERT = 256

LANES = 128
MOD_ROWS = 8
VMEM_LIMIT = 56 * 1024 * 1024
TM = 512
TM_PROJ = 1024


def _cparams(sem):
    return pltpu.CompilerParams(dimension_semantics=sem, vmem_limit_bytes=VMEM_LIMIT)


def _const_spec(shape):
    nd = len(shape)
    return pl.BlockSpec(shape, lambda *_: (0,) * nd)


def _norm_mod(x, g, shift, scale):
    ms = jnp.mean(x * x, axis=-1, keepdims=True)
    y = x * lax.rsqrt(ms + EPS) * g
    return y * (1.0 + scale) + shift


def _rms(x, g):
    ms = jnp.mean(x * x, axis=-1, keepdims=True)
    return x * lax.rsqrt(ms + EPS) * g


def _silu(x):
    return x / (1.0 + jnp.exp(-x))


def _ada_kernel(cond_ref, w_ref, b_ref, o_ref):
    s = _silu(cond_ref[...]).astype(BF16)
    o_ref[0] = jnp.dot(s, w_ref[0].astype(BF16), preferred_element_type=F32) + b_ref[0]


def _ada_mod(cond8, ada_w, ada_b):
    tn = 1024
    n = N_MOD * D_MODEL
    out = pl.pallas_call(
        _ada_kernel,
        grid=(DEPTH, n // tn),
        in_specs=[
            _const_spec((MOD_ROWS, D_MODEL)),
            pl.BlockSpec((1, D_MODEL, tn), lambda l, j: (l, 0, j)),
            pl.BlockSpec((1, 1, tn), lambda l, j: (l, 0, j)),
        ],
        out_specs=pl.BlockSpec((1, MOD_ROWS, tn), lambda l, j: (l, 0, j)),
        out_shape=jax.ShapeDtypeStruct((DEPTH, MOD_ROWS, n), F32),
        compiler_params=_cparams(("arbitrary", "arbitrary")),
        name="ada_mod",
    )(cond8, ada_w, ada_b.reshape(DEPTH, 1, n))
    return out.reshape(DEPTH * MOD_ROWS * N_MOD, 1, D_MODEL)


def _mod_spec(layer, k, row0, tokens_per_row, tm=TM):
    def imap(i):
        row = row0 + (i * tm) // tokens_per_row
        return ((layer * MOD_ROWS + row) * N_MOD + k, 0, 0)
    return pl.BlockSpec((1, 1, D_MODEL), imap)


def _qkv_kernel(x_ref, g_ref, sh_ref, sc_ref, w_ref, q_ref, k_ref, v_ref, ks_ref, vs_ref):
    h = _norm_mod(x_ref[...], g_ref[...], sh_ref[0], sc_ref[0]).astype(BF16)
    y = jnp.dot(h, w_ref[...], preferred_element_type=F32)
    q_ref[...] = (y[:, :D_MODEL] * (NA_HEAD_DIM ** -0.5)).astype(q_ref.dtype)
    k = y[:, D_MODEL:2 * D_MODEL]
    v = y[:, 2 * D_MODEL:]
    k_ref[...] = k.astype(k_ref.dtype)
    v_ref[...] = v.astype(v_ref.dtype)
    ks_ref[...] = pltpu.einshape("t(hd)->thd", k, h=NA_HEADS).reshape(ks_ref.shape)
    vs_ref[...] = pltpu.einshape("t(hd)->thd", v, h=NA_HEADS).reshape(vs_ref.shape)


def _na_qkv(x, g, mod, layer, row0, tokens_per_row, w_qkv, n_batch):
    t = x.shape[0]
    seq = t // n_batch
    nb = TM // seq
    tok = pl.BlockSpec((TM, D_MODEL), lambda i: (i, 0))
    state = pl.BlockSpec((nb, 1, seq, NA_HEADS, NA_HEAD_DIM), lambda i: (i, 0, 0, 0, 0))
    state_shape = jax.ShapeDtypeStruct((n_batch, 1, seq, NA_HEADS, NA_HEAD_DIM), F32)
    return pl.pallas_call(
        _qkv_kernel,
        grid=(t // TM,),
        in_specs=[tok, _const_spec((1, D_MODEL)),
                  _mod_spec(layer, 0, row0, tokens_per_row), _mod_spec(layer, 1, row0, tokens_per_row),
                  _const_spec((D_MODEL, 3 * D_MODEL))],
        out_specs=[tok, tok, tok, state, state],
        out_shape=[jax.ShapeDtypeStruct((t, D_MODEL), BF16)] * 3 + [state_shape, state_shape],
        compiler_params=_cparams(("arbitrary",)),
        name="na_qkv",
    )(x, g, mod, mod, w_qkv)


def _ones_row_slots(vt):
    row = lax.broadcasted_iota(jnp.int32, vt.shape, 0)
    return jnp.where((row & (LANES - 1)) == NA_HEAD_DIM, 1.0, vt)


def _qkv_t_kernel(x_ref, g_ref, sh_ref, sc_ref, wqk_ref, wvt_ref, q_ref, k_ref, vt_ref):
    h = _norm_mod(x_ref[...], g_ref[...], sh_ref[0], sc_ref[0]).astype(BF16)
    y = jnp.dot(h, wqk_ref[...], preferred_element_type=F32)
    q_ref[...] = (y[:, :D_MODEL] * (NA_HEAD_DIM ** -0.5 * LOG2_E)).astype(q_ref.dtype)
    k_ref[...] = y[:, D_MODEL:].astype(k_ref.dtype)
    vt_ref[...] = _ones_row_slots(_qk(wvt_ref[...], h)).astype(vt_ref.dtype)


def _na_qkv_t(x, g, mod, layer, row0, tokens_per_row, w_qk, w_vt):
    t = x.shape[0]
    tok = pl.BlockSpec((TM, D_MODEL), lambda i: (i, 0))
    slots = NA_HEADS * LANES
    return pl.pallas_call(
        _qkv_t_kernel,
        grid=(t // TM,),
        in_specs=[tok, _const_spec((1, D_MODEL)),
                  _mod_spec(layer, 0, row0, tokens_per_row), _mod_spec(layer, 1, row0, tokens_per_row),
                  _const_spec(w_qk.shape), _const_spec(w_vt.shape)],
        out_specs=[tok, tok, pl.BlockSpec((slots, TM), lambda i: (0, i))],
        out_shape=[jax.ShapeDtypeStruct((t, D_MODEL), BF16), jax.ShapeDtypeStruct((t, D_MODEL), BF16),
                   jax.ShapeDtypeStruct((slots, t), BF16)],
        compiler_params=_cparams(("arbitrary",)),
        name="na_qkv_t",
    )(x, g, mod, mod, w_qk, w_vt)


def _softmax_pv(s_list, v_list):
    m = s_list[0].max(axis=-1, keepdims=True)
    for s in s_list[1:]:
        m = jnp.maximum(m, s.max(axis=-1, keepdims=True))
    l = None
    acc = None
    for s, v in zip(s_list, v_list):
        p = jnp.exp(s - m)
        ls = p.sum(axis=-1, keepdims=True)
        a = jnp.dot(p.astype(BF16), v, preferred_element_type=F32)
        l = ls if l is None else l + ls
        acc = a if acc is None else acc + a
    return acc * (1.0 / l)


def _qk(q, k):
    return lax.dot_general(q, k, (((1,), (1,)), ((), ())), preferred_element_type=F32)


def _head_masks():
    lane = lax.broadcasted_iota(jnp.int32, (1, LANES), 1)
    lo = lane < (LANES // 2)
    return lo, jnp.where(lo, 1.0, 0.0).astype(BF16), jnp.where(lo, 0.0, 1.0).astype(BF16)


def _attn_kernel(*refs, n_src, wide_qk, pairs):
    q_ref, o_ref = refs[0], refs[-1]
    kv = refs[1:-1]
    lo, m0, m1 = _head_masks()
    wq = 2 * LANES if wide_qk else LANES
    for p in range(pairs):
        q = q_ref[:, p * wq:(p + 1) * wq]
        ks = [kv[2 * t][:, p * wq:(p + 1) * wq].astype(BF16) for t in range(n_src)]
        vs = [kv[2 * t + 1][:, p * LANES:(p + 1) * LANES].astype(BF16) for t in range(n_src)]
        outs = []
        for j in range(2):
            if wide_qk:
                s_list = [_qk(q[:, j * LANES:(j + 1) * LANES], k[:, j * LANES:(j + 1) * LANES]) for k in ks]
            else:
                s_list = [_qk(q * (m0 if j == 0 else m1), k) for k in ks]
            outs.append(_softmax_pv(s_list, vs))
        o_ref[:, p * LANES:(p + 1) * LANES] = jnp.where(lo, outs[0], outs[1]).astype(o_ref.dtype)


def _attention(q, kv_srcs, n_batch, tq, wide_qk, pairs):
    wq = (2 * LANES if wide_qk else LANES) * pairs
    wv = LANES * pairs
    t_q = q.shape[0] // n_batch
    nq = t_q // tq
    n_steps = D_MODEL // wv
    in_specs = [pl.BlockSpec((tq, wq), lambda b, p, i: (b * nq + i, p))]
    args = [q]
    for k, v in kv_srcs:
        t_k = k.shape[0] // n_batch
        in_specs.append(pl.BlockSpec((t_k, wq), lambda b, p, i: (b, p)))
        in_specs.append(pl.BlockSpec((t_k, wv), lambda b, p, i: (b, p)))
        args += [k, v]
    return pl.pallas_call(
        functools.partial(_attn_kernel, n_src=len(kv_srcs), wide_qk=wide_qk, pairs=pairs),
        grid=(n_batch, n_steps, nq),
        in_specs=in_specs,
        out_specs=pl.BlockSpec((tq, wv), lambda b, p, i: (b * nq + i, p)),
        out_shape=jax.ShapeDtypeStruct((q.shape[0], D_MODEL), BF16),
        compiler_params=_cparams(("arbitrary", "arbitrary", "arbitrary")),
        name="attention",
    )(*args)


def _col_max(s, chunk=64):
    acc = s[:chunk]
    for r in range(chunk, s.shape[0], chunk):
        acc = jnp.maximum(acc, s[r:r + chunk])
    return acc.max(axis=0, keepdims=True)


def _attn_t_kernel(q_ref, k1_ref, k2_ref, v1_ref, v2_ref, o_ref):
    n_heads = q_ref.shape[1] // LANES
    slots = [slice(j * LANES, (j + 1) * LANES) for j in range(n_heads)]
    def scores(sl):
        return _qk(k1_ref[:, sl], q_ref[:, sl]), _qk(k2_ref[:, sl], q_ref[:, sl])

    nxt = scores(slots[0])
    outs = []
    for n, sl in enumerate(slots):
        s1, s2 = nxt
        if n + 1 < n_heads:
            nxt = scores(slots[n + 1])
        m = jnp.maximum(_col_max(s1), _col_max(s2))
        p1 = jnp.exp2(s1 - m).astype(BF16)
        p2 = jnp.exp2(s2 - m).astype(BF16)
        ot = (jnp.dot(v1_ref[sl, :], p1, preferred_element_type=F32)
              + jnp.dot(v2_ref[sl, :], p2, preferred_element_type=F32))
        outs.append(ot[:MLA_V_DIM] * (1.0 / ot[MLA_V_DIM:MLA_V_DIM + 1]))
    for j in range(0, n_heads, 2):
        pair = jnp.concatenate(outs[j:j + 2], axis=0).T
        o_ref[:, (j // 2) * LANES:(j // 2 + 1) * LANES] = pair.astype(o_ref.dtype)


def _attention_t(q, k1, v1t, k2, v2t, n_batch, tq, heads):
    wq = heads * LANES
    wo = heads * MLA_V_DIM
    nq = q.shape[0] // n_batch // tq
    t1 = k1.shape[0] // n_batch
    t2 = k2.shape[0] // n_batch
    return pl.pallas_call(
        _attn_t_kernel,
        grid=(n_batch, MLA_HEADS // heads, nq),
        in_specs=[pl.BlockSpec((tq, wq), lambda b, p, i: (b * nq + i, p)),
                  pl.BlockSpec((t1, wq), lambda b, p, i: (b, p)),
                  pl.BlockSpec((t2, wq), lambda b, p, i: (b, p)),
                  pl.BlockSpec((wq, t1), lambda b, p, i: (p, b)),
                  pl.BlockSpec((wq, t2), lambda b, p, i: (p, b))],
        out_specs=pl.BlockSpec((tq, wo), lambda b, p, i: (b * nq + i, p)),
        out_shape=jax.ShapeDtypeStruct((q.shape[0], D_MODEL), BF16),
        compiler_params=_cparams(("arbitrary", "arbitrary", "arbitrary")),
        name="attention_t",
    )(q, k1, k2, v1t, v2t)


def _na_group_layout(rows):
    n_groups = rows // NA_GROUP_ROWS
    kh = min(NA_KH, rows)
    bases, types, keys = [], [], {}
    for gi in range(n_groups):
        r0 = gi * NA_GROUP_ROWS
        base = int(np.clip(r0 - kh // 2, 0, rows - NA_WIN_ROWS))
        starts = tuple(int(np.clip(r - kh // 2, 0, rows - kh)) - base for r in range(r0, r0 + NA_GROUP_ROWS))
        key = (starts, r0 - base)
        if key not in keys:
            keys[key] = len(keys)
        bases.append(base)
        types.append(keys[key])
    return bases, types, list(keys.keys())


def _na_bias_tables(rpb, rows):
    kh = min(NA_KH, rows)
    _, _, type_keys = _na_group_layout(rows)
    qc = np.arange(GRID_W)
    kc = np.arange(GRID_W)
    col_start = np.clip(qc - NA_KW // 2, 0, GRID_W - NA_KW)
    col_valid = (kc[None, :] >= col_start[:, None]) & (kc[None, :] < col_start[:, None] + NA_KW)
    dx = kc[None, :] - qc[:, None] + (NA_KW - 1)
    onehot = ((dx[None] == np.arange(2 * NA_KW - 1)[:, None, None]) & col_valid[None]).astype(np.float32)
    band = jnp.einsum('hyd,dqk->hykq', rpb.astype(F32), onehot, precision=lax.Precision.HIGHEST)
    band = jnp.where(col_valid.T, band, NEG_INF)
    strip = band.reshape(NA_HEADS, (2 * NA_KH - 1) * GRID_W, GRID_W)
    tables = []
    for starts, r_off in type_keys:
        q_cols = []
        for rq in range(NA_GROUP_ROWS):
            dy0 = starts[rq] - (rq + r_off) + (NA_KH - 1)
            piece = strip[:, dy0 * GRID_W:(dy0 + kh) * GRID_W, :]
            pad = ((0, 0), (starts[rq] * GRID_W, (NA_WIN_ROWS - kh - starts[rq]) * GRID_W), (0, 0))
            q_cols.append(jnp.pad(piece, pad, constant_values=NEG_INF))
        tables.append(jnp.concatenate(q_cols, axis=2))
    return jnp.stack(tables, axis=1) * LOG2_E


def _state_k_kernel(x_ref, o_ref):
    for h in range(NA_HEADS):
        o_ref[:, h * NA_HEAD_DIM:(h + 1) * NA_HEAD_DIM] = x_ref[0, 0, :, h, :].astype(o_ref.dtype)


def _state_vt_kernel(x_ref, o_ref, flat_ref):
    for h in range(NA_HEADS):
        flat_ref[:, h * NA_HEAD_DIM:(h + 1) * NA_HEAD_DIM] = x_ref[0, 0, :, h, :]
    xt = flat_ref[...].T
    t = xt.shape[1]
    tail = jnp.where(lax.broadcasted_iota(jnp.int32, (LANES - NA_HEAD_DIM, t), 0) == 0, 1.0, 0.0)
    for h in range(NA_HEADS):
        o_ref[h * LANES:h * LANES + NA_HEAD_DIM, :] = xt[h * NA_HEAD_DIM:(h + 1) * NA_HEAD_DIM, :].astype(o_ref.dtype)
        o_ref[h * LANES + NA_HEAD_DIM:(h + 1) * LANES, :] = tail.astype(o_ref.dtype)


def _state_k(x, layer):
    b, _, t, h, dh = x.shape
    return pl.pallas_call(
        _state_k_kernel,
        grid=(b,),
        in_specs=[pl.BlockSpec((1, 1, t, h, dh), lambda i: (i, layer, 0, 0, 0))],
        out_specs=pl.BlockSpec((t, h * dh), lambda i: (i, 0)),
        out_shape=jax.ShapeDtypeStruct((b * t, h * dh), BF16),
        compiler_params=_cparams(("arbitrary",)),
        name="state_k",
    )(x)


def _state_vt(x, layer):
    b, _, t, h, dh = x.shape
    return pl.pallas_call(
        _state_vt_kernel,
        grid=(b,),
        in_specs=[pl.BlockSpec((1, 1, t, h, dh), lambda i: (i, layer, 0, 0, 0))],
        out_specs=pl.BlockSpec((h * LANES, t), lambda i: (0, i)),
        out_shape=jax.ShapeDtypeStruct((h * LANES, b * t), BF16),
        scratch_shapes=[pltpu.VMEM((t, h * dh), F32)],
        compiler_params=_cparams(("arbitrary",)),
        name="state_vt",
    )(x)


def _na_latent_kernel(q_ref, k_ref, vt_ref, kc_ref, vct_ref, tab_ref, o_ref, *, bases, types):
    _, m0, m1 = _head_masks()
    kctx = kc_ref[...]
    gq = NA_GROUP_ROWS * GRID_W
    gk = NA_WIN_ROWS * GRID_W
    blocks = [(gi, base, ty, j) for gi, (base, ty) in enumerate(zip(bases, types)) for j in range(2)]

    def scores(gi, base, ty, j):
        qj = q_ref[gi * gq:(gi + 1) * gq, :] * (m0 if j == 0 else m1)
        kwin = k_ref[base * GRID_W:base * GRID_W + gk, :]
        return _qk(kwin, qj) + tab_ref[j, ty], _qk(kctx, qj)

    nxt = scores(*blocks[0])
    outs = []
    for n, (gi, base, ty, j) in enumerate(blocks):
        s_win, s_ctx = nxt
        if n + 1 < len(blocks):
            nxt = scores(*blocks[n + 1])
        m = jnp.maximum(_col_max(s_win), _col_max(s_ctx))
        p_win = jnp.exp2(s_win - m).astype(BF16)
        p_ctx = jnp.exp2(s_ctx - m).astype(BF16)
        sl = slice(j * LANES, (j + 1) * LANES)
        ot = (jnp.dot(vt_ref[sl, base * GRID_W:base * GRID_W + gk], p_win, preferred_element_type=F32)
              + jnp.dot(vct_ref[sl, :], p_ctx, preferred_element_type=F32))
        outs.append(ot[:NA_HEAD_DIM] * (1.0 / ot[NA_HEAD_DIM:NA_HEAD_DIM + 1]))
        if j == 1:
            o_ref[gi * gq:(gi + 1) * gq, :] = jnp.concatenate(outs, axis=0).T.astype(o_ref.dtype)
            outs = []


def _na_latent(q, k, vt, k_ctx, vt_ctx, tables, n_batch):
    t = q.shape[0] // n_batch
    rows = t // GRID_W
    t_ctx = k_ctx.shape[0] // n_batch
    bases, types, type_keys = _na_group_layout(rows)
    n_types = len(type_keys)
    gq = NA_GROUP_ROWS * GRID_W
    gk = NA_WIN_ROWS * GRID_W
    tok = pl.BlockSpec((t, LANES), lambda p, b: (b, p))
    ctx = pl.BlockSpec((t_ctx, LANES), lambda p, b: (b, p))
    return pl.pallas_call(
        functools.partial(_na_latent_kernel, bases=bases, types=types),
        grid=(D_MODEL // LANES, n_batch),
        in_specs=[tok, tok, pl.BlockSpec((2 * LANES, t), lambda p, b: (p, b)),
                  ctx, pl.BlockSpec((2 * LANES, t_ctx), lambda p, b: (p, b)),
                  pl.BlockSpec((2, n_types, gk, gq), lambda p, b: (p, 0, 0, 0))],
        out_specs=tok,
        out_shape=jax.ShapeDtypeStruct(q.shape, BF16),
        compiler_params=_cparams(("arbitrary", "arbitrary")),
        name="na_latent",
    )(q, k, vt, k_ctx, vt_ctx, tables)


def _rope_lane_tables(n_tokens):
    n_freq = MLA_ROPE_DIM // 4
    inv = 1.0 / (ROPE_THETA ** (jnp.arange(n_freq, dtype=F32) / n_freq))
    tt = jnp.arange(n_tokens)
    row = (tt // GRID_W).astype(F32)
    col = (tt % GRID_W).astype(F32)
    ang = jnp.concatenate([row[:, None] * inv, col[:, None] * inv], axis=-1)
    cos, sin = jnp.cos(ang), jnp.sin(ang)
    ones = jnp.ones((n_tokens, MLA_NOPE_DIM), F32)
    z16 = jnp.zeros((n_tokens, MLA_ROPE_DIM // 2), F32)
    z32 = jnp.zeros((n_tokens, LANES - MLA_QK_DIM), F32)
    z64 = jnp.zeros((n_tokens, MLA_NOPE_DIM), F32)
    c_tab = jnp.concatenate([ones, cos, cos, z32], axis=-1)
    s_dn = jnp.concatenate([z64, -sin, z16, z32], axis=-1)
    s_up = jnp.concatenate([z64, z16, sin, z32], axis=-1)
    return c_tab, s_dn, s_up


def _rope_slot(x, c_tab, s_dn, s_up):
    half = MLA_ROPE_DIM // 2
    return x * c_tab + pltpu.roll(x, LANES - half, 1) * s_dn + pltpu.roll(x, half, 1) * s_up


def _mla_kv(ckv_n, kr_slot, wuk_ref, wuv_ref, k_ref, v_ref, v_transposed):
    c = ckv_n.astype(BF16)
    k_nope = jnp.dot(c, wuk_ref[...], preferred_element_type=F32)
    for h in range(MLA_HEADS):
        k_ref[:, h * LANES:(h + 1) * LANES] = (k_nope[:, h * LANES:(h + 1) * LANES] + kr_slot).astype(k_ref.dtype)
    if v_transposed:
        vt = _qk(wuv_ref[...], c)
        row = lax.broadcasted_iota(jnp.int32, vt.shape, 0)
        v_ref[...] = jnp.where((row & (LANES - 1)) == MLA_V_DIM, 1.0, vt).astype(v_ref.dtype)
    else:
        v_ref[...] = jnp.dot(c, wuv_ref[...], preferred_element_type=F32).astype(v_ref.dtype)


def _mla_proj_kernel(*refs, rope, emit_state, q_scale):
    (x_ref, g_ref, sh_ref, sc_ref, win_ref, qg_ref, kvg_ref, wuq_ref, wuk_ref, wuv_ref) = refs[:10]
    pos = 10
    if rope:
        ct_ref, sd_ref, su_ref = refs[pos:pos + 3]
        pos += 3
    q_ref, k_ref, v_ref = refs[pos:pos + 3]
    pos += 3
    h = _norm_mod(x_ref[...], g_ref[...], sh_ref[0], sc_ref[0]).astype(BF16)
    c = jnp.dot(h, win_ref[...], preferred_element_type=F32)
    cq = _rms(c[:, :MLA_Q_RANK], qg_ref[...]).astype(BF16)
    ckv = _rms(c[:, MLA_Q_RANK:MLA_Q_RANK + MLA_KV_RANK], kvg_ref[...])
    kr_chunk = c[:, MLA_Q_RANK + MLA_KV_RANK:]
    if emit_state:
        ckv_out, kr_out = refs[pos:pos + 2]
        ckv_out[...] = ckv
        kr_out[...] = kr_chunk[:, :MLA_ROPE_DIM]
    kr_slot = pltpu.roll(kr_chunk, MLA_NOPE_DIM, 1)
    q = jnp.dot(cq, wuq_ref[...], preferred_element_type=F32) * q_scale
    if rope:
        ct, sd, su = ct_ref[...], sd_ref[...], su_ref[...]
        kr_slot = _rope_slot(kr_slot, ct, sd, su)
    for hh in range(MLA_HEADS):
        qh = q[:, hh * LANES:(hh + 1) * LANES]
        if rope:
            qh = _rope_slot(qh, ct, sd, su)
        q_ref[:, hh * LANES:(hh + 1) * LANES] = qh.astype(q_ref.dtype)
    _mla_kv(ckv, kr_slot, wuk_ref, wuv_ref, k_ref, v_ref, v_transposed=rope)


def _mla_proj(x, g, mod, layer, row0, tokens_per_row, w, rope_tabs, emit_state):
    t = x.shape[0]
    win, qg, kvg, wuq, wuk, wuv = w
    tok = lambda n: pl.BlockSpec((TM, n), lambda i: (i, 0))
    wq = MLA_HEADS * LANES
    in_specs = [tok(D_MODEL), _const_spec((1, D_MODEL)),
                _mod_spec(layer, 0, row0, tokens_per_row), _mod_spec(layer, 1, row0, tokens_per_row),
                _const_spec(win.shape), _const_spec(qg.shape), _const_spec(kvg.shape),
                _const_spec(wuq.shape), _const_spec(wuk.shape), _const_spec(wuv.shape)]
    args = [x, g, mod, mod, win, qg, kvg, wuq, wuk, wuv]
    rope = rope_tabs is not None
    if rope:
        n_tab = rope_tabs[0].shape[0] // TM
        in_specs += [pl.BlockSpec((TM, LANES), lambda i: (i % n_tab, 0)) for _ in range(3)]
        args += list(rope_tabs)
    if rope:
        v_spec, v_shape = pl.BlockSpec((wq, TM), lambda i: (0, i)), jax.ShapeDtypeStruct((wq, t), BF16)
        q_scale = MLA_QK_DIM ** -0.5 * LOG2_E
    else:
        v_spec, v_shape = tok(D_MODEL), jax.ShapeDtypeStruct((t, D_MODEL), BF16)
        q_scale = MLA_QK_DIM ** -0.5
    out_specs = [tok(wq), tok(wq), v_spec]
    out_shape = [jax.ShapeDtypeStruct((t, wq), BF16), jax.ShapeDtypeStruct((t, wq), BF16), v_shape]
    if emit_state:
        out_specs += [tok(MLA_KV_RANK), tok(MLA_ROPE_DIM)]
        out_shape += [jax.ShapeDtypeStruct((t, MLA_KV_RANK), F32), jax.ShapeDtypeStruct((t, MLA_ROPE_DIM), F32)]
    return pl.pallas_call(
        functools.partial(_mla_proj_kernel, rope=rope, emit_state=emit_state, q_scale=q_scale),
        grid=(t // TM,),
        in_specs=in_specs, out_specs=out_specs, out_shape=out_shape,
        compiler_params=_cparams(("arbitrary",)),
        name="mla_proj",
    )(*args)


def _mla_state_kv_kernel(ckv_ref, kr_ref, wuk_ref, wuv_ref, k_ref, v_ref):
    kr_slot = pltpu.roll(kr_ref[...], MLA_NOPE_DIM, 1)
    _mla_kv(ckv_ref[...], kr_slot, wuk_ref, wuv_ref, k_ref, v_ref, v_transposed=True)


def _mla_state_kv(ckv, kr_pad, wuk, wuv_t):
    t = ckv.shape[0]
    wq = MLA_HEADS * LANES
    tok = lambda n: pl.BlockSpec((TM, n), lambda i: (i, 0))
    return pl.pallas_call(
        _mla_state_kv_kernel,
        grid=(t // TM,),
        in_specs=[tok(MLA_KV_RANK), tok(LANES), _const_spec(wuk.shape), _const_spec(wuv_t.shape)],
        out_specs=[tok(wq), pl.BlockSpec((wq, TM), lambda i: (0, i))],
        out_shape=[jax.ShapeDtypeStruct((t, wq), BF16), jax.ShapeDtypeStruct((wq, t), BF16)],
        compiler_params=_cparams(("arbitrary",)),
        name="mla_state_kv",
    )(ckv, kr_pad, wuk, wuv_t)


def _first_argmax_rows(v, row):
    top = v.max(axis=0, keepdims=True)
    idx = jnp.where(v == top, row, float(v.shape[0])).min(axis=0, keepdims=True)
    return top, idx


ROUTER_ROWS = 32
ROUTER_EXPERT_ROW = 8


def _route_t(lt):
    n = lt.shape[1]
    row8 = lax.broadcasted_iota(jnp.int32, (8, n), 0).astype(F32)
    row16 = lax.broadcasted_iota(jnp.int32, (N_EXPERTS, n), 0).astype(F32)
    is_g = row8 < N_GROUPS
    gl = jnp.where(is_g, lt[:8], -jnp.inf)
    gmax, gsel = _first_argmax_rows(gl, row8)
    p_group = 1.0 / jnp.where(is_g, jnp.exp(gl - gmax), 0.0).sum(axis=0, keepdims=True)
    first = gsel * EXPERTS_PER_GROUP
    e_all = lt[ROUTER_EXPERT_ROW:ROUTER_EXPERT_ROW + N_EXPERTS]
    el = jnp.where((row16 >= first) & (row16 < first + EXPERTS_PER_GROUP), e_all, -jnp.inf)
    t1, i1 = _first_argmax_rows(el, row16)
    t2, i2 = _first_argmax_rows(jnp.where(row16 == i1, -jnp.inf, el), row16)
    e2 = jnp.exp(t2 - t1)
    w1 = 1.0 / (1.0 + e2)
    gates = jnp.where(row16 == i1, w1 * p_group, jnp.where(row16 == i2, e2 * w1 * p_group, 0.0))
    rel = gates[:8] + gates[8:]
    rel = rel + pltpu.roll(rel, EXPERTS_PER_GROUP, 0)
    meta_t = jnp.where(row8 < EXPERTS_PER_GROUP, rel, jnp.where(row8 == META_GROUP_LANE, gsel, 0.0))
    onehot_t = jnp.where(row8 == gsel, 1.0, 0.0)
    return meta_t, onehot_t, row8, gsel


def _pack_halves(x):
    half = x.shape[1] // 2
    lo = lax.bitcast_convert_type(x[:, :half].astype(BF16).astype(F32), jnp.uint32)
    hi = lax.bitcast_convert_type(x[:, half:].astype(BF16).astype(F32), jnp.uint32)
    return (lo >> 16) | hi


def _unpack_halves(p):
    lo = lax.bitcast_convert_type(p << 16, F32)
    hi = lax.bitcast_convert_type(p & jnp.uint32(0xFFFF0000), F32)
    return lo, hi


ROW_WORDS = D_MODEL // 2 + LANES
META_GROUP_LANE = EXPERTS_PER_GROUP
META_RANK_LANE = EXPERTS_PER_GROUP + 1


def _proj_res_kernel(o_ref, x_ref, w_ref, gate_ref, g_ref, sh_ref, sc_ref, wrt_ref, brt_ref, tri_ref,
                     xn_ref, hpk_ref, meta_ref, cnt_ref):
    @pl.when(pl.program_id(0) == 0)
    def _():
        cnt_ref[...] = jnp.zeros(cnt_ref.shape, cnt_ref.dtype)

    sub = tri_ref.shape[0]
    cnt = cnt_ref[:, :1]
    for r0 in range(0, x_ref.shape[0], sub):
        rows = slice(r0, r0 + sub)
        y = jnp.dot(o_ref[rows, :], w_ref[...], preferred_element_type=F32)
        xn = x_ref[rows, :] + gate_ref[0] * y
        xn_ref[rows, :] = xn
        h = _norm_mod(xn, g_ref[...], sh_ref[0], sc_ref[0])
        lt = _qk(wrt_ref[...], h.astype(BF16)) + brt_ref[...]
        meta_t, onehot_t, row8, gsel = _route_t(lt)
        prefix = jnp.dot(onehot_t.astype(BF16), tri_ref[...], preferred_element_type=F32)
        rank = jnp.where(row8 == gsel, prefix + cnt - 1.0, 0.0).sum(axis=0, keepdims=True)
        cnt = cnt + onehot_t.sum(axis=1, keepdims=True)
        meta_t = jnp.where(row8 == META_RANK_LANE, rank, meta_t)
        meta = jnp.concatenate([meta_t, jnp.zeros((LANES - 8, sub), F32)], axis=0).T
        meta_ref[rows, :] = meta
        hpk_ref[rows, :] = jnp.concatenate([_pack_halves(h), lax.bitcast_convert_type(meta, jnp.uint32)], axis=1)
    cnt_ref[...] = jnp.broadcast_to(cnt, cnt_ref.shape)


def _proj_res(o, x, w_o, g_ffn, mod, layer, row0, tokens_per_row, w_router_t, b_router_t):
    t = x.shape[0]
    tm = TM_PROJ
    tok = lambda n: pl.BlockSpec((tm, n), lambda i: (i, 0))
    ms = lambda k: _mod_spec(layer, k, row0, tokens_per_row, tm)
    tri = jnp.asarray(np.triu(np.ones((TM, TM), np.float32)), BF16)
    return pl.pallas_call(
        _proj_res_kernel,
        grid=(t // tm,),
        in_specs=[tok(D_MODEL), tok(D_MODEL), _const_spec(w_o.shape), ms(2), _const_spec((1, D_MODEL)), ms(3), ms(4),
                  _const_spec(w_router_t.shape), _const_spec(b_router_t.shape), _const_spec((TM, TM))],
        out_specs=[tok(D_MODEL), tok(ROW_WORDS), tok(LANES), _const_spec((8, LANES))],
        out_shape=[jax.ShapeDtypeStruct((t, D_MODEL), F32), jax.ShapeDtypeStruct((t, ROW_WORDS), jnp.uint32),
                   jax.ShapeDtypeStruct((t, LANES), F32), jax.ShapeDtypeStruct((8, LANES), F32)],
        compiler_params=_cparams(("arbitrary",)),
        name="proj_res",
    )(o, x, w_o, mod, g_ffn, mod, mod, w_router_t, b_router_t, tri)


def _moe_plan(meta, cnt):
    t = meta.shape[0]
    n_tiles = t // TM + N_GROUPS
    group = meta[:, META_GROUP_LANE].astype(jnp.int32)
    rank = meta[:, META_RANK_LANE].astype(jnp.int32)
    counts = cnt[:N_GROUPS, 0].astype(jnp.int32)
    tiles = (counts + TM - 1) // TM
    tile_end = jnp.cumsum(tiles)
    tile_start = tile_end - tiles
    pos = rank + TM * jnp.sum(jnp.where(group[:, None] == jnp.arange(N_GROUPS)[None], tile_start[None], 0), axis=1)
    tile_group = jnp.minimum(jnp.sum(jnp.arange(n_tiles)[:, None] >= tile_end[None], axis=1), N_GROUPS - 1)
    pad_bounds = jnp.stack([tile_start * TM + counts, tile_end * TM], axis=1).reshape(-1)
    return (pos.astype(jnp.int32), tile_group.astype(jnp.int32), tile_end[-1:].astype(jnp.int32),
            pad_bounds.astype(jnp.int32))


def _gather_rows(srcs, idx_ref, base, dsts, n_rows):
    for r in range(n_rows):
        s = idx_ref[base + r]
        for src_ref, dst_ref in zip(srcs, dsts):
            dst_ref[r:r + 1, :] = src_ref[pl.ds(s, 1), :]


def _moe_kernel(pos_ref, tg_ref, nt_ref, pad_ref, hpk_ref, wgu_ref, wd_ref, ypk_ref, src_ref, hbuf, hid_ref):
    i = pl.program_id(0)
    n_tok = hpk_ref.shape[0]

    @pl.when(i == 0)
    def _():
        def clear(p, c):
            src_ref[p] = 0
            return c
        for g in range(N_GROUPS):
            lax.fori_loop(pad_ref[2 * g], pad_ref[2 * g + 1], clear, 0)

        def fill(t, c):
            src_ref[pos_ref[t]] = t
            return c
        lax.fori_loop(0, n_tok, fill, 0, unroll=8)
        _gather_rows((hpk_ref,), src_ref, 0, (hbuf.at[0],), TM)

    @pl.when(i < nt_ref[0])
    def _():
        slot = i % 2
        row = hbuf[slot]
        lo, hi = _unpack_halves(row[:, :D_MODEL // 2])
        h = jnp.concatenate([lo.astype(BF16), hi.astype(BF16)], axis=1)
        gates = lax.bitcast_convert_type(row[:, D_MODEL // 2:], F32)
        nxt = jnp.minimum(i + 1, nt_ref[0] - 1) * TM
        _gather_rows((hpk_ref,), src_ref, nxt, (hbuf.at[1 - slot],), TM)
        f = D_FF_EXPERT
        for j in range(EXPERTS_PER_GROUP):
            gu = jnp.dot(h, wgu_ref[0, j], preferred_element_type=F32)
            hid = _silu(gu[:, :f]) * gu[:, f:] * gates[:, j:j + 1]
            hid_ref[:, j * f:(j + 1) * f] = hid.astype(BF16)
        ypk_ref[...] = _pack_halves(jnp.dot(hid_ref[...], wd_ref[0], preferred_element_type=F32))

    @pl.when(i >= nt_ref[0])
    def _():
        ypk_ref[...] = jnp.zeros(ypk_ref.shape, ypk_ref.dtype)


def _moe(hpk, pos, tile_group, n_used, pad_bounds, wgu, wd):
    t = hpk.shape[0]
    n_tiles = t // TM + N_GROUPS
    f = D_FF_EXPERT
    grid_spec = pltpu.PrefetchScalarGridSpec(
        num_scalar_prefetch=4,
        grid=(n_tiles,),
        in_specs=[
            pl.BlockSpec(hpk.shape, lambda i, *_: (0, 0), pipeline_mode=pl.Buffered(1)),
            pl.BlockSpec((1, EXPERTS_PER_GROUP, D_MODEL, 2 * f), lambda i, pos, tg, nt, pad: (tg[i], 0, 0, 0)),
            pl.BlockSpec((1, EXPERTS_PER_GROUP * f, D_MODEL), lambda i, pos, tg, nt, pad: (tg[i], 0, 0)),
        ],
        out_specs=pl.BlockSpec((TM, D_MODEL // 2), lambda i, *_: (i, 0)),
        scratch_shapes=[pltpu.SMEM((n_tiles * TM,), jnp.int32),
                        pltpu.VMEM((2, TM, ROW_WORDS), jnp.uint32),
                        pltpu.VMEM((TM, EXPERTS_PER_GROUP * f), BF16)],
    )
    return pl.pallas_call(
        _moe_kernel,
        grid_spec=grid_spec,
        out_shape=jax.ShapeDtypeStruct((n_tiles * TM, D_MODEL // 2), jnp.uint32),
        compiler_params=_cparams(("arbitrary",)),
        name="moe",
    )(pos, tile_group, n_used, pad_bounds, hpk, wgu, wd)


def _combine_kernel(*refs, final_norm):
    pos_ref, ypk_ref, x_ref, gate_ref = refs[:4]
    o_ref, ybuf = refs[-2], refs[-1]
    _gather_rows((ypk_ref,), pos_ref, pl.program_id(0) * TM, (ybuf,), TM)
    lo, hi = _unpack_halves(ybuf[...])
    out = x_ref[...] + gate_ref[0] * jnp.concatenate([lo, hi], axis=1)
    if final_norm:
        out = _rms(out, refs[4][...])
    o_ref[...] = out


def _combine(ypk, pos, x, mod, layer, row0, tokens_per_row, final_g):
    t = x.shape[0]

    def mod_map(i, pos):
        row = row0 + (i * TM) // tokens_per_row
        return ((layer * MOD_ROWS + row) * N_MOD + 5, 0, 0)

    in_specs = [pl.BlockSpec(ypk.shape, lambda i, pos: (0, 0), pipeline_mode=pl.Buffered(1)),
                pl.BlockSpec((TM, D_MODEL), lambda i, pos: (i, 0)),
                pl.BlockSpec((1, 1, D_MODEL), mod_map)]
    args = [ypk, x, mod]
    if final_g is not None:
        in_specs.append(pl.BlockSpec((1, D_MODEL), lambda i, pos: (0, 0)))
        args.append(final_g)
    grid_spec = pltpu.PrefetchScalarGridSpec(
        num_scalar_prefetch=1,
        grid=(t // TM,),
        in_specs=in_specs,
        out_specs=pl.BlockSpec((TM, D_MODEL), lambda i, pos: (i, 0)),
        scratch_shapes=[pltpu.VMEM((TM, D_MODEL // 2), jnp.uint32)],
    )
    return pl.pallas_call(
        functools.partial(_combine_kernel, final_norm=final_g is not None),
        grid_spec=grid_spec,
        out_shape=jax.ShapeDtypeStruct((t, D_MODEL), F32),
        compiler_params=_cparams(("arbitrary",)),
        name="moe_combine",
    )(pos, *args)


def _pad_heads(w, used):
    k = w.shape[0]
    w = w.reshape(k, MLA_HEADS, used)
    return jnp.pad(w, ((0, 0), (0, 0), (0, LANES - used))).reshape(k, MLA_HEADS * LANES)


def kernel(x_prompt, x_sample, state_na_k, state_na_v, state_mla_ckv, state_mla_kr, c, c_ctx, ada_w, ada_b, norm_mix_g, norm_ffn_g, final_norm_g, na_w_qkv, na_w_o, na_rpb, mla_w_in, mla_q_norm_g, mla_kv_norm_g, mla_w_uq, mla_w_uk, mla_w_uv, mla_w_o, moe_w_group, moe_b_group, moe_w_expert, moe_b_expert, moe_w_gate, moe_w_up, moe_w_down):
    n_b, seq, d = x_prompt.shape
    dec_b, n_lat, _ = x_sample.shape
    past = state_na_k.shape[2]
    rows = n_lat // GRID_W
    assert d == D_MODEL and dec_b + 1 <= MOD_ROWS and n_lat % TM_PROJ == 0 and (n_b * seq) % TM_PROJ == 0

    cond8 = jnp.concatenate([c_ctx[None], c, jnp.zeros((MOD_ROWS - 1 - dec_b, d), F32)], axis=0)
    mod = _ada_mod(cond8, ada_w, ada_b)

    xp = x_prompt.reshape(n_b * seq, d)
    xs = x_sample.reshape(dec_b * n_lat, d)
    p_rows = (0, n_b * seq)
    s_rows = (1, n_lat)
    outs = {}

    for i in range(DEPTH):
        j = i // 2
        g_mix = norm_mix_g[i][None]
        g_ffn = norm_ffn_g[i][None]
        if i % 2 == 0:
            w_qkv = na_w_qkv[j].astype(BF16)
            w_vt = jnp.pad(na_w_qkv[j][:, 2 * d:].T.reshape(NA_HEADS, NA_HEAD_DIM, d),
                           ((0, 0), (0, LANES - NA_HEAD_DIM), (0, 0))).reshape(NA_HEADS * LANES, d).astype(BF16)
            qp, kp, vp, outs['na_k'], outs['na_v'] = _na_qkv(xp, g_mix, mod, i, *p_rows, w_qkv, n_b)
            qs, ks, vs_t = _na_qkv_t(xs, g_mix, mod, i, *s_rows, w_qkv[:, :2 * d], w_vt)
            op = _attention(qp, [(kp, vp)], n_b, seq, wide_qk=False, pairs=8)
            tables = _na_bias_tables(na_rpb[j], rows)
            os_ = _na_latent(qs, ks, vs_t, _state_k(state_na_k, j), _state_vt(state_na_v, j), tables, dec_b)
            w_o = na_w_o[j].astype(BF16)
        else:
            pad_in = MLA_IN_PAD - mla_w_in.shape[-1]
            w_mla = (jnp.pad(mla_w_in[j], ((0, 0), (0, pad_in))).astype(BF16),
                     mla_q_norm_g[j][None], mla_kv_norm_g[j][None],
                     _pad_heads(mla_w_uq[j], MLA_QK_DIM).astype(BF16),
                     _pad_heads(mla_w_uk[j], MLA_NOPE_DIM).astype(BF16),
                     mla_w_uv[j].astype(BF16))
            wuv_t = _pad_heads(mla_w_uv[j], MLA_V_DIM).T.astype(BF16)
            w_mla_t = w_mla[:5] + (wuv_t,)
            qp, kp, vp, ckv_p, kr_p = _mla_proj(xp, g_mix, mod, i, *p_rows, w_mla, None, True)
            qs, ks, vs_t = _mla_proj(xs, g_mix, mod, i, *s_rows, w_mla_t, _rope_lane_tables(n_lat), False)
            outs['ckv'] = ckv_p.reshape(n_b, 1, seq, MLA_KV_RANK)
            outs['kr'] = kr_p.reshape(n_b, 1, seq, MLA_ROPE_DIM)
            kr_state = jnp.pad(state_mla_kr[:, j].reshape(dec_b * past, MLA_ROPE_DIM),
                               ((0, 0), (0, LANES - MLA_ROPE_DIM)))
            kc, vc_t = _mla_state_kv(state_mla_ckv[:, j].reshape(dec_b * past, MLA_KV_RANK), kr_state,
                                     w_mla[4], wuv_t)
            op = _attention(qp, [(kp, vp)], n_b, seq, wide_qk=True, pairs=8)
            os_ = _attention_t(qs, ks, vs_t, kc, vc_t, dec_b, 512, heads=4)
            w_o = mla_w_o[j].astype(BF16)

        pad_g = ((0, ROUTER_EXPERT_ROW - N_GROUPS), (0, 0))
        pad_e = ((0, ROUTER_ROWS - ROUTER_EXPERT_ROW - N_EXPERTS), (0, 0))
        w_router_t = jnp.concatenate([jnp.pad(moe_w_group[i].T, pad_g), jnp.pad(moe_w_expert[i].T, pad_e)],
                                     axis=0).astype(BF16)
        b_router_t = jnp.concatenate([jnp.pad(moe_b_group[i][:, None], pad_g),
                                      jnp.pad(moe_b_expert[i][:, None], pad_e)], axis=0)
        b_router_t = jnp.broadcast_to(b_router_t, (ROUTER_ROWS, TM))
        f = D_FF_EXPERT
        wgu = jnp.concatenate([moe_w_gate[i], moe_w_up[i]], axis=-1).astype(BF16)
        wgu = wgu.reshape(N_GROUPS, EXPERTS_PER_GROUP, d, 2 * f)
        wd = moe_w_down[i].astype(BF16).reshape(N_GROUPS, EXPERTS_PER_GROUP * f, d)
        final_g = final_norm_g[None] if i == DEPTH - 1 else None
        new_x = []
        for o_att, x, rows_ in ((op, xp, p_rows), (os_, xs, s_rows)):
            xn, hpk, meta, cnt = _proj_res(o_att, x, w_o, g_ffn, mod, i, *rows_, w_router_t, b_router_t)
            pos, tile_group, n_used, pad_bounds = _moe_plan(meta, cnt)
            ypk = _moe(hpk, pos, tile_group, n_used, pad_bounds, wgu, wd)
            new_x.append(_combine(ypk, pos, xn, mod, i, *rows_, final_g))
        xp, xs = new_x

    return (xp.reshape(n_b, seq, d), xs.reshape(dec_b, n_lat, d),
            outs['na_k'], outs['na_v'], outs['ckv'], outs['kr'])
```

```python
import functools

import numpy as np
import jax
import jax.numpy as jnp
from jax import lax
from jax.experimental import pallas as pl
from jax.experimental.pallas import tpu as pltpu

F32 = jnp.float32
BF16 = jnp.bfloat16

D_MODEL = 1024
DEPTH = 2
GRID_W = 64
N_MOD = 6
EPS = 1e-6
NEG_INF = -1e30
NA_HEADS = 16
NA_HEAD_DIM = 64
NA_KH = 8
NA_KW = 16
NA_GROUP_ROWS = 4
NA_WIN_ROWS = 12
MLA_HEADS = 16
MLA_Q_RANK = 384
MLA_KV_RANK = 256
MLA_NOPE_DIM = 64
MLA_ROPE_DIM = 32
MLA_V_DIM = 64
MLA_QK_DIM = MLA_NOPE_DIM + MLA_ROPE_DIM
MLA_IN_PAD = 768
ROPE_THETA = 10000.0
LOG2_E = 1.4426950408889634
N_GROUPS = 4
EXPERTS_PER_GROUP = 4
N_EXPERTS = 16
D_FF_EXPERT = 256

LANES = 128
MOD_ROWS = 8
VMEM_LIMIT = 56 * 1024 * 1024
TM = 512
TM_PROJ = 1024


def _cparams(sem):
    return pltpu.CompilerParams(dimension_semantics=sem, vmem_limit_bytes=VMEM_LIMIT)


def _const_spec(shape):
    nd = len(shape)
    return pl.BlockSpec(shape, lambda *_: (0,) * nd)


def _norm_mod(x, g, shift, scale):
    ms = jnp.mean(x * x, axis=-1, keepdims=True)
    y = x * lax.rsqrt(ms + EPS) * g
    return y * (1.0 + scale) + shift


def _rms(x, g):
    ms = jnp.mean(x * x, axis=-1, keepdims=True)
    return x * lax.rsqrt(ms + EPS) * g


def _silu(x):
    return x / (1.0 + jnp.exp(-x))


def _ada_kernel(cond_ref, w_ref, b_ref, o_ref):
    s = _silu(cond_ref[...]).astype(BF16)
    o_ref[0] = jnp.dot(s, w_ref[0].astype(BF16), preferred_element_type=F32) + b_ref[0]


def _ada_mod(cond8, ada_w, ada_b):
    tn = 1024
    n = N_MOD * D_MODEL
    out = pl.pallas_call(
        _ada_kernel,
        grid=(DEPTH, n // tn),
        in_specs=[
            _const_spec((MOD_ROWS, D_MODEL)),
            pl.BlockSpec((1, D_MODEL, tn), lambda l, j: (l, 0, j)),
            pl.BlockSpec((1, 1, tn), lambda l, j: (l, 0, j)),
        ],
        out_specs=pl.BlockSpec((1, MOD_ROWS, tn), lambda l, j: (l, 0, j)),
        out_shape=jax.ShapeDtypeStruct((DEPTH, MOD_ROWS, n), F32),
        compiler_params=_cparams(("arbitrary", "arbitrary")),
        name="ada_mod",
    )(cond8, ada_w, ada_b.reshape(DEPTH, 1, n))
    return out.reshape(DEPTH * MOD_ROWS * N_MOD, 1, D_MODEL)


def _mod_spec(layer, k, row0, tokens_per_row, tm=TM):
    def imap(i):
        row = row0 + (i * tm) // tokens_per_row
        return ((layer * MOD_ROWS + row) * N_MOD + k, 0, 0)
    return pl.BlockSpec((1, 1, D_MODEL), imap)


def _qkv_kernel(x_ref, g_ref, sh_ref, sc_ref, w_ref, q_ref, k_ref, v_ref, ks_ref, vs_ref):
    h = _norm_mod(x_ref[...], g_ref[...], sh_ref[0], sc_ref[0]).astype(BF16)
    y = jnp.dot(h, w_ref[...], preferred_element_type=F32)
    q_ref[...] = (y[:, :D_MODEL] * (NA_HEAD_DIM ** -0.5)).astype(q_ref.dtype)
    k = y[:, D_MODEL:2 * D_MODEL]
    v = y[:, 2 * D_MODEL:]
    k_ref[...] = k.astype(k_ref.dtype)
    v_ref[...] = v.astype(v_ref.dtype)
    ks_ref[...] = pltpu.einshape("t(hd)->thd", k, h=NA_HEADS).reshape(ks_ref.shape)
    vs_ref[...] = pltpu.einshape("t(hd)->thd", v, h=NA_HEADS).reshape(vs_ref.shape)


def _na_qkv(x, g, mod, layer, row0, tokens_per_row, w_qkv, n_batch):
    t = x.shape[0]
    seq = t // n_batch
    nb = TM // seq
    tok = pl.BlockSpec((TM, D_MODEL), lambda i: (i, 0))
    state = pl.BlockSpec((nb, 1, seq, NA_HEADS, NA_HEAD_DIM), lambda i: (i, 0, 0, 0, 0))
    state_shape = jax.ShapeDtypeStruct((n_batch, 1, seq, NA_HEADS, NA_HEAD_DIM), F32)
    return pl.pallas_call(
        _qkv_kernel,
        grid=(t // TM,),
        in_specs=[tok, _const_spec((1, D_MODEL)),
                  _mod_spec(layer, 0, row0, tokens_per_row), _mod_spec(layer, 1, row0, tokens_per_row),
                  _const_spec((D_MODEL, 3 * D_MODEL))],
        out_specs=[tok, tok, tok, state, state],
        out_shape=[jax.ShapeDtypeStruct((t, D_MODEL), BF16)] * 3 + [state_shape, state_shape],
        compiler_params=_cparams(("arbitrary",)),
        name="na_qkv",
    )(x, g, mod, mod, w_qkv)


def _ones_row_slots(vt):
    row = lax.broadcasted_iota(jnp.int32, vt.shape, 0)
    return jnp.where((row & (LANES - 1)) == NA_HEAD_DIM, 1.0, vt)


def _qkv_t_kernel(x_ref, g_ref, sh_ref, sc_ref, wqk_ref, wvt_ref, q_ref, k_ref, vt_ref):
    h = _norm_mod(x_ref[...], g_ref[...], sh_ref[0], sc_ref[0]).astype(BF16)
    y = jnp.dot(h, wqk_ref[...], preferred_element_type=F32)
    q_ref[...] = (y[:, :D_MODEL] * (NA_HEAD_DIM ** -0.5 * LOG2_E)).astype(q_ref.dtype)
    k_ref[...] = y[:, D_MODEL:].astype(k_ref.dtype)
    vt_ref[...] = _ones_row_slots(_qk(wvt_ref[...], h)).astype(vt_ref.dtype)


def _na_qkv_t(x, g, mod, layer, row0, tokens_per_row, w_qkv, w_vt):
    t = x.shape[0]
    tok = pl.BlockSpec((TM, D_MODEL), lambda i: (i, 0))
    slots = NA_HEADS * LANES
    return pl.pallas_call(
        _qkv_t_kernel,
        grid=(t // TM,),
        in_specs=[tok, _const_spec((1, D_MODEL)),
                  _mod_spec(layer, 0, row0, tokens_per_row), _mod_spec(layer, 1, row0, tokens_per_row),
                  pl.BlockSpec((D_MODEL, 2 * D_MODEL), lambda i: (0, 0)), _const_spec(w_vt.shape)],
        out_specs=[tok, tok, pl.BlockSpec((slots, TM), lambda i: (0, i))],
        out_shape=[jax.ShapeDtypeStruct((t, D_MODEL), BF16), jax.ShapeDtypeStruct((t, D_MODEL), BF16),
                   jax.ShapeDtypeStruct((slots, t), BF16)],
        compiler_params=_cparams(("arbitrary",)),
        name="na_qkv_t",
    )(x, g, mod, mod, w_qkv, w_vt)


def _softmax_pv(s_list, v_list):
    m = s_list[0].max(axis=-1, keepdims=True)
    for s in s_list[1:]:
        m = jnp.maximum(m, s.max(axis=-1, keepdims=True))
    l = None
    acc = None
    for s, v in zip(s_list, v_list):
        p = jnp.exp(s - m)
        ls = p.sum(axis=-1, keepdims=True)
        a = jnp.dot(p.astype(BF16), v, preferred_element_type=F32)
        l = ls if l is None else l + ls
        acc = a if acc is None else acc + a
    return acc * (1.0 / l)


def _qk(q, k):
    return lax.dot_general(q, k, (((1,), (1,)), ((), ())), preferred_element_type=F32)


def _head_masks():
    lane = lax.broadcasted_iota(jnp.int32, (1, LANES), 1)
    lo = lane < (LANES // 2)
    return lo, jnp.where(lo, 1.0, 0.0).astype(BF16), jnp.where(lo, 0.0, 1.0).astype(BF16)


def _attn_kernel(*refs, n_src, wide_qk, pairs):
    q_ref, o_ref = refs[0], refs[-1]
    kv = refs[1:-1]
    lo, m0, m1 = _head_masks()
    wq = 2 * LANES if wide_qk else LANES
    for p in range(pairs):
        q = q_ref[:, p * wq:(p + 1) * wq]
        ks = [kv[2 * t][:, p * wq:(p + 1) * wq].astype(BF16) for t in range(n_src)]
        vs = [kv[2 * t + 1][:, p * LANES:(p + 1) * LANES].astype(BF16) for t in range(n_src)]
        outs = []
        for j in range(2):
            if wide_qk:
                s_list = [_qk(q[:, j * LANES:(j + 1) * LANES], k[:, j * LANES:(j + 1) * LANES]) for k in ks]
            else:
                s_list = [_qk(q * (m0 if j == 0 else m1), k) for k in ks]
            outs.append(_softmax_pv(s_list, vs))
        o_ref[:, p * LANES:(p + 1) * LANES] = jnp.where(lo, outs[0], outs[1]).astype(o_ref.dtype)


def _attention(q, kv_srcs, n_batch, tq, wide_qk, pairs):
    wq = (2 * LANES if wide_qk else LANES) * pairs
    wv = LANES * pairs
    t_q = q.shape[0] // n_batch
    nq = t_q // tq
    n_steps = D_MODEL // wv
    in_specs = [pl.BlockSpec((tq, wq), lambda b, p, i: (b * nq + i, p))]
    args = [q]
    for k, v in kv_srcs:
        t_k = k.shape[0] // n_batch
        in_specs.append(pl.BlockSpec((t_k, wq), lambda b, p, i: (b, p)))
        in_specs.append(pl.BlockSpec((t_k, wv), lambda b, p, i: (b, p)))
        args += [k, v]
    return pl.pallas_call(
        functools.partial(_attn_kernel, n_src=len(kv_srcs), wide_qk=wide_qk, pairs=pairs),
        grid=(n_batch, n_steps, nq),
        in_specs=in_specs,
        out_specs=pl.BlockSpec((tq, wv), lambda b, p, i: (b * nq + i, p)),
        out_shape=jax.ShapeDtypeStruct((q.shape[0], D_MODEL), BF16),
        compiler_params=_cparams(("arbitrary", "arbitrary", "arbitrary")),
        name="attention",
    )(*args)


def _col_max(s, chunk=64):
    acc = s[:chunk]
    for r in range(chunk, s.shape[0], chunk):
        acc = jnp.maximum(acc, s[r:r + chunk])
    return acc.max(axis=0, keepdims=True)


def _attn_t_kernel(q_ref, k1_ref, k2_ref, v1_ref, v2_ref, o_ref):
    n_heads = q_ref.shape[1] // LANES
    slots = [slice(j * LANES, (j + 1) * LANES) for j in range(n_heads)]
    def scores(sl):
        return _qk(k1_ref[:, sl], q_ref[:, sl]), _qk(k2_ref[:, sl], q_ref[:, sl])

    nxt = scores(slots[0])
    outs = []
    for n, sl in enumerate(slots):
        s1, s2 = nxt
        if n + 1 < n_heads:
            nxt = scores(slots[n + 1])
        m = jnp.maximum(_col_max(s1), _col_max(s2))
        p1 = jnp.exp2(s1 - m).astype(BF16)
        p2 = jnp.exp2(s2 - m).astype(BF16)
        ot = (jnp.dot(v1_ref[sl, :], p1, preferred_element_type=F32)
              + jnp.dot(v2_ref[sl, :], p2, preferred_element_type=F32))
        outs.append(ot[:MLA_V_DIM] * (1.0 / ot[MLA_V_DIM:MLA_V_DIM + 1]))
    for j in range(0, n_heads, 2):
        pair = jnp.concatenate(outs[j:j + 2], axis=0).T
        o_ref[:, (j // 2) * LANES:(j // 2 + 1) * LANES] = pair.astype(o_ref.dtype)


def _attention_t(q, k1, v1t, k2, v2t, n_batch, tq, heads):
    wq = heads * LANES
    wo = heads * MLA_V_DIM
    nq = q.shape[0] // n_batch // tq
    t1 = k1.shape[0] // n_batch
    t2 = k2.shape[0] // n_batch
    return pl.pallas_call(
        _attn_t_kernel,
        grid=(n_batch, MLA_HEADS // heads, nq),
        in_specs=[pl.BlockSpec((tq, wq), lambda b, p, i: (b * nq + i, p)),
                  pl.BlockSpec((t1, wq), lambda b, p, i: (b, p)),
                  pl.BlockSpec((t2, wq), lambda b, p, i: (b, p)),
                  pl.BlockSpec((wq, t1), lambda b, p, i: (p, b)),
                  pl.BlockSpec((wq, t2), lambda b, p, i: (p, b))],
        out_specs=pl.BlockSpec((tq, wo), lambda b, p, i: (b * nq + i, p)),
        out_shape=jax.ShapeDtypeStruct((q.shape[0], D_MODEL), BF16),
        compiler_params=_cparams(("arbitrary", "arbitrary", "arbitrary")),
        name="attention_t",
    )(q, k1, k2, v1t, v2t)


def _na_group_layout(rows):
    n_groups = rows // NA_GROUP_ROWS
    kh = min(NA_KH, rows)
    bases, types, keys = [], [], {}
    for gi in range(n_groups):
        r0 = gi * NA_GROUP_ROWS
        base = int(np.clip(r0 - kh // 2, 0, rows - NA_WIN_ROWS))
        starts = tuple(int(np.clip(r - kh // 2, 0, rows - kh)) - base for r in range(r0, r0 + NA_GROUP_ROWS))
        key = (starts, r0 - base)
        if key not in keys:
            keys[key] = len(keys)
        bases.append(base)
        types.append(keys[key])
    return bases, types, list(keys.keys())


def _na_bias_tables(rpb, rows):
    kh = min(NA_KH, rows)
    _, _, type_keys = _na_group_layout(rows)
    qc = np.arange(GRID_W)
    kc = np.arange(GRID_W)
    col_start = np.clip(qc - NA_KW // 2, 0, GRID_W - NA_KW)
    col_valid = (kc[None, :] >= col_start[:, None]) & (kc[None, :] < col_start[:, None] + NA_KW)
    dx = kc[None, :] - qc[:, None] + (NA_KW - 1)
    onehot = ((dx[None] == np.arange(2 * NA_KW - 1)[:, None, None]) & col_valid[None]).astype(np.float32)
    band = jnp.einsum('hyd,dqk->hykq', rpb.astype(F32), onehot, precision=lax.Precision.HIGHEST)
    band = jnp.where(col_valid.T, band, NEG_INF)
    strip = band.reshape(NA_HEADS, (2 * NA_KH - 1) * GRID_W, GRID_W)
    tables = []
    for starts, r_off in type_keys:
        q_cols = []
        for rq in range(NA_GROUP_ROWS):
            dy0 = starts[rq] - (rq + r_off) + (NA_KH - 1)
            piece = strip[:, dy0 * GRID_W:(dy0 + kh) * GRID_W, :]
            pad = ((0, 0), (starts[rq] * GRID_W, (NA_WIN_ROWS - kh - starts[rq]) * GRID_W), (0, 0))
            q_cols.append(jnp.pad(piece, pad, constant_values=NEG_INF))
        tables.append(jnp.concatenate(q_cols, axis=2))
    return jnp.stack(tables, axis=1) * LOG2_E


def _state_k_kernel(x_ref, o_ref):
    for h in range(NA_HEADS):
        o_ref[:, h * NA_HEAD_DIM:(h + 1) * NA_HEAD_DIM] = x_ref[0, 0, :, h, :].astype(o_ref.dtype)


def _state_vt_kernel(x_ref, o_ref, flat_ref):
    for h in range(NA_HEADS):
        flat_ref[:, h * NA_HEAD_DIM:(h + 1) * NA_HEAD_DIM] = x_ref[0, 0, :, h, :]
    xt = flat_ref[...].T
    t = xt.shape[1]
    tail = jnp.where(lax.broadcasted_iota(jnp.int32, (LANES - NA_HEAD_DIM, t), 0) == 0, 1.0, 0.0)
    for h in range(NA_HEADS):
        o_ref[h * LANES:h * LANES + NA_HEAD_DIM, :] = xt[h * NA_HEAD_DIM:(h + 1) * NA_HEAD_DIM, :].astype(o_ref.dtype)
        o_ref[h * LANES + NA_HEAD_DIM:(h + 1) * LANES, :] = tail.astype(o_ref.dtype)


def _state_k(x, layer):
    b, _, t, h, dh = x.shape
    return pl.pallas_call(
        _state_k_kernel,
        grid=(b,),
        in_specs=[pl.BlockSpec((1, 1, t, h, dh), lambda i: (i, layer, 0, 0, 0))],
        out_specs=pl.BlockSpec((t, h * dh), lambda i: (i, 0)),
        out_shape=jax.ShapeDtypeStruct((b * t, h * dh), BF16),
        compiler_params=_cparams(("arbitrary",)),
        name="state_k",
    )(x)


def _state_vt(x, layer):
    b, _, t, h, dh = x.shape
    return pl.pallas_call(
        _state_vt_kernel,
        grid=(b,),
        in_specs=[pl.BlockSpec((1, 1, t, h, dh), lambda i: (i, layer, 0, 0, 0))],
        out_specs=pl.BlockSpec((h * LANES, t), lambda i: (0, i)),
        out_shape=jax.ShapeDtypeStruct((h * LANES, b * t), BF16),
        scratch_shapes=[pltpu.VMEM((t, h * dh), F32)],
        compiler_params=_cparams(("arbitrary",)),
        name="state_vt",
    )(x)


def _na_latent_kernel(q_ref, k_ref, vt_ref, kc_ref, vct_ref, tab_ref, o_ref, *, bases, types):
    _, m0, m1 = _head_masks()
    kctx = kc_ref[...]
    gq = NA_GROUP_ROWS * GRID_W
    gk = NA_WIN_ROWS * GRID_W
    blocks = [(gi, base, ty, j) for gi, (base, ty) in enumerate(zip(bases, types)) for j in range(2)]

    def scores(gi, base, ty, j):
        qj = q_ref[gi * gq:(gi + 1) * gq, :] * (m0 if j == 0 else m1)
        kwin = k_ref[base * GRID_W:base * GRID_W + gk, :]
        return _qk(kwin, qj) + tab_ref[j, ty], _qk(kctx, qj)

    nxt = scores(*blocks[0])
    outs = []
    for n, (gi, base, ty, j) in enumerate(blocks):
        s_win, s_ctx = nxt
        if n + 1 < len(blocks):
            nxt = scores(*blocks[n + 1])
        m = jnp.maximum(_col_max(s_win), _col_max(s_ctx))
        p_win = jnp.exp2(s_win - m).astype(BF16)
        p_ctx = jnp.exp2(s_ctx - m).astype(BF16)
        sl = slice(j * LANES, (j + 1) * LANES)
        ot = (jnp.dot(vt_ref[sl, base * GRID_W:base * GRID_W + gk], p_win, preferred_element_type=F32)
              + jnp.dot(vct_ref[sl, :], p_ctx, preferred_element_type=F32))
        outs.append(ot[:NA_HEAD_DIM] * (1.0 / ot[NA_HEAD_DIM:NA_HEAD_DIM + 1]))
        if j == 1:
            o_ref[gi * gq:(gi + 1) * gq, :] = jnp.concatenate(outs, axis=0).T.astype(o_ref.dtype)
            outs = []


def _na_latent(q, k, vt, k_ctx, vt_ctx, tables, n_batch):
    t = q.shape[0] // n_batch
    rows = t // GRID_W
    t_ctx = k_ctx.shape[0] // n_batch
    bases, types, type_keys = _na_group_layout(rows)
    n_types = len(type_keys)
    gq = NA_GROUP_ROWS * GRID_W
    gk = NA_WIN_ROWS * GRID_W
    tok = pl.BlockSpec((t, LANES), lambda p, b: (b, p))
    ctx = pl.BlockSpec((t_ctx, LANES), lambda p, b: (b, p))
    return pl.pallas_call(
        functools.partial(_na_latent_kernel, bases=bases, types=types),
        grid=(D_MODEL // LANES, n_batch),
        in_specs=[tok, tok, pl.BlockSpec((2 * LANES, t), lambda p, b: (p, b)),
                  ctx, pl.BlockSpec((2 * LANES, t_ctx), lambda p, b: (p, b)),
                  pl.BlockSpec((2, n_types, gk, gq), lambda p, b: (p, 0, 0, 0))],
        out_specs=tok,
        out_shape=jax.ShapeDtypeStruct(q.shape, BF16),
        compiler_params=_cparams(("arbitrary", "arbitrary")),
        name="na_latent",
    )(q, k, vt, k_ctx, vt_ctx, tables)


def _rope_lane_tables(n_tokens):
    n_freq = MLA_ROPE_DIM // 4
    inv = 1.0 / (ROPE_THETA ** (jnp.arange(n_freq, dtype=F32) / n_freq))
    tt = jnp.arange(n_tokens)
    row = (tt // GRID_W).astype(F32)
    col = (tt % GRID_W).astype(F32)
    ang = jnp.concatenate([row[:, None] * inv, col[:, None] * inv], axis=-1)
    cos, sin = jnp.cos(ang), jnp.sin(ang)
    ones = jnp.ones((n_tokens, MLA_NOPE_DIM), F32)
    z16 = jnp.zeros((n_tokens, MLA_ROPE_DIM // 2), F32)
    z32 = jnp.zeros((n_tokens, LANES - MLA_QK_DIM), F32)
    z64 = jnp.zeros((n_tokens, MLA_NOPE_DIM), F32)
    c_tab = jnp.concatenate([ones, cos, cos, z32], axis=-1)
    s_dn = jnp.concatenate([z64, -sin, z16, z32], axis=-1)
    s_up = jnp.concatenate([z64, z16, sin, z32], axis=-1)
    return c_tab, s_dn, s_up


def _rope_slot(x, c_tab, s_dn, s_up):
    half = MLA_ROPE_DIM // 2
    return x * c_tab + pltpu.roll(x, LANES - half, 1) * s_dn + pltpu.roll(x, half, 1) * s_up


def _mla_kv(ckv_n, kr_slot, wuk_ref, wuv_ref, k_ref, v_ref, v_transposed):
    c = ckv_n.astype(BF16)
    k_nope = jnp.dot(c, wuk_ref[...], preferred_element_type=F32)
    for h in range(MLA_HEADS):
        k_ref[:, h * LANES:(h + 1) * LANES] = (k_nope[:, h * LANES:(h + 1) * LANES] + kr_slot).astype(k_ref.dtype)
    if v_transposed:
        vt = _qk(wuv_ref[...], c)
        row = lax.broadcasted_iota(jnp.int32, vt.shape, 0)
        v_ref[...] = jnp.where((row & (LANES - 1)) == MLA_V_DIM, 1.0, vt).astype(v_ref.dtype)
    else:
        v_ref[...] = jnp.dot(c, wuv_ref[...], preferred_element_type=F32).astype(v_ref.dtype)


def _mla_proj_kernel(*refs, rope, emit_state, q_scale):
    (x_ref, g_ref, sh_ref, sc_ref, win_ref, qg_ref, kvg_ref, wuq_ref, wuk_ref, wuv_ref) = refs[:10]
    pos = 10
    if rope:
        ct_ref, sd_ref, su_ref = refs[pos:pos + 3]
        pos += 3
    q_ref, k_ref, v_ref = refs[pos:pos + 3]
    pos += 3
    h = _norm_mod(x_ref[...], g_ref[...], sh_ref[0], sc_ref[0]).astype(BF16)
    c = jnp.dot(h, win_ref[...], preferred_element_type=F32)
    cq = _rms(c[:, :MLA_Q_RANK], qg_ref[...]).astype(BF16)
    ckv = _rms(c[:, MLA_Q_RANK:MLA_Q_RANK + MLA_KV_RANK], kvg_ref[...])
    kr_chunk = c[:, MLA_Q_RANK + MLA_KV_RANK:]
    if emit_state:
        ckv_out, kr_out = refs[pos:pos + 2]
        ckv_out[...] = ckv
        kr_out[...] = kr_chunk[:, :MLA_ROPE_DIM]
    kr_slot = pltpu.roll(kr_chunk, MLA_NOPE_DIM, 1)
    q = jnp.dot(cq, wuq_ref[...], preferred_element_type=F32) * q_scale
    if rope:
        ct, sd, su = ct_ref[...], sd_ref[...], su_ref[...]
        kr_slot = _rope_slot(kr_slot, ct, sd, su)
    for hh in range(MLA_HEADS):
        qh = q[:, hh * LANES:(hh + 1) * LANES]
        if rope:
            qh = _rope_slot(qh, ct, sd, su)
        q_ref[:, hh * LANES:(hh + 1) * LANES] = qh.astype(q_ref.dtype)
    _mla_kv(ckv, kr_slot, wuk_ref, wuv_ref, k_ref, v_ref, v_transposed=rope)


def _mla_proj(x, g, mod, layer, row0, tokens_per_row, w, rope_tabs, emit_state):
    t = x.shape[0]
    win, qg, kvg, wuq, wuk, wuv = w
    tok = lambda n: pl.BlockSpec((TM, n), lambda i: (i, 0))
    wq = MLA_HEADS * LANES
    in_specs = [tok(D_MODEL), _const_spec((1, D_MODEL)),
                _mod_spec(layer, 0, row0, tokens_per_row), _mod_spec(layer, 1, row0, tokens_per_row),
                _const_spec(win.shape), _const_spec(qg.shape), _const_spec(kvg.shape),
                _const_spec(wuq.shape), _const_spec(wuk.shape), _const_spec(wuv.shape)]
    args = [x, g, mod, mod, win, qg, kvg, wuq, wuk, wuv]
    rope = rope_tabs is not None
    if rope:
        n_tab = rope_tabs[0].shape[0] // TM
        in_specs += [pl.BlockSpec((TM, LANES), lambda i: (i % n_tab, 0)) for _ in range(3)]
        args += list(rope_tabs)
    if rope:
        v_spec, v_shape = pl.BlockSpec((wq, TM), lambda i: (0, i)), jax.ShapeDtypeStruct((wq, t), BF16)
        q_scale = MLA_QK_DIM ** -0.5 * LOG2_E
    else:
        v_spec, v_shape = tok(D_MODEL), jax.ShapeDtypeStruct((t, D_MODEL), BF16)
        q_scale = MLA_QK_DIM ** -0.5
    out_specs = [tok(wq), tok(wq), v_spec]
    out_shape = [jax.ShapeDtypeStruct((t, wq), BF16), jax.ShapeDtypeStruct((t, wq), BF16), v_shape]
    if emit_state:
        out_specs += [tok(MLA_KV_RANK), tok(MLA_ROPE_DIM)]
        out_shape += [jax.ShapeDtypeStruct((t, MLA_KV_RANK), F32), jax.ShapeDtypeStruct((t, MLA_ROPE_DIM), F32)]
    return pl.pallas_call(
        functools.partial(_mla_proj_kernel, rope=rope, emit_state=emit_state, q_scale=q_scale),
        grid=(t // TM,),
        in_specs=in_specs, out_specs=out_specs, out_shape=out_shape,
        compiler_params=_cparams(("arbitrary",)),
        name="mla_proj",
    )(*args)


def _mla_state_kv_kernel(ckv_ref, kr_ref, wuk_ref, wuv_ref, k_ref, v_ref):
    kr_slot = pltpu.roll(kr_ref[...], MLA_NOPE_DIM, 1)
    _mla_kv(ckv_ref[...], kr_slot, wuk_ref, wuv_ref, k_ref, v_ref, v_transposed=True)


def _mla_state_kv(ckv, kr_pad, wuk, wuv_t):
    t = ckv.shape[0]
    wq = MLA_HEADS * LANES
    tok = lambda n: pl.BlockSpec((TM, n), lambda i: (i, 0))
    return pl.pallas_call(
        _mla_state_kv_kernel,
        grid=(t // TM,),
        in_specs=[tok(MLA_KV_RANK), tok(LANES), _const_spec(wuk.shape), _const_spec(wuv_t.shape)],
        out_specs=[tok(wq), pl.BlockSpec((wq, TM), lambda i: (0, i))],
        out_shape=[jax.ShapeDtypeStruct((t, wq), BF16), jax.ShapeDtypeStruct((wq, t), BF16)],
        compiler_params=_cparams(("arbitrary",)),
        name="mla_state_kv",
    )(ckv, kr_pad, wuk, wuv_t)


def _first_argmax_rows(v, row):
    top = v.max(axis=0, keepdims=True)
    idx = jnp.where(v == top, row, float(v.shape[0])).min(axis=0, keepdims=True)
    return top, idx


ROUTER_ROWS = 32
ROUTER_EXPERT_ROW = 8


def _route_t(lt):
    n = lt.shape[1]
    row8 = lax.broadcasted_iota(jnp.int32, (8, n), 0).astype(F32)
    row16 = lax.broadcasted_iota(jnp.int32, (N_EXPERTS, n), 0).astype(F32)
    is_g = row8 < N_GROUPS
    gl = jnp.where(is_g, lt[:8], -jnp.inf)
    gmax, gsel = _first_argmax_rows(gl, row8)
    p_group = 1.0 / jnp.where(is_g, jnp.exp(gl - gmax), 0.0).sum(axis=0, keepdims=True)
    first = gsel * EXPERTS_PER_GROUP
    e_all = lt[ROUTER_EXPERT_ROW:ROUTER_EXPERT_ROW + N_EXPERTS]
    el = jnp.where((row16 >= first) & (row16 < first + EXPERTS_PER_GROUP), e_all, -jnp.inf)
    t1, i1 = _first_argmax_rows(el, row16)
    t2, i2 = _first_argmax_rows(jnp.where(row16 == i1, -jnp.inf, el), row16)
    e2 = jnp.exp(t2 - t1)
    w1 = 1.0 / (1.0 + e2)
    gates = jnp.where(row16 == i1, w1 * p_group, jnp.where(row16 == i2, e2 * w1 * p_group, 0.0))
    rel = gates[:8] + gates[8:]
    rel = rel + pltpu.roll(rel, EXPERTS_PER_GROUP, 0)
    meta_t = jnp.where(row8 < EXPERTS_PER_GROUP, rel, jnp.where(row8 == META_GROUP_LANE, gsel, 0.0))
    onehot_t = jnp.where(row8 == gsel, 1.0, 0.0)
    return meta_t, onehot_t, row8, gsel


def _pack_halves(x):
    half = x.shape[1] // 2
    lo = lax.bitcast_convert_type(x[:, :half].astype(BF16).astype(F32), jnp.uint32)
    hi = lax.bitcast_convert_type(x[:, half:].astype(BF16).astype(F32), jnp.uint32)
    return (lo >> 16) | hi


def _unpack_halves(p):
    lo = lax.bitcast_convert_type(p << 16, F32)
    hi = lax.bitcast_convert_type(p & jnp.uint32(0xFFFF0000), F32)
    return lo, hi


ROW_WORDS = D_MODEL // 2 + LANES
META_GROUP_LANE = EXPERTS_PER_GROUP
META_RANK_LANE = EXPERTS_PER_GROUP + 1


def _proj_res_kernel(o_ref, x_ref, w_ref, gate_ref, g_ref, sh_ref, sc_ref, wrt_ref, brt_ref, tri_ref,
                     xn_ref, hpk_ref, metat_ref, cnt_ref):
    @pl.when(pl.program_id(0) == 0)
    def _():
        cnt_ref[...] = jnp.zeros(cnt_ref.shape, cnt_ref.dtype)

    sub = tri_ref.shape[0]
    cnt = cnt_ref[:, :1]
    for r0 in range(0, x_ref.shape[0], sub):
        rows = slice(r0, r0 + sub)
        y = jnp.dot(o_ref[rows, :], w_ref[...], preferred_element_type=F32)
        xn = x_ref[rows, :] + gate_ref[0] * y
        xn_ref[rows, :] = xn
        h = _norm_mod(xn, g_ref[...], sh_ref[0], sc_ref[0])
        lt = _qk(wrt_ref[...], h.astype(BF16)) + brt_ref[...]
        meta_t, onehot_t, row8, gsel = _route_t(lt)
        prefix = jnp.dot(onehot_t.astype(BF16), tri_ref[...], preferred_element_type=F32)
        rank = jnp.where(row8 == gsel, prefix + cnt - 1.0, 0.0).sum(axis=0, keepdims=True)
        cnt = cnt + onehot_t.sum(axis=1, keepdims=True)
        meta_t = jnp.where(row8 == META_RANK_LANE, rank, meta_t)
        meta = jnp.concatenate([meta_t, jnp.zeros((LANES - 8, sub), F32)], axis=0).T
        metat_ref[:, rows] = meta_t
        hpk_ref[rows, :] = jnp.concatenate([_pack_halves(h), lax.bitcast_convert_type(meta, jnp.uint32)], axis=1)
    cnt_ref[...] = jnp.broadcast_to(cnt, cnt_ref.shape)


def _proj_res(o, x, w_o, g_ffn, mod, layer, row0, tokens_per_row, w_router_t, b_router_t):
    t = x.shape[0]
    tm = TM_PROJ
    tok = lambda n: pl.BlockSpec((tm, n), lambda i: (i, 0))
    ms = lambda k: _mod_spec(layer, k, row0, tokens_per_row, tm)
    tri = jnp.asarray(np.triu(np.ones((TM, TM), np.float32)), BF16)
    return pl.pallas_call(
        _proj_res_kernel,
        grid=(t // tm,),
        in_specs=[tok(D_MODEL), tok(D_MODEL), _const_spec(w_o.shape), ms(2), _const_spec((1, D_MODEL)), ms(3), ms(4),
                  _const_spec(w_router_t.shape), _const_spec(b_router_t.shape), _const_spec((TM, TM))],
        out_specs=[tok(D_MODEL), tok(ROW_WORDS), pl.BlockSpec((8, tm), lambda i: (0, i)), _const_spec((8, LANES))],
        out_shape=[jax.ShapeDtypeStruct((t, D_MODEL), F32), jax.ShapeDtypeStruct((t, ROW_WORDS), jnp.uint32),
                   jax.ShapeDtypeStruct((8, t), F32), jax.ShapeDtypeStruct((8, LANES), F32)],
        compiler_params=_cparams(("arbitrary",)),
        name="proj_res",
    )(o, x, w_o, mod, g_ffn, mod, mod, w_router_t, b_router_t, tri)


def _moe_plan(meta_t, cnt):
    t = meta_t.shape[1]
    n_tiles = t // TM + N_GROUPS
    group = meta_t[META_GROUP_LANE].astype(jnp.int32)
    rank = meta_t[META_RANK_LANE].astype(jnp.int32)
    counts = cnt[:N_GROUPS, 0].astype(jnp.int32)
    tiles = (counts + TM - 1) // TM
    tile_end = jnp.cumsum(tiles)
    tile_start = tile_end - tiles
    pos = rank + TM * jnp.sum(jnp.where(group[:, None] == jnp.arange(N_GROUPS)[None], tile_start[None], 0), axis=1)
    tile_group = jnp.minimum(jnp.sum(jnp.arange(n_tiles)[:, None] >= tile_end[None], axis=1), N_GROUPS - 1)
    pad_bounds = jnp.stack([tile_start * TM + counts, tile_end * TM], axis=1).reshape(-1)
    return (pos.astype(jnp.int32), tile_group.astype(jnp.int32), tile_end[-1:].astype(jnp.int32),
            pad_bounds.astype(jnp.int32))


def _gather_rows(srcs, idx_ref, base, dsts, n_rows):
    for r in range(n_rows):
        s = idx_ref[base + r]
        for src_ref, dst_ref in zip(srcs, dsts):
            dst_ref[r:r + 1, :] = src_ref[pl.ds(s, 1), :]


def _moe_kernel(pos_ref, tg_ref, nt_ref, pad_ref, hpk_ref, wg_ref, wu_ref, wd_ref, ypk_ref, src_ref, hbuf, hid_ref):
    i = pl.program_id(0)
    n_tok = hpk_ref.shape[0]

    @pl.when(i == 0)
    def _():
        def clear(p, c):
            src_ref[p] = 0
            return c
        for g in range(N_GROUPS):
            lax.fori_loop(pad_ref[2 * g], pad_ref[2 * g + 1], clear, 0)

        def fill(t, c):
            src_ref[pos_ref[t]] = t
            return c
        lax.fori_loop(0, n_tok, fill, 0, unroll=8)
        _gather_rows((hpk_ref,), src_ref, 0, (hbuf.at[0],), TM)

    @pl.when(i < nt_ref[0])
    def _():
        slot = i % 2
        row = hbuf[slot]
        lo, hi = _unpack_halves(row[:, :D_MODEL // 2])
        h = jnp.concatenate([lo.astype(BF16), hi.astype(BF16)], axis=1)
        gates = lax.bitcast_convert_type(row[:, D_MODEL // 2:], F32)
        nxt = jnp.minimum(i + 1, nt_ref[0] - 1) * TM
        _gather_rows((hpk_ref,), src_ref, nxt, (hbuf.at[1 - slot],), TM)
        f = D_FF_EXPERT
        for j in range(EXPERTS_PER_GROUP):
            gate_act = _silu(jnp.dot(h, wg_ref[0, j], preferred_element_type=F32))
            hid = gate_act * jnp.dot(h, wu_ref[0, j], preferred_element_type=F32) * gates[:, j:j + 1]
            hid_ref[:, j * f:(j + 1) * f] = hid.astype(BF16)
        ypk_ref[...] = _pack_halves(jnp.dot(hid_ref[...], wd_ref[0], preferred_element_type=F32))

    @pl.when(i >= nt_ref[0])
    def _():
        ypk_ref[...] = jnp.zeros(ypk_ref.shape, ypk_ref.dtype)


def _moe(hpk, pos, tile_group, n_used, pad_bounds, wg, wu, wd):
    t = hpk.shape[0]
    n_tiles = t // TM + N_GROUPS
    f = D_FF_EXPERT
    grid_spec = pltpu.PrefetchScalarGridSpec(
        num_scalar_prefetch=4,
        grid=(n_tiles,),
        in_specs=[
            pl.BlockSpec(hpk.shape, lambda i, *_: (0, 0), pipeline_mode=pl.Buffered(1)),
            pl.BlockSpec((1, EXPERTS_PER_GROUP, D_MODEL, f), lambda i, pos, tg, nt, pad: (tg[i], 0, 0, 0)),
            pl.BlockSpec((1, EXPERTS_PER_GROUP, D_MODEL, f), lambda i, pos, tg, nt, pad: (tg[i], 0, 0, 0)),
            pl.BlockSpec((1, EXPERTS_PER_GROUP * f, D_MODEL), lambda i, pos, tg, nt, pad: (tg[i], 0, 0)),
        ],
        out_specs=pl.BlockSpec((TM, D_MODEL // 2), lambda i, *_: (i, 0)),
        scratch_shapes=[pltpu.SMEM((n_tiles * TM,), jnp.int32),
                        pltpu.VMEM((2, TM, ROW_WORDS), jnp.uint32),
                        pltpu.VMEM((TM, EXPERTS_PER_GROUP * f), BF16)],
    )
    return pl.pallas_call(
        _moe_kernel,
        grid_spec=grid_spec,
        out_shape=jax.ShapeDtypeStruct((n_tiles * TM, D_MODEL // 2), jnp.uint32),
        compiler_params=_cparams(("arbitrary",)),
        name="moe",
    )(pos, tile_group, n_used, pad_bounds, hpk, wg, wu, wd)


def _combine_kernel(*refs, final_norm):
    pos_ref, ypk_ref, x_ref, gate_ref = refs[:4]
    o_ref, ybuf = refs[-2], refs[-1]
    _gather_rows((ypk_ref,), pos_ref, pl.program_id(0) * TM, (ybuf,), TM)
    lo, hi = _unpack_halves(ybuf[...])
    out = x_ref[...] + gate_ref[0] * jnp.concatenate([lo, hi], axis=1)
    if final_norm:
        out = _rms(out, refs[4][...])
    o_ref[...] = out


def _combine(ypk, pos, x, mod, layer, row0, tokens_per_row, final_g):
    t = x.shape[0]

    def mod_map(i, pos):
        row = row0 + (i * TM) // tokens_per_row
        return ((layer * MOD_ROWS + row) * N_MOD + 5, 0, 0)

    in_specs = [pl.BlockSpec(ypk.shape, lambda i, pos: (0, 0), pipeline_mode=pl.Buffered(1)),
                pl.BlockSpec((TM, D_MODEL), lambda i, pos: (i, 0)),
                pl.BlockSpec((1, 1, D_MODEL), mod_map)]
    args = [ypk, x, mod]
    if final_g is not None:
        in_specs.append(pl.BlockSpec((1, D_MODEL), lambda i, pos: (0, 0)))
        args.append(final_g)
    grid_spec = pltpu.PrefetchScalarGridSpec(
        num_scalar_prefetch=1,
        grid=(t // TM,),
        in_specs=in_specs,
        out_specs=pl.BlockSpec((TM, D_MODEL), lambda i, pos: (i, 0)),
        scratch_shapes=[pltpu.VMEM((TM, D_MODEL // 2), jnp.uint32)],
    )
    return pl.pallas_call(
        functools.partial(_combine_kernel, final_norm=final_g is not None),
        grid_spec=grid_spec,
        out_shape=jax.ShapeDtypeStruct((t, D_MODEL), F32),
        compiler_params=_cparams(("arbitrary",)),
        name="moe_combine",
    )(pos, *args)


def _pad_heads(w, used):
    k = w.shape[0]
    w = w.reshape(k, MLA_HEADS, used)
    return jnp.pad(w, ((0, 0), (0, 0), (0, LANES - used))).reshape(k, MLA_HEADS * LANES)


def kernel(x_prompt, x_sample, state_na_k, state_na_v, state_mla_ckv, state_mla_kr, c, c_ctx, ada_w, ada_b, norm_mix_g, norm_ffn_g, final_norm_g, na_w_qkv, na_w_o, na_rpb, mla_w_in, mla_q_norm_g, mla_kv_norm_g, mla_w_uq, mla_w_uk, mla_w_uv, mla_w_o, moe_w_group, moe_b_group, moe_w_expert, moe_b_expert, moe_w_gate, moe_w_up, moe_w_down):
    n_b, seq, d = x_prompt.shape
    dec_b, n_lat, _ = x_sample.shape
    past = state_na_k.shape[2]
    rows = n_lat // GRID_W
    assert d == D_MODEL and dec_b + 1 <= MOD_ROWS and n_lat % TM_PROJ == 0 and (n_b * seq) % TM_PROJ == 0

    cond8 = jnp.concatenate([c_ctx[None], c, jnp.zeros((MOD_ROWS - 1 - dec_b, d), F32)], axis=0)
    mod = _ada_mod(cond8, ada_w, ada_b)

    xp = x_prompt.reshape(n_b * seq, d)
    xs = x_sample.reshape(dec_b * n_lat, d)
    p_rows = (0, n_b * seq)
    s_rows = (1, n_lat)
    outs = {}

    for i in range(DEPTH):
        j = i // 2
        g_mix = norm_mix_g[i][None]
        g_ffn = norm_ffn_g[i][None]
        if i % 2 == 0:
            w_qkv = na_w_qkv[j].astype(BF16)
            w_vt = jnp.pad(na_w_qkv[j][:, 2 * d:].T.reshape(NA_HEADS, NA_HEAD_DIM, d),
                           ((0, 0), (0, LANES - NA_HEAD_DIM), (0, 0))).reshape(NA_HEADS * LANES, d).astype(BF16)
            qp, kp, vp, outs['na_k'], outs['na_v'] = _na_qkv(xp, g_mix, mod, i, *p_rows, w_qkv, n_b)
            qs, ks, vs_t = _na_qkv_t(xs, g_mix, mod, i, *s_rows, w_qkv, w_vt)
            op = _attention(qp, [(kp, vp)], n_b, seq, wide_qk=False, pairs=8)
            tables = _na_bias_tables(na_rpb[j], rows)
            os_ = _na_latent(qs, ks, vs_t, _state_k(state_na_k, j), _state_vt(state_na_v, j), tables, dec_b)
            w_o = na_w_o[j].astype(BF16)
        else:
            pad_in = MLA_IN_PAD - mla_w_in.shape[-1]
            w_mla = (jnp.pad(mla_w_in[j], ((0, 0), (0, pad_in))).astype(BF16),
                     mla_q_norm_g[j][None], mla_kv_norm_g[j][None],
                     _pad_heads(mla_w_uq[j], MLA_QK_DIM).astype(BF16),
                     _pad_heads(mla_w_uk[j], MLA_NOPE_DIM).astype(BF16),
                     mla_w_uv[j].astype(BF16))
            wuv_t = _pad_heads(mla_w_uv[j], MLA_V_DIM).T.astype(BF16)
            w_mla_t = w_mla[:5] + (wuv_t,)
            qp, kp, vp, ckv_p, kr_p = _mla_proj(xp, g_mix, mod, i, *p_rows, w_mla, None, True)
            qs, ks, vs_t = _mla_proj(xs, g_mix, mod, i, *s_rows, w_mla_t, _rope_lane_tables(n_lat), False)
            outs['ckv'] = ckv_p.reshape(n_b, 1, seq, MLA_KV_RANK)
            outs['kr'] = kr_p.reshape(n_b, 1, seq, MLA_ROPE_DIM)
            kr_state = jnp.pad(state_mla_kr[:, j].reshape(dec_b * past, MLA_ROPE_DIM),
                               ((0, 0), (0, LANES - MLA_ROPE_DIM)))
            kc, vc_t = _mla_state_kv(state_mla_ckv[:, j].reshape(dec_b * past, MLA_KV_RANK), kr_state,
                                     w_mla[4], wuv_t)
            op = _attention(qp, [(kp, vp)], n_b, seq, wide_qk=True, pairs=8)
            os_ = _attention_t(qs, ks, vs_t, kc, vc_t, dec_b, 512, heads=4)
            w_o = mla_w_o[j].astype(BF16)

        pad_g = ((0, ROUTER_EXPERT_ROW - N_GROUPS), (0, 0))
        pad_e = ((0, ROUTER_ROWS - ROUTER_EXPERT_ROW - N_EXPERTS), (0, 0))
        w_router_t = jnp.concatenate([jnp.pad(moe_w_group[i].T, pad_g), jnp.pad(moe_w_expert[i].T, pad_e)],
                                     axis=0).astype(BF16)
        b_router_t = jnp.concatenate([jnp.pad(moe_b_group[i][:, None], pad_g),
                                      jnp.pad(moe_b_expert[i][:, None], pad_e)], axis=0)
        b_router_t = jnp.broadcast_to(b_router_t, (ROUTER_ROWS, TM))
        f = D_FF_EXPERT
        wg = moe_w_gate[i].astype(BF16).reshape(N_GROUPS, EXPERTS_PER_GROUP, d, f)
        wu = moe_w_up[i].astype(BF16).reshape(N_GROUPS, EXPERTS_PER_GROUP, d, f)
        wd = moe_w_down[i].astype(BF16).reshape(N_GROUPS, EXPERTS_PER_GROUP * f, d)
        final_g = final_norm_g[None] if i == DEPTH - 1 else None
        new_x = []
        for o_att, x, rows_ in ((op, xp, p_rows), (os_, xs, s_rows)):
            xn, hpk, meta, cnt = _proj_res(o_att, x, w_o, g_ffn, mod, i, *rows_, w_router_t, b_router_t)
            pos, tile_group, n_used, pad_bounds = _moe_plan(meta, cnt)
            ypk = _moe(hpk, pos, tile_group, n_used, pad_bounds, wg, wu, wd)
            new_x.append(_combine(ypk, pos, xn, mod, i, *rows_, final_g))
        xp, xs = new_x

    return (xp.reshape(n_b, seq, d), xs.reshape(dec_b, n_lat, d),
            outs['na_k'], outs['na_v'], outs['ckv'], outs['kr'])
```

```python
import functools

import numpy as np
import jax
import jax.numpy as jnp
from jax import lax
from jax.experimental import pallas as pl
from jax.experimental.pallas import tpu as pltpu

F32 = jnp.float32
BF16 = jnp.bfloat16

D_MODEL = 1024
DEPTH = 2
GRID_W = 64
N_MOD = 6
EPS = 1e-6
NEG_INF = -1e30
NA_HEADS = 16
NA_HEAD_DIM = 64
NA_KH = 8
NA_KW = 16
NA_GROUP_ROWS = 4
NA_WIN_ROWS = 12
MLA_HEADS = 16
MLA_Q_RANK = 384
MLA_KV_RANK = 256
MLA_NOPE_DIM = 64
MLA_ROPE_DIM = 32
MLA_V_DIM = 64
MLA_QK_DIM = MLA_NOPE_DIM + MLA_ROPE_DIM
MLA_IN_PAD = 768
ROPE_THETA = 10000.0
LOG2_E = 1.4426950408889634
N_GROUPS = 4
EXPERTS_PER_GROUP = 4
N_EXPERTS = 16
D_FF_EXPERT = 256

LANES = 128
MOD_ROWS = 8
VMEM_LIMIT = 56 * 1024 * 1024
TM = 512
CAST_BLOCK_ELEMS = 512 * 1024
TM_PROJ = 1024


def _cparams(sem):
    return pltpu.CompilerParams(dimension_semantics=sem, vmem_limit_bytes=VMEM_LIMIT)


def _const_spec(shape):
    nd = len(shape)
    return pl.BlockSpec(shape, lambda *_: (0,) * nd)


def _cast_kernel(x_ref, o_ref):
    o_ref[...] = x_ref[...].astype(o_ref.dtype)


def _to_bf16(x):
    cols = x.shape[-1]
    x2 = x.reshape(-1, cols)
    rows = x2.shape[0]
    block = rows
    while block * cols > CAST_BLOCK_ELEMS and block % 32 == 0:
        block //= 2
    out = pl.pallas_call(
        _cast_kernel,
        grid=(rows // block,),
        in_specs=[pl.BlockSpec((block, cols), lambda i: (i, 0))],
        out_specs=pl.BlockSpec((block, cols), lambda i: (i, 0)),
        out_shape=jax.ShapeDtypeStruct((rows, cols), BF16),
        compiler_params=_cparams(("arbitrary",)),
        name="to_bf16",
    )(x2)
    return out.reshape(x.shape)


def _norm_mod(x, g, shift, scale):
    ms = jnp.mean(x * x, axis=-1, keepdims=True)
    y = x * lax.rsqrt(ms + EPS) * g
    return y * (1.0 + scale) + shift


def _rms(x, g):
    ms = jnp.mean(x * x, axis=-1, keepdims=True)
    return x * lax.rsqrt(ms + EPS) * g


def _silu(x):
    return x / (1.0 + jnp.exp(-x))


def _ada_kernel(cond_ref, w_ref, b_ref, o_ref):
    s = _silu(cond_ref[...]).astype(BF16)
    o_ref[0] = jnp.dot(s, w_ref[0].astype(BF16), preferred_element_type=F32) + b_ref[0]


def _ada_mod(cond8, ada_w, ada_b):
    tn = 1024
    n = N_MOD * D_MODEL
    out = pl.pallas_call(
        _ada_kernel,
        grid=(DEPTH, n // tn),
        in_specs=[
            _const_spec((MOD_ROWS, D_MODEL)),
            pl.BlockSpec((1, D_MODEL, tn), lambda l, j: (l, 0, j)),
            pl.BlockSpec((1, 1, tn), lambda l, j: (l, 0, j)),
        ],
        out_specs=pl.BlockSpec((1, MOD_ROWS, tn), lambda l, j: (l, 0, j)),
        out_shape=jax.ShapeDtypeStruct((DEPTH, MOD_ROWS, n), F32),
        compiler_params=_cparams(("arbitrary", "arbitrary")),
        name="ada_mod",
    )(cond8, ada_w, ada_b.reshape(DEPTH, 1, n))
    return out.reshape(DEPTH * MOD_ROWS * N_MOD, 1, D_MODEL)


def _mod_spec(layer, k, row0, tokens_per_row, tm=TM):
    def imap(i):
        row = row0 + (i * tm) // tokens_per_row
        return ((layer * MOD_ROWS + row) * N_MOD + k, 0, 0)
    return pl.BlockSpec((1, 1, D_MODEL), imap)


def _qkv_kernel(x_ref, g_ref, sh_ref, sc_ref, w_ref, q_ref, k_ref, v_ref, ks_ref, vs_ref):
    h = _norm_mod(x_ref[...], g_ref[...], sh_ref[0], sc_ref[0]).astype(BF16)
    y = jnp.dot(h, w_ref[...], preferred_element_type=F32)
    q_ref[...] = (y[:, :D_MODEL] * (NA_HEAD_DIM ** -0.5)).astype(q_ref.dtype)
    k = y[:, D_MODEL:2 * D_MODEL]
    v = y[:, 2 * D_MODEL:]
    k_ref[...] = k.astype(k_ref.dtype)
    v_ref[...] = v.astype(v_ref.dtype)
    ks_ref[...] = pltpu.einshape("t(hd)->thd", k, h=NA_HEADS).reshape(ks_ref.shape)
    vs_ref[...] = pltpu.einshape("t(hd)->thd", v, h=NA_HEADS).reshape(vs_ref.shape)


def _na_qkv(x, g, mod, layer, row0, tokens_per_row, w_qkv, n_batch):
    t = x.shape[0]
    seq = t // n_batch
    nb = TM // seq
    tok = pl.BlockSpec((TM, D_MODEL), lambda i: (i, 0))
    state = pl.BlockSpec((nb, 1, seq, NA_HEADS, NA_HEAD_DIM), lambda i: (i, 0, 0, 0, 0))
    state_shape = jax.ShapeDtypeStruct((n_batch, 1, seq, NA_HEADS, NA_HEAD_DIM), F32)
    return pl.pallas_call(
        _qkv_kernel,
        grid=(t // TM,),
        in_specs=[tok, _const_spec((1, D_MODEL)),
                  _mod_spec(layer, 0, row0, tokens_per_row), _mod_spec(layer, 1, row0, tokens_per_row),
                  _const_spec((D_MODEL, 3 * D_MODEL))],
        out_specs=[tok, tok, tok, state, state],
        out_shape=[jax.ShapeDtypeStruct((t, D_MODEL), BF16)] * 3 + [state_shape, state_shape],
        compiler_params=_cparams(("arbitrary",)),
        name="na_qkv",
    )(x, g, mod, mod, w_qkv)


def _ones_row_slots(vt):
    row = lax.broadcasted_iota(jnp.int32, vt.shape, 0)
    return jnp.where((row & (LANES - 1)) == NA_HEAD_DIM, 1.0, vt)


def _qkv_t_kernel(x_ref, g_ref, sh_ref, sc_ref, wqk_ref, wvt_ref, q_ref, k_ref, vt_ref):
    h = _norm_mod(x_ref[...], g_ref[...], sh_ref[0], sc_ref[0]).astype(BF16)
    y = jnp.dot(h, wqk_ref[...], preferred_element_type=F32)
    q_ref[...] = (y[:, :D_MODEL] * (NA_HEAD_DIM ** -0.5 * LOG2_E)).astype(q_ref.dtype)
    k_ref[...] = y[:, D_MODEL:].astype(k_ref.dtype)
    vt_ref[...] = _ones_row_slots(_qk(wvt_ref[...], h)).astype(vt_ref.dtype)


def _na_qkv_t(x, g, mod, layer, row0, tokens_per_row, w_qkv, w_vt):
    t = x.shape[0]
    tok = pl.BlockSpec((TM, D_MODEL), lambda i: (i, 0))
    slots = NA_HEADS * LANES
    return pl.pallas_call(
        _qkv_t_kernel,
        grid=(t // TM,),
        in_specs=[tok, _const_spec((1, D_MODEL)),
                  _mod_spec(layer, 0, row0, tokens_per_row), _mod_spec(layer, 1, row0, tokens_per_row),
                  pl.BlockSpec((D_MODEL, 2 * D_MODEL), lambda i: (0, 0)), _const_spec(w_vt.shape)],
        out_specs=[tok, tok, pl.BlockSpec((slots, TM), lambda i: (0, i))],
        out_shape=[jax.ShapeDtypeStruct((t, D_MODEL), BF16), jax.ShapeDtypeStruct((t, D_MODEL), BF16),
                   jax.ShapeDtypeStruct((slots, t), BF16)],
        compiler_params=_cparams(("arbitrary",)),
        name="na_qkv_t",
    )(x, g, mod, mod, w_qkv, w_vt)


def _softmax_pv(s_list, v_list):
    m = s_list[0].max(axis=-1, keepdims=True)
    for s in s_list[1:]:
        m = jnp.maximum(m, s.max(axis=-1, keepdims=True))
    l = None
    acc = None
    for s, v in zip(s_list, v_list):
        p = jnp.exp(s - m)
        ls = p.sum(axis=-1, keepdims=True)
        a = jnp.dot(p.astype(BF16), v, preferred_element_type=F32)
        l = ls if l is None else l + ls
        acc = a if acc is None else acc + a
    return acc * (1.0 / l)


def _qk(q, k):
    return lax.dot_general(q, k, (((1,), (1,)), ((), ())), preferred_element_type=F32)


def _head_masks():
    lane = lax.broadcasted_iota(jnp.int32, (1, LANES), 1)
    lo = lane < (LANES // 2)
    return lo, jnp.where(lo, 1.0, 0.0).astype(BF16), jnp.where(lo, 0.0, 1.0).astype(BF16)


def _attn_kernel(*refs, n_src, wide_qk, pairs):
    q_ref, o_ref = refs[0], refs[-1]
    kv = refs[1:-1]
    lo, m0, m1 = _head_masks()
    wq = 2 * LANES if wide_qk else LANES
    for p in range(pairs):
        q = q_ref[:, p * wq:(p + 1) * wq]
        ks = [kv[2 * t][:, p * wq:(p + 1) * wq].astype(BF16) for t in range(n_src)]
        vs = [kv[2 * t + 1][:, p * LANES:(p + 1) * LANES].astype(BF16) for t in range(n_src)]
        outs = []
        for j in range(2):
            if wide_qk:
                s_list = [_qk(q[:, j * LANES:(j + 1) * LANES], k[:, j * LANES:(j + 1) * LANES]) for k in ks]
            else:
                s_list = [_qk(q * (m0 if j == 0 else m1), k) for k in ks]
            outs.append(_softmax_pv(s_list, vs))
        o_ref[:, p * LANES:(p + 1) * LANES] = jnp.where(lo, outs[0], outs[1]).astype(o_ref.dtype)


def _attention(q, kv_srcs, n_batch, tq, wide_qk, pairs):
    wq = (2 * LANES if wide_qk else LANES) * pairs
    wv = LANES * pairs
    t_q = q.shape[0] // n_batch
    nq = t_q // tq
    n_steps = D_MODEL // wv
    in_specs = [pl.BlockSpec((tq, wq), lambda b, p, i: (b * nq + i, p))]
    args = [q]
    for k, v in kv_srcs:
        t_k = k.shape[0] // n_batch
        in_specs.append(pl.BlockSpec((t_k, wq), lambda b, p, i: (b, p)))
        in_specs.append(pl.BlockSpec((t_k, wv), lambda b, p, i: (b, p)))
        args += [k, v]
    return pl.pallas_call(
        functools.partial(_attn_kernel, n_src=len(kv_srcs), wide_qk=wide_qk, pairs=pairs),
        grid=(n_batch, n_steps, nq),
        in_specs=in_specs,
        out_specs=pl.BlockSpec((tq, wv), lambda b, p, i: (b * nq + i, p)),
        out_shape=jax.ShapeDtypeStruct((q.shape[0], D_MODEL), BF16),
        compiler_params=_cparams(("arbitrary", "arbitrary", "arbitrary")),
        name="attention",
    )(*args)


def _col_max(s, chunk=64):
    acc = s[:chunk]
    for r in range(chunk, s.shape[0], chunk):
        acc = jnp.maximum(acc, s[r:r + chunk])
    return acc.max(axis=0, keepdims=True)


def _attn_t_kernel(q_ref, k1_ref, k2_ref, v1_ref, v2_ref, o_ref):
    n_heads = q_ref.shape[1] // LANES
    slots = [slice(j * LANES, (j + 1) * LANES) for j in range(n_heads)]
    def scores(sl):
        return _qk(k1_ref[:, sl], q_ref[:, sl]), _qk(k2_ref[:, sl], q_ref[:, sl])

    nxt = scores(slots[0])
    outs = []
    for n, sl in enumerate(slots):
        s1, s2 = nxt
        if n + 1 < n_heads:
            nxt = scores(slots[n + 1])
        m = jnp.maximum(_col_max(s1), _col_max(s2))
        p1 = jnp.exp2(s1 - m).astype(BF16)
        p2 = jnp.exp2(s2 - m).astype(BF16)
        ot = (jnp.dot(v1_ref[sl, :], p1, preferred_element_type=F32)
              + jnp.dot(v2_ref[sl, :], p2, preferred_element_type=F32))
        outs.append(ot[:MLA_V_DIM] * (1.0 / ot[MLA_V_DIM:MLA_V_DIM + 1]))
    for j in range(0, n_heads, 2):
        pair = jnp.concatenate(outs[j:j + 2], axis=0).T
        o_ref[:, (j // 2) * LANES:(j // 2 + 1) * LANES] = pair.astype(o_ref.dtype)


def _attention_t(q, k1, v1t, k2, v2t, n_batch, tq, heads):
    wq = heads * LANES
    wo = heads * MLA_V_DIM
    nq = q.shape[0] // n_batch // tq
    t1 = k1.shape[0] // n_batch
    t2 = k2.shape[0] // n_batch
    return pl.pallas_call(
        _attn_t_kernel,
        grid=(n_batch, MLA_HEADS // heads, nq),
        in_specs=[pl.BlockSpec((tq, wq), lambda b, p, i: (b * nq + i, p)),
                  pl.BlockSpec((t1, wq), lambda b, p, i: (b, p)),
                  pl.BlockSpec((t2, wq), lambda b, p, i: (b, p)),
                  pl.BlockSpec((wq, t1), lambda b, p, i: (p, b)),
                  pl.BlockSpec((wq, t2), lambda b, p, i: (p, b))],
        out_specs=pl.BlockSpec((tq, wo), lambda b, p, i: (b * nq + i, p)),
        out_shape=jax.ShapeDtypeStruct((q.shape[0], D_MODEL), BF16),
        compiler_params=_cparams(("arbitrary", "arbitrary", "arbitrary")),
        name="attention_t",
    )(q, k1, k2, v1t, v2t)


def _na_group_layout(rows):
    n_groups = rows // NA_GROUP_ROWS
    kh = min(NA_KH, rows)
    bases, types, keys = [], [], {}
    for gi in range(n_groups):
        r0 = gi * NA_GROUP_ROWS
        base = int(np.clip(r0 - kh // 2, 0, rows - NA_WIN_ROWS))
        starts = tuple(int(np.clip(r - kh // 2, 0, rows - kh)) - base for r in range(r0, r0 + NA_GROUP_ROWS))
        key = (starts, r0 - base)
        if key not in keys:
            keys[key] = len(keys)
        bases.append(base)
        types.append(keys[key])
    return bases, types, list(keys.keys())


def _na_bias_tables(rpb, rows):
    kh = min(NA_KH, rows)
    _, _, type_keys = _na_group_layout(rows)
    qc = np.arange(GRID_W)
    kc = np.arange(GRID_W)
    col_start = np.clip(qc - NA_KW // 2, 0, GRID_W - NA_KW)
    col_valid = (kc[None, :] >= col_start[:, None]) & (kc[None, :] < col_start[:, None] + NA_KW)
    dx = kc[None, :] - qc[:, None] + (NA_KW - 1)
    onehot = ((dx[None] == np.arange(2 * NA_KW - 1)[:, None, None]) & col_valid[None]).astype(np.float32)
    band = jnp.einsum('hyd,dqk->hykq', rpb.astype(F32), onehot, precision=lax.Precision.HIGHEST)
    band = jnp.where(col_valid.T, band, NEG_INF)
    strip = band.reshape(NA_HEADS, (2 * NA_KH - 1) * GRID_W, GRID_W)
    tables = []
    for starts, r_off in type_keys:
        q_cols = []
        for rq in range(NA_GROUP_ROWS):
            dy0 = starts[rq] - (rq + r_off) + (NA_KH - 1)
            piece = strip[:, dy0 * GRID_W:(dy0 + kh) * GRID_W, :]
            pad = ((0, 0), (starts[rq] * GRID_W, (NA_WIN_ROWS - kh - starts[rq]) * GRID_W), (0, 0))
            q_cols.append(jnp.pad(piece, pad, constant_values=NEG_INF))
        tables.append(jnp.concatenate(q_cols, axis=2))
    return jnp.stack(tables, axis=1) * LOG2_E


def _state_k_kernel(x_ref, o_ref):
    for h in range(NA_HEADS):
        o_ref[:, h * NA_HEAD_DIM:(h + 1) * NA_HEAD_DIM] = x_ref[0, 0, :, h, :].astype(o_ref.dtype)


def _state_vt_kernel(x_ref, o_ref, flat_ref):
    for h in range(NA_HEADS):
        flat_ref[:, h * NA_HEAD_DIM:(h + 1) * NA_HEAD_DIM] = x_ref[0, 0, :, h, :]
    xt = flat_ref[...].T
    t = xt.shape[1]
    tail = jnp.where(lax.broadcasted_iota(jnp.int32, (LANES - NA_HEAD_DIM, t), 0) == 0, 1.0, 0.0)
    for h in range(NA_HEADS):
        o_ref[h * LANES:h * LANES + NA_HEAD_DIM, :] = xt[h * NA_HEAD_DIM:(h + 1) * NA_HEAD_DIM, :].astype(o_ref.dtype)
        o_ref[h * LANES + NA_HEAD_DIM:(h + 1) * LANES, :] = tail.astype(o_ref.dtype)


def _state_k(x, layer):
    b, _, t, h, dh = x.shape
    return pl.pallas_call(
        _state_k_kernel,
        grid=(b,),
        in_specs=[pl.BlockSpec((1, 1, t, h, dh), lambda i: (i, layer, 0, 0, 0))],
        out_specs=pl.BlockSpec((t, h * dh), lambda i: (i, 0)),
        out_shape=jax.ShapeDtypeStruct((b * t, h * dh), BF16),
        compiler_params=_cparams(("arbitrary",)),
        name="state_k",
    )(x)


def _state_vt(x, layer):
    b, _, t, h, dh = x.shape
    return pl.pallas_call(
        _state_vt_kernel,
        grid=(b,),
        in_specs=[pl.BlockSpec((1, 1, t, h, dh), lambda i: (i, layer, 0, 0, 0))],
        out_specs=pl.BlockSpec((h * LANES, t), lambda i: (0, i)),
        out_shape=jax.ShapeDtypeStruct((h * LANES, b * t), BF16),
        scratch_shapes=[pltpu.VMEM((t, h * dh), F32)],
        compiler_params=_cparams(("arbitrary",)),
        name="state_vt",
    )(x)


def _na_latent_kernel(q_ref, k_ref, vt_ref, kc_ref, vct_ref, tab_ref, o_ref, *, bases, types):
    _, m0, m1 = _head_masks()
    kctx = kc_ref[...]
    gq = NA_GROUP_ROWS * GRID_W
    gk = NA_WIN_ROWS * GRID_W
    blocks = [(gi, base, ty, j) for gi, (base, ty) in enumerate(zip(bases, types)) for j in range(2)]

    def scores(gi, base, ty, j):
        qj = q_ref[gi * gq:(gi + 1) * gq, :] * (m0 if j == 0 else m1)
        kwin = k_ref[base * GRID_W:base * GRID_W + gk, :]
        return _qk(kwin, qj) + tab_ref[j, ty], _qk(kctx, qj)

    nxt = scores(*blocks[0])
    outs = []
    for n, (gi, base, ty, j) in enumerate(blocks):
        s_win, s_ctx = nxt
        if n + 1 < len(blocks):
            nxt = scores(*blocks[n + 1])
        m = jnp.maximum(_col_max(s_win), _col_max(s_ctx))
        p_win = jnp.exp2(s_win - m).astype(BF16)
        p_ctx = jnp.exp2(s_ctx - m).astype(BF16)
        sl = slice(j * LANES, (j + 1) * LANES)
        ot = (jnp.dot(vt_ref[sl, base * GRID_W:base * GRID_W + gk], p_win, preferred_element_type=F32)
              + jnp.dot(vct_ref[sl, :], p_ctx, preferred_element_type=F32))
        outs.append(ot[:NA_HEAD_DIM] * (1.0 / ot[NA_HEAD_DIM:NA_HEAD_DIM + 1]))
        if j == 1:
            o_ref[gi * gq:(gi + 1) * gq, :] = jnp.concatenate(outs, axis=0).T.astype(o_ref.dtype)
            outs = []


def _na_latent(q, k, vt, k_ctx, vt_ctx, tables, n_batch):
    t = q.shape[0] // n_batch
    rows = t // GRID_W
    t_ctx = k_ctx.shape[0] // n_batch
    bases, types, type_keys = _na_group_layout(rows)
    n_types = len(type_keys)
    gq = NA_GROUP_ROWS * GRID_W
    gk = NA_WIN_ROWS * GRID_W
    tok = pl.BlockSpec((t, LANES), lambda p, b: (b, p))
    ctx = pl.BlockSpec((t_ctx, LANES), lambda p, b: (b, p))
    return pl.pallas_call(
        functools.partial(_na_latent_kernel, bases=bases, types=types),
        grid=(D_MODEL // LANES, n_batch),
        in_specs=[tok, tok, pl.BlockSpec((2 * LANES, t), lambda p, b: (p, b)),
                  ctx, pl.BlockSpec((2 * LANES, t_ctx), lambda p, b: (p, b)),
                  pl.BlockSpec((2, n_types, gk, gq), lambda p, b: (p, 0, 0, 0))],
        out_specs=tok,
        out_shape=jax.ShapeDtypeStruct(q.shape, BF16),
        compiler_params=_cparams(("arbitrary", "arbitrary")),
        name="na_latent",
    )(q, k, vt, k_ctx, vt_ctx, tables)


def _rope_lane_tables(n_tokens):
    n_freq = MLA_ROPE_DIM // 4
    inv = 1.0 / (ROPE_THETA ** (jnp.arange(n_freq, dtype=F32) / n_freq))
    tt = jnp.arange(n_tokens)
    row = (tt // GRID_W).astype(F32)
    col = (tt % GRID_W).astype(F32)
    ang = jnp.concatenate([row[:, None] * inv, col[:, None] * inv], axis=-1)
    cos, sin = jnp.cos(ang), jnp.sin(ang)
    ones = jnp.ones((n_tokens, MLA_NOPE_DIM), F32)
    z16 = jnp.zeros((n_tokens, MLA_ROPE_DIM // 2), F32)
    z32 = jnp.zeros((n_tokens, LANES - MLA_QK_DIM), F32)
    z64 = jnp.zeros((n_tokens, MLA_NOPE_DIM), F32)
    c_tab = jnp.concatenate([ones, cos, cos, z32], axis=-1)
    s_dn = jnp.concatenate([z64, -sin, z16, z32], axis=-1)
    s_up = jnp.concatenate([z64, z16, sin, z32], axis=-1)
    return c_tab, s_dn, s_up


def _rope_slot(x, c_tab, s_dn, s_up):
    half = MLA_ROPE_DIM // 2
    return x * c_tab + pltpu.roll(x, LANES - half, 1) * s_dn + pltpu.roll(x, half, 1) * s_up


def _mla_kv(ckv_n, kr_slot, wuk_ref, wuv_ref, k_ref, v_ref, v_transposed):
    c = ckv_n.astype(BF16)
    k_nope = jnp.dot(c, wuk_ref[...], preferred_element_type=F32)
    for h in range(MLA_HEADS):
        k_ref[:, h * LANES:(h + 1) * LANES] = (k_nope[:, h * LANES:(h + 1) * LANES] + kr_slot).astype(k_ref.dtype)
    if v_transposed:
        vt = _qk(wuv_ref[...], c)
        row = lax.broadcasted_iota(jnp.int32, vt.shape, 0)
        v_ref[...] = jnp.where((row & (LANES - 1)) == MLA_V_DIM, 1.0, vt).astype(v_ref.dtype)
    else:
        v_ref[...] = jnp.dot(c, wuv_ref[...], preferred_element_type=F32).astype(v_ref.dtype)


def _mla_proj_kernel(*refs, rope, emit_state, q_scale):
    (x_ref, g_ref, sh_ref, sc_ref, win_ref, qg_ref, kvg_ref, wuq_ref, wuk_ref, wuv_ref) = refs[:10]
    pos = 10
    if rope:
        ct_ref, sd_ref, su_ref = refs[pos:pos + 3]
        pos += 3
    q_ref, k_ref, v_ref = refs[pos:pos + 3]
    pos += 3
    h = _norm_mod(x_ref[...], g_ref[...], sh_ref[0], sc_ref[0]).astype(BF16)
    c = jnp.dot(h, win_ref[...], preferred_element_type=F32)
    cq = _rms(c[:, :MLA_Q_RANK], qg_ref[...]).astype(BF16)
    ckv = _rms(c[:, MLA_Q_RANK:MLA_Q_RANK + MLA_KV_RANK], kvg_ref[...])
    kr_chunk = c[:, MLA_Q_RANK + MLA_KV_RANK:]
    if emit_state:
        ckv_out, kr_out = refs[pos:pos + 2]
        ckv_out[...] = ckv
        kr_out[...] = kr_chunk[:, :MLA_ROPE_DIM]
    kr_slot = pltpu.roll(kr_chunk, MLA_NOPE_DIM, 1)
    q = jnp.dot(cq, wuq_ref[...], preferred_element_type=F32) * q_scale
    if rope:
        ct, sd, su = ct_ref[...], sd_ref[...], su_ref[...]
        kr_slot = _rope_slot(kr_slot, ct, sd, su)
    for hh in range(MLA_HEADS):
        qh = q[:, hh * LANES:(hh + 1) * LANES]
        if rope:
            qh = _rope_slot(qh, ct, sd, su)
        q_ref[:, hh * LANES:(hh + 1) * LANES] = qh.astype(q_ref.dtype)
    _mla_kv(ckv, kr_slot, wuk_ref, wuv_ref, k_ref, v_ref, v_transposed=rope)


def _mla_proj(x, g, mod, layer, row0, tokens_per_row, w, rope_tabs, emit_state):
    t = x.shape[0]
    win, qg, kvg, wuq, wuk, wuv = w
    tok = lambda n: pl.BlockSpec((TM, n), lambda i: (i, 0))
    wq = MLA_HEADS * LANES
    in_specs = [tok(D_MODEL), _const_spec((1, D_MODEL)),
                _mod_spec(layer, 0, row0, tokens_per_row), _mod_spec(layer, 1, row0, tokens_per_row),
                _const_spec(win.shape), _const_spec(qg.shape), _const_spec(kvg.shape),
                _const_spec(wuq.shape), _const_spec(wuk.shape), _const_spec(wuv.shape)]
    args = [x, g, mod, mod, win, qg, kvg, wuq, wuk, wuv]
    rope = rope_tabs is not None
    if rope:
        n_tab = rope_tabs[0].shape[0] // TM
        in_specs += [pl.BlockSpec((TM, LANES), lambda i: (i % n_tab, 0)) for _ in range(3)]
        args += list(rope_tabs)
    if rope:
        v_spec, v_shape = pl.BlockSpec((wq, TM), lambda i: (0, i)), jax.ShapeDtypeStruct((wq, t), BF16)
        q_scale = MLA_QK_DIM ** -0.5 * LOG2_E
    else:
        v_spec, v_shape = tok(D_MODEL), jax.ShapeDtypeStruct((t, D_MODEL), BF16)
        q_scale = MLA_QK_DIM ** -0.5
    out_specs = [tok(wq), tok(wq), v_spec]
    out_shape = [jax.ShapeDtypeStruct((t, wq), BF16), jax.ShapeDtypeStruct((t, wq), BF16), v_shape]
    if emit_state:
        out_specs += [tok(MLA_KV_RANK), tok(MLA_ROPE_DIM)]
        out_shape += [jax.ShapeDtypeStruct((t, MLA_KV_RANK), F32), jax.ShapeDtypeStruct((t, MLA_ROPE_DIM), F32)]
    return pl.pallas_call(
        functools.partial(_mla_proj_kernel, rope=rope, emit_state=emit_state, q_scale=q_scale),
        grid=(t // TM,),
        in_specs=in_specs, out_specs=out_specs, out_shape=out_shape,
        compiler_params=_cparams(("arbitrary",)),
        name="mla_proj",
    )(*args)


def _mla_state_kv_kernel(ckv_ref, kr_ref, wuk_ref, wuv_ref, k_ref, v_ref):
    kr_slot = pltpu.roll(kr_ref[...], MLA_NOPE_DIM, 1)
    _mla_kv(ckv_ref[...], kr_slot, wuk_ref, wuv_ref, k_ref, v_ref, v_transposed=True)


def _mla_state_kv(ckv, kr_pad, wuk, wuv_t):
    t = ckv.shape[0]
    wq = MLA_HEADS * LANES
    tok = lambda n: pl.BlockSpec((TM, n), lambda i: (i, 0))
    return pl.pallas_call(
        _mla_state_kv_kernel,
        grid=(t // TM,),
        in_specs=[tok(MLA_KV_RANK), tok(LANES), _const_spec(wuk.shape), _const_spec(wuv_t.shape)],
        out_specs=[tok(wq), pl.BlockSpec((wq, TM), lambda i: (0, i))],
        out_shape=[jax.ShapeDtypeStruct((t, wq), BF16), jax.ShapeDtypeStruct((wq, t), BF16)],
        compiler_params=_cparams(("arbitrary",)),
        name="mla_state_kv",
    )(ckv, kr_pad, wuk, wuv_t)


def _first_argmax_rows(v, row):
    top = v.max(axis=0, keepdims=True)
    idx = jnp.where(v == top, row, float(v.shape[0])).min(axis=0, keepdims=True)
    return top, idx


ROUTER_ROWS = 32
ROUTER_EXPERT_ROW = 8


def _route_t(lt):
    n = lt.shape[1]
    row8 = lax.broadcasted_iota(jnp.int32, (8, n), 0).astype(F32)
    row16 = lax.broadcasted_iota(jnp.int32, (N_EXPERTS, n), 0).astype(F32)
    is_g = row8 < N_GROUPS
    gl = jnp.where(is_g, lt[:8], -jnp.inf)
    gmax, gsel = _first_argmax_rows(gl, row8)
    p_group = 1.0 / jnp.where(is_g, jnp.exp(gl - gmax), 0.0).sum(axis=0, keepdims=True)
    first = gsel * EXPERTS_PER_GROUP
    e_all = lt[ROUTER_EXPERT_ROW:ROUTER_EXPERT_ROW + N_EXPERTS]
    el = jnp.where((row16 >= first) & (row16 < first + EXPERTS_PER_GROUP), e_all, -jnp.inf)
    t1, i1 = _first_argmax_rows(el, row16)
    t2, i2 = _first_argmax_rows(jnp.where(row16 == i1, -jnp.inf, el), row16)
    e2 = jnp.exp(t2 - t1)
    w1 = 1.0 / (1.0 + e2)
    gates = jnp.where(row16 == i1, w1 * p_group, jnp.where(row16 == i2, e2 * w1 * p_group, 0.0))
    rel = gates[:8] + gates[8:]
    rel = rel + pltpu.roll(rel, EXPERTS_PER_GROUP, 0)
    meta_t = jnp.where(row8 < EXPERTS_PER_GROUP, rel, jnp.where(row8 == META_GROUP_LANE, gsel, 0.0))
    onehot_t = jnp.where(row8 == gsel, 1.0, 0.0)
    return meta_t, onehot_t, row8, gsel


def _pack_halves(x):
    half = x.shape[1] // 2
    lo = lax.bitcast_convert_type(x[:, :half].astype(BF16).astype(F32), jnp.uint32)
    hi = lax.bitcast_convert_type(x[:, half:].astype(BF16).astype(F32), jnp.uint32)
    return (lo >> 16) | hi


def _unpack_halves(p):
    lo = lax.bitcast_convert_type(p << 16, F32)
    hi = lax.bitcast_convert_type(p & jnp.uint32(0xFFFF0000), F32)
    return lo, hi


ROW_WORDS = D_MODEL // 2 + LANES
META_GROUP_LANE = EXPERTS_PER_GROUP
META_RANK_LANE = EXPERTS_PER_GROUP + 1


def _proj_res_kernel(o_ref, x_ref, w_ref, gate_ref, g_ref, sh_ref, sc_ref, wrt_ref, brt_ref, tri_ref,
                     xn_ref, hpk_ref, metat_ref, cnt_ref):
    @pl.when(pl.program_id(0) == 0)
    def _():
        cnt_ref[...] = jnp.zeros(cnt_ref.shape, cnt_ref.dtype)

    sub = tri_ref.shape[0]
    cnt = cnt_ref[:, :1]
    for r0 in range(0, x_ref.shape[0], sub):
        rows = slice(r0, r0 + sub)
        y = jnp.dot(o_ref[rows, :], w_ref[...], preferred_element_type=F32)
        xn = x_ref[rows, :] + gate_ref[0] * y
        xn_ref[rows, :] = xn
        h = _norm_mod(xn, g_ref[...], sh_ref[0], sc_ref[0])
        lt = _qk(wrt_ref[...], h.astype(BF16)) + brt_ref[...]
        meta_t, onehot_t, row8, gsel = _route_t(lt)
        prefix = jnp.dot(onehot_t.astype(BF16), tri_ref[...], preferred_element_type=F32)
        rank = jnp.where(row8 == gsel, prefix + cnt - 1.0, 0.0).sum(axis=0, keepdims=True)
        cnt = cnt + onehot_t.sum(axis=1, keepdims=True)
        meta_t = jnp.where(row8 == META_RANK_LANE, rank, meta_t)
        meta = jnp.concatenate([meta_t, jnp.zeros((LANES - 8, sub), F32)], axis=0).T
        metat_ref[:, rows] = meta_t
        hpk_ref[rows, :] = jnp.concatenate([_pack_halves(h), lax.bitcast_convert_type(meta, jnp.uint32)], axis=1)
    cnt_ref[...] = jnp.broadcast_to(cnt, cnt_ref.shape)


def _proj_res(o, x, w_o, g_ffn, mod, layer, row0, tokens_per_row, w_router_t, b_router_t):
    t = x.shape[0]
    tm = TM_PROJ
    tok = lambda n: pl.BlockSpec((tm, n), lambda i: (i, 0))
    ms = lambda k: _mod_spec(layer, k, row0, tokens_per_row, tm)
    tri = jnp.asarray(np.triu(np.ones((TM, TM), np.float32)), BF16)
    return pl.pallas_call(
        _proj_res_kernel,
        grid=(t // tm,),
        in_specs=[tok(D_MODEL), tok(D_MODEL), _const_spec(w_o.shape), ms(2), _const_spec((1, D_MODEL)), ms(3), ms(4),
                  _const_spec(w_router_t.shape), _const_spec(b_router_t.shape), _const_spec((TM, TM))],
        out_specs=[tok(D_MODEL), tok(ROW_WORDS), pl.BlockSpec((8, tm), lambda i: (0, i)), _const_spec((8, LANES))],
        out_shape=[jax.ShapeDtypeStruct((t, D_MODEL), F32), jax.ShapeDtypeStruct((t, ROW_WORDS), jnp.uint32),
                   jax.ShapeDtypeStruct((8, t), F32), jax.ShapeDtypeStruct((8, LANES), F32)],
        compiler_params=_cparams(("arbitrary",)),
        name="proj_res",
    )(o, x, w_o, mod, g_ffn, mod, mod, w_router_t, b_router_t, tri)


def _moe_plan(meta_t, cnt):
    t = meta_t.shape[1]
    n_tiles = t // TM + N_GROUPS
    group = meta_t[META_GROUP_LANE].astype(jnp.int32)
    rank = meta_t[META_RANK_LANE].astype(jnp.int32)
    counts = cnt[:N_GROUPS, 0].astype(jnp.int32)
    tiles = (counts + TM - 1) // TM
    tile_end = jnp.cumsum(tiles)
    tile_start = tile_end - tiles
    pos = rank + TM * jnp.sum(jnp.where(group[:, None] == jnp.arange(N_GROUPS)[None], tile_start[None], 0), axis=1)
    tile_group = jnp.minimum(jnp.sum(jnp.arange(n_tiles)[:, None] >= tile_end[None], axis=1), N_GROUPS - 1)
    pad_bounds = jnp.stack([tile_start * TM + counts, tile_end * TM], axis=1).reshape(-1)
    return (pos.astype(jnp.int32), tile_group.astype(jnp.int32), tile_end[-1:].astype(jnp.int32),
            pad_bounds.astype(jnp.int32))


def _gather_rows(srcs, idx_ref, base, dsts, n_rows):
    for r in range(n_rows):
        s = idx_ref[base + r]
        for src_ref, dst_ref in zip(srcs, dsts):
            dst_ref[r:r + 1, :] = src_ref[pl.ds(s, 1), :]


def _moe_kernel(pos_ref, tg_ref, nt_ref, pad_ref, hpk_ref, wg_ref, wu_ref, wd_ref, ypk_ref, src_ref, hbuf, hid_ref):
    i = pl.program_id(0)
    n_tok = hpk_ref.shape[0]

    @pl.when(i == 0)
    def _():
        def clear(p, c):
            src_ref[p] = 0
            return c
        for g in range(N_GROUPS):
            lax.fori_loop(pad_ref[2 * g], pad_ref[2 * g + 1], clear, 0)

        def fill(t, c):
            src_ref[pos_ref[t]] = t
            return c
        lax.fori_loop(0, n_tok, fill, 0, unroll=8)
        _gather_rows((hpk_ref,), src_ref, 0, (hbuf.at[0],), TM)

    @pl.when(i < nt_ref[0])
    def _():
        slot = i % 2
        row = hbuf[slot]
        lo, hi = _unpack_halves(row[:, :D_MODEL // 2])
        h = jnp.concatenate([lo.astype(BF16), hi.astype(BF16)], axis=1)
        gates = lax.bitcast_convert_type(row[:, D_MODEL // 2:], F32)
        nxt = jnp.minimum(i + 1, nt_ref[0] - 1) * TM
        _gather_rows((hpk_ref,), src_ref, nxt, (hbuf.at[1 - slot],), TM)
        f = D_FF_EXPERT
        for j in range(EXPERTS_PER_GROUP):
            gate_act = _silu(jnp.dot(h, wg_ref[0, j], preferred_element_type=F32))
            hid = gate_act * jnp.dot(h, wu_ref[0, j], preferred_element_type=F32) * gates[:, j:j + 1]
            hid_ref[:, j * f:(j + 1) * f] = hid.astype(BF16)
        ypk_ref[...] = _pack_halves(jnp.dot(hid_ref[...], wd_ref[0], preferred_element_type=F32))

    @pl.when(i >= nt_ref[0])
    def _():
        ypk_ref[...] = jnp.zeros(ypk_ref.shape, ypk_ref.dtype)


def _moe(hpk, pos, tile_group, n_used, pad_bounds, wg, wu, wd, layer):
    t = hpk.shape[0]
    n_tiles = t // TM + N_GROUPS
    f = D_FF_EXPERT
    g0 = layer * N_GROUPS
    grid_spec = pltpu.PrefetchScalarGridSpec(
        num_scalar_prefetch=4,
        grid=(n_tiles,),
        in_specs=[
            pl.BlockSpec(hpk.shape, lambda i, *_: (0, 0), pipeline_mode=pl.Buffered(1)),
            pl.BlockSpec((1, EXPERTS_PER_GROUP, D_MODEL, f), lambda i, pos, tg, nt, pad: (g0 + tg[i], 0, 0, 0)),
            pl.BlockSpec((1, EXPERTS_PER_GROUP, D_MODEL, f), lambda i, pos, tg, nt, pad: (g0 + tg[i], 0, 0, 0)),
            pl.BlockSpec((1, EXPERTS_PER_GROUP * f, D_MODEL), lambda i, pos, tg, nt, pad: (g0 + tg[i], 0, 0)),
        ],
        out_specs=pl.BlockSpec((TM, D_MODEL // 2), lambda i, *_: (i, 0)),
        scratch_shapes=[pltpu.SMEM((n_tiles * TM,), jnp.int32),
                        pltpu.VMEM((2, TM, ROW_WORDS), jnp.uint32),
                        pltpu.VMEM((TM, EXPERTS_PER_GROUP * f), BF16)],
    )
    return pl.pallas_call(
        _moe_kernel,
        grid_spec=grid_spec,
        out_shape=jax.ShapeDtypeStruct((n_tiles * TM, D_MODEL // 2), jnp.uint32),
        compiler_params=_cparams(("arbitrary",)),
        name="moe",
    )(pos, tile_group, n_used, pad_bounds, hpk, wg, wu, wd)


def _combine_kernel(*refs, final_norm):
    pos_ref, ypk_ref, x_ref, gate_ref = refs[:4]
    o_ref, ybuf = refs[-2], refs[-1]
    _gather_rows((ypk_ref,), pos_ref, pl.program_id(0) * TM, (ybuf,), TM)
    lo, hi = _unpack_halves(ybuf[...])
    out = x_ref[...] + gate_ref[0] * jnp.concatenate([lo, hi], axis=1)
    if final_norm:
        out = _rms(out, refs[4][...])
    o_ref[...] = out


def _combine(ypk, pos, x, mod, layer, row0, tokens_per_row, final_g):
    t = x.shape[0]

    def mod_map(i, pos):
        row = row0 + (i * TM) // tokens_per_row
        return ((layer * MOD_ROWS + row) * N_MOD + 5, 0, 0)

    in_specs = [pl.BlockSpec(ypk.shape, lambda i, pos: (0, 0), pipeline_mode=pl.Buffered(1)),
                pl.BlockSpec((TM, D_MODEL), lambda i, pos: (i, 0)),
                pl.BlockSpec((1, 1, D_MODEL), mod_map)]
    args = [ypk, x, mod]
    if final_g is not None:
        in_specs.append(pl.BlockSpec((1, D_MODEL), lambda i, pos: (0, 0)))
        args.append(final_g)
    grid_spec = pltpu.PrefetchScalarGridSpec(
        num_scalar_prefetch=1,
        grid=(t // TM,),
        in_specs=in_specs,
        out_specs=pl.BlockSpec((TM, D_MODEL), lambda i, pos: (i, 0)),
        scratch_shapes=[pltpu.VMEM((TM, D_MODEL // 2), jnp.uint32)],
    )
    return pl.pallas_call(
        functools.partial(_combine_kernel, final_norm=final_g is not None),
        grid_spec=grid_spec,
        out_shape=jax.ShapeDtypeStruct((t, D_MODEL), F32),
        compiler_params=_cparams(("arbitrary",)),
        name="moe_combine",
    )(pos, *args)


def _pad_heads(w, used):
    k = w.shape[0]
    w = w.reshape(k, MLA_HEADS, used)
    return jnp.pad(w, ((0, 0), (0, 0), (0, LANES - used))).reshape(k, MLA_HEADS * LANES)


def kernel(x_prompt, x_sample, state_na_k, state_na_v, state_mla_ckv, state_mla_kr, c, c_ctx, ada_w, ada_b, norm_mix_g, norm_ffn_g, final_norm_g, na_w_qkv, na_w_o, na_rpb, mla_w_in, mla_q_norm_g, mla_kv_norm_g, mla_w_uq, mla_w_uk, mla_w_uv, mla_w_o, moe_w_group, moe_b_group, moe_w_expert, moe_b_expert, moe_w_gate, moe_w_up, moe_w_down):
    n_b, seq, d = x_prompt.shape
    dec_b, n_lat, _ = x_sample.shape
    past = state_na_k.shape[2]
    rows = n_lat // GRID_W
    assert d == D_MODEL and dec_b + 1 <= MOD_ROWS and n_lat % TM_PROJ == 0 and (n_b * seq) % TM_PROJ == 0

    cond8 = jnp.concatenate([c_ctx[None], c, jnp.zeros((MOD_ROWS - 1 - dec_b, d), F32)], axis=0)
    mod = _ada_mod(cond8, ada_w, ada_b)

    xp = x_prompt.reshape(n_b * seq, d)
    xs = x_sample.reshape(dec_b * n_lat, d)
    p_rows = (0, n_b * seq)
    s_rows = (1, n_lat)
    outs = {}

    f = D_FF_EXPERT
    wg = _to_bf16(moe_w_gate).reshape(DEPTH * N_GROUPS, EXPERTS_PER_GROUP, d, f)
    wu = _to_bf16(moe_w_up).reshape(DEPTH * N_GROUPS, EXPERTS_PER_GROUP, d, f)
    wd = _to_bf16(moe_w_down).reshape(DEPTH * N_GROUPS, EXPERTS_PER_GROUP * f, d)

    for i in range(DEPTH):
        j = i // 2
        g_mix = norm_mix_g[i][None]
        g_ffn = norm_ffn_g[i][None]
        if i % 2 == 0:
            w_qkv = _to_bf16(na_w_qkv[j])
            w_vt = jnp.pad(na_w_qkv[j][:, 2 * d:].T.reshape(NA_HEADS, NA_HEAD_DIM, d),
                           ((0, 0), (0, LANES - NA_HEAD_DIM), (0, 0))).reshape(NA_HEADS * LANES, d).astype(BF16)
            qp, kp, vp, outs['na_k'], outs['na_v'] = _na_qkv(xp, g_mix, mod, i, *p_rows, w_qkv, n_b)
            qs, ks, vs_t = _na_qkv_t(xs, g_mix, mod, i, *s_rows, w_qkv, w_vt)
            op = _attention(qp, [(kp, vp)], n_b, seq, wide_qk=False, pairs=8)
            tables = _na_bias_tables(na_rpb[j], rows)
            os_ = _na_latent(qs, ks, vs_t, _state_k(state_na_k, j), _state_vt(state_na_v, j), tables, dec_b)
            w_o = _to_bf16(na_w_o[j])
        else:
            pad_in = MLA_IN_PAD - mla_w_in.shape[-1]
            w_mla = (jnp.pad(mla_w_in[j], ((0, 0), (0, pad_in))).astype(BF16),
                     mla_q_norm_g[j][None], mla_kv_norm_g[j][None],
                     _pad_heads(mla_w_uq[j], MLA_QK_DIM).astype(BF16),
                     _pad_heads(mla_w_uk[j], MLA_NOPE_DIM).astype(BF16),
                     mla_w_uv[j].astype(BF16))
            wuv_t = _pad_heads(mla_w_uv[j], MLA_V_DIM).T.astype(BF16)
            w_mla_t = w_mla[:5] + (wuv_t,)
            qp, kp, vp, ckv_p, kr_p = _mla_proj(xp, g_mix, mod, i, *p_rows, w_mla, None, True)
            qs, ks, vs_t = _mla_proj(xs, g_mix, mod, i, *s_rows, w_mla_t, _rope_lane_tables(n_lat), False)
            outs['ckv'] = ckv_p.reshape(n_b, 1, seq, MLA_KV_RANK)
            outs['kr'] = kr_p.reshape(n_b, 1, seq, MLA_ROPE_DIM)
            kr_state = jnp.pad(state_mla_kr[:, j].reshape(dec_b * past, MLA_ROPE_DIM),
                               ((0, 0), (0, LANES - MLA_ROPE_DIM)))
            kc, vc_t = _mla_state_kv(state_mla_ckv[:, j].reshape(dec_b * past, MLA_KV_RANK), kr_state,
                                     w_mla[4], wuv_t)
            op = _attention(qp, [(kp, vp)], n_b, seq, wide_qk=True, pairs=8)
            os_ = _attention_t(qs, ks, vs_t, kc, vc_t, dec_b, 512, heads=4)
            w_o = _to_bf16(mla_w_o[j])

        pad_g = ((0, ROUTER_EXPERT_ROW - N_GROUPS), (0, 0))
        pad_e = ((0, ROUTER_ROWS - ROUTER_EXPERT_ROW - N_EXPERTS), (0, 0))
        w_router_t = jnp.concatenate([jnp.pad(moe_w_group[i].T, pad_g), jnp.pad(moe_w_expert[i].T, pad_e)],
                                     axis=0).astype(BF16)
        b_router_t = jnp.concatenate([jnp.pad(moe_b_group[i][:, None], pad_g),
                                      jnp.pad(moe_b_expert[i][:, None], pad_e)], axis=0)
        b_router_t = jnp.broadcast_to(b_router_t, (ROUTER_ROWS, TM))
        final_g = final_norm_g[None] if i == DEPTH - 1 else None
        new_x = []
        for o_att, x, rows_ in ((op, xp, p_rows), (os_, xs, s_rows)):
            xn, hpk, meta, cnt = _proj_res(o_att, x, w_o, g_ffn, mod, i, *rows_, w_router_t, b_router_t)
            pos, tile_group, n_used, pad_bounds = _moe_plan(meta, cnt)
            ypk = _moe(hpk, pos, tile_group, n_used, pad_bounds, wg, wu, wd, i)
            new_x.append(_combine(ypk, pos, xn, mod, i, *rows_, final_g))
        xp, xs = new_x

    return (xp.reshape(n_b, seq, d), xs.reshape(dec_b, n_lat, d),
            outs['na_k'], outs['na_v'], outs['ckv'], outs['kr'])
```

```python
import functools

import numpy as np
import jax
import jax.numpy as jnp
from jax import lax
from jax.experimental import pallas as pl
from jax.experimental.pallas import tpu as pltpu

F32 = jnp.float32
BF16 = jnp.bfloat16

D_MODEL = 1024
DEPTH = 2
GRID_W = 64
N_MOD = 6
EPS = 1e-6
NEG_INF = -1e30
NA_HEADS = 16
NA_HEAD_DIM = 64
NA_KH = 8
NA_KW = 16
NA_GROUP_ROWS = 4
NA_WIN_ROWS = 12
MLA_HEADS = 16
MLA_Q_RANK = 384
MLA_KV_RANK = 256
MLA_NOPE_DIM = 64
MLA_ROPE_DIM = 32
MLA_V_DIM = 64
MLA_QK_DIM = MLA_NOPE_DIM + MLA_ROPE_DIM
MLA_IN_PAD = 768
ROPE_THETA = 10000.0
LOG2_E = 1.4426950408889634
N_GROUPS = 4
EXPERTS_PER_GROUP = 4
N_EXPERTS = 16
D_FF_EXPERT = 256

LANES = 128
MOD_ROWS = 8
VMEM_LIMIT = 56 * 1024 * 1024
TM = 512
CAST_BLOCK_ELEMS = 512 * 1024
TM_PROJ = 1024


def _cparams(sem):
    return pltpu.CompilerParams(dimension_semantics=sem, vmem_limit_bytes=VMEM_LIMIT)


def _const_spec(shape):
    nd = len(shape)
    return pl.BlockSpec(shape, lambda *_: (0,) * nd)


def _cast_kernel(x_ref, o_ref):
    o_ref[...] = x_ref[...].astype(o_ref.dtype)


def _to_bf16(x):
    cols = x.shape[-1]
    x2 = x.reshape(-1, cols)
    rows = x2.shape[0]
    block = rows
    while block * cols > CAST_BLOCK_ELEMS and block % 32 == 0:
        block //= 2
    out = pl.pallas_call(
        _cast_kernel,
        grid=(rows // block,),
        in_specs=[pl.BlockSpec((block, cols), lambda i: (i, 0))],
        out_specs=pl.BlockSpec((block, cols), lambda i: (i, 0)),
        out_shape=jax.ShapeDtypeStruct((rows, cols), BF16),
        compiler_params=_cparams(("arbitrary",)),
        name="to_bf16",
    )(x2)
    return out.reshape(x.shape)


def _norm_mod(x, g, shift, scale):
    ms = jnp.mean(x * x, axis=-1, keepdims=True)
    y = x * lax.rsqrt(ms + EPS) * g
    return y * (1.0 + scale) + shift


def _rms(x, g):
    ms = jnp.mean(x * x, axis=-1, keepdims=True)
    return x * lax.rsqrt(ms + EPS) * g


def _silu(x):
    return x / (1.0 + jnp.exp(-x))


def _ada_kernel(cond_ref, w_ref, b_ref, o_ref):
    s = _silu(cond_ref[...]).astype(BF16)
    o_ref[0] = jnp.dot(s, w_ref[0].astype(BF16), preferred_element_type=F32) + b_ref[0]


def _ada_mod(cond8, ada_w, ada_b):
    tn = 1024
    n = N_MOD * D_MODEL
    out = pl.pallas_call(
        _ada_kernel,
        grid=(DEPTH, n // tn),
        in_specs=[
            _const_spec((MOD_ROWS, D_MODEL)),
            pl.BlockSpec((1, D_MODEL, tn), lambda l, j: (l, 0, j)),
            pl.BlockSpec((1, 1, tn), lambda l, j: (l, 0, j)),
        ],
        out_specs=pl.BlockSpec((1, MOD_ROWS, tn), lambda l, j: (l, 0, j)),
        out_shape=jax.ShapeDtypeStruct((DEPTH, MOD_ROWS, n), F32),
        compiler_params=_cparams(("arbitrary", "arbitrary")),
        name="ada_mod",
    )(cond8, ada_w, ada_b.reshape(DEPTH, 1, n))
    return out.reshape(DEPTH * MOD_ROWS * N_MOD, 1, D_MODEL)


def _mod_spec(layer, k, row0, tokens_per_row, tm=TM):
    def imap(i):
        row = row0 + (i * tm) // tokens_per_row
        return ((layer * MOD_ROWS + row) * N_MOD + k, 0, 0)
    return pl.BlockSpec((1, 1, D_MODEL), imap)


def _qkv_kernel(x_ref, g_ref, sh_ref, sc_ref, w_ref, q_ref, k_ref, v_ref, ks_ref, vs_ref):
    h = _norm_mod(x_ref[...], g_ref[...], sh_ref[0], sc_ref[0]).astype(BF16)
    y = jnp.dot(h, w_ref[...], preferred_element_type=F32)
    q_ref[...] = (y[:, :D_MODEL] * (NA_HEAD_DIM ** -0.5)).astype(q_ref.dtype)
    k = y[:, D_MODEL:2 * D_MODEL]
    v = y[:, 2 * D_MODEL:]
    k_ref[...] = k.astype(k_ref.dtype)
    v_ref[...] = v.astype(v_ref.dtype)
    ks_ref[...] = pltpu.einshape("t(hd)->thd", k, h=NA_HEADS).reshape(ks_ref.shape)
    vs_ref[...] = pltpu.einshape("t(hd)->thd", v, h=NA_HEADS).reshape(vs_ref.shape)


def _na_qkv(x, g, mod, layer, row0, tokens_per_row, w_qkv, n_batch):
    t = x.shape[0]
    seq = t // n_batch
    nb = TM // seq
    tok = pl.BlockSpec((TM, D_MODEL), lambda i: (i, 0))
    state = pl.BlockSpec((nb, 1, seq, NA_HEADS, NA_HEAD_DIM), lambda i: (i, 0, 0, 0, 0))
    state_shape = jax.ShapeDtypeStruct((n_batch, 1, seq, NA_HEADS, NA_HEAD_DIM), F32)
    return pl.pallas_call(
        _qkv_kernel,
        grid=(t // TM,),
        in_specs=[tok, _const_spec((1, D_MODEL)),
                  _mod_spec(layer, 0, row0, tokens_per_row), _mod_spec(layer, 1, row0, tokens_per_row),
                  _const_spec((D_MODEL, 3 * D_MODEL))],
        out_specs=[tok, tok, tok, state, state],
        out_shape=[jax.ShapeDtypeStruct((t, D_MODEL), BF16)] * 3 + [state_shape, state_shape],
        compiler_params=_cparams(("arbitrary",)),
        name="na_qkv",
    )(x, g, mod, mod, w_qkv)


def _store_head_slots(vt, vt_ref):
    n = vt.shape[1]
    n_heads = vt_ref.shape[0] // LANES
    dim = vt.shape[0] // n_heads
    tail = jnp.where(lax.broadcasted_iota(jnp.int32, (LANES - dim, n), 0) == 0, 1.0, 0.0).astype(vt_ref.dtype)
    for hh in range(n_heads):
        vt_ref[hh * LANES:hh * LANES + dim, :] = vt[hh * dim:(hh + 1) * dim, :].astype(vt_ref.dtype)
        vt_ref[hh * LANES + dim:(hh + 1) * LANES, :] = tail


def _qkv_t_kernel(x_ref, g_ref, sh_ref, sc_ref, wqk_ref, wvt_ref, q_ref, k_ref, vt_ref):
    h = _norm_mod(x_ref[...], g_ref[...], sh_ref[0], sc_ref[0]).astype(BF16)
    y = jnp.dot(h, wqk_ref[...], preferred_element_type=F32)
    q_ref[...] = (y[:, :D_MODEL] * (NA_HEAD_DIM ** -0.5 * LOG2_E)).astype(q_ref.dtype)
    k_ref[...] = y[:, D_MODEL:].astype(k_ref.dtype)
    _store_head_slots(_qk(wvt_ref[...], h), vt_ref)


def _na_qkv_t(x, g, mod, layer, row0, tokens_per_row, w_qkv, w_vt):
    t = x.shape[0]
    tok = pl.BlockSpec((TM, D_MODEL), lambda i: (i, 0))
    slots = NA_HEADS * LANES
    return pl.pallas_call(
        _qkv_t_kernel,
        grid=(t // TM,),
        in_specs=[tok, _const_spec((1, D_MODEL)),
                  _mod_spec(layer, 0, row0, tokens_per_row), _mod_spec(layer, 1, row0, tokens_per_row),
                  pl.BlockSpec((D_MODEL, 2 * D_MODEL), lambda i: (0, 0)), _const_spec(w_vt.shape)],
        out_specs=[tok, tok, pl.BlockSpec((slots, TM), lambda i: (0, i))],
        out_shape=[jax.ShapeDtypeStruct((t, D_MODEL), BF16), jax.ShapeDtypeStruct((t, D_MODEL), BF16),
                   jax.ShapeDtypeStruct((slots, t), BF16)],
        compiler_params=_cparams(("arbitrary",)),
        name="na_qkv_t",
    )(x, g, mod, mod, w_qkv, w_vt)


def _softmax_pv(s_list, v_list):
    m = s_list[0].max(axis=-1, keepdims=True)
    for s in s_list[1:]:
        m = jnp.maximum(m, s.max(axis=-1, keepdims=True))
    l = None
    acc = None
    for s, v in zip(s_list, v_list):
        p = jnp.exp(s - m)
        ls = p.sum(axis=-1, keepdims=True)
        a = jnp.dot(p.astype(BF16), v, preferred_element_type=F32)
        l = ls if l is None else l + ls
        acc = a if acc is None else acc + a
    return acc * (1.0 / l)


def _qk(q, k):
    return lax.dot_general(q, k, (((1,), (1,)), ((), ())), preferred_element_type=F32)


def _head_masks():
    lane = lax.broadcasted_iota(jnp.int32, (1, LANES), 1)
    lo = lane < (LANES // 2)
    return lo, jnp.where(lo, 1.0, 0.0).astype(BF16), jnp.where(lo, 0.0, 1.0).astype(BF16)


def _attn_kernel(*refs, n_src, wide_qk, pairs):
    q_ref, o_ref = refs[0], refs[-1]
    kv = refs[1:-1]
    lo, m0, m1 = _head_masks()
    wq = 2 * LANES if wide_qk else LANES
    for p in range(pairs):
        q = q_ref[:, p * wq:(p + 1) * wq]
        ks = [kv[2 * t][:, p * wq:(p + 1) * wq].astype(BF16) for t in range(n_src)]
        vs = [kv[2 * t + 1][:, p * LANES:(p + 1) * LANES].astype(BF16) for t in range(n_src)]
        outs = []
        for j in range(2):
            if wide_qk:
                s_list = [_qk(q[:, j * LANES:(j + 1) * LANES], k[:, j * LANES:(j + 1) * LANES]) for k in ks]
            else:
                s_list = [_qk(q * (m0 if j == 0 else m1), k) for k in ks]
            outs.append(_softmax_pv(s_list, vs))
        o_ref[:, p * LANES:(p + 1) * LANES] = jnp.where(lo, outs[0], outs[1]).astype(o_ref.dtype)


def _attention(q, kv_srcs, n_batch, tq, wide_qk, pairs):
    wq = (2 * LANES if wide_qk else LANES) * pairs
    wv = LANES * pairs
    t_q = q.shape[0] // n_batch
    nq = t_q // tq
    n_steps = D_MODEL // wv
    in_specs = [pl.BlockSpec((tq, wq), lambda b, p, i: (b * nq + i, p))]
    args = [q]
    for k, v in kv_srcs:
        t_k = k.shape[0] // n_batch
        in_specs.append(pl.BlockSpec((t_k, wq), lambda b, p, i: (b, p)))
        in_specs.append(pl.BlockSpec((t_k, wv), lambda b, p, i: (b, p)))
        args += [k, v]
    return pl.pallas_call(
        functools.partial(_attn_kernel, n_src=len(kv_srcs), wide_qk=wide_qk, pairs=pairs),
        grid=(n_batch, n_steps, nq),
        in_specs=in_specs,
        out_specs=pl.BlockSpec((tq, wv), lambda b, p, i: (b * nq + i, p)),
        out_shape=jax.ShapeDtypeStruct((q.shape[0], D_MODEL), BF16),
        compiler_params=_cparams(("arbitrary", "arbitrary", "arbitrary")),
        name="attention",
    )(*args)


def _col_max(s, chunk=64):
    acc = s[:chunk]
    for r in range(chunk, s.shape[0], chunk):
        acc = jnp.maximum(acc, s[r:r + chunk])
    return acc.max(axis=0, keepdims=True)


def _attn_t_kernel(q_ref, k1_ref, k2_ref, v1_ref, v2_ref, o_ref):
    n_heads = q_ref.shape[1] // LANES
    slots = [slice(j * LANES, (j + 1) * LANES) for j in range(n_heads)]
    def scores(sl):
        return _qk(k1_ref[:, sl], q_ref[:, sl]), _qk(k2_ref[:, sl], q_ref[:, sl])

    nxt = scores(slots[0])
    outs = []
    for n, sl in enumerate(slots):
        s1, s2 = nxt
        if n + 1 < n_heads:
            nxt = scores(slots[n + 1])
        m = jnp.maximum(_col_max(s1), _col_max(s2))
        p1 = jnp.exp2(s1 - m).astype(BF16)
        p2 = jnp.exp2(s2 - m).astype(BF16)
        ot = (jnp.dot(v1_ref[sl, :], p1, preferred_element_type=F32)
              + jnp.dot(v2_ref[sl, :], p2, preferred_element_type=F32))
        outs.append(ot[:MLA_V_DIM] * (1.0 / ot[MLA_V_DIM:MLA_V_DIM + 1]))
    for j in range(0, n_heads, 2):
        pair = jnp.concatenate(outs[j:j + 2], axis=0).T
        o_ref[:, (j // 2) * LANES:(j // 2 + 1) * LANES] = pair.astype(o_ref.dtype)


def _attention_t(q, k1, v1t, k2, v2t, n_batch, tq, heads):
    wq = heads * LANES
    wo = heads * MLA_V_DIM
    nq = q.shape[0] // n_batch // tq
    t1 = k1.shape[0] // n_batch
    t2 = k2.shape[0] // n_batch
    return pl.pallas_call(
        _attn_t_kernel,
        grid=(n_batch, MLA_HEADS // heads, nq),
        in_specs=[pl.BlockSpec((tq, wq), lambda b, p, i: (b * nq + i, p)),
                  pl.BlockSpec((t1, wq), lambda b, p, i: (b, p)),
                  pl.BlockSpec((t2, wq), lambda b, p, i: (b, p)),
                  pl.BlockSpec((wq, t1), lambda b, p, i: (p, b)),
                  pl.BlockSpec((wq, t2), lambda b, p, i: (p, b))],
        out_specs=pl.BlockSpec((tq, wo), lambda b, p, i: (b * nq + i, p)),
        out_shape=jax.ShapeDtypeStruct((q.shape[0], D_MODEL), BF16),
        compiler_params=_cparams(("arbitrary", "arbitrary", "arbitrary")),
        name="attention_t",
    )(q, k1, k2, v1t, v2t)


def _na_group_layout(rows):
    n_groups = rows // NA_GROUP_ROWS
    kh = min(NA_KH, rows)
    bases, types, keys = [], [], {}
    for gi in range(n_groups):
        r0 = gi * NA_GROUP_ROWS
        base = int(np.clip(r0 - kh // 2, 0, rows - NA_WIN_ROWS))
        starts = tuple(int(np.clip(r - kh // 2, 0, rows - kh)) - base for r in range(r0, r0 + NA_GROUP_ROWS))
        key = (starts, r0 - base)
        if key not in keys:
            keys[key] = len(keys)
        bases.append(base)
        types.append(keys[key])
    return bases, types, list(keys.keys())


def _na_bias_tables(rpb, rows):
    kh = min(NA_KH, rows)
    _, _, type_keys = _na_group_layout(rows)
    qc = np.arange(GRID_W)
    kc = np.arange(GRID_W)
    col_start = np.clip(qc - NA_KW // 2, 0, GRID_W - NA_KW)
    col_valid = (kc[None, :] >= col_start[:, None]) & (kc[None, :] < col_start[:, None] + NA_KW)
    dx = kc[None, :] - qc[:, None] + (NA_KW - 1)
    onehot = ((dx[None] == np.arange(2 * NA_KW - 1)[:, None, None]) & col_valid[None]).astype(np.float32)
    band = jnp.einsum('hyd,dqk->hykq', rpb.astype(F32), onehot, precision=lax.Precision.HIGHEST)
    band = jnp.where(col_valid.T, band, NEG_INF)
    strip = band.reshape(NA_HEADS, (2 * NA_KH - 1) * GRID_W, GRID_W)
    tables = []
    for starts, r_off in type_keys:
        q_cols = []
        for rq in range(NA_GROUP_ROWS):
            dy0 = starts[rq] - (rq + r_off) + (NA_KH - 1)
            piece = strip[:, dy0 * GRID_W:(dy0 + kh) * GRID_W, :]
            pad = ((0, 0), (starts[rq] * GRID_W, (NA_WIN_ROWS - kh - starts[rq]) * GRID_W), (0, 0))
            q_cols.append(jnp.pad(piece, pad, constant_values=NEG_INF))
        tables.append(jnp.concatenate(q_cols, axis=2))
    return jnp.stack(tables, axis=1) * LOG2_E


def _state_k_kernel(x_ref, o_ref):
    for h in range(NA_HEADS):
        o_ref[:, h * NA_HEAD_DIM:(h + 1) * NA_HEAD_DIM] = x_ref[0, 0, :, h, :].astype(o_ref.dtype)


def _state_vt_kernel(x_ref, o_ref, flat_ref):
    for h in range(NA_HEADS):
        flat_ref[:, h * NA_HEAD_DIM:(h + 1) * NA_HEAD_DIM] = x_ref[0, 0, :, h, :]
    _store_head_slots(flat_ref[...].T, o_ref)


def _state_k(x, layer):
    b, _, t, h, dh = x.shape
    return pl.pallas_call(
        _state_k_kernel,
        grid=(b,),
        in_specs=[pl.BlockSpec((1, 1, t, h, dh), lambda i: (i, layer, 0, 0, 0))],
        out_specs=pl.BlockSpec((t, h * dh), lambda i: (i, 0)),
        out_shape=jax.ShapeDtypeStruct((b * t, h * dh), BF16),
        compiler_params=_cparams(("arbitrary",)),
        name="state_k",
    )(x)


def _state_vt(x, layer):
    b, _, t, h, dh = x.shape
    return pl.pallas_call(
        _state_vt_kernel,
        grid=(b,),
        in_specs=[pl.BlockSpec((1, 1, t, h, dh), lambda i: (i, layer, 0, 0, 0))],
        out_specs=pl.BlockSpec((h * LANES, t), lambda i: (0, i)),
        out_shape=jax.ShapeDtypeStruct((h * LANES, b * t), BF16),
        scratch_shapes=[pltpu.VMEM((t, h * dh), F32)],
        compiler_params=_cparams(("arbitrary",)),
        name="state_vt",
    )(x)


def _na_latent_kernel(q_ref, k_ref, vt_ref, kc_ref, vct_ref, tab_ref, o_ref, *, bases, types):
    _, m0, m1 = _head_masks()
    kctx = kc_ref[...]
    gq = NA_GROUP_ROWS * GRID_W
    gk = NA_WIN_ROWS * GRID_W
    blocks = [(gi, base, ty, j) for gi, (base, ty) in enumerate(zip(bases, types)) for j in range(2)]

    def scores(gi, base, ty, j):
        qj = q_ref[gi * gq:(gi + 1) * gq, :] * (m0 if j == 0 else m1)
        kwin = k_ref[base * GRID_W:base * GRID_W + gk, :]
        return _qk(kwin, qj) + tab_ref[j, ty], _qk(kctx, qj)

    nxt = scores(*blocks[0])
    outs = []
    for n, (gi, base, ty, j) in enumerate(blocks):
        s_win, s_ctx = nxt
        if n + 1 < len(blocks):
            nxt = scores(*blocks[n + 1])
        m = jnp.maximum(_col_max(s_win), _col_max(s_ctx))
        p_win = jnp.exp2(s_win - m).astype(BF16)
        p_ctx = jnp.exp2(s_ctx - m).astype(BF16)
        sl = slice(j * LANES, (j + 1) * LANES)
        ot = (jnp.dot(vt_ref[sl, base * GRID_W:base * GRID_W + gk], p_win, preferred_element_type=F32)
              + jnp.dot(vct_ref[sl, :], p_ctx, preferred_element_type=F32))
        outs.append(ot[:NA_HEAD_DIM] * (1.0 / ot[NA_HEAD_DIM:NA_HEAD_DIM + 1]))
        if j == 1:
            o_ref[gi * gq:(gi + 1) * gq, :] = jnp.concatenate(outs, axis=0).T.astype(o_ref.dtype)
            outs = []


def _na_latent(q, k, vt, k_ctx, vt_ctx, tables, n_batch):
    t = q.shape[0] // n_batch
    rows = t // GRID_W
    t_ctx = k_ctx.shape[0] // n_batch
    bases, types, type_keys = _na_group_layout(rows)
    n_types = len(type_keys)
    gq = NA_GROUP_ROWS * GRID_W
    gk = NA_WIN_ROWS * GRID_W
    tok = pl.BlockSpec((t, LANES), lambda p, b: (b, p))
    ctx = pl.BlockSpec((t_ctx, LANES), lambda p, b: (b, p))
    return pl.pallas_call(
        functools.partial(_na_latent_kernel, bases=bases, types=types),
        grid=(D_MODEL // LANES, n_batch),
        in_specs=[tok, tok, pl.BlockSpec((2 * LANES, t), lambda p, b: (p, b)),
                  ctx, pl.BlockSpec((2 * LANES, t_ctx), lambda p, b: (p, b)),
                  pl.BlockSpec((2, n_types, gk, gq), lambda p, b: (p, 0, 0, 0))],
        out_specs=tok,
        out_shape=jax.ShapeDtypeStruct(q.shape, BF16),
        compiler_params=_cparams(("arbitrary", "arbitrary")),
        name="na_latent",
    )(q, k, vt, k_ctx, vt_ctx, tables)


def _rope_lane_tables(n_tokens):
    n_freq = MLA_ROPE_DIM // 4
    inv = 1.0 / (ROPE_THETA ** (jnp.arange(n_freq, dtype=F32) / n_freq))
    tt = jnp.arange(n_tokens)
    row = (tt // GRID_W).astype(F32)
    col = (tt % GRID_W).astype(F32)
    ang = jnp.concatenate([row[:, None] * inv, col[:, None] * inv], axis=-1)
    cos, sin = jnp.cos(ang), jnp.sin(ang)
    ones = jnp.ones((n_tokens, MLA_NOPE_DIM), F32)
    z16 = jnp.zeros((n_tokens, MLA_ROPE_DIM // 2), F32)
    z32 = jnp.zeros((n_tokens, LANES - MLA_QK_DIM), F32)
    z64 = jnp.zeros((n_tokens, MLA_NOPE_DIM), F32)
    c_tab = jnp.concatenate([ones, cos, cos, z32], axis=-1)
    s_dn = jnp.concatenate([z64, -sin, z16, z32], axis=-1)
    s_up = jnp.concatenate([z64, z16, sin, z32], axis=-1)
    return c_tab, s_dn, s_up


def _rope_slot(x, c_tab, s_dn, s_up):
    half = MLA_ROPE_DIM // 2
    return x * c_tab + pltpu.roll(x, LANES - half, 1) * s_dn + pltpu.roll(x, half, 1) * s_up


def _mla_kv(ckv_n, kr_slot, wuk_ref, wuv_ref, k_ref, v_ref, v_transposed):
    c = ckv_n.astype(BF16)
    k_nope = jnp.dot(c, wuk_ref[...], preferred_element_type=F32)
    for h in range(MLA_HEADS):
        k_ref[:, h * LANES:(h + 1) * LANES] = (k_nope[:, h * LANES:(h + 1) * LANES] + kr_slot).astype(k_ref.dtype)
    if v_transposed:
        _store_head_slots(_qk(wuv_ref[...], c), v_ref)
    else:
        v_ref[...] = jnp.dot(c, wuv_ref[...], preferred_element_type=F32).astype(v_ref.dtype)


def _mla_proj_kernel(*refs, rope, emit_state, q_scale):
    (x_ref, g_ref, sh_ref, sc_ref, win_ref, qg_ref, kvg_ref, wuq_ref, wuk_ref, wuv_ref) = refs[:10]
    pos = 10
    if rope:
        ct_ref, sd_ref, su_ref = refs[pos:pos + 3]
        pos += 3
    q_ref, k_ref, v_ref = refs[pos:pos + 3]
    pos += 3
    h = _norm_mod(x_ref[...], g_ref[...], sh_ref[0], sc_ref[0]).astype(BF16)
    c = jnp.dot(h, win_ref[...], preferred_element_type=F32)
    cq = _rms(c[:, :MLA_Q_RANK], qg_ref[...]).astype(BF16)
    ckv = _rms(c[:, MLA_Q_RANK:MLA_Q_RANK + MLA_KV_RANK], kvg_ref[...])
    kr_chunk = c[:, MLA_Q_RANK + MLA_KV_RANK:]
    if emit_state:
        ckv_out, kr_out = refs[pos:pos + 2]
        ckv_out[...] = ckv
        kr_out[...] = kr_chunk[:, :MLA_ROPE_DIM]
    kr_slot = pltpu.roll(kr_chunk, MLA_NOPE_DIM, 1)
    q = jnp.dot(cq, wuq_ref[...], preferred_element_type=F32) * q_scale
    if rope:
        ct, sd, su = ct_ref[...], sd_ref[...], su_ref[...]
        kr_slot = _rope_slot(kr_slot, ct, sd, su)
    for hh in range(MLA_HEADS):
        qh = q[:, hh * LANES:(hh + 1) * LANES]
        if rope:
            qh = _rope_slot(qh, ct, sd, su)
        q_ref[:, hh * LANES:(hh + 1) * LANES] = qh.astype(q_ref.dtype)
    _mla_kv(ckv, kr_slot, wuk_ref, wuv_ref, k_ref, v_ref, v_transposed=rope)


def _mla_proj(x, g, mod, layer, row0, tokens_per_row, w, rope_tabs, emit_state):
    t = x.shape[0]
    win, qg, kvg, wuq, wuk, wuv = w
    tok = lambda n: pl.BlockSpec((TM, n), lambda i: (i, 0))
    wq = MLA_HEADS * LANES
    in_specs = [tok(D_MODEL), _const_spec((1, D_MODEL)),
                _mod_spec(layer, 0, row0, tokens_per_row), _mod_spec(layer, 1, row0, tokens_per_row),
                _const_spec(win.shape), _const_spec(qg.shape), _const_spec(kvg.shape),
                _const_spec(wuq.shape), _const_spec(wuk.shape), _const_spec(wuv.shape)]
    args = [x, g, mod, mod, win, qg, kvg, wuq, wuk, wuv]
    rope = rope_tabs is not None
    if rope:
        n_tab = rope_tabs[0].shape[0] // TM
        in_specs += [pl.BlockSpec((TM, LANES), lambda i: (i % n_tab, 0)) for _ in range(3)]
        args += list(rope_tabs)
    if rope:
        v_spec, v_shape = pl.BlockSpec((wq, TM), lambda i: (0, i)), jax.ShapeDtypeStruct((wq, t), BF16)
        q_scale = MLA_QK_DIM ** -0.5 * LOG2_E
    else:
        v_spec, v_shape = tok(D_MODEL), jax.ShapeDtypeStruct((t, D_MODEL), BF16)
        q_scale = MLA_QK_DIM ** -0.5
    out_specs = [tok(wq), tok(wq), v_spec]
    out_shape = [jax.ShapeDtypeStruct((t, wq), BF16), jax.ShapeDtypeStruct((t, wq), BF16), v_shape]
    if emit_state:
        out_specs += [tok(MLA_KV_RANK), tok(MLA_ROPE_DIM)]
        out_shape += [jax.ShapeDtypeStruct((t, MLA_KV_RANK), F32), jax.ShapeDtypeStruct((t, MLA_ROPE_DIM), F32)]
    return pl.pallas_call(
        functools.partial(_mla_proj_kernel, rope=rope, emit_state=emit_state, q_scale=q_scale),
        grid=(t // TM,),
        in_specs=in_specs, out_specs=out_specs, out_shape=out_shape,
        compiler_params=_cparams(("arbitrary",)),
        name="mla_proj",
    )(*args)


def _mla_state_kv_kernel(ckv_ref, kr_ref, wuk_ref, wuv_ref, k_ref, v_ref):
    kr_slot = pltpu.roll(kr_ref[...], MLA_NOPE_DIM, 1)
    _mla_kv(ckv_ref[...], kr_slot, wuk_ref, wuv_ref, k_ref, v_ref, v_transposed=True)


def _mla_state_kv(ckv, kr_pad, wuk, wuv_t):
    t = ckv.shape[0]
    wq = MLA_HEADS * LANES
    tok = lambda n: pl.BlockSpec((TM, n), lambda i: (i, 0))
    return pl.pallas_call(
        _mla_state_kv_kernel,
        grid=(t // TM,),
        in_specs=[tok(MLA_KV_RANK), tok(LANES), _const_spec(wuk.shape), _const_spec(wuv_t.shape)],
        out_specs=[tok(wq), pl.BlockSpec((wq, TM), lambda i: (0, i))],
        out_shape=[jax.ShapeDtypeStruct((t, wq), BF16), jax.ShapeDtypeStruct((wq, t), BF16)],
        compiler_params=_cparams(("arbitrary",)),
        name="mla_state_kv",
    )(ckv, kr_pad, wuk, wuv_t)


def _first_argmax_rows(v, row):
    top = v.max(axis=0, keepdims=True)
    idx = jnp.where(v == top, row, float(v.shape[0])).min(axis=0, keepdims=True)
    return top, idx


ROUTER_ROWS = 32
ROUTER_EXPERT_ROW = 8


def _route_t(lt):
    n = lt.shape[1]
    row8 = lax.broadcasted_iota(jnp.int32, (8, n), 0).astype(F32)
    row16 = lax.broadcasted_iota(jnp.int32, (N_EXPERTS, n), 0).astype(F32)
    is_g = row8 < N_GROUPS
    gl = jnp.where(is_g, lt[:8], -jnp.inf)
    gmax, gsel = _first_argmax_rows(gl, row8)
    p_group = 1.0 / jnp.where(is_g, jnp.exp(gl - gmax), 0.0).sum(axis=0, keepdims=True)
    first = gsel * EXPERTS_PER_GROUP
    e_all = lt[ROUTER_EXPERT_ROW:ROUTER_EXPERT_ROW + N_EXPERTS]
    el = jnp.where((row16 >= first) & (row16 < first + EXPERTS_PER_GROUP), e_all, -jnp.inf)
    t1, i1 = _first_argmax_rows(el, row16)
    t2, i2 = _first_argmax_rows(jnp.where(row16 == i1, -jnp.inf, el), row16)
    e2 = jnp.exp(t2 - t1)
    w1 = 1.0 / (1.0 + e2)
    gates = jnp.where(row16 == i1, w1 * p_group, jnp.where(row16 == i2, e2 * w1 * p_group, 0.0))
    rel = gates[:8] + gates[8:]
    rel = rel + pltpu.roll(rel, EXPERTS_PER_GROUP, 0)
    meta_t = jnp.where(row8 < EXPERTS_PER_GROUP, rel, jnp.where(row8 == META_GROUP_LANE, gsel, 0.0))
    onehot_t = jnp.where(row8 == gsel, 1.0, 0.0)
    return meta_t, onehot_t, row8, gsel


def _pack_halves(x):
    half = x.shape[1] // 2
    lo = lax.bitcast_convert_type(x[:, :half].astype(BF16).astype(F32), jnp.uint32)
    hi = lax.bitcast_convert_type(x[:, half:].astype(BF16).astype(F32), jnp.uint32)
    return (lo >> 16) | hi


def _unpack_halves(p):
    lo = lax.bitcast_convert_type(p << 16, F32)
    hi = lax.bitcast_convert_type(p & jnp.uint32(0xFFFF0000), F32)
    return lo, hi


ROW_WORDS = D_MODEL // 2 + LANES
META_GROUP_LANE = EXPERTS_PER_GROUP
META_RANK_LANE = EXPERTS_PER_GROUP + 1


def _proj_res_kernel(o_ref, x_ref, w_ref, gate_ref, g_ref, sh_ref, sc_ref, wrt_ref, brt_ref, tri_ref,
                     xn_ref, hpk_ref, metat_ref, cnt_ref):
    @pl.when(pl.program_id(0) == 0)
    def _():
        cnt_ref[...] = jnp.zeros(cnt_ref.shape, cnt_ref.dtype)

    sub = tri_ref.shape[0]
    cnt = cnt_ref[:, :1]
    for r0 in range(0, x_ref.shape[0], sub):
        rows = slice(r0, r0 + sub)
        y = jnp.dot(o_ref[rows, :], w_ref[...], preferred_element_type=F32)
        xn = x_ref[rows, :] + gate_ref[0] * y
        xn_ref[rows, :] = xn
        h = _norm_mod(xn, g_ref[...], sh_ref[0], sc_ref[0])
        lt = _qk(wrt_ref[...], h.astype(BF16)) + brt_ref[...]
        meta_t, onehot_t, row8, gsel = _route_t(lt)
        prefix = jnp.dot(onehot_t.astype(BF16), tri_ref[...], preferred_element_type=F32)
        rank = jnp.where(row8 == gsel, prefix + cnt - 1.0, 0.0).sum(axis=0, keepdims=True)
        cnt = cnt + onehot_t.sum(axis=1, keepdims=True)
        meta_t = jnp.where(row8 == META_RANK_LANE, rank, meta_t)
        meta = jnp.concatenate([meta_t, jnp.zeros((LANES - 8, sub), F32)], axis=0).T
        metat_ref[:, rows] = meta_t
        hpk_ref[rows, :] = jnp.concatenate([_pack_halves(h), lax.bitcast_convert_type(meta, jnp.uint32)], axis=1)
    cnt_ref[...] = jnp.broadcast_to(cnt, cnt_ref.shape)


def _proj_res(o, x, w_o, g_ffn, mod, layer, row0, tokens_per_row, w_router_t, b_router_t):
    t = x.shape[0]
    tm = TM_PROJ
    tok = lambda n: pl.BlockSpec((tm, n), lambda i: (i, 0))
    ms = lambda k: _mod_spec(layer, k, row0, tokens_per_row, tm)
    tri = jnp.asarray(np.triu(np.ones((TM, TM), np.float32)), BF16)
    return pl.pallas_call(
        _proj_res_kernel,
        grid=(t // tm,),
        in_specs=[tok(D_MODEL), tok(D_MODEL), _const_spec(w_o.shape), ms(2), _const_spec((1, D_MODEL)), ms(3), ms(4),
                  _const_spec(w_router_t.shape), _const_spec(b_router_t.shape), _const_spec((TM, TM))],
        out_specs=[tok(D_MODEL), tok(ROW_WORDS), pl.BlockSpec((8, tm), lambda i: (0, i)), _const_spec((8, LANES))],
        out_shape=[jax.ShapeDtypeStruct((t, D_MODEL), F32), jax.ShapeDtypeStruct((t, ROW_WORDS), jnp.uint32),
                   jax.ShapeDtypeStruct((8, t), F32), jax.ShapeDtypeStruct((8, LANES), F32)],
        compiler_params=_cparams(("arbitrary",)),
        name="proj_res",
    )(o, x, w_o, mod, g_ffn, mod, mod, w_router_t, b_router_t, tri)


def _moe_plan(meta_t, cnt):
    t = meta_t.shape[1]
    n_tiles = t // TM + N_GROUPS
    group = meta_t[META_GROUP_LANE].astype(jnp.int32)
    rank = meta_t[META_RANK_LANE].astype(jnp.int32)
    counts = cnt[:N_GROUPS, 0].astype(jnp.int32)
    tiles = (counts + TM - 1) // TM
    tile_end = jnp.cumsum(tiles)
    tile_start = tile_end - tiles
    pos = rank + TM * jnp.sum(jnp.where(group[:, None] == jnp.arange(N_GROUPS)[None], tile_start[None], 0), axis=1)
    tile_group = jnp.minimum(jnp.sum(jnp.arange(n_tiles)[:, None] >= tile_end[None], axis=1), N_GROUPS - 1)
    pad_bounds = jnp.stack([tile_start * TM + counts, tile_end * TM], axis=1).reshape(-1)
    return (pos.astype(jnp.int32), tile_group.astype(jnp.int32), tile_end[-1:].astype(jnp.int32),
            pad_bounds.astype(jnp.int32))


def _gather_rows(srcs, idx_ref, base, dsts, n_rows):
    for r in range(n_rows):
        s = idx_ref[base + r]
        for src_ref, dst_ref in zip(srcs, dsts):
            dst_ref[r:r + 1, :] = src_ref[pl.ds(s, 1), :]


def _moe_kernel(pos_ref, tg_ref, nt_ref, pad_ref, hpk_ref, wg_ref, wu_ref, wd_ref, ypk_ref, src_ref, hbuf, hid_ref):
    i = pl.program_id(0)
    n_tok = hpk_ref.shape[0]

    @pl.when(i == 0)
    def _():
        def clear(p, c):
            src_ref[p] = 0
            return c
        for g in range(N_GROUPS):
            lax.fori_loop(pad_ref[2 * g], pad_ref[2 * g + 1], clear, 0)

        def fill(t, c):
            src_ref[pos_ref[t]] = t
            return c
        lax.fori_loop(0, n_tok, fill, 0, unroll=8)
        _gather_rows((hpk_ref,), src_ref, 0, (hbuf.at[0],), TM)

    @pl.when(i < nt_ref[0])
    def _():
        slot = i % 2
        row = hbuf[slot]
        lo, hi = _unpack_halves(row[:, :D_MODEL // 2])
        h = jnp.concatenate([lo.astype(BF16), hi.astype(BF16)], axis=1)
        gates = lax.bitcast_convert_type(row[:, D_MODEL // 2:], F32)
        nxt = jnp.minimum(i + 1, nt_ref[0] - 1) * TM
        _gather_rows((hpk_ref,), src_ref, nxt, (hbuf.at[1 - slot],), TM)
        f = D_FF_EXPERT
        for j in range(EXPERTS_PER_GROUP):
            gate_act = _silu(jnp.dot(h, wg_ref[0, j], preferred_element_type=F32))
            hid = gate_act * jnp.dot(h, wu_ref[0, j], preferred_element_type=F32) * gates[:, j:j + 1]
            hid_ref[:, j * f:(j + 1) * f] = hid.astype(BF16)
        ypk_ref[...] = _pack_halves(jnp.dot(hid_ref[...], wd_ref[0], preferred_element_type=F32))

    @pl.when(i >= nt_ref[0])
    def _():
        ypk_ref[...] = jnp.zeros(ypk_ref.shape, ypk_ref.dtype)


def _moe(hpk, pos, tile_group, n_used, pad_bounds, wg, wu, wd, layer):
    t = hpk.shape[0]
    n_tiles = t // TM + N_GROUPS
    f = D_FF_EXPERT
    g0 = layer * N_GROUPS
    grid_spec = pltpu.PrefetchScalarGridSpec(
        num_scalar_prefetch=4,
        grid=(n_tiles,),
        in_specs=[
            pl.BlockSpec(hpk.shape, lambda i, *_: (0, 0), pipeline_mode=pl.Buffered(1)),
            pl.BlockSpec((1, EXPERTS_PER_GROUP, D_MODEL, f), lambda i, pos, tg, nt, pad: (g0 + tg[i], 0, 0, 0)),
            pl.BlockSpec((1, EXPERTS_PER_GROUP, D_MODEL, f), lambda i, pos, tg, nt, pad: (g0 + tg[i], 0, 0, 0)),
            pl.BlockSpec((1, EXPERTS_PER_GROUP * f, D_MODEL), lambda i, pos, tg, nt, pad: (g0 + tg[i], 0, 0)),
        ],
        out_specs=pl.BlockSpec((TM, D_MODEL // 2), lambda i, *_: (i, 0)),
        scratch_shapes=[pltpu.SMEM((n_tiles * TM,), jnp.int32),
                        pltpu.VMEM((2, TM, ROW_WORDS), jnp.uint32),
                        pltpu.VMEM((TM, EXPERTS_PER_GROUP * f), BF16)],
    )
    return pl.pallas_call(
        _moe_kernel,
        grid_spec=grid_spec,
        out_shape=jax.ShapeDtypeStruct((n_tiles * TM, D_MODEL // 2), jnp.uint32),
        compiler_params=_cparams(("arbitrary",)),
        name="moe",
    )(pos, tile_group, n_used, pad_bounds, hpk, wg, wu, wd)


def _combine_kernel(*refs, final_norm):
    pos_ref, ypk_ref, x_ref, gate_ref = refs[:4]
    o_ref, ybuf = refs[-2], refs[-1]
    _gather_rows((ypk_ref,), pos_ref, pl.program_id(0) * TM, (ybuf,), TM)
    lo, hi = _unpack_halves(ybuf[...])
    out = x_ref[...] + gate_ref[0] * jnp.concatenate([lo, hi], axis=1)
    if final_norm:
        out = _rms(out, refs[4][...])
    o_ref[...] = out


def _combine(ypk, pos, x, mod, layer, row0, tokens_per_row, final_g):
    t = x.shape[0]

    def mod_map(i, pos):
        row = row0 + (i * TM) // tokens_per_row
        return ((layer * MOD_ROWS + row) * N_MOD + 5, 0, 0)

    in_specs = [pl.BlockSpec(ypk.shape, lambda i, pos: (0, 0), pipeline_mode=pl.Buffered(1)),
                pl.BlockSpec((TM, D_MODEL), lambda i, pos: (i, 0)),
                pl.BlockSpec((1, 1, D_MODEL), mod_map)]
    args = [ypk, x, mod]
    if final_g is not None:
        in_specs.append(pl.BlockSpec((1, D_MODEL), lambda i, pos: (0, 0)))
        args.append(final_g)
    grid_spec = pltpu.PrefetchScalarGridSpec(
        num_scalar_prefetch=1,
        grid=(t // TM,),
        in_specs=in_specs,
        out_specs=pl.BlockSpec((TM, D_MODEL), lambda i, pos: (i, 0)),
        scratch_shapes=[pltpu.VMEM((TM, D_MODEL // 2), jnp.uint32)],
    )
    return pl.pallas_call(
        functools.partial(_combine_kernel, final_norm=final_g is not None),
        grid_spec=grid_spec,
        out_shape=jax.ShapeDtypeStruct((t, D_MODEL), F32),
        compiler_params=_cparams(("arbitrary",)),
        name="moe_combine",
    )(pos, *args)


def _pad_heads(w, used):
    k = w.shape[0]
    w = w.reshape(k, MLA_HEADS, used)
    return jnp.pad(w, ((0, 0), (0, 0), (0, LANES - used))).reshape(k, MLA_HEADS * LANES)


def kernel(x_prompt, x_sample, state_na_k, state_na_v, state_mla_ckv, state_mla_kr, c, c_ctx, ada_w, ada_b, norm_mix_g, norm_ffn_g, final_norm_g, na_w_qkv, na_w_o, na_rpb, mla_w_in, mla_q_norm_g, mla_kv_norm_g, mla_w_uq, mla_w_uk, mla_w_uv, mla_w_o, moe_w_group, moe_b_group, moe_w_expert, moe_b_expert, moe_w_gate, moe_w_up, moe_w_down):
    n_b, seq, d = x_prompt.shape
    dec_b, n_lat, _ = x_sample.shape
    past = state_na_k.shape[2]
    rows = n_lat // GRID_W
    assert d == D_MODEL and dec_b + 1 <= MOD_ROWS and n_lat % TM_PROJ == 0 and (n_b * seq) % TM_PROJ == 0

    cond8 = jnp.concatenate([c_ctx[None], c, jnp.zeros((MOD_ROWS - 1 - dec_b, d), F32)], axis=0)
    mod = _ada_mod(cond8, ada_w, ada_b)

    xp = x_prompt.reshape(n_b * seq, d)
    xs = x_sample.reshape(dec_b * n_lat, d)
    p_rows = (0, n_b * seq)
    s_rows = (1, n_lat)
    outs = {}

    f = D_FF_EXPERT
    wg = _to_bf16(moe_w_gate).reshape(DEPTH * N_GROUPS, EXPERTS_PER_GROUP, d, f)
    wu = _to_bf16(moe_w_up).reshape(DEPTH * N_GROUPS, EXPERTS_PER_GROUP, d, f)
    wd = _to_bf16(moe_w_down).reshape(DEPTH * N_GROUPS, EXPERTS_PER_GROUP * f, d)

    for i in range(DEPTH):
        j = i // 2
        g_mix = norm_mix_g[i][None]
        g_ffn = norm_ffn_g[i][None]
        if i % 2 == 0:
            w_qkv = _to_bf16(na_w_qkv[j])
            w_vt = na_w_qkv[j][:, 2 * d:].T.astype(BF16)
            qp, kp, vp, outs['na_k'], outs['na_v'] = _na_qkv(xp, g_mix, mod, i, *p_rows, w_qkv, n_b)
            qs, ks, vs_t = _na_qkv_t(xs, g_mix, mod, i, *s_rows, w_qkv, w_vt)
            op = _attention(qp, [(kp, vp)], n_b, seq, wide_qk=False, pairs=8)
            tables = _na_bias_tables(na_rpb[j], rows)
            os_ = _na_latent(qs, ks, vs_t, _state_k(state_na_k, j), _state_vt(state_na_v, j), tables, dec_b)
            w_o = _to_bf16(na_w_o[j])
        else:
            pad_in = MLA_IN_PAD - mla_w_in.shape[-1]
            w_mla = (jnp.pad(mla_w_in[j], ((0, 0), (0, pad_in))).astype(BF16),
                     mla_q_norm_g[j][None], mla_kv_norm_g[j][None],
                     _pad_heads(mla_w_uq[j], MLA_QK_DIM).astype(BF16),
                     _pad_heads(mla_w_uk[j], MLA_NOPE_DIM).astype(BF16),
                     mla_w_uv[j].astype(BF16))
            wuv_t = mla_w_uv[j].T.astype(BF16)
            w_mla_t = w_mla[:5] + (wuv_t,)
            qp, kp, vp, ckv_p, kr_p = _mla_proj(xp, g_mix, mod, i, *p_rows, w_mla, None, True)
            qs, ks, vs_t = _mla_proj(xs, g_mix, mod, i, *s_rows, w_mla_t, _rope_lane_tables(n_lat), False)
            outs['ckv'] = ckv_p.reshape(n_b, 1, seq, MLA_KV_RANK)
            outs['kr'] = kr_p.reshape(n_b, 1, seq, MLA_ROPE_DIM)
            kr_state = jnp.pad(state_mla_kr[:, j].reshape(dec_b * past, MLA_ROPE_DIM),
                               ((0, 0), (0, LANES - MLA_ROPE_DIM)))
            kc, vc_t = _mla_state_kv(state_mla_ckv[:, j].reshape(dec_b * past, MLA_KV_RANK), kr_state,
                                     w_mla[4], wuv_t)
            op = _attention(qp, [(kp, vp)], n_b, seq, wide_qk=True, pairs=8)
            os_ = _attention_t(qs, ks, vs_t, kc, vc_t, dec_b, 512, heads=8)
            w_o = _to_bf16(mla_w_o[j])

        pad_g = ((0, ROUTER_EXPERT_ROW - N_GROUPS), (0, 0))
        pad_e = ((0, ROUTER_ROWS - ROUTER_EXPERT_ROW - N_EXPERTS), (0, 0))
        w_router_t = jnp.concatenate([jnp.pad(moe_w_group[i].T, pad_g), jnp.pad(moe_w_expert[i].T, pad_e)],
                                     axis=0).astype(BF16)
        b_router_t = jnp.concatenate([jnp.pad(moe_b_group[i][:, None], pad_g),
                                      jnp.pad(moe_b_expert[i][:, None], pad_e)], axis=0)
        b_router_t = jnp.broadcast_to(b_router_t, (ROUTER_ROWS, TM))
        final_g = final_norm_g[None] if i == DEPTH - 1 else None
        new_x = []
        for o_att, x, rows_ in ((op, xp, p_rows), (os_, xs, s_rows)):
            xn, hpk, meta, cnt = _proj_res(o_att, x, w_o, g_ffn, mod, i, *rows_, w_router_t, b_router_t)
            pos, tile_group, n_used, pad_bounds = _moe_plan(meta, cnt)
            ypk = _moe(hpk, pos, tile_group, n_used, pad_bounds, wg, wu, wd, i)
            new_x.append(_combine(ypk, pos, xn, mod, i, *rows_, final_g))
        xp, xs = new_x

    return (xp.reshape(n_b, seq, d), xs.reshape(dec_b, n_lat, d),
            outs['na_k'], outs['na_v'], outs['ckv'], outs['kr'])
```

```python
import functools

import numpy as np
import jax
import jax.numpy as jnp
from jax import lax
from jax.experimental import pallas as pl
from jax.experimental.pallas import tpu as pltpu

F32 = jnp.float32
BF16 = jnp.bfloat16

D_MODEL = 1024
DEPTH = 2
GRID_W = 64
N_MOD = 6
EPS = 1e-6
NEG_INF = -1e30
NA_HEADS = 16
NA_HEAD_DIM = 64
NA_KH = 8
NA_KW = 16
NA_GROUP_ROWS = 4
NA_WIN_ROWS = 12
MLA_HEADS = 16
MLA_Q_RANK = 384
MLA_KV_RANK = 256
MLA_NOPE_DIM = 64
MLA_ROPE_DIM = 32
MLA_V_DIM = 64
MLA_QK_DIM = MLA_NOPE_DIM + MLA_ROPE_DIM
MLA_IN_PAD = 768
ROPE_THETA = 10000.0
LOG2_E = 1.4426950408889634
N_GROUPS = 4
EXPERTS_PER_GROUP = 4
N_EXPERTS = 16
D_FF_EXPERT = 256

LANES = 128
MOD_ROWS = 8
VMEM_LIMIT = 56 * 1024 * 1024
TM = 512
CAST_BLOCK_ELEMS = 512 * 1024
TM_PROJ = 1024


def _cparams(sem):
    return pltpu.CompilerParams(dimension_semantics=sem, vmem_limit_bytes=VMEM_LIMIT)


def _const_spec(shape):
    nd = len(shape)
    return pl.BlockSpec(shape, lambda *_: (0,) * nd)


def _cast_kernel(x_ref, o_ref):
    o_ref[...] = x_ref[...].astype(o_ref.dtype)


def _to_bf16(x):
    cols = x.shape[-1]
    x2 = x.reshape(-1, cols)
    rows = x2.shape[0]
    block = rows
    while block * cols > CAST_BLOCK_ELEMS and block % 32 == 0:
        block //= 2
    out = pl.pallas_call(
        _cast_kernel,
        grid=(rows // block,),
        in_specs=[pl.BlockSpec((block, cols), lambda i: (i, 0))],
        out_specs=pl.BlockSpec((block, cols), lambda i: (i, 0)),
        out_shape=jax.ShapeDtypeStruct((rows, cols), BF16),
        compiler_params=_cparams(("arbitrary",)),
        name="to_bf16",
    )(x2)
    return out.reshape(x.shape)


def _norm_mod(x, g, shift, scale):
    ms = jnp.mean(x * x, axis=-1, keepdims=True)
    y = x * lax.rsqrt(ms + EPS) * g
    return y * (1.0 + scale) + shift


def _rms(x, g):
    ms = jnp.mean(x * x, axis=-1, keepdims=True)
    return x * lax.rsqrt(ms + EPS) * g


def _silu(x):
    return x / (1.0 + jnp.exp(-x))


def _ada_kernel(cond_ref, w_ref, b_ref, o_ref):
    s = _silu(cond_ref[...]).astype(BF16)
    o_ref[0] = jnp.dot(s, w_ref[0].astype(BF16), preferred_element_type=F32) + b_ref[0]


def _ada_mod(cond8, ada_w, ada_b):
    tn = 1024
    n = N_MOD * D_MODEL
    out = pl.pallas_call(
        _ada_kernel,
        grid=(DEPTH, n // tn),
        in_specs=[
            _const_spec((MOD_ROWS, D_MODEL)),
            pl.BlockSpec((1, D_MODEL, tn), lambda l, j: (l, 0, j)),
            pl.BlockSpec((1, 1, tn), lambda l, j: (l, 0, j)),
        ],
        out_specs=pl.BlockSpec((1, MOD_ROWS, tn), lambda l, j: (l, 0, j)),
        out_shape=jax.ShapeDtypeStruct((DEPTH, MOD_ROWS, n), F32),
        compiler_params=_cparams(("arbitrary", "arbitrary")),
        name="ada_mod",
    )(cond8, ada_w, ada_b.reshape(DEPTH, 1, n))
    return out.reshape(DEPTH * MOD_ROWS * N_MOD, 1, D_MODEL)


def _mod_spec(layer, k, row0, tokens_per_row, tm=TM):
    def imap(i):
        row = row0 + (i * tm) // tokens_per_row
        return ((layer * MOD_ROWS + row) * N_MOD + k, 0, 0)
    return pl.BlockSpec((1, 1, D_MODEL), imap)


def _qkv_kernel(x_ref, g_ref, sh_ref, sc_ref, w_ref, q_ref, k_ref, v_ref, ks_ref, vs_ref):
    h = _norm_mod(x_ref[...], g_ref[...], sh_ref[0], sc_ref[0]).astype(BF16)
    y = jnp.dot(h, w_ref[...], preferred_element_type=F32)
    q_ref[...] = (y[:, :D_MODEL] * (NA_HEAD_DIM ** -0.5)).astype(q_ref.dtype)
    k = y[:, D_MODEL:2 * D_MODEL]
    v = y[:, 2 * D_MODEL:]
    k_ref[...] = k.astype(k_ref.dtype)
    v_ref[...] = v.astype(v_ref.dtype)
    ks_ref[...] = pltpu.einshape("t(hd)->thd", k, h=NA_HEADS).reshape(ks_ref.shape)
    vs_ref[...] = pltpu.einshape("t(hd)->thd", v, h=NA_HEADS).reshape(vs_ref.shape)


def _na_qkv(x, g, mod, layer, row0, tokens_per_row, w_qkv, n_batch):
    t = x.shape[0]
    seq = t // n_batch
    nb = TM // seq
    tok = pl.BlockSpec((TM, D_MODEL), lambda i: (i, 0))
    state = pl.BlockSpec((nb, 1, seq, NA_HEADS, NA_HEAD_DIM), lambda i: (i, 0, 0, 0, 0))
    state_shape = jax.ShapeDtypeStruct((n_batch, 1, seq, NA_HEADS, NA_HEAD_DIM), F32)
    return pl.pallas_call(
        _qkv_kernel,
        grid=(t // TM,),
        in_specs=[tok, _const_spec((1, D_MODEL)),
                  _mod_spec(layer, 0, row0, tokens_per_row), _mod_spec(layer, 1, row0, tokens_per_row),
                  _const_spec((D_MODEL, 3 * D_MODEL))],
        out_specs=[tok, tok, tok, state, state],
        out_shape=[jax.ShapeDtypeStruct((t, D_MODEL), BF16)] * 3 + [state_shape, state_shape],
        compiler_params=_cparams(("arbitrary",)),
        name="na_qkv",
    )(x, g, mod, mod, w_qkv)


def _store_head_slots(vt, vt_ref):
    n = vt.shape[1]
    n_heads = vt_ref.shape[0] // LANES
    dim = vt.shape[0] // n_heads
    tail = jnp.where(lax.broadcasted_iota(jnp.int32, (LANES - dim, n), 0) == 0, 1.0, 0.0).astype(vt_ref.dtype)
    for hh in range(n_heads):
        vt_ref[hh * LANES:hh * LANES + dim, :] = vt[hh * dim:(hh + 1) * dim, :].astype(vt_ref.dtype)
        vt_ref[hh * LANES + dim:(hh + 1) * LANES, :] = tail


def _qkv_t_kernel(x_ref, g_ref, sh_ref, sc_ref, wqk_ref, wvt_ref, q_ref, k_ref, vt_ref):
    h = _norm_mod(x_ref[...], g_ref[...], sh_ref[0], sc_ref[0]).astype(BF16)
    y = jnp.dot(h, wqk_ref[...], preferred_element_type=F32)
    q_ref[...] = (y[:, :D_MODEL] * (NA_HEAD_DIM ** -0.5 * LOG2_E)).astype(q_ref.dtype)
    k_ref[...] = y[:, D_MODEL:].astype(k_ref.dtype)
    _store_head_slots(_qk(wvt_ref[...], h), vt_ref)


def _na_qkv_t(x, g, mod, layer, row0, tokens_per_row, w_qkv, w_vt):
    t = x.shape[0]
    tok = pl.BlockSpec((TM, D_MODEL), lambda i: (i, 0))
    slots = NA_HEADS * LANES
    return pl.pallas_call(
        _qkv_t_kernel,
        grid=(t // TM,),
        in_specs=[tok, _const_spec((1, D_MODEL)),
                  _mod_spec(layer, 0, row0, tokens_per_row), _mod_spec(layer, 1, row0, tokens_per_row),
                  pl.BlockSpec((D_MODEL, 2 * D_MODEL), lambda i: (0, 0)), _const_spec(w_vt.shape)],
        out_specs=[tok, tok, pl.BlockSpec((slots, TM), lambda i: (0, i))],
        out_shape=[jax.ShapeDtypeStruct((t, D_MODEL), BF16), jax.ShapeDtypeStruct((t, D_MODEL), BF16),
                   jax.ShapeDtypeStruct((slots, t), BF16)],
        compiler_params=_cparams(("arbitrary",)),
        name="na_qkv_t",
    )(x, g, mod, mod, w_qkv, w_vt)


def _softmax_pv(s_list, v_list):
    m = s_list[0].max(axis=-1, keepdims=True)
    for s in s_list[1:]:
        m = jnp.maximum(m, s.max(axis=-1, keepdims=True))
    l = None
    acc = None
    for s, v in zip(s_list, v_list):
        p = jnp.exp(s - m)
        ls = p.sum(axis=-1, keepdims=True)
        a = jnp.dot(p.astype(BF16), v, preferred_element_type=F32)
        l = ls if l is None else l + ls
        acc = a if acc is None else acc + a
    return acc * (1.0 / l)


def _qk(q, k):
    return lax.dot_general(q, k, (((1,), (1,)), ((), ())), preferred_element_type=F32)


def _head_masks():
    lane = lax.broadcasted_iota(jnp.int32, (1, LANES), 1)
    lo = lane < (LANES // 2)
    return lo, jnp.where(lo, 1.0, 0.0).astype(BF16), jnp.where(lo, 0.0, 1.0).astype(BF16)


def _attn_kernel(*refs, n_src, wide_qk, pairs, nb):
    q_ref, o_ref = refs[0], refs[-1]
    kv = refs[1:-1]
    lo, m0, m1 = _head_masks()
    wq = 2 * LANES if wide_qk else LANES
    tq = q_ref.shape[0] // nb
    for bb in range(nb):
        rq = slice(bb * tq, (bb + 1) * tq)
        rk = [slice(bb * (kv[2 * t].shape[0] // nb), (bb + 1) * (kv[2 * t].shape[0] // nb)) for t in range(n_src)]
        for p in range(pairs):
            q = q_ref[rq, p * wq:(p + 1) * wq]
            ks = [kv[2 * t][rk[t], p * wq:(p + 1) * wq].astype(BF16) for t in range(n_src)]
            vs = [kv[2 * t + 1][rk[t], p * LANES:(p + 1) * LANES].astype(BF16) for t in range(n_src)]
            outs = []
            for j in range(2):
                if wide_qk:
                    s_list = [_qk(q[:, j * LANES:(j + 1) * LANES], k[:, j * LANES:(j + 1) * LANES]) for k in ks]
                else:
                    s_list = [_qk(q * (m0 if j == 0 else m1), k) for k in ks]
                outs.append(_softmax_pv(s_list, vs))
            o_ref[rq, p * LANES:(p + 1) * LANES] = jnp.where(lo, outs[0], outs[1]).astype(o_ref.dtype)


def _attention(q, kv_srcs, n_batch, wide_qk, pairs, nb):
    wq = (2 * LANES if wide_qk else LANES) * pairs
    wv = LANES * pairs
    tq = q.shape[0] // n_batch * nb
    n_steps = D_MODEL // wv
    in_specs = [pl.BlockSpec((tq, wq), lambda b, p: (b, p))]
    args = [q]
    for k, v in kv_srcs:
        t_k = k.shape[0] // n_batch * nb
        in_specs.append(pl.BlockSpec((t_k, wq), lambda b, p: (b, p)))
        in_specs.append(pl.BlockSpec((t_k, wv), lambda b, p: (b, p)))
        args += [k, v]
    return pl.pallas_call(
        functools.partial(_attn_kernel, n_src=len(kv_srcs), wide_qk=wide_qk, pairs=pairs, nb=nb),
        grid=(n_batch // nb, n_steps),
        in_specs=in_specs,
        out_specs=pl.BlockSpec((tq, wv), lambda b, p: (b, p)),
        out_shape=jax.ShapeDtypeStruct((q.shape[0], D_MODEL), BF16),
        compiler_params=_cparams(("arbitrary", "arbitrary")),
        name="attention",
    )(*args)


def _col_max(s, chunk=64):
    acc = s[:chunk]
    for r in range(chunk, s.shape[0], chunk):
        acc = jnp.maximum(acc, s[r:r + chunk])
    return acc.max(axis=0, keepdims=True)


def _attn_t_kernel(q_ref, k1_ref, k2_ref, v1_ref, v2_ref, o_ref):
    n_heads = q_ref.shape[1] // LANES
    slots = [slice(j * LANES, (j + 1) * LANES) for j in range(n_heads)]
    def scores(sl):
        return _qk(k1_ref[:, sl], q_ref[:, sl]), _qk(k2_ref[:, sl], q_ref[:, sl])

    nxt = scores(slots[0])
    outs = []
    for n, sl in enumerate(slots):
        s1, s2 = nxt
        if n + 1 < n_heads:
            nxt = scores(slots[n + 1])
        m = jnp.maximum(_col_max(s1), _col_max(s2))
        p1 = jnp.exp2(s1 - m).astype(BF16)
        p2 = jnp.exp2(s2 - m).astype(BF16)
        ot = (jnp.dot(v1_ref[sl, :], p1, preferred_element_type=F32)
              + jnp.dot(v2_ref[sl, :], p2, preferred_element_type=F32))
        outs.append(ot[:MLA_V_DIM] * (1.0 / ot[MLA_V_DIM:MLA_V_DIM + 1]))
    for j in range(0, n_heads, 2):
        pair = jnp.concatenate(outs[j:j + 2], axis=0).T
        o_ref[:, (j // 2) * LANES:(j // 2 + 1) * LANES] = pair.astype(o_ref.dtype)


def _attention_t(q, k1, v1t, k2, v2t, n_batch, tq, heads):
    wq = heads * LANES
    wo = heads * MLA_V_DIM
    nq = q.shape[0] // n_batch // tq
    t1 = k1.shape[0] // n_batch
    t2 = k2.shape[0] // n_batch
    return pl.pallas_call(
        _attn_t_kernel,
        grid=(n_batch, MLA_HEADS // heads, nq),
        in_specs=[pl.BlockSpec((tq, wq), lambda b, p, i: (b * nq + i, p)),
                  pl.BlockSpec((t1, wq), lambda b, p, i: (b, p)),
                  pl.BlockSpec((t2, wq), lambda b, p, i: (b, p)),
                  pl.BlockSpec((wq, t1), lambda b, p, i: (p, b)),
                  pl.BlockSpec((wq, t2), lambda b, p, i: (p, b))],
        out_specs=pl.BlockSpec((tq, wo), lambda b, p, i: (b * nq + i, p)),
        out_shape=jax.ShapeDtypeStruct((q.shape[0], D_MODEL), BF16),
        compiler_params=_cparams(("arbitrary", "arbitrary", "arbitrary")),
        name="attention_t",
    )(q, k1, k2, v1t, v2t)


def _na_group_layout(rows):
    n_groups = rows // NA_GROUP_ROWS
    kh = min(NA_KH, rows)
    bases, types, keys = [], [], {}
    for gi in range(n_groups):
        r0 = gi * NA_GROUP_ROWS
        base = int(np.clip(r0 - kh // 2, 0, rows - NA_WIN_ROWS))
        starts = tuple(int(np.clip(r - kh // 2, 0, rows - kh)) - base for r in range(r0, r0 + NA_GROUP_ROWS))
        key = (starts, r0 - base)
        if key not in keys:
            keys[key] = len(keys)
        bases.append(base)
        types.append(keys[key])
    return bases, types, list(keys.keys())


def _na_bias_tables(rpb, rows):
    kh = min(NA_KH, rows)
    _, _, type_keys = _na_group_layout(rows)
    qc = np.arange(GRID_W)
    kc = np.arange(GRID_W)
    col_start = np.clip(qc - NA_KW // 2, 0, GRID_W - NA_KW)
    col_valid = (kc[None, :] >= col_start[:, None]) & (kc[None, :] < col_start[:, None] + NA_KW)
    dx = kc[None, :] - qc[:, None] + (NA_KW - 1)
    onehot = ((dx[None] == np.arange(2 * NA_KW - 1)[:, None, None]) & col_valid[None]).astype(np.float32)
    band = jnp.einsum('hyd,dqk->hykq', rpb.astype(F32), onehot, precision=lax.Precision.HIGHEST)
    band = jnp.where(col_valid.T, band, NEG_INF)
    strip = band.reshape(NA_HEADS, (2 * NA_KH - 1) * GRID_W, GRID_W)
    tables = []
    for starts, r_off in type_keys:
        q_cols = []
        for rq in range(NA_GROUP_ROWS):
            dy0 = starts[rq] - (rq + r_off) + (NA_KH - 1)
            piece = strip[:, dy0 * GRID_W:(dy0 + kh) * GRID_W, :]
            pad = ((0, 0), (starts[rq] * GRID_W, (NA_WIN_ROWS - kh - starts[rq]) * GRID_W), (0, 0))
            q_cols.append(jnp.pad(piece, pad, constant_values=NEG_INF))
        tables.append(jnp.concatenate(q_cols, axis=2))
    return jnp.stack(tables, axis=1) * LOG2_E


def _state_k_kernel(x_ref, o_ref):
    for h in range(NA_HEADS):
        o_ref[:, h * NA_HEAD_DIM:(h + 1) * NA_HEAD_DIM] = x_ref[0, 0, :, h, :].astype(o_ref.dtype)


def _state_vt_kernel(x_ref, o_ref, flat_ref):
    for h in range(NA_HEADS):
        flat_ref[:, h * NA_HEAD_DIM:(h + 1) * NA_HEAD_DIM] = x_ref[0, 0, :, h, :]
    _store_head_slots(flat_ref[...].T, o_ref)


def _state_k(x, layer):
    b, _, t, h, dh = x.shape
    return pl.pallas_call(
        _state_k_kernel,
        grid=(b,),
        in_specs=[pl.BlockSpec((1, 1, t, h, dh), lambda i: (i, layer, 0, 0, 0))],
        out_specs=pl.BlockSpec((t, h * dh), lambda i: (i, 0)),
        out_shape=jax.ShapeDtypeStruct((b * t, h * dh), BF16),
        compiler_params=_cparams(("arbitrary",)),
        name="state_k",
    )(x)


def _state_vt(x, layer):
    b, _, t, h, dh = x.shape
    return pl.pallas_call(
        _state_vt_kernel,
        grid=(b,),
        in_specs=[pl.BlockSpec((1, 1, t, h, dh), lambda i: (i, layer, 0, 0, 0))],
        out_specs=pl.BlockSpec((h * LANES, t), lambda i: (0, i)),
        out_shape=jax.ShapeDtypeStruct((h * LANES, b * t), BF16),
        scratch_shapes=[pltpu.VMEM((t, h * dh), F32)],
        compiler_params=_cparams(("arbitrary",)),
        name="state_vt",
    )(x)


def _na_latent_kernel(q_ref, k_ref, vt_ref, kc_ref, vct_ref, tab_ref, o_ref, *, bases, types):
    _, m0, m1 = _head_masks()
    kctx = kc_ref[...]
    gq = NA_GROUP_ROWS * GRID_W
    gk = NA_WIN_ROWS * GRID_W
    blocks = [(gi, base, ty, j) for gi, (base, ty) in enumerate(zip(bases, types)) for j in range(2)]

    def scores(gi, base, ty, j):
        qj = q_ref[gi * gq:(gi + 1) * gq, :] * (m0 if j == 0 else m1)
        kwin = k_ref[base * GRID_W:base * GRID_W + gk, :]
        return _qk(kwin, qj) + tab_ref[j, ty], _qk(kctx, qj)

    nxt = scores(*blocks[0])
    outs = []
    for n, (gi, base, ty, j) in enumerate(blocks):
        s_win, s_ctx = nxt
        if n + 1 < len(blocks):
            nxt = scores(*blocks[n + 1])
        m = jnp.maximum(_col_max(s_win), _col_max(s_ctx))
        p_win = jnp.exp2(s_win - m).astype(BF16)
        p_ctx = jnp.exp2(s_ctx - m).astype(BF16)
        sl = slice(j * LANES, (j + 1) * LANES)
        ot = (jnp.dot(vt_ref[sl, base * GRID_W:base * GRID_W + gk], p_win, preferred_element_type=F32)
              + jnp.dot(vct_ref[sl, :], p_ctx, preferred_element_type=F32))
        outs.append(ot[:NA_HEAD_DIM] * (1.0 / ot[NA_HEAD_DIM:NA_HEAD_DIM + 1]))
        if j == 1:
            o_ref[gi * gq:(gi + 1) * gq, :] = jnp.concatenate(outs, axis=0).T.astype(o_ref.dtype)
            outs = []


def _na_latent(q, k, vt, k_ctx, vt_ctx, tables, n_batch):
    t = q.shape[0] // n_batch
    rows = t // GRID_W
    t_ctx = k_ctx.shape[0] // n_batch
    bases, types, type_keys = _na_group_layout(rows)
    n_types = len(type_keys)
    gq = NA_GROUP_ROWS * GRID_W
    gk = NA_WIN_ROWS * GRID_W
    tok = pl.BlockSpec((t, LANES), lambda p, b: (b, p))
    ctx = pl.BlockSpec((t_ctx, LANES), lambda p, b: (b, p))
    return pl.pallas_call(
        functools.partial(_na_latent_kernel, bases=bases, types=types),
        grid=(D_MODEL // LANES, n_batch),
        in_specs=[tok, tok, pl.BlockSpec((2 * LANES, t), lambda p, b: (p, b)),
                  ctx, pl.BlockSpec((2 * LANES, t_ctx), lambda p, b: (p, b)),
                  pl.BlockSpec((2, n_types, gk, gq), lambda p, b: (p, 0, 0, 0))],
        out_specs=tok,
        out_shape=jax.ShapeDtypeStruct(q.shape, BF16),
        compiler_params=_cparams(("arbitrary", "arbitrary")),
        name="na_latent",
    )(q, k, vt, k_ctx, vt_ctx, tables)


def _rope_lane_tables(n_tokens):
    f32 = np.float32
    n_freq = MLA_ROPE_DIM // 4
    inv = f32(1.0) / (f32(ROPE_THETA) ** (np.arange(n_freq, dtype=f32) / f32(n_freq)))
    tt = np.arange(n_tokens)
    row = (tt // GRID_W).astype(f32)
    col = (tt % GRID_W).astype(f32)
    ang = np.concatenate([row[:, None] * inv, col[:, None] * inv], axis=-1).astype(f32)
    cos, sin = np.cos(ang).astype(f32), np.sin(ang).astype(f32)
    ones = np.ones((n_tokens, MLA_NOPE_DIM), f32)
    z16 = np.zeros((n_tokens, MLA_ROPE_DIM // 2), f32)
    z32 = np.zeros((n_tokens, LANES - MLA_QK_DIM), f32)
    z64 = np.zeros((n_tokens, MLA_NOPE_DIM), f32)
    c_tab = np.concatenate([ones, cos, cos, z32], axis=-1)
    s_dn = np.concatenate([z64, -sin, z16, z32], axis=-1)
    s_up = np.concatenate([z64, z16, sin, z32], axis=-1)
    return jnp.asarray(c_tab), jnp.asarray(s_dn), jnp.asarray(s_up)


def _rope_slot(x, c_tab, s_dn, s_up):
    half = MLA_ROPE_DIM // 2
    return x * c_tab + pltpu.roll(x, LANES - half, 1) * s_dn + pltpu.roll(x, half, 1) * s_up


def _mla_kv(ckv_n, kr_slot, wuk_ref, wuv_ref, k_ref, v_ref, v_transposed):
    c = ckv_n.astype(BF16)
    k_nope = jnp.dot(c, wuk_ref[...], preferred_element_type=F32)
    for h in range(MLA_HEADS):
        k_ref[:, h * LANES:(h + 1) * LANES] = (k_nope[:, h * LANES:(h + 1) * LANES] + kr_slot).astype(k_ref.dtype)
    if v_transposed:
        _store_head_slots(_qk(wuv_ref[...], c), v_ref)
    else:
        v_ref[...] = jnp.dot(c, wuv_ref[...], preferred_element_type=F32).astype(v_ref.dtype)


def _mla_proj_kernel(*refs, rope, emit_state, q_scale):
    (x_ref, g_ref, sh_ref, sc_ref, win_ref, qg_ref, kvg_ref, wuq_ref, wuk_ref, wuv_ref) = refs[:10]
    pos = 10
    if rope:
        ct_ref, sd_ref, su_ref = refs[pos:pos + 3]
        pos += 3
    q_ref, k_ref, v_ref = refs[pos:pos + 3]
    pos += 3
    h = _norm_mod(x_ref[...], g_ref[...], sh_ref[0], sc_ref[0]).astype(BF16)
    c = jnp.dot(h, win_ref[...], preferred_element_type=F32)
    cq = _rms(c[:, :MLA_Q_RANK], qg_ref[...]).astype(BF16)
    ckv = _rms(c[:, MLA_Q_RANK:MLA_Q_RANK + MLA_KV_RANK], kvg_ref[...])
    kr_chunk = c[:, MLA_Q_RANK + MLA_KV_RANK:]
    if emit_state:
        ckv_out, kr_out = refs[pos:pos + 2]
        ckv_out[...] = ckv
        kr_out[...] = kr_chunk[:, :MLA_ROPE_DIM]
    kr_slot = pltpu.roll(kr_chunk, MLA_NOPE_DIM, 1)
    q = jnp.dot(cq, wuq_ref[...], preferred_element_type=F32) * q_scale
    if rope:
        ct, sd, su = ct_ref[...], sd_ref[...], su_ref[...]
        kr_slot = _rope_slot(kr_slot, ct, sd, su)
    for hh in range(MLA_HEADS):
        qh = q[:, hh * LANES:(hh + 1) * LANES]
        if rope:
            qh = _rope_slot(qh, ct, sd, su)
        q_ref[:, hh * LANES:(hh + 1) * LANES] = qh.astype(q_ref.dtype)
    _mla_kv(ckv, kr_slot, wuk_ref, wuv_ref, k_ref, v_ref, v_transposed=rope)


def _mla_proj(x, g, mod, layer, row0, tokens_per_row, w, rope_tabs, emit_state):
    t = x.shape[0]
    win, qg, kvg, wuq, wuk, wuv = w
    tok = lambda n: pl.BlockSpec((TM, n), lambda i: (i, 0))
    wq = MLA_HEADS * LANES
    in_specs = [tok(D_MODEL), _const_spec((1, D_MODEL)),
                _mod_spec(layer, 0, row0, tokens_per_row), _mod_spec(layer, 1, row0, tokens_per_row),
                _const_spec(win.shape), _const_spec(qg.shape), _const_spec(kvg.shape),
                _const_spec(wuq.shape), _const_spec(wuk.shape), _const_spec(wuv.shape)]
    args = [x, g, mod, mod, win, qg, kvg, wuq, wuk, wuv]
    rope = rope_tabs is not None
    if rope:
        n_tab = rope_tabs[0].shape[0] // TM
        in_specs += [pl.BlockSpec((TM, LANES), lambda i: (i % n_tab, 0)) for _ in range(3)]
        args += list(rope_tabs)
    if rope:
        v_spec, v_shape = pl.BlockSpec((wq, TM), lambda i: (0, i)), jax.ShapeDtypeStruct((wq, t), BF16)
        q_scale = MLA_QK_DIM ** -0.5 * LOG2_E
    else:
        v_spec, v_shape = tok(D_MODEL), jax.ShapeDtypeStruct((t, D_MODEL), BF16)
        q_scale = MLA_QK_DIM ** -0.5
    out_specs = [tok(wq), tok(wq), v_spec]
    out_shape = [jax.ShapeDtypeStruct((t, wq), BF16), jax.ShapeDtypeStruct((t, wq), BF16), v_shape]
    if emit_state:
        out_specs += [tok(MLA_KV_RANK), tok(MLA_ROPE_DIM)]
        out_shape += [jax.ShapeDtypeStruct((t, MLA_KV_RANK), F32), jax.ShapeDtypeStruct((t, MLA_ROPE_DIM), F32)]
    return pl.pallas_call(
        functools.partial(_mla_proj_kernel, rope=rope, emit_state=emit_state, q_scale=q_scale),
        grid=(t // TM,),
        in_specs=in_specs, out_specs=out_specs, out_shape=out_shape,
        compiler_params=_cparams(("arbitrary",)),
        name="mla_proj",
    )(*args)


def _mla_state_kv_kernel(ckv_ref, kr_ref, wuk_ref, wuv_ref, k_ref, v_ref):
    kr_slot = pltpu.roll(kr_ref[...], MLA_NOPE_DIM, 1)
    _mla_kv(ckv_ref[...], kr_slot, wuk_ref, wuv_ref, k_ref, v_ref, v_transposed=True)


def _mla_state_kv(ckv, kr_pad, wuk, wuv_t):
    t = ckv.shape[0]
    wq = MLA_HEADS * LANES
    tok = lambda n: pl.BlockSpec((TM, n), lambda i: (i, 0))
    return pl.pallas_call(
        _mla_state_kv_kernel,
        grid=(t // TM,),
        in_specs=[tok(MLA_KV_RANK), tok(LANES), _const_spec(wuk.shape), _const_spec(wuv_t.shape)],
        out_specs=[tok(wq), pl.BlockSpec((wq, TM), lambda i: (0, i))],
        out_shape=[jax.ShapeDtypeStruct((t, wq), BF16), jax.ShapeDtypeStruct((wq, t), BF16)],
        compiler_params=_cparams(("arbitrary",)),
        name="mla_state_kv",
    )(ckv, kr_pad, wuk, wuv_t)


def _first_argmax_rows(v, row):
    top = v.max(axis=0, keepdims=True)
    idx = jnp.where(v == top, row, float(v.shape[0])).min(axis=0, keepdims=True)
    return top, idx


ROUTER_ROWS = 32
ROUTER_EXPERT_ROW = 8


def _route_t(lt):
    n = lt.shape[1]
    row8 = lax.broadcasted_iota(jnp.int32, (8, n), 0).astype(F32)
    row16 = lax.broadcasted_iota(jnp.int32, (N_EXPERTS, n), 0).astype(F32)
    is_g = row8 < N_GROUPS
    gl = jnp.where(is_g, lt[:8], -jnp.inf)
    gmax, gsel = _first_argmax_rows(gl, row8)
    p_group = 1.0 / jnp.where(is_g, jnp.exp(gl - gmax), 0.0).sum(axis=0, keepdims=True)
    first = gsel * EXPERTS_PER_GROUP
    e_all = lt[ROUTER_EXPERT_ROW:ROUTER_EXPERT_ROW + N_EXPERTS]
    el = jnp.where((row16 >= first) & (row16 < first + EXPERTS_PER_GROUP), e_all, -jnp.inf)
    t1, i1 = _first_argmax_rows(el, row16)
    t2, i2 = _first_argmax_rows(jnp.where(row16 == i1, -jnp.inf, el), row16)
    e2 = jnp.exp(t2 - t1)
    w1 = 1.0 / (1.0 + e2)
    gates = jnp.where(row16 == i1, w1 * p_group, jnp.where(row16 == i2, e2 * w1 * p_group, 0.0))
    rel = gates[:8] + gates[8:]
    rel = rel + pltpu.roll(rel, EXPERTS_PER_GROUP, 0)
    meta_t = jnp.where(row8 < EXPERTS_PER_GROUP, rel, jnp.where(row8 == META_GROUP_LANE, gsel, 0.0))
    onehot_t = jnp.where(row8 == gsel, 1.0, 0.0)
    return meta_t, onehot_t, row8, gsel


def _pack_halves(x):
    half = x.shape[1] // 2
    lo = lax.bitcast_convert_type(x[:, :half].astype(BF16).astype(F32), jnp.uint32)
    hi = lax.bitcast_convert_type(x[:, half:].astype(BF16).astype(F32), jnp.uint32)
    return (lo >> 16) | hi


def _unpack_halves(p):
    lo = lax.bitcast_convert_type(p << 16, F32)
    hi = lax.bitcast_convert_type(p & jnp.uint32(0xFFFF0000), F32)
    return lo, hi


ROW_WORDS = D_MODEL // 2 + LANES
META_GROUP_LANE = EXPERTS_PER_GROUP
META_RANK_LANE = EXPERTS_PER_GROUP + 1


def _proj_res_kernel(o_ref, x_ref, w_ref, gate_ref, g_ref, sh_ref, sc_ref, wrt_ref, brt_ref, tri_ref,
                     xn_ref, hpk_ref, metat_ref, cnt_ref):
    @pl.when(pl.program_id(0) == 0)
    def _():
        cnt_ref[...] = jnp.zeros(cnt_ref.shape, cnt_ref.dtype)

    sub = tri_ref.shape[0]
    cnt = cnt_ref[:, :1]
    for r0 in range(0, x_ref.shape[0], sub):
        rows = slice(r0, r0 + sub)
        y = jnp.dot(o_ref[rows, :], w_ref[...], preferred_element_type=F32)
        xn = x_ref[rows, :] + gate_ref[0] * y
        xn_ref[rows, :] = xn
        h = _norm_mod(xn, g_ref[...], sh_ref[0], sc_ref[0])
        lt = _qk(wrt_ref[...], h.astype(BF16)) + brt_ref[...]
        meta_t, onehot_t, row8, gsel = _route_t(lt)
        prefix = jnp.dot(onehot_t.astype(BF16), tri_ref[...], preferred_element_type=F32)
        rank = jnp.where(row8 == gsel, prefix + cnt - 1.0, 0.0).sum(axis=0, keepdims=True)
        cnt = cnt + onehot_t.sum(axis=1, keepdims=True)
        meta_t = jnp.where(row8 == META_RANK_LANE, rank, meta_t)
        meta = jnp.concatenate([meta_t, jnp.zeros((LANES - 8, sub), F32)], axis=0).T
        metat_ref[:, rows] = meta_t
        hpk_ref[rows, :] = jnp.concatenate([_pack_halves(h), lax.bitcast_convert_type(meta, jnp.uint32)], axis=1)
    cnt_ref[...] = jnp.broadcast_to(cnt, cnt_ref.shape)


def _proj_res(o, x, w_o, g_ffn, mod, layer, row0, tokens_per_row, w_router_t, b_router_t):
    t = x.shape[0]
    tm = TM_PROJ
    tok = lambda n: pl.BlockSpec((tm, n), lambda i: (i, 0))
    ms = lambda k: _mod_spec(layer, k, row0, tokens_per_row, tm)
    tri = jnp.asarray(np.triu(np.ones((TM, TM), np.float32)), BF16)
    return pl.pallas_call(
        _proj_res_kernel,
        grid=(t // tm,),
        in_specs=[tok(D_MODEL), tok(D_MODEL), _const_spec(w_o.shape), ms(2), _const_spec((1, D_MODEL)), ms(3), ms(4),
                  _const_spec(w_router_t.shape), _const_spec(b_router_t.shape), _const_spec((TM, TM))],
        out_specs=[tok(D_MODEL), tok(ROW_WORDS), pl.BlockSpec((8, tm), lambda i: (0, i)), _const_spec((8, LANES))],
        out_shape=[jax.ShapeDtypeStruct((t, D_MODEL), F32), jax.ShapeDtypeStruct((t, ROW_WORDS), jnp.uint32),
                   jax.ShapeDtypeStruct((8, t), F32), jax.ShapeDtypeStruct((8, LANES), F32)],
        compiler_params=_cparams(("arbitrary",)),
        name="proj_res",
    )(o, x, w_o, mod, g_ffn, mod, mod, w_router_t, b_router_t, tri)


def _moe_plan(meta_t, cnt):
    t = meta_t.shape[1]
    n_tiles = t // TM + N_GROUPS
    group = meta_t[META_GROUP_LANE].astype(jnp.int32)
    rank = meta_t[META_RANK_LANE].astype(jnp.int32)
    counts = cnt[:N_GROUPS, 0].astype(jnp.int32)
    tiles = (counts + TM - 1) // TM
    tile_end = jnp.cumsum(tiles)
    tile_start = tile_end - tiles
    pos = rank + TM * jnp.sum(jnp.where(group[:, None] == jnp.arange(N_GROUPS)[None], tile_start[None], 0), axis=1)
    tile_group = jnp.minimum(jnp.sum(jnp.arange(n_tiles)[:, None] >= tile_end[None], axis=1), N_GROUPS - 1)
    pad_bounds = jnp.stack([tile_start * TM + counts, tile_end * TM], axis=1).reshape(-1)
    return (pos.astype(jnp.int32), tile_group.astype(jnp.int32), tile_end[-1:].astype(jnp.int32),
            pad_bounds.astype(jnp.int32))


def _gather_rows(srcs, idx_ref, base, dsts, n_rows):
    for r in range(n_rows):
        s = idx_ref[base + r]
        for src_ref, dst_ref in zip(srcs, dsts):
            dst_ref[r:r + 1, :] = src_ref[pl.ds(s, 1), :]


def _moe_kernel(pos_ref, tg_ref, nt_ref, pad_ref, hpk_ref, wg_ref, wu_ref, wd_ref, ypk_ref, src_ref, hbuf, hid_ref):
    i = pl.program_id(0)
    n_tok = hpk_ref.shape[0]

    @pl.when(i == 0)
    def _():
        def clear(p, c):
            src_ref[p] = 0
            return c
        for g in range(N_GROUPS):
            lax.fori_loop(pad_ref[2 * g], pad_ref[2 * g + 1], clear, 0)

        def fill(t, c):
            src_ref[pos_ref[t]] = t
            return c
        lax.fori_loop(0, n_tok, fill, 0, unroll=8)
        _gather_rows((hpk_ref,), src_ref, 0, (hbuf.at[0],), TM)

    @pl.when(i < nt_ref[0])
    def _():
        slot = i % 2
        row = hbuf[slot]
        lo, hi = _unpack_halves(row[:, :D_MODEL // 2])
        h = jnp.concatenate([lo.astype(BF16), hi.astype(BF16)], axis=1)
        gates = lax.bitcast_convert_type(row[:, D_MODEL // 2:], F32)
        nxt = jnp.minimum(i + 1, nt_ref[0] - 1) * TM
        _gather_rows((hpk_ref,), src_ref, nxt, (hbuf.at[1 - slot],), TM)
        f = D_FF_EXPERT
        for j in range(EXPERTS_PER_GROUP):
            gate_act = _silu(jnp.dot(h, wg_ref[0, j], preferred_element_type=F32))
            hid = gate_act * jnp.dot(h, wu_ref[0, j], preferred_element_type=F32) * gates[:, j:j + 1]
            hid_ref[:, j * f:(j + 1) * f] = hid.astype(BF16)
        ypk_ref[...] = _pack_halves(jnp.dot(hid_ref[...], wd_ref[0], preferred_element_type=F32))

    @pl.when(i >= nt_ref[0])
    def _():
        ypk_ref[...] = jnp.zeros(ypk_ref.shape, ypk_ref.dtype)


def _moe(hpk, pos, tile_group, n_used, pad_bounds, wg, wu, wd, layer):
    t = hpk.shape[0]
    n_tiles = t // TM + N_GROUPS
    f = D_FF_EXPERT
    g0 = layer * N_GROUPS
    grid_spec = pltpu.PrefetchScalarGridSpec(
        num_scalar_prefetch=4,
        grid=(n_tiles,),
        in_specs=[
            pl.BlockSpec(hpk.shape, lambda i, *_: (0, 0), pipeline_mode=pl.Buffered(1)),
            pl.BlockSpec((1, EXPERTS_PER_GROUP, D_MODEL, f), lambda i, pos, tg, nt, pad: (g0 + tg[i], 0, 0, 0)),
            pl.BlockSpec((1, EXPERTS_PER_GROUP, D_MODEL, f), lambda i, pos, tg, nt, pad: (g0 + tg[i], 0, 0, 0)),
            pl.BlockSpec((1, EXPERTS_PER_GROUP * f, D_MODEL), lambda i, pos, tg, nt, pad: (g0 + tg[i], 0, 0)),
        ],
        out_specs=pl.BlockSpec((TM, D_MODEL // 2), lambda i, *_: (i, 0)),
        scratch_shapes=[pltpu.SMEM((n_tiles * TM,), jnp.int32),
                        pltpu.VMEM((2, TM, ROW_WORDS), jnp.uint32),
                        pltpu.VMEM((TM, EXPERTS_PER_GROUP * f), BF16)],
    )
    return pl.pallas_call(
        _moe_kernel,
        grid_spec=grid_spec,
        out_shape=jax.ShapeDtypeStruct((n_tiles * TM, D_MODEL // 2), jnp.uint32),
        compiler_params=_cparams(("arbitrary",)),
        name="moe",
    )(pos, tile_group, n_used, pad_bounds, hpk, wg, wu, wd)


def _combine_kernel(*refs, final_norm):
    pos_ref, ypk_ref, x_ref, gate_ref = refs[:4]
    o_ref, ybuf = refs[-2], refs[-1]
    _gather_rows((ypk_ref,), pos_ref, pl.program_id(0) * TM, (ybuf,), TM)
    lo, hi = _unpack_halves(ybuf[...])
    out = x_ref[...] + gate_ref[0] * jnp.concatenate([lo, hi], axis=1)
    if final_norm:
        out = _rms(out, refs[4][...])
    o_ref[...] = out


def _combine(ypk, pos, x, mod, layer, row0, tokens_per_row, final_g):
    t = x.shape[0]

    def mod_map(i, pos):
        row = row0 + (i * TM) // tokens_per_row
        return ((layer * MOD_ROWS + row) * N_MOD + 5, 0, 0)

    in_specs = [pl.BlockSpec(ypk.shape, lambda i, pos: (0, 0), pipeline_mode=pl.Buffered(1)),
                pl.BlockSpec((TM, D_MODEL), lambda i, pos: (i, 0)),
                pl.BlockSpec((1, 1, D_MODEL), mod_map)]
    args = [ypk, x, mod]
    if final_g is not None:
        in_specs.append(pl.BlockSpec((1, D_MODEL), lambda i, pos: (0, 0)))
        args.append(final_g)
    grid_spec = pltpu.PrefetchScalarGridSpec(
        num_scalar_prefetch=1,
        grid=(t // TM,),
        in_specs=in_specs,
        out_specs=pl.BlockSpec((TM, D_MODEL), lambda i, pos: (i, 0)),
        scratch_shapes=[pltpu.VMEM((TM, D_MODEL // 2), jnp.uint32)],
    )
    return pl.pallas_call(
        functools.partial(_combine_kernel, final_norm=final_g is not None),
        grid_spec=grid_spec,
        out_shape=jax.ShapeDtypeStruct((t, D_MODEL), F32),
        compiler_params=_cparams(("arbitrary",)),
        name="moe_combine",
    )(pos, *args)


def _pad_heads(w, used):
    k = w.shape[0]
    w = w.reshape(k, MLA_HEADS, used)
    return jnp.pad(w, ((0, 0), (0, 0), (0, LANES - used))).reshape(k, MLA_HEADS * LANES)


def kernel(x_prompt, x_sample, state_na_k, state_na_v, state_mla_ckv, state_mla_kr, c, c_ctx, ada_w, ada_b, norm_mix_g, norm_ffn_g, final_norm_g, na_w_qkv, na_w_o, na_rpb, mla_w_in, mla_q_norm_g, mla_kv_norm_g, mla_w_uq, mla_w_uk, mla_w_uv, mla_w_o, moe_w_group, moe_b_group, moe_w_expert, moe_b_expert, moe_w_gate, moe_w_up, moe_w_down):
    n_b, seq, d = x_prompt.shape
    dec_b, n_lat, _ = x_sample.shape
    past = state_na_k.shape[2]
    rows = n_lat // GRID_W
    assert d == D_MODEL and dec_b + 1 <= MOD_ROWS and n_lat % TM_PROJ == 0 and (n_b * seq) % TM_PROJ == 0

    cond8 = jnp.concatenate([c_ctx[None], c, jnp.zeros((MOD_ROWS - 1 - dec_b, d), F32)], axis=0)
    mod = _ada_mod(cond8, ada_w, ada_b)

    xp = x_prompt.reshape(n_b * seq, d)
    xs = x_sample.reshape(dec_b * n_lat, d)
    p_rows = (0, n_b * seq)
    s_rows = (1, n_lat)
    outs = {}

    f = D_FF_EXPERT
    wg = _to_bf16(moe_w_gate).reshape(DEPTH * N_GROUPS, EXPERTS_PER_GROUP, d, f)
    wu = _to_bf16(moe_w_up).reshape(DEPTH * N_GROUPS, EXPERTS_PER_GROUP, d, f)
    wd = _to_bf16(moe_w_down).reshape(DEPTH * N_GROUPS, EXPERTS_PER_GROUP * f, d)

    for i in range(DEPTH):
        j = i // 2
        g_mix = norm_mix_g[i][None]
        g_ffn = norm_ffn_g[i][None]
        if i % 2 == 0:
            w_qkv = _to_bf16(na_w_qkv[j])
            w_vt = na_w_qkv[j][:, 2 * d:].T.astype(BF16)
            qp, kp, vp, outs['na_k'], outs['na_v'] = _na_qkv(xp, g_mix, mod, i, *p_rows, w_qkv, n_b)
            qs, ks, vs_t = _na_qkv_t(xs, g_mix, mod, i, *s_rows, w_qkv, w_vt)
            op = _attention(qp, [(kp, vp)], n_b, wide_qk=False, pairs=8, nb=4)
            tables = _na_bias_tables(na_rpb[j], rows)
            os_ = _na_latent(qs, ks, vs_t, _state_k(state_na_k, j), _state_vt(state_na_v, j), tables, dec_b)
            w_o = _to_bf16(na_w_o[j])
        else:
            pad_in = MLA_IN_PAD - mla_w_in.shape[-1]
            w_mla = (jnp.pad(mla_w_in[j], ((0, 0), (0, pad_in))).astype(BF16),
                     mla_q_norm_g[j][None], mla_kv_norm_g[j][None],
                     _pad_heads(mla_w_uq[j], MLA_QK_DIM).astype(BF16),
                     _pad_heads(mla_w_uk[j], MLA_NOPE_DIM).astype(BF16),
                     mla_w_uv[j].astype(BF16))
            wuv_t = mla_w_uv[j].T.astype(BF16)
            w_mla_t = w_mla[:5] + (wuv_t,)
            qp, kp, vp, ckv_p, kr_p = _mla_proj(xp, g_mix, mod, i, *p_rows, w_mla, None, True)
            qs, ks, vs_t = _mla_proj(xs, g_mix, mod, i, *s_rows, w_mla_t, _rope_lane_tables(n_lat), False)
            outs['ckv'] = ckv_p.reshape(n_b, 1, seq, MLA_KV_RANK)
            outs['kr'] = kr_p.reshape(n_b, 1, seq, MLA_ROPE_DIM)
            kr_state = jnp.pad(state_mla_kr[:, j].reshape(dec_b * past, MLA_ROPE_DIM),
                               ((0, 0), (0, LANES - MLA_ROPE_DIM)))
            kc, vc_t = _mla_state_kv(state_mla_ckv[:, j].reshape(dec_b * past, MLA_KV_RANK), kr_state,
                                     w_mla[4], wuv_t)
            op = _attention(qp, [(kp, vp)], n_b, wide_qk=True, pairs=8, nb=4)
            os_ = _attention_t(qs, ks, vs_t, kc, vc_t, dec_b, 512, heads=8)
            w_o = _to_bf16(mla_w_o[j])

        pad_g = ((0, ROUTER_EXPERT_ROW - N_GROUPS), (0, 0))
        pad_e = ((0, ROUTER_ROWS - ROUTER_EXPERT_ROW - N_EXPERTS), (0, 0))
        w_router_t = jnp.concatenate([jnp.pad(moe_w_group[i].T, pad_g), jnp.pad(moe_w_expert[i].T, pad_e)],
                                     axis=0).astype(BF16)
        b_router_t = jnp.concatenate([jnp.pad(moe_b_group[i][:, None], pad_g),
                                      jnp.pad(moe_b_expert[i][:, None], pad_e)], axis=0)
        b_router_t = jnp.broadcast_to(b_router_t, (ROUTER_ROWS, TM))
        final_g = final_norm_g[None] if i == DEPTH - 1 else None
        new_x = []
        for o_att, x, rows_ in ((op, xp, p_rows), (os_, xs, s_rows)):
            xn, hpk, meta, cnt = _proj_res(o_att, x, w_o, g_ffn, mod, i, *rows_, w_router_t, b_router_t)
            pos, tile_group, n_used, pad_bounds = _moe_plan(meta, cnt)
            ypk = _moe(hpk, pos, tile_group, n_used, pad_bounds, wg, wu, wd, i)
            new_x.append(_combine(ypk, pos, xn, mod, i, *rows_, final_g))
        xp, xs = new_x

    return (xp.reshape(n_b, seq, d), xs.reshape(dec_b, n_lat, d),
            outs['na_k'], outs['na_v'], outs['ckv'], outs['kr'])
```

```python
import functools

import numpy as np
import jax
import jax.numpy as jnp
from jax import lax
from jax.experimental import pallas as pl
from jax.experimental.pallas import tpu as pltpu

F32 = jnp.float32
BF16 = jnp.bfloat16

D_MODEL = 1024
DEPTH = 2
GRID_W = 64
N_MOD = 6
EPS = 1e-6
NEG_INF = -1e30
NA_HEADS = 16
NA_HEAD_DIM = 64
NA_KH = 8
NA_KW = 16
NA_GROUP_ROWS = 4
NA_WIN_ROWS = 12
MLA_HEADS = 16
MLA_Q_RANK = 384
MLA_KV_RANK = 256
MLA_NOPE_DIM = 64
MLA_ROPE_DIM = 32
MLA_V_DIM = 64
MLA_QK_DIM = MLA_NOPE_DIM + MLA_ROPE_DIM
MLA_IN_PAD = 768
ROPE_THETA = 10000.0
LOG2_E = 1.4426950408889634
N_GROUPS = 4
EXPERTS_PER_GROUP = 4
N_EXPERTS = 16
D_FF_EXPERT = 256

LANES = 128
MOD_ROWS = 8
VMEM_LIMIT = 56 * 1024 * 1024
TM = 512
CAST_BLOCK_ELEMS = 512 * 1024
TM_MLA = 1024
TM_SUB_MLA = 512
TM_SUB = 512
TM_PROJ = 1024


def _cparams(sem):
    return pltpu.CompilerParams(dimension_semantics=sem, vmem_limit_bytes=VMEM_LIMIT)


def _const_spec(shape):
    nd = len(shape)
    return pl.BlockSpec(shape, lambda *_: (0,) * nd)


def _cast_kernel(x_ref, o_ref):
    o_ref[...] = x_ref[...].astype(o_ref.dtype)


def _to_bf16(x):
    cols = x.shape[-1]
    x2 = x.reshape(-1, cols)
    rows = x2.shape[0]
    block = rows
    while block * cols > CAST_BLOCK_ELEMS and block % 32 == 0:
        block //= 2
    out = pl.pallas_call(
        _cast_kernel,
        grid=(rows // block,),
        in_specs=[pl.BlockSpec((block, cols), lambda i: (i, 0))],
        out_specs=pl.BlockSpec((block, cols), lambda i: (i, 0)),
        out_shape=jax.ShapeDtypeStruct((rows, cols), BF16),
        compiler_params=_cparams(("arbitrary",)),
        name="to_bf16",
    )(x2)
    return out.reshape(x.shape)


def _norm_mod(x, g, shift, scale):
    ms = jnp.mean(x * x, axis=-1, keepdims=True)
    y = x * lax.rsqrt(ms + EPS) * g
    return y * (1.0 + scale) + shift


def _rms(x, g):
    ms = jnp.mean(x * x, axis=-1, keepdims=True)
    return x * lax.rsqrt(ms + EPS) * g


def _silu(x):
    return x / (1.0 + jnp.exp(-x))


def _ada_kernel(cond_ref, w_ref, b_ref, o_ref):
    s = _silu(cond_ref[...]).astype(BF16)
    o_ref[0] = jnp.dot(s, w_ref[0].astype(BF16), preferred_element_type=F32) + b_ref[0]


def _ada_mod(cond8, ada_w, ada_b):
    tn = 1024
    n = N_MOD * D_MODEL
    out = pl.pallas_call(
        _ada_kernel,
        grid=(DEPTH, n // tn),
        in_specs=[
            _const_spec((MOD_ROWS, D_MODEL)),
            pl.BlockSpec((1, D_MODEL, tn), lambda l, j: (l, 0, j)),
            pl.BlockSpec((1, 1, tn), lambda l, j: (l, 0, j)),
        ],
        out_specs=pl.BlockSpec((1, MOD_ROWS, tn), lambda l, j: (l, 0, j)),
        out_shape=jax.ShapeDtypeStruct((DEPTH, MOD_ROWS, n), F32),
        compiler_params=_cparams(("arbitrary", "arbitrary")),
        name="ada_mod",
    )(cond8, ada_w, ada_b.reshape(DEPTH, 1, n))
    return out.reshape(DEPTH * MOD_ROWS * N_MOD, 1, D_MODEL)


def _mod_spec(layer, k, row0, tokens_per_row, tm=TM):
    def imap(i):
        row = row0 + (i * tm) // tokens_per_row
        return ((layer * MOD_ROWS + row) * N_MOD + k, 0, 0)
    return pl.BlockSpec((1, 1, D_MODEL), imap)


def _qkv_kernel(x_ref, g_ref, sh_ref, sc_ref, w_ref, q_ref, k_ref, v_ref, ks_ref, vs_ref):
    h = _norm_mod(x_ref[...], g_ref[...], sh_ref[0], sc_ref[0]).astype(BF16)
    y = jnp.dot(h, w_ref[...], preferred_element_type=F32)
    q_ref[...] = (y[:, :D_MODEL] * (NA_HEAD_DIM ** -0.5)).astype(q_ref.dtype)
    k = y[:, D_MODEL:2 * D_MODEL]
    v = y[:, 2 * D_MODEL:]
    k_ref[...] = k.astype(k_ref.dtype)
    v_ref[...] = v.astype(v_ref.dtype)
    ks_ref[...] = pltpu.einshape("t(hd)->thd", k, h=NA_HEADS).reshape(ks_ref.shape)
    vs_ref[...] = pltpu.einshape("t(hd)->thd", v, h=NA_HEADS).reshape(vs_ref.shape)


def _na_qkv(x, g, mod, layer, row0, tokens_per_row, w_qkv, n_batch):
    t = x.shape[0]
    seq = t // n_batch
    nb = TM // seq
    tok = pl.BlockSpec((TM, D_MODEL), lambda i: (i, 0))
    state = pl.BlockSpec((nb, 1, seq, NA_HEADS, NA_HEAD_DIM), lambda i: (i, 0, 0, 0, 0))
    state_shape = jax.ShapeDtypeStruct((n_batch, 1, seq, NA_HEADS, NA_HEAD_DIM), F32)
    return pl.pallas_call(
        _qkv_kernel,
        grid=(t // TM,),
        in_specs=[tok, _const_spec((1, D_MODEL)),
                  _mod_spec(layer, 0, row0, tokens_per_row), _mod_spec(layer, 1, row0, tokens_per_row),
                  _const_spec((D_MODEL, 3 * D_MODEL))],
        out_specs=[tok, tok, tok, state, state],
        out_shape=[jax.ShapeDtypeStruct((t, D_MODEL), BF16)] * 3 + [state_shape, state_shape],
        compiler_params=_cparams(("arbitrary",)),
        name="na_qkv",
    )(x, g, mod, mod, w_qkv)


def _store_head_slots(vt, vt_ref):
    n = vt.shape[1]
    n_heads = vt_ref.shape[0] // LANES
    dim = vt.shape[0] // n_heads
    tail = jnp.where(lax.broadcasted_iota(jnp.int32, (LANES - dim, n), 0) == 0, 1.0, 0.0).astype(vt_ref.dtype)
    for hh in range(n_heads):
        vt_ref[hh * LANES:hh * LANES + dim, :] = vt[hh * dim:(hh + 1) * dim, :].astype(vt_ref.dtype)
        vt_ref[hh * LANES + dim:(hh + 1) * LANES, :] = tail


def _qkv_t_kernel(x_ref, g_ref, sh_ref, sc_ref, wqk_ref, wvt_ref, q_ref, k_ref, vt_ref):
    h = _norm_mod(x_ref[...], g_ref[...], sh_ref[0], sc_ref[0]).astype(BF16)
    y = jnp.dot(h, wqk_ref[...], preferred_element_type=F32)
    q_ref[...] = (y[:, :D_MODEL] * (NA_HEAD_DIM ** -0.5 * LOG2_E)).astype(q_ref.dtype)
    k_ref[...] = y[:, D_MODEL:].astype(k_ref.dtype)
    _store_head_slots(_qk(wvt_ref[...], h), vt_ref)


def _na_qkv_t(x, g, mod, layer, row0, tokens_per_row, w_qkv, w_vt):
    t = x.shape[0]
    tok = pl.BlockSpec((TM, D_MODEL), lambda i: (i, 0))
    slots = NA_HEADS * LANES
    return pl.pallas_call(
        _qkv_t_kernel,
        grid=(t // TM,),
        in_specs=[tok, _const_spec((1, D_MODEL)),
                  _mod_spec(layer, 0, row0, tokens_per_row), _mod_spec(layer, 1, row0, tokens_per_row),
                  pl.BlockSpec((D_MODEL, 2 * D_MODEL), lambda i: (0, 0)), _const_spec(w_vt.shape)],
        out_specs=[tok, tok, pl.BlockSpec((slots, TM), lambda i: (0, i))],
        out_shape=[jax.ShapeDtypeStruct((t, D_MODEL), BF16), jax.ShapeDtypeStruct((t, D_MODEL), BF16),
                   jax.ShapeDtypeStruct((slots, t), BF16)],
        compiler_params=_cparams(("arbitrary",)),
        name="na_qkv_t",
    )(x, g, mod, mod, w_qkv, w_vt)


def _softmax_pv(s_list, v_list):
    m = s_list[0].max(axis=-1, keepdims=True)
    for s in s_list[1:]:
        m = jnp.maximum(m, s.max(axis=-1, keepdims=True))
    l = None
    acc = None
    for s, v in zip(s_list, v_list):
        p = jnp.exp(s - m)
        ls = p.sum(axis=-1, keepdims=True)
        a = jnp.dot(p.astype(BF16), v, preferred_element_type=F32)
        l = ls if l is None else l + ls
        acc = a if acc is None else acc + a
    return acc * (1.0 / l)


def _qk(q, k):
    return lax.dot_general(q, k, (((1,), (1,)), ((), ())), preferred_element_type=F32)


def _head_masks():
    lane = lax.broadcasted_iota(jnp.int32, (1, LANES), 1)
    lo = lane < (LANES // 2)
    return lo, jnp.where(lo, 1.0, 0.0).astype(BF16), jnp.where(lo, 0.0, 1.0).astype(BF16)


def _attn_kernel(*refs, n_src, wide_qk, pairs, nb):
    q_ref, o_ref = refs[0], refs[-1]
    kv = refs[1:-1]
    lo, m0, m1 = _head_masks()
    wq = 2 * LANES if wide_qk else LANES
    tq = q_ref.shape[0] // nb
    for bb in range(nb):
        rq = slice(bb * tq, (bb + 1) * tq)
        rk = [slice(bb * (kv[2 * t].shape[0] // nb), (bb + 1) * (kv[2 * t].shape[0] // nb)) for t in range(n_src)]
        for p in range(pairs):
            q = q_ref[rq, p * wq:(p + 1) * wq]
            ks = [kv[2 * t][rk[t], p * wq:(p + 1) * wq].astype(BF16) for t in range(n_src)]
            vs = [kv[2 * t + 1][rk[t], p * LANES:(p + 1) * LANES].astype(BF16) for t in range(n_src)]
            outs = []
            for j in range(2):
                if wide_qk:
                    s_list = [_qk(q[:, j * LANES:(j + 1) * LANES], k[:, j * LANES:(j + 1) * LANES]) for k in ks]
                else:
                    s_list = [_qk(q * (m0 if j == 0 else m1), k) for k in ks]
                outs.append(_softmax_pv(s_list, vs))
            o_ref[rq, p * LANES:(p + 1) * LANES] = jnp.where(lo, outs[0], outs[1]).astype(o_ref.dtype)


def _attention(q, kv_srcs, n_batch, wide_qk, pairs, nb):
    wq = (2 * LANES if wide_qk else LANES) * pairs
    wv = LANES * pairs
    tq = q.shape[0] // n_batch * nb
    n_steps = D_MODEL // wv
    in_specs = [pl.BlockSpec((tq, wq), lambda b, p: (b, p))]
    args = [q]
    for k, v in kv_srcs:
        t_k = k.shape[0] // n_batch * nb
        in_specs.append(pl.BlockSpec((t_k, wq), lambda b, p: (b, p)))
        in_specs.append(pl.BlockSpec((t_k, wv), lambda b, p: (b, p)))
        args += [k, v]
    return pl.pallas_call(
        functools.partial(_attn_kernel, n_src=len(kv_srcs), wide_qk=wide_qk, pairs=pairs, nb=nb),
        grid=(n_batch // nb, n_steps),
        in_specs=in_specs,
        out_specs=pl.BlockSpec((tq, wv), lambda b, p: (b, p)),
        out_shape=jax.ShapeDtypeStruct((q.shape[0], D_MODEL), BF16),
        compiler_params=_cparams(("arbitrary", "arbitrary")),
        name="attention",
    )(*args)


def _col_max(s, chunk=64):
    acc = s[:chunk]
    for r in range(chunk, s.shape[0], chunk):
        acc = jnp.maximum(acc, s[r:r + chunk])
    return acc.max(axis=0, keepdims=True)


def _attn_t_kernel(q_ref, k1_ref, k2_ref, v1_ref, v2_ref, o_ref):
    n_heads = q_ref.shape[1] // LANES
    slots = [slice(j * LANES, (j + 1) * LANES) for j in range(n_heads)]
    def scores(sl):
        return _qk(k1_ref[:, sl], q_ref[:, sl]), _qk(k2_ref[:, sl], q_ref[:, sl])

    nxt = scores(slots[0])
    outs = []
    for n, sl in enumerate(slots):
        s1, s2 = nxt
        if n + 1 < n_heads:
            nxt = scores(slots[n + 1])
        m = jnp.maximum(_col_max(s1), _col_max(s2))
        p1 = jnp.exp2(s1 - m).astype(BF16)
        p2 = jnp.exp2(s2 - m).astype(BF16)
        ot = (jnp.dot(v1_ref[sl, :], p1, preferred_element_type=F32)
              + jnp.dot(v2_ref[sl, :], p2, preferred_element_type=F32))
        outs.append(ot[:MLA_V_DIM] * (1.0 / ot[MLA_V_DIM:MLA_V_DIM + 1]))
    for j in range(0, n_heads, 2):
        pair = jnp.concatenate(outs[j:j + 2], axis=0).T
        o_ref[:, (j // 2) * LANES:(j // 2 + 1) * LANES] = pair.astype(o_ref.dtype)


def _attention_t(q, k1, v1t, k2, v2t, n_batch, tq, heads):
    wq = heads * LANES
    wo = heads * MLA_V_DIM
    nq = q.shape[0] // n_batch // tq
    t1 = k1.shape[0] // n_batch
    t2 = k2.shape[0] // n_batch
    return pl.pallas_call(
        _attn_t_kernel,
        grid=(n_batch, MLA_HEADS // heads, nq),
        in_specs=[pl.BlockSpec((tq, wq), lambda b, p, i: (b * nq + i, p)),
                  pl.BlockSpec((t1, wq), lambda b, p, i: (b, p)),
                  pl.BlockSpec((t2, wq), lambda b, p, i: (b, p)),
                  pl.BlockSpec((wq, t1), lambda b, p, i: (p, b)),
                  pl.BlockSpec((wq, t2), lambda b, p, i: (p, b))],
        out_specs=pl.BlockSpec((tq, wo), lambda b, p, i: (b * nq + i, p)),
        out_shape=jax.ShapeDtypeStruct((q.shape[0], D_MODEL), BF16),
        compiler_params=_cparams(("arbitrary", "arbitrary", "arbitrary")),
        name="attention_t",
    )(q, k1, k2, v1t, v2t)


def _na_group_layout(rows):
    n_groups = rows // NA_GROUP_ROWS
    kh = min(NA_KH, rows)
    bases, types, keys = [], [], {}
    for gi in range(n_groups):
        r0 = gi * NA_GROUP_ROWS
        base = int(np.clip(r0 - kh // 2, 0, rows - NA_WIN_ROWS))
        starts = tuple(int(np.clip(r - kh // 2, 0, rows - kh)) - base for r in range(r0, r0 + NA_GROUP_ROWS))
        key = (starts, r0 - base)
        if key not in keys:
            keys[key] = len(keys)
        bases.append(base)
        types.append(keys[key])
    return bases, types, list(keys.keys())


def _na_bias_tables(rpb, rows):
    kh = min(NA_KH, rows)
    _, _, type_keys = _na_group_layout(rows)
    qc = np.arange(GRID_W)
    kc = np.arange(GRID_W)
    col_start = np.clip(qc - NA_KW // 2, 0, GRID_W - NA_KW)
    col_valid = (kc[None, :] >= col_start[:, None]) & (kc[None, :] < col_start[:, None] + NA_KW)
    dx = kc[None, :] - qc[:, None] + (NA_KW - 1)
    onehot = ((dx[None] == np.arange(2 * NA_KW - 1)[:, None, None]) & col_valid[None]).astype(np.float32)
    band = jnp.einsum('hyd,dqk->hykq', rpb.astype(F32), onehot, precision=lax.Precision.HIGHEST)
    band = jnp.where(col_valid.T, band, NEG_INF)
    strip = band.reshape(NA_HEADS, (2 * NA_KH - 1) * GRID_W, GRID_W)
    tables = []
    for starts, r_off in type_keys:
        q_cols = []
        for rq in range(NA_GROUP_ROWS):
            dy0 = starts[rq] - (rq + r_off) + (NA_KH - 1)
            piece = strip[:, dy0 * GRID_W:(dy0 + kh) * GRID_W, :]
            pad = ((0, 0), (starts[rq] * GRID_W, (NA_WIN_ROWS - kh - starts[rq]) * GRID_W), (0, 0))
            q_cols.append(jnp.pad(piece, pad, constant_values=NEG_INF))
        tables.append(jnp.concatenate(q_cols, axis=2))
    return jnp.stack(tables, axis=1) * LOG2_E


def _state_k_kernel(x_ref, o_ref):
    for h in range(NA_HEADS):
        o_ref[:, h * NA_HEAD_DIM:(h + 1) * NA_HEAD_DIM] = x_ref[0, 0, :, h, :].astype(o_ref.dtype)


def _state_vt_kernel(x_ref, o_ref, flat_ref):
    for h in range(NA_HEADS):
        flat_ref[:, h * NA_HEAD_DIM:(h + 1) * NA_HEAD_DIM] = x_ref[0, 0, :, h, :]
    _store_head_slots(flat_ref[...].T, o_ref)


def _state_k(x, layer):
    b, _, t, h, dh = x.shape
    return pl.pallas_call(
        _state_k_kernel,
        grid=(b,),
        in_specs=[pl.BlockSpec((1, 1, t, h, dh), lambda i: (i, layer, 0, 0, 0))],
        out_specs=pl.BlockSpec((t, h * dh), lambda i: (i, 0)),
        out_shape=jax.ShapeDtypeStruct((b * t, h * dh), BF16),
        compiler_params=_cparams(("arbitrary",)),
        name="state_k",
    )(x)


def _state_vt(x, layer):
    b, _, t, h, dh = x.shape
    return pl.pallas_call(
        _state_vt_kernel,
        grid=(b,),
        in_specs=[pl.BlockSpec((1, 1, t, h, dh), lambda i: (i, layer, 0, 0, 0))],
        out_specs=pl.BlockSpec((h * LANES, t), lambda i: (0, i)),
        out_shape=jax.ShapeDtypeStruct((h * LANES, b * t), BF16),
        scratch_shapes=[pltpu.VMEM((t, h * dh), F32)],
        compiler_params=_cparams(("arbitrary",)),
        name="state_vt",
    )(x)


def _na_latent_kernel(q_ref, k_ref, vt_ref, kc_ref, vct_ref, tab_ref, o_ref, *, bases, types):
    _, m0, m1 = _head_masks()
    kctx = kc_ref[...]
    gq = NA_GROUP_ROWS * GRID_W
    gk = NA_WIN_ROWS * GRID_W
    blocks = [(gi, base, ty, j) for gi, (base, ty) in enumerate(zip(bases, types)) for j in range(2)]

    def scores(gi, base, ty, j):
        qj = q_ref[gi * gq:(gi + 1) * gq, :] * (m0 if j == 0 else m1)
        kwin = k_ref[base * GRID_W:base * GRID_W + gk, :]
        return _qk(kwin, qj) + tab_ref[j, ty], _qk(kctx, qj)

    nxt = scores(*blocks[0])
    outs = []
    for n, (gi, base, ty, j) in enumerate(blocks):
        s_win, s_ctx = nxt
        if n + 1 < len(blocks):
            nxt = scores(*blocks[n + 1])
        m = jnp.maximum(_col_max(s_win), _col_max(s_ctx))
        p_win = jnp.exp2(s_win - m).astype(BF16)
        p_ctx = jnp.exp2(s_ctx - m).astype(BF16)
        sl = slice(j * LANES, (j + 1) * LANES)
        ot = (jnp.dot(vt_ref[sl, base * GRID_W:base * GRID_W + gk], p_win, preferred_element_type=F32)
              + jnp.dot(vct_ref[sl, :], p_ctx, preferred_element_type=F32))
        outs.append(ot[:NA_HEAD_DIM] * (1.0 / ot[NA_HEAD_DIM:NA_HEAD_DIM + 1]))
        if j == 1:
            o_ref[gi * gq:(gi + 1) * gq, :] = jnp.concatenate(outs, axis=0).T.astype(o_ref.dtype)
            outs = []


def _na_latent(q, k, vt, k_ctx, vt_ctx, tables, n_batch):
    t = q.shape[0] // n_batch
    rows = t // GRID_W
    t_ctx = k_ctx.shape[0] // n_batch
    bases, types, type_keys = _na_group_layout(rows)
    n_types = len(type_keys)
    gq = NA_GROUP_ROWS * GRID_W
    gk = NA_WIN_ROWS * GRID_W
    tok = pl.BlockSpec((t, LANES), lambda p, b: (b, p))
    ctx = pl.BlockSpec((t_ctx, LANES), lambda p, b: (b, p))
    return pl.pallas_call(
        functools.partial(_na_latent_kernel, bases=bases, types=types),
        grid=(D_MODEL // LANES, n_batch),
        in_specs=[tok, tok, pl.BlockSpec((2 * LANES, t), lambda p, b: (p, b)),
                  ctx, pl.BlockSpec((2 * LANES, t_ctx), lambda p, b: (p, b)),
                  pl.BlockSpec((2, n_types, gk, gq), lambda p, b: (p, 0, 0, 0))],
        out_specs=tok,
        out_shape=jax.ShapeDtypeStruct(q.shape, BF16),
        compiler_params=_cparams(("arbitrary", "arbitrary")),
        name="na_latent",
    )(q, k, vt, k_ctx, vt_ctx, tables)


def _rope_lane_tables(n_tokens):
    f32 = np.float32
    n_freq = MLA_ROPE_DIM // 4
    inv = f32(1.0) / (f32(ROPE_THETA) ** (np.arange(n_freq, dtype=f32) / f32(n_freq)))
    tt = np.arange(n_tokens)
    row = (tt // GRID_W).astype(f32)
    col = (tt % GRID_W).astype(f32)
    ang = np.concatenate([row[:, None] * inv, col[:, None] * inv], axis=-1).astype(f32)
    cos, sin = np.cos(ang).astype(f32), np.sin(ang).astype(f32)
    ones = np.ones((n_tokens, MLA_NOPE_DIM), f32)
    z16 = np.zeros((n_tokens, MLA_ROPE_DIM // 2), f32)
    z32 = np.zeros((n_tokens, LANES - MLA_QK_DIM), f32)
    z64 = np.zeros((n_tokens, MLA_NOPE_DIM), f32)
    c_tab = np.concatenate([ones, cos, cos, z32], axis=-1)
    s_dn = np.concatenate([z64, -sin, z16, z32], axis=-1)
    s_up = np.concatenate([z64, z16, sin, z32], axis=-1)
    return jnp.asarray(c_tab), jnp.asarray(s_dn), jnp.asarray(s_up)


def _rope_slot(x, c_tab, s_dn, s_up):
    half = MLA_ROPE_DIM // 2
    return x * c_tab + pltpu.roll(x, LANES - half, 1) * s_dn + pltpu.roll(x, half, 1) * s_up


def _mla_kv(ckv_n, kr_slot, wuk_ref, wuv_ref, k_ref, v_ref, v_transposed, rows=slice(None)):
    c = ckv_n.astype(BF16)
    k_nope = jnp.dot(c, wuk_ref[...], preferred_element_type=F32)
    for h in range(MLA_HEADS):
        k_ref[rows, h * LANES:(h + 1) * LANES] = (k_nope[:, h * LANES:(h + 1) * LANES] + kr_slot).astype(k_ref.dtype)
    if v_transposed:
        _store_head_slots(_qk(wuv_ref[...], c), v_ref.at[:, rows])
    else:
        v_ref[rows, :] = jnp.dot(c, wuv_ref[...], preferred_element_type=F32).astype(v_ref.dtype)


def _mla_proj_kernel(*refs, rope, emit_state, q_scale):
    (x_ref, g_ref, sh_ref, sc_ref, win_ref, qg_ref, kvg_ref, wuq_ref, wuk_ref, wuv_ref) = refs[:10]
    pos = 10
    if rope:
        ct_ref, sd_ref, su_ref = refs[pos:pos + 3]
        pos += 3
    q_ref, k_ref, v_ref = refs[pos:pos + 3]
    pos += 3

    def front(rows):
        h = _norm_mod(x_ref[rows, :], g_ref[...], sh_ref[0], sc_ref[0]).astype(BF16)
        c = jnp.dot(h, win_ref[...], preferred_element_type=F32)
        cq = _rms(c[:, :MLA_Q_RANK], qg_ref[...]).astype(BF16)
        q = jnp.dot(cq, wuq_ref[...], preferred_element_type=F32) * q_scale
        ckv = _rms(c[:, MLA_Q_RANK:MLA_Q_RANK + MLA_KV_RANK], kvg_ref[...])
        kr_chunk = c[:, MLA_Q_RANK + MLA_KV_RANK:]
        if emit_state:
            ckv_out, kr_out = refs[pos:pos + 2]
            ckv_out[rows, :] = ckv
            kr_out[rows, :] = kr_chunk[:, :MLA_ROPE_DIM]
        return q, ckv, pltpu.roll(kr_chunk, MLA_NOPE_DIM, 1)

    def back(rows, q, ckv, kr_slot):
        if rope:
            ct, sd, su = ct_ref[rows, :], sd_ref[rows, :], su_ref[rows, :]
            kr_slot = _rope_slot(kr_slot, ct, sd, su)
        _mla_kv(ckv, kr_slot, wuk_ref, wuv_ref, k_ref, v_ref, v_transposed=rope, rows=rows)
        for hh in range(MLA_HEADS):
            qh = q[:, hh * LANES:(hh + 1) * LANES]
            if rope:
                qh = _rope_slot(qh, ct, sd, su)
            q_ref[rows, hh * LANES:(hh + 1) * LANES] = qh.astype(q_ref.dtype)

    tiles = [slice(r0, r0 + TM_SUB_MLA) for r0 in range(0, x_ref.shape[0], TM_SUB_MLA)]
    nxt = front(tiles[0])
    for n, rows in enumerate(tiles):
        cur = nxt
        if n + 1 < len(tiles):
            nxt = front(tiles[n + 1])
        back(rows, *cur)


def _mla_proj(x, g, mod, layer, row0, tokens_per_row, w, rope_tabs, emit_state):
    t = x.shape[0]
    win, qg, kvg, wuq, wuk, wuv = w
    tm = TM_MLA
    tok = lambda n: pl.BlockSpec((tm, n), lambda i: (i, 0))
    wq = MLA_HEADS * LANES
    in_specs = [tok(D_MODEL), _const_spec((1, D_MODEL)),
                _mod_spec(layer, 0, row0, tokens_per_row, tm), _mod_spec(layer, 1, row0, tokens_per_row, tm),
                _const_spec(win.shape), _const_spec(qg.shape), _const_spec(kvg.shape),
                _const_spec(wuq.shape), _const_spec(wuk.shape), _const_spec(wuv.shape)]
    args = [x, g, mod, mod, win, qg, kvg, wuq, wuk, wuv]
    rope = rope_tabs is not None
    if rope:
        n_tab = rope_tabs[0].shape[0] // tm
        in_specs += [pl.BlockSpec((tm, LANES), lambda i: (i % n_tab, 0)) for _ in range(3)]
        args += list(rope_tabs)
    if rope:
        v_spec, v_shape = pl.BlockSpec((wq, tm), lambda i: (0, i)), jax.ShapeDtypeStruct((wq, t), BF16)
        q_scale = MLA_QK_DIM ** -0.5 * LOG2_E
    else:
        v_spec, v_shape = tok(D_MODEL), jax.ShapeDtypeStruct((t, D_MODEL), BF16)
        q_scale = MLA_QK_DIM ** -0.5
    out_specs = [tok(wq), tok(wq), v_spec]
    out_shape = [jax.ShapeDtypeStruct((t, wq), BF16), jax.ShapeDtypeStruct((t, wq), BF16), v_shape]
    if emit_state:
        out_specs += [tok(MLA_KV_RANK), tok(MLA_ROPE_DIM)]
        out_shape += [jax.ShapeDtypeStruct((t, MLA_KV_RANK), F32), jax.ShapeDtypeStruct((t, MLA_ROPE_DIM), F32)]
    return pl.pallas_call(
        functools.partial(_mla_proj_kernel, rope=rope, emit_state=emit_state, q_scale=q_scale),
        grid=(t // tm,),
        in_specs=in_specs, out_specs=out_specs, out_shape=out_shape,
        compiler_params=_cparams(("arbitrary",)),
        name="mla_proj",
    )(*args)


def _mla_state_kv_kernel(ckv_ref, kr_ref, wuk_ref, wuv_ref, k_ref, v_ref):
    kr_slot = pltpu.roll(kr_ref[...], MLA_NOPE_DIM, 1)
    _mla_kv(ckv_ref[...], kr_slot, wuk_ref, wuv_ref, k_ref, v_ref, v_transposed=True)


def _mla_state_kv(ckv, kr_pad, wuk, wuv_t):
    t = ckv.shape[0]
    wq = MLA_HEADS * LANES
    tok = lambda n: pl.BlockSpec((TM, n), lambda i: (i, 0))
    return pl.pallas_call(
        _mla_state_kv_kernel,
        grid=(t // TM,),
        in_specs=[tok(MLA_KV_RANK), tok(LANES), _const_spec(wuk.shape), _const_spec(wuv_t.shape)],
        out_specs=[tok(wq), pl.BlockSpec((wq, TM), lambda i: (0, i))],
        out_shape=[jax.ShapeDtypeStruct((t, wq), BF16), jax.ShapeDtypeStruct((wq, t), BF16)],
        compiler_params=_cparams(("arbitrary",)),
        name="mla_state_kv",
    )(ckv, kr_pad, wuk, wuv_t)


def _first_argmax_rows(v, row):
    top = v.max(axis=0, keepdims=True)
    idx = jnp.where(v == top, row, float(v.shape[0])).min(axis=0, keepdims=True)
    return top, idx


ROUTER_ROWS = 32
ROUTER_EXPERT_ROW = 8


def _route_t(lt):
    n = lt.shape[1]
    row8 = lax.broadcasted_iota(jnp.int32, (8, n), 0).astype(F32)
    row16 = lax.broadcasted_iota(jnp.int32, (N_EXPERTS, n), 0).astype(F32)
    is_g = row8 < N_GROUPS
    gl = jnp.where(is_g, lt[:8], -jnp.inf)
    gmax, gsel = _first_argmax_rows(gl, row8)
    p_group = 1.0 / jnp.where(is_g, jnp.exp(gl - gmax), 0.0).sum(axis=0, keepdims=True)
    first = gsel * EXPERTS_PER_GROUP
    e_all = lt[ROUTER_EXPERT_ROW:ROUTER_EXPERT_ROW + N_EXPERTS]
    el = jnp.where((row16 >= first) & (row16 < first + EXPERTS_PER_GROUP), e_all, -jnp.inf)
    t1, i1 = _first_argmax_rows(el, row16)
    t2, i2 = _first_argmax_rows(jnp.where(row16 == i1, -jnp.inf, el), row16)
    e2 = jnp.exp(t2 - t1)
    w1 = 1.0 / (1.0 + e2)
    gates = jnp.where(row16 == i1, w1 * p_group, jnp.where(row16 == i2, e2 * w1 * p_group, 0.0))
    rel = gates[:8] + gates[8:]
    rel = rel + pltpu.roll(rel, EXPERTS_PER_GROUP, 0)
    meta_t = jnp.where(row8 < EXPERTS_PER_GROUP, rel, jnp.where(row8 == META_GROUP_LANE, gsel, 0.0))
    onehot_t = jnp.where(row8 == gsel, 1.0, 0.0)
    return meta_t, onehot_t, row8, gsel


def _pack_halves(x):
    half = x.shape[1] // 2
    lo = lax.bitcast_convert_type(x[:, :half].astype(BF16).astype(F32), jnp.uint32)
    hi = lax.bitcast_convert_type(x[:, half:].astype(BF16).astype(F32), jnp.uint32)
    return (lo >> 16) | hi


def _unpack_halves(p):
    lo = lax.bitcast_convert_type(p << 16, F32)
    hi = lax.bitcast_convert_type(p & jnp.uint32(0xFFFF0000), F32)
    return lo, hi


ROW_WORDS = D_MODEL // 2 + LANES
META_GROUP_LANE = EXPERTS_PER_GROUP
META_RANK_LANE = EXPERTS_PER_GROUP + 1


def _proj_res_kernel(o_ref, x_ref, w_ref, gate_ref, g_ref, sh_ref, sc_ref, wrt_ref, brt_ref, tri_ref,
                     xn_ref, hpk_ref, metat_ref, cnt_ref):
    @pl.when(pl.program_id(0) == 0)
    def _():
        cnt_ref[...] = jnp.zeros(cnt_ref.shape, cnt_ref.dtype)

    sub = tri_ref.shape[0]
    cnt = cnt_ref[:, :1]
    tiles = [slice(r0, r0 + sub) for r0 in range(0, x_ref.shape[0], sub)]

    def out_proj(rows):
        return jnp.dot(o_ref[rows, :], w_ref[...], preferred_element_type=F32)

    y_next = out_proj(tiles[0])
    for n, rows in enumerate(tiles):
        y = y_next
        if n + 1 < len(tiles):
            y_next = out_proj(tiles[n + 1])
        xn = x_ref[rows, :] + gate_ref[0] * y
        xn_ref[rows, :] = xn
        h = _norm_mod(xn, g_ref[...], sh_ref[0], sc_ref[0])
        lt = _qk(wrt_ref[...], h.astype(BF16)) + brt_ref[...]
        meta_t, onehot_t, row8, gsel = _route_t(lt)
        prefix = jnp.dot(onehot_t.astype(BF16), tri_ref[...], preferred_element_type=F32)
        rank = jnp.where(row8 == gsel, prefix + cnt - 1.0, 0.0).sum(axis=0, keepdims=True)
        cnt = cnt + onehot_t.sum(axis=1, keepdims=True)
        meta_t = jnp.where(row8 == META_RANK_LANE, rank, meta_t)
        meta = jnp.concatenate([meta_t, jnp.zeros((LANES - 8, sub), F32)], axis=0).T
        metat_ref[:, rows] = meta_t
        hpk_ref[rows, :] = jnp.concatenate([_pack_halves(h), lax.bitcast_convert_type(meta, jnp.uint32)], axis=1)
    cnt_ref[...] = jnp.broadcast_to(cnt, cnt_ref.shape)


def _proj_res(o, x, w_o, g_ffn, mod, layer, row0, tokens_per_row, w_router_t, b_router_t):
    t = x.shape[0]
    tm = TM_PROJ
    tok = lambda n: pl.BlockSpec((tm, n), lambda i: (i, 0))
    ms = lambda k: _mod_spec(layer, k, row0, tokens_per_row, tm)
    tri = jnp.asarray(np.triu(np.ones((TM_SUB, TM_SUB), np.float32)), BF16)
    return pl.pallas_call(
        _proj_res_kernel,
        grid=(t // tm,),
        in_specs=[tok(D_MODEL), tok(D_MODEL), _const_spec(w_o.shape), ms(2), _const_spec((1, D_MODEL)), ms(3), ms(4),
                  _const_spec(w_router_t.shape), _const_spec(b_router_t.shape), _const_spec((TM_SUB, TM_SUB))],
        out_specs=[tok(D_MODEL), tok(ROW_WORDS), pl.BlockSpec((8, tm), lambda i: (0, i)), _const_spec((8, LANES))],
        out_shape=[jax.ShapeDtypeStruct((t, D_MODEL), F32), jax.ShapeDtypeStruct((t, ROW_WORDS), jnp.uint32),
                   jax.ShapeDtypeStruct((8, t), F32), jax.ShapeDtypeStruct((8, LANES), F32)],
        compiler_params=_cparams(("arbitrary",)),
        name="proj_res",
    )(o, x, w_o, mod, g_ffn, mod, mod, w_router_t, b_router_t, tri)


def _moe_plan(meta_t, cnt):
    t = meta_t.shape[1]
    n_tiles = t // TM + N_GROUPS
    group = meta_t[META_GROUP_LANE].astype(jnp.int32)
    rank = meta_t[META_RANK_LANE].astype(jnp.int32)
    counts = cnt[:N_GROUPS, 0].astype(jnp.int32)
    tiles = (counts + TM - 1) // TM
    tile_end = jnp.cumsum(tiles)
    tile_start = tile_end - tiles
    pos = rank + TM * jnp.sum(jnp.where(group[:, None] == jnp.arange(N_GROUPS)[None], tile_start[None], 0), axis=1)
    tile_group = jnp.minimum(jnp.sum(jnp.arange(n_tiles)[:, None] >= tile_end[None], axis=1), N_GROUPS - 1)
    pad_bounds = jnp.stack([tile_start * TM + counts, tile_end * TM], axis=1).reshape(-1)
    return (pos.astype(jnp.int32), tile_group.astype(jnp.int32), tile_end[-1:].astype(jnp.int32),
            pad_bounds.astype(jnp.int32))


def _gather_rows(srcs, idx_ref, base, dsts, n_rows):
    for r in range(n_rows):
        s = idx_ref[base + r]
        for src_ref, dst_ref in zip(srcs, dsts):
            dst_ref[r:r + 1, :] = src_ref[pl.ds(s, 1), :]


def _moe_kernel(pos_ref, tg_ref, nt_ref, pad_ref, hpk_ref, wg_ref, wu_ref, wd_ref, ypk_ref, src_ref, hbuf, hid_ref):
    i = pl.program_id(0)
    n_tok = hpk_ref.shape[0]

    @pl.when(i == 0)
    def _():
        def clear(p, c):
            src_ref[p] = 0
            return c
        for g in range(N_GROUPS):
            lax.fori_loop(pad_ref[2 * g], pad_ref[2 * g + 1], clear, 0)

        def fill(t, c):
            src_ref[pos_ref[t]] = t
            return c
        lax.fori_loop(0, n_tok, fill, 0, unroll=8)
        _gather_rows((hpk_ref,), src_ref, 0, (hbuf.at[0],), TM)

    @pl.when(i < nt_ref[0])
    def _():
        slot = i % 2
        row = hbuf[slot]
        lo, hi = _unpack_halves(row[:, :D_MODEL // 2])
        h = jnp.concatenate([lo.astype(BF16), hi.astype(BF16)], axis=1)
        gates = lax.bitcast_convert_type(row[:, D_MODEL // 2:], F32)
        nxt = jnp.minimum(i + 1, nt_ref[0] - 1) * TM
        _gather_rows((hpk_ref,), src_ref, nxt, (hbuf.at[1 - slot],), TM)
        f = D_FF_EXPERT
        for j in range(EXPERTS_PER_GROUP):
            gate_act = _silu(jnp.dot(h, wg_ref[0, j], preferred_element_type=F32))
            hid = gate_act * jnp.dot(h, wu_ref[0, j], preferred_element_type=F32) * gates[:, j:j + 1]
            hid_ref[:, j * f:(j + 1) * f] = hid.astype(BF16)
        ypk_ref[...] = _pack_halves(jnp.dot(hid_ref[...], wd_ref[0], preferred_element_type=F32))

    @pl.when(i >= nt_ref[0])
    def _():
        ypk_ref[...] = jnp.zeros(ypk_ref.shape, ypk_ref.dtype)


def _moe(hpk, pos, tile_group, n_used, pad_bounds, wg, wu, wd, layer):
    t = hpk.shape[0]
    n_tiles = t // TM + N_GROUPS
    f = D_FF_EXPERT
    g0 = layer * N_GROUPS
    grid_spec = pltpu.PrefetchScalarGridSpec(
        num_scalar_prefetch=4,
        grid=(n_tiles,),
        in_specs=[
            pl.BlockSpec(hpk.shape, lambda i, *_: (0, 0), pipeline_mode=pl.Buffered(1)),
            pl.BlockSpec((1, EXPERTS_PER_GROUP, D_MODEL, f), lambda i, pos, tg, nt, pad: (g0 + tg[i], 0, 0, 0)),
            pl.BlockSpec((1, EXPERTS_PER_GROUP, D_MODEL, f), lambda i, pos, tg, nt, pad: (g0 + tg[i], 0, 0, 0)),
            pl.BlockSpec((1, EXPERTS_PER_GROUP * f, D_MODEL), lambda i, pos, tg, nt, pad: (g0 + tg[i], 0, 0)),
        ],
        out_specs=pl.BlockSpec((TM, D_MODEL // 2), lambda i, *_: (i, 0)),
        scratch_shapes=[pltpu.SMEM((n_tiles * TM,), jnp.int32),
                        pltpu.VMEM((2, TM, ROW_WORDS), jnp.uint32),
                        pltpu.VMEM((TM, EXPERTS_PER_GROUP * f), BF16)],
    )
    return pl.pallas_call(
        _moe_kernel,
        grid_spec=grid_spec,
        out_shape=jax.ShapeDtypeStruct((n_tiles * TM, D_MODEL // 2), jnp.uint32),
        compiler_params=_cparams(("arbitrary",)),
        name="moe",
    )(pos, tile_group, n_used, pad_bounds, hpk, wg, wu, wd)


def _combine_kernel(*refs, final_norm):
    pos_ref, ypk_ref, x_ref, gate_ref = refs[:4]
    o_ref, ybuf = refs[-2], refs[-1]
    _gather_rows((ypk_ref,), pos_ref, pl.program_id(0) * TM, (ybuf,), TM)
    lo, hi = _unpack_halves(ybuf[...])
    out = x_ref[...] + gate_ref[0] * jnp.concatenate([lo, hi], axis=1)
    if final_norm:
        out = _rms(out, refs[4][...])
    o_ref[...] = out


def _combine(ypk, pos, x, mod, layer, row0, tokens_per_row, final_g):
    t = x.shape[0]

    def mod_map(i, pos):
        row = row0 + (i * TM) // tokens_per_row
        return ((layer * MOD_ROWS + row) * N_MOD + 5, 0, 0)

    in_specs = [pl.BlockSpec(ypk.shape, lambda i, pos: (0, 0), pipeline_mode=pl.Buffered(1)),
                pl.BlockSpec((TM, D_MODEL), lambda i, pos: (i, 0)),
                pl.BlockSpec((1, 1, D_MODEL), mod_map)]
    args = [ypk, x, mod]
    if final_g is not None:
        in_specs.append(pl.BlockSpec((1, D_MODEL), lambda i, pos: (0, 0)))
        args.append(final_g)
    grid_spec = pltpu.PrefetchScalarGridSpec(
        num_scalar_prefetch=1,
        grid=(t // TM,),
        in_specs=in_specs,
        out_specs=pl.BlockSpec((TM, D_MODEL), lambda i, pos: (i, 0)),
        scratch_shapes=[pltpu.VMEM((TM, D_MODEL // 2), jnp.uint32)],
    )
    return pl.pallas_call(
        functools.partial(_combine_kernel, final_norm=final_g is not None),
        grid_spec=grid_spec,
        out_shape=jax.ShapeDtypeStruct((t, D_MODEL), F32),
        compiler_params=_cparams(("arbitrary",)),
        name="moe_combine",
    )(pos, *args)


def _pad_heads(w, used):
    k = w.shape[0]
    w = w.reshape(k, MLA_HEADS, used)
    return jnp.pad(w, ((0, 0), (0, 0), (0, LANES - used))).reshape(k, MLA_HEADS * LANES)


def kernel(x_prompt, x_sample, state_na_k, state_na_v, state_mla_ckv, state_mla_kr, c, c_ctx, ada_w, ada_b, norm_mix_g, norm_ffn_g, final_norm_g, na_w_qkv, na_w_o, na_rpb, mla_w_in, mla_q_norm_g, mla_kv_norm_g, mla_w_uq, mla_w_uk, mla_w_uv, mla_w_o, moe_w_group, moe_b_group, moe_w_expert, moe_b_expert, moe_w_gate, moe_w_up, moe_w_down):
    n_b, seq, d = x_prompt.shape
    dec_b, n_lat, _ = x_sample.shape
    past = state_na_k.shape[2]
    rows = n_lat // GRID_W
    assert d == D_MODEL and dec_b + 1 <= MOD_ROWS and n_lat % TM_PROJ == 0 and (n_b * seq) % TM_PROJ == 0

    cond8 = jnp.concatenate([c_ctx[None], c, jnp.zeros((MOD_ROWS - 1 - dec_b, d), F32)], axis=0)
    mod = _ada_mod(cond8, ada_w, ada_b)

    xp = x_prompt.reshape(n_b * seq, d)
    xs = x_sample.reshape(dec_b * n_lat, d)
    p_rows = (0, n_b * seq)
    s_rows = (1, n_lat)
    outs = {}

    f = D_FF_EXPERT
    wg = _to_bf16(moe_w_gate).reshape(DEPTH * N_GROUPS, EXPERTS_PER_GROUP, d, f)
    wu = _to_bf16(moe_w_up).reshape(DEPTH * N_GROUPS, EXPERTS_PER_GROUP, d, f)
    wd = _to_bf16(moe_w_down).reshape(DEPTH * N_GROUPS, EXPERTS_PER_GROUP * f, d)

    for i in range(DEPTH):
        j = i // 2
        g_mix = norm_mix_g[i][None]
        g_ffn = norm_ffn_g[i][None]
        if i % 2 == 0:
            w_qkv = _to_bf16(na_w_qkv[j])
            w_vt = na_w_qkv[j][:, 2 * d:].T.astype(BF16)
            qp, kp, vp, outs['na_k'], outs['na_v'] = _na_qkv(xp, g_mix, mod, i, *p_rows, w_qkv, n_b)
            qs, ks, vs_t = _na_qkv_t(xs, g_mix, mod, i, *s_rows, w_qkv, w_vt)
            op = _attention(qp, [(kp, vp)], n_b, wide_qk=False, pairs=8, nb=4)
            tables = _na_bias_tables(na_rpb[j], rows)
            os_ = _na_latent(qs, ks, vs_t, _state_k(state_na_k, j), _state_vt(state_na_v, j), tables, dec_b)
            w_o = _to_bf16(na_w_o[j])
        else:
            pad_in = MLA_IN_PAD - mla_w_in.shape[-1]
            w_mla = (jnp.pad(mla_w_in[j], ((0, 0), (0, pad_in))).astype(BF16),
                     mla_q_norm_g[j][None], mla_kv_norm_g[j][None],
                     _pad_heads(mla_w_uq[j], MLA_QK_DIM).astype(BF16),
                     _pad_heads(mla_w_uk[j], MLA_NOPE_DIM).astype(BF16),
                     mla_w_uv[j].astype(BF16))
            wuv_t = mla_w_uv[j].T.astype(BF16)
            w_mla_t = w_mla[:5] + (wuv_t,)
            qp, kp, vp, ckv_p, kr_p = _mla_proj(xp, g_mix, mod, i, *p_rows, w_mla, None, True)
            qs, ks, vs_t = _mla_proj(xs, g_mix, mod, i, *s_rows, w_mla_t, _rope_lane_tables(n_lat), False)
            outs['ckv'] = ckv_p.reshape(n_b, 1, seq, MLA_KV_RANK)
            outs['kr'] = kr_p.reshape(n_b, 1, seq, MLA_ROPE_DIM)
            kr_state = jnp.pad(state_mla_kr[:, j].reshape(dec_b * past, MLA_ROPE_DIM),
                               ((0, 0), (0, LANES - MLA_ROPE_DIM)))
            kc, vc_t = _mla_state_kv(state_mla_ckv[:, j].reshape(dec_b * past, MLA_KV_RANK), kr_state,
                                     w_mla[4], wuv_t)
            op = _attention(qp, [(kp, vp)], n_b, wide_qk=True, pairs=8, nb=4)
            os_ = _attention_t(qs, ks, vs_t, kc, vc_t, dec_b, 512, heads=8)
            w_o = _to_bf16(mla_w_o[j])

        pad_g = ((0, ROUTER_EXPERT_ROW - N_GROUPS), (0, 0))
        pad_e = ((0, ROUTER_ROWS - ROUTER_EXPERT_ROW - N_EXPERTS), (0, 0))
        w_router_t = jnp.concatenate([jnp.pad(moe_w_group[i].T, pad_g), jnp.pad(moe_w_expert[i].T, pad_e)],
                                     axis=0).astype(BF16)
        b_router_t = jnp.concatenate([jnp.pad(moe_b_group[i][:, None], pad_g),
                                      jnp.pad(moe_b_expert[i][:, None], pad_e)], axis=0)
        b_router_t = jnp.broadcast_to(b_router_t, (ROUTER_ROWS, TM_SUB))
        final_g = final_norm_g[None] if i == DEPTH - 1 else None
        new_x = []
        for o_att, x, rows_ in ((op, xp, p_rows), (os_, xs, s_rows)):
            xn, hpk, meta, cnt = _proj_res(o_att, x, w_o, g_ffn, mod, i, *rows_, w_router_t, b_router_t)
            pos, tile_group, n_used, pad_bounds = _moe_plan(meta, cnt)
            ypk = _moe(hpk, pos, tile_group, n_used, pad_bounds, wg, wu, wd, i)
            new_x.append(_combine(ypk, pos, xn, mod, i, *rows_, final_g))
        xp, xs = new_x

    return (xp.reshape(n_b, seq, d), xs.reshape(dec_b, n_lat, d),
            outs['na_k'], outs['na_v'], outs['ckv'], outs['kr'])
```

```python
import functools

import numpy as np
import jax
import jax.numpy as jnp
from jax import lax
from jax.experimental import pallas as pl
from jax.experimental.pallas import tpu as pltpu

F32 = jnp.float32
BF16 = jnp.bfloat16

D_MODEL = 1024
DEPTH = 2
GRID_W = 64
N_MOD = 6
EPS = 1e-6
NEG_INF = -1e30
NA_HEADS = 16
NA_HEAD_DIM = 64
NA_KH = 8
NA_KW = 16
NA_GROUP_ROWS = 4
NA_WIN_ROWS = 12
MLA_HEADS = 16
MLA_Q_RANK = 384
MLA_KV_RANK = 256
MLA_NOPE_DIM = 64
MLA_ROPE_DIM = 32
MLA_V_DIM = 64
MLA_QK_DIM = MLA_NOPE_DIM + MLA_ROPE_DIM
MLA_IN_PAD = 768
ROPE_THETA = 10000.0
LOG2_E = 1.4426950408889634
N_GROUPS = 4
EXPERTS_PER_GROUP = 4
N_EXPERTS = 16
D_FF_EXPERT = 256

LANES = 128
MOD_ROWS = 8
VMEM_LIMIT = 56 * 1024 * 1024
TM = 512
CAST_BLOCK_ELEMS = 512 * 1024
TM_MLA = 1024
TM_SUB_MLA = 512
TM_SUB = 512
TM_PROJ = 1024


def _cparams(sem):
    return pltpu.CompilerParams(dimension_semantics=sem, vmem_limit_bytes=VMEM_LIMIT)


def _const_spec(shape):
    nd = len(shape)
    return pl.BlockSpec(shape, lambda *_: (0,) * nd)


def _cast_kernel(x_ref, o_ref):
    o_ref[...] = x_ref[...].astype(o_ref.dtype)


def _to_bf16(x):
    cols = x.shape[-1]
    x2 = x.reshape(-1, cols)
    rows = x2.shape[0]
    block = rows
    while block * cols > CAST_BLOCK_ELEMS and block % 32 == 0:
        block //= 2
    out = pl.pallas_call(
        _cast_kernel,
        grid=(rows // block,),
        in_specs=[pl.BlockSpec((block, cols), lambda i: (i, 0))],
        out_specs=pl.BlockSpec((block, cols), lambda i: (i, 0)),
        out_shape=jax.ShapeDtypeStruct((rows, cols), BF16),
        compiler_params=_cparams(("arbitrary",)),
        name="to_bf16",
    )(x2)
    return out.reshape(x.shape)


def _norm_mod(x, g, shift, scale):
    ms = jnp.mean(x * x, axis=-1, keepdims=True)
    y = x * lax.rsqrt(ms + EPS) * g
    return y * (1.0 + scale) + shift


def _rms(x, g):
    ms = jnp.mean(x * x, axis=-1, keepdims=True)
    return x * lax.rsqrt(ms + EPS) * g


def _silu(x):
    return x / (1.0 + jnp.exp(-x))


def _ada_kernel(cond_ref, w_ref, b_ref, o_ref):
    s = _silu(cond_ref[...]).astype(BF16)
    o_ref[0] = jnp.dot(s, w_ref[0].astype(BF16), preferred_element_type=F32) + b_ref[0]


def _ada_mod(cond8, ada_w, ada_b):
    tn = 1024
    n = N_MOD * D_MODEL
    out = pl.pallas_call(
        _ada_kernel,
        grid=(DEPTH, n // tn),
        in_specs=[
            _const_spec((MOD_ROWS, D_MODEL)),
            pl.BlockSpec((1, D_MODEL, tn), lambda l, j: (l, 0, j)),
            pl.BlockSpec((1, 1, tn), lambda l, j: (l, 0, j)),
        ],
        out_specs=pl.BlockSpec((1, MOD_ROWS, tn), lambda l, j: (l, 0, j)),
        out_shape=jax.ShapeDtypeStruct((DEPTH, MOD_ROWS, n), F32),
        compiler_params=_cparams(("arbitrary", "arbitrary")),
        name="ada_mod",
    )(cond8, ada_w, ada_b.reshape(DEPTH, 1, n))
    return out.reshape(DEPTH * MOD_ROWS * N_MOD, 1, D_MODEL)


def _mod_spec(layer, k, row0, tokens_per_row, tm=TM):
    def imap(i):
        row = row0 + (i * tm) // tokens_per_row
        return ((layer * MOD_ROWS + row) * N_MOD + k, 0, 0)
    return pl.BlockSpec((1, 1, D_MODEL), imap)


def _qkv_kernel(x_ref, g_ref, sh_ref, sc_ref, w_ref, q_ref, k_ref, v_ref, ks_ref, vs_ref):
    h = _norm_mod(x_ref[...], g_ref[...], sh_ref[0], sc_ref[0]).astype(BF16)
    y = jnp.dot(h, w_ref[...], preferred_element_type=F32)
    q_ref[...] = (y[:, :D_MODEL] * (NA_HEAD_DIM ** -0.5)).astype(q_ref.dtype)
    k = y[:, D_MODEL:2 * D_MODEL]
    v = y[:, 2 * D_MODEL:]
    k_ref[...] = k.astype(k_ref.dtype)
    v_ref[...] = v.astype(v_ref.dtype)
    ks_ref[...] = pltpu.einshape("t(hd)->thd", k, h=NA_HEADS).reshape(ks_ref.shape)
    vs_ref[...] = pltpu.einshape("t(hd)->thd", v, h=NA_HEADS).reshape(vs_ref.shape)


def _na_qkv(x, g, mod, layer, row0, tokens_per_row, w_qkv, n_batch):
    t = x.shape[0]
    seq = t // n_batch
    nb = TM // seq
    tok = pl.BlockSpec((TM, D_MODEL), lambda i: (i, 0))
    state = pl.BlockSpec((nb, 1, seq, NA_HEADS, NA_HEAD_DIM), lambda i: (i, 0, 0, 0, 0))
    state_shape = jax.ShapeDtypeStruct((n_batch, 1, seq, NA_HEADS, NA_HEAD_DIM), F32)
    return pl.pallas_call(
        _qkv_kernel,
        grid=(t // TM,),
        in_specs=[tok, _const_spec((1, D_MODEL)),
                  _mod_spec(layer, 0, row0, tokens_per_row), _mod_spec(layer, 1, row0, tokens_per_row),
                  _const_spec((D_MODEL, 3 * D_MODEL))],
        out_specs=[tok, tok, tok, state, state],
        out_shape=[jax.ShapeDtypeStruct((t, D_MODEL), BF16)] * 3 + [state_shape, state_shape],
        compiler_params=_cparams(("arbitrary",)),
        name="na_qkv",
    )(x, g, mod, mod, w_qkv)


def _store_head_slots(vt, vt_ref):
    n = vt.shape[1]
    n_heads = vt_ref.shape[0] // LANES
    dim = vt.shape[0] // n_heads
    tail = jnp.where(lax.broadcasted_iota(jnp.int32, (LANES - dim, n), 0) == 0, 1.0, 0.0).astype(vt_ref.dtype)
    for hh in range(n_heads):
        vt_ref[hh * LANES:hh * LANES + dim, :] = vt[hh * dim:(hh + 1) * dim, :].astype(vt_ref.dtype)
        vt_ref[hh * LANES + dim:(hh + 1) * LANES, :] = tail


def _qkv_t_kernel(x_ref, g_ref, sh_ref, sc_ref, wqk_ref, wvt_ref, q_ref, k_ref, vt_ref):
    h = _norm_mod(x_ref[...], g_ref[...], sh_ref[0], sc_ref[0]).astype(BF16)
    y = jnp.dot(h, wqk_ref[...], preferred_element_type=F32)
    q_ref[...] = (y[:, :D_MODEL] * (NA_HEAD_DIM ** -0.5 * LOG2_E)).astype(q_ref.dtype)
    k_ref[...] = y[:, D_MODEL:].astype(k_ref.dtype)
    _store_head_slots(_qk(wvt_ref[...], h), vt_ref)


def _na_qkv_t(x, g, mod, layer, row0, tokens_per_row, w_qkv, w_vt):
    t = x.shape[0]
    tok = pl.BlockSpec((TM, D_MODEL), lambda i: (i, 0))
    slots = NA_HEADS * LANES
    return pl.pallas_call(
        _qkv_t_kernel,
        grid=(t // TM,),
        in_specs=[tok, _const_spec((1, D_MODEL)),
                  _mod_spec(layer, 0, row0, tokens_per_row), _mod_spec(layer, 1, row0, tokens_per_row),
                  pl.BlockSpec((D_MODEL, 2 * D_MODEL), lambda i: (0, 0)), _const_spec(w_vt.shape)],
        out_specs=[tok, tok, pl.BlockSpec((slots, TM), lambda i: (0, i))],
        out_shape=[jax.ShapeDtypeStruct((t, D_MODEL), BF16), jax.ShapeDtypeStruct((t, D_MODEL), BF16),
                   jax.ShapeDtypeStruct((slots, t), BF16)],
        compiler_params=_cparams(("arbitrary",)),
        name="na_qkv_t",
    )(x, g, mod, mod, w_qkv, w_vt)


def _softmax_pv(s_list, v_list):
    m = s_list[0].max(axis=-1, keepdims=True)
    for s in s_list[1:]:
        m = jnp.maximum(m, s.max(axis=-1, keepdims=True))
    l = None
    acc = None
    for s, v in zip(s_list, v_list):
        p = jnp.exp(s - m)
        ls = p.sum(axis=-1, keepdims=True)
        a = jnp.dot(p.astype(BF16), v, preferred_element_type=F32)
        l = ls if l is None else l + ls
        acc = a if acc is None else acc + a
    return acc * (1.0 / l)


def _qk(q, k):
    return lax.dot_general(q, k, (((1,), (1,)), ((), ())), preferred_element_type=F32)


def _head_masks():
    lane = lax.broadcasted_iota(jnp.int32, (1, LANES), 1)
    lo = lane < (LANES // 2)
    return lo, jnp.where(lo, 1.0, 0.0).astype(BF16), jnp.where(lo, 0.0, 1.0).astype(BF16)


def _attn_kernel(*refs, n_src, wide_qk, pairs, nb):
    q_ref, o_ref = refs[0], refs[-1]
    kv = refs[1:-1]
    lo, m0, m1 = _head_masks()
    wq = 2 * LANES if wide_qk else LANES
    tq = q_ref.shape[0] // nb
    for bb in range(nb):
        rq = slice(bb * tq, (bb + 1) * tq)
        rk = [slice(bb * (kv[2 * t].shape[0] // nb), (bb + 1) * (kv[2 * t].shape[0] // nb)) for t in range(n_src)]
        for p in range(pairs):
            q = q_ref[rq, p * wq:(p + 1) * wq]
            ks = [kv[2 * t][rk[t], p * wq:(p + 1) * wq].astype(BF16) for t in range(n_src)]
            vs = [kv[2 * t + 1][rk[t], p * LANES:(p + 1) * LANES].astype(BF16) for t in range(n_src)]
            outs = []
            for j in range(2):
                if wide_qk:
                    s_list = [_qk(q[:, j * LANES:(j + 1) * LANES], k[:, j * LANES:(j + 1) * LANES]) for k in ks]
                else:
                    s_list = [_qk(q * (m0 if j == 0 else m1), k) for k in ks]
                outs.append(_softmax_pv(s_list, vs))
            o_ref[rq, p * LANES:(p + 1) * LANES] = jnp.where(lo, outs[0], outs[1]).astype(o_ref.dtype)


def _attention(q, kv_srcs, n_batch, wide_qk, pairs, nb):
    wq = (2 * LANES if wide_qk else LANES) * pairs
    wv = LANES * pairs
    tq = q.shape[0] // n_batch * nb
    n_steps = D_MODEL // wv
    in_specs = [pl.BlockSpec((tq, wq), lambda b, p: (b, p))]
    args = [q]
    for k, v in kv_srcs:
        t_k = k.shape[0] // n_batch * nb
        in_specs.append(pl.BlockSpec((t_k, wq), lambda b, p: (b, p)))
        in_specs.append(pl.BlockSpec((t_k, wv), lambda b, p: (b, p)))
        args += [k, v]
    return pl.pallas_call(
        functools.partial(_attn_kernel, n_src=len(kv_srcs), wide_qk=wide_qk, pairs=pairs, nb=nb),
        grid=(n_batch // nb, n_steps),
        in_specs=in_specs,
        out_specs=pl.BlockSpec((tq, wv), lambda b, p: (b, p)),
        out_shape=jax.ShapeDtypeStruct((q.shape[0], D_MODEL), BF16),
        compiler_params=_cparams(("arbitrary", "arbitrary")),
        name="attention",
    )(*args)


def _col_max(s, chunk=64):
    acc = s[:chunk]
    for r in range(chunk, s.shape[0], chunk):
        acc = jnp.maximum(acc, s[r:r + chunk])
    return acc.max(axis=0, keepdims=True)


def _attn_t_kernel(q_ref, k1_ref, k2_ref, v1_ref, v2_ref, o_ref):
    n_heads = q_ref.shape[0] // LANES
    slots = [slice(j * LANES, (j + 1) * LANES) for j in range(n_heads)]
    def scores(sl):
        qt = q_ref[sl, :]
        return (jnp.dot(k1_ref[:, sl], qt, preferred_element_type=F32),
                jnp.dot(k2_ref[:, sl], qt, preferred_element_type=F32))

    nxt = scores(slots[0])
    outs = []
    for n, sl in enumerate(slots):
        s1, s2 = nxt
        if n + 1 < n_heads:
            nxt = scores(slots[n + 1])
        m = jnp.maximum(_col_max(s1), _col_max(s2))
        p1 = jnp.exp2(s1 - m).astype(BF16)
        p2 = jnp.exp2(s2 - m).astype(BF16)
        ot = (jnp.dot(v1_ref[sl, :], p1, preferred_element_type=F32)
              + jnp.dot(v2_ref[sl, :], p2, preferred_element_type=F32))
        outs.append(ot[:MLA_V_DIM] * (1.0 / ot[MLA_V_DIM:MLA_V_DIM + 1]))
    for j in range(0, n_heads, 2):
        pair = jnp.concatenate(outs[j:j + 2], axis=0).T
        o_ref[:, (j // 2) * LANES:(j // 2 + 1) * LANES] = pair.astype(o_ref.dtype)


def _attention_t(qt, k1, v1t, k2, v2t, n_batch, tq, heads):
    wq = heads * LANES
    wo = heads * MLA_V_DIM
    nq = qt.shape[1] // n_batch // tq
    t1 = k1.shape[0] // n_batch
    t2 = k2.shape[0] // n_batch
    return pl.pallas_call(
        _attn_t_kernel,
        grid=(n_batch, MLA_HEADS // heads, nq),
        in_specs=[pl.BlockSpec((wq, tq), lambda b, p, i: (p, b * nq + i)),
                  pl.BlockSpec((t1, wq), lambda b, p, i: (b, p)),
                  pl.BlockSpec((t2, wq), lambda b, p, i: (b, p)),
                  pl.BlockSpec((wq, t1), lambda b, p, i: (p, b)),
                  pl.BlockSpec((wq, t2), lambda b, p, i: (p, b))],
        out_specs=pl.BlockSpec((tq, wo), lambda b, p, i: (b * nq + i, p)),
        out_shape=jax.ShapeDtypeStruct((qt.shape[1], D_MODEL), BF16),
        compiler_params=_cparams(("arbitrary", "arbitrary", "arbitrary")),
        name="attention_t",
    )(qt, k1, k2, v1t, v2t)


def _na_group_layout(rows):
    n_groups = rows // NA_GROUP_ROWS
    kh = min(NA_KH, rows)
    bases, types, keys = [], [], {}
    for gi in range(n_groups):
        r0 = gi * NA_GROUP_ROWS
        base = int(np.clip(r0 - kh // 2, 0, rows - NA_WIN_ROWS))
        starts = tuple(int(np.clip(r - kh // 2, 0, rows - kh)) - base for r in range(r0, r0 + NA_GROUP_ROWS))
        key = (starts, r0 - base)
        if key not in keys:
            keys[key] = len(keys)
        bases.append(base)
        types.append(keys[key])
    return bases, types, list(keys.keys())


def _na_bias_tables(rpb, rows):
    kh = min(NA_KH, rows)
    _, _, type_keys = _na_group_layout(rows)
    qc = np.arange(GRID_W)
    kc = np.arange(GRID_W)
    col_start = np.clip(qc - NA_KW // 2, 0, GRID_W - NA_KW)
    col_valid = (kc[None, :] >= col_start[:, None]) & (kc[None, :] < col_start[:, None] + NA_KW)
    dx = kc[None, :] - qc[:, None] + (NA_KW - 1)
    onehot = ((dx[None] == np.arange(2 * NA_KW - 1)[:, None, None]) & col_valid[None]).astype(np.float32)
    band = jnp.einsum('hyd,dqk->hykq', rpb.astype(F32), onehot, precision=lax.Precision.HIGHEST)
    band = jnp.where(col_valid.T, band, NEG_INF)
    strip = band.reshape(NA_HEADS, (2 * NA_KH - 1) * GRID_W, GRID_W)
    tables = []
    for starts, r_off in type_keys:
        q_cols = []
        for rq in range(NA_GROUP_ROWS):
            dy0 = starts[rq] - (rq + r_off) + (NA_KH - 1)
            piece = strip[:, dy0 * GRID_W:(dy0 + kh) * GRID_W, :]
            pad = ((0, 0), (starts[rq] * GRID_W, (NA_WIN_ROWS - kh - starts[rq]) * GRID_W), (0, 0))
            q_cols.append(jnp.pad(piece, pad, constant_values=NEG_INF))
        tables.append(jnp.concatenate(q_cols, axis=2))
    return jnp.stack(tables, axis=1) * LOG2_E


def _state_k_kernel(x_ref, o_ref):
    for h in range(NA_HEADS):
        o_ref[:, h * NA_HEAD_DIM:(h + 1) * NA_HEAD_DIM] = x_ref[0, 0, :, h, :].astype(o_ref.dtype)


def _state_vt_kernel(x_ref, o_ref, flat_ref):
    for h in range(NA_HEADS):
        flat_ref[:, h * NA_HEAD_DIM:(h + 1) * NA_HEAD_DIM] = x_ref[0, 0, :, h, :]
    _store_head_slots(flat_ref[...].T, o_ref)


def _state_k(x, layer):
    b, _, t, h, dh = x.shape
    return pl.pallas_call(
        _state_k_kernel,
        grid=(b,),
        in_specs=[pl.BlockSpec((1, 1, t, h, dh), lambda i: (i, layer, 0, 0, 0))],
        out_specs=pl.BlockSpec((t, h * dh), lambda i: (i, 0)),
        out_shape=jax.ShapeDtypeStruct((b * t, h * dh), BF16),
        compiler_params=_cparams(("arbitrary",)),
        name="state_k",
    )(x)


def _state_vt(x, layer):
    b, _, t, h, dh = x.shape
    return pl.pallas_call(
        _state_vt_kernel,
        grid=(b,),
        in_specs=[pl.BlockSpec((1, 1, t, h, dh), lambda i: (i, layer, 0, 0, 0))],
        out_specs=pl.BlockSpec((h * LANES, t), lambda i: (0, i)),
        out_shape=jax.ShapeDtypeStruct((h * LANES, b * t), BF16),
        scratch_shapes=[pltpu.VMEM((t, h * dh), F32)],
        compiler_params=_cparams(("arbitrary",)),
        name="state_vt",
    )(x)


def _na_latent_kernel(q_ref, k_ref, vt_ref, kc_ref, vct_ref, tab_ref, o_ref, *, bases, types):
    _, m0, m1 = _head_masks()
    kctx = kc_ref[...]
    gq = NA_GROUP_ROWS * GRID_W
    gk = NA_WIN_ROWS * GRID_W
    blocks = [(gi, base, ty, j) for gi, (base, ty) in enumerate(zip(bases, types)) for j in range(2)]

    def scores(gi, base, ty, j):
        qj = q_ref[gi * gq:(gi + 1) * gq, :] * (m0 if j == 0 else m1)
        kwin = k_ref[base * GRID_W:base * GRID_W + gk, :]
        return _qk(kwin, qj) + tab_ref[j, ty], _qk(kctx, qj)

    nxt = scores(*blocks[0])
    outs = []
    for n, (gi, base, ty, j) in enumerate(blocks):
        s_win, s_ctx = nxt
        if n + 1 < len(blocks):
            nxt = scores(*blocks[n + 1])
        m = jnp.maximum(_col_max(s_win), _col_max(s_ctx))
        p_win = jnp.exp2(s_win - m).astype(BF16)
        p_ctx = jnp.exp2(s_ctx - m).astype(BF16)
        sl = slice(j * LANES, (j + 1) * LANES)
        ot = (jnp.dot(vt_ref[sl, base * GRID_W:base * GRID_W + gk], p_win, preferred_element_type=F32)
              + jnp.dot(vct_ref[sl, :], p_ctx, preferred_element_type=F32))
        outs.append(ot[:NA_HEAD_DIM] * (1.0 / ot[NA_HEAD_DIM:NA_HEAD_DIM + 1]))
        if j == 1:
            o_ref[gi * gq:(gi + 1) * gq, :] = jnp.concatenate(outs, axis=0).T.astype(o_ref.dtype)
            outs = []


def _na_latent(q, k, vt, k_ctx, vt_ctx, tables, n_batch):
    t = q.shape[0] // n_batch
    rows = t // GRID_W
    t_ctx = k_ctx.shape[0] // n_batch
    bases, types, type_keys = _na_group_layout(rows)
    n_types = len(type_keys)
    gq = NA_GROUP_ROWS * GRID_W
    gk = NA_WIN_ROWS * GRID_W
    tok = pl.BlockSpec((t, LANES), lambda p, b: (b, p))
    ctx = pl.BlockSpec((t_ctx, LANES), lambda p, b: (b, p))
    return pl.pallas_call(
        functools.partial(_na_latent_kernel, bases=bases, types=types),
        grid=(D_MODEL // LANES, n_batch),
        in_specs=[tok, tok, pl.BlockSpec((2 * LANES, t), lambda p, b: (p, b)),
                  ctx, pl.BlockSpec((2 * LANES, t_ctx), lambda p, b: (p, b)),
                  pl.BlockSpec((2, n_types, gk, gq), lambda p, b: (p, 0, 0, 0))],
        out_specs=tok,
        out_shape=jax.ShapeDtypeStruct(q.shape, BF16),
        compiler_params=_cparams(("arbitrary", "arbitrary")),
        name="na_latent",
    )(q, k, vt, k_ctx, vt_ctx, tables)


def _rope_lane_tables(n_tokens):
    f32 = np.float32
    n_freq = MLA_ROPE_DIM // 4
    inv = f32(1.0) / (f32(ROPE_THETA) ** (np.arange(n_freq, dtype=f32) / f32(n_freq)))
    tt = np.arange(n_tokens)
    row = (tt // GRID_W).astype(f32)
    col = (tt % GRID_W).astype(f32)
    ang = np.concatenate([row[:, None] * inv, col[:, None] * inv], axis=-1).astype(f32)
    cos, sin = np.cos(ang).astype(f32), np.sin(ang).astype(f32)
    ones = np.ones((n_tokens, MLA_NOPE_DIM), f32)
    z16 = np.zeros((n_tokens, MLA_ROPE_DIM // 2), f32)
    z32 = np.zeros((n_tokens, LANES - MLA_QK_DIM), f32)
    z64 = np.zeros((n_tokens, MLA_NOPE_DIM), f32)
    c_tab = np.concatenate([ones, cos, cos, z32], axis=-1)
    s_dn = np.concatenate([z64, -sin, z16, z32], axis=-1)
    s_up = np.concatenate([z64, z16, sin, z32], axis=-1)
    return (jnp.asarray(c_tab), jnp.asarray(s_dn), jnp.asarray(s_up),
            jnp.asarray(np.ascontiguousarray(cos.T)), jnp.asarray(np.ascontiguousarray(sin.T)))


def _rope_slot(x, c_tab, s_dn, s_up):
    half = MLA_ROPE_DIM // 2
    return x * c_tab + pltpu.roll(x, LANES - half, 1) * s_dn + pltpu.roll(x, half, 1) * s_up


def _mla_kv(ckv_n, kr_slot, wuk_ref, wuv_ref, k_ref, v_ref, v_transposed, rows=slice(None)):
    c = ckv_n.astype(BF16)
    k_nope = jnp.dot(c, wuk_ref[...], preferred_element_type=F32)
    for h in range(MLA_HEADS):
        k_ref[rows, h * LANES:(h + 1) * LANES] = (k_nope[:, h * LANES:(h + 1) * LANES] + kr_slot).astype(k_ref.dtype)
    if v_transposed:
        _store_head_slots(_qk(wuv_ref[...], c), v_ref.at[:, rows])
    else:
        v_ref[rows, :] = jnp.dot(c, wuv_ref[...], preferred_element_type=F32).astype(v_ref.dtype)


def _mla_proj_kernel(*refs, rope, emit_state, q_scale):
    (x_ref, g_ref, sh_ref, sc_ref, win_ref, qg_ref, kvg_ref, wuq_ref, wuk_ref, wuv_ref) = refs[:10]
    pos = 10
    if rope:
        ct_ref, sd_ref, su_ref, cos_ref, sin_ref = refs[pos:pos + 5]
        pos += 5
    q_ref, k_ref, v_ref = refs[pos:pos + 3]
    pos += 3

    def front(rows):
        h = _norm_mod(x_ref[rows, :], g_ref[...], sh_ref[0], sc_ref[0]).astype(BF16)
        c = jnp.dot(h, win_ref[...], preferred_element_type=F32)
        cq = _rms(c[:, :MLA_Q_RANK], qg_ref[...]).astype(BF16)
        if rope:
            q = _qk(wuq_ref[...], cq) * q_scale
        else:
            q = jnp.dot(cq, wuq_ref[...], preferred_element_type=F32) * q_scale
        ckv = _rms(c[:, MLA_Q_RANK:MLA_Q_RANK + MLA_KV_RANK], kvg_ref[...])
        kr_chunk = c[:, MLA_Q_RANK + MLA_KV_RANK:]
        if emit_state:
            ckv_out, kr_out = refs[pos:pos + 2]
            ckv_out[rows, :] = ckv
            kr_out[rows, :] = kr_chunk[:, :MLA_ROPE_DIM]
        return q, ckv, pltpu.roll(kr_chunk, MLA_NOPE_DIM, 1)

    def back(rows, q, ckv, kr_slot):
        if rope:
            ct, sd, su = ct_ref[rows, :], sd_ref[rows, :], su_ref[rows, :]
            kr_slot = _rope_slot(kr_slot, ct, sd, su)
        _mla_kv(ckv, kr_slot, wuk_ref, wuv_ref, k_ref, v_ref, v_transposed=rope, rows=rows)
        if not rope:
            q_ref[rows, :] = q.astype(q_ref.dtype)
            return
        cs, sn = cos_ref[:, rows], sin_ref[:, rows]
        half = MLA_ROPE_DIM // 2
        for hh in range(MLA_HEADS):
            r0 = hh * LANES
            r1 = r0 + MLA_NOPE_DIM
            x1, x2 = q[r1:r1 + half], q[r1 + half:r1 + 2 * half]
            q_ref[r0:r1, rows] = q[r0:r1].astype(q_ref.dtype)
            q_ref[r1:r1 + half, rows] = (x1 * cs - x2 * sn).astype(q_ref.dtype)
            q_ref[r1 + half:r1 + 2 * half, rows] = (x2 * cs + x1 * sn).astype(q_ref.dtype)
            q_ref[r1 + 2 * half:r0 + LANES, rows] = q[r1 + 2 * half:r0 + LANES].astype(q_ref.dtype)

    tiles = [slice(r0, r0 + TM_SUB_MLA) for r0 in range(0, x_ref.shape[0], TM_SUB_MLA)]
    nxt = front(tiles[0])
    for n, rows in enumerate(tiles):
        cur = nxt
        if n + 1 < len(tiles):
            nxt = front(tiles[n + 1])
        back(rows, *cur)


def _mla_proj(x, g, mod, layer, row0, tokens_per_row, w, rope_tabs, emit_state):
    t = x.shape[0]
    win, qg, kvg, wuq, wuk, wuv = w
    tm = TM_MLA
    tok = lambda n: pl.BlockSpec((tm, n), lambda i: (i, 0))
    wq = MLA_HEADS * LANES
    in_specs = [tok(D_MODEL), _const_spec((1, D_MODEL)),
                _mod_spec(layer, 0, row0, tokens_per_row, tm), _mod_spec(layer, 1, row0, tokens_per_row, tm),
                _const_spec(win.shape), _const_spec(qg.shape), _const_spec(kvg.shape),
                _const_spec(wuq.shape), _const_spec(wuk.shape), _const_spec(wuv.shape)]
    args = [x, g, mod, mod, win, qg, kvg, wuq, wuk, wuv]
    rope = rope_tabs is not None
    if rope:
        n_tab = rope_tabs[0].shape[0] // tm
        in_specs += [pl.BlockSpec((tm, LANES), lambda i: (i % n_tab, 0)) for _ in range(3)]
        in_specs += [pl.BlockSpec((MLA_ROPE_DIM // 2, tm), lambda i: (0, i % n_tab)) for _ in range(2)]
        args += list(rope_tabs)
    if rope:
        q_spec, q_shape = pl.BlockSpec((wq, tm), lambda i: (0, i)), jax.ShapeDtypeStruct((wq, t), BF16)
        v_spec, v_shape = q_spec, q_shape
        q_scale = MLA_QK_DIM ** -0.5 * LOG2_E
    else:
        q_spec, q_shape = tok(wq), jax.ShapeDtypeStruct((t, wq), BF16)
        v_spec, v_shape = tok(D_MODEL), jax.ShapeDtypeStruct((t, D_MODEL), BF16)
        q_scale = MLA_QK_DIM ** -0.5
    out_specs = [q_spec, tok(wq), v_spec]
    out_shape = [q_shape, jax.ShapeDtypeStruct((t, wq), BF16), v_shape]
    if emit_state:
        out_specs += [tok(MLA_KV_RANK), tok(MLA_ROPE_DIM)]
        out_shape += [jax.ShapeDtypeStruct((t, MLA_KV_RANK), F32), jax.ShapeDtypeStruct((t, MLA_ROPE_DIM), F32)]
    return pl.pallas_call(
        functools.partial(_mla_proj_kernel, rope=rope, emit_state=emit_state, q_scale=q_scale),
        grid=(t // tm,),
        in_specs=in_specs, out_specs=out_specs, out_shape=out_shape,
        compiler_params=_cparams(("arbitrary",)),
        name="mla_proj",
    )(*args)


def _mla_state_kv_kernel(ckv_ref, kr_ref, wuk_ref, wuv_ref, k_ref, v_ref):
    kr_slot = pltpu.roll(kr_ref[...], MLA_NOPE_DIM, 1)
    _mla_kv(ckv_ref[...], kr_slot, wuk_ref, wuv_ref, k_ref, v_ref, v_transposed=True)


def _mla_state_kv(ckv, kr_pad, wuk, wuv_t):
    t = ckv.shape[0]
    wq = MLA_HEADS * LANES
    tok = lambda n: pl.BlockSpec((TM, n), lambda i: (i, 0))
    return pl.pallas_call(
        _mla_state_kv_kernel,
        grid=(t // TM,),
        in_specs=[tok(MLA_KV_RANK), tok(LANES), _const_spec(wuk.shape), _const_spec(wuv_t.shape)],
        out_specs=[tok(wq), pl.BlockSpec((wq, TM), lambda i: (0, i))],
        out_shape=[jax.ShapeDtypeStruct((t, wq), BF16), jax.ShapeDtypeStruct((wq, t), BF16)],
        compiler_params=_cparams(("arbitrary",)),
        name="mla_state_kv",
    )(ckv, kr_pad, wuk, wuv_t)


def _first_argmax_rows(v, row):
    top = v.max(axis=0, keepdims=True)
    idx = jnp.where(v == top, row, float(v.shape[0])).min(axis=0, keepdims=True)
    return top, idx


ROUTER_ROWS = 32
ROUTER_EXPERT_ROW = 8


def _route_t(lt):
    n = lt.shape[1]
    row8 = lax.broadcasted_iota(jnp.int32, (8, n), 0).astype(F32)
    row16 = lax.broadcasted_iota(jnp.int32, (N_EXPERTS, n), 0).astype(F32)
    is_g = row8 < N_GROUPS
    gl = jnp.where(is_g, lt[:8], -jnp.inf)
    gmax, gsel = _first_argmax_rows(gl, row8)
    p_group = 1.0 / jnp.where(is_g, jnp.exp(gl - gmax), 0.0).sum(axis=0, keepdims=True)
    first = gsel * EXPERTS_PER_GROUP
    e_all = lt[ROUTER_EXPERT_ROW:ROUTER_EXPERT_ROW + N_EXPERTS]
    el = jnp.where((row16 >= first) & (row16 < first + EXPERTS_PER_GROUP), e_all, -jnp.inf)
    t1, i1 = _first_argmax_rows(el, row16)
    t2, i2 = _first_argmax_rows(jnp.where(row16 == i1, -jnp.inf, el), row16)
    e2 = jnp.exp(t2 - t1)
    w1 = 1.0 / (1.0 + e2)
    gates = jnp.where(row16 == i1, w1 * p_group, jnp.where(row16 == i2, e2 * w1 * p_group, 0.0))
    rel = gates[:8] + gates[8:]
    rel = rel + pltpu.roll(rel, EXPERTS_PER_GROUP, 0)
    meta_t = jnp.where(row8 < EXPERTS_PER_GROUP, rel, jnp.where(row8 == META_GROUP_LANE, gsel, 0.0))
    onehot_t = jnp.where(row8 == gsel, 1.0, 0.0)
    return meta_t, onehot_t, row8, gsel


def _pack_halves(x):
    half = x.shape[1] // 2
    lo = lax.bitcast_convert_type(x[:, :half].astype(BF16).astype(F32), jnp.uint32)
    hi = lax.bitcast_convert_type(x[:, half:].astype(BF16).astype(F32), jnp.uint32)
    return (lo >> 16) | hi


def _unpack_halves(p):
    lo = lax.bitcast_convert_type(p << 16, F32)
    hi = lax.bitcast_convert_type(p & jnp.uint32(0xFFFF0000), F32)
    return lo, hi


ROW_WORDS = D_MODEL // 2 + LANES
META_GROUP_LANE = EXPERTS_PER_GROUP
META_RANK_LANE = EXPERTS_PER_GROUP + 1


def _proj_res_kernel(o_ref, x_ref, w_ref, gate_ref, g_ref, sh_ref, sc_ref, wrt_ref, brt_ref, tri_ref,
                     xn_ref, hpk_ref, metat_ref, cnt_ref):
    @pl.when(pl.program_id(0) == 0)
    def _():
        cnt_ref[...] = jnp.zeros(cnt_ref.shape, cnt_ref.dtype)

    sub = tri_ref.shape[0]
    cnt = cnt_ref[:, :1]
    tiles = [slice(r0, r0 + sub) for r0 in range(0, x_ref.shape[0], sub)]

    def out_proj(rows):
        return jnp.dot(o_ref[rows, :], w_ref[...], preferred_element_type=F32)

    y_next = out_proj(tiles[0])
    for n, rows in enumerate(tiles):
        y = y_next
        if n + 1 < len(tiles):
            y_next = out_proj(tiles[n + 1])
        xn = x_ref[rows, :] + gate_ref[0] * y
        xn_ref[rows, :] = xn
        h = _norm_mod(xn, g_ref[...], sh_ref[0], sc_ref[0])
        lt = _qk(wrt_ref[...], h.astype(BF16)) + brt_ref[...]
        meta_t, onehot_t, row8, gsel = _route_t(lt)
        prefix = jnp.dot(onehot_t.astype(BF16), tri_ref[...], preferred_element_type=F32)
        rank = jnp.where(row8 == gsel, prefix + cnt - 1.0, 0.0).sum(axis=0, keepdims=True)
        cnt = cnt + onehot_t.sum(axis=1, keepdims=True)
        meta_t = jnp.where(row8 == META_RANK_LANE, rank, meta_t)
        meta = jnp.concatenate([meta_t, jnp.zeros((LANES - 8, sub), F32)], axis=0).T
        metat_ref[:, rows] = meta_t
        hpk_ref[rows, :] = jnp.concatenate([_pack_halves(h), lax.bitcast_convert_type(meta, jnp.uint32)], axis=1)
    cnt_ref[...] = jnp.broadcast_to(cnt, cnt_ref.shape)


def _proj_res(o, x, w_o, g_ffn, mod, layer, row0, tokens_per_row, w_router_t, b_router_t):
    t = x.shape[0]
    tm = TM_PROJ
    tok = lambda n: pl.BlockSpec((tm, n), lambda i: (i, 0))
    ms = lambda k: _mod_spec(layer, k, row0, tokens_per_row, tm)
    tri = jnp.asarray(np.triu(np.ones((TM_SUB, TM_SUB), np.float32)), BF16)
    return pl.pallas_call(
        _proj_res_kernel,
        grid=(t // tm,),
        in_specs=[tok(D_MODEL), tok(D_MODEL), _const_spec(w_o.shape), ms(2), _const_spec((1, D_MODEL)), ms(3), ms(4),
                  _const_spec(w_router_t.shape), _const_spec(b_router_t.shape), _const_spec((TM_SUB, TM_SUB))],
        out_specs=[tok(D_MODEL), tok(ROW_WORDS), pl.BlockSpec((8, tm), lambda i: (0, i)), _const_spec((8, LANES))],
        out_shape=[jax.ShapeDtypeStruct((t, D_MODEL), F32), jax.ShapeDtypeStruct((t, ROW_WORDS), jnp.uint32),
                   jax.ShapeDtypeStruct((8, t), F32), jax.ShapeDtypeStruct((8, LANES), F32)],
        compiler_params=_cparams(("arbitrary",)),
        name="proj_res",
    )(o, x, w_o, mod, g_ffn, mod, mod, w_router_t, b_router_t, tri)


def _moe_plan(meta_t, cnt):
    t = meta_t.shape[1]
    n_tiles = t // TM + N_GROUPS
    group = meta_t[META_GROUP_LANE].astype(jnp.int32)
    rank = meta_t[META_RANK_LANE].astype(jnp.int32)
    counts = cnt[:N_GROUPS, 0].astype(jnp.int32)
    tiles = (counts + TM - 1) // TM
    tile_end = jnp.cumsum(tiles)
    tile_start = tile_end - tiles
    pos = rank + TM * jnp.sum(jnp.where(group[:, None] == jnp.arange(N_GROUPS)[None], tile_start[None], 0), axis=1)
    tile_group = jnp.minimum(jnp.sum(jnp.arange(n_tiles)[:, None] >= tile_end[None], axis=1), N_GROUPS - 1)
    pad_bounds = jnp.stack([tile_start * TM + counts, tile_end * TM], axis=1).reshape(-1)
    return (pos.astype(jnp.int32), tile_group.astype(jnp.int32), tile_end[-1:].astype(jnp.int32),
            pad_bounds.astype(jnp.int32))


def _gather_rows(srcs, idx_ref, base, dsts, n_rows):
    for r in range(n_rows):
        s = idx_ref[base + r]
        for src_ref, dst_ref in zip(srcs, dsts):
            dst_ref[r:r + 1, :] = src_ref[pl.ds(s, 1), :]


def _moe_kernel(pos_ref, tg_ref, nt_ref, pad_ref, hpk_ref, wg_ref, wu_ref, wd_ref, ypk_ref, src_ref, hbuf, hid_ref):
    i = pl.program_id(0)
    n_tok = hpk_ref.shape[0]

    @pl.when(i == 0)
    def _():
        def clear(p, c):
            src_ref[p] = 0
            return c
        for g in range(N_GROUPS):
            lax.fori_loop(pad_ref[2 * g], pad_ref[2 * g + 1], clear, 0)

        def fill(t, c):
            src_ref[pos_ref[t]] = t
            return c
        lax.fori_loop(0, n_tok, fill, 0, unroll=8)
        _gather_rows((hpk_ref,), src_ref, 0, (hbuf.at[0],), TM)

    @pl.when(i < nt_ref[0])
    def _():
        slot = i % 2
        row = hbuf[slot]
        lo, hi = _unpack_halves(row[:, :D_MODEL // 2])
        h = jnp.concatenate([lo.astype(BF16), hi.astype(BF16)], axis=1)
        gates = lax.bitcast_convert_type(row[:, D_MODEL // 2:], F32)
        nxt = jnp.minimum(i + 1, nt_ref[0] - 1) * TM
        _gather_rows((hpk_ref,), src_ref, nxt, (hbuf.at[1 - slot],), TM)
        f = D_FF_EXPERT
        for j in range(EXPERTS_PER_GROUP):
            gate_act = _silu(jnp.dot(h, wg_ref[0, j], preferred_element_type=F32))
            hid = gate_act * jnp.dot(h, wu_ref[0, j], preferred_element_type=F32) * gates[:, j:j + 1]
            hid_ref[:, j * f:(j + 1) * f] = hid.astype(BF16)
        ypk_ref[...] = _pack_halves(jnp.dot(hid_ref[...], wd_ref[0], preferred_element_type=F32))

    @pl.when(i >= nt_ref[0])
    def _():
        ypk_ref[...] = jnp.zeros(ypk_ref.shape, ypk_ref.dtype)


def _moe(hpk, pos, tile_group, n_used, pad_bounds, wg, wu, wd, layer):
    t = hpk.shape[0]
    n_tiles = t // TM + N_GROUPS
    f = D_FF_EXPERT
    g0 = layer * N_GROUPS
    grid_spec = pltpu.PrefetchScalarGridSpec(
        num_scalar_prefetch=4,
        grid=(n_tiles,),
        in_specs=[
            pl.BlockSpec(hpk.shape, lambda i, *_: (0, 0), pipeline_mode=pl.Buffered(1)),
            pl.BlockSpec((1, EXPERTS_PER_GROUP, D_MODEL, f), lambda i, pos, tg, nt, pad: (g0 + tg[i], 0, 0, 0)),
            pl.BlockSpec((1, EXPERTS_PER_GROUP, D_MODEL, f), lambda i, pos, tg, nt, pad: (g0 + tg[i], 0, 0, 0)),
            pl.BlockSpec((1, EXPERTS_PER_GROUP * f, D_MODEL), lambda i, pos, tg, nt, pad: (g0 + tg[i], 0, 0)),
        ],
        out_specs=pl.BlockSpec((TM, D_MODEL // 2), lambda i, *_: (i, 0)),
        scratch_shapes=[pltpu.SMEM((n_tiles * TM,), jnp.int32),
                        pltpu.VMEM((2, TM, ROW_WORDS), jnp.uint32),
                        pltpu.VMEM((TM, EXPERTS_PER_GROUP * f), BF16)],
    )
    return pl.pallas_call(
        _moe_kernel,
        grid_spec=grid_spec,
        out_shape=jax.ShapeDtypeStruct((n_tiles * TM, D_MODEL // 2), jnp.uint32),
        compiler_params=_cparams(("arbitrary",)),
        name="moe",
    )(pos, tile_group, n_used, pad_bounds, hpk, wg, wu, wd)


def _combine_kernel(*refs, final_norm):
    pos_ref, ypk_ref, x_ref, gate_ref = refs[:4]
    o_ref, ybuf = refs[-2], refs[-1]
    _gather_rows((ypk_ref,), pos_ref, pl.program_id(0) * TM, (ybuf,), TM)
    lo, hi = _unpack_halves(ybuf[...])
    out = x_ref[...] + gate_ref[0] * jnp.concatenate([lo, hi], axis=1)
    if final_norm:
        out = _rms(out, refs[4][...])
    o_ref[...] = out


def _combine(ypk, pos, x, mod, layer, row0, tokens_per_row, final_g):
    t = x.shape[0]

    def mod_map(i, pos):
        row = row0 + (i * TM) // tokens_per_row
        return ((layer * MOD_ROWS + row) * N_MOD + 5, 0, 0)

    in_specs = [pl.BlockSpec(ypk.shape, lambda i, pos: (0, 0), pipeline_mode=pl.Buffered(1)),
                pl.BlockSpec((TM, D_MODEL), lambda i, pos: (i, 0)),
                pl.BlockSpec((1, 1, D_MODEL), mod_map)]
    args = [ypk, x, mod]
    if final_g is not None:
        in_specs.append(pl.BlockSpec((1, D_MODEL), lambda i, pos: (0, 0)))
        args.append(final_g)
    grid_spec = pltpu.PrefetchScalarGridSpec(
        num_scalar_prefetch=1,
        grid=(t // TM,),
        in_specs=in_specs,
        out_specs=pl.BlockSpec((TM, D_MODEL), lambda i, pos: (i, 0)),
        scratch_shapes=[pltpu.VMEM((TM, D_MODEL // 2), jnp.uint32)],
    )
    return pl.pallas_call(
        functools.partial(_combine_kernel, final_norm=final_g is not None),
        grid_spec=grid_spec,
        out_shape=jax.ShapeDtypeStruct((t, D_MODEL), F32),
        compiler_params=_cparams(("arbitrary",)),
        name="moe_combine",
    )(pos, *args)


def _pad_heads(w, used):
    k = w.shape[0]
    w = w.reshape(k, MLA_HEADS, used)
    return jnp.pad(w, ((0, 0), (0, 0), (0, LANES - used))).reshape(k, MLA_HEADS * LANES)


def kernel(x_prompt, x_sample, state_na_k, state_na_v, state_mla_ckv, state_mla_kr, c, c_ctx, ada_w, ada_b, norm_mix_g, norm_ffn_g, final_norm_g, na_w_qkv, na_w_o, na_rpb, mla_w_in, mla_q_norm_g, mla_kv_norm_g, mla_w_uq, mla_w_uk, mla_w_uv, mla_w_o, moe_w_group, moe_b_group, moe_w_expert, moe_b_expert, moe_w_gate, moe_w_up, moe_w_down):
    n_b, seq, d = x_prompt.shape
    dec_b, n_lat, _ = x_sample.shape
    past = state_na_k.shape[2]
    rows = n_lat // GRID_W
    assert d == D_MODEL and dec_b + 1 <= MOD_ROWS and n_lat % TM_PROJ == 0 and (n_b * seq) % TM_PROJ == 0

    cond8 = jnp.concatenate([c_ctx[None], c, jnp.zeros((MOD_ROWS - 1 - dec_b, d), F32)], axis=0)
    mod = _ada_mod(cond8, ada_w, ada_b)

    xp = x_prompt.reshape(n_b * seq, d)
    xs = x_sample.reshape(dec_b * n_lat, d)
    p_rows = (0, n_b * seq)
    s_rows = (1, n_lat)
    outs = {}

    f = D_FF_EXPERT
    wg = _to_bf16(moe_w_gate).reshape(DEPTH * N_GROUPS, EXPERTS_PER_GROUP, d, f)
    wu = _to_bf16(moe_w_up).reshape(DEPTH * N_GROUPS, EXPERTS_PER_GROUP, d, f)
    wd = _to_bf16(moe_w_down).reshape(DEPTH * N_GROUPS, EXPERTS_PER_GROUP * f, d)

    for i in range(DEPTH):
        j = i // 2
        g_mix = norm_mix_g[i][None]
        g_ffn = norm_ffn_g[i][None]
        if i % 2 == 0:
            w_qkv = _to_bf16(na_w_qkv[j])
            w_vt = na_w_qkv[j][:, 2 * d:].T.astype(BF16)
            qp, kp, vp, outs['na_k'], outs['na_v'] = _na_qkv(xp, g_mix, mod, i, *p_rows, w_qkv, n_b)
            qs, ks, vs_t = _na_qkv_t(xs, g_mix, mod, i, *s_rows, w_qkv, w_vt)
            op = _attention(qp, [(kp, vp)], n_b, wide_qk=False, pairs=8, nb=4)
            tables = _na_bias_tables(na_rpb[j], rows)
            os_ = _na_latent(qs, ks, vs_t, _state_k(state_na_k, j), _state_vt(state_na_v, j), tables, dec_b)
            w_o = _to_bf16(na_w_o[j])
        else:
            pad_in = MLA_IN_PAD - mla_w_in.shape[-1]
            w_mla = (jnp.pad(mla_w_in[j], ((0, 0), (0, pad_in))).astype(BF16),
                     mla_q_norm_g[j][None], mla_kv_norm_g[j][None],
                     _pad_heads(mla_w_uq[j], MLA_QK_DIM).astype(BF16),
                     _pad_heads(mla_w_uk[j], MLA_NOPE_DIM).astype(BF16),
                     mla_w_uv[j].astype(BF16))
            wuv_t = mla_w_uv[j].T.astype(BF16)
            w_mla_t = w_mla[:3] + (w_mla[3].T, w_mla[4], wuv_t)
            qp, kp, vp, ckv_p, kr_p = _mla_proj(xp, g_mix, mod, i, *p_rows, w_mla, None, True)
            qs, ks, vs_t = _mla_proj(xs, g_mix, mod, i, *s_rows, w_mla_t, _rope_lane_tables(n_lat), False)
            outs['ckv'] = ckv_p.reshape(n_b, 1, seq, MLA_KV_RANK)
            outs['kr'] = kr_p.reshape(n_b, 1, seq, MLA_ROPE_DIM)
            kr_state = jnp.pad(state_mla_kr[:, j].reshape(dec_b * past, MLA_ROPE_DIM),
                               ((0, 0), (0, LANES - MLA_ROPE_DIM)))
            kc, vc_t = _mla_state_kv(state_mla_ckv[:, j].reshape(dec_b * past, MLA_KV_RANK), kr_state,
                                     w_mla[4], wuv_t)
            op = _attention(qp, [(kp, vp)], n_b, wide_qk=True, pairs=8, nb=4)
            os_ = _attention_t(qs, ks, vs_t, kc, vc_t, dec_b, 512, heads=8)
            w_o = _to_bf16(mla_w_o[j])

        pad_g = ((0, ROUTER_EXPERT_ROW - N_GROUPS), (0, 0))
        pad_e = ((0, ROUTER_ROWS - ROUTER_EXPERT_ROW - N_EXPERTS), (0, 0))
        w_router_t = jnp.concatenate([jnp.pad(moe_w_group[i].T, pad_g), jnp.pad(moe_w_expert[i].T, pad_e)],
                                     axis=0).astype(BF16)
        b_router_t = jnp.concatenate([jnp.pad(moe_b_group[i][:, None], pad_g),
                                      jnp.pad(moe_b_expert[i][:, None], pad_e)], axis=0)
        b_router_t = jnp.broadcast_to(b_router_t, (ROUTER_ROWS, TM_SUB))
        final_g = final_norm_g[None] if i == DEPTH - 1 else None
        new_x = []
        for o_att, x, rows_ in ((op, xp, p_rows), (os_, xs, s_rows)):
            xn, hpk, meta, cnt = _proj_res(o_att, x, w_o, g_ffn, mod, i, *rows_, w_router_t, b_router_t)
            pos, tile_group, n_used, pad_bounds = _moe_plan(meta, cnt)
            ypk = _moe(hpk, pos, tile_group, n_used, pad_bounds, wg, wu, wd, i)
            new_x.append(_combine(ypk, pos, xn, mod, i, *rows_, final_g))
        xp, xs = new_x

    return (xp.reshape(n_b, seq, d), xs.reshape(dec_b, n_lat, d),
            outs['na_k'], outs['na_v'], outs['ckv'], outs['kr'])
```

```python
import functools

import numpy as np
import jax
import jax.numpy as jnp
from jax import lax
from jax.experimental import pallas as pl
from jax.experimental.pallas import tpu as pltpu

F32 = jnp.float32
BF16 = jnp.bfloat16

D_MODEL = 1024
DEPTH = 2
GRID_W = 64
N_MOD = 6
EPS = 1e-6
NEG_INF = -1e30
NA_HEADS = 16
NA_HEAD_DIM = 64
NA_KH = 8
NA_KW = 16
NA_GROUP_ROWS = 4
NA_WIN_ROWS = 12
MLA_HEADS = 16
MLA_Q_RANK = 384
MLA_KV_RANK = 256
MLA_NOPE_DIM = 64
MLA_ROPE_DIM = 32
MLA_V_DIM = 64
MLA_QK_DIM = MLA_NOPE_DIM + MLA_ROPE_DIM
MLA_IN_PAD = 768
ROPE_THETA = 10000.0
LOG2_E = 1.4426950408889634
N_GROUPS = 4
EXPERTS_PER_GROUP = 4
N_EXPERTS = 16
D_FF_EXPERT = 256

LANES = 128
MOD_ROWS = 8
VMEM_LIMIT = 56 * 1024 * 1024
TM = 512
CAST_BLOCK_ELEMS = 512 * 1024
TM_MLA = 1024
TM_SUB_MLA = 512
TM_SUB = 512
TM_PROJ = 1024


def _cparams(sem):
    return pltpu.CompilerParams(dimension_semantics=sem, vmem_limit_bytes=VMEM_LIMIT)


def _const_spec(shape):
    nd = len(shape)
    return pl.BlockSpec(shape, lambda *_: (0,) * nd)


def _cast_kernel(x_ref, o_ref):
    o_ref[...] = x_ref[...].astype(o_ref.dtype)


def _to_bf16(x):
    cols = x.shape[-1]
    x2 = x.reshape(-1, cols)
    rows = x2.shape[0]
    block = rows
    while block * cols > CAST_BLOCK_ELEMS and block % 32 == 0:
        block //= 2
    out = pl.pallas_call(
        _cast_kernel,
        grid=(rows // block,),
        in_specs=[pl.BlockSpec((block, cols), lambda i: (i, 0))],
        out_specs=pl.BlockSpec((block, cols), lambda i: (i, 0)),
        out_shape=jax.ShapeDtypeStruct((rows, cols), BF16),
        compiler_params=_cparams(("arbitrary",)),
        name="to_bf16",
    )(x2)
    return out.reshape(x.shape)


def _transpose_kernel(x_ref, o_ref):
    o_ref[...] = x_ref[...].T.astype(o_ref.dtype)


def _transposed_bf16(x, col0=0, n_cols=None):
    rows = x.shape[0]
    n_cols = x.shape[1] - col0 if n_cols is None else n_cols
    bc = 2 * LANES
    assert col0 % bc == 0 and n_cols % bc == 0
    return pl.pallas_call(
        _transpose_kernel,
        grid=(n_cols // bc,),
        in_specs=[pl.BlockSpec((rows, bc), lambda i: (0, col0 // bc + i))],
        out_specs=pl.BlockSpec((bc, rows), lambda i: (i, 0)),
        out_shape=jax.ShapeDtypeStruct((n_cols, rows), BF16),
        compiler_params=_cparams(("arbitrary",)),
        name="transposed_bf16",
    )(x)


def _norm_mod(x, g, shift, scale):
    ms = jnp.mean(x * x, axis=-1, keepdims=True)
    y = x * lax.rsqrt(ms + EPS) * g
    return y * (1.0 + scale) + shift


def _rms(x, g):
    ms = jnp.mean(x * x, axis=-1, keepdims=True)
    return x * lax.rsqrt(ms + EPS) * g


def _silu(x):
    return x / (1.0 + jnp.exp(-x))


def _ada_kernel(cond_ref, w_ref, b_ref, o_ref):
    s = _silu(cond_ref[...]).astype(BF16)
    o_ref[0] = jnp.dot(s, w_ref[0].astype(BF16), preferred_element_type=F32) + b_ref[0]


def _ada_mod(cond8, ada_w, ada_b):
    tn = 1024
    n = N_MOD * D_MODEL
    out = pl.pallas_call(
        _ada_kernel,
        grid=(DEPTH, n // tn),
        in_specs=[
            _const_spec((MOD_ROWS, D_MODEL)),
            pl.BlockSpec((1, D_MODEL, tn), lambda l, j: (l, 0, j)),
            pl.BlockSpec((1, 1, tn), lambda l, j: (l, 0, j)),
        ],
        out_specs=pl.BlockSpec((1, MOD_ROWS, tn), lambda l, j: (l, 0, j)),
        out_shape=jax.ShapeDtypeStruct((DEPTH, MOD_ROWS, n), F32),
        compiler_params=_cparams(("arbitrary", "arbitrary")),
        name="ada_mod",
    )(cond8, ada_w, ada_b.reshape(DEPTH, 1, n))
    return out.reshape(DEPTH * MOD_ROWS * N_MOD, 1, D_MODEL)


def _mod_spec(layer, k, row0, tokens_per_row, tm=TM):
    def imap(i):
        row = row0 + (i * tm) // tokens_per_row
        return ((layer * MOD_ROWS + row) * N_MOD + k, 0, 0)
    return pl.BlockSpec((1, 1, D_MODEL), imap)


def _qkv_kernel(x_ref, g_ref, sh_ref, sc_ref, w_ref, q_ref, k_ref, v_ref, ks_ref, vs_ref):
    h = _norm_mod(x_ref[...], g_ref[...], sh_ref[0], sc_ref[0]).astype(BF16)
    y = jnp.dot(h, w_ref[...], preferred_element_type=F32)
    q_ref[...] = (y[:, :D_MODEL] * (NA_HEAD_DIM ** -0.5)).astype(q_ref.dtype)
    k = y[:, D_MODEL:2 * D_MODEL]
    v = y[:, 2 * D_MODEL:]
    k_ref[...] = k.astype(k_ref.dtype)
    v_ref[...] = v.astype(v_ref.dtype)
    ks_ref[...] = pltpu.einshape("t(hd)->thd", k, h=NA_HEADS).reshape(ks_ref.shape)
    vs_ref[...] = pltpu.einshape("t(hd)->thd", v, h=NA_HEADS).reshape(vs_ref.shape)


def _na_qkv(x, g, mod, layer, row0, tokens_per_row, w_qkv, n_batch):
    t = x.shape[0]
    seq = t // n_batch
    nb = TM // seq
    tok = pl.BlockSpec((TM, D_MODEL), lambda i: (i, 0))
    state = pl.BlockSpec((nb, 1, seq, NA_HEADS, NA_HEAD_DIM), lambda i: (i, 0, 0, 0, 0))
    state_shape = jax.ShapeDtypeStruct((n_batch, 1, seq, NA_HEADS, NA_HEAD_DIM), F32)
    return pl.pallas_call(
        _qkv_kernel,
        grid=(t // TM,),
        in_specs=[tok, _const_spec((1, D_MODEL)),
                  _mod_spec(layer, 0, row0, tokens_per_row), _mod_spec(layer, 1, row0, tokens_per_row),
                  _const_spec((D_MODEL, 3 * D_MODEL))],
        out_specs=[tok, tok, tok, state, state],
        out_shape=[jax.ShapeDtypeStruct((t, D_MODEL), BF16)] * 3 + [state_shape, state_shape],
        compiler_params=_cparams(("arbitrary",)),
        name="na_qkv",
    )(x, g, mod, mod, w_qkv)


def _store_head_slots(vt, vt_ref):
    n = vt.shape[1]
    n_heads = vt_ref.shape[0] // LANES
    dim = vt.shape[0] // n_heads
    tail = jnp.where(lax.broadcasted_iota(jnp.int32, (LANES - dim, n), 0) == 0, 1.0, 0.0).astype(vt_ref.dtype)
    for hh in range(n_heads):
        vt_ref[hh * LANES:hh * LANES + dim, :] = vt[hh * dim:(hh + 1) * dim, :].astype(vt_ref.dtype)
        vt_ref[hh * LANES + dim:(hh + 1) * LANES, :] = tail


def _qkv_t_kernel(x_ref, g_ref, sh_ref, sc_ref, wqk_ref, wvt_ref, q_ref, k_ref, vt_ref):
    h = _norm_mod(x_ref[...], g_ref[...], sh_ref[0], sc_ref[0]).astype(BF16)
    y = jnp.dot(h, wqk_ref[...], preferred_element_type=F32)
    q_ref[...] = (y[:, :D_MODEL] * (NA_HEAD_DIM ** -0.5 * LOG2_E)).astype(q_ref.dtype)
    k_ref[...] = y[:, D_MODEL:].astype(k_ref.dtype)
    _store_head_slots(_qk(wvt_ref[...], h), vt_ref)


def _na_qkv_t(x, g, mod, layer, row0, tokens_per_row, w_qkv, w_vt):
    t = x.shape[0]
    tok = pl.BlockSpec((TM, D_MODEL), lambda i: (i, 0))
    slots = NA_HEADS * LANES
    return pl.pallas_call(
        _qkv_t_kernel,
        grid=(t // TM,),
        in_specs=[tok, _const_spec((1, D_MODEL)),
                  _mod_spec(layer, 0, row0, tokens_per_row), _mod_spec(layer, 1, row0, tokens_per_row),
                  pl.BlockSpec((D_MODEL, 2 * D_MODEL), lambda i: (0, 0)), _const_spec(w_vt.shape)],
        out_specs=[tok, tok, pl.BlockSpec((slots, TM), lambda i: (0, i))],
        out_shape=[jax.ShapeDtypeStruct((t, D_MODEL), BF16), jax.ShapeDtypeStruct((t, D_MODEL), BF16),
                   jax.ShapeDtypeStruct((slots, t), BF16)],
        compiler_params=_cparams(("arbitrary",)),
        name="na_qkv_t",
    )(x, g, mod, mod, w_qkv, w_vt)


def _softmax_pv(s_list, v_list):
    m = s_list[0].max(axis=-1, keepdims=True)
    for s in s_list[1:]:
        m = jnp.maximum(m, s.max(axis=-1, keepdims=True))
    l = None
    acc = None
    for s, v in zip(s_list, v_list):
        p = jnp.exp(s - m)
        ls = p.sum(axis=-1, keepdims=True)
        a = jnp.dot(p.astype(BF16), v, preferred_element_type=F32)
        l = ls if l is None else l + ls
        acc = a if acc is None else acc + a
    return acc * (1.0 / l)


def _qk(q, k):
    return lax.dot_general(q, k, (((1,), (1,)), ((), ())), preferred_element_type=F32)


def _head_masks():
    lane = lax.broadcasted_iota(jnp.int32, (1, LANES), 1)
    lo = lane < (LANES // 2)
    return lo, jnp.where(lo, 1.0, 0.0).astype(BF16), jnp.where(lo, 0.0, 1.0).astype(BF16)


def _attn_kernel(*refs, n_src, wide_qk, pairs, nb):
    q_ref, o_ref = refs[0], refs[-1]
    kv = refs[1:-1]
    lo, m0, m1 = _head_masks()
    wq = 2 * LANES if wide_qk else LANES
    tq = q_ref.shape[0] // nb
    for bb in range(nb):
        rq = slice(bb * tq, (bb + 1) * tq)
        rk = [slice(bb * (kv[2 * t].shape[0] // nb), (bb + 1) * (kv[2 * t].shape[0] // nb)) for t in range(n_src)]
        for p in range(pairs):
            q = q_ref[rq, p * wq:(p + 1) * wq]
            ks = [kv[2 * t][rk[t], p * wq:(p + 1) * wq].astype(BF16) for t in range(n_src)]
            vs = [kv[2 * t + 1][rk[t], p * LANES:(p + 1) * LANES].astype(BF16) for t in range(n_src)]
            outs = []
            for j in range(2):
                if wide_qk:
                    s_list = [_qk(q[:, j * LANES:(j + 1) * LANES], k[:, j * LANES:(j + 1) * LANES]) for k in ks]
                else:
                    s_list = [_qk(q * (m0 if j == 0 else m1), k) for k in ks]
                outs.append(_softmax_pv(s_list, vs))
            o_ref[rq, p * LANES:(p + 1) * LANES] = jnp.where(lo, outs[0], outs[1]).astype(o_ref.dtype)


def _attention(q, kv_srcs, n_batch, wide_qk, pairs, nb):
    wq = (2 * LANES if wide_qk else LANES) * pairs
    wv = LANES * pairs
    tq = q.shape[0] // n_batch * nb
    n_steps = D_MODEL // wv
    in_specs = [pl.BlockSpec((tq, wq), lambda b, p: (b, p))]
    args = [q]
    for k, v in kv_srcs:
        t_k = k.shape[0] // n_batch * nb
        in_specs.append(pl.BlockSpec((t_k, wq), lambda b, p: (b, p)))
        in_specs.append(pl.BlockSpec((t_k, wv), lambda b, p: (b, p)))
        args += [k, v]
    return pl.pallas_call(
        functools.partial(_attn_kernel, n_src=len(kv_srcs), wide_qk=wide_qk, pairs=pairs, nb=nb),
        grid=(n_batch // nb, n_steps),
        in_specs=in_specs,
        out_specs=pl.BlockSpec((tq, wv), lambda b, p: (b, p)),
        out_shape=jax.ShapeDtypeStruct((q.shape[0], D_MODEL), BF16),
        compiler_params=_cparams(("arbitrary", "arbitrary")),
        name="attention",
    )(*args)


def _col_max(s, chunk=64):
    acc = s[:chunk]
    for r in range(chunk, s.shape[0], chunk):
        acc = jnp.maximum(acc, s[r:r + chunk])
    return acc.max(axis=0, keepdims=True)


def _attn_t_kernel(q_ref, k1_ref, k2_ref, v1_ref, v2_ref, o_ref):
    n_heads = q_ref.shape[0] // LANES
    slots = [slice(j * LANES, (j + 1) * LANES) for j in range(n_heads)]
    def scores(sl):
        qt = q_ref[sl, :]
        return (jnp.dot(k1_ref[:, sl], qt, preferred_element_type=F32),
                jnp.dot(k2_ref[:, sl], qt, preferred_element_type=F32))

    nxt = scores(slots[0])
    outs = []
    for n, sl in enumerate(slots):
        s1, s2 = nxt
        if n + 1 < n_heads:
            nxt = scores(slots[n + 1])
        m = jnp.maximum(_col_max(s1), _col_max(s2))
        p1 = jnp.exp2(s1 - m).astype(BF16)
        p2 = jnp.exp2(s2 - m).astype(BF16)
        ot = (jnp.dot(v1_ref[sl, :], p1, preferred_element_type=F32)
              + jnp.dot(v2_ref[sl, :], p2, preferred_element_type=F32))
        outs.append(ot[:MLA_V_DIM] * (1.0 / ot[MLA_V_DIM:MLA_V_DIM + 1]))
    for j in range(0, n_heads, 2):
        pair = jnp.concatenate(outs[j:j + 2], axis=0).T
        o_ref[:, (j // 2) * LANES:(j // 2 + 1) * LANES] = pair.astype(o_ref.dtype)


def _attention_t(qt, k1, v1t, k2, v2t, n_batch, tq, heads):
    wq = heads * LANES
    wo = heads * MLA_V_DIM
    nq = qt.shape[1] // n_batch // tq
    t1 = k1.shape[0] // n_batch
    t2 = k2.shape[0] // n_batch
    return pl.pallas_call(
        _attn_t_kernel,
        grid=(n_batch, MLA_HEADS // heads, nq),
        in_specs=[pl.BlockSpec((wq, tq), lambda b, p, i: (p, b * nq + i)),
                  pl.BlockSpec((t1, wq), lambda b, p, i: (b, p)),
                  pl.BlockSpec((t2, wq), lambda b, p, i: (b, p)),
                  pl.BlockSpec((wq, t1), lambda b, p, i: (p, b)),
                  pl.BlockSpec((wq, t2), lambda b, p, i: (p, b))],
        out_specs=pl.BlockSpec((tq, wo), lambda b, p, i: (b * nq + i, p)),
        out_shape=jax.ShapeDtypeStruct((qt.shape[1], D_MODEL), BF16),
        compiler_params=_cparams(("arbitrary", "arbitrary", "arbitrary")),
        name="attention_t",
    )(qt, k1, k2, v1t, v2t)


def _na_group_layout(rows):
    n_groups = rows // NA_GROUP_ROWS
    kh = min(NA_KH, rows)
    bases, types, keys = [], [], {}
    for gi in range(n_groups):
        r0 = gi * NA_GROUP_ROWS
        base = int(np.clip(r0 - kh // 2, 0, rows - NA_WIN_ROWS))
        starts = tuple(int(np.clip(r - kh // 2, 0, rows - kh)) - base for r in range(r0, r0 + NA_GROUP_ROWS))
        key = (starts, r0 - base)
        if key not in keys:
            keys[key] = len(keys)
        bases.append(base)
        types.append(keys[key])
    return bases, types, list(keys.keys())


def _na_bias_tables(rpb, rows):
    kh = min(NA_KH, rows)
    _, _, type_keys = _na_group_layout(rows)
    qc = np.arange(GRID_W)
    kc = np.arange(GRID_W)
    col_start = np.clip(qc - NA_KW // 2, 0, GRID_W - NA_KW)
    col_valid = (kc[None, :] >= col_start[:, None]) & (kc[None, :] < col_start[:, None] + NA_KW)
    dx = kc[None, :] - qc[:, None] + (NA_KW - 1)
    onehot = ((dx[None] == np.arange(2 * NA_KW - 1)[:, None, None]) & col_valid[None]).astype(np.float32)
    band = jnp.einsum('hyd,dqk->hykq', rpb.astype(F32), onehot, precision=lax.Precision.HIGHEST)
    band = jnp.where(col_valid.T, band, NEG_INF)
    strip = band.reshape(NA_HEADS, (2 * NA_KH - 1) * GRID_W, GRID_W)
    tables = []
    for starts, r_off in type_keys:
        q_cols = []
        for rq in range(NA_GROUP_ROWS):
            dy0 = starts[rq] - (rq + r_off) + (NA_KH - 1)
            piece = strip[:, dy0 * GRID_W:(dy0 + kh) * GRID_W, :]
            pad = ((0, 0), (starts[rq] * GRID_W, (NA_WIN_ROWS - kh - starts[rq]) * GRID_W), (0, 0))
            q_cols.append(jnp.pad(piece, pad, constant_values=NEG_INF))
        tables.append(jnp.concatenate(q_cols, axis=2))
    return jnp.stack(tables, axis=1) * LOG2_E


def _state_k_kernel(x_ref, o_ref):
    for h in range(NA_HEADS):
        o_ref[:, h * NA_HEAD_DIM:(h + 1) * NA_HEAD_DIM] = x_ref[0, 0, :, h, :].astype(o_ref.dtype)


def _state_vt_kernel(x_ref, o_ref, flat_ref):
    for h in range(NA_HEADS):
        flat_ref[:, h * NA_HEAD_DIM:(h + 1) * NA_HEAD_DIM] = x_ref[0, 0, :, h, :]
    _store_head_slots(flat_ref[...].T, o_ref)


def _state_k(x, layer):
    b, _, t, h, dh = x.shape
    return pl.pallas_call(
        _state_k_kernel,
        grid=(b,),
        in_specs=[pl.BlockSpec((1, 1, t, h, dh), lambda i: (i, layer, 0, 0, 0))],
        out_specs=pl.BlockSpec((t, h * dh), lambda i: (i, 0)),
        out_shape=jax.ShapeDtypeStruct((b * t, h * dh), BF16),
        compiler_params=_cparams(("arbitrary",)),
        name="state_k",
    )(x)


def _state_vt(x, layer):
    b, _, t, h, dh = x.shape
    return pl.pallas_call(
        _state_vt_kernel,
        grid=(b,),
        in_specs=[pl.BlockSpec((1, 1, t, h, dh), lambda i: (i, layer, 0, 0, 0))],
        out_specs=pl.BlockSpec((h * LANES, t), lambda i: (0, i)),
        out_shape=jax.ShapeDtypeStruct((h * LANES, b * t), BF16),
        scratch_shapes=[pltpu.VMEM((t, h * dh), F32)],
        compiler_params=_cparams(("arbitrary",)),
        name="state_vt",
    )(x)


def _na_latent_kernel(q_ref, k_ref, vt_ref, kc_ref, vct_ref, tab_ref, o_ref, *, bases, types):
    _, m0, m1 = _head_masks()
    kctx = kc_ref[...]
    gq = NA_GROUP_ROWS * GRID_W
    gk = NA_WIN_ROWS * GRID_W
    blocks = [(gi, base, ty, j) for gi, (base, ty) in enumerate(zip(bases, types)) for j in range(2)]

    def scores(gi, base, ty, j):
        qj = q_ref[gi * gq:(gi + 1) * gq, :] * (m0 if j == 0 else m1)
        kwin = k_ref[base * GRID_W:base * GRID_W + gk, :]
        return _qk(kwin, qj) + tab_ref[j, ty], _qk(kctx, qj)

    nxt = scores(*blocks[0])
    outs = []
    for n, (gi, base, ty, j) in enumerate(blocks):
        s_win, s_ctx = nxt
        if n + 1 < len(blocks):
            nxt = scores(*blocks[n + 1])
        m = jnp.maximum(_col_max(s_win), _col_max(s_ctx))
        p_win = jnp.exp2(s_win - m).astype(BF16)
        p_ctx = jnp.exp2(s_ctx - m).astype(BF16)
        sl = slice(j * LANES, (j + 1) * LANES)
        ot = (jnp.dot(vt_ref[sl, base * GRID_W:base * GRID_W + gk], p_win, preferred_element_type=F32)
              + jnp.dot(vct_ref[sl, :], p_ctx, preferred_element_type=F32))
        outs.append(ot[:NA_HEAD_DIM] * (1.0 / ot[NA_HEAD_DIM:NA_HEAD_DIM + 1]))
        if j == 1:
            o_ref[gi * gq:(gi + 1) * gq, :] = jnp.concatenate(outs, axis=0).T.astype(o_ref.dtype)
            outs = []


def _na_latent(q, k, vt, k_ctx, vt_ctx, tables, n_batch):
    t = q.shape[0] // n_batch
    rows = t // GRID_W
    t_ctx = k_ctx.shape[0] // n_batch
    bases, types, type_keys = _na_group_layout(rows)
    n_types = len(type_keys)
    gq = NA_GROUP_ROWS * GRID_W
    gk = NA_WIN_ROWS * GRID_W
    tok = pl.BlockSpec((t, LANES), lambda p, b: (b, p))
    ctx = pl.BlockSpec((t_ctx, LANES), lambda p, b: (b, p))
    return pl.pallas_call(
        functools.partial(_na_latent_kernel, bases=bases, types=types),
        grid=(D_MODEL // LANES, n_batch),
        in_specs=[tok, tok, pl.BlockSpec((2 * LANES, t), lambda p, b: (p, b)),
                  ctx, pl.BlockSpec((2 * LANES, t_ctx), lambda p, b: (p, b)),
                  pl.BlockSpec((2, n_types, gk, gq), lambda p, b: (p, 0, 0, 0))],
        out_specs=tok,
        out_shape=jax.ShapeDtypeStruct(q.shape, BF16),
        compiler_params=_cparams(("arbitrary", "arbitrary")),
        name="na_latent",
    )(q, k, vt, k_ctx, vt_ctx, tables)


def _rope_lane_tables(n_tokens):
    f32 = np.float32
    n_freq = MLA_ROPE_DIM // 4
    inv = f32(1.0) / (f32(ROPE_THETA) ** (np.arange(n_freq, dtype=f32) / f32(n_freq)))
    tt = np.arange(n_tokens)
    row = (tt // GRID_W).astype(f32)
    col = (tt % GRID_W).astype(f32)
    ang = np.concatenate([row[:, None] * inv, col[:, None] * inv], axis=-1).astype(f32)
    cos, sin = np.cos(ang).astype(f32), np.sin(ang).astype(f32)
    ones = np.ones((n_tokens, MLA_NOPE_DIM), f32)
    z16 = np.zeros((n_tokens, MLA_ROPE_DIM // 2), f32)
    z32 = np.zeros((n_tokens, LANES - MLA_QK_DIM), f32)
    z64 = np.zeros((n_tokens, MLA_NOPE_DIM), f32)
    c_tab = np.concatenate([ones, cos, cos, z32], axis=-1)
    s_dn = np.concatenate([z64, -sin, z16, z32], axis=-1)
    s_up = np.concatenate([z64, z16, sin, z32], axis=-1)
    return (jnp.asarray(c_tab), jnp.asarray(s_dn), jnp.asarray(s_up),
            jnp.asarray(np.ascontiguousarray(cos.T)), jnp.asarray(np.ascontiguousarray(sin.T)))


def _rope_slot(x, c_tab, s_dn, s_up):
    half = MLA_ROPE_DIM // 2
    return x * c_tab + pltpu.roll(x, LANES - half, 1) * s_dn + pltpu.roll(x, half, 1) * s_up


def _mla_kv(ckv_n, kr_slot, wuk_ref, wuv_ref, k_ref, v_ref, v_transposed, rows=slice(None)):
    c = ckv_n.astype(BF16)
    k_nope = jnp.dot(c, wuk_ref[...], preferred_element_type=F32)
    for h in range(MLA_HEADS):
        k_ref[rows, h * LANES:(h + 1) * LANES] = (k_nope[:, h * LANES:(h + 1) * LANES] + kr_slot).astype(k_ref.dtype)
    if v_transposed:
        _store_head_slots(_qk(wuv_ref[...], c), v_ref.at[:, rows])
    else:
        v_ref[rows, :] = jnp.dot(c, wuv_ref[...], preferred_element_type=F32).astype(v_ref.dtype)


def _mla_proj_kernel(*refs, rope, emit_state, q_scale):
    (x_ref, g_ref, sh_ref, sc_ref, win_ref, qg_ref, kvg_ref, wuq_ref, wuk_ref, wuv_ref) = refs[:10]
    pos = 10
    if rope:
        ct_ref, sd_ref, su_ref, cos_ref, sin_ref = refs[pos:pos + 5]
        pos += 5
    q_ref, k_ref, v_ref = refs[pos:pos + 3]
    pos += 3

    def front(rows):
        h = _norm_mod(x_ref[rows, :], g_ref[...], sh_ref[0], sc_ref[0]).astype(BF16)
        c = jnp.dot(h, win_ref[...], preferred_element_type=F32)
        cq = _rms(c[:, :MLA_Q_RANK], qg_ref[...]).astype(BF16)
        if rope:
            q = _qk(wuq_ref[...], cq) * q_scale
        else:
            q = jnp.dot(cq, wuq_ref[...], preferred_element_type=F32) * q_scale
        ckv = _rms(c[:, MLA_Q_RANK:MLA_Q_RANK + MLA_KV_RANK], kvg_ref[...])
        kr_chunk = c[:, MLA_Q_RANK + MLA_KV_RANK:]
        if emit_state:
            ckv_out, kr_out = refs[pos:pos + 2]
            ckv_out[rows, :] = ckv
            kr_out[rows, :] = kr_chunk[:, :MLA_ROPE_DIM]
        return q, ckv, pltpu.roll(kr_chunk, MLA_NOPE_DIM, 1)

    def back(rows, q, ckv, kr_slot):
        if rope:
            ct, sd, su = ct_ref[rows, :], sd_ref[rows, :], su_ref[rows, :]
            kr_slot = _rope_slot(kr_slot, ct, sd, su)
        _mla_kv(ckv, kr_slot, wuk_ref, wuv_ref, k_ref, v_ref, v_transposed=rope, rows=rows)
        if not rope:
            q_ref[rows, :] = q.astype(q_ref.dtype)
            return
        cs, sn = cos_ref[:, rows], sin_ref[:, rows]
        half = MLA_ROPE_DIM // 2
        for hh in range(MLA_HEADS):
            r0 = hh * LANES
            r1 = r0 + MLA_NOPE_DIM
            x1, x2 = q[r1:r1 + half], q[r1 + half:r1 + 2 * half]
            q_ref[r0:r1, rows] = q[r0:r1].astype(q_ref.dtype)
            q_ref[r1:r1 + half, rows] = (x1 * cs - x2 * sn).astype(q_ref.dtype)
            q_ref[r1 + half:r1 + 2 * half, rows] = (x2 * cs + x1 * sn).astype(q_ref.dtype)
            q_ref[r1 + 2 * half:r0 + LANES, rows] = q[r1 + 2 * half:r0 + LANES].astype(q_ref.dtype)

    tiles = [slice(r0, r0 + TM_SUB_MLA) for r0 in range(0, x_ref.shape[0], TM_SUB_MLA)]
    nxt = front(tiles[0])
    for n, rows in enumerate(tiles):
        cur = nxt
        if n + 1 < len(tiles):
            nxt = front(tiles[n + 1])
        back(rows, *cur)


def _mla_proj(x, g, mod, layer, row0, tokens_per_row, w, rope_tabs, emit_state):
    t = x.shape[0]
    win, qg, kvg, wuq, wuk, wuv = w
    tm = TM_MLA
    tok = lambda n: pl.BlockSpec((tm, n), lambda i: (i, 0))
    wq = MLA_HEADS * LANES
    in_specs = [tok(D_MODEL), _const_spec((1, D_MODEL)),
                _mod_spec(layer, 0, row0, tokens_per_row, tm), _mod_spec(layer, 1, row0, tokens_per_row, tm),
                _const_spec(win.shape), _const_spec(qg.shape), _const_spec(kvg.shape),
                _const_spec(wuq.shape), _const_spec(wuk.shape), _const_spec(wuv.shape)]
    args = [x, g, mod, mod, win, qg, kvg, wuq, wuk, wuv]
    rope = rope_tabs is not None
    if rope:
        n_tab = rope_tabs[0].shape[0] // tm
        in_specs += [pl.BlockSpec((tm, LANES), lambda i: (i % n_tab, 0)) for _ in range(3)]
        in_specs += [pl.BlockSpec((MLA_ROPE_DIM // 2, tm), lambda i: (0, i % n_tab)) for _ in range(2)]
        args += list(rope_tabs)
    if rope:
        q_spec, q_shape = pl.BlockSpec((wq, tm), lambda i: (0, i)), jax.ShapeDtypeStruct((wq, t), BF16)
        v_spec, v_shape = q_spec, q_shape
        q_scale = MLA_QK_DIM ** -0.5 * LOG2_E
    else:
        q_spec, q_shape = tok(wq), jax.ShapeDtypeStruct((t, wq), BF16)
        v_spec, v_shape = tok(D_MODEL), jax.ShapeDtypeStruct((t, D_MODEL), BF16)
        q_scale = MLA_QK_DIM ** -0.5
    out_specs = [q_spec, tok(wq), v_spec]
    out_shape = [q_shape, jax.ShapeDtypeStruct((t, wq), BF16), v_shape]
    if emit_state:
        out_specs += [tok(MLA_KV_RANK), tok(MLA_ROPE_DIM)]
        out_shape += [jax.ShapeDtypeStruct((t, MLA_KV_RANK), F32), jax.ShapeDtypeStruct((t, MLA_ROPE_DIM), F32)]
    return pl.pallas_call(
        functools.partial(_mla_proj_kernel, rope=rope, emit_state=emit_state, q_scale=q_scale),
        grid=(t // tm,),
        in_specs=in_specs, out_specs=out_specs, out_shape=out_shape,
        compiler_params=_cparams(("arbitrary",)),
        name="mla_proj",
    )(*args)


def _mla_state_kv_kernel(ckv_ref, kr_ref, wuk_ref, wuv_ref, k_ref, v_ref):
    kr_slot = pltpu.roll(kr_ref[...], MLA_NOPE_DIM, 1)
    _mla_kv(ckv_ref[...], kr_slot, wuk_ref, wuv_ref, k_ref, v_ref, v_transposed=True)


def _mla_state_kv(ckv, kr_pad, wuk, wuv_t):
    t = ckv.shape[0]
    wq = MLA_HEADS * LANES
    tok = lambda n: pl.BlockSpec((TM, n), lambda i: (i, 0))
    return pl.pallas_call(
        _mla_state_kv_kernel,
        grid=(t // TM,),
        in_specs=[tok(MLA_KV_RANK), tok(LANES), _const_spec(wuk.shape), _const_spec(wuv_t.shape)],
        out_specs=[tok(wq), pl.BlockSpec((wq, TM), lambda i: (0, i))],
        out_shape=[jax.ShapeDtypeStruct((t, wq), BF16), jax.ShapeDtypeStruct((wq, t), BF16)],
        compiler_params=_cparams(("arbitrary",)),
        name="mla_state_kv",
    )(ckv, kr_pad, wuk, wuv_t)


def _first_argmax_rows(v, row):
    top = v.max(axis=0, keepdims=True)
    idx = jnp.where(v == top, row, float(v.shape[0])).min(axis=0, keepdims=True)
    return top, idx


ROUTER_ROWS = 32
ROUTER_EXPERT_ROW = 8


def _route_t(lt):
    n = lt.shape[1]
    row8 = lax.broadcasted_iota(jnp.int32, (8, n), 0).astype(F32)
    row16 = lax.broadcasted_iota(jnp.int32, (N_EXPERTS, n), 0).astype(F32)
    is_g = row8 < N_GROUPS
    gl = jnp.where(is_g, lt[:8], -jnp.inf)
    gmax, gsel = _first_argmax_rows(gl, row8)
    p_group = 1.0 / jnp.where(is_g, jnp.exp(gl - gmax), 0.0).sum(axis=0, keepdims=True)
    first = gsel * EXPERTS_PER_GROUP
    e_all = lt[ROUTER_EXPERT_ROW:ROUTER_EXPERT_ROW + N_EXPERTS]
    el = jnp.where((row16 >= first) & (row16 < first + EXPERTS_PER_GROUP), e_all, -jnp.inf)
    t1, i1 = _first_argmax_rows(el, row16)
    t2, i2 = _first_argmax_rows(jnp.where(row16 == i1, -jnp.inf, el), row16)
    e2 = jnp.exp(t2 - t1)
    w1 = 1.0 / (1.0 + e2)
    gates = jnp.where(row16 == i1, w1 * p_group, jnp.where(row16 == i2, e2 * w1 * p_group, 0.0))
    rel = gates[:8] + gates[8:]
    rel = rel + pltpu.roll(rel, EXPERTS_PER_GROUP, 0)
    meta_t = jnp.where(row8 < EXPERTS_PER_GROUP, rel, jnp.where(row8 == META_GROUP_LANE, gsel, 0.0))
    onehot_t = jnp.where(row8 == gsel, 1.0, 0.0)
    return meta_t, onehot_t, row8, gsel


def _pack_halves(x):
    half = x.shape[1] // 2
    lo = lax.bitcast_convert_type(x[:, :half].astype(BF16).astype(F32), jnp.uint32)
    hi = lax.bitcast_convert_type(x[:, half:].astype(BF16).astype(F32), jnp.uint32)
    return (lo >> 16) | hi


def _unpack_halves(p):
    lo = lax.bitcast_convert_type(p << 16, F32)
    hi = lax.bitcast_convert_type(p & jnp.uint32(0xFFFF0000), F32)
    return lo, hi


ROW_WORDS = D_MODEL // 2 + LANES
META_GROUP_LANE = EXPERTS_PER_GROUP
META_RANK_LANE = EXPERTS_PER_GROUP + 1


def _proj_res_kernel(o_ref, x_ref, w_ref, gate_ref, g_ref, sh_ref, sc_ref, wrt_ref, brt_ref, tri_ref,
                     xn_ref, hpk_ref, metat_ref, cnt_ref):
    @pl.when(pl.program_id(0) == 0)
    def _():
        cnt_ref[...] = jnp.zeros(cnt_ref.shape, cnt_ref.dtype)

    sub = tri_ref.shape[0]
    cnt = cnt_ref[:, :1]
    tiles = [slice(r0, r0 + sub) for r0 in range(0, x_ref.shape[0], sub)]

    def out_proj(rows):
        return jnp.dot(o_ref[rows, :], w_ref[...], preferred_element_type=F32)

    y_next = out_proj(tiles[0])
    for n, rows in enumerate(tiles):
        y = y_next
        if n + 1 < len(tiles):
            y_next = out_proj(tiles[n + 1])
        xn = x_ref[rows, :] + gate_ref[0] * y
        xn_ref[rows, :] = xn
        h = _norm_mod(xn, g_ref[...], sh_ref[0], sc_ref[0])
        lt = _qk(wrt_ref[...], h.astype(BF16)) + brt_ref[...]
        meta_t, onehot_t, row8, gsel = _route_t(lt)
        prefix = jnp.dot(onehot_t.astype(BF16), tri_ref[...], preferred_element_type=F32)
        rank = jnp.where(row8 == gsel, prefix + cnt - 1.0, 0.0).sum(axis=0, keepdims=True)
        cnt = cnt + onehot_t.sum(axis=1, keepdims=True)
        meta_t = jnp.where(row8 == META_RANK_LANE, rank, meta_t)
        meta = jnp.concatenate([meta_t, jnp.zeros((LANES - 8, sub), F32)], axis=0).T
        metat_ref[:, rows] = meta_t
        hpk_ref[rows, :] = jnp.concatenate([_pack_halves(h), lax.bitcast_convert_type(meta, jnp.uint32)], axis=1)
    cnt_ref[...] = jnp.broadcast_to(cnt, cnt_ref.shape)


def _proj_res(o, x, w_o, g_ffn, mod, layer, row0, tokens_per_row, w_router_t, b_router_t):
    t = x.shape[0]
    tm = TM_PROJ
    tok = lambda n: pl.BlockSpec((tm, n), lambda i: (i, 0))
    ms = lambda k: _mod_spec(layer, k, row0, tokens_per_row, tm)
    tri = jnp.asarray(np.triu(np.ones((TM_SUB, TM_SUB), np.float32)), BF16)
    return pl.pallas_call(
        _proj_res_kernel,
        grid=(t // tm,),
        in_specs=[tok(D_MODEL), tok(D_MODEL), _const_spec(w_o.shape), ms(2), _const_spec((1, D_MODEL)), ms(3), ms(4),
                  _const_spec(w_router_t.shape), _const_spec(b_router_t.shape), _const_spec((TM_SUB, TM_SUB))],
        out_specs=[tok(D_MODEL), tok(ROW_WORDS), pl.BlockSpec((8, tm), lambda i: (0, i)), _const_spec((8, LANES))],
        out_shape=[jax.ShapeDtypeStruct((t, D_MODEL), F32), jax.ShapeDtypeStruct((t, ROW_WORDS), jnp.uint32),
                   jax.ShapeDtypeStruct((8, t), F32), jax.ShapeDtypeStruct((8, LANES), F32)],
        compiler_params=_cparams(("arbitrary",)),
        name="proj_res",
    )(o, x, w_o, mod, g_ffn, mod, mod, w_router_t, b_router_t, tri)


def _moe_plan(meta_t, cnt):
    t = meta_t.shape[1]
    n_tiles = t // TM + N_GROUPS
    group = meta_t[META_GROUP_LANE].astype(jnp.int32)
    rank = meta_t[META_RANK_LANE].astype(jnp.int32)
    counts = cnt[:N_GROUPS, 0].astype(jnp.int32)
    tiles = (counts + TM - 1) // TM
    tile_end = jnp.cumsum(tiles)
    tile_start = tile_end - tiles
    pos = rank + TM * jnp.sum(jnp.where(group[:, None] == jnp.arange(N_GROUPS)[None], tile_start[None], 0), axis=1)
    tile_group = jnp.minimum(jnp.sum(jnp.arange(n_tiles)[:, None] >= tile_end[None], axis=1), N_GROUPS - 1)
    pad_bounds = jnp.stack([tile_start * TM + counts, tile_end * TM], axis=1).reshape(-1)
    return (pos.astype(jnp.int32), tile_group.astype(jnp.int32), tile_end[-1:].astype(jnp.int32),
            pad_bounds.astype(jnp.int32))


def _gather_rows(srcs, idx_ref, base, dsts, n_rows):
    for r in range(n_rows):
        s = idx_ref[base + r]
        for src_ref, dst_ref in zip(srcs, dsts):
            dst_ref[r:r + 1, :] = src_ref[pl.ds(s, 1), :]


def _moe_kernel(pos_ref, tg_ref, nt_ref, pad_ref, hpk_hbm, wg_ref, wu_ref, wd_ref, ypk_ref,
                src_ref, hpk_ref, hbuf, hid_ref, load_sem):
    i = pl.program_id(0)
    n_tok = hpk_ref.shape[0]

    @pl.when(i == 0)
    def _():
        load = pltpu.make_async_copy(hpk_hbm, hpk_ref, load_sem)
        load.start()
        def clear(p, c):
            src_ref[p] = 0
            return c
        for g in range(N_GROUPS):
            lax.fori_loop(pad_ref[2 * g], pad_ref[2 * g + 1], clear, 0)

        def fill(t, c):
            src_ref[pos_ref[t]] = t
            return c
        lax.fori_loop(0, n_tok, fill, 0, unroll=8)
        load.wait()
        _gather_rows((hpk_ref,), src_ref, 0, (hbuf.at[0],), TM)

    @pl.when(i < nt_ref[0])
    def _():
        slot = i % 2
        row = hbuf[slot]
        lo, hi = _unpack_halves(row[:, :D_MODEL // 2])
        h = jnp.concatenate([lo.astype(BF16), hi.astype(BF16)], axis=1)
        gates = lax.bitcast_convert_type(row[:, D_MODEL // 2:], F32)
        nxt = jnp.minimum(i + 1, nt_ref[0] - 1) * TM
        _gather_rows((hpk_ref,), src_ref, nxt, (hbuf.at[1 - slot],), TM)
        f = D_FF_EXPERT
        for j in range(EXPERTS_PER_GROUP):
            gate_act = _silu(jnp.dot(h, wg_ref[0, j], preferred_element_type=F32))
            hid = gate_act * jnp.dot(h, wu_ref[0, j], preferred_element_type=F32) * gates[:, j:j + 1]
            hid_ref[:, j * f:(j + 1) * f] = hid.astype(BF16)
        ypk_ref[...] = _pack_halves(jnp.dot(hid_ref[...], wd_ref[0], preferred_element_type=F32))

    @pl.when(i >= nt_ref[0])
    def _():
        ypk_ref[...] = jnp.zeros(ypk_ref.shape, ypk_ref.dtype)


def _moe(hpk, pos, tile_group, n_used, pad_bounds, wg, wu, wd, layer):
    t = hpk.shape[0]
    n_tiles = t // TM + N_GROUPS
    f = D_FF_EXPERT
    g0 = layer * N_GROUPS
    grid_spec = pltpu.PrefetchScalarGridSpec(
        num_scalar_prefetch=4,
        grid=(n_tiles,),
        in_specs=[
            pl.BlockSpec(memory_space=pl.ANY),
            pl.BlockSpec((1, EXPERTS_PER_GROUP, D_MODEL, f), lambda i, pos, tg, nt, pad: (g0 + tg[i], 0, 0, 0)),
            pl.BlockSpec((1, EXPERTS_PER_GROUP, D_MODEL, f), lambda i, pos, tg, nt, pad: (g0 + tg[i], 0, 0, 0)),
            pl.BlockSpec((1, EXPERTS_PER_GROUP * f, D_MODEL), lambda i, pos, tg, nt, pad: (g0 + tg[i], 0, 0)),
        ],
        out_specs=pl.BlockSpec((TM, D_MODEL // 2), lambda i, *_: (i, 0)),
        scratch_shapes=[pltpu.SMEM((n_tiles * TM,), jnp.int32),
                        pltpu.VMEM(hpk.shape, hpk.dtype),
                        pltpu.VMEM((2, TM, ROW_WORDS), jnp.uint32),
                        pltpu.VMEM((TM, EXPERTS_PER_GROUP * f), BF16),
                        pltpu.SemaphoreType.DMA(())],
    )
    return pl.pallas_call(
        _moe_kernel,
        grid_spec=grid_spec,
        out_shape=jax.ShapeDtypeStruct((n_tiles * TM, D_MODEL // 2), jnp.uint32),
        compiler_params=_cparams(("arbitrary",)),
        name="moe",
    )(pos, tile_group, n_used, pad_bounds, hpk, wg, wu, wd)


def _combine_kernel(*refs, final_norm):
    pos_ref, ypk_ref, x_ref, gate_ref = refs[:4]
    o_ref, ybuf = refs[-2], refs[-1]
    _gather_rows((ypk_ref,), pos_ref, pl.program_id(0) * TM, (ybuf,), TM)
    lo, hi = _unpack_halves(ybuf[...])
    out = x_ref[...] + gate_ref[0] * jnp.concatenate([lo, hi], axis=1)
    if final_norm:
        out = _rms(out, refs[4][...])
    o_ref[...] = out


def _combine(ypk, pos, x, mod, layer, row0, tokens_per_row, final_g):
    t = x.shape[0]

    def mod_map(i, pos):
        row = row0 + (i * TM) // tokens_per_row
        return ((layer * MOD_ROWS + row) * N_MOD + 5, 0, 0)

    in_specs = [pl.BlockSpec(ypk.shape, lambda i, pos: (0, 0), pipeline_mode=pl.Buffered(1)),
                pl.BlockSpec((TM, D_MODEL), lambda i, pos: (i, 0)),
                pl.BlockSpec((1, 1, D_MODEL), mod_map)]
    args = [ypk, x, mod]
    if final_g is not None:
        in_specs.append(pl.BlockSpec((1, D_MODEL), lambda i, pos: (0, 0)))
        args.append(final_g)
    grid_spec = pltpu.PrefetchScalarGridSpec(
        num_scalar_prefetch=1,
        grid=(t // TM,),
        in_specs=in_specs,
        out_specs=pl.BlockSpec((TM, D_MODEL), lambda i, pos: (i, 0)),
        scratch_shapes=[pltpu.VMEM((TM, D_MODEL // 2), jnp.uint32)],
    )
    return pl.pallas_call(
        functools.partial(_combine_kernel, final_norm=final_g is not None),
        grid_spec=grid_spec,
        out_shape=jax.ShapeDtypeStruct((t, D_MODEL), F32),
        compiler_params=_cparams(("arbitrary",)),
        name="moe_combine",
    )(pos, *args)


def _pad_heads(w, used):
    k = w.shape[0]
    w = w.reshape(k, MLA_HEADS, used)
    return jnp.pad(w, ((0, 0), (0, 0), (0, LANES - used))).reshape(k, MLA_HEADS * LANES)


def kernel(x_prompt, x_sample, state_na_k, state_na_v, state_mla_ckv, state_mla_kr, c, c_ctx, ada_w, ada_b, norm_mix_g, norm_ffn_g, final_norm_g, na_w_qkv, na_w_o, na_rpb, mla_w_in, mla_q_norm_g, mla_kv_norm_g, mla_w_uq, mla_w_uk, mla_w_uv, mla_w_o, moe_w_group, moe_b_group, moe_w_expert, moe_b_expert, moe_w_gate, moe_w_up, moe_w_down):
    n_b, seq, d = x_prompt.shape
    dec_b, n_lat, _ = x_sample.shape
    past = state_na_k.shape[2]
    rows = n_lat // GRID_W
    assert d == D_MODEL and dec_b + 1 <= MOD_ROWS and n_lat % TM_PROJ == 0 and (n_b * seq) % TM_PROJ == 0

    cond8 = jnp.concatenate([c_ctx[None], c, jnp.zeros((MOD_ROWS - 1 - dec_b, d), F32)], axis=0)
    mod = _ada_mod(cond8, ada_w, ada_b)

    xp = x_prompt.reshape(n_b * seq, d)
    xs = x_sample.reshape(dec_b * n_lat, d)
    p_rows = (0, n_b * seq)
    s_rows = (1, n_lat)
    outs = {}

    f = D_FF_EXPERT
    wg = _to_bf16(moe_w_gate).reshape(DEPTH * N_GROUPS, EXPERTS_PER_GROUP, d, f)
    wu = _to_bf16(moe_w_up).reshape(DEPTH * N_GROUPS, EXPERTS_PER_GROUP, d, f)
    wd = _to_bf16(moe_w_down).reshape(DEPTH * N_GROUPS, EXPERTS_PER_GROUP * f, d)

    for i in range(DEPTH):
        j = i // 2
        g_mix = norm_mix_g[i][None]
        g_ffn = norm_ffn_g[i][None]
        if i % 2 == 0:
            w_qkv = _to_bf16(na_w_qkv[j])
            w_vt = _transposed_bf16(na_w_qkv[j], 2 * d)
            qp, kp, vp, outs['na_k'], outs['na_v'] = _na_qkv(xp, g_mix, mod, i, *p_rows, w_qkv, n_b)
            qs, ks, vs_t = _na_qkv_t(xs, g_mix, mod, i, *s_rows, w_qkv, w_vt)
            op = _attention(qp, [(kp, vp)], n_b, wide_qk=False, pairs=8, nb=4)
            tables = _na_bias_tables(na_rpb[j], rows)
            os_ = _na_latent(qs, ks, vs_t, _state_k(state_na_k, j), _state_vt(state_na_v, j), tables, dec_b)
            w_o = _to_bf16(na_w_o[j])
        else:
            pad_in = MLA_IN_PAD - mla_w_in.shape[-1]
            w_mla = (jnp.pad(mla_w_in[j], ((0, 0), (0, pad_in))).astype(BF16),
                     mla_q_norm_g[j][None], mla_kv_norm_g[j][None],
                     _pad_heads(mla_w_uq[j], MLA_QK_DIM).astype(BF16),
                     _pad_heads(mla_w_uk[j], MLA_NOPE_DIM).astype(BF16),
                     mla_w_uv[j].astype(BF16))
            wuv_t = _transposed_bf16(mla_w_uv[j])
            wuq_t = _transposed_bf16(_pad_heads(mla_w_uq[j], MLA_QK_DIM))
            w_mla_t = w_mla[:3] + (wuq_t, w_mla[4], wuv_t)
            qp, kp, vp, ckv_p, kr_p = _mla_proj(xp, g_mix, mod, i, *p_rows, w_mla, None, True)
            qs, ks, vs_t = _mla_proj(xs, g_mix, mod, i, *s_rows, w_mla_t, _rope_lane_tables(n_lat), False)
            outs['ckv'] = ckv_p.reshape(n_b, 1, seq, MLA_KV_RANK)
            outs['kr'] = kr_p.reshape(n_b, 1, seq, MLA_ROPE_DIM)
            kr_state = jnp.pad(state_mla_kr[:, j].reshape(dec_b * past, MLA_ROPE_DIM),
                               ((0, 0), (0, LANES - MLA_ROPE_DIM)))
            kc, vc_t = _mla_state_kv(state_mla_ckv[:, j].reshape(dec_b * past, MLA_KV_RANK), kr_state,
                                     w_mla[4], wuv_t)
            op = _attention(qp, [(kp, vp)], n_b, wide_qk=True, pairs=8, nb=4)
            os_ = _attention_t(qs, ks, vs_t, kc, vc_t, dec_b, 512, heads=8)
            w_o = _to_bf16(mla_w_o[j])

        pad_g = ((0, ROUTER_EXPERT_ROW - N_GROUPS), (0, 0))
        pad_e = ((0, ROUTER_ROWS - ROUTER_EXPERT_ROW - N_EXPERTS), (0, 0))
        w_router_t = jnp.concatenate([jnp.pad(moe_w_group[i].T, pad_g), jnp.pad(moe_w_expert[i].T, pad_e)],
                                     axis=0).astype(BF16)
        b_router_t = jnp.concatenate([jnp.pad(moe_b_group[i][:, None], pad_g),
                                      jnp.pad(moe_b_expert[i][:, None], pad_e)], axis=0)
        b_router_t = jnp.broadcast_to(b_router_t, (ROUTER_ROWS, TM_SUB))
        final_g = final_norm_g[None] if i == DEPTH - 1 else None
        new_x = []
        for o_att, x, rows_ in ((op, xp, p_rows), (os_, xs, s_rows)):
            xn, hpk, meta, cnt = _proj_res(o_att, x, w_o, g_ffn, mod, i, *rows_, w_router_t, b_router_t)
            pos, tile_group, n_used, pad_bounds = _moe_plan(meta, cnt)
            ypk = _moe(hpk, pos, tile_group, n_used, pad_bounds, wg, wu, wd, i)
            new_x.append(_combine(ypk, pos, xn, mod, i, *rows_, final_g))
        xp, xs = new_x

    return (xp.reshape(n_b, seq, d), xs.reshape(dec_b, n_lat, d),
            outs['na_k'], outs['na_v'], outs['ckv'], outs['kr'])
```

```python
import functools

import numpy as np
import jax
import jax.numpy as jnp
from jax import lax
from jax.experimental import pallas as pl
from jax.experimental.pallas import tpu as pltpu

F32 = jnp.float32
BF16 = jnp.bfloat16

D_MODEL = 1024
DEPTH = 2
GRID_W = 64
N_MOD = 6
EPS = 1e-6
NEG_INF = -1e30
NA_HEADS = 16
NA_HEAD_DIM = 64
NA_KH = 8
NA_KW = 16
NA_GROUP_ROWS = 4
NA_WIN_ROWS = 12
MLA_HEADS = 16
MLA_Q_RANK = 384
MLA_KV_RANK = 256
MLA_NOPE_DIM = 64
MLA_ROPE_DIM = 32
MLA_V_DIM = 64
MLA_QK_DIM = MLA_NOPE_DIM + MLA_ROPE_DIM
MLA_IN_PAD = 768
ROPE_THETA = 10000.0
LOG2_E = 1.4426950408889634
N_GROUPS = 4
EXPERTS_PER_GROUP = 4
N_EXPERTS = 16
D_FF_EXPERT = 256

LANES = 128
MOD_ROWS = 8
VMEM_LIMIT = 56 * 1024 * 1024
TM = 512
CAST_BLOCK_ELEMS = 512 * 1024
TM_MLA = 1024
TM_SUB_MLA = 512
TM_SUB = 512
TM_PROJ = 1024


def _cparams(sem):
    return pltpu.CompilerParams(dimension_semantics=sem, vmem_limit_bytes=VMEM_LIMIT)


def _const_spec(shape):
    nd = len(shape)
    return pl.BlockSpec(shape, lambda *_: (0,) * nd)


def _cast_kernel(x_ref, o_ref):
    o_ref[...] = x_ref[...].astype(o_ref.dtype)


def _to_bf16(x):
    cols = x.shape[-1]
    x2 = x.reshape(-1, cols)
    rows = x2.shape[0]
    block = rows
    while block * cols > CAST_BLOCK_ELEMS and block % 32 == 0:
        block //= 2
    out = pl.pallas_call(
        _cast_kernel,
        grid=(rows // block,),
        in_specs=[pl.BlockSpec((block, cols), lambda i: (i, 0))],
        out_specs=pl.BlockSpec((block, cols), lambda i: (i, 0)),
        out_shape=jax.ShapeDtypeStruct((rows, cols), BF16),
        compiler_params=_cparams(("arbitrary",)),
        name="to_bf16",
    )(x2)
    return out.reshape(x.shape)


def _transpose_kernel(x_ref, o_ref):
    o_ref[...] = x_ref[...].T.astype(o_ref.dtype)


def _transposed_bf16(x, col0=0, n_cols=None):
    rows = x.shape[0]
    n_cols = x.shape[1] - col0 if n_cols is None else n_cols
    bc = 2 * LANES
    assert col0 % bc == 0 and n_cols % bc == 0
    return pl.pallas_call(
        _transpose_kernel,
        grid=(n_cols // bc,),
        in_specs=[pl.BlockSpec((rows, bc), lambda i: (0, col0 // bc + i))],
        out_specs=pl.BlockSpec((bc, rows), lambda i: (i, 0)),
        out_shape=jax.ShapeDtypeStruct((n_cols, rows), BF16),
        compiler_params=_cparams(("arbitrary",)),
        name="transposed_bf16",
    )(x)


def _norm_mod(x, g, shift, scale):
    ms = jnp.mean(x * x, axis=-1, keepdims=True)
    y = x * lax.rsqrt(ms + EPS) * g
    return y * (1.0 + scale) + shift


def _rms(x, g):
    ms = jnp.mean(x * x, axis=-1, keepdims=True)
    return x * lax.rsqrt(ms + EPS) * g


def _silu(x):
    return x / (1.0 + jnp.exp(-x))


def _ada_kernel(cond_ref, w_ref, b_ref, o_ref):
    s = _silu(cond_ref[...]).astype(BF16)
    o_ref[0] = jnp.dot(s, w_ref[0].astype(BF16), preferred_element_type=F32) + b_ref[0]


def _ada_mod(cond8, ada_w, ada_b):
    tn = 1024
    n = N_MOD * D_MODEL
    out = pl.pallas_call(
        _ada_kernel,
        grid=(DEPTH, n // tn),
        in_specs=[
            _const_spec((MOD_ROWS, D_MODEL)),
            pl.BlockSpec((1, D_MODEL, tn), lambda l, j: (l, 0, j)),
            pl.BlockSpec((1, 1, tn), lambda l, j: (l, 0, j)),
        ],
        out_specs=pl.BlockSpec((1, MOD_ROWS, tn), lambda l, j: (l, 0, j)),
        out_shape=jax.ShapeDtypeStruct((DEPTH, MOD_ROWS, n), F32),
        compiler_params=_cparams(("arbitrary", "arbitrary")),
        name="ada_mod",
    )(cond8, ada_w, ada_b.reshape(DEPTH, 1, n))
    return out.reshape(DEPTH * MOD_ROWS * N_MOD, 1, D_MODEL)


def _mod_spec(layer, k, row0, tokens_per_row, tm=TM):
    def imap(i):
        row = row0 + (i * tm) // tokens_per_row
        return ((layer * MOD_ROWS + row) * N_MOD + k, 0, 0)
    return pl.BlockSpec((1, 1, D_MODEL), imap)


def _qkv_kernel(x_ref, g_ref, sh_ref, sc_ref, w_ref, q_ref, k_ref, v_ref, ks_ref, vs_ref):
    h = _norm_mod(x_ref[...], g_ref[...], sh_ref[0], sc_ref[0]).astype(BF16)
    y = jnp.dot(h, w_ref[...], preferred_element_type=F32)
    q_ref[...] = (y[:, :D_MODEL] * (NA_HEAD_DIM ** -0.5)).astype(q_ref.dtype)
    k = y[:, D_MODEL:2 * D_MODEL]
    v = y[:, 2 * D_MODEL:]
    k_ref[...] = k.astype(k_ref.dtype)
    v_ref[...] = v.astype(v_ref.dtype)
    ks_ref[...] = pltpu.einshape("t(hd)->thd", k, h=NA_HEADS).reshape(ks_ref.shape)
    vs_ref[...] = pltpu.einshape("t(hd)->thd", v, h=NA_HEADS).reshape(vs_ref.shape)


def _na_qkv(x, g, mod, layer, row0, tokens_per_row, w_qkv, n_batch):
    t = x.shape[0]
    seq = t // n_batch
    nb = TM // seq
    tok = pl.BlockSpec((TM, D_MODEL), lambda i: (i, 0))
    state = pl.BlockSpec((nb, 1, seq, NA_HEADS, NA_HEAD_DIM), lambda i: (i, 0, 0, 0, 0))
    state_shape = jax.ShapeDtypeStruct((n_batch, 1, seq, NA_HEADS, NA_HEAD_DIM), F32)
    return pl.pallas_call(
        _qkv_kernel,
        grid=(t // TM,),
        in_specs=[tok, _const_spec((1, D_MODEL)),
                  _mod_spec(layer, 0, row0, tokens_per_row), _mod_spec(layer, 1, row0, tokens_per_row),
                  _const_spec((D_MODEL, 3 * D_MODEL))],
        out_specs=[tok, tok, tok, state, state],
        out_shape=[jax.ShapeDtypeStruct((t, D_MODEL), BF16)] * 3 + [state_shape, state_shape],
        compiler_params=_cparams(("arbitrary",)),
        name="na_qkv",
    )(x, g, mod, mod, w_qkv)


def _store_head_slots(vt, vt_ref):
    n = vt.shape[1]
    n_heads = vt_ref.shape[0] // LANES
    dim = vt.shape[0] // n_heads
    tail = jnp.where(lax.broadcasted_iota(jnp.int32, (LANES - dim, n), 0) == 0, 1.0, 0.0).astype(vt_ref.dtype)
    for hh in range(n_heads):
        vt_ref[hh * LANES:hh * LANES + dim, :] = vt[hh * dim:(hh + 1) * dim, :].astype(vt_ref.dtype)
        vt_ref[hh * LANES + dim:(hh + 1) * LANES, :] = tail


def _qkv_t_kernel(x_ref, g_ref, sh_ref, sc_ref, wqk_ref, wvt_ref, q_ref, k_ref, vt_ref):
    h = _norm_mod(x_ref[...], g_ref[...], sh_ref[0], sc_ref[0]).astype(BF16)
    y = jnp.dot(h, wqk_ref[...], preferred_element_type=F32)
    q_ref[...] = (y[:, :D_MODEL] * (NA_HEAD_DIM ** -0.5 * LOG2_E)).astype(q_ref.dtype)
    k_ref[...] = y[:, D_MODEL:].astype(k_ref.dtype)
    _store_head_slots(_qk(wvt_ref[...], h), vt_ref)


def _na_qkv_t(x, g, mod, layer, row0, tokens_per_row, w_qkv, w_vt):
    t = x.shape[0]
    tok = pl.BlockSpec((TM, D_MODEL), lambda i: (i, 0))
    slots = NA_HEADS * LANES
    return pl.pallas_call(
        _qkv_t_kernel,
        grid=(t // TM,),
        in_specs=[tok, _const_spec((1, D_MODEL)),
                  _mod_spec(layer, 0, row0, tokens_per_row), _mod_spec(layer, 1, row0, tokens_per_row),
                  pl.BlockSpec((D_MODEL, 2 * D_MODEL), lambda i: (0, 0)), _const_spec(w_vt.shape)],
        out_specs=[tok, tok, pl.BlockSpec((slots, TM), lambda i: (0, i))],
        out_shape=[jax.ShapeDtypeStruct((t, D_MODEL), BF16), jax.ShapeDtypeStruct((t, D_MODEL), BF16),
                   jax.ShapeDtypeStruct((slots, t), BF16)],
        compiler_params=_cparams(("arbitrary",)),
        name="na_qkv_t",
    )(x, g, mod, mod, w_qkv, w_vt)


def _softmax_pv(s_list, v_list):
    m = s_list[0].max(axis=-1, keepdims=True)
    for s in s_list[1:]:
        m = jnp.maximum(m, s.max(axis=-1, keepdims=True))
    l = None
    acc = None
    for s, v in zip(s_list, v_list):
        p = jnp.exp(s - m)
        ls = p.sum(axis=-1, keepdims=True)
        a = jnp.dot(p.astype(BF16), v, preferred_element_type=F32)
        l = ls if l is None else l + ls
        acc = a if acc is None else acc + a
    return acc * (1.0 / l)


def _qk(q, k):
    return lax.dot_general(q, k, (((1,), (1,)), ((), ())), preferred_element_type=F32)


def _head_masks():
    lane = lax.broadcasted_iota(jnp.int32, (1, LANES), 1)
    lo = lane < (LANES // 2)
    return lo, jnp.where(lo, 1.0, 0.0).astype(BF16), jnp.where(lo, 0.0, 1.0).astype(BF16)


def _attn_kernel(*refs, n_src, wide_qk, pairs, nb):
    q_ref, o_ref = refs[0], refs[-1]
    kv = refs[1:-1]
    lo, m0, m1 = _head_masks()
    wq = 2 * LANES if wide_qk else LANES
    tq = q_ref.shape[0] // nb
    for bb in range(nb):
        rq = slice(bb * tq, (bb + 1) * tq)
        rk = [slice(bb * (kv[2 * t].shape[0] // nb), (bb + 1) * (kv[2 * t].shape[0] // nb)) for t in range(n_src)]
        for p in range(pairs):
            q = q_ref[rq, p * wq:(p + 1) * wq]
            ks = [kv[2 * t][rk[t], p * wq:(p + 1) * wq].astype(BF16) for t in range(n_src)]
            vs = [kv[2 * t + 1][rk[t], p * LANES:(p + 1) * LANES].astype(BF16) for t in range(n_src)]
            outs = []
            for j in range(2):
                if wide_qk:
                    s_list = [_qk(q[:, j * LANES:(j + 1) * LANES], k[:, j * LANES:(j + 1) * LANES]) for k in ks]
                else:
                    s_list = [_qk(q * (m0 if j == 0 else m1), k) for k in ks]
                outs.append(_softmax_pv(s_list, vs))
            o_ref[rq, p * LANES:(p + 1) * LANES] = jnp.where(lo, outs[0], outs[1]).astype(o_ref.dtype)


def _attention(q, kv_srcs, n_batch, wide_qk, pairs, nb):
    wq = (2 * LANES if wide_qk else LANES) * pairs
    wv = LANES * pairs
    tq = q.shape[0] // n_batch * nb
    n_steps = D_MODEL // wv
    in_specs = [pl.BlockSpec((tq, wq), lambda b, p: (b, p))]
    args = [q]
    for k, v in kv_srcs:
        t_k = k.shape[0] // n_batch * nb
        in_specs.append(pl.BlockSpec((t_k, wq), lambda b, p: (b, p)))
        in_specs.append(pl.BlockSpec((t_k, wv), lambda b, p: (b, p)))
        args += [k, v]
    return pl.pallas_call(
        functools.partial(_attn_kernel, n_src=len(kv_srcs), wide_qk=wide_qk, pairs=pairs, nb=nb),
        grid=(n_batch // nb, n_steps),
        in_specs=in_specs,
        out_specs=pl.BlockSpec((tq, wv), lambda b, p: (b, p)),
        out_shape=jax.ShapeDtypeStruct((q.shape[0], D_MODEL), BF16),
        compiler_params=_cparams(("arbitrary", "arbitrary")),
        name="attention",
    )(*args)


def _col_max(s, chunk=64):
    acc = s[:chunk]
    for r in range(chunk, s.shape[0], chunk):
        acc = jnp.maximum(acc, s[r:r + chunk])
    return acc.max(axis=0, keepdims=True)


def _attn_t_kernel(q_ref, k1_ref, k2_ref, v1_ref, v2_ref, o_ref):
    n_heads = q_ref.shape[0] // LANES
    slots = [slice(j * LANES, (j + 1) * LANES) for j in range(n_heads)]
    def scores(sl):
        qt = q_ref[sl, :]
        return (jnp.dot(k1_ref[:, sl], qt, preferred_element_type=F32),
                jnp.dot(k2_ref[:, sl], qt, preferred_element_type=F32))

    nxt = scores(slots[0])
    outs = []
    for n, sl in enumerate(slots):
        s1, s2 = nxt
        if n + 1 < n_heads:
            nxt = scores(slots[n + 1])
        m = jnp.maximum(_col_max(s1), _col_max(s2))
        p1 = jnp.exp2(s1 - m).astype(BF16)
        p2 = jnp.exp2(s2 - m).astype(BF16)
        ot = (jnp.dot(v1_ref[sl, :], p1, preferred_element_type=F32)
              + jnp.dot(v2_ref[sl, :], p2, preferred_element_type=F32))
        outs.append(ot[:MLA_V_DIM] * (1.0 / ot[MLA_V_DIM:MLA_V_DIM + 1]))
    for j in range(0, n_heads, 2):
        pair = jnp.concatenate(outs[j:j + 2], axis=0).T
        o_ref[:, (j // 2) * LANES:(j // 2 + 1) * LANES] = pair.astype(o_ref.dtype)


def _attention_t(qt, k1, v1t, k2, v2t, n_batch, tq, heads):
    wq = heads * LANES
    wo = heads * MLA_V_DIM
    nq = qt.shape[1] // n_batch // tq
    t1 = k1.shape[0] // n_batch
    t2 = k2.shape[0] // n_batch
    return pl.pallas_call(
        _attn_t_kernel,
        grid=(n_batch, MLA_HEADS // heads, nq),
        in_specs=[pl.BlockSpec((wq, tq), lambda b, p, i: (p, b * nq + i)),
                  pl.BlockSpec((t1, wq), lambda b, p, i: (b, p)),
                  pl.BlockSpec((t2, wq), lambda b, p, i: (b, p)),
                  pl.BlockSpec((wq, t1), lambda b, p, i: (p, b)),
                  pl.BlockSpec((wq, t2), lambda b, p, i: (p, b))],
        out_specs=pl.BlockSpec((tq, wo), lambda b, p, i: (b * nq + i, p)),
        out_shape=jax.ShapeDtypeStruct((qt.shape[1], D_MODEL), BF16),
        compiler_params=_cparams(("arbitrary", "arbitrary", "arbitrary")),
        name="attention_t",
    )(qt, k1, k2, v1t, v2t)


def _na_group_layout(rows):
    n_groups = rows // NA_GROUP_ROWS
    kh = min(NA_KH, rows)
    bases, types, keys = [], [], {}
    for gi in range(n_groups):
        r0 = gi * NA_GROUP_ROWS
        base = int(np.clip(r0 - kh // 2, 0, rows - NA_WIN_ROWS))
        starts = tuple(int(np.clip(r - kh // 2, 0, rows - kh)) - base for r in range(r0, r0 + NA_GROUP_ROWS))
        key = (starts, r0 - base)
        if key not in keys:
            keys[key] = len(keys)
        bases.append(base)
        types.append(keys[key])
    return bases, types, list(keys.keys())


def _na_bias_tables(rpb, rows):
    kh = min(NA_KH, rows)
    _, _, type_keys = _na_group_layout(rows)
    qc = np.arange(GRID_W)
    kc = np.arange(GRID_W)
    col_start = np.clip(qc - NA_KW // 2, 0, GRID_W - NA_KW)
    col_valid = (kc[None, :] >= col_start[:, None]) & (kc[None, :] < col_start[:, None] + NA_KW)
    dx = kc[None, :] - qc[:, None] + (NA_KW - 1)
    onehot = ((dx[None] == np.arange(2 * NA_KW - 1)[:, None, None]) & col_valid[None]).astype(np.float32)
    band = jnp.einsum('hyd,dqk->hykq', rpb.astype(F32), onehot, precision=lax.Precision.HIGHEST)
    band = jnp.where(col_valid.T, band, NEG_INF)
    strip = band.reshape(NA_HEADS, (2 * NA_KH - 1) * GRID_W, GRID_W)
    tables = []
    for starts, r_off in type_keys:
        q_cols = []
        for rq in range(NA_GROUP_ROWS):
            dy0 = starts[rq] - (rq + r_off) + (NA_KH - 1)
            piece = strip[:, dy0 * GRID_W:(dy0 + kh) * GRID_W, :]
            pad = ((0, 0), (starts[rq] * GRID_W, (NA_WIN_ROWS - kh - starts[rq]) * GRID_W), (0, 0))
            q_cols.append(jnp.pad(piece, pad, constant_values=NEG_INF))
        tables.append(jnp.concatenate(q_cols, axis=2))
    return jnp.stack(tables, axis=1) * LOG2_E


def _state_k_kernel(x_ref, o_ref):
    for h in range(NA_HEADS):
        o_ref[:, h * NA_HEAD_DIM:(h + 1) * NA_HEAD_DIM] = x_ref[0, 0, :, h, :].astype(o_ref.dtype)


def _state_vt_kernel(x_ref, o_ref, flat_ref):
    for h in range(NA_HEADS):
        flat_ref[:, h * NA_HEAD_DIM:(h + 1) * NA_HEAD_DIM] = x_ref[0, 0, :, h, :]
    _store_head_slots(flat_ref[...].T, o_ref)


def _state_k(x, layer):
    b, _, t, h, dh = x.shape
    return pl.pallas_call(
        _state_k_kernel,
        grid=(b,),
        in_specs=[pl.BlockSpec((1, 1, t, h, dh), lambda i: (i, layer, 0, 0, 0))],
        out_specs=pl.BlockSpec((t, h * dh), lambda i: (i, 0)),
        out_shape=jax.ShapeDtypeStruct((b * t, h * dh), BF16),
        compiler_params=_cparams(("arbitrary",)),
        name="state_k",
    )(x)


def _state_vt(x, layer):
    b, _, t, h, dh = x.shape
    return pl.pallas_call(
        _state_vt_kernel,
        grid=(b,),
        in_specs=[pl.BlockSpec((1, 1, t, h, dh), lambda i: (i, layer, 0, 0, 0))],
        out_specs=pl.BlockSpec((h * LANES, t), lambda i: (0, i)),
        out_shape=jax.ShapeDtypeStruct((h * LANES, b * t), BF16),
        scratch_shapes=[pltpu.VMEM((t, h * dh), F32)],
        compiler_params=_cparams(("arbitrary",)),
        name="state_vt",
    )(x)


def _na_latent_kernel(q_ref, k_ref, vt_ref, kc_ref, vct_ref, tab_ref, o_ref, *, bases, types):
    _, m0, m1 = _head_masks()
    kctx = kc_ref[...]
    gq = NA_GROUP_ROWS * GRID_W
    gk = NA_WIN_ROWS * GRID_W
    blocks = [(gi, base, ty, j) for gi, (base, ty) in enumerate(zip(bases, types)) for j in range(2)]

    def scores(gi, base, ty, j):
        qj = q_ref[gi * gq:(gi + 1) * gq, :] * (m0 if j == 0 else m1)
        kwin = k_ref[base * GRID_W:base * GRID_W + gk, :]
        return _qk(kwin, qj) + tab_ref[j, ty], _qk(kctx, qj)

    nxt = scores(*blocks[0])
    outs = []
    for n, (gi, base, ty, j) in enumerate(blocks):
        s_win, s_ctx = nxt
        if n + 1 < len(blocks):
            nxt = scores(*blocks[n + 1])
        m = jnp.maximum(_col_max(s_win), _col_max(s_ctx))
        p_win = jnp.exp2(s_win - m).astype(BF16)
        p_ctx = jnp.exp2(s_ctx - m).astype(BF16)
        sl = slice(j * LANES, (j + 1) * LANES)
        ot = (jnp.dot(vt_ref[sl, base * GRID_W:base * GRID_W + gk], p_win, preferred_element_type=F32)
              + jnp.dot(vct_ref[sl, :], p_ctx, preferred_element_type=F32))
        outs.append(ot[:NA_HEAD_DIM] * (1.0 / ot[NA_HEAD_DIM:NA_HEAD_DIM + 1]))
        if j == 1:
            o_ref[gi * gq:(gi + 1) * gq, :] = jnp.concatenate(outs, axis=0).T.astype(o_ref.dtype)
            outs = []


def _na_latent(q, k, vt, k_ctx, vt_ctx, tables, n_batch):
    t = q.shape[0] // n_batch
    rows = t // GRID_W
    t_ctx = k_ctx.shape[0] // n_batch
    bases, types, type_keys = _na_group_layout(rows)
    n_types = len(type_keys)
    gq = NA_GROUP_ROWS * GRID_W
    gk = NA_WIN_ROWS * GRID_W
    tok = pl.BlockSpec((t, LANES), lambda p, b: (b, p))
    ctx = pl.BlockSpec((t_ctx, LANES), lambda p, b: (b, p))
    return pl.pallas_call(
        functools.partial(_na_latent_kernel, bases=bases, types=types),
        grid=(D_MODEL // LANES, n_batch),
        in_specs=[tok, tok, pl.BlockSpec((2 * LANES, t), lambda p, b: (p, b)),
                  ctx, pl.BlockSpec((2 * LANES, t_ctx), lambda p, b: (p, b)),
                  pl.BlockSpec((2, n_types, gk, gq), lambda p, b: (p, 0, 0, 0))],
        out_specs=tok,
        out_shape=jax.ShapeDtypeStruct(q.shape, BF16),
        compiler_params=_cparams(("arbitrary", "arbitrary")),
        name="na_latent",
    )(q, k, vt, k_ctx, vt_ctx, tables)


def _rope_lane_tables(n_tokens):
    f32 = np.float32
    n_freq = MLA_ROPE_DIM // 4
    inv = f32(1.0) / (f32(ROPE_THETA) ** (np.arange(n_freq, dtype=f32) / f32(n_freq)))
    tt = np.arange(n_tokens)
    row = (tt // GRID_W).astype(f32)
    col = (tt % GRID_W).astype(f32)
    ang = np.concatenate([row[:, None] * inv, col[:, None] * inv], axis=-1).astype(f32)
    cos, sin = np.cos(ang).astype(f32), np.sin(ang).astype(f32)
    ones = np.ones((n_tokens, MLA_NOPE_DIM), f32)
    z16 = np.zeros((n_tokens, MLA_ROPE_DIM // 2), f32)
    z32 = np.zeros((n_tokens, LANES - MLA_QK_DIM), f32)
    z64 = np.zeros((n_tokens, MLA_NOPE_DIM), f32)
    c_tab = np.concatenate([ones, cos, cos, z32], axis=-1)
    s_dn = np.concatenate([z64, -sin, z16, z32], axis=-1)
    s_up = np.concatenate([z64, z16, sin, z32], axis=-1)
    return (jnp.asarray(c_tab), jnp.asarray(s_dn), jnp.asarray(s_up),
            jnp.asarray(np.ascontiguousarray(cos.T)), jnp.asarray(np.ascontiguousarray(sin.T)))


def _rope_slot(x, c_tab, s_dn, s_up):
    half = MLA_ROPE_DIM // 2
    return x * c_tab + pltpu.roll(x, LANES - half, 1) * s_dn + pltpu.roll(x, half, 1) * s_up


def _mla_kv(ckv_n, kr_slot, wuk_ref, wuv_ref, k_ref, v_ref, v_transposed, rows=slice(None)):
    c = ckv_n.astype(BF16)
    k_nope = jnp.dot(c, wuk_ref[...], preferred_element_type=F32)
    for h in range(MLA_HEADS):
        k_ref[rows, h * LANES:(h + 1) * LANES] = (k_nope[:, h * LANES:(h + 1) * LANES] + kr_slot).astype(k_ref.dtype)
    if v_transposed:
        _store_head_slots(_qk(wuv_ref[...], c), v_ref.at[:, rows])
    else:
        v_ref[rows, :] = jnp.dot(c, wuv_ref[...], preferred_element_type=F32).astype(v_ref.dtype)


def _mla_proj_kernel(*refs, rope, emit_state, q_scale):
    (x_ref, g_ref, sh_ref, sc_ref, win_ref, qg_ref, kvg_ref, wuq_ref, wuk_ref, wuv_ref) = refs[:10]
    pos = 10
    if rope:
        ct_ref, sd_ref, su_ref, cos_ref, sin_ref = refs[pos:pos + 5]
        pos += 5
    q_ref, k_ref, v_ref = refs[pos:pos + 3]
    pos += 3

    def front(rows):
        h = _norm_mod(x_ref[rows, :], g_ref[...], sh_ref[0], sc_ref[0]).astype(BF16)
        c = jnp.dot(h, win_ref[...], preferred_element_type=F32)
        cq = _rms(c[:, :MLA_Q_RANK], qg_ref[...]).astype(BF16)
        if rope:
            q = _qk(wuq_ref[...], cq) * q_scale
        else:
            q = jnp.dot(cq, wuq_ref[...], preferred_element_type=F32) * q_scale
        ckv = _rms(c[:, MLA_Q_RANK:MLA_Q_RANK + MLA_KV_RANK], kvg_ref[...])
        kr_chunk = c[:, MLA_Q_RANK + MLA_KV_RANK:]
        if emit_state:
            ckv_out, kr_out = refs[pos:pos + 2]
            ckv_out[rows, :] = ckv
            kr_out[rows, :] = kr_chunk[:, :MLA_ROPE_DIM]
        return q, ckv, pltpu.roll(kr_chunk, MLA_NOPE_DIM, 1)

    def back(rows, q, ckv, kr_slot):
        if rope:
            ct, sd, su = ct_ref[rows, :], sd_ref[rows, :], su_ref[rows, :]
            kr_slot = _rope_slot(kr_slot, ct, sd, su)
        _mla_kv(ckv, kr_slot, wuk_ref, wuv_ref, k_ref, v_ref, v_transposed=rope, rows=rows)
        if not rope:
            q_ref[rows, :] = q.astype(q_ref.dtype)
            return
        cs, sn = cos_ref[:, rows], sin_ref[:, rows]
        half = MLA_ROPE_DIM // 2
        for hh in range(MLA_HEADS):
            r0 = hh * LANES
            r1 = r0 + MLA_NOPE_DIM
            x1, x2 = q[r1:r1 + half], q[r1 + half:r1 + 2 * half]
            q_ref[r0:r1, rows] = q[r0:r1].astype(q_ref.dtype)
            q_ref[r1:r1 + half, rows] = (x1 * cs - x2 * sn).astype(q_ref.dtype)
            q_ref[r1 + half:r1 + 2 * half, rows] = (x2 * cs + x1 * sn).astype(q_ref.dtype)
            q_ref[r1 + 2 * half:r0 + LANES, rows] = q[r1 + 2 * half:r0 + LANES].astype(q_ref.dtype)

    tiles = [slice(r0, r0 + TM_SUB_MLA) for r0 in range(0, x_ref.shape[0], TM_SUB_MLA)]
    nxt = front(tiles[0])
    for n, rows in enumerate(tiles):
        cur = nxt
        if n + 1 < len(tiles):
            nxt = front(tiles[n + 1])
        back(rows, *cur)


def _mla_proj(x, g, mod, layer, row0, tokens_per_row, w, rope_tabs, emit_state):
    t = x.shape[0]
    win, qg, kvg, wuq, wuk, wuv = w
    tm = TM_MLA
    tok = lambda n: pl.BlockSpec((tm, n), lambda i: (i, 0))
    wq = MLA_HEADS * LANES
    in_specs = [tok(D_MODEL), _const_spec((1, D_MODEL)),
                _mod_spec(layer, 0, row0, tokens_per_row, tm), _mod_spec(layer, 1, row0, tokens_per_row, tm),
                _const_spec(win.shape), _const_spec(qg.shape), _const_spec(kvg.shape),
                _const_spec(wuq.shape), _const_spec(wuk.shape), _const_spec(wuv.shape)]
    args = [x, g, mod, mod, win, qg, kvg, wuq, wuk, wuv]
    rope = rope_tabs is not None
    if rope:
        n_tab = rope_tabs[0].shape[0] // tm
        in_specs += [pl.BlockSpec((tm, LANES), lambda i: (i % n_tab, 0)) for _ in range(3)]
        in_specs += [pl.BlockSpec((MLA_ROPE_DIM // 2, tm), lambda i: (0, i % n_tab)) for _ in range(2)]
        args += list(rope_tabs)
    if rope:
        q_spec, q_shape = pl.BlockSpec((wq, tm), lambda i: (0, i)), jax.ShapeDtypeStruct((wq, t), BF16)
        v_spec, v_shape = q_spec, q_shape
        q_scale = MLA_QK_DIM ** -0.5 * LOG2_E
    else:
        q_spec, q_shape = tok(wq), jax.ShapeDtypeStruct((t, wq), BF16)
        v_spec, v_shape = tok(D_MODEL), jax.ShapeDtypeStruct((t, D_MODEL), BF16)
        q_scale = MLA_QK_DIM ** -0.5
    out_specs = [q_spec, tok(wq), v_spec]
    out_shape = [q_shape, jax.ShapeDtypeStruct((t, wq), BF16), v_shape]
    if emit_state:
        out_specs += [tok(MLA_KV_RANK), tok(MLA_ROPE_DIM)]
        out_shape += [jax.ShapeDtypeStruct((t, MLA_KV_RANK), F32), jax.ShapeDtypeStruct((t, MLA_ROPE_DIM), F32)]
    return pl.pallas_call(
        functools.partial(_mla_proj_kernel, rope=rope, emit_state=emit_state, q_scale=q_scale),
        grid=(t // tm,),
        in_specs=in_specs, out_specs=out_specs, out_shape=out_shape,
        compiler_params=_cparams(("arbitrary",)),
        name="mla_proj",
    )(*args)


def _mla_state_kv_kernel(ckv_ref, kr_ref, wuk_ref, wuv_ref, k_ref, v_ref):
    kr_slot = pltpu.roll(kr_ref[...], MLA_NOPE_DIM, 1)
    _mla_kv(ckv_ref[...], kr_slot, wuk_ref, wuv_ref, k_ref, v_ref, v_transposed=True)


def _mla_state_kv(ckv, kr_pad, wuk, wuv_t):
    t = ckv.shape[0]
    wq = MLA_HEADS * LANES
    tok = lambda n: pl.BlockSpec((TM, n), lambda i: (i, 0))
    return pl.pallas_call(
        _mla_state_kv_kernel,
        grid=(t // TM,),
        in_specs=[tok(MLA_KV_RANK), tok(LANES), _const_spec(wuk.shape), _const_spec(wuv_t.shape)],
        out_specs=[tok(wq), pl.BlockSpec((wq, TM), lambda i: (0, i))],
        out_shape=[jax.ShapeDtypeStruct((t, wq), BF16), jax.ShapeDtypeStruct((wq, t), BF16)],
        compiler_params=_cparams(("arbitrary",)),
        name="mla_state_kv",
    )(ckv, kr_pad, wuk, wuv_t)


def _first_argmax_rows(v, row):
    top = v.max(axis=0, keepdims=True)
    idx = jnp.where(v == top, row, float(v.shape[0])).min(axis=0, keepdims=True)
    return top, idx


ROUTER_ROWS = 32
ROUTER_EXPERT_ROW = 8


def _route_t(lt):
    n = lt.shape[1]
    row8 = lax.broadcasted_iota(jnp.int32, (8, n), 0).astype(F32)
    row16 = lax.broadcasted_iota(jnp.int32, (N_EXPERTS, n), 0).astype(F32)
    is_g = row8 < N_GROUPS
    gl = jnp.where(is_g, lt[:8], -jnp.inf)
    gmax, gsel = _first_argmax_rows(gl, row8)
    p_group = 1.0 / jnp.where(is_g, jnp.exp(gl - gmax), 0.0).sum(axis=0, keepdims=True)
    first = gsel * EXPERTS_PER_GROUP
    e_all = lt[ROUTER_EXPERT_ROW:ROUTER_EXPERT_ROW + N_EXPERTS]
    el = jnp.where((row16 >= first) & (row16 < first + EXPERTS_PER_GROUP), e_all, -jnp.inf)
    t1, i1 = _first_argmax_rows(el, row16)
    t2, i2 = _first_argmax_rows(jnp.where(row16 == i1, -jnp.inf, el), row16)
    e2 = jnp.exp(t2 - t1)
    w1 = 1.0 / (1.0 + e2)
    gates = jnp.where(row16 == i1, w1 * p_group, jnp.where(row16 == i2, e2 * w1 * p_group, 0.0))
    rel = gates[:8] + gates[8:]
    rel = rel + pltpu.roll(rel, EXPERTS_PER_GROUP, 0)
    meta_t = jnp.where(row8 < EXPERTS_PER_GROUP, rel, jnp.where(row8 == META_GROUP_LANE, gsel, 0.0))
    onehot_t = jnp.where(row8 == gsel, 1.0, 0.0)
    return meta_t, onehot_t, row8, gsel


def _pack_halves(x):
    half = x.shape[1] // 2
    lo = lax.bitcast_convert_type(x[:, :half].astype(BF16).astype(F32), jnp.uint32)
    hi = lax.bitcast_convert_type(x[:, half:].astype(BF16).astype(F32), jnp.uint32)
    return (lo >> 16) | hi


def _unpack_halves(p):
    lo = lax.bitcast_convert_type(p << 16, F32)
    hi = lax.bitcast_convert_type(p & jnp.uint32(0xFFFF0000), F32)
    return lo, hi


ROW_WORDS = D_MODEL // 2 + LANES
META_GROUP_LANE = EXPERTS_PER_GROUP
META_RANK_LANE = EXPERTS_PER_GROUP + 1


def _proj_res_kernel(o_ref, x_ref, w_ref, gate_ref, g_ref, sh_ref, sc_ref, wrt_ref, brt_ref, tri_ref,
                     xn_ref, hpk_ref, metat_ref, cnt_ref):
    @pl.when(pl.program_id(0) == 0)
    def _():
        cnt_ref[...] = jnp.zeros(cnt_ref.shape, cnt_ref.dtype)

    sub = tri_ref.shape[0]
    cnt = cnt_ref[:, :1]
    tiles = [slice(r0, r0 + sub) for r0 in range(0, x_ref.shape[0], sub)]

    def out_proj(rows):
        return jnp.dot(o_ref[rows, :], w_ref[...], preferred_element_type=F32)

    y_next = out_proj(tiles[0])
    for n, rows in enumerate(tiles):
        y = y_next
        if n + 1 < len(tiles):
            y_next = out_proj(tiles[n + 1])
        xn = x_ref[rows, :] + gate_ref[0] * y
        xn_ref[rows, :] = xn
        h = _norm_mod(xn, g_ref[...], sh_ref[0], sc_ref[0])
        lt = _qk(wrt_ref[...], h.astype(BF16)) + brt_ref[...]
        meta_t, onehot_t, row8, gsel = _route_t(lt)
        prefix = jnp.dot(onehot_t.astype(BF16), tri_ref[...], preferred_element_type=F32)
        rank = jnp.where(row8 == gsel, prefix + cnt - 1.0, 0.0).sum(axis=0, keepdims=True)
        cnt = cnt + onehot_t.sum(axis=1, keepdims=True)
        meta_t = jnp.where(row8 == META_RANK_LANE, rank, meta_t)
        meta = jnp.concatenate([meta_t, jnp.zeros((LANES - 8, sub), F32)], axis=0).T
        metat_ref[:, rows] = meta_t
        hpk_ref[rows, :] = jnp.concatenate([_pack_halves(h), lax.bitcast_convert_type(meta, jnp.uint32)], axis=1)
    cnt_ref[...] = jnp.broadcast_to(cnt, cnt_ref.shape)


def _proj_res(o, x, w_o, g_ffn, mod, layer, row0, tokens_per_row, w_router_t, b_router_t):
    t = x.shape[0]
    tm = TM_PROJ
    tok = lambda n: pl.BlockSpec((tm, n), lambda i: (i, 0))
    ms = lambda k: _mod_spec(layer, k, row0, tokens_per_row, tm)
    tri = jnp.asarray(np.triu(np.ones((TM_SUB, TM_SUB), np.float32)), BF16)
    return pl.pallas_call(
        _proj_res_kernel,
        grid=(t // tm,),
        in_specs=[tok(D_MODEL), tok(D_MODEL), _const_spec(w_o.shape), ms(2), _const_spec((1, D_MODEL)), ms(3), ms(4),
                  _const_spec(w_router_t.shape), _const_spec(b_router_t.shape), _const_spec((TM_SUB, TM_SUB))],
        out_specs=[tok(D_MODEL), tok(ROW_WORDS), pl.BlockSpec((8, tm), lambda i: (0, i)), _const_spec((8, LANES))],
        out_shape=[jax.ShapeDtypeStruct((t, D_MODEL), F32), jax.ShapeDtypeStruct((t, ROW_WORDS), jnp.uint32),
                   jax.ShapeDtypeStruct((8, t), F32), jax.ShapeDtypeStruct((8, LANES), F32)],
        compiler_params=_cparams(("arbitrary",)),
        name="proj_res",
    )(o, x, w_o, mod, g_ffn, mod, mod, w_router_t, b_router_t, tri)


def _moe_plan(meta_t, cnt):
    t = meta_t.shape[1]
    n_tiles = t // TM + N_GROUPS
    group = meta_t[META_GROUP_LANE].astype(jnp.int32)
    rank = meta_t[META_RANK_LANE].astype(jnp.int32)
    counts = cnt[:N_GROUPS, 0].astype(jnp.int32)
    tiles = (counts + TM - 1) // TM
    tile_end = jnp.cumsum(tiles)
    tile_start = tile_end - tiles
    pos = rank + TM * jnp.sum(jnp.where(group[:, None] == jnp.arange(N_GROUPS)[None], tile_start[None], 0), axis=1)
    tile_group = jnp.minimum(jnp.sum(jnp.arange(n_tiles)[:, None] >= tile_end[None], axis=1), N_GROUPS - 1)
    pad_bounds = jnp.stack([tile_start * TM + counts, tile_end * TM], axis=1).reshape(-1)
    return (pos.astype(jnp.int32), tile_group.astype(jnp.int32), tile_end[-1:].astype(jnp.int32),
            pad_bounds.astype(jnp.int32))


def _gather_rows(srcs, idx_ref, base, dsts, n_rows):
    for r in range(n_rows):
        s = idx_ref[base + r]
        for src_ref, dst_ref in zip(srcs, dsts):
            dst_ref[r:r + 1, :] = src_ref[pl.ds(s, 1), :]


def _moe_kernel(pos_ref, tg_ref, nt_ref, pad_ref, hpk_hbm, wg_ref, wu_ref, wd_ref, ypk_ref,
                src_ref, hpk_ref, hbuf, hid_ref, load_sem):
    i = pl.program_id(0)
    n_tok = hpk_ref.shape[0]

    @pl.when(i == 0)
    def _():
        load = pltpu.make_async_copy(hpk_hbm, hpk_ref, load_sem)
        load.start()
        def clear(p, c):
            src_ref[p] = 0
            return c
        for g in range(N_GROUPS):
            lax.fori_loop(pad_ref[2 * g], pad_ref[2 * g + 1], clear, 0)

        def fill(t, c):
            src_ref[pos_ref[t]] = t
            return c
        lax.fori_loop(0, n_tok, fill, 0, unroll=8)
        load.wait()
        _gather_rows((hpk_ref,), src_ref, 0, (hbuf.at[0],), TM)

    @pl.when(i < nt_ref[0])
    def _():
        slot = i % 2
        row = hbuf[slot]
        lo, hi = _unpack_halves(row[:, :D_MODEL // 2])
        h = jnp.concatenate([lo.astype(BF16), hi.astype(BF16)], axis=1)
        gates = lax.bitcast_convert_type(row[:, D_MODEL // 2:], F32)
        nxt = jnp.minimum(i + 1, nt_ref[0] - 1) * TM
        _gather_rows((hpk_ref,), src_ref, nxt, (hbuf.at[1 - slot],), TM)
        f = D_FF_EXPERT
        for j in range(EXPERTS_PER_GROUP):
            gate_act = _silu(jnp.dot(h, wg_ref[0, j], preferred_element_type=F32))
            hid = gate_act * jnp.dot(h, wu_ref[0, j], preferred_element_type=F32) * gates[:, j:j + 1]
            hid_ref[:, j * f:(j + 1) * f] = hid.astype(BF16)
        ypk_ref[...] = _pack_halves(jnp.dot(hid_ref[...], wd_ref[0], preferred_element_type=F32))

    @pl.when(i >= nt_ref[0])
    def _():
        ypk_ref[...] = jnp.zeros(ypk_ref.shape, ypk_ref.dtype)


def _moe(hpk, pos, tile_group, n_used, pad_bounds, wg, wu, wd, layer):
    t = hpk.shape[0]
    n_tiles = t // TM + N_GROUPS
    f = D_FF_EXPERT
    g0 = layer * N_GROUPS
    grid_spec = pltpu.PrefetchScalarGridSpec(
        num_scalar_prefetch=4,
        grid=(n_tiles,),
        in_specs=[
            pl.BlockSpec(memory_space=pl.ANY),
            pl.BlockSpec((1, EXPERTS_PER_GROUP, D_MODEL, f), lambda i, pos, tg, nt, pad: (g0 + tg[i], 0, 0, 0)),
            pl.BlockSpec((1, EXPERTS_PER_GROUP, D_MODEL, f), lambda i, pos, tg, nt, pad: (g0 + tg[i], 0, 0, 0)),
            pl.BlockSpec((1, EXPERTS_PER_GROUP * f, D_MODEL), lambda i, pos, tg, nt, pad: (g0 + tg[i], 0, 0)),
        ],
        out_specs=pl.BlockSpec((TM, D_MODEL // 2), lambda i, *_: (i, 0)),
        scratch_shapes=[pltpu.SMEM((n_tiles * TM,), jnp.int32),
                        pltpu.VMEM(hpk.shape, hpk.dtype),
                        pltpu.VMEM((2, TM, ROW_WORDS), jnp.uint32),
                        pltpu.VMEM((TM, EXPERTS_PER_GROUP * f), BF16),
                        pltpu.SemaphoreType.DMA(())],
    )
    return pl.pallas_call(
        _moe_kernel,
        grid_spec=grid_spec,
        out_shape=jax.ShapeDtypeStruct((n_tiles * TM, D_MODEL // 2), jnp.uint32),
        compiler_params=_cparams(("arbitrary",)),
        name="moe",
    )(pos, tile_group, n_used, pad_bounds, hpk, wg, wu, wd)


def _combine_kernel(*refs, final_norm):
    pos_ref, ypk_ref, x_ref, gate_ref = refs[:4]
    o_ref, ybuf = refs[-2], refs[-1]
    _gather_rows((ypk_ref,), pos_ref, pl.program_id(0) * TM, (ybuf,), TM)
    lo, hi = _unpack_halves(ybuf[...])
    out = x_ref[...] + gate_ref[0] * jnp.concatenate([lo, hi], axis=1)
    if final_norm:
        out = _rms(out, refs[4][...])
    o_ref[...] = out


def _combine(ypk, pos, x, mod, layer, row0, tokens_per_row, final_g):
    t = x.shape[0]

    def mod_map(i, pos):
        row = row0 + (i * TM) // tokens_per_row
        return ((layer * MOD_ROWS + row) * N_MOD + 5, 0, 0)

    in_specs = [pl.BlockSpec(ypk.shape, lambda i, pos: (0, 0), pipeline_mode=pl.Buffered(1)),
                pl.BlockSpec((TM, D_MODEL), lambda i, pos: (i, 0)),
                pl.BlockSpec((1, 1, D_MODEL), mod_map)]
    args = [ypk, x, mod]
    if final_g is not None:
        in_specs.append(pl.BlockSpec((1, D_MODEL), lambda i, pos: (0, 0)))
        args.append(final_g)
    grid_spec = pltpu.PrefetchScalarGridSpec(
        num_scalar_prefetch=1,
        grid=(t // TM,),
        in_specs=in_specs,
        out_specs=pl.BlockSpec((TM, D_MODEL), lambda i, pos: (i, 0)),
        scratch_shapes=[pltpu.VMEM((TM, D_MODEL // 2), jnp.uint32)],
    )
    return pl.pallas_call(
        functools.partial(_combine_kernel, final_norm=final_g is not None),
        grid_spec=grid_spec,
        out_shape=jax.ShapeDtypeStruct((t, D_MODEL), F32),
        compiler_params=_cparams(("arbitrary",)),
        name="moe_combine",
    )(pos, *args)


def _pad_cols_kernel(x_ref, o_ref):
    n = x_ref.shape[1]
    o_ref[:, :n] = x_ref[...].astype(o_ref.dtype)
    o_ref[:, n:] = jnp.zeros((o_ref.shape[0], o_ref.shape[1] - n), o_ref.dtype)


def _pad_cols_bf16(w, n_cols):
    return pl.pallas_call(
        _pad_cols_kernel,
        out_shape=jax.ShapeDtypeStruct((w.shape[0], n_cols), BF16),
        compiler_params=pltpu.CompilerParams(vmem_limit_bytes=VMEM_LIMIT),
        name="pad_cols",
    )(w)


def _pad_heads_kernel(x_ref, o_ref, *, used):
    o_ref[...] = jnp.zeros(o_ref.shape, o_ref.dtype)
    for h in range(MLA_HEADS):
        o_ref[:, h * LANES:h * LANES + used] = x_ref[:, h * used:(h + 1) * used]


def _pad_heads(w, used):
    k = w.shape[0]
    return pl.pallas_call(
        functools.partial(_pad_heads_kernel, used=used),
        out_shape=jax.ShapeDtypeStruct((k, MLA_HEADS * LANES), w.dtype),
        compiler_params=pltpu.CompilerParams(vmem_limit_bytes=VMEM_LIMIT),
        name="pad_heads",
    )(w)


def kernel(x_prompt, x_sample, state_na_k, state_na_v, state_mla_ckv, state_mla_kr, c, c_ctx, ada_w, ada_b, norm_mix_g, norm_ffn_g, final_norm_g, na_w_qkv, na_w_o, na_rpb, mla_w_in, mla_q_norm_g, mla_kv_norm_g, mla_w_uq, mla_w_uk, mla_w_uv, mla_w_o, moe_w_group, moe_b_group, moe_w_expert, moe_b_expert, moe_w_gate, moe_w_up, moe_w_down):
    n_b, seq, d = x_prompt.shape
    dec_b, n_lat, _ = x_sample.shape
    past = state_na_k.shape[2]
    rows = n_lat // GRID_W
    assert d == D_MODEL and dec_b + 1 <= MOD_ROWS and n_lat % TM_PROJ == 0 and (n_b * seq) % TM_PROJ == 0

    cond8 = jnp.concatenate([c_ctx[None], c, jnp.zeros((MOD_ROWS - 1 - dec_b, d), F32)], axis=0)
    mod = _ada_mod(cond8, ada_w, ada_b)

    xp = x_prompt.reshape(n_b * seq, d)
    xs = x_sample.reshape(dec_b * n_lat, d)
    p_rows = (0, n_b * seq)
    s_rows = (1, n_lat)
    outs = {}

    f = D_FF_EXPERT
    wg = _to_bf16(moe_w_gate).reshape(DEPTH * N_GROUPS, EXPERTS_PER_GROUP, d, f)
    wu = _to_bf16(moe_w_up).reshape(DEPTH * N_GROUPS, EXPERTS_PER_GROUP, d, f)
    wd = _to_bf16(moe_w_down).reshape(DEPTH * N_GROUPS, EXPERTS_PER_GROUP * f, d)

    for i in range(DEPTH):
        j = i // 2
        g_mix = norm_mix_g[i][None]
        g_ffn = norm_ffn_g[i][None]
        if i % 2 == 0:
            w_qkv = _to_bf16(na_w_qkv[j])
            w_vt = _transposed_bf16(na_w_qkv[j], 2 * d)
            qp, kp, vp, outs['na_k'], outs['na_v'] = _na_qkv(xp, g_mix, mod, i, *p_rows, w_qkv, n_b)
            qs, ks, vs_t = _na_qkv_t(xs, g_mix, mod, i, *s_rows, w_qkv, w_vt)
            op = _attention(qp, [(kp, vp)], n_b, wide_qk=False, pairs=8, nb=4)
            tables = _na_bias_tables(na_rpb[j], rows)
            os_ = _na_latent(qs, ks, vs_t, _state_k(state_na_k, j), _state_vt(state_na_v, j), tables, dec_b)
            w_o = _to_bf16(na_w_o[j])
        else:
            wuq_pad = _pad_heads(mla_w_uq[j], MLA_QK_DIM)
            w_mla = (_pad_cols_bf16(mla_w_in[j], MLA_IN_PAD),
                     mla_q_norm_g[j][None], mla_kv_norm_g[j][None],
                     wuq_pad.astype(BF16),
                     _pad_heads(mla_w_uk[j], MLA_NOPE_DIM).astype(BF16),
                     mla_w_uv[j].astype(BF16))
            wuv_t = _transposed_bf16(mla_w_uv[j])
            wuq_t = _transposed_bf16(wuq_pad)
            w_mla_t = w_mla[:3] + (wuq_t, w_mla[4], wuv_t)
            qp, kp, vp, ckv_p, kr_p = _mla_proj(xp, g_mix, mod, i, *p_rows, w_mla, None, True)
            qs, ks, vs_t = _mla_proj(xs, g_mix, mod, i, *s_rows, w_mla_t, _rope_lane_tables(n_lat), False)
            outs['ckv'] = ckv_p.reshape(n_b, 1, seq, MLA_KV_RANK)
            outs['kr'] = kr_p.reshape(n_b, 1, seq, MLA_ROPE_DIM)
            kr_state = jnp.pad(state_mla_kr[:, j].reshape(dec_b * past, MLA_ROPE_DIM),
                               ((0, 0), (0, LANES - MLA_ROPE_DIM)))
            kc, vc_t = _mla_state_kv(state_mla_ckv[:, j].reshape(dec_b * past, MLA_KV_RANK), kr_state,
                                     w_mla[4], wuv_t)
            op = _attention(qp, [(kp, vp)], n_b, wide_qk=True, pairs=8, nb=4)
            os_ = _attention_t(qs, ks, vs_t, kc, vc_t, dec_b, 512, heads=8)
            w_o = _to_bf16(mla_w_o[j])

        pad_g = ((0, ROUTER_EXPERT_ROW - N_GROUPS), (0, 0))
        pad_e = ((0, ROUTER_ROWS - ROUTER_EXPERT_ROW - N_EXPERTS), (0, 0))
        w_router_t = jnp.concatenate([jnp.pad(moe_w_group[i].T, pad_g), jnp.pad(moe_w_expert[i].T, pad_e)],
                                     axis=0).astype(BF16)
        b_router_t = jnp.concatenate([jnp.pad(moe_b_group[i][:, None], pad_g),
                                      jnp.pad(moe_b_expert[i][:, None], pad_e)], axis=0)
        b_router_t = jnp.broadcast_to(b_router_t, (ROUTER_ROWS, TM_SUB))
        final_g = final_norm_g[None] if i == DEPTH - 1 else None
        new_x = []
        for o_att, x, rows_ in ((op, xp, p_rows), (os_, xs, s_rows)):
            xn, hpk, meta, cnt = _proj_res(o_att, x, w_o, g_ffn, mod, i, *rows_, w_router_t, b_router_t)
            pos, tile_group, n_used, pad_bounds = _moe_plan(meta, cnt)
            ypk = _moe(hpk, pos, tile_group, n_used, pad_bounds, wg, wu, wd, i)
            new_x.append(_combine(ypk, pos, xn, mod, i, *rows_, final_g))
        xp, xs = new_x

    return (xp.reshape(n_b, seq, d), xs.reshape(dec_b, n_lat, d),
            outs['na_k'], outs['na_v'], outs['ckv'], outs['kr'])
```

```python
import functools

import numpy as np
import jax
import jax.numpy as jnp
from jax import lax
from jax.experimental import pallas as pl
from jax.experimental.pallas import tpu as pltpu

F32 = jnp.float32
BF16 = jnp.bfloat16

D_MODEL = 1024
DEPTH = 2
GRID_W = 64
N_MOD = 6
EPS = 1e-6
NEG_INF = -1e30
NA_HEADS = 16
NA_HEAD_DIM = 64
NA_KH = 8
NA_KW = 16
NA_GROUP_ROWS = 4
NA_WIN_ROWS = 12
MLA_HEADS = 16
MLA_Q_RANK = 384
MLA_KV_RANK = 256
MLA_NOPE_DIM = 64
MLA_ROPE_DIM = 32
MLA_V_DIM = 64
MLA_QK_DIM = MLA_NOPE_DIM + MLA_ROPE_DIM
MLA_IN_PAD = 768
ROPE_THETA = 10000.0
LOG2_E = 1.4426950408889634
N_GROUPS = 4
EXPERTS_PER_GROUP = 4
N_EXPERTS = 16
D_FF_EXPERT = 256

LANES = 128
MOD_ROWS = 8
VMEM_LIMIT = 56 * 1024 * 1024
TM = 512
CAST_BLOCK_ELEMS = 512 * 1024
TM_MLA = 1024
TM_SUB_MLA = 512
TM_SUB = 512
TM_PROJ = 1024


def _cparams(sem):
    return pltpu.CompilerParams(dimension_semantics=sem, vmem_limit_bytes=VMEM_LIMIT)


def _const_spec(shape):
    nd = len(shape)
    return pl.BlockSpec(shape, lambda *_: (0,) * nd)


def _cast_kernel(x_ref, o_ref):
    o_ref[...] = x_ref[...].astype(o_ref.dtype)


def _to_bf16(x):
    cols = x.shape[-1]
    x2 = x.reshape(-1, cols)
    rows = x2.shape[0]
    block = rows
    while block * cols > CAST_BLOCK_ELEMS and block % 32 == 0:
        block //= 2
    out = pl.pallas_call(
        _cast_kernel,
        grid=(rows // block,),
        in_specs=[pl.BlockSpec((block, cols), lambda i: (i, 0))],
        out_specs=pl.BlockSpec((block, cols), lambda i: (i, 0)),
        out_shape=jax.ShapeDtypeStruct((rows, cols), BF16),
        compiler_params=_cparams(("arbitrary",)),
        name="to_bf16",
    )(x2)
    return out.reshape(x.shape)


def _transpose_kernel(x_ref, o_ref):
    o_ref[...] = x_ref[...].T.astype(o_ref.dtype)


def _transposed_bf16(x, col0=0, n_cols=None):
    rows = x.shape[0]
    n_cols = x.shape[1] - col0 if n_cols is None else n_cols
    bc = 2 * LANES
    assert col0 % bc == 0 and n_cols % bc == 0
    return pl.pallas_call(
        _transpose_kernel,
        grid=(n_cols // bc,),
        in_specs=[pl.BlockSpec((rows, bc), lambda i: (0, col0 // bc + i))],
        out_specs=pl.BlockSpec((bc, rows), lambda i: (i, 0)),
        out_shape=jax.ShapeDtypeStruct((n_cols, rows), BF16),
        compiler_params=_cparams(("arbitrary",)),
        name="transposed_bf16",
    )(x)


def _norm_mod(x, g, shift, scale):
    ms = jnp.mean(x * x, axis=-1, keepdims=True)
    y = x * lax.rsqrt(ms + EPS) * g
    return y * (1.0 + scale) + shift


def _rms(x, g):
    ms = jnp.mean(x * x, axis=-1, keepdims=True)
    return x * lax.rsqrt(ms + EPS) * g


def _silu(x):
    return x / (1.0 + jnp.exp(-x))


def _ada_kernel(cond_ref, w_ref, b_ref, o_ref):
    s = _silu(cond_ref[...]).astype(BF16)
    o_ref[0] = jnp.dot(s, w_ref[0].astype(BF16), preferred_element_type=F32) + b_ref[0]


def _ada_mod(cond8, ada_w, ada_b):
    tn = 1024
    n = N_MOD * D_MODEL
    out = pl.pallas_call(
        _ada_kernel,
        grid=(DEPTH, n // tn),
        in_specs=[
            _const_spec((MOD_ROWS, D_MODEL)),
            pl.BlockSpec((1, D_MODEL, tn), lambda l, j: (l, 0, j)),
            pl.BlockSpec((1, 1, tn), lambda l, j: (l, 0, j)),
        ],
        out_specs=pl.BlockSpec((1, MOD_ROWS, tn), lambda l, j: (l, 0, j)),
        out_shape=jax.ShapeDtypeStruct((DEPTH, MOD_ROWS, n), F32),
        compiler_params=_cparams(("arbitrary", "arbitrary")),
        name="ada_mod",
    )(cond8, ada_w, ada_b.reshape(DEPTH, 1, n))
    return out.reshape(DEPTH * MOD_ROWS * N_MOD, 1, D_MODEL)


def _mod_spec(layer, k, row0, tokens_per_row, tm=TM):
    def imap(i):
        row = row0 + (i * tm) // tokens_per_row
        return ((layer * MOD_ROWS + row) * N_MOD + k, 0, 0)
    return pl.BlockSpec((1, 1, D_MODEL), imap)


def _qkv_kernel(x_ref, g_ref, sh_ref, sc_ref, w_ref, q_ref, k_ref, v_ref, ks_ref, vs_ref):
    h = _norm_mod(x_ref[...], g_ref[...], sh_ref[0], sc_ref[0]).astype(BF16)
    y = jnp.dot(h, w_ref[...], preferred_element_type=F32)
    q_ref[...] = (y[:, :D_MODEL] * (NA_HEAD_DIM ** -0.5)).astype(q_ref.dtype)
    k = y[:, D_MODEL:2 * D_MODEL]
    v = y[:, 2 * D_MODEL:]
    k_ref[...] = k.astype(k_ref.dtype)
    v_ref[...] = v.astype(v_ref.dtype)
    ks_ref[...] = pltpu.einshape("t(hd)->thd", k, h=NA_HEADS).reshape(ks_ref.shape)
    vs_ref[...] = pltpu.einshape("t(hd)->thd", v, h=NA_HEADS).reshape(vs_ref.shape)


def _na_qkv(x, g, mod, layer, row0, tokens_per_row, w_qkv, n_batch):
    t = x.shape[0]
    seq = t // n_batch
    nb = TM // seq
    tok = pl.BlockSpec((TM, D_MODEL), lambda i: (i, 0))
    state = pl.BlockSpec((nb, 1, seq, NA_HEADS, NA_HEAD_DIM), lambda i: (i, 0, 0, 0, 0))
    state_shape = jax.ShapeDtypeStruct((n_batch, 1, seq, NA_HEADS, NA_HEAD_DIM), F32)
    return pl.pallas_call(
        _qkv_kernel,
        grid=(t // TM,),
        in_specs=[tok, _const_spec((1, D_MODEL)),
                  _mod_spec(layer, 0, row0, tokens_per_row), _mod_spec(layer, 1, row0, tokens_per_row),
                  _const_spec((D_MODEL, 3 * D_MODEL))],
        out_specs=[tok, tok, tok, state, state],
        out_shape=[jax.ShapeDtypeStruct((t, D_MODEL), BF16)] * 3 + [state_shape, state_shape],
        compiler_params=_cparams(("arbitrary",)),
        name="na_qkv",
    )(x, g, mod, mod, w_qkv)


def _store_head_slots(vt, vt_ref):
    n = vt.shape[1]
    n_heads = vt_ref.shape[0] // LANES
    dim = vt.shape[0] // n_heads
    tail = jnp.where(lax.broadcasted_iota(jnp.int32, (LANES - dim, n), 0) == 0, 1.0, 0.0).astype(vt_ref.dtype)
    for hh in range(n_heads):
        vt_ref[hh * LANES:hh * LANES + dim, :] = vt[hh * dim:(hh + 1) * dim, :].astype(vt_ref.dtype)
        vt_ref[hh * LANES + dim:(hh + 1) * LANES, :] = tail


def _qkv_t_kernel(x_ref, g_ref, sh_ref, sc_ref, wqk_ref, wvt_ref, q_ref, k_ref, vt_ref):
    h = _norm_mod(x_ref[...], g_ref[...], sh_ref[0], sc_ref[0]).astype(BF16)
    y = jnp.dot(h, wqk_ref[...], preferred_element_type=F32)
    q_ref[...] = (y[:, :D_MODEL] * (NA_HEAD_DIM ** -0.5 * LOG2_E)).astype(q_ref.dtype)
    k_ref[...] = y[:, D_MODEL:].astype(k_ref.dtype)
    _store_head_slots(_qk(wvt_ref[...], h), vt_ref)


def _na_qkv_t(x, g, mod, layer, row0, tokens_per_row, w_qkv, w_vt):
    t = x.shape[0]
    tok = pl.BlockSpec((TM, D_MODEL), lambda i: (i, 0))
    slots = NA_HEADS * LANES
    return pl.pallas_call(
        _qkv_t_kernel,
        grid=(t // TM,),
        in_specs=[tok, _const_spec((1, D_MODEL)),
                  _mod_spec(layer, 0, row0, tokens_per_row), _mod_spec(layer, 1, row0, tokens_per_row),
                  pl.BlockSpec((D_MODEL, 2 * D_MODEL), lambda i: (0, 0)), _const_spec(w_vt.shape)],
        out_specs=[tok, tok, pl.BlockSpec((slots, TM), lambda i: (0, i))],
        out_shape=[jax.ShapeDtypeStruct((t, D_MODEL), BF16), jax.ShapeDtypeStruct((t, D_MODEL), BF16),
                   jax.ShapeDtypeStruct((slots, t), BF16)],
        compiler_params=_cparams(("arbitrary",)),
        name="na_qkv_t",
    )(x, g, mod, mod, w_qkv, w_vt)


def _softmax_pv(s_list, v_list):
    m = s_list[0].max(axis=-1, keepdims=True)
    for s in s_list[1:]:
        m = jnp.maximum(m, s.max(axis=-1, keepdims=True))
    l = None
    acc = None
    for s, v in zip(s_list, v_list):
        p = jnp.exp(s - m)
        ls = p.sum(axis=-1, keepdims=True)
        a = jnp.dot(p.astype(BF16), v, preferred_element_type=F32)
        l = ls if l is None else l + ls
        acc = a if acc is None else acc + a
    return acc * (1.0 / l)


def _qk(q, k):
    return lax.dot_general(q, k, (((1,), (1,)), ((), ())), preferred_element_type=F32)


def _head_masks():
    lane = lax.broadcasted_iota(jnp.int32, (1, LANES), 1)
    lo = lane < (LANES // 2)
    return lo, jnp.where(lo, 1.0, 0.0).astype(BF16), jnp.where(lo, 0.0, 1.0).astype(BF16)


def _attn_kernel(*refs, n_src, wide_qk, pairs, nb):
    q_ref, o_ref = refs[0], refs[-1]
    kv = refs[1:-1]
    lo, m0, m1 = _head_masks()
    wq = 2 * LANES if wide_qk else LANES
    tq = q_ref.shape[0] // nb
    for bb in range(nb):
        rq = slice(bb * tq, (bb + 1) * tq)
        rk = [slice(bb * (kv[2 * t].shape[0] // nb), (bb + 1) * (kv[2 * t].shape[0] // nb)) for t in range(n_src)]
        for p in range(pairs):
            q = q_ref[rq, p * wq:(p + 1) * wq]
            ks = [kv[2 * t][rk[t], p * wq:(p + 1) * wq].astype(BF16) for t in range(n_src)]
            vs = [kv[2 * t + 1][rk[t], p * LANES:(p + 1) * LANES].astype(BF16) for t in range(n_src)]
            outs = []
            for j in range(2):
                if wide_qk:
                    s_list = [_qk(q[:, j * LANES:(j + 1) * LANES], k[:, j * LANES:(j + 1) * LANES]) for k in ks]
                else:
                    s_list = [_qk(q * (m0 if j == 0 else m1), k) for k in ks]
                outs.append(_softmax_pv(s_list, vs))
            o_ref[rq, p * LANES:(p + 1) * LANES] = jnp.where(lo, outs[0], outs[1]).astype(o_ref.dtype)


def _attention(q, kv_srcs, n_batch, wide_qk, pairs, nb):
    wq = (2 * LANES if wide_qk else LANES) * pairs
    wv = LANES * pairs
    tq = q.shape[0] // n_batch * nb
    n_steps = D_MODEL // wv
    in_specs = [pl.BlockSpec((tq, wq), lambda b, p: (b, p))]
    args = [q]
    for k, v in kv_srcs:
        t_k = k.shape[0] // n_batch * nb
        in_specs.append(pl.BlockSpec((t_k, wq), lambda b, p: (b, p)))
        in_specs.append(pl.BlockSpec((t_k, wv), lambda b, p: (b, p)))
        args += [k, v]
    return pl.pallas_call(
        functools.partial(_attn_kernel, n_src=len(kv_srcs), wide_qk=wide_qk, pairs=pairs, nb=nb),
        grid=(n_batch // nb, n_steps),
        in_specs=in_specs,
        out_specs=pl.BlockSpec((tq, wv), lambda b, p: (b, p)),
        out_shape=jax.ShapeDtypeStruct((q.shape[0], D_MODEL), BF16),
        compiler_params=_cparams(("arbitrary", "arbitrary")),
        name="attention",
    )(*args)


def _col_max(s, chunk=64):
    acc = s[:chunk]
    for r in range(chunk, s.shape[0], chunk):
        acc = jnp.maximum(acc, s[r:r + chunk])
    return acc.max(axis=0, keepdims=True)


def _attn_t_kernel(q_ref, k1_ref, k2_ref, v1_ref, v2_ref, o_ref):
    n_heads = q_ref.shape[0] // LANES
    slots = [slice(j * LANES, (j + 1) * LANES) for j in range(n_heads)]
    def scores(sl):
        qt = q_ref[sl, :]
        return (jnp.dot(k1_ref[:, sl], qt, preferred_element_type=F32),
                jnp.dot(k2_ref[:, sl], qt, preferred_element_type=F32))

    nxt = scores(slots[0])
    outs = []
    for n, sl in enumerate(slots):
        s1, s2 = nxt
        if n + 1 < n_heads:
            nxt = scores(slots[n + 1])
        m = jnp.maximum(_col_max(s1), _col_max(s2))
        p1 = jnp.exp2(s1 - m).astype(BF16)
        p2 = jnp.exp2(s2 - m).astype(BF16)
        ot = (jnp.dot(v1_ref[sl, :], p1, preferred_element_type=F32)
              + jnp.dot(v2_ref[sl, :], p2, preferred_element_type=F32))
        outs.append(ot[:MLA_V_DIM] * (1.0 / ot[MLA_V_DIM:MLA_V_DIM + 1]))
    for j in range(0, n_heads, 2):
        pair = jnp.concatenate(outs[j:j + 2], axis=0).T
        o_ref[:, (j // 2) * LANES:(j // 2 + 1) * LANES] = pair.astype(o_ref.dtype)


def _attention_t(qt, k1, v1t, k2, v2t, n_batch, tq, heads):
    wq = heads * LANES
    wo = heads * MLA_V_DIM
    nq = qt.shape[1] // n_batch // tq
    t1 = k1.shape[0] // n_batch
    t2 = k2.shape[0] // n_batch
    return pl.pallas_call(
        _attn_t_kernel,
        grid=(n_batch, MLA_HEADS // heads, nq),
        in_specs=[pl.BlockSpec((wq, tq), lambda b, p, i: (p, b * nq + i)),
                  pl.BlockSpec((t1, wq), lambda b, p, i: (b, p)),
                  pl.BlockSpec((t2, wq), lambda b, p, i: (b, p)),
                  pl.BlockSpec((wq, t1), lambda b, p, i: (p, b)),
                  pl.BlockSpec((wq, t2), lambda b, p, i: (p, b))],
        out_specs=pl.BlockSpec((tq, wo), lambda b, p, i: (b * nq + i, p)),
        out_shape=jax.ShapeDtypeStruct((qt.shape[1], D_MODEL), BF16),
        compiler_params=_cparams(("arbitrary", "arbitrary", "arbitrary")),
        name="attention_t",
    )(qt, k1, k2, v1t, v2t)


def _na_group_layout(rows):
    n_groups = rows // NA_GROUP_ROWS
    kh = min(NA_KH, rows)
    bases, types, keys = [], [], {}
    for gi in range(n_groups):
        r0 = gi * NA_GROUP_ROWS
        base = int(np.clip(r0 - kh // 2, 0, rows - NA_WIN_ROWS))
        starts = tuple(int(np.clip(r - kh // 2, 0, rows - kh)) - base for r in range(r0, r0 + NA_GROUP_ROWS))
        key = (starts, r0 - base)
        if key not in keys:
            keys[key] = len(keys)
        bases.append(base)
        types.append(keys[key])
    return bases, types, list(keys.keys())


def _na_bias_tables(rpb, rows):
    kh = min(NA_KH, rows)
    _, _, type_keys = _na_group_layout(rows)
    qc = np.arange(GRID_W)
    kc = np.arange(GRID_W)
    col_start = np.clip(qc - NA_KW // 2, 0, GRID_W - NA_KW)
    col_valid = (kc[None, :] >= col_start[:, None]) & (kc[None, :] < col_start[:, None] + NA_KW)
    dx = kc[None, :] - qc[:, None] + (NA_KW - 1)
    onehot = ((dx[None] == np.arange(2 * NA_KW - 1)[:, None, None]) & col_valid[None]).astype(np.float32)
    band = jnp.einsum('hyd,dqk->hykq', rpb.astype(F32), onehot, precision=lax.Precision.HIGHEST)
    band = jnp.where(col_valid.T, band, NEG_INF)
    strip = band.reshape(NA_HEADS, (2 * NA_KH - 1) * GRID_W, GRID_W)
    strip2 = jnp.concatenate([strip, strip], axis=-1)
    layout = []
    for starts, r_off in type_keys:
        layout.append(tuple((starts[rq], starts[rq] - (rq + r_off) + (NA_KH - 1)) for rq in range(NA_GROUP_ROWS)))
    gq = NA_GROUP_ROWS * GRID_W
    gk = NA_WIN_ROWS * GRID_W
    return pl.pallas_call(
        functools.partial(_bias_table_kernel, layout=tuple(layout), kh=kh),
        grid=(NA_HEADS,),
        in_specs=[pl.BlockSpec((1,) + strip2.shape[1:], lambda h: (h, 0, 0))],
        out_specs=pl.BlockSpec((1, len(layout), gk, gq), lambda h: (h, 0, 0, 0)),
        out_shape=jax.ShapeDtypeStruct((NA_HEADS, len(layout), gk, gq), F32),
        compiler_params=_cparams(("arbitrary",)),
        name="na_bias_table",
    )(strip2)


def _bias_table_kernel(strip_ref, o_ref, *, layout, kh):
    lo = lax.broadcasted_iota(jnp.int32, (1, LANES), 1) < GRID_W
    neg = NEG_INF * LOG2_E

    def column(start, dy0):
        parts = []
        if start:
            parts.append(jnp.full((start * GRID_W, LANES), neg, F32))
        parts.append(strip_ref[0, dy0 * GRID_W:(dy0 + kh) * GRID_W, :] * LOG2_E)
        if NA_WIN_ROWS - kh - start:
            parts.append(jnp.full(((NA_WIN_ROWS - kh - start) * GRID_W, LANES), neg, F32))
        return jnp.concatenate(parts, axis=0)

    for ty, rows in enumerate(layout):
        for c in range(0, NA_GROUP_ROWS, 2):
            even, odd = column(*rows[c]), column(*rows[c + 1])
            o_ref[0, ty, :, c * GRID_W:(c + 2) * GRID_W] = jnp.where(lo, even, odd)


def _state_k_kernel(x_ref, o_ref):
    for h in range(NA_HEADS):
        o_ref[:, h * NA_HEAD_DIM:(h + 1) * NA_HEAD_DIM] = x_ref[0, 0, :, h, :].astype(o_ref.dtype)


def _state_vt_kernel(x_ref, o_ref, flat_ref):
    for h in range(NA_HEADS):
        flat_ref[:, h * NA_HEAD_DIM:(h + 1) * NA_HEAD_DIM] = x_ref[0, 0, :, h, :]
    _store_head_slots(flat_ref[...].T, o_ref)


def _state_k(x, layer):
    b, _, t, h, dh = x.shape
    return pl.pallas_call(
        _state_k_kernel,
        grid=(b,),
        in_specs=[pl.BlockSpec((1, 1, t, h, dh), lambda i: (i, layer, 0, 0, 0))],
        out_specs=pl.BlockSpec((t, h * dh), lambda i: (i, 0)),
        out_shape=jax.ShapeDtypeStruct((b * t, h * dh), BF16),
        compiler_params=_cparams(("arbitrary",)),
        name="state_k",
    )(x)


def _state_vt(x, layer):
    b, _, t, h, dh = x.shape
    return pl.pallas_call(
        _state_vt_kernel,
        grid=(b,),
        in_specs=[pl.BlockSpec((1, 1, t, h, dh), lambda i: (i, layer, 0, 0, 0))],
        out_specs=pl.BlockSpec((h * LANES, t), lambda i: (0, i)),
        out_shape=jax.ShapeDtypeStruct((h * LANES, b * t), BF16),
        scratch_shapes=[pltpu.VMEM((t, h * dh), F32)],
        compiler_params=_cparams(("arbitrary",)),
        name="state_vt",
    )(x)


def _na_latent_kernel(q_ref, k_ref, vt_ref, kc_ref, vct_ref, tab_ref, o_ref, *, bases, types):
    _, m0, m1 = _head_masks()
    kctx = kc_ref[...]
    gq = NA_GROUP_ROWS * GRID_W
    gk = NA_WIN_ROWS * GRID_W
    blocks = [(gi, base, ty, j) for gi, (base, ty) in enumerate(zip(bases, types)) for j in range(2)]

    def scores(gi, base, ty, j):
        qj = q_ref[gi * gq:(gi + 1) * gq, :] * (m0 if j == 0 else m1)
        kwin = k_ref[base * GRID_W:base * GRID_W + gk, :]
        return _qk(kwin, qj) + tab_ref[j, ty], _qk(kctx, qj)

    nxt = scores(*blocks[0])
    outs = []
    for n, (gi, base, ty, j) in enumerate(blocks):
        s_win, s_ctx = nxt
        if n + 1 < len(blocks):
            nxt = scores(*blocks[n + 1])
        m = jnp.maximum(_col_max(s_win), _col_max(s_ctx))
        p_win = jnp.exp2(s_win - m).astype(BF16)
        p_ctx = jnp.exp2(s_ctx - m).astype(BF16)
        sl = slice(j * LANES, (j + 1) * LANES)
        ot = (jnp.dot(vt_ref[sl, base * GRID_W:base * GRID_W + gk], p_win, preferred_element_type=F32)
              + jnp.dot(vct_ref[sl, :], p_ctx, preferred_element_type=F32))
        outs.append(ot[:NA_HEAD_DIM] * (1.0 / ot[NA_HEAD_DIM:NA_HEAD_DIM + 1]))
        if j == 1:
            o_ref[gi * gq:(gi + 1) * gq, :] = jnp.concatenate(outs, axis=0).T.astype(o_ref.dtype)
            outs = []


def _na_latent(q, k, vt, k_ctx, vt_ctx, tables, n_batch):
    t = q.shape[0] // n_batch
    rows = t // GRID_W
    t_ctx = k_ctx.shape[0] // n_batch
    bases, types, type_keys = _na_group_layout(rows)
    n_types = len(type_keys)
    gq = NA_GROUP_ROWS * GRID_W
    gk = NA_WIN_ROWS * GRID_W
    tok = pl.BlockSpec((t, LANES), lambda p, b: (b, p))
    ctx = pl.BlockSpec((t_ctx, LANES), lambda p, b: (b, p))
    return pl.pallas_call(
        functools.partial(_na_latent_kernel, bases=bases, types=types),
        grid=(D_MODEL // LANES, n_batch),
        in_specs=[tok, tok, pl.BlockSpec((2 * LANES, t), lambda p, b: (p, b)),
                  ctx, pl.BlockSpec((2 * LANES, t_ctx), lambda p, b: (p, b)),
                  pl.BlockSpec((2, n_types, gk, gq), lambda p, b: (p, 0, 0, 0))],
        out_specs=tok,
        out_shape=jax.ShapeDtypeStruct(q.shape, BF16),
        compiler_params=_cparams(("arbitrary", "arbitrary")),
        name="na_latent",
    )(q, k, vt, k_ctx, vt_ctx, tables)


def _rope_lane_tables(n_tokens):
    f32 = np.float32
    n_freq = MLA_ROPE_DIM // 4
    inv = f32(1.0) / (f32(ROPE_THETA) ** (np.arange(n_freq, dtype=f32) / f32(n_freq)))
    tt = np.arange(n_tokens)
    row = (tt // GRID_W).astype(f32)
    col = (tt % GRID_W).astype(f32)
    ang = np.concatenate([row[:, None] * inv, col[:, None] * inv], axis=-1).astype(f32)
    cos, sin = np.cos(ang).astype(f32), np.sin(ang).astype(f32)
    ones = np.ones((n_tokens, MLA_NOPE_DIM), f32)
    z16 = np.zeros((n_tokens, MLA_ROPE_DIM // 2), f32)
    z32 = np.zeros((n_tokens, LANES - MLA_QK_DIM), f32)
    z64 = np.zeros((n_tokens, MLA_NOPE_DIM), f32)
    c_tab = np.concatenate([ones, cos, cos, z32], axis=-1)
    s_dn = np.concatenate([z64, -sin, z16, z32], axis=-1)
    s_up = np.concatenate([z64, z16, sin, z32], axis=-1)
    return (jnp.asarray(c_tab), jnp.asarray(s_dn), jnp.asarray(s_up),
            jnp.asarray(np.ascontiguousarray(cos.T)), jnp.asarray(np.ascontiguousarray(sin.T)))


def _rope_slot(x, c_tab, s_dn, s_up):
    half = MLA_ROPE_DIM // 2
    return x * c_tab + pltpu.roll(x, LANES - half, 1) * s_dn + pltpu.roll(x, half, 1) * s_up


def _mla_kv(ckv_n, kr_slot, wuk_ref, wuv_ref, k_ref, v_ref, v_transposed, rows=slice(None)):
    c = ckv_n.astype(BF16)
    k_nope = jnp.dot(c, wuk_ref[...], preferred_element_type=F32)
    for h in range(MLA_HEADS):
        k_ref[rows, h * LANES:(h + 1) * LANES] = (k_nope[:, h * LANES:(h + 1) * LANES] + kr_slot).astype(k_ref.dtype)
    if v_transposed:
        _store_head_slots(_qk(wuv_ref[...], c), v_ref.at[:, rows])
    else:
        v_ref[rows, :] = jnp.dot(c, wuv_ref[...], preferred_element_type=F32).astype(v_ref.dtype)


def _mla_proj_kernel(*refs, rope, emit_state, q_scale):
    (x_ref, g_ref, sh_ref, sc_ref, win_ref, qg_ref, kvg_ref, wuq_ref, wuk_ref, wuv_ref) = refs[:10]
    pos = 10
    if rope:
        ct_ref, sd_ref, su_ref, cos_ref, sin_ref = refs[pos:pos + 5]
        pos += 5
    q_ref, k_ref, v_ref = refs[pos:pos + 3]
    pos += 3

    def front(rows):
        h = _norm_mod(x_ref[rows, :], g_ref[...], sh_ref[0], sc_ref[0]).astype(BF16)
        c = jnp.dot(h, win_ref[...], preferred_element_type=F32)
        cq = _rms(c[:, :MLA_Q_RANK], qg_ref[...]).astype(BF16)
        if rope:
            q = _qk(wuq_ref[...], cq) * q_scale
        else:
            q = jnp.dot(cq, wuq_ref[...], preferred_element_type=F32) * q_scale
        ckv = _rms(c[:, MLA_Q_RANK:MLA_Q_RANK + MLA_KV_RANK], kvg_ref[...])
        kr_chunk = c[:, MLA_Q_RANK + MLA_KV_RANK:]
        if emit_state:
            ckv_out, kr_out = refs[pos:pos + 2]
            ckv_out[rows, :] = ckv
            kr_out[rows, :] = kr_chunk[:, :MLA_ROPE_DIM]
        return q, ckv, pltpu.roll(kr_chunk, MLA_NOPE_DIM, 1)

    def back(rows, q, ckv, kr_slot):
        if rope:
            ct, sd, su = ct_ref[rows, :], sd_ref[rows, :], su_ref[rows, :]
            kr_slot = _rope_slot(kr_slot, ct, sd, su)
        _mla_kv(ckv, kr_slot, wuk_ref, wuv_ref, k_ref, v_ref, v_transposed=rope, rows=rows)
        if not rope:
            q_ref[rows, :] = q.astype(q_ref.dtype)
            return
        cs, sn = cos_ref[:, rows], sin_ref[:, rows]
        half = MLA_ROPE_DIM // 2
        for hh in range(MLA_HEADS):
            r0 = hh * LANES
            r1 = r0 + MLA_NOPE_DIM
            x1, x2 = q[r1:r1 + half], q[r1 + half:r1 + 2 * half]
            q_ref[r0:r1, rows] = q[r0:r1].astype(q_ref.dtype)
            q_ref[r1:r1 + half, rows] = (x1 * cs - x2 * sn).astype(q_ref.dtype)
            q_ref[r1 + half:r1 + 2 * half, rows] = (x2 * cs + x1 * sn).astype(q_ref.dtype)
            q_ref[r1 + 2 * half:r0 + LANES, rows] = q[r1 + 2 * half:r0 + LANES].astype(q_ref.dtype)

    tiles = [slice(r0, r0 + TM_SUB_MLA) for r0 in range(0, x_ref.shape[0], TM_SUB_MLA)]
    nxt = front(tiles[0])
    for n, rows in enumerate(tiles):
        cur = nxt
        if n + 1 < len(tiles):
            nxt = front(tiles[n + 1])
        back(rows, *cur)


def _mla_proj(x, g, mod, layer, row0, tokens_per_row, w, rope_tabs, emit_state):
    t = x.shape[0]
    win, qg, kvg, wuq, wuk, wuv = w
    tm = TM_MLA
    tok = lambda n: pl.BlockSpec((tm, n), lambda i: (i, 0))
    wq = MLA_HEADS * LANES
    in_specs = [tok(D_MODEL), _const_spec((1, D_MODEL)),
                _mod_spec(layer, 0, row0, tokens_per_row, tm), _mod_spec(layer, 1, row0, tokens_per_row, tm),
                _const_spec(win.shape), _const_spec(qg.shape), _const_spec(kvg.shape),
                _const_spec(wuq.shape), _const_spec(wuk.shape), _const_spec(wuv.shape)]
    args = [x, g, mod, mod, win, qg, kvg, wuq, wuk, wuv]
    rope = rope_tabs is not None
    if rope:
        n_tab = rope_tabs[0].shape[0] // tm
        in_specs += [pl.BlockSpec((tm, LANES), lambda i: (i % n_tab, 0)) for _ in range(3)]
        in_specs += [pl.BlockSpec((MLA_ROPE_DIM // 2, tm), lambda i: (0, i % n_tab)) for _ in range(2)]
        args += list(rope_tabs)
    if rope:
        q_spec, q_shape = pl.BlockSpec((wq, tm), lambda i: (0, i)), jax.ShapeDtypeStruct((wq, t), BF16)
        v_spec, v_shape = q_spec, q_shape
        q_scale = MLA_QK_DIM ** -0.5 * LOG2_E
    else:
        q_spec, q_shape = tok(wq), jax.ShapeDtypeStruct((t, wq), BF16)
        v_spec, v_shape = tok(D_MODEL), jax.ShapeDtypeStruct((t, D_MODEL), BF16)
        q_scale = MLA_QK_DIM ** -0.5
    out_specs = [q_spec, tok(wq), v_spec]
    out_shape = [q_shape, jax.ShapeDtypeStruct((t, wq), BF16), v_shape]
    if emit_state:
        out_specs += [tok(MLA_KV_RANK), tok(MLA_ROPE_DIM)]
        out_shape += [jax.ShapeDtypeStruct((t, MLA_KV_RANK), F32), jax.ShapeDtypeStruct((t, MLA_ROPE_DIM), F32)]
    return pl.pallas_call(
        functools.partial(_mla_proj_kernel, rope=rope, emit_state=emit_state, q_scale=q_scale),
        grid=(t // tm,),
        in_specs=in_specs, out_specs=out_specs, out_shape=out_shape,
        compiler_params=_cparams(("arbitrary",)),
        name="mla_proj",
    )(*args)


def _mla_state_kv_kernel(ckv_ref, kr_ref, wuk_ref, wuv_ref, k_ref, v_ref):
    kr_slot = pltpu.roll(kr_ref[...], MLA_NOPE_DIM, 1)
    _mla_kv(ckv_ref[...], kr_slot, wuk_ref, wuv_ref, k_ref, v_ref, v_transposed=True)


def _mla_state_kv(ckv, kr_pad, wuk, wuv_t):
    t = ckv.shape[0]
    wq = MLA_HEADS * LANES
    tok = lambda n: pl.BlockSpec((TM, n), lambda i: (i, 0))
    return pl.pallas_call(
        _mla_state_kv_kernel,
        grid=(t // TM,),
        in_specs=[tok(MLA_KV_RANK), tok(LANES), _const_spec(wuk.shape), _const_spec(wuv_t.shape)],
        out_specs=[tok(wq), pl.BlockSpec((wq, TM), lambda i: (0, i))],
        out_shape=[jax.ShapeDtypeStruct((t, wq), BF16), jax.ShapeDtypeStruct((wq, t), BF16)],
        compiler_params=_cparams(("arbitrary",)),
        name="mla_state_kv",
    )(ckv, kr_pad, wuk, wuv_t)


def _first_argmax_rows(v, row):
    top = v.max(axis=0, keepdims=True)
    idx = jnp.where(v == top, row, float(v.shape[0])).min(axis=0, keepdims=True)
    return top, idx


ROUTER_ROWS = 32
ROUTER_EXPERT_ROW = 8


def _route_t(lt):
    n = lt.shape[1]
    row8 = lax.broadcasted_iota(jnp.int32, (8, n), 0).astype(F32)
    row16 = lax.broadcasted_iota(jnp.int32, (N_EXPERTS, n), 0).astype(F32)
    is_g = row8 < N_GROUPS
    gl = jnp.where(is_g, lt[:8], -jnp.inf)
    gmax, gsel = _first_argmax_rows(gl, row8)
    p_group = 1.0 / jnp.where(is_g, jnp.exp(gl - gmax), 0.0).sum(axis=0, keepdims=True)
    first = gsel * EXPERTS_PER_GROUP
    e_all = lt[ROUTER_EXPERT_ROW:ROUTER_EXPERT_ROW + N_EXPERTS]
    el = jnp.where((row16 >= first) & (row16 < first + EXPERTS_PER_GROUP), e_all, -jnp.inf)
    t1, i1 = _first_argmax_rows(el, row16)
    t2, i2 = _first_argmax_rows(jnp.where(row16 == i1, -jnp.inf, el), row16)
    e2 = jnp.exp(t2 - t1)
    w1 = 1.0 / (1.0 + e2)
    gates = jnp.where(row16 == i1, w1 * p_group, jnp.where(row16 == i2, e2 * w1 * p_group, 0.0))
    rel = gates[:8] + gates[8:]
    rel = rel + pltpu.roll(rel, EXPERTS_PER_GROUP, 0)
    meta_t = jnp.where(row8 < EXPERTS_PER_GROUP, rel, jnp.where(row8 == META_GROUP_LANE, gsel, 0.0))
    onehot_t = jnp.where(row8 == gsel, 1.0, 0.0)
    return meta_t, onehot_t, row8, gsel


def _pack_halves(x):
    half = x.shape[1] // 2
    lo = lax.bitcast_convert_type(x[:, :half].astype(BF16).astype(F32), jnp.uint32)
    hi = lax.bitcast_convert_type(x[:, half:].astype(BF16).astype(F32), jnp.uint32)
    return (lo >> 16) | hi


def _unpack_halves(p):
    lo = lax.bitcast_convert_type(p << 16, F32)
    hi = lax.bitcast_convert_type(p & jnp.uint32(0xFFFF0000), F32)
    return lo, hi


ROW_WORDS = D_MODEL // 2 + LANES
META_GROUP_LANE = EXPERTS_PER_GROUP
META_RANK_LANE = EXPERTS_PER_GROUP + 1


def _proj_res_kernel(o_ref, x_ref, w_ref, gate_ref, g_ref, sh_ref, sc_ref, wrt_ref, brt_ref, tri_ref,
                     xn_ref, hpk_ref, metat_ref, cnt_ref):
    @pl.when(pl.program_id(0) == 0)
    def _():
        cnt_ref[...] = jnp.zeros(cnt_ref.shape, cnt_ref.dtype)

    sub = tri_ref.shape[0]
    cnt = cnt_ref[:, :1]
    tiles = [slice(r0, r0 + sub) for r0 in range(0, x_ref.shape[0], sub)]

    def out_proj(rows):
        return jnp.dot(o_ref[rows, :], w_ref[...], preferred_element_type=F32)

    y_next = out_proj(tiles[0])
    for n, rows in enumerate(tiles):
        y = y_next
        if n + 1 < len(tiles):
            y_next = out_proj(tiles[n + 1])
        xn = x_ref[rows, :] + gate_ref[0] * y
        xn_ref[rows, :] = xn
        h = _norm_mod(xn, g_ref[...], sh_ref[0], sc_ref[0])
        lt = _qk(wrt_ref[...], h.astype(BF16)) + brt_ref[...]
        meta_t, onehot_t, row8, gsel = _route_t(lt)
        prefix = jnp.dot(onehot_t.astype(BF16), tri_ref[...], preferred_element_type=F32)
        rank = jnp.where(row8 == gsel, prefix + cnt - 1.0, 0.0).sum(axis=0, keepdims=True)
        cnt = cnt + onehot_t.sum(axis=1, keepdims=True)
        meta_t = jnp.where(row8 == META_RANK_LANE, rank, meta_t)
        meta = jnp.concatenate([meta_t, jnp.zeros((LANES - 8, sub), F32)], axis=0).T
        metat_ref[:, rows] = meta_t
        hpk_ref[rows, :] = jnp.concatenate([_pack_halves(h), lax.bitcast_convert_type(meta, jnp.uint32)], axis=1)
    cnt_ref[...] = jnp.broadcast_to(cnt, cnt_ref.shape)


def _proj_res(o, x, w_o, g_ffn, mod, layer, row0, tokens_per_row, w_router_t, b_router_t):
    t = x.shape[0]
    tm = TM_PROJ
    tok = lambda n: pl.BlockSpec((tm, n), lambda i: (i, 0))
    ms = lambda k: _mod_spec(layer, k, row0, tokens_per_row, tm)
    tri = jnp.asarray(np.triu(np.ones((TM_SUB, TM_SUB), np.float32)), BF16)
    return pl.pallas_call(
        _proj_res_kernel,
        grid=(t // tm,),
        in_specs=[tok(D_MODEL), tok(D_MODEL), _const_spec(w_o.shape), ms(2), _const_spec((1, D_MODEL)), ms(3), ms(4),
                  _const_spec(w_router_t.shape), _const_spec(b_router_t.shape), _const_spec((TM_SUB, TM_SUB))],
        out_specs=[tok(D_MODEL), tok(ROW_WORDS), pl.BlockSpec((8, tm), lambda i: (0, i)), _const_spec((8, LANES))],
        out_shape=[jax.ShapeDtypeStruct((t, D_MODEL), F32), jax.ShapeDtypeStruct((t, ROW_WORDS), jnp.uint32),
                   jax.ShapeDtypeStruct((8, t), F32), jax.ShapeDtypeStruct((8, LANES), F32)],
        compiler_params=_cparams(("arbitrary",)),
        name="proj_res",
    )(o, x, w_o, mod, g_ffn, mod, mod, w_router_t, b_router_t, tri)


def _moe_plan(meta_t, cnt):
    t = meta_t.shape[1]
    n_tiles = t // TM + N_GROUPS
    group = meta_t[META_GROUP_LANE].astype(jnp.int32)
    rank = meta_t[META_RANK_LANE].astype(jnp.int32)
    counts = cnt[:N_GROUPS, 0].astype(jnp.int32)
    tiles = (counts + TM - 1) // TM
    tile_end = jnp.cumsum(tiles)
    tile_start = tile_end - tiles
    pos = rank + TM * jnp.sum(jnp.where(group[:, None] == jnp.arange(N_GROUPS)[None], tile_start[None], 0), axis=1)
    tile_group = jnp.minimum(jnp.sum(jnp.arange(n_tiles)[:, None] >= tile_end[None], axis=1), N_GROUPS - 1)
    pad_bounds = jnp.stack([tile_start * TM + counts, tile_end * TM], axis=1).reshape(-1)
    return (pos.astype(jnp.int32), tile_group.astype(jnp.int32), tile_end[-1:].astype(jnp.int32),
            pad_bounds.astype(jnp.int32))


def _gather_rows(srcs, idx_ref, base, dsts, n_rows):
    for r in range(n_rows):
        s = idx_ref[base + r]
        for src_ref, dst_ref in zip(srcs, dsts):
            dst_ref[r:r + 1, :] = src_ref[pl.ds(s, 1), :]


def _moe_kernel(pos_ref, tg_ref, nt_ref, pad_ref, hpk_hbm, wg_ref, wu_ref, wd_ref, ypk_ref,
                src_ref, hpk_ref, hbuf, hid_ref, load_sem):
    i = pl.program_id(0)
    n_tok = hpk_ref.shape[0]

    @pl.when(i == 0)
    def _():
        load = pltpu.make_async_copy(hpk_hbm, hpk_ref, load_sem)
        load.start()
        def clear(p, c):
            src_ref[p] = 0
            return c
        for g in range(N_GROUPS):
            lax.fori_loop(pad_ref[2 * g], pad_ref[2 * g + 1], clear, 0)

        def fill(t, c):
            src_ref[pos_ref[t]] = t
            return c
        lax.fori_loop(0, n_tok, fill, 0, unroll=8)
        load.wait()
        _gather_rows((hpk_ref,), src_ref, 0, (hbuf.at[0],), TM)

    @pl.when(i < nt_ref[0])
    def _():
        slot = i % 2
        row = hbuf[slot]
        lo, hi = _unpack_halves(row[:, :D_MODEL // 2])
        h = jnp.concatenate([lo.astype(BF16), hi.astype(BF16)], axis=1)
        gates = lax.bitcast_convert_type(row[:, D_MODEL // 2:], F32)
        nxt = jnp.minimum(i + 1, nt_ref[0] - 1) * TM
        _gather_rows((hpk_ref,), src_ref, nxt, (hbuf.at[1 - slot],), TM)
        f = D_FF_EXPERT
        for j in range(EXPERTS_PER_GROUP):
            gate_act = _silu(jnp.dot(h, wg_ref[0, j], preferred_element_type=F32))
            hid = gate_act * jnp.dot(h, wu_ref[0, j], preferred_element_type=F32) * gates[:, j:j + 1]
            hid_ref[:, j * f:(j + 1) * f] = hid.astype(BF16)
        ypk_ref[...] = _pack_halves(jnp.dot(hid_ref[...], wd_ref[0], preferred_element_type=F32))

    @pl.when(i >= nt_ref[0])
    def _():
        ypk_ref[...] = jnp.zeros(ypk_ref.shape, ypk_ref.dtype)


def _moe(hpk, pos, tile_group, n_used, pad_bounds, wg, wu, wd, layer):
    t = hpk.shape[0]
    n_tiles = t // TM + N_GROUPS
    f = D_FF_EXPERT
    g0 = layer * N_GROUPS
    grid_spec = pltpu.PrefetchScalarGridSpec(
        num_scalar_prefetch=4,
        grid=(n_tiles,),
        in_specs=[
            pl.BlockSpec(memory_space=pl.ANY),
            pl.BlockSpec((1, EXPERTS_PER_GROUP, D_MODEL, f), lambda i, pos, tg, nt, pad: (g0 + tg[i], 0, 0, 0)),
            pl.BlockSpec((1, EXPERTS_PER_GROUP, D_MODEL, f), lambda i, pos, tg, nt, pad: (g0 + tg[i], 0, 0, 0)),
            pl.BlockSpec((1, EXPERTS_PER_GROUP * f, D_MODEL), lambda i, pos, tg, nt, pad: (g0 + tg[i], 0, 0)),
        ],
        out_specs=pl.BlockSpec((TM, D_MODEL // 2), lambda i, *_: (i, 0)),
        scratch_shapes=[pltpu.SMEM((n_tiles * TM,), jnp.int32),
                        pltpu.VMEM(hpk.shape, hpk.dtype),
                        pltpu.VMEM((2, TM, ROW_WORDS), jnp.uint32),
                        pltpu.VMEM((TM, EXPERTS_PER_GROUP * f), BF16),
                        pltpu.SemaphoreType.DMA(())],
    )
    return pl.pallas_call(
        _moe_kernel,
        grid_spec=grid_spec,
        out_shape=jax.ShapeDtypeStruct((n_tiles * TM, D_MODEL // 2), jnp.uint32),
        compiler_params=_cparams(("arbitrary",)),
        name="moe",
    )(pos, tile_group, n_used, pad_bounds, hpk, wg, wu, wd)


def _combine_kernel(*refs, final_norm):
    pos_ref, ypk_ref, x_ref, gate_ref = refs[:4]
    o_ref, ybuf = refs[-2], refs[-1]
    _gather_rows((ypk_ref,), pos_ref, pl.program_id(0) * TM, (ybuf,), TM)
    lo, hi = _unpack_halves(ybuf[...])
    out = x_ref[...] + gate_ref[0] * jnp.concatenate([lo, hi], axis=1)
    if final_norm:
        out = _rms(out, refs[4][...])
    o_ref[...] = out


def _combine(ypk, pos, x, mod, layer, row0, tokens_per_row, final_g):
    t = x.shape[0]

    def mod_map(i, pos):
        row = row0 + (i * TM) // tokens_per_row
        return ((layer * MOD_ROWS + row) * N_MOD + 5, 0, 0)

    in_specs = [pl.BlockSpec(ypk.shape, lambda i, pos: (0, 0), pipeline_mode=pl.Buffered(1)),
                pl.BlockSpec((TM, D_MODEL), lambda i, pos: (i, 0)),
                pl.BlockSpec((1, 1, D_MODEL), mod_map)]
    args = [ypk, x, mod]
    if final_g is not None:
        in_specs.append(pl.BlockSpec((1, D_MODEL), lambda i, pos: (0, 0)))
        args.append(final_g)
    grid_spec = pltpu.PrefetchScalarGridSpec(
        num_scalar_prefetch=1,
        grid=(t // TM,),
        in_specs=in_specs,
        out_specs=pl.BlockSpec((TM, D_MODEL), lambda i, pos: (i, 0)),
        scratch_shapes=[pltpu.VMEM((TM, D_MODEL // 2), jnp.uint32)],
    )
    return pl.pallas_call(
        functools.partial(_combine_kernel, final_norm=final_g is not None),
        grid_spec=grid_spec,
        out_shape=jax.ShapeDtypeStruct((t, D_MODEL), F32),
        compiler_params=_cparams(("arbitrary",)),
        name="moe_combine",
    )(pos, *args)


def _pad_heads(w, used):
    k = w.shape[0]
    w = w.reshape(k, MLA_HEADS, used)
    return jnp.pad(w, ((0, 0), (0, 0), (0, LANES - used))).reshape(k, MLA_HEADS * LANES)


def kernel(x_prompt, x_sample, state_na_k, state_na_v, state_mla_ckv, state_mla_kr, c, c_ctx, ada_w, ada_b, norm_mix_g, norm_ffn_g, final_norm_g, na_w_qkv, na_w_o, na_rpb, mla_w_in, mla_q_norm_g, mla_kv_norm_g, mla_w_uq, mla_w_uk, mla_w_uv, mla_w_o, moe_w_group, moe_b_group, moe_w_expert, moe_b_expert, moe_w_gate, moe_w_up, moe_w_down):
    n_b, seq, d = x_prompt.shape
    dec_b, n_lat, _ = x_sample.shape
    past = state_na_k.shape[2]
    rows = n_lat // GRID_W
    assert d == D_MODEL and dec_b + 1 <= MOD_ROWS and n_lat % TM_PROJ == 0 and (n_b * seq) % TM_PROJ == 0

    cond8 = jnp.concatenate([c_ctx[None], c, jnp.zeros((MOD_ROWS - 1 - dec_b, d), F32)], axis=0)
    mod = _ada_mod(cond8, ada_w, ada_b)

    xp = x_prompt.reshape(n_b * seq, d)
    xs = x_sample.reshape(dec_b * n_lat, d)
    p_rows = (0, n_b * seq)
    s_rows = (1, n_lat)
    outs = {}

    f = D_FF_EXPERT
    wg = _to_bf16(moe_w_gate).reshape(DEPTH * N_GROUPS, EXPERTS_PER_GROUP, d, f)
    wu = _to_bf16(moe_w_up).reshape(DEPTH * N_GROUPS, EXPERTS_PER_GROUP, d, f)
    wd = _to_bf16(moe_w_down).reshape(DEPTH * N_GROUPS, EXPERTS_PER_GROUP * f, d)

    for i in range(DEPTH):
        j = i // 2
        g_mix = norm_mix_g[i][None]
        g_ffn = norm_ffn_g[i][None]
        if i % 2 == 0:
            w_qkv = _to_bf16(na_w_qkv[j])
            w_vt = _transposed_bf16(na_w_qkv[j], 2 * d)
            qp, kp, vp, outs['na_k'], outs['na_v'] = _na_qkv(xp, g_mix, mod, i, *p_rows, w_qkv, n_b)
            qs, ks, vs_t = _na_qkv_t(xs, g_mix, mod, i, *s_rows, w_qkv, w_vt)
            op = _attention(qp, [(kp, vp)], n_b, wide_qk=False, pairs=8, nb=4)
            tables = _na_bias_tables(na_rpb[j], rows)
            os_ = _na_latent(qs, ks, vs_t, _state_k(state_na_k, j), _state_vt(state_na_v, j), tables, dec_b)
            w_o = _to_bf16(na_w_o[j])
        else:
            pad_in = MLA_IN_PAD - mla_w_in.shape[-1]
            w_mla = (jnp.pad(mla_w_in[j], ((0, 0), (0, pad_in))).astype(BF16),
                     mla_q_norm_g[j][None], mla_kv_norm_g[j][None],
                     _pad_heads(mla_w_uq[j], MLA_QK_DIM).astype(BF16),
                     _pad_heads(mla_w_uk[j], MLA_NOPE_DIM).astype(BF16),
                     mla_w_uv[j].astype(BF16))
            wuv_t = _transposed_bf16(mla_w_uv[j])
            wuq_t = _transposed_bf16(_pad_heads(mla_w_uq[j], MLA_QK_DIM))
            w_mla_t = w_mla[:3] + (wuq_t, w_mla[4], wuv_t)
            qp, kp, vp, ckv_p, kr_p = _mla_proj(xp, g_mix, mod, i, *p_rows, w_mla, None, True)
            qs, ks, vs_t = _mla_proj(xs, g_mix, mod, i, *s_rows, w_mla_t, _rope_lane_tables(n_lat), False)
            outs['ckv'] = ckv_p.reshape(n_b, 1, seq, MLA_KV_RANK)
            outs['kr'] = kr_p.reshape(n_b, 1, seq, MLA_ROPE_DIM)
            kr_state = jnp.pad(state_mla_kr[:, j].reshape(dec_b * past, MLA_ROPE_DIM),
                               ((0, 0), (0, LANES - MLA_ROPE_DIM)))
            kc, vc_t = _mla_state_kv(state_mla_ckv[:, j].reshape(dec_b * past, MLA_KV_RANK), kr_state,
                                     w_mla[4], wuv_t)
            op = _attention(qp, [(kp, vp)], n_b, wide_qk=True, pairs=8, nb=4)
            os_ = _attention_t(qs, ks, vs_t, kc, vc_t, dec_b, 512, heads=8)
            w_o = _to_bf16(mla_w_o[j])

        pad_g = ((0, ROUTER_EXPERT_ROW - N_GROUPS), (0, 0))
        pad_e = ((0, ROUTER_ROWS - ROUTER_EXPERT_ROW - N_EXPERTS), (0, 0))
        w_router_t = jnp.concatenate([jnp.pad(moe_w_group[i].T, pad_g), jnp.pad(moe_w_expert[i].T, pad_e)],
                                     axis=0).astype(BF16)
        b_router_t = jnp.concatenate([jnp.pad(moe_b_group[i][:, None], pad_g),
                                      jnp.pad(moe_b_expert[i][:, None], pad_e)], axis=0)
        b_router_t = jnp.broadcast_to(b_router_t, (ROUTER_ROWS, TM_SUB))
        final_g = final_norm_g[None] if i == DEPTH - 1 else None
        new_x = []
        for o_att, x, rows_ in ((op, xp, p_rows), (os_, xs, s_rows)):
            xn, hpk, meta, cnt = _proj_res(o_att, x, w_o, g_ffn, mod, i, *rows_, w_router_t, b_router_t)
            pos, tile_group, n_used, pad_bounds = _moe_plan(meta, cnt)
            ypk = _moe(hpk, pos, tile_group, n_used, pad_bounds, wg, wu, wd, i)
            new_x.append(_combine(ypk, pos, xn, mod, i, *rows_, final_g))
        xp, xs = new_x

    return (xp.reshape(n_b, seq, d), xs.reshape(dec_b, n_lat, d),
            outs['na_k'], outs['na_v'], outs['ckv'], outs['kr'])
```

```python
import functools

import numpy as np
import jax
import jax.numpy as jnp
from jax import lax
from jax.experimental import pallas as pl
from jax.experimental.pallas import tpu as pltpu

F32 = jnp.float32
BF16 = jnp.bfloat16

D_MODEL = 1024
DEPTH = 2
GRID_W = 64
N_MOD = 6
EPS = 1e-6
NEG_INF = -1e30
NA_HEADS = 16
NA_HEAD_DIM = 64
NA_KH = 8
NA_KW = 16
NA_GROUP_ROWS = 4
NA_WIN_ROWS = 12
MLA_HEADS = 16
MLA_Q_RANK = 384
MLA_KV_RANK = 256
MLA_NOPE_DIM = 64
MLA_ROPE_DIM = 32
MLA_V_DIM = 64
MLA_QK_DIM = MLA_NOPE_DIM + MLA_ROPE_DIM
MLA_IN_PAD = 768
ROPE_THETA = 10000.0
LOG2_E = 1.4426950408889634
N_GROUPS = 4
EXPERTS_PER_GROUP = 4
N_EXPERTS = 16
D_FF_EXPERT = 256

LANES = 128
MOD_ROWS = 8
VMEM_LIMIT = 56 * 1024 * 1024
TM = 512
CAST_BLOCK_ELEMS = 512 * 1024
TM_MLA = 1024
TM_SUB_MLA = 512
TM_SUB = 512
TM_PROJ = 1024


def _cparams(sem):
    return pltpu.CompilerParams(dimension_semantics=sem, vmem_limit_bytes=VMEM_LIMIT)


def _const_spec(shape):
    nd = len(shape)
    return pl.BlockSpec(shape, lambda *_: (0,) * nd)


def _cast_kernel(x_ref, o_ref):
    o_ref[...] = x_ref[...].astype(o_ref.dtype)


def _to_bf16(x):
    cols = x.shape[-1]
    x2 = x.reshape(-1, cols)
    rows = x2.shape[0]
    block = rows
    while block * cols > CAST_BLOCK_ELEMS and block % 32 == 0:
        block //= 2
    out = pl.pallas_call(
        _cast_kernel,
        grid=(rows // block,),
        in_specs=[pl.BlockSpec((block, cols), lambda i: (i, 0))],
        out_specs=pl.BlockSpec((block, cols), lambda i: (i, 0)),
        out_shape=jax.ShapeDtypeStruct((rows, cols), BF16),
        compiler_params=_cparams(("arbitrary",)),
        name="to_bf16",
    )(x2)
    return out.reshape(x.shape)


def _transpose_kernel(x_ref, o_ref):
    o_ref[...] = x_ref[...].T.astype(o_ref.dtype)


def _transposed_bf16(x, col0=0, n_cols=None):
    rows = x.shape[0]
    n_cols = x.shape[1] - col0 if n_cols is None else n_cols
    bc = 2 * LANES
    assert col0 % bc == 0 and n_cols % bc == 0
    return pl.pallas_call(
        _transpose_kernel,
        grid=(n_cols // bc,),
        in_specs=[pl.BlockSpec((rows, bc), lambda i: (0, col0 // bc + i))],
        out_specs=pl.BlockSpec((bc, rows), lambda i: (i, 0)),
        out_shape=jax.ShapeDtypeStruct((n_cols, rows), BF16),
        compiler_params=_cparams(("arbitrary",)),
        name="transposed_bf16",
    )(x)


def _norm_mod(x, g, shift, scale):
    ms = jnp.mean(x * x, axis=-1, keepdims=True)
    y = x * lax.rsqrt(ms + EPS) * g
    return y * (1.0 + scale) + shift


def _rms(x, g):
    ms = jnp.mean(x * x, axis=-1, keepdims=True)
    return x * lax.rsqrt(ms + EPS) * g


def _silu(x):
    return x / (1.0 + jnp.exp(-x))


def _ada_kernel(cond_ref, w_ref, b_ref, o_ref):
    s = _silu(cond_ref[...]).astype(BF16)
    o_ref[0] = jnp.dot(s, w_ref[0].astype(BF16), preferred_element_type=F32) + b_ref[0]


def _ada_mod(cond8, ada_w, ada_b):
    tn = 1024
    n = N_MOD * D_MODEL
    out = pl.pallas_call(
        _ada_kernel,
        grid=(DEPTH, n // tn),
        in_specs=[
            _const_spec((MOD_ROWS, D_MODEL)),
            pl.BlockSpec((1, D_MODEL, tn), lambda l, j: (l, 0, j)),
            pl.BlockSpec((1, 1, tn), lambda l, j: (l, 0, j)),
        ],
        out_specs=pl.BlockSpec((1, MOD_ROWS, tn), lambda l, j: (l, 0, j)),
        out_shape=jax.ShapeDtypeStruct((DEPTH, MOD_ROWS, n), F32),
        compiler_params=_cparams(("arbitrary", "arbitrary")),
        name="ada_mod",
    )(cond8, ada_w, ada_b.reshape(DEPTH, 1, n))
    return out.reshape(DEPTH * MOD_ROWS * N_MOD, 1, D_MODEL)


def _mod_spec(layer, k, row0, tokens_per_row, tm=TM):
    def imap(i):
        row = row0 + (i * tm) // tokens_per_row
        return ((layer * MOD_ROWS + row) * N_MOD + k, 0, 0)
    return pl.BlockSpec((1, 1, D_MODEL), imap)


def _qkv_kernel(x_ref, g_ref, sh_ref, sc_ref, w_ref, q_ref, k_ref, v_ref, ks_ref, vs_ref):
    h = _norm_mod(x_ref[...], g_ref[...], sh_ref[0], sc_ref[0]).astype(BF16)
    y = jnp.dot(h, w_ref[...], preferred_element_type=F32)
    q_ref[...] = (y[:, :D_MODEL] * (NA_HEAD_DIM ** -0.5)).astype(q_ref.dtype)
    k = y[:, D_MODEL:2 * D_MODEL]
    v = y[:, 2 * D_MODEL:]
    k_ref[...] = k.astype(k_ref.dtype)
    v_ref[...] = v.astype(v_ref.dtype)
    ks_ref[...] = pltpu.einshape("t(hd)->thd", k, h=NA_HEADS).reshape(ks_ref.shape)
    vs_ref[...] = pltpu.einshape("t(hd)->thd", v, h=NA_HEADS).reshape(vs_ref.shape)


def _na_qkv(x, g, mod, layer, row0, tokens_per_row, w_qkv, n_batch):
    t = x.shape[0]
    seq = t // n_batch
    nb = TM // seq
    tok = pl.BlockSpec((TM, D_MODEL), lambda i: (i, 0))
    state = pl.BlockSpec((nb, 1, seq, NA_HEADS, NA_HEAD_DIM), lambda i: (i, 0, 0, 0, 0))
    state_shape = jax.ShapeDtypeStruct((n_batch, 1, seq, NA_HEADS, NA_HEAD_DIM), F32)
    return pl.pallas_call(
        _qkv_kernel,
        grid=(t // TM,),
        in_specs=[tok, _const_spec((1, D_MODEL)),
                  _mod_spec(layer, 0, row0, tokens_per_row), _mod_spec(layer, 1, row0, tokens_per_row),
                  _const_spec((D_MODEL, 3 * D_MODEL))],
        out_specs=[tok, tok, tok, state, state],
        out_shape=[jax.ShapeDtypeStruct((t, D_MODEL), BF16)] * 3 + [state_shape, state_shape],
        compiler_params=_cparams(("arbitrary",)),
        name="na_qkv",
    )(x, g, mod, mod, w_qkv)


def _store_head_slots(vt, vt_ref):
    n = vt.shape[1]
    n_heads = vt_ref.shape[0] // LANES
    dim = vt.shape[0] // n_heads
    tail = jnp.where(lax.broadcasted_iota(jnp.int32, (LANES - dim, n), 0) == 0, 1.0, 0.0).astype(vt_ref.dtype)
    for hh in range(n_heads):
        vt_ref[hh * LANES:hh * LANES + dim, :] = vt[hh * dim:(hh + 1) * dim, :].astype(vt_ref.dtype)
        vt_ref[hh * LANES + dim:(hh + 1) * LANES, :] = tail


def _qkv_t_kernel(x_ref, g_ref, sh_ref, sc_ref, wqk_ref, wvt_ref, q_ref, k_ref, vt_ref):
    h = _norm_mod(x_ref[...], g_ref[...], sh_ref[0], sc_ref[0]).astype(BF16)
    y = jnp.dot(h, wqk_ref[...], preferred_element_type=F32)
    q_ref[...] = (y[:, :D_MODEL] * (NA_HEAD_DIM ** -0.5 * LOG2_E)).astype(q_ref.dtype)
    k_ref[...] = y[:, D_MODEL:].astype(k_ref.dtype)
    _store_head_slots(_qk(wvt_ref[...], h), vt_ref)


def _na_qkv_t(x, g, mod, layer, row0, tokens_per_row, w_qkv, w_vt):
    t = x.shape[0]
    tm = TM_MLA
    tok = pl.BlockSpec((tm, D_MODEL), lambda i: (i, 0))
    slots = NA_HEADS * LANES
    return pl.pallas_call(
        _qkv_t_kernel,
        grid=(t // tm,),
        in_specs=[tok, _const_spec((1, D_MODEL)),
                  _mod_spec(layer, 0, row0, tokens_per_row, tm), _mod_spec(layer, 1, row0, tokens_per_row, tm),
                  pl.BlockSpec((D_MODEL, 2 * D_MODEL), lambda i: (0, 0)), _const_spec(w_vt.shape)],
        out_specs=[tok, tok, pl.BlockSpec((slots, tm), lambda i: (0, i))],
        out_shape=[jax.ShapeDtypeStruct((t, D_MODEL), BF16), jax.ShapeDtypeStruct((t, D_MODEL), BF16),
                   jax.ShapeDtypeStruct((slots, t), BF16)],
        compiler_params=_cparams(("arbitrary",)),
        name="na_qkv_t",
    )(x, g, mod, mod, w_qkv, w_vt)


def _softmax_pv(s_list, v_list):
    m = s_list[0].max(axis=-1, keepdims=True)
    for s in s_list[1:]:
        m = jnp.maximum(m, s.max(axis=-1, keepdims=True))
    l = None
    acc = None
    for s, v in zip(s_list, v_list):
        p = jnp.exp(s - m)
        ls = p.sum(axis=-1, keepdims=True)
        a = jnp.dot(p.astype(BF16), v, preferred_element_type=F32)
        l = ls if l is None else l + ls
        acc = a if acc is None else acc + a
    return acc * (1.0 / l)


def _qk(q, k):
    return lax.dot_general(q, k, (((1,), (1,)), ((), ())), preferred_element_type=F32)


def _head_masks():
    lane = lax.broadcasted_iota(jnp.int32, (1, LANES), 1)
    lo = lane < (LANES // 2)
    return lo, jnp.where(lo, 1.0, 0.0).astype(BF16), jnp.where(lo, 0.0, 1.0).astype(BF16)


def _attn_kernel(*refs, n_src, wide_qk, pairs, nb):
    q_ref, o_ref = refs[0], refs[-1]
    kv = refs[1:-1]
    lo, m0, m1 = _head_masks()
    wq = 2 * LANES if wide_qk else LANES
    tq = q_ref.shape[0] // nb
    for bb in range(nb):
        rq = slice(bb * tq, (bb + 1) * tq)
        rk = [slice(bb * (kv[2 * t].shape[0] // nb), (bb + 1) * (kv[2 * t].shape[0] // nb)) for t in range(n_src)]
        for p in range(pairs):
            q = q_ref[rq, p * wq:(p + 1) * wq]
            ks = [kv[2 * t][rk[t], p * wq:(p + 1) * wq].astype(BF16) for t in range(n_src)]
            vs = [kv[2 * t + 1][rk[t], p * LANES:(p + 1) * LANES].astype(BF16) for t in range(n_src)]
            outs = []
            for j in range(2):
                if wide_qk:
                    s_list = [_qk(q[:, j * LANES:(j + 1) * LANES], k[:, j * LANES:(j + 1) * LANES]) for k in ks]
                else:
                    s_list = [_qk(q * (m0 if j == 0 else m1), k) for k in ks]
                outs.append(_softmax_pv(s_list, vs))
            o_ref[rq, p * LANES:(p + 1) * LANES] = jnp.where(lo, outs[0], outs[1]).astype(o_ref.dtype)


def _attention(q, kv_srcs, n_batch, wide_qk, pairs, nb):
    wq = (2 * LANES if wide_qk else LANES) * pairs
    wv = LANES * pairs
    tq = q.shape[0] // n_batch * nb
    n_steps = D_MODEL // wv
    in_specs = [pl.BlockSpec((tq, wq), lambda b, p: (b, p))]
    args = [q]
    for k, v in kv_srcs:
        t_k = k.shape[0] // n_batch * nb
        in_specs.append(pl.BlockSpec((t_k, wq), lambda b, p: (b, p)))
        in_specs.append(pl.BlockSpec((t_k, wv), lambda b, p: (b, p)))
        args += [k, v]
    return pl.pallas_call(
        functools.partial(_attn_kernel, n_src=len(kv_srcs), wide_qk=wide_qk, pairs=pairs, nb=nb),
        grid=(n_batch // nb, n_steps),
        in_specs=in_specs,
        out_specs=pl.BlockSpec((tq, wv), lambda b, p: (b, p)),
        out_shape=jax.ShapeDtypeStruct((q.shape[0], D_MODEL), BF16),
        compiler_params=_cparams(("arbitrary", "arbitrary")),
        name="attention",
    )(*args)


def _col_max(s, chunk=64):
    acc = s[:chunk]
    for r in range(chunk, s.shape[0], chunk):
        acc = jnp.maximum(acc, s[r:r + chunk])
    return acc.max(axis=0, keepdims=True)


def _attn_t_kernel(q_ref, k1_ref, k2_ref, v1_ref, v2_ref, o_ref):
    n_heads = q_ref.shape[0] // LANES
    slots = [slice(j * LANES, (j + 1) * LANES) for j in range(n_heads)]
    def scores(sl):
        qt = q_ref[sl, :]
        return (jnp.dot(k1_ref[:, sl], qt, preferred_element_type=F32),
                jnp.dot(k2_ref[:, sl], qt, preferred_element_type=F32))

    nxt = scores(slots[0])
    outs = []
    for n, sl in enumerate(slots):
        s1, s2 = nxt
        if n + 1 < n_heads:
            nxt = scores(slots[n + 1])
        m = jnp.maximum(_col_max(s1), _col_max(s2))
        p1 = jnp.exp2(s1 - m).astype(BF16)
        p2 = jnp.exp2(s2 - m).astype(BF16)
        ot = (jnp.dot(v1_ref[sl, :], p1, preferred_element_type=F32)
              + jnp.dot(v2_ref[sl, :], p2, preferred_element_type=F32))
        outs.append(ot[:MLA_V_DIM] * (1.0 / ot[MLA_V_DIM:MLA_V_DIM + 1]))
    for j in range(0, n_heads, 2):
        pair = jnp.concatenate(outs[j:j + 2], axis=0).T
        o_ref[:, (j // 2) * LANES:(j // 2 + 1) * LANES] = pair.astype(o_ref.dtype)


def _attention_t(qt, k1, v1t, k2, v2t, n_batch, tq, heads):
    wq = heads * LANES
    wo = heads * MLA_V_DIM
    nq = qt.shape[1] // n_batch // tq
    t1 = k1.shape[0] // n_batch
    t2 = k2.shape[0] // n_batch
    return pl.pallas_call(
        _attn_t_kernel,
        grid=(n_batch, MLA_HEADS // heads, nq),
        in_specs=[pl.BlockSpec((wq, tq), lambda b, p, i: (p, b * nq + i)),
                  pl.BlockSpec((t1, wq), lambda b, p, i: (b, p)),
                  pl.BlockSpec((t2, wq), lambda b, p, i: (b, p)),
                  pl.BlockSpec((wq, t1), lambda b, p, i: (p, b)),
                  pl.BlockSpec((wq, t2), lambda b, p, i: (p, b))],
        out_specs=pl.BlockSpec((tq, wo), lambda b, p, i: (b * nq + i, p)),
        out_shape=jax.ShapeDtypeStruct((qt.shape[1], D_MODEL), BF16),
        compiler_params=_cparams(("arbitrary", "arbitrary", "arbitrary")),
        name="attention_t",
    )(qt, k1, k2, v1t, v2t)


def _na_group_layout(rows):
    n_groups = rows // NA_GROUP_ROWS
    kh = min(NA_KH, rows)
    bases, types, keys = [], [], {}
    for gi in range(n_groups):
        r0 = gi * NA_GROUP_ROWS
        base = int(np.clip(r0 - kh // 2, 0, rows - NA_WIN_ROWS))
        starts = tuple(int(np.clip(r - kh // 2, 0, rows - kh)) - base for r in range(r0, r0 + NA_GROUP_ROWS))
        key = (starts, r0 - base)
        if key not in keys:
            keys[key] = len(keys)
        bases.append(base)
        types.append(keys[key])
    return bases, types, list(keys.keys())


def _na_bias_tables(rpb, rows):
    kh = min(NA_KH, rows)
    _, _, type_keys = _na_group_layout(rows)
    qc = np.arange(GRID_W)
    kc = np.arange(GRID_W)
    col_start = np.clip(qc - NA_KW // 2, 0, GRID_W - NA_KW)
    col_valid = (kc[None, :] >= col_start[:, None]) & (kc[None, :] < col_start[:, None] + NA_KW)
    dx = kc[None, :] - qc[:, None] + (NA_KW - 1)
    onehot = ((dx[None] == np.arange(2 * NA_KW - 1)[:, None, None]) & col_valid[None]).astype(np.float32)
    band = jnp.einsum('hyd,dqk->hykq', rpb.astype(F32), onehot, precision=lax.Precision.HIGHEST)
    band = jnp.where(col_valid.T, band, NEG_INF)
    strip = band.reshape(NA_HEADS, (2 * NA_KH - 1) * GRID_W, GRID_W)
    strip2 = jnp.concatenate([strip, strip], axis=-1)
    layout = []
    for starts, r_off in type_keys:
        layout.append(tuple((starts[rq], starts[rq] - (rq + r_off) + (NA_KH - 1)) for rq in range(NA_GROUP_ROWS)))
    gq = NA_GROUP_ROWS * GRID_W
    gk = NA_WIN_ROWS * GRID_W
    return pl.pallas_call(
        functools.partial(_bias_table_kernel, layout=tuple(layout), kh=kh),
        grid=(NA_HEADS,),
        in_specs=[pl.BlockSpec((1,) + strip2.shape[1:], lambda h: (h, 0, 0))],
        out_specs=pl.BlockSpec((1, len(layout), gk, gq), lambda h: (h, 0, 0, 0)),
        out_shape=jax.ShapeDtypeStruct((NA_HEADS, len(layout), gk, gq), F32),
        compiler_params=_cparams(("arbitrary",)),
        name="na_bias_table",
    )(strip2)


def _bias_table_kernel(strip_ref, o_ref, *, layout, kh):
    lo = lax.broadcasted_iota(jnp.int32, (1, LANES), 1) < GRID_W
    neg = NEG_INF * LOG2_E

    def column(start, dy0):
        parts = []
        if start:
            parts.append(jnp.full((start * GRID_W, LANES), neg, F32))
        parts.append(strip_ref[0, dy0 * GRID_W:(dy0 + kh) * GRID_W, :] * LOG2_E)
        if NA_WIN_ROWS - kh - start:
            parts.append(jnp.full(((NA_WIN_ROWS - kh - start) * GRID_W, LANES), neg, F32))
        return jnp.concatenate(parts, axis=0)

    for ty, rows in enumerate(layout):
        for c in range(0, NA_GROUP_ROWS, 2):
            even, odd = column(*rows[c]), column(*rows[c + 1])
            o_ref[0, ty, :, c * GRID_W:(c + 2) * GRID_W] = jnp.where(lo, even, odd)


def _state_k_kernel(x_ref, o_ref):
    for h in range(NA_HEADS):
        o_ref[:, h * NA_HEAD_DIM:(h + 1) * NA_HEAD_DIM] = x_ref[0, 0, :, h, :].astype(o_ref.dtype)


def _state_vt_kernel(x_ref, o_ref, flat_ref):
    for h in range(NA_HEADS):
        flat_ref[:, h * NA_HEAD_DIM:(h + 1) * NA_HEAD_DIM] = x_ref[0, 0, :, h, :]
    _store_head_slots(flat_ref[...].T, o_ref)


def _state_k(x, layer):
    b, _, t, h, dh = x.shape
    return pl.pallas_call(
        _state_k_kernel,
        grid=(b,),
        in_specs=[pl.BlockSpec((1, 1, t, h, dh), lambda i: (i, layer, 0, 0, 0))],
        out_specs=pl.BlockSpec((t, h * dh), lambda i: (i, 0)),
        out_shape=jax.ShapeDtypeStruct((b * t, h * dh), BF16),
        compiler_params=_cparams(("arbitrary",)),
        name="state_k",
    )(x)


def _state_vt(x, layer):
    b, _, t, h, dh = x.shape
    return pl.pallas_call(
        _state_vt_kernel,
        grid=(b,),
        in_specs=[pl.BlockSpec((1, 1, t, h, dh), lambda i: (i, layer, 0, 0, 0))],
        out_specs=pl.BlockSpec((h * LANES, t), lambda i: (0, i)),
        out_shape=jax.ShapeDtypeStruct((h * LANES, b * t), BF16),
        scratch_shapes=[pltpu.VMEM((t, h * dh), F32)],
        compiler_params=_cparams(("arbitrary",)),
        name="state_vt",
    )(x)


def _na_latent_kernel(q_ref, k_ref, vt_ref, kc_ref, vct_ref, tab_ref, o_ref, *, bases, types):
    _, m0, m1 = _head_masks()
    kctx = kc_ref[...]
    gq = NA_GROUP_ROWS * GRID_W
    gk = NA_WIN_ROWS * GRID_W
    blocks = [(gi, base, ty, j) for gi, (base, ty) in enumerate(zip(bases, types)) for j in range(2)]

    def scores(gi, base, ty, j):
        qj = q_ref[gi * gq:(gi + 1) * gq, :] * (m0 if j == 0 else m1)
        kwin = k_ref[base * GRID_W:base * GRID_W + gk, :]
        return _qk(kwin, qj) + tab_ref[j, ty], _qk(kctx, qj)

    nxt = scores(*blocks[0])
    outs = []
    for n, (gi, base, ty, j) in enumerate(blocks):
        s_win, s_ctx = nxt
        if n + 1 < len(blocks):
            nxt = scores(*blocks[n + 1])
        m = jnp.maximum(_col_max(s_win), _col_max(s_ctx))
        p_win = jnp.exp2(s_win - m).astype(BF16)
        p_ctx = jnp.exp2(s_ctx - m).astype(BF16)
        sl = slice(j * LANES, (j + 1) * LANES)
        ot = (jnp.dot(vt_ref[sl, base * GRID_W:base * GRID_W + gk], p_win, preferred_element_type=F32)
              + jnp.dot(vct_ref[sl, :], p_ctx, preferred_element_type=F32))
        outs.append(ot[:NA_HEAD_DIM] * (1.0 / ot[NA_HEAD_DIM:NA_HEAD_DIM + 1]))
        if j == 1:
            o_ref[gi * gq:(gi + 1) * gq, :] = jnp.concatenate(outs, axis=0).T.astype(o_ref.dtype)
            outs = []


def _na_latent(q, k, vt, k_ctx, vt_ctx, tables, n_batch):
    t = q.shape[0] // n_batch
    rows = t // GRID_W
    t_ctx = k_ctx.shape[0] // n_batch
    bases, types, type_keys = _na_group_layout(rows)
    n_types = len(type_keys)
    gq = NA_GROUP_ROWS * GRID_W
    gk = NA_WIN_ROWS * GRID_W
    tok = pl.BlockSpec((t, LANES), lambda p, b: (b, p))
    ctx = pl.BlockSpec((t_ctx, LANES), lambda p, b: (b, p))
    return pl.pallas_call(
        functools.partial(_na_latent_kernel, bases=bases, types=types),
        grid=(D_MODEL // LANES, n_batch),
        in_specs=[tok, tok, pl.BlockSpec((2 * LANES, t), lambda p, b: (p, b)),
                  ctx, pl.BlockSpec((2 * LANES, t_ctx), lambda p, b: (p, b)),
                  pl.BlockSpec((2, n_types, gk, gq), lambda p, b: (p, 0, 0, 0))],
        out_specs=tok,
        out_shape=jax.ShapeDtypeStruct(q.shape, BF16),
        compiler_params=_cparams(("arbitrary", "arbitrary")),
        name="na_latent",
    )(q, k, vt, k_ctx, vt_ctx, tables)


def _rope_lane_tables(n_tokens):
    f32 = np.float32
    n_freq = MLA_ROPE_DIM // 4
    inv = f32(1.0) / (f32(ROPE_THETA) ** (np.arange(n_freq, dtype=f32) / f32(n_freq)))
    tt = np.arange(n_tokens)
    row = (tt // GRID_W).astype(f32)
    col = (tt % GRID_W).astype(f32)
    ang = np.concatenate([row[:, None] * inv, col[:, None] * inv], axis=-1).astype(f32)
    cos, sin = np.cos(ang).astype(f32), np.sin(ang).astype(f32)
    ones = np.ones((n_tokens, MLA_NOPE_DIM), f32)
    z16 = np.zeros((n_tokens, MLA_ROPE_DIM // 2), f32)
    z32 = np.zeros((n_tokens, LANES - MLA_QK_DIM), f32)
    z64 = np.zeros((n_tokens, MLA_NOPE_DIM), f32)
    c_tab = np.concatenate([ones, cos, cos, z32], axis=-1)
    s_dn = np.concatenate([z64, -sin, z16, z32], axis=-1)
    s_up = np.concatenate([z64, z16, sin, z32], axis=-1)
    return (jnp.asarray(c_tab), jnp.asarray(s_dn), jnp.asarray(s_up),
            jnp.asarray(np.ascontiguousarray(cos.T)), jnp.asarray(np.ascontiguousarray(sin.T)))


def _rope_slot(x, c_tab, s_dn, s_up):
    half = MLA_ROPE_DIM // 2
    return x * c_tab + pltpu.roll(x, LANES - half, 1) * s_dn + pltpu.roll(x, half, 1) * s_up


def _mla_kv(ckv_n, kr_slot, wuk_ref, wuv_ref, k_ref, v_ref, v_transposed, rows=slice(None)):
    c = ckv_n.astype(BF16)
    k_nope = jnp.dot(c, wuk_ref[...], preferred_element_type=F32)
    for h in range(MLA_HEADS):
        k_ref[rows, h * LANES:(h + 1) * LANES] = (k_nope[:, h * LANES:(h + 1) * LANES] + kr_slot).astype(k_ref.dtype)
    if v_transposed:
        _store_head_slots(_qk(wuv_ref[...], c), v_ref.at[:, rows])
    else:
        v_ref[rows, :] = jnp.dot(c, wuv_ref[...], preferred_element_type=F32).astype(v_ref.dtype)


def _mla_proj_kernel(*refs, rope, emit_state, q_scale):
    (x_ref, g_ref, sh_ref, sc_ref, win_ref, qg_ref, kvg_ref, wuq_ref, wuk_ref, wuv_ref) = refs[:10]
    pos = 10
    if rope:
        ct_ref, sd_ref, su_ref, cos_ref, sin_ref = refs[pos:pos + 5]
        pos += 5
    q_ref, k_ref, v_ref = refs[pos:pos + 3]
    pos += 3

    def front(rows):
        h = _norm_mod(x_ref[rows, :], g_ref[...], sh_ref[0], sc_ref[0]).astype(BF16)
        c = jnp.dot(h, win_ref[...], preferred_element_type=F32)
        cq = _rms(c[:, :MLA_Q_RANK], qg_ref[...]).astype(BF16)
        if rope:
            q = _qk(wuq_ref[...], cq) * q_scale
        else:
            q = jnp.dot(cq, wuq_ref[...], preferred_element_type=F32) * q_scale
        ckv = _rms(c[:, MLA_Q_RANK:MLA_Q_RANK + MLA_KV_RANK], kvg_ref[...])
        kr_chunk = c[:, MLA_Q_RANK + MLA_KV_RANK:]
        if emit_state:
            ckv_out, kr_out = refs[pos:pos + 2]
            ckv_out[rows, :] = ckv
            kr_out[rows, :] = kr_chunk[:, :MLA_ROPE_DIM]
        return q, ckv, pltpu.roll(kr_chunk, MLA_NOPE_DIM, 1)

    def back(rows, q, ckv, kr_slot):
        if rope:
            ct, sd, su = ct_ref[rows, :], sd_ref[rows, :], su_ref[rows, :]
            kr_slot = _rope_slot(kr_slot, ct, sd, su)
        _mla_kv(ckv, kr_slot, wuk_ref, wuv_ref, k_ref, v_ref, v_transposed=rope, rows=rows)
        if not rope:
            q_ref[rows, :] = q.astype(q_ref.dtype)
            return
        cs, sn = cos_ref[:, rows], sin_ref[:, rows]
        half = MLA_ROPE_DIM // 2
        for hh in range(MLA_HEADS):
            r0 = hh * LANES
            r1 = r0 + MLA_NOPE_DIM
            x1, x2 = q[r1:r1 + half], q[r1 + half:r1 + 2 * half]
            q_ref[r0:r1, rows] = q[r0:r1].astype(q_ref.dtype)
            q_ref[r1:r1 + half, rows] = (x1 * cs - x2 * sn).astype(q_ref.dtype)
            q_ref[r1 + half:r1 + 2 * half, rows] = (x2 * cs + x1 * sn).astype(q_ref.dtype)
            q_ref[r1 + 2 * half:r0 + LANES, rows] = q[r1 + 2 * half:r0 + LANES].astype(q_ref.dtype)

    tiles = [slice(r0, r0 + TM_SUB_MLA) for r0 in range(0, x_ref.shape[0], TM_SUB_MLA)]
    nxt = front(tiles[0])
    for n, rows in enumerate(tiles):
        cur = nxt
        if n + 1 < len(tiles):
            nxt = front(tiles[n + 1])
        back(rows, *cur)


def _mla_proj(x, g, mod, layer, row0, tokens_per_row, w, rope_tabs, emit_state):
    t = x.shape[0]
    win, qg, kvg, wuq, wuk, wuv = w
    tm = TM_MLA
    tok = lambda n: pl.BlockSpec((tm, n), lambda i: (i, 0))
    wq = MLA_HEADS * LANES
    in_specs = [tok(D_MODEL), _const_spec((1, D_MODEL)),
                _mod_spec(layer, 0, row0, tokens_per_row, tm), _mod_spec(layer, 1, row0, tokens_per_row, tm),
                _const_spec(win.shape), _const_spec(qg.shape), _const_spec(kvg.shape),
                _const_spec(wuq.shape), _const_spec(wuk.shape), _const_spec(wuv.shape)]
    args = [x, g, mod, mod, win, qg, kvg, wuq, wuk, wuv]
    rope = rope_tabs is not None
    if rope:
        n_tab = rope_tabs[0].shape[0] // tm
        in_specs += [pl.BlockSpec((tm, LANES), lambda i: (i % n_tab, 0)) for _ in range(3)]
        in_specs += [pl.BlockSpec((MLA_ROPE_DIM // 2, tm), lambda i: (0, i % n_tab)) for _ in range(2)]
        args += list(rope_tabs)
    if rope:
        q_spec, q_shape = pl.BlockSpec((wq, tm), lambda i: (0, i)), jax.ShapeDtypeStruct((wq, t), BF16)
        v_spec, v_shape = q_spec, q_shape
        q_scale = MLA_QK_DIM ** -0.5 * LOG2_E
    else:
        q_spec, q_shape = tok(wq), jax.ShapeDtypeStruct((t, wq), BF16)
        v_spec, v_shape = tok(D_MODEL), jax.ShapeDtypeStruct((t, D_MODEL), BF16)
        q_scale = MLA_QK_DIM ** -0.5
    out_specs = [q_spec, tok(wq), v_spec]
    out_shape = [q_shape, jax.ShapeDtypeStruct((t, wq), BF16), v_shape]
    if emit_state:
        out_specs += [tok(MLA_KV_RANK), tok(MLA_ROPE_DIM)]
        out_shape += [jax.ShapeDtypeStruct((t, MLA_KV_RANK), F32), jax.ShapeDtypeStruct((t, MLA_ROPE_DIM), F32)]
    return pl.pallas_call(
        functools.partial(_mla_proj_kernel, rope=rope, emit_state=emit_state, q_scale=q_scale),
        grid=(t // tm,),
        in_specs=in_specs, out_specs=out_specs, out_shape=out_shape,
        compiler_params=_cparams(("arbitrary",)),
        name="mla_proj",
    )(*args)


def _mla_state_kv_kernel(ckv_ref, kr_ref, wuk_ref, wuv_ref, k_ref, v_ref):
    kr_slot = pltpu.roll(kr_ref[...], MLA_NOPE_DIM, 1)
    _mla_kv(ckv_ref[...], kr_slot, wuk_ref, wuv_ref, k_ref, v_ref, v_transposed=True)


def _mla_state_kv(ckv, kr_pad, wuk, wuv_t):
    t = ckv.shape[0]
    wq = MLA_HEADS * LANES
    tok = lambda n: pl.BlockSpec((TM, n), lambda i: (i, 0))
    return pl.pallas_call(
        _mla_state_kv_kernel,
        grid=(t // TM,),
        in_specs=[tok(MLA_KV_RANK), tok(LANES), _const_spec(wuk.shape), _const_spec(wuv_t.shape)],
        out_specs=[tok(wq), pl.BlockSpec((wq, TM), lambda i: (0, i))],
        out_shape=[jax.ShapeDtypeStruct((t, wq), BF16), jax.ShapeDtypeStruct((wq, t), BF16)],
        compiler_params=_cparams(("arbitrary",)),
        name="mla_state_kv",
    )(ckv, kr_pad, wuk, wuv_t)


def _first_argmax_rows(v, row):
    top = v.max(axis=0, keepdims=True)
    idx = jnp.where(v == top, row, float(v.shape[0])).min(axis=0, keepdims=True)
    return top, idx


ROUTER_ROWS = 32
ROUTER_EXPERT_ROW = 8


def _route_t(lt):
    n = lt.shape[1]
    row8 = lax.broadcasted_iota(jnp.int32, (8, n), 0).astype(F32)
    row16 = lax.broadcasted_iota(jnp.int32, (N_EXPERTS, n), 0).astype(F32)
    is_g = row8 < N_GROUPS
    gl = jnp.where(is_g, lt[:8], -jnp.inf)
    gmax, gsel = _first_argmax_rows(gl, row8)
    p_group = 1.0 / jnp.where(is_g, jnp.exp(gl - gmax), 0.0).sum(axis=0, keepdims=True)
    first = gsel * EXPERTS_PER_GROUP
    e_all = lt[ROUTER_EXPERT_ROW:ROUTER_EXPERT_ROW + N_EXPERTS]
    el = jnp.where((row16 >= first) & (row16 < first + EXPERTS_PER_GROUP), e_all, -jnp.inf)
    t1, i1 = _first_argmax_rows(el, row16)
    t2, i2 = _first_argmax_rows(jnp.where(row16 == i1, -jnp.inf, el), row16)
    e2 = jnp.exp(t2 - t1)
    w1 = 1.0 / (1.0 + e2)
    gates = jnp.where(row16 == i1, w1 * p_group, jnp.where(row16 == i2, e2 * w1 * p_group, 0.0))
    rel = gates[:8] + gates[8:]
    rel = rel + pltpu.roll(rel, EXPERTS_PER_GROUP, 0)
    meta_t = jnp.where(row8 < EXPERTS_PER_GROUP, rel, jnp.where(row8 == META_GROUP_LANE, gsel, 0.0))
    onehot_t = jnp.where(row8 == gsel, 1.0, 0.0)
    return meta_t, onehot_t, row8, gsel


def _pack_halves(x):
    half = x.shape[1] // 2
    lo = lax.bitcast_convert_type(x[:, :half].astype(BF16).astype(F32), jnp.uint32)
    hi = lax.bitcast_convert_type(x[:, half:].astype(BF16).astype(F32), jnp.uint32)
    return (lo >> 16) | hi


def _unpack_halves(p):
    lo = lax.bitcast_convert_type(p << 16, F32)
    hi = lax.bitcast_convert_type(p & jnp.uint32(0xFFFF0000), F32)
    return lo, hi


ROW_WORDS = D_MODEL // 2 + LANES
META_GROUP_LANE = EXPERTS_PER_GROUP
META_RANK_LANE = EXPERTS_PER_GROUP + 1


def _proj_res_kernel(o_ref, x_ref, w_ref, gate_ref, g_ref, sh_ref, sc_ref, wrt_ref, brt_ref, tri_ref,
                     xn_ref, hpk_ref, metat_ref, cnt_ref):
    @pl.when(pl.program_id(0) == 0)
    def _():
        cnt_ref[...] = jnp.zeros(cnt_ref.shape, cnt_ref.dtype)

    sub = tri_ref.shape[0]
    cnt = cnt_ref[:, :1]
    tiles = [slice(r0, r0 + sub) for r0 in range(0, x_ref.shape[0], sub)]

    def out_proj(rows):
        return jnp.dot(o_ref[rows, :], w_ref[...], preferred_element_type=F32)

    y_next = out_proj(tiles[0])
    for n, rows in enumerate(tiles):
        y = y_next
        if n + 1 < len(tiles):
            y_next = out_proj(tiles[n + 1])
        xn = x_ref[rows, :] + gate_ref[0] * y
        xn_ref[rows, :] = xn
        h = _norm_mod(xn, g_ref[...], sh_ref[0], sc_ref[0])
        lt = _qk(wrt_ref[...], h.astype(BF16)) + brt_ref[...]
        meta_t, onehot_t, row8, gsel = _route_t(lt)
        prefix = jnp.dot(onehot_t.astype(BF16), tri_ref[...], preferred_element_type=F32)
        rank = jnp.where(row8 == gsel, prefix + cnt - 1.0, 0.0).sum(axis=0, keepdims=True)
        cnt = cnt + onehot_t.sum(axis=1, keepdims=True)
        meta_t = jnp.where(row8 == META_RANK_LANE, rank, meta_t)
        meta = jnp.concatenate([meta_t, jnp.zeros((LANES - 8, sub), F32)], axis=0).T
        metat_ref[:, rows] = meta_t
        hpk_ref[rows, :] = jnp.concatenate([_pack_halves(h), lax.bitcast_convert_type(meta, jnp.uint32)], axis=1)
    cnt_ref[...] = jnp.broadcast_to(cnt, cnt_ref.shape)


def _proj_res(o, x, w_o, g_ffn, mod, layer, row0, tokens_per_row, w_router_t, b_router_t):
    t = x.shape[0]
    tm = TM_PROJ
    tok = lambda n: pl.BlockSpec((tm, n), lambda i: (i, 0))
    ms = lambda k: _mod_spec(layer, k, row0, tokens_per_row, tm)
    tri = jnp.asarray(np.triu(np.ones((TM_SUB, TM_SUB), np.float32)), BF16)
    return pl.pallas_call(
        _proj_res_kernel,
        grid=(t // tm,),
        in_specs=[tok(D_MODEL), tok(D_MODEL), _const_spec(w_o.shape), ms(2), _const_spec((1, D_MODEL)), ms(3), ms(4),
                  _const_spec(w_router_t.shape), _const_spec(b_router_t.shape), _const_spec((TM_SUB, TM_SUB))],
        out_specs=[tok(D_MODEL), tok(ROW_WORDS), pl.BlockSpec((8, tm), lambda i: (0, i)), _const_spec((8, LANES))],
        out_shape=[jax.ShapeDtypeStruct((t, D_MODEL), F32), jax.ShapeDtypeStruct((t, ROW_WORDS), jnp.uint32),
                   jax.ShapeDtypeStruct((8, t), F32), jax.ShapeDtypeStruct((8, LANES), F32)],
        compiler_params=_cparams(("arbitrary",)),
        name="proj_res",
    )(o, x, w_o, mod, g_ffn, mod, mod, w_router_t, b_router_t, tri)


def _moe_plan(meta_t, cnt):
    t = meta_t.shape[1]
    n_tiles = t // TM + N_GROUPS
    group = meta_t[META_GROUP_LANE].astype(jnp.int32)
    rank = meta_t[META_RANK_LANE].astype(jnp.int32)
    counts = cnt[:N_GROUPS, 0].astype(jnp.int32)
    tiles = (counts + TM - 1) // TM
    tile_end = jnp.cumsum(tiles)
    tile_start = tile_end - tiles
    pos = rank + TM * jnp.sum(jnp.where(group[:, None] == jnp.arange(N_GROUPS)[None], tile_start[None], 0), axis=1)
    tile_group = jnp.minimum(jnp.sum(jnp.arange(n_tiles)[:, None] >= tile_end[None], axis=1), N_GROUPS - 1)
    pad_bounds = jnp.stack([tile_start * TM + counts, tile_end * TM], axis=1).reshape(-1)
    return (pos.astype(jnp.int32), tile_group.astype(jnp.int32), tile_end[-1:].astype(jnp.int32),
            pad_bounds.astype(jnp.int32))


def _gather_rows(srcs, idx_ref, base, dsts, n_rows):
    for r in range(n_rows):
        s = idx_ref[base + r]
        for src_ref, dst_ref in zip(srcs, dsts):
            dst_ref[r:r + 1, :] = src_ref[pl.ds(s, 1), :]


def _moe_kernel(pos_ref, tg_ref, nt_ref, pad_ref, hpk_hbm, wg_ref, wu_ref, wd_ref, ypk_ref,
                src_ref, hpk_ref, hbuf, hid_ref, load_sem):
    i = pl.program_id(0)
    n_tok = hpk_ref.shape[0]

    @pl.when(i == 0)
    def _():
        load = pltpu.make_async_copy(hpk_hbm, hpk_ref, load_sem)
        load.start()
        def clear(p, c):
            src_ref[p] = 0
            return c
        for g in range(N_GROUPS):
            lax.fori_loop(pad_ref[2 * g], pad_ref[2 * g + 1], clear, 0)

        def fill(t, c):
            src_ref[pos_ref[t]] = t
            return c
        lax.fori_loop(0, n_tok, fill, 0, unroll=8)
        load.wait()
        _gather_rows((hpk_ref,), src_ref, 0, (hbuf.at[0],), TM)

    @pl.when(i < nt_ref[0])
    def _():
        slot = i % 2
        row = hbuf[slot]
        lo, hi = _unpack_halves(row[:, :D_MODEL // 2])
        h = jnp.concatenate([lo.astype(BF16), hi.astype(BF16)], axis=1)
        gates = lax.bitcast_convert_type(row[:, D_MODEL // 2:], F32)
        nxt = jnp.minimum(i + 1, nt_ref[0] - 1) * TM
        _gather_rows((hpk_ref,), src_ref, nxt, (hbuf.at[1 - slot],), TM)
        f = D_FF_EXPERT
        for j in range(EXPERTS_PER_GROUP):
            gate_act = _silu(jnp.dot(h, wg_ref[0, j], preferred_element_type=F32))
            hid = gate_act * jnp.dot(h, wu_ref[0, j], preferred_element_type=F32) * gates[:, j:j + 1]
            hid_ref[:, j * f:(j + 1) * f] = hid.astype(BF16)
        ypk_ref[...] = _pack_halves(jnp.dot(hid_ref[...], wd_ref[0], preferred_element_type=F32))

    @pl.when(i >= nt_ref[0])
    def _():
        ypk_ref[...] = jnp.zeros(ypk_ref.shape, ypk_ref.dtype)


def _moe(hpk, pos, tile_group, n_used, pad_bounds, wg, wu, wd, layer):
    t = hpk.shape[0]
    n_tiles = t // TM + N_GROUPS
    f = D_FF_EXPERT
    g0 = layer * N_GROUPS
    grid_spec = pltpu.PrefetchScalarGridSpec(
        num_scalar_prefetch=4,
        grid=(n_tiles,),
        in_specs=[
            pl.BlockSpec(memory_space=pl.ANY),
            pl.BlockSpec((1, EXPERTS_PER_GROUP, D_MODEL, f), lambda i, pos, tg, nt, pad: (g0 + tg[i], 0, 0, 0)),
            pl.BlockSpec((1, EXPERTS_PER_GROUP, D_MODEL, f), lambda i, pos, tg, nt, pad: (g0 + tg[i], 0, 0, 0)),
            pl.BlockSpec((1, EXPERTS_PER_GROUP * f, D_MODEL), lambda i, pos, tg, nt, pad: (g0 + tg[i], 0, 0)),
        ],
        out_specs=pl.BlockSpec((TM, D_MODEL // 2), lambda i, *_: (i, 0)),
        scratch_shapes=[pltpu.SMEM((n_tiles * TM,), jnp.int32),
                        pltpu.VMEM(hpk.shape, hpk.dtype),
                        pltpu.VMEM((2, TM, ROW_WORDS), jnp.uint32),
                        pltpu.VMEM((TM, EXPERTS_PER_GROUP * f), BF16),
                        pltpu.SemaphoreType.DMA(())],
    )
    return pl.pallas_call(
        _moe_kernel,
        grid_spec=grid_spec,
        out_shape=jax.ShapeDtypeStruct((n_tiles * TM, D_MODEL // 2), jnp.uint32),
        compiler_params=_cparams(("arbitrary",)),
        name="moe",
    )(pos, tile_group, n_used, pad_bounds, hpk, wg, wu, wd)


def _combine_kernel(*refs, final_norm):
    pos_ref, ypk_ref, x_ref, gate_ref = refs[:4]
    o_ref, ybuf = refs[-2], refs[-1]
    _gather_rows((ypk_ref,), pos_ref, pl.program_id(0) * TM, (ybuf,), TM)
    lo, hi = _unpack_halves(ybuf[...])
    out = x_ref[...] + gate_ref[0] * jnp.concatenate([lo, hi], axis=1)
    if final_norm:
        out = _rms(out, refs[4][...])
    o_ref[...] = out


def _combine(ypk, pos, x, mod, layer, row0, tokens_per_row, final_g):
    t = x.shape[0]

    def mod_map(i, pos):
        row = row0 + (i * TM) // tokens_per_row
        return ((layer * MOD_ROWS + row) * N_MOD + 5, 0, 0)

    in_specs = [pl.BlockSpec(ypk.shape, lambda i, pos: (0, 0), pipeline_mode=pl.Buffered(1)),
                pl.BlockSpec((TM, D_MODEL), lambda i, pos: (i, 0)),
                pl.BlockSpec((1, 1, D_MODEL), mod_map)]
    args = [ypk, x, mod]
    if final_g is not None:
        in_specs.append(pl.BlockSpec((1, D_MODEL), lambda i, pos: (0, 0)))
        args.append(final_g)
    grid_spec = pltpu.PrefetchScalarGridSpec(
        num_scalar_prefetch=1,
        grid=(t // TM,),
        in_specs=in_specs,
        out_specs=pl.BlockSpec((TM, D_MODEL), lambda i, pos: (i, 0)),
        scratch_shapes=[pltpu.VMEM((TM, D_MODEL // 2), jnp.uint32)],
    )
    return pl.pallas_call(
        functools.partial(_combine_kernel, final_norm=final_g is not None),
        grid_spec=grid_spec,
        out_shape=jax.ShapeDtypeStruct((t, D_MODEL), F32),
        compiler_params=_cparams(("arbitrary",)),
        name="moe_combine",
    )(pos, *args)


def _pad_heads(w, used):
    k = w.shape[0]
    w = w.reshape(k, MLA_HEADS, used)
    return jnp.pad(w, ((0, 0), (0, 0), (0, LANES - used))).reshape(k, MLA_HEADS * LANES)


def kernel(x_prompt, x_sample, state_na_k, state_na_v, state_mla_ckv, state_mla_kr, c, c_ctx, ada_w, ada_b, norm_mix_g, norm_ffn_g, final_norm_g, na_w_qkv, na_w_o, na_rpb, mla_w_in, mla_q_norm_g, mla_kv_norm_g, mla_w_uq, mla_w_uk, mla_w_uv, mla_w_o, moe_w_group, moe_b_group, moe_w_expert, moe_b_expert, moe_w_gate, moe_w_up, moe_w_down):
    n_b, seq, d = x_prompt.shape
    dec_b, n_lat, _ = x_sample.shape
    past = state_na_k.shape[2]
    rows = n_lat // GRID_W
    assert d == D_MODEL and dec_b + 1 <= MOD_ROWS and n_lat % TM_PROJ == 0 and (n_b * seq) % TM_PROJ == 0

    cond8 = jnp.concatenate([c_ctx[None], c, jnp.zeros((MOD_ROWS - 1 - dec_b, d), F32)], axis=0)
    mod = _ada_mod(cond8, ada_w, ada_b)

    xp = x_prompt.reshape(n_b * seq, d)
    xs = x_sample.reshape(dec_b * n_lat, d)
    p_rows = (0, n_b * seq)
    s_rows = (1, n_lat)
    outs = {}

    f = D_FF_EXPERT
    wg = _to_bf16(moe_w_gate).reshape(DEPTH * N_GROUPS, EXPERTS_PER_GROUP, d, f)
    wu = _to_bf16(moe_w_up).reshape(DEPTH * N_GROUPS, EXPERTS_PER_GROUP, d, f)
    wd = _to_bf16(moe_w_down).reshape(DEPTH * N_GROUPS, EXPERTS_PER_GROUP * f, d)

    for i in range(DEPTH):
        j = i // 2
        g_mix = norm_mix_g[i][None]
        g_ffn = norm_ffn_g[i][None]
        if i % 2 == 0:
            w_qkv = _to_bf16(na_w_qkv[j])
            w_vt = _transposed_bf16(na_w_qkv[j], 2 * d)
            qp, kp, vp, outs['na_k'], outs['na_v'] = _na_qkv(xp, g_mix, mod, i, *p_rows, w_qkv, n_b)
            qs, ks, vs_t = _na_qkv_t(xs, g_mix, mod, i, *s_rows, w_qkv, w_vt)
            op = _attention(qp, [(kp, vp)], n_b, wide_qk=False, pairs=8, nb=4)
            tables = _na_bias_tables(na_rpb[j], rows)
            os_ = _na_latent(qs, ks, vs_t, _state_k(state_na_k, j), _state_vt(state_na_v, j), tables, dec_b)
            w_o = _to_bf16(na_w_o[j])
        else:
            pad_in = MLA_IN_PAD - mla_w_in.shape[-1]
            w_mla = (jnp.pad(mla_w_in[j], ((0, 0), (0, pad_in))).astype(BF16),
                     mla_q_norm_g[j][None], mla_kv_norm_g[j][None],
                     _pad_heads(mla_w_uq[j], MLA_QK_DIM).astype(BF16),
                     _pad_heads(mla_w_uk[j], MLA_NOPE_DIM).astype(BF16),
                     mla_w_uv[j].astype(BF16))
            wuv_t = _transposed_bf16(mla_w_uv[j])
            wuq_t = _transposed_bf16(_pad_heads(mla_w_uq[j], MLA_QK_DIM))
            w_mla_t = w_mla[:3] + (wuq_t, w_mla[4], wuv_t)
            qp, kp, vp, ckv_p, kr_p = _mla_proj(xp, g_mix, mod, i, *p_rows, w_mla, None, True)
            qs, ks, vs_t = _mla_proj(xs, g_mix, mod, i, *s_rows, w_mla_t, _rope_lane_tables(n_lat), False)
            outs['ckv'] = ckv_p.reshape(n_b, 1, seq, MLA_KV_RANK)
            outs['kr'] = kr_p.reshape(n_b, 1, seq, MLA_ROPE_DIM)
            kr_state = jnp.pad(state_mla_kr[:, j].reshape(dec_b * past, MLA_ROPE_DIM),
                               ((0, 0), (0, LANES - MLA_ROPE_DIM)))
            kc, vc_t = _mla_state_kv(state_mla_ckv[:, j].reshape(dec_b * past, MLA_KV_RANK), kr_state,
                                     w_mla[4], wuv_t)
            op = _attention(qp, [(kp, vp)], n_b, wide_qk=True, pairs=8, nb=4)
            os_ = _attention_t(qs, ks, vs_t, kc, vc_t, dec_b, 1024, heads=8)
            w_o = _to_bf16(mla_w_o[j])

        pad_g = ((0, ROUTER_EXPERT_ROW - N_GROUPS), (0, 0))
        pad_e = ((0, ROUTER_ROWS - ROUTER_EXPERT_ROW - N_EXPERTS), (0, 0))
        w_router_t = jnp.concatenate([jnp.pad(moe_w_group[i].T, pad_g), jnp.pad(moe_w_expert[i].T, pad_e)],
                                     axis=0).astype(BF16)
        b_router_t = jnp.concatenate([jnp.pad(moe_b_group[i][:, None], pad_g),
                                      jnp.pad(moe_b_expert[i][:, None], pad_e)], axis=0)
        b_router_t = jnp.broadcast_to(b_router_t, (ROUTER_ROWS, TM_SUB))
        final_g = final_norm_g[None] if i == DEPTH - 1 else None
        new_x = []
        for o_att, x, rows_ in ((op, xp, p_rows), (os_, xs, s_rows)):
            xn, hpk, meta, cnt = _proj_res(o_att, x, w_o, g_ffn, mod, i, *rows_, w_router_t, b_router_t)
            pos, tile_group, n_used, pad_bounds = _moe_plan(meta, cnt)
            ypk = _moe(hpk, pos, tile_group, n_used, pad_bounds, wg, wu, wd, i)
            new_x.append(_combine(ypk, pos, xn, mod, i, *rows_, final_g))
        xp, xs = new_x

    return (xp.reshape(n_b, seq, d), xs.reshape(dec_b, n_lat, d),
            outs['na_k'], outs['na_v'], outs['ckv'], outs['kr'])
```

```python
import functools

import numpy as np
import jax
import jax.numpy as jnp
from jax import lax
from jax.experimental import pallas as pl
from jax.experimental.pallas import tpu as pltpu

F32 = jnp.float32
BF16 = jnp.bfloat16

D_MODEL = 1024
DEPTH = 2
GRID_W = 64
N_MOD = 6
EPS = 1e-6
NEG_INF = -1e30
NA_HEADS = 16
NA_HEAD_DIM = 64
NA_KH = 8
NA_KW = 16
NA_GROUP_ROWS = 4
NA_WIN_ROWS = 12
MLA_HEADS = 16
MLA_Q_RANK = 384
MLA_KV_RANK = 256
MLA_NOPE_DIM = 64
MLA_ROPE_DIM = 32
MLA_V_DIM = 64
MLA_QK_DIM = MLA_NOPE_DIM + MLA_ROPE_DIM
MLA_IN_PAD = 768
ROPE_THETA = 10000.0
LOG2_E = 1.4426950408889634
N_GROUPS = 4
EXPERTS_PER_GROUP = 4
N_EXPERTS = 16
D_FF_EXPERT = 256

LANES = 128
MOD_ROWS = 8
VMEM_LIMIT = 56 * 1024 * 1024
TM = 512
CAST_BLOCK_ELEMS = 512 * 1024
TM_MLA = 1024
TM_SUB_MLA = 512
TM_SUB = 512
TM_PROJ = 1024


def _cparams(sem):
    return pltpu.CompilerParams(dimension_semantics=sem, vmem_limit_bytes=VMEM_LIMIT)


def _const_spec(shape):
    nd = len(shape)
    return pl.BlockSpec(shape, lambda *_: (0,) * nd)


def _cast_kernel(x_ref, o_ref):
    o_ref[...] = x_ref[...].astype(o_ref.dtype)


def _to_bf16(x):
    cols = x.shape[-1]
    x2 = x.reshape(-1, cols)
    rows = x2.shape[0]
    block = rows
    while block * cols > CAST_BLOCK_ELEMS and block % 32 == 0:
        block //= 2
    out = pl.pallas_call(
        _cast_kernel,
        grid=(rows // block,),
        in_specs=[pl.BlockSpec((block, cols), lambda i: (i, 0))],
        out_specs=pl.BlockSpec((block, cols), lambda i: (i, 0)),
        out_shape=jax.ShapeDtypeStruct((rows, cols), BF16),
        compiler_params=_cparams(("arbitrary",)),
        name="to_bf16",
    )(x2)
    return out.reshape(x.shape)


def _transpose_kernel(x_ref, o_ref):
    o_ref[...] = x_ref[...].T.astype(o_ref.dtype)


def _transposed_bf16(x, col0=0, n_cols=None):
    rows = x.shape[0]
    n_cols = x.shape[1] - col0 if n_cols is None else n_cols
    bc = 2 * LANES
    assert col0 % bc == 0 and n_cols % bc == 0
    return pl.pallas_call(
        _transpose_kernel,
        grid=(n_cols // bc,),
        in_specs=[pl.BlockSpec((rows, bc), lambda i: (0, col0 // bc + i))],
        out_specs=pl.BlockSpec((bc, rows), lambda i: (i, 0)),
        out_shape=jax.ShapeDtypeStruct((n_cols, rows), BF16),
        compiler_params=_cparams(("arbitrary",)),
        name="transposed_bf16",
    )(x)


def _norm_mod(x, g, shift, scale):
    ms = jnp.mean(x * x, axis=-1, keepdims=True)
    y = x * lax.rsqrt(ms + EPS) * g
    return y * (1.0 + scale) + shift


def _rms(x, g):
    ms = jnp.mean(x * x, axis=-1, keepdims=True)
    return x * lax.rsqrt(ms + EPS) * g


def _silu(x):
    return x / (1.0 + jnp.exp(-x))


def _ada_kernel(cond_ref, w_ref, b_ref, o_ref):
    s = _silu(cond_ref[...]).astype(BF16)
    o_ref[0] = jnp.dot(s, w_ref[0].astype(BF16), preferred_element_type=F32) + b_ref[0]


def _ada_mod(cond8, ada_w, ada_b):
    tn = 1024
    n = N_MOD * D_MODEL
    out = pl.pallas_call(
        _ada_kernel,
        grid=(DEPTH, n // tn),
        in_specs=[
            _const_spec((MOD_ROWS, D_MODEL)),
            pl.BlockSpec((1, D_MODEL, tn), lambda l, j: (l, 0, j)),
            pl.BlockSpec((1, 1, tn), lambda l, j: (l, 0, j)),
        ],
        out_specs=pl.BlockSpec((1, MOD_ROWS, tn), lambda l, j: (l, 0, j)),
        out_shape=jax.ShapeDtypeStruct((DEPTH, MOD_ROWS, n), F32),
        compiler_params=_cparams(("arbitrary", "arbitrary")),
        name="ada_mod",
    )(cond8, ada_w, ada_b.reshape(DEPTH, 1, n))
    return out.reshape(DEPTH * MOD_ROWS * N_MOD, 1, D_MODEL)


def _mod_spec(layer, k, row0, tokens_per_row, tm=TM):
    def imap(i):
        row = row0 + (i * tm) // tokens_per_row
        return ((layer * MOD_ROWS + row) * N_MOD + k, 0, 0)
    return pl.BlockSpec((1, 1, D_MODEL), imap)


def _qkv_kernel(x_ref, g_ref, sh_ref, sc_ref, w_ref, q_ref, k_ref, v_ref, ks_ref, vs_ref):
    h = _norm_mod(x_ref[...], g_ref[...], sh_ref[0], sc_ref[0]).astype(BF16)
    y = jnp.dot(h, w_ref[...], preferred_element_type=F32)
    q_ref[...] = (y[:, :D_MODEL] * (NA_HEAD_DIM ** -0.5)).astype(q_ref.dtype)
    k = y[:, D_MODEL:2 * D_MODEL]
    v = y[:, 2 * D_MODEL:]
    k_ref[...] = k.astype(k_ref.dtype)
    v_ref[...] = v.astype(v_ref.dtype)
    ks_ref[...] = pltpu.einshape("t(hd)->thd", k, h=NA_HEADS).reshape(ks_ref.shape)
    vs_ref[...] = pltpu.einshape("t(hd)->thd", v, h=NA_HEADS).reshape(vs_ref.shape)


def _na_qkv(x, g, mod, layer, row0, tokens_per_row, w_qkv, n_batch):
    t = x.shape[0]
    seq = t // n_batch
    nb = TM // seq
    tok = pl.BlockSpec((TM, D_MODEL), lambda i: (i, 0))
    state = pl.BlockSpec((nb, 1, seq, NA_HEADS, NA_HEAD_DIM), lambda i: (i, 0, 0, 0, 0))
    state_shape = jax.ShapeDtypeStruct((n_batch, 1, seq, NA_HEADS, NA_HEAD_DIM), F32)
    return pl.pallas_call(
        _qkv_kernel,
        grid=(t // TM,),
        in_specs=[tok, _const_spec((1, D_MODEL)),
                  _mod_spec(layer, 0, row0, tokens_per_row), _mod_spec(layer, 1, row0, tokens_per_row),
                  _const_spec((D_MODEL, 3 * D_MODEL))],
        out_specs=[tok, tok, tok, state, state],
        out_shape=[jax.ShapeDtypeStruct((t, D_MODEL), BF16)] * 3 + [state_shape, state_shape],
        compiler_params=_cparams(("arbitrary",)),
        name="na_qkv",
    )(x, g, mod, mod, w_qkv)


def _store_head_slots(vt, vt_ref):
    n = vt.shape[1]
    n_heads = vt_ref.shape[0] // LANES
    dim = vt.shape[0] // n_heads
    tail = jnp.where(lax.broadcasted_iota(jnp.int32, (LANES - dim, n), 0) == 0, 1.0, 0.0).astype(vt_ref.dtype)
    for hh in range(n_heads):
        vt_ref[hh * LANES:hh * LANES + dim, :] = vt[hh * dim:(hh + 1) * dim, :].astype(vt_ref.dtype)
        vt_ref[hh * LANES + dim:(hh + 1) * LANES, :] = tail


def _qkv_t_kernel(x_ref, g_ref, sh_ref, sc_ref, wqk_ref, wvt_ref, q_ref, k_ref, vt_ref):
    h = _norm_mod(x_ref[...], g_ref[...], sh_ref[0], sc_ref[0]).astype(BF16)
    y = jnp.dot(h, wqk_ref[...], preferred_element_type=F32)
    q_ref[...] = (y[:, :D_MODEL] * (NA_HEAD_DIM ** -0.5 * LOG2_E)).astype(q_ref.dtype)
    k_ref[...] = y[:, D_MODEL:].astype(k_ref.dtype)
    _store_head_slots(_qk(wvt_ref[...], h), vt_ref)


def _na_qkv_t(x, g, mod, layer, row0, tokens_per_row, w_qkv, w_vt):
    t = x.shape[0]
    tm = TM_MLA
    tok = pl.BlockSpec((tm, D_MODEL), lambda i: (i, 0))
    slots = NA_HEADS * LANES
    return pl.pallas_call(
        _qkv_t_kernel,
        grid=(t // tm,),
        in_specs=[tok, _const_spec((1, D_MODEL)),
                  _mod_spec(layer, 0, row0, tokens_per_row, tm), _mod_spec(layer, 1, row0, tokens_per_row, tm),
                  pl.BlockSpec((D_MODEL, 2 * D_MODEL), lambda i: (0, 0)), _const_spec(w_vt.shape)],
        out_specs=[tok, tok, pl.BlockSpec((slots, tm), lambda i: (0, i))],
        out_shape=[jax.ShapeDtypeStruct((t, D_MODEL), BF16), jax.ShapeDtypeStruct((t, D_MODEL), BF16),
                   jax.ShapeDtypeStruct((slots, t), BF16)],
        compiler_params=_cparams(("arbitrary",)),
        name="na_qkv_t",
    )(x, g, mod, mod, w_qkv, w_vt)


def _softmax_pv(s_list, v_list):
    m = s_list[0].max(axis=-1, keepdims=True)
    for s in s_list[1:]:
        m = jnp.maximum(m, s.max(axis=-1, keepdims=True))
    l = None
    acc = None
    for s, v in zip(s_list, v_list):
        p = jnp.exp(s - m)
        ls = p.sum(axis=-1, keepdims=True)
        a = jnp.dot(p.astype(BF16), v, preferred_element_type=F32)
        l = ls if l is None else l + ls
        acc = a if acc is None else acc + a
    return acc * (1.0 / l)


def _qk(q, k):
    return lax.dot_general(q, k, (((1,), (1,)), ((), ())), preferred_element_type=F32)


def _head_masks():
    lane = lax.broadcasted_iota(jnp.int32, (1, LANES), 1)
    lo = lane < (LANES // 2)
    return lo, jnp.where(lo, 1.0, 0.0).astype(BF16), jnp.where(lo, 0.0, 1.0).astype(BF16)


def _attn_kernel(*refs, n_src, wide_qk, pairs, nb):
    q_ref, o_ref = refs[0], refs[-1]
    kv = refs[1:-1]
    lo, m0, m1 = _head_masks()
    wq = 2 * LANES if wide_qk else LANES
    tq = q_ref.shape[0] // nb
    for bb in range(nb):
        rq = slice(bb * tq, (bb + 1) * tq)
        rk = [slice(bb * (kv[2 * t].shape[0] // nb), (bb + 1) * (kv[2 * t].shape[0] // nb)) for t in range(n_src)]
        for p in range(pairs):
            q = q_ref[rq, p * wq:(p + 1) * wq]
            ks = [kv[2 * t][rk[t], p * wq:(p + 1) * wq].astype(BF16) for t in range(n_src)]
            vs = [kv[2 * t + 1][rk[t], p * LANES:(p + 1) * LANES].astype(BF16) for t in range(n_src)]
            outs = []
            for j in range(2):
                if wide_qk:
                    s_list = [_qk(q[:, j * LANES:(j + 1) * LANES], k[:, j * LANES:(j + 1) * LANES]) for k in ks]
                else:
                    s_list = [_qk(q * (m0 if j == 0 else m1), k) for k in ks]
                outs.append(_softmax_pv(s_list, vs))
            o_ref[rq, p * LANES:(p + 1) * LANES] = jnp.where(lo, outs[0], outs[1]).astype(o_ref.dtype)


def _attention(q, kv_srcs, n_batch, wide_qk, pairs, nb):
    wq = (2 * LANES if wide_qk else LANES) * pairs
    wv = LANES * pairs
    tq = q.shape[0] // n_batch * nb
    n_steps = D_MODEL // wv
    in_specs = [pl.BlockSpec((tq, wq), lambda b, p: (b, p))]
    args = [q]
    for k, v in kv_srcs:
        t_k = k.shape[0] // n_batch * nb
        in_specs.append(pl.BlockSpec((t_k, wq), lambda b, p: (b, p)))
        in_specs.append(pl.BlockSpec((t_k, wv), lambda b, p: (b, p)))
        args += [k, v]
    return pl.pallas_call(
        functools.partial(_attn_kernel, n_src=len(kv_srcs), wide_qk=wide_qk, pairs=pairs, nb=nb),
        grid=(n_batch // nb, n_steps),
        in_specs=in_specs,
        out_specs=pl.BlockSpec((tq, wv), lambda b, p: (b, p)),
        out_shape=jax.ShapeDtypeStruct((q.shape[0], D_MODEL), BF16),
        compiler_params=_cparams(("arbitrary", "arbitrary")),
        name="attention",
    )(*args)


def _col_max(s, chunk=64):
    acc = s[:chunk]
    for r in range(chunk, s.shape[0], chunk):
        acc = jnp.maximum(acc, s[r:r + chunk])
    return acc.max(axis=0, keepdims=True)


def _attn_t_kernel(q_ref, k1_ref, k2_ref, v1_ref, v2_ref, o_ref):
    n_heads = q_ref.shape[0] // LANES
    slots = [slice(j * LANES, (j + 1) * LANES) for j in range(n_heads)]
    def scores(sl):
        qt = q_ref[sl, :]
        return (jnp.dot(k1_ref[:, sl], qt, preferred_element_type=F32),
                jnp.dot(k2_ref[:, sl], qt, preferred_element_type=F32))

    nxt = scores(slots[0])
    outs = []
    for n, sl in enumerate(slots):
        s1, s2 = nxt
        if n + 1 < n_heads:
            nxt = scores(slots[n + 1])
        m = jnp.maximum(_col_max(s1), _col_max(s2))
        p1 = jnp.exp2(s1 - m).astype(BF16)
        p2 = jnp.exp2(s2 - m).astype(BF16)
        ot = (jnp.dot(v1_ref[sl, :], p1, preferred_element_type=F32)
              + jnp.dot(v2_ref[sl, :], p2, preferred_element_type=F32))
        outs.append(ot[:MLA_V_DIM] * (1.0 / ot[MLA_V_DIM:MLA_V_DIM + 1]))
    for j in range(0, n_heads, 2):
        pair = jnp.concatenate(outs[j:j + 2], axis=0).T
        o_ref[:, (j // 2) * LANES:(j // 2 + 1) * LANES] = pair.astype(o_ref.dtype)


def _attention_t(qt, k1, v1t, k2, v2t, n_batch, tq, heads):
    wq = heads * LANES
    wo = heads * MLA_V_DIM
    nq = qt.shape[1] // n_batch // tq
    t1 = k1.shape[0] // n_batch
    t2 = k2.shape[0] // n_batch
    return pl.pallas_call(
        _attn_t_kernel,
        grid=(n_batch, MLA_HEADS // heads, nq),
        in_specs=[pl.BlockSpec((wq, tq), lambda b, p, i: (p, b * nq + i)),
                  pl.BlockSpec((t1, wq), lambda b, p, i: (b, p)),
                  pl.BlockSpec((t2, wq), lambda b, p, i: (b, p)),
                  pl.BlockSpec((wq, t1), lambda b, p, i: (p, b)),
                  pl.BlockSpec((wq, t2), lambda b, p, i: (p, b))],
        out_specs=pl.BlockSpec((tq, wo), lambda b, p, i: (b * nq + i, p)),
        out_shape=jax.ShapeDtypeStruct((qt.shape[1], D_MODEL), BF16),
        compiler_params=_cparams(("arbitrary", "arbitrary", "arbitrary")),
        name="attention_t",
    )(qt, k1, k2, v1t, v2t)


def _na_group_layout(rows):
    n_groups = rows // NA_GROUP_ROWS
    kh = min(NA_KH, rows)
    bases, types, keys = [], [], {}
    for gi in range(n_groups):
        r0 = gi * NA_GROUP_ROWS
        base = int(np.clip(r0 - kh // 2, 0, rows - NA_WIN_ROWS))
        starts = tuple(int(np.clip(r - kh // 2, 0, rows - kh)) - base for r in range(r0, r0 + NA_GROUP_ROWS))
        key = (starts, r0 - base)
        if key not in keys:
            keys[key] = len(keys)
        bases.append(base)
        types.append(keys[key])
    return bases, types, list(keys.keys())


def _na_bias_tables(rpb, rows):
    kh = min(NA_KH, rows)
    _, _, type_keys = _na_group_layout(rows)
    qc = np.arange(GRID_W)
    kc = np.arange(GRID_W)
    col_start = np.clip(qc - NA_KW // 2, 0, GRID_W - NA_KW)
    col_valid = (kc[None, :] >= col_start[:, None]) & (kc[None, :] < col_start[:, None] + NA_KW)
    dx = kc[None, :] - qc[:, None] + (NA_KW - 1)
    onehot = ((dx[None] == np.arange(2 * NA_KW - 1)[:, None, None]) & col_valid[None]).astype(np.float32)
    band = jnp.einsum('hyd,dqk->hykq', rpb.astype(F32), onehot, precision=lax.Precision.HIGHEST)
    band = jnp.where(col_valid.T, band, NEG_INF)
    strip = band.reshape(NA_HEADS, (2 * NA_KH - 1) * GRID_W, GRID_W)
    strip2 = jnp.concatenate([strip, strip], axis=-1)
    layout = []
    for starts, r_off in type_keys:
        layout.append(tuple((starts[rq], starts[rq] - (rq + r_off) + (NA_KH - 1)) for rq in range(NA_GROUP_ROWS)))
    gq = NA_GROUP_ROWS * GRID_W
    gk = NA_WIN_ROWS * GRID_W
    return pl.pallas_call(
        functools.partial(_bias_table_kernel, layout=tuple(layout), kh=kh),
        grid=(NA_HEADS,),
        in_specs=[pl.BlockSpec((1,) + strip2.shape[1:], lambda h: (h, 0, 0))],
        out_specs=pl.BlockSpec((1, len(layout), gk, gq), lambda h: (h, 0, 0, 0)),
        out_shape=jax.ShapeDtypeStruct((NA_HEADS, len(layout), gk, gq), F32),
        compiler_params=_cparams(("arbitrary",)),
        name="na_bias_table",
    )(strip2)


def _bias_table_kernel(strip_ref, o_ref, *, layout, kh):
    lo = lax.broadcasted_iota(jnp.int32, (1, LANES), 1) < GRID_W
    neg = NEG_INF * LOG2_E

    def column(start, dy0):
        parts = []
        if start:
            parts.append(jnp.full((start * GRID_W, LANES), neg, F32))
        parts.append(strip_ref[0, dy0 * GRID_W:(dy0 + kh) * GRID_W, :] * LOG2_E)
        if NA_WIN_ROWS - kh - start:
            parts.append(jnp.full(((NA_WIN_ROWS - kh - start) * GRID_W, LANES), neg, F32))
        return jnp.concatenate(parts, axis=0)

    for ty, rows in enumerate(layout):
        for c in range(0, NA_GROUP_ROWS, 2):
            even, odd = column(*rows[c]), column(*rows[c + 1])
            o_ref[0, ty, :, c * GRID_W:(c + 2) * GRID_W] = jnp.where(lo, even, odd)


def _state_k_kernel(x_ref, o_ref):
    for h in range(NA_HEADS):
        o_ref[:, h * NA_HEAD_DIM:(h + 1) * NA_HEAD_DIM] = x_ref[0, 0, :, h, :].astype(o_ref.dtype)


def _state_vt_kernel(x_ref, o_ref, flat_ref):
    for h in range(NA_HEADS):
        flat_ref[:, h * NA_HEAD_DIM:(h + 1) * NA_HEAD_DIM] = x_ref[0, 0, :, h, :]
    _store_head_slots(flat_ref[...].T, o_ref)


def _state_k(x, layer):
    b, _, t, h, dh = x.shape
    return pl.pallas_call(
        _state_k_kernel,
        grid=(b,),
        in_specs=[pl.BlockSpec((1, 1, t, h, dh), lambda i: (i, layer, 0, 0, 0))],
        out_specs=pl.BlockSpec((t, h * dh), lambda i: (i, 0)),
        out_shape=jax.ShapeDtypeStruct((b * t, h * dh), BF16),
        compiler_params=_cparams(("arbitrary",)),
        name="state_k",
    )(x)


def _state_vt(x, layer):
    b, _, t, h, dh = x.shape
    return pl.pallas_call(
        _state_vt_kernel,
        grid=(b,),
        in_specs=[pl.BlockSpec((1, 1, t, h, dh), lambda i: (i, layer, 0, 0, 0))],
        out_specs=pl.BlockSpec((h * LANES, t), lambda i: (0, i)),
        out_shape=jax.ShapeDtypeStruct((h * LANES, b * t), BF16),
        scratch_shapes=[pltpu.VMEM((t, h * dh), F32)],
        compiler_params=_cparams(("arbitrary",)),
        name="state_vt",
    )(x)


def _na_latent_kernel(q_ref, k_ref, vt_ref, kc_ref, vct_ref, tab_ref, o_ref, *, bases, types):
    _, m0, m1 = _head_masks()
    kctx = kc_ref[...]
    gq = NA_GROUP_ROWS * GRID_W
    gk = NA_WIN_ROWS * GRID_W
    blocks = [(gi, base, ty, j) for gi, (base, ty) in enumerate(zip(bases, types)) for j in range(2)]

    def scores(gi, base, ty, j):
        qj = q_ref[gi * gq:(gi + 1) * gq, :] * (m0 if j == 0 else m1)
        kwin = k_ref[base * GRID_W:base * GRID_W + gk, :]
        return _qk(kwin, qj) + tab_ref[j, ty], _qk(kctx, qj)

    nxt = scores(*blocks[0])
    outs = []
    for n, (gi, base, ty, j) in enumerate(blocks):
        s_win, s_ctx = nxt
        if n + 1 < len(blocks):
            nxt = scores(*blocks[n + 1])
        m = jnp.maximum(_col_max(s_win), _col_max(s_ctx))
        p_win = jnp.exp2(s_win - m).astype(BF16)
        p_ctx = jnp.exp2(s_ctx - m).astype(BF16)
        sl = slice(j * LANES, (j + 1) * LANES)
        ot = (jnp.dot(vt_ref[sl, base * GRID_W:base * GRID_W + gk], p_win, preferred_element_type=F32)
              + jnp.dot(vct_ref[sl, :], p_ctx, preferred_element_type=F32))
        outs.append(ot[:NA_HEAD_DIM] * (1.0 / ot[NA_HEAD_DIM:NA_HEAD_DIM + 1]))
        if j == 1:
            o_ref[gi * gq:(gi + 1) * gq, :] = jnp.concatenate(outs, axis=0).T.astype(o_ref.dtype)
            outs = []


def _na_latent(q, k, vt, k_ctx, vt_ctx, tables, n_batch):
    t = q.shape[0] // n_batch
    rows = t // GRID_W
    t_ctx = k_ctx.shape[0] // n_batch
    bases, types, type_keys = _na_group_layout(rows)
    n_types = len(type_keys)
    gq = NA_GROUP_ROWS * GRID_W
    gk = NA_WIN_ROWS * GRID_W
    tok = pl.BlockSpec((t, LANES), lambda p, b: (b, p))
    ctx = pl.BlockSpec((t_ctx, LANES), lambda p, b: (b, p))
    return pl.pallas_call(
        functools.partial(_na_latent_kernel, bases=bases, types=types),
        grid=(D_MODEL // LANES, n_batch),
        in_specs=[tok, tok, pl.BlockSpec((2 * LANES, t), lambda p, b: (p, b)),
                  ctx, pl.BlockSpec((2 * LANES, t_ctx), lambda p, b: (p, b)),
                  pl.BlockSpec((2, n_types, gk, gq), lambda p, b: (p, 0, 0, 0))],
        out_specs=tok,
        out_shape=jax.ShapeDtypeStruct(q.shape, BF16),
        compiler_params=_cparams(("arbitrary", "arbitrary")),
        name="na_latent",
    )(q, k, vt, k_ctx, vt_ctx, tables)


def _rope_lane_tables(n_tokens):
    f32 = np.float32
    n_freq = MLA_ROPE_DIM // 4
    inv = f32(1.0) / (f32(ROPE_THETA) ** (np.arange(n_freq, dtype=f32) / f32(n_freq)))
    tt = np.arange(n_tokens)
    row = (tt // GRID_W).astype(f32)
    col = (tt % GRID_W).astype(f32)
    ang = np.concatenate([row[:, None] * inv, col[:, None] * inv], axis=-1).astype(f32)
    cos, sin = np.cos(ang).astype(f32), np.sin(ang).astype(f32)
    ones = np.ones((n_tokens, MLA_NOPE_DIM), f32)
    z16 = np.zeros((n_tokens, MLA_ROPE_DIM // 2), f32)
    z32 = np.zeros((n_tokens, LANES - MLA_QK_DIM), f32)
    z64 = np.zeros((n_tokens, MLA_NOPE_DIM), f32)
    c_tab = np.concatenate([ones, cos, cos, z32], axis=-1)
    s_dn = np.concatenate([z64, -sin, z16, z32], axis=-1)
    s_up = np.concatenate([z64, z16, sin, z32], axis=-1)
    return (jnp.asarray(c_tab), jnp.asarray(s_dn), jnp.asarray(s_up),
            jnp.asarray(np.ascontiguousarray(cos.T)), jnp.asarray(np.ascontiguousarray(sin.T)))


def _rope_slot(x, c_tab, s_dn, s_up):
    half = MLA_ROPE_DIM // 2
    return x * c_tab + pltpu.roll(x, LANES - half, 1) * s_dn + pltpu.roll(x, half, 1) * s_up


def _mla_kv(ckv_n, kr_slot, wuk_ref, wuv_ref, k_ref, v_ref, v_transposed, rows=slice(None)):
    c = ckv_n.astype(BF16)
    k_nope = jnp.dot(c, wuk_ref[...], preferred_element_type=F32)
    for h in range(MLA_HEADS):
        k_ref[rows, h * LANES:(h + 1) * LANES] = (k_nope[:, h * LANES:(h + 1) * LANES] + kr_slot).astype(k_ref.dtype)
    if v_transposed:
        _store_head_slots(_qk(wuv_ref[...], c), v_ref.at[:, rows])
    else:
        v_ref[rows, :] = jnp.dot(c, wuv_ref[...], preferred_element_type=F32).astype(v_ref.dtype)


def _mla_proj_kernel(*refs, rope, emit_state, q_scale):
    (x_ref, g_ref, sh_ref, sc_ref, win_ref, qg_ref, kvg_ref, wuq_ref, wuk_ref, wuv_ref) = refs[:10]
    pos = 10
    if rope:
        ct_ref, sd_ref, su_ref, cos_ref, sin_ref = refs[pos:pos + 5]
        pos += 5
    q_ref, k_ref, v_ref = refs[pos:pos + 3]
    pos += 3

    def front(rows):
        h = _norm_mod(x_ref[rows, :], g_ref[...], sh_ref[0], sc_ref[0]).astype(BF16)
        c = jnp.dot(h, win_ref[...], preferred_element_type=F32)
        cq = _rms(c[:, :MLA_Q_RANK], qg_ref[...]).astype(BF16)
        if rope:
            q = _qk(wuq_ref[...], cq) * q_scale
        else:
            q = jnp.dot(cq, wuq_ref[...], preferred_element_type=F32) * q_scale
        ckv = _rms(c[:, MLA_Q_RANK:MLA_Q_RANK + MLA_KV_RANK], kvg_ref[...])
        kr_chunk = c[:, MLA_Q_RANK + MLA_KV_RANK:]
        if emit_state:
            ckv_out, kr_out = refs[pos:pos + 2]
            ckv_out[rows, :] = ckv
            kr_out[rows, :] = kr_chunk[:, :MLA_ROPE_DIM]
        return q, ckv, pltpu.roll(kr_chunk, MLA_NOPE_DIM, 1)

    def back(rows, q, ckv, kr_slot):
        if rope:
            ct, sd, su = ct_ref[rows, :], sd_ref[rows, :], su_ref[rows, :]
            kr_slot = _rope_slot(kr_slot, ct, sd, su)
        _mla_kv(ckv, kr_slot, wuk_ref, wuv_ref, k_ref, v_ref, v_transposed=rope, rows=rows)
        if not rope:
            q_ref[rows, :] = q.astype(q_ref.dtype)
            return
        cs, sn = cos_ref[:, rows], sin_ref[:, rows]
        half = MLA_ROPE_DIM // 2
        for hh in range(MLA_HEADS):
            r0 = hh * LANES
            r1 = r0 + MLA_NOPE_DIM
            x1, x2 = q[r1:r1 + half], q[r1 + half:r1 + 2 * half]
            q_ref[r0:r1, rows] = q[r0:r1].astype(q_ref.dtype)
            q_ref[r1:r1 + half, rows] = (x1 * cs - x2 * sn).astype(q_ref.dtype)
            q_ref[r1 + half:r1 + 2 * half, rows] = (x2 * cs + x1 * sn).astype(q_ref.dtype)
            q_ref[r1 + 2 * half:r0 + LANES, rows] = q[r1 + 2 * half:r0 + LANES].astype(q_ref.dtype)

    sub = x_ref.shape[0] if rope else TM_SUB_MLA
    tiles = [slice(r0, r0 + sub) for r0 in range(0, x_ref.shape[0], sub)]
    nxt = front(tiles[0])
    for n, rows in enumerate(tiles):
        cur = nxt
        if n + 1 < len(tiles):
            nxt = front(tiles[n + 1])
        back(rows, *cur)


def _mla_proj(x, g, mod, layer, row0, tokens_per_row, w, rope_tabs, emit_state):
    t = x.shape[0]
    win, qg, kvg, wuq, wuk, wuv = w
    tm = TM_MLA
    tok = lambda n: pl.BlockSpec((tm, n), lambda i: (i, 0))
    wq = MLA_HEADS * LANES
    in_specs = [tok(D_MODEL), _const_spec((1, D_MODEL)),
                _mod_spec(layer, 0, row0, tokens_per_row, tm), _mod_spec(layer, 1, row0, tokens_per_row, tm),
                _const_spec(win.shape), _const_spec(qg.shape), _const_spec(kvg.shape),
                _const_spec(wuq.shape), _const_spec(wuk.shape), _const_spec(wuv.shape)]
    args = [x, g, mod, mod, win, qg, kvg, wuq, wuk, wuv]
    rope = rope_tabs is not None
    if rope:
        n_tab = rope_tabs[0].shape[0] // tm
        in_specs += [pl.BlockSpec((tm, LANES), lambda i: (i % n_tab, 0)) for _ in range(3)]
        in_specs += [pl.BlockSpec((MLA_ROPE_DIM // 2, tm), lambda i: (0, i % n_tab)) for _ in range(2)]
        args += list(rope_tabs)
    if rope:
        q_spec, q_shape = pl.BlockSpec((wq, tm), lambda i: (0, i)), jax.ShapeDtypeStruct((wq, t), BF16)
        v_spec, v_shape = q_spec, q_shape
        q_scale = MLA_QK_DIM ** -0.5 * LOG2_E
    else:
        q_spec, q_shape = tok(wq), jax.ShapeDtypeStruct((t, wq), BF16)
        v_spec, v_shape = tok(D_MODEL), jax.ShapeDtypeStruct((t, D_MODEL), BF16)
        q_scale = MLA_QK_DIM ** -0.5
    out_specs = [q_spec, tok(wq), v_spec]
    out_shape = [q_shape, jax.ShapeDtypeStruct((t, wq), BF16), v_shape]
    if emit_state:
        out_specs += [tok(MLA_KV_RANK), tok(MLA_ROPE_DIM)]
        out_shape += [jax.ShapeDtypeStruct((t, MLA_KV_RANK), F32), jax.ShapeDtypeStruct((t, MLA_ROPE_DIM), F32)]
    return pl.pallas_call(
        functools.partial(_mla_proj_kernel, rope=rope, emit_state=emit_state, q_scale=q_scale),
        grid=(t // tm,),
        in_specs=in_specs, out_specs=out_specs, out_shape=out_shape,
        compiler_params=_cparams(("arbitrary",)),
        name="mla_proj",
    )(*args)


def _mla_state_kv_kernel(ckv_ref, kr_ref, wuk_ref, wuv_ref, k_ref, v_ref):
    kr_slot = pltpu.roll(kr_ref[...], MLA_NOPE_DIM, 1)
    _mla_kv(ckv_ref[...], kr_slot, wuk_ref, wuv_ref, k_ref, v_ref, v_transposed=True)


def _mla_state_kv(ckv, kr_pad, wuk, wuv_t):
    t = ckv.shape[0]
    wq = MLA_HEADS * LANES
    tok = lambda n: pl.BlockSpec((TM, n), lambda i: (i, 0))
    return pl.pallas_call(
        _mla_state_kv_kernel,
        grid=(t // TM,),
        in_specs=[tok(MLA_KV_RANK), tok(LANES), _const_spec(wuk.shape), _const_spec(wuv_t.shape)],
        out_specs=[tok(wq), pl.BlockSpec((wq, TM), lambda i: (0, i))],
        out_shape=[jax.ShapeDtypeStruct((t, wq), BF16), jax.ShapeDtypeStruct((wq, t), BF16)],
        compiler_params=_cparams(("arbitrary",)),
        name="mla_state_kv",
    )(ckv, kr_pad, wuk, wuv_t)


def _first_argmax_rows(v, row):
    top = v.max(axis=0, keepdims=True)
    idx = jnp.where(v == top, row, float(v.shape[0])).min(axis=0, keepdims=True)
    return top, idx


ROUTER_ROWS = 32
ROUTER_EXPERT_ROW = 8


def _route_t(lt):
    n = lt.shape[1]
    row8 = lax.broadcasted_iota(jnp.int32, (8, n), 0).astype(F32)
    row16 = lax.broadcasted_iota(jnp.int32, (N_EXPERTS, n), 0).astype(F32)
    is_g = row8 < N_GROUPS
    gl = jnp.where(is_g, lt[:8], -jnp.inf)
    gmax, gsel = _first_argmax_rows(gl, row8)
    p_group = 1.0 / jnp.where(is_g, jnp.exp(gl - gmax), 0.0).sum(axis=0, keepdims=True)
    first = gsel * EXPERTS_PER_GROUP
    e_all = lt[ROUTER_EXPERT_ROW:ROUTER_EXPERT_ROW + N_EXPERTS]
    el = jnp.where((row16 >= first) & (row16 < first + EXPERTS_PER_GROUP), e_all, -jnp.inf)
    t1, i1 = _first_argmax_rows(el, row16)
    t2, i2 = _first_argmax_rows(jnp.where(row16 == i1, -jnp.inf, el), row16)
    e2 = jnp.exp(t2 - t1)
    w1 = 1.0 / (1.0 + e2)
    gates = jnp.where(row16 == i1, w1 * p_group, jnp.where(row16 == i2, e2 * w1 * p_group, 0.0))
    rel = gates[:8] + gates[8:]
    rel = rel + pltpu.roll(rel, EXPERTS_PER_GROUP, 0)
    meta_t = jnp.where(row8 < EXPERTS_PER_GROUP, rel, jnp.where(row8 == META_GROUP_LANE, gsel, 0.0))
    onehot_t = jnp.where(row8 == gsel, 1.0, 0.0)
    return meta_t, onehot_t, row8, gsel


def _pack_halves(x):
    half = x.shape[1] // 2
    lo = lax.bitcast_convert_type(x[:, :half].astype(BF16).astype(F32), jnp.uint32)
    hi = lax.bitcast_convert_type(x[:, half:].astype(BF16).astype(F32), jnp.uint32)
    return (lo >> 16) | hi


def _unpack_halves(p):
    lo = lax.bitcast_convert_type(p << 16, F32)
    hi = lax.bitcast_convert_type(p & jnp.uint32(0xFFFF0000), F32)
    return lo, hi


ROW_WORDS = D_MODEL // 2 + LANES
META_GROUP_LANE = EXPERTS_PER_GROUP
META_RANK_LANE = EXPERTS_PER_GROUP + 1


def _proj_res_kernel(o_ref, x_ref, w_ref, gate_ref, g_ref, sh_ref, sc_ref, wrt_ref, brt_ref, tri_ref,
                     xn_ref, hpk_ref, metat_ref, cnt_ref):
    @pl.when(pl.program_id(0) == 0)
    def _():
        cnt_ref[...] = jnp.zeros(cnt_ref.shape, cnt_ref.dtype)

    sub = tri_ref.shape[0]
    cnt = cnt_ref[:, :1]
    tiles = [slice(r0, r0 + sub) for r0 in range(0, x_ref.shape[0], sub)]

    def out_proj(rows):
        return jnp.dot(o_ref[rows, :], w_ref[...], preferred_element_type=F32)

    y_next = out_proj(tiles[0])
    for n, rows in enumerate(tiles):
        y = y_next
        if n + 1 < len(tiles):
            y_next = out_proj(tiles[n + 1])
        xn = x_ref[rows, :] + gate_ref[0] * y
        xn_ref[rows, :] = xn
        h = _norm_mod(xn, g_ref[...], sh_ref[0], sc_ref[0])
        lt = _qk(wrt_ref[...], h.astype(BF16)) + brt_ref[...]
        meta_t, onehot_t, row8, gsel = _route_t(lt)
        prefix = jnp.dot(onehot_t.astype(BF16), tri_ref[...], preferred_element_type=F32)
        rank = jnp.where(row8 == gsel, prefix + cnt - 1.0, 0.0).sum(axis=0, keepdims=True)
        cnt = cnt + onehot_t.sum(axis=1, keepdims=True)
        meta_t = jnp.where(row8 == META_RANK_LANE, rank, meta_t)
        meta = jnp.concatenate([meta_t, jnp.zeros((LANES - 8, sub), F32)], axis=0).T
        metat_ref[:, rows] = meta_t
        hpk_ref[rows, :] = jnp.concatenate([_pack_halves(h), lax.bitcast_convert_type(meta, jnp.uint32)], axis=1)
    cnt_ref[...] = jnp.broadcast_to(cnt, cnt_ref.shape)


def _proj_res(o, x, w_o, g_ffn, mod, layer, row0, tokens_per_row, w_router_t, b_router_t):
    t = x.shape[0]
    tm = TM_PROJ
    tok = lambda n: pl.BlockSpec((tm, n), lambda i: (i, 0))
    ms = lambda k: _mod_spec(layer, k, row0, tokens_per_row, tm)
    tri = jnp.asarray(np.triu(np.ones((TM_SUB, TM_SUB), np.float32)), BF16)
    return pl.pallas_call(
        _proj_res_kernel,
        grid=(t // tm,),
        in_specs=[tok(D_MODEL), tok(D_MODEL), _const_spec(w_o.shape), ms(2), _const_spec((1, D_MODEL)), ms(3), ms(4),
                  _const_spec(w_router_t.shape), _const_spec(b_router_t.shape), _const_spec((TM_SUB, TM_SUB))],
        out_specs=[tok(D_MODEL), tok(ROW_WORDS), pl.BlockSpec((8, tm), lambda i: (0, i)), _const_spec((8, LANES))],
        out_shape=[jax.ShapeDtypeStruct((t, D_MODEL), F32), jax.ShapeDtypeStruct((t, ROW_WORDS), jnp.uint32),
                   jax.ShapeDtypeStruct((8, t), F32), jax.ShapeDtypeStruct((8, LANES), F32)],
        compiler_params=_cparams(("arbitrary",)),
        name="proj_res",
    )(o, x, w_o, mod, g_ffn, mod, mod, w_router_t, b_router_t, tri)


def _moe_plan(meta_t, cnt):
    t = meta_t.shape[1]
    n_tiles = t // TM + N_GROUPS
    group = meta_t[META_GROUP_LANE].astype(jnp.int32)
    rank = meta_t[META_RANK_LANE].astype(jnp.int32)
    counts = cnt[:N_GROUPS, 0].astype(jnp.int32)
    tiles = (counts + TM - 1) // TM
    tile_end = jnp.cumsum(tiles)
    tile_start = tile_end - tiles
    pos = rank + TM * jnp.sum(jnp.where(group[:, None] == jnp.arange(N_GROUPS)[None], tile_start[None], 0), axis=1)
    tile_group = jnp.minimum(jnp.sum(jnp.arange(n_tiles)[:, None] >= tile_end[None], axis=1), N_GROUPS - 1)
    pad_bounds = jnp.stack([tile_start * TM + counts, tile_end * TM], axis=1).reshape(-1)
    return (pos.astype(jnp.int32), tile_group.astype(jnp.int32), tile_end[-1:].astype(jnp.int32),
            pad_bounds.astype(jnp.int32))


def _gather_rows(srcs, idx_ref, base, dsts, n_rows):
    for r in range(n_rows):
        s = idx_ref[base + r]
        for src_ref, dst_ref in zip(srcs, dsts):
            dst_ref[r:r + 1, :] = src_ref[pl.ds(s, 1), :]


def _moe_kernel(pos_ref, tg_ref, nt_ref, pad_ref, hpk_hbm, wg_ref, wu_ref, wd_ref, ypk_ref,
                src_ref, hpk_ref, hbuf, hid_ref, load_sem):
    i = pl.program_id(0)
    n_tok = hpk_ref.shape[0]

    @pl.when(i == 0)
    def _():
        load = pltpu.make_async_copy(hpk_hbm, hpk_ref, load_sem)
        load.start()
        def clear(p, c):
            src_ref[p] = 0
            return c
        for g in range(N_GROUPS):
            lax.fori_loop(pad_ref[2 * g], pad_ref[2 * g + 1], clear, 0)

        def fill(t, c):
            src_ref[pos_ref[t]] = t
            return c
        lax.fori_loop(0, n_tok, fill, 0, unroll=8)
        load.wait()
        _gather_rows((hpk_ref,), src_ref, 0, (hbuf.at[0],), TM)

    @pl.when(i < nt_ref[0])
    def _():
        slot = i % 2
        row = hbuf[slot]
        lo, hi = _unpack_halves(row[:, :D_MODEL // 2])
        h = jnp.concatenate([lo.astype(BF16), hi.astype(BF16)], axis=1)
        gates = lax.bitcast_convert_type(row[:, D_MODEL // 2:], F32)
        nxt = jnp.minimum(i + 1, nt_ref[0] - 1) * TM
        _gather_rows((hpk_ref,), src_ref, nxt, (hbuf.at[1 - slot],), TM)
        f = D_FF_EXPERT
        for j in range(EXPERTS_PER_GROUP):
            gate_act = _silu(jnp.dot(h, wg_ref[0, j], preferred_element_type=F32))
            hid = gate_act * jnp.dot(h, wu_ref[0, j], preferred_element_type=F32) * gates[:, j:j + 1]
            hid_ref[:, j * f:(j + 1) * f] = hid.astype(BF16)
        ypk_ref[...] = _pack_halves(jnp.dot(hid_ref[...], wd_ref[0], preferred_element_type=F32))

    @pl.when(i >= nt_ref[0])
    def _():
        ypk_ref[...] = jnp.zeros(ypk_ref.shape, ypk_ref.dtype)


def _moe(hpk, pos, tile_group, n_used, pad_bounds, wg, wu, wd, layer):
    t = hpk.shape[0]
    n_tiles = t // TM + N_GROUPS
    f = D_FF_EXPERT
    g0 = layer * N_GROUPS
    grid_spec = pltpu.PrefetchScalarGridSpec(
        num_scalar_prefetch=4,
        grid=(n_tiles,),
        in_specs=[
            pl.BlockSpec(memory_space=pl.ANY),
            pl.BlockSpec((1, EXPERTS_PER_GROUP, D_MODEL, f), lambda i, pos, tg, nt, pad: (g0 + tg[i], 0, 0, 0)),
            pl.BlockSpec((1, EXPERTS_PER_GROUP, D_MODEL, f), lambda i, pos, tg, nt, pad: (g0 + tg[i], 0, 0, 0)),
            pl.BlockSpec((1, EXPERTS_PER_GROUP * f, D_MODEL), lambda i, pos, tg, nt, pad: (g0 + tg[i], 0, 0)),
        ],
        out_specs=pl.BlockSpec((TM, D_MODEL // 2), lambda i, *_: (i, 0)),
        scratch_shapes=[pltpu.SMEM((n_tiles * TM,), jnp.int32),
                        pltpu.VMEM(hpk.shape, hpk.dtype),
                        pltpu.VMEM((2, TM, ROW_WORDS), jnp.uint32),
                        pltpu.VMEM((TM, EXPERTS_PER_GROUP * f), BF16),
                        pltpu.SemaphoreType.DMA(())],
    )
    return pl.pallas_call(
        _moe_kernel,
        grid_spec=grid_spec,
        out_shape=jax.ShapeDtypeStruct((n_tiles * TM, D_MODEL // 2), jnp.uint32),
        compiler_params=_cparams(("arbitrary",)),
        name="moe",
    )(pos, tile_group, n_used, pad_bounds, hpk, wg, wu, wd)


def _combine_kernel(*refs, final_norm):
    pos_ref, ypk_ref, x_ref, gate_ref = refs[:4]
    o_ref, ybuf = refs[-2], refs[-1]
    _gather_rows((ypk_ref,), pos_ref, pl.program_id(0) * TM, (ybuf,), TM)
    lo, hi = _unpack_halves(ybuf[...])
    out = x_ref[...] + gate_ref[0] * jnp.concatenate([lo, hi], axis=1)
    if final_norm:
        out = _rms(out, refs[4][...])
    o_ref[...] = out


def _combine(ypk, pos, x, mod, layer, row0, tokens_per_row, final_g):
    t = x.shape[0]

    def mod_map(i, pos):
        row = row0 + (i * TM) // tokens_per_row
        return ((layer * MOD_ROWS + row) * N_MOD + 5, 0, 0)

    in_specs = [pl.BlockSpec(ypk.shape, lambda i, pos: (0, 0), pipeline_mode=pl.Buffered(1)),
                pl.BlockSpec((TM, D_MODEL), lambda i, pos: (i, 0)),
                pl.BlockSpec((1, 1, D_MODEL), mod_map)]
    args = [ypk, x, mod]
    if final_g is not None:
        in_specs.append(pl.BlockSpec((1, D_MODEL), lambda i, pos: (0, 0)))
        args.append(final_g)
    grid_spec = pltpu.PrefetchScalarGridSpec(
        num_scalar_prefetch=1,
        grid=(t // TM,),
        in_specs=in_specs,
        out_specs=pl.BlockSpec((TM, D_MODEL), lambda i, pos: (i, 0)),
        scratch_shapes=[pltpu.VMEM((TM, D_MODEL // 2), jnp.uint32)],
    )
    return pl.pallas_call(
        functools.partial(_combine_kernel, final_norm=final_g is not None),
        grid_spec=grid_spec,
        out_shape=jax.ShapeDtypeStruct((t, D_MODEL), F32),
        compiler_params=_cparams(("arbitrary",)),
        name="moe_combine",
    )(pos, *args)


def _pad_heads(w, used):
    k = w.shape[0]
    w = w.reshape(k, MLA_HEADS, used)
    return jnp.pad(w, ((0, 0), (0, 0), (0, LANES - used))).reshape(k, MLA_HEADS * LANES)


def kernel(x_prompt, x_sample, state_na_k, state_na_v, state_mla_ckv, state_mla_kr, c, c_ctx, ada_w, ada_b, norm_mix_g, norm_ffn_g, final_norm_g, na_w_qkv, na_w_o, na_rpb, mla_w_in, mla_q_norm_g, mla_kv_norm_g, mla_w_uq, mla_w_uk, mla_w_uv, mla_w_o, moe_w_group, moe_b_group, moe_w_expert, moe_b_expert, moe_w_gate, moe_w_up, moe_w_down):
    n_b, seq, d = x_prompt.shape
    dec_b, n_lat, _ = x_sample.shape
    past = state_na_k.shape[2]
    rows = n_lat // GRID_W
    assert d == D_MODEL and dec_b + 1 <= MOD_ROWS and n_lat % TM_PROJ == 0 and (n_b * seq) % TM_PROJ == 0

    cond8 = jnp.concatenate([c_ctx[None], c, jnp.zeros((MOD_ROWS - 1 - dec_b, d), F32)], axis=0)
    mod = _ada_mod(cond8, ada_w, ada_b)

    xp = x_prompt.reshape(n_b * seq, d)
    xs = x_sample.reshape(dec_b * n_lat, d)
    p_rows = (0, n_b * seq)
    s_rows = (1, n_lat)
    outs = {}

    f = D_FF_EXPERT
    wg = _to_bf16(moe_w_gate).reshape(DEPTH * N_GROUPS, EXPERTS_PER_GROUP, d, f)
    wu = _to_bf16(moe_w_up).reshape(DEPTH * N_GROUPS, EXPERTS_PER_GROUP, d, f)
    wd = _to_bf16(moe_w_down).reshape(DEPTH * N_GROUPS, EXPERTS_PER_GROUP * f, d)

    for i in range(DEPTH):
        j = i // 2
        g_mix = norm_mix_g[i][None]
        g_ffn = norm_ffn_g[i][None]
        if i % 2 == 0:
            w_qkv = _to_bf16(na_w_qkv[j])
            w_vt = _transposed_bf16(na_w_qkv[j], 2 * d)
            qp, kp, vp, outs['na_k'], outs['na_v'] = _na_qkv(xp, g_mix, mod, i, *p_rows, w_qkv, n_b)
            qs, ks, vs_t = _na_qkv_t(xs, g_mix, mod, i, *s_rows, w_qkv, w_vt)
            op = _attention(qp, [(kp, vp)], n_b, wide_qk=False, pairs=8, nb=8)
            tables = _na_bias_tables(na_rpb[j], rows)
            os_ = _na_latent(qs, ks, vs_t, _state_k(state_na_k, j), _state_vt(state_na_v, j), tables, dec_b)
            w_o = _to_bf16(na_w_o[j])
        else:
            pad_in = MLA_IN_PAD - mla_w_in.shape[-1]
            w_mla = (jnp.pad(mla_w_in[j], ((0, 0), (0, pad_in))).astype(BF16),
                     mla_q_norm_g[j][None], mla_kv_norm_g[j][None],
                     _pad_heads(mla_w_uq[j], MLA_QK_DIM).astype(BF16),
                     _pad_heads(mla_w_uk[j], MLA_NOPE_DIM).astype(BF16),
                     mla_w_uv[j].astype(BF16))
            wuv_t = _transposed_bf16(mla_w_uv[j])
            wuq_t = _transposed_bf16(_pad_heads(mla_w_uq[j], MLA_QK_DIM))
            w_mla_t = w_mla[:3] + (wuq_t, w_mla[4], wuv_t)
            qp, kp, vp, ckv_p, kr_p = _mla_proj(xp, g_mix, mod, i, *p_rows, w_mla, None, True)
            qs, ks, vs_t = _mla_proj(xs, g_mix, mod, i, *s_rows, w_mla_t, _rope_lane_tables(n_lat), False)
            outs['ckv'] = ckv_p.reshape(n_b, 1, seq, MLA_KV_RANK)
            outs['kr'] = kr_p.reshape(n_b, 1, seq, MLA_ROPE_DIM)
            kr_state = jnp.pad(state_mla_kr[:, j].reshape(dec_b * past, MLA_ROPE_DIM),
                               ((0, 0), (0, LANES - MLA_ROPE_DIM)))
            kc, vc_t = _mla_state_kv(state_mla_ckv[:, j].reshape(dec_b * past, MLA_KV_RANK), kr_state,
                                     w_mla[4], wuv_t)
            op = _attention(qp, [(kp, vp)], n_b, wide_qk=True, pairs=8, nb=8)
            os_ = _attention_t(qs, ks, vs_t, kc, vc_t, dec_b, 1024, heads=8)
            w_o = _to_bf16(mla_w_o[j])

        pad_g = ((0, ROUTER_EXPERT_ROW - N_GROUPS), (0, 0))
        pad_e = ((0, ROUTER_ROWS - ROUTER_EXPERT_ROW - N_EXPERTS), (0, 0))
        w_router_t = jnp.concatenate([jnp.pad(moe_w_group[i].T, pad_g), jnp.pad(moe_w_expert[i].T, pad_e)],
                                     axis=0).astype(BF16)
        b_router_t = jnp.concatenate([jnp.pad(moe_b_group[i][:, None], pad_g),
                                      jnp.pad(moe_b_expert[i][:, None], pad_e)], axis=0)
        b_router_t = jnp.broadcast_to(b_router_t, (ROUTER_ROWS, TM_SUB))
        final_g = final_norm_g[None] if i == DEPTH - 1 else None
        new_x = []
        for o_att, x, rows_ in ((op, xp, p_rows), (os_, xs, s_rows)):
            xn, hpk, meta, cnt = _proj_res(o_att, x, w_o, g_ffn, mod, i, *rows_, w_router_t, b_router_t)
            pos, tile_group, n_used, pad_bounds = _moe_plan(meta, cnt)
            ypk = _moe(hpk, pos, tile_group, n_used, pad_bounds, wg, wu, wd, i)
            new_x.append(_combine(ypk, pos, xn, mod, i, *rows_, final_g))
        xp, xs = new_x

    return (xp.reshape(n_b, seq, d), xs.reshape(dec_b, n_lat, d),
            outs['na_k'], outs['na_v'], outs['ckv'], outs['kr'])
```

```python
import functools

import numpy as np
import jax
import jax.numpy as jnp
from jax import lax
from jax.experimental import pallas as pl
from jax.experimental.pallas import tpu as pltpu

F32 = jnp.float32
BF16 = jnp.bfloat16

D_MODEL = 1024
DEPTH = 2
GRID_W = 64
N_MOD = 6
EPS = 1e-6
NEG_INF = -1e30
NA_HEADS = 16
NA_HEAD_DIM = 64
NA_KH = 8
NA_KW = 16
NA_GROUP_ROWS = 4
NA_WIN_ROWS = 12
MLA_HEADS = 16
MLA_Q_RANK = 384
MLA_KV_RANK = 256
MLA_NOPE_DIM = 64
MLA_ROPE_DIM = 32
MLA_V_DIM = 64
MLA_QK_DIM = MLA_NOPE_DIM + MLA_ROPE_DIM
MLA_IN_PAD = 768
ROPE_THETA = 10000.0
LOG2_E = 1.4426950408889634
N_GROUPS = 4
EXPERTS_PER_GROUP = 4
N_EXPERTS = 16
D_FF_EXPERT = 256

LANES = 128
MOD_ROWS = 8
VMEM_LIMIT = 56 * 1024 * 1024
TM = 512
CAST_BLOCK_ELEMS = 512 * 1024
TM_MLA = 1024
ATTN_KEY_CHUNK = 512
TM_SUB_MLA = 512
TM_SUB = 512
TM_PROJ = 1024


def _cparams(sem):
    return pltpu.CompilerParams(dimension_semantics=sem, vmem_limit_bytes=VMEM_LIMIT)


def _const_spec(shape):
    nd = len(shape)
    return pl.BlockSpec(shape, lambda *_: (0,) * nd)


def _cast_kernel(x_ref, o_ref):
    o_ref[...] = x_ref[...].astype(o_ref.dtype)


def _to_bf16(x):
    cols = x.shape[-1]
    x2 = x.reshape(-1, cols)
    rows = x2.shape[0]
    block = rows
    while block * cols > CAST_BLOCK_ELEMS and block % 32 == 0:
        block //= 2
    out = pl.pallas_call(
        _cast_kernel,
        grid=(rows // block,),
        in_specs=[pl.BlockSpec((block, cols), lambda i: (i, 0))],
        out_specs=pl.BlockSpec((block, cols), lambda i: (i, 0)),
        out_shape=jax.ShapeDtypeStruct((rows, cols), BF16),
        compiler_params=_cparams(("arbitrary",)),
        name="to_bf16",
    )(x2)
    return out.reshape(x.shape)


def _transpose_kernel(x_ref, o_ref):
    o_ref[...] = x_ref[...].T.astype(o_ref.dtype)


def _transposed_bf16(x, col0=0, n_cols=None):
    rows = x.shape[0]
    n_cols = x.shape[1] - col0 if n_cols is None else n_cols
    bc = 2 * LANES
    assert col0 % bc == 0 and n_cols % bc == 0
    return pl.pallas_call(
        _transpose_kernel,
        grid=(n_cols // bc,),
        in_specs=[pl.BlockSpec((rows, bc), lambda i: (0, col0 // bc + i))],
        out_specs=pl.BlockSpec((bc, rows), lambda i: (i, 0)),
        out_shape=jax.ShapeDtypeStruct((n_cols, rows), BF16),
        compiler_params=_cparams(("arbitrary",)),
        name="transposed_bf16",
    )(x)


def _norm_mod(x, g, shift, scale):
    ms = jnp.mean(x * x, axis=-1, keepdims=True)
    y = x * lax.rsqrt(ms + EPS) * g
    return y * (1.0 + scale) + shift


def _rms(x, g):
    ms = jnp.mean(x * x, axis=-1, keepdims=True)
    return x * lax.rsqrt(ms + EPS) * g


def _silu(x):
    return x / (1.0 + jnp.exp(-x))


def _ada_kernel(cond_ref, w_ref, b_ref, o_ref):
    s = _silu(cond_ref[...]).astype(BF16)
    o_ref[0] = jnp.dot(s, w_ref[0].astype(BF16), preferred_element_type=F32) + b_ref[0]


def _ada_mod(cond8, ada_w, ada_b):
    tn = 1024
    n = N_MOD * D_MODEL
    out = pl.pallas_call(
        _ada_kernel,
        grid=(DEPTH, n // tn),
        in_specs=[
            _const_spec((MOD_ROWS, D_MODEL)),
            pl.BlockSpec((1, D_MODEL, tn), lambda l, j: (l, 0, j)),
            pl.BlockSpec((1, 1, tn), lambda l, j: (l, 0, j)),
        ],
        out_specs=pl.BlockSpec((1, MOD_ROWS, tn), lambda l, j: (l, 0, j)),
        out_shape=jax.ShapeDtypeStruct((DEPTH, MOD_ROWS, n), F32),
        compiler_params=_cparams(("arbitrary", "arbitrary")),
        name="ada_mod",
    )(cond8, ada_w, ada_b.reshape(DEPTH, 1, n))
    return out.reshape(DEPTH * MOD_ROWS * N_MOD, 1, D_MODEL)


def _mod_spec(layer, k, row0, tokens_per_row, tm=TM):
    def imap(i):
        row = row0 + (i * tm) // tokens_per_row
        return ((layer * MOD_ROWS + row) * N_MOD + k, 0, 0)
    return pl.BlockSpec((1, 1, D_MODEL), imap)


def _qkv_kernel(x_ref, g_ref, sh_ref, sc_ref, w_ref, q_ref, k_ref, v_ref, ks_ref, vs_ref):
    h = _norm_mod(x_ref[...], g_ref[...], sh_ref[0], sc_ref[0]).astype(BF16)
    y = jnp.dot(h, w_ref[...], preferred_element_type=F32)
    q_ref[...] = (y[:, :D_MODEL] * (NA_HEAD_DIM ** -0.5)).astype(q_ref.dtype)
    k = y[:, D_MODEL:2 * D_MODEL]
    v = y[:, 2 * D_MODEL:]
    k_ref[...] = k.astype(k_ref.dtype)
    v_ref[...] = v.astype(v_ref.dtype)
    ks_ref[...] = pltpu.einshape("t(hd)->thd", k, h=NA_HEADS).reshape(ks_ref.shape)
    vs_ref[...] = pltpu.einshape("t(hd)->thd", v, h=NA_HEADS).reshape(vs_ref.shape)


def _na_qkv(x, g, mod, layer, row0, tokens_per_row, w_qkv, n_batch):
    t = x.shape[0]
    seq = t // n_batch
    nb = TM // seq
    tok = pl.BlockSpec((TM, D_MODEL), lambda i: (i, 0))
    state = pl.BlockSpec((nb, 1, seq, NA_HEADS, NA_HEAD_DIM), lambda i: (i, 0, 0, 0, 0))
    state_shape = jax.ShapeDtypeStruct((n_batch, 1, seq, NA_HEADS, NA_HEAD_DIM), F32)
    return pl.pallas_call(
        _qkv_kernel,
        grid=(t // TM,),
        in_specs=[tok, _const_spec((1, D_MODEL)),
                  _mod_spec(layer, 0, row0, tokens_per_row), _mod_spec(layer, 1, row0, tokens_per_row),
                  _const_spec((D_MODEL, 3 * D_MODEL))],
        out_specs=[tok, tok, tok, state, state],
        out_shape=[jax.ShapeDtypeStruct((t, D_MODEL), BF16)] * 3 + [state_shape, state_shape],
        compiler_params=_cparams(("arbitrary",)),
        name="na_qkv",
    )(x, g, mod, mod, w_qkv)


def _store_head_slots(vt, vt_ref):
    n = vt.shape[1]
    n_heads = vt_ref.shape[0] // LANES
    dim = vt.shape[0] // n_heads
    tail = jnp.where(lax.broadcasted_iota(jnp.int32, (LANES - dim, n), 0) == 0, 1.0, 0.0).astype(vt_ref.dtype)
    for hh in range(n_heads):
        vt_ref[hh * LANES:hh * LANES + dim, :] = vt[hh * dim:(hh + 1) * dim, :].astype(vt_ref.dtype)
        vt_ref[hh * LANES + dim:(hh + 1) * LANES, :] = tail


def _qkv_t_kernel(x_ref, g_ref, sh_ref, sc_ref, wqk_ref, wvt_ref, q_ref, k_ref, vt_ref):
    h = _norm_mod(x_ref[...], g_ref[...], sh_ref[0], sc_ref[0]).astype(BF16)
    y = jnp.dot(h, wqk_ref[...], preferred_element_type=F32)
    q_ref[...] = (y[:, :D_MODEL] * (NA_HEAD_DIM ** -0.5 * LOG2_E)).astype(q_ref.dtype)
    k_ref[...] = y[:, D_MODEL:].astype(k_ref.dtype)
    _store_head_slots(_qk(wvt_ref[...], h), vt_ref)


def _na_qkv_t(x, g, mod, layer, row0, tokens_per_row, w_qkv, w_vt):
    t = x.shape[0]
    tm = TM_MLA
    tok = pl.BlockSpec((tm, D_MODEL), lambda i: (i, 0))
    slots = NA_HEADS * LANES
    return pl.pallas_call(
        _qkv_t_kernel,
        grid=(t // tm,),
        in_specs=[tok, _const_spec((1, D_MODEL)),
                  _mod_spec(layer, 0, row0, tokens_per_row, tm), _mod_spec(layer, 1, row0, tokens_per_row, tm),
                  pl.BlockSpec((D_MODEL, 2 * D_MODEL), lambda i: (0, 0)), _const_spec(w_vt.shape)],
        out_specs=[tok, tok, pl.BlockSpec((slots, tm), lambda i: (0, i))],
        out_shape=[jax.ShapeDtypeStruct((t, D_MODEL), BF16), jax.ShapeDtypeStruct((t, D_MODEL), BF16),
                   jax.ShapeDtypeStruct((slots, t), BF16)],
        compiler_params=_cparams(("arbitrary",)),
        name="na_qkv_t",
    )(x, g, mod, mod, w_qkv, w_vt)


def _softmax_pv(s_list, v_list):
    m = s_list[0].max(axis=-1, keepdims=True)
    for s in s_list[1:]:
        m = jnp.maximum(m, s.max(axis=-1, keepdims=True))
    l = None
    acc = None
    for s, v in zip(s_list, v_list):
        p = jnp.exp(s - m)
        ls = p.sum(axis=-1, keepdims=True)
        a = jnp.dot(p.astype(BF16), v, preferred_element_type=F32)
        l = ls if l is None else l + ls
        acc = a if acc is None else acc + a
    return acc * (1.0 / l)


def _qk(q, k):
    return lax.dot_general(q, k, (((1,), (1,)), ((), ())), preferred_element_type=F32)


def _head_masks():
    lane = lax.broadcasted_iota(jnp.int32, (1, LANES), 1)
    lo = lane < (LANES // 2)
    return lo, jnp.where(lo, 1.0, 0.0).astype(BF16), jnp.where(lo, 0.0, 1.0).astype(BF16)


def _attn_kernel(*refs, n_src, wide_qk, pairs, nb):
    q_ref, o_ref = refs[0], refs[-1]
    kv = refs[1:-1]
    lo, m0, m1 = _head_masks()
    wq = 2 * LANES if wide_qk else LANES
    tq = q_ref.shape[0] // nb
    for bb in range(nb):
        rq = slice(bb * tq, (bb + 1) * tq)
        rk = [slice(bb * (kv[2 * t].shape[0] // nb), (bb + 1) * (kv[2 * t].shape[0] // nb)) for t in range(n_src)]
        for p in range(pairs):
            q = q_ref[rq, p * wq:(p + 1) * wq]
            ks = [kv[2 * t][rk[t], p * wq:(p + 1) * wq].astype(BF16) for t in range(n_src)]
            vs = [kv[2 * t + 1][rk[t], p * LANES:(p + 1) * LANES].astype(BF16) for t in range(n_src)]
            outs = []
            for j in range(2):
                if wide_qk:
                    s_list = [_qk(q[:, j * LANES:(j + 1) * LANES], k[:, j * LANES:(j + 1) * LANES]) for k in ks]
                else:
                    s_list = [_qk(q * (m0 if j == 0 else m1), k) for k in ks]
                outs.append(_softmax_pv(s_list, vs))
            o_ref[rq, p * LANES:(p + 1) * LANES] = jnp.where(lo, outs[0], outs[1]).astype(o_ref.dtype)


def _attention(q, kv_srcs, n_batch, wide_qk, pairs, nb):
    wq = (2 * LANES if wide_qk else LANES) * pairs
    wv = LANES * pairs
    tq = q.shape[0] // n_batch * nb
    n_steps = D_MODEL // wv
    in_specs = [pl.BlockSpec((tq, wq), lambda b, p: (b, p))]
    args = [q]
    for k, v in kv_srcs:
        t_k = k.shape[0] // n_batch * nb
        in_specs.append(pl.BlockSpec((t_k, wq), lambda b, p: (b, p)))
        in_specs.append(pl.BlockSpec((t_k, wv), lambda b, p: (b, p)))
        args += [k, v]
    return pl.pallas_call(
        functools.partial(_attn_kernel, n_src=len(kv_srcs), wide_qk=wide_qk, pairs=pairs, nb=nb),
        grid=(n_batch // nb, n_steps),
        in_specs=in_specs,
        out_specs=pl.BlockSpec((tq, wv), lambda b, p: (b, p)),
        out_shape=jax.ShapeDtypeStruct((q.shape[0], D_MODEL), BF16),
        compiler_params=_cparams(("arbitrary", "arbitrary")),
        name="attention",
    )(*args)


def _col_max(s, chunk=64):
    acc = s[:chunk]
    for r in range(chunk, s.shape[0], chunk):
        acc = jnp.maximum(acc, s[r:r + chunk])
    return acc.max(axis=0, keepdims=True)


def _attn_t_kernel(q_ref, k1_ref, k2_ref, v1_ref, v2_ref, o_ref):
    n_heads = q_ref.shape[0] // LANES
    slots = [slice(j * LANES, (j + 1) * LANES) for j in range(n_heads)]
    t1 = k1_ref.shape[0]
    chunks = [(k1_ref, v1_ref, slice(r, min(r + ATTN_KEY_CHUNK, t1))) for r in range(0, t1, ATTN_KEY_CHUNK)]
    chunks.append((k2_ref, v2_ref, slice(0, k2_ref.shape[0])))
    items = [(sl, c) for sl in slots for c in range(len(chunks))]

    def scores(sl, c):
        k_ref, _, rows = chunks[c]
        return jnp.dot(k_ref[rows, sl], q_ref[sl, :], preferred_element_type=F32)

    nxt = scores(*items[0])
    outs = []
    for n, (sl, c) in enumerate(items):
        s = nxt
        if n + 1 < len(items):
            nxt = scores(*items[n + 1])
        _, vt_ref, rows = chunks[c]
        cm = _col_max(s)
        if c == 0:
            m = cm
            acc = jnp.dot(vt_ref[sl, rows], jnp.exp2(s - m).astype(BF16), preferred_element_type=F32)
        else:
            m_new = jnp.maximum(m, cm)
            acc = (acc * jnp.exp2(m - m_new)
                   + jnp.dot(vt_ref[sl, rows], jnp.exp2(s - m_new).astype(BF16), preferred_element_type=F32))
            m = m_new
        if c == len(chunks) - 1:
            outs.append(acc[:MLA_V_DIM] * (1.0 / acc[MLA_V_DIM:MLA_V_DIM + 1]))
    for j in range(0, n_heads, 2):
        pair = jnp.concatenate(outs[j:j + 2], axis=0).T
        o_ref[:, (j // 2) * LANES:(j // 2 + 1) * LANES] = pair.astype(o_ref.dtype)


def _attention_t(qt, k1, v1t, k2, v2t, n_batch, tq, heads):
    wq = heads * LANES
    wo = heads * MLA_V_DIM
    nq = qt.shape[1] // n_batch // tq
    t1 = k1.shape[0] // n_batch
    t2 = k2.shape[0] // n_batch
    return pl.pallas_call(
        _attn_t_kernel,
        grid=(n_batch, MLA_HEADS // heads, nq),
        in_specs=[pl.BlockSpec((wq, tq), lambda b, p, i: (p, b * nq + i)),
                  pl.BlockSpec((t1, wq), lambda b, p, i: (b, p)),
                  pl.BlockSpec((t2, wq), lambda b, p, i: (b, p)),
                  pl.BlockSpec((wq, t1), lambda b, p, i: (p, b)),
                  pl.BlockSpec((wq, t2), lambda b, p, i: (p, b))],
        out_specs=pl.BlockSpec((tq, wo), lambda b, p, i: (b * nq + i, p)),
        out_shape=jax.ShapeDtypeStruct((qt.shape[1], D_MODEL), BF16),
        compiler_params=_cparams(("arbitrary", "arbitrary", "arbitrary")),
        name="attention_t",
    )(qt, k1, k2, v1t, v2t)


def _na_group_layout(rows):
    n_groups = rows // NA_GROUP_ROWS
    kh = min(NA_KH, rows)
    bases, types, keys = [], [], {}
    for gi in range(n_groups):
        r0 = gi * NA_GROUP_ROWS
        base = int(np.clip(r0 - kh // 2, 0, rows - NA_WIN_ROWS))
        starts = tuple(int(np.clip(r - kh // 2, 0, rows - kh)) - base for r in range(r0, r0 + NA_GROUP_ROWS))
        key = (starts, r0 - base)
        if key not in keys:
            keys[key] = len(keys)
        bases.append(base)
        types.append(keys[key])
    return bases, types, list(keys.keys())


def _na_bias_tables(rpb, rows):
    kh = min(NA_KH, rows)
    _, _, type_keys = _na_group_layout(rows)
    qc = np.arange(GRID_W)
    kc = np.arange(GRID_W)
    col_start = np.clip(qc - NA_KW // 2, 0, GRID_W - NA_KW)
    col_valid = (kc[None, :] >= col_start[:, None]) & (kc[None, :] < col_start[:, None] + NA_KW)
    dx = kc[None, :] - qc[:, None] + (NA_KW - 1)
    onehot = ((dx[None] == np.arange(2 * NA_KW - 1)[:, None, None]) & col_valid[None]).astype(np.float32)
    band = jnp.einsum('hyd,dqk->hykq', rpb.astype(F32), onehot, precision=lax.Precision.HIGHEST)
    band = jnp.where(col_valid.T, band, NEG_INF)
    strip = band.reshape(NA_HEADS, (2 * NA_KH - 1) * GRID_W, GRID_W)
    strip2 = jnp.concatenate([strip, strip], axis=-1)
    layout = []
    for starts, r_off in type_keys:
        layout.append(tuple((starts[rq], starts[rq] - (rq + r_off) + (NA_KH - 1)) for rq in range(NA_GROUP_ROWS)))
    gq = NA_GROUP_ROWS * GRID_W
    gk = NA_WIN_ROWS * GRID_W
    return pl.pallas_call(
        functools.partial(_bias_table_kernel, layout=tuple(layout), kh=kh),
        grid=(NA_HEADS,),
        in_specs=[pl.BlockSpec((1,) + strip2.shape[1:], lambda h: (h, 0, 0))],
        out_specs=pl.BlockSpec((1, len(layout), gk, gq), lambda h: (h, 0, 0, 0)),
        out_shape=jax.ShapeDtypeStruct((NA_HEADS, len(layout), gk, gq), F32),
        compiler_params=_cparams(("arbitrary",)),
        name="na_bias_table",
    )(strip2)


def _bias_table_kernel(strip_ref, o_ref, *, layout, kh):
    lo = lax.broadcasted_iota(jnp.int32, (1, LANES), 1) < GRID_W
    neg = NEG_INF * LOG2_E

    def column(start, dy0):
        parts = []
        if start:
            parts.append(jnp.full((start * GRID_W, LANES), neg, F32))
        parts.append(strip_ref[0, dy0 * GRID_W:(dy0 + kh) * GRID_W, :] * LOG2_E)
        if NA_WIN_ROWS - kh - start:
            parts.append(jnp.full(((NA_WIN_ROWS - kh - start) * GRID_W, LANES), neg, F32))
        return jnp.concatenate(parts, axis=0)

    for ty, rows in enumerate(layout):
        for c in range(0, NA_GROUP_ROWS, 2):
            even, odd = column(*rows[c]), column(*rows[c + 1])
            o_ref[0, ty, :, c * GRID_W:(c + 2) * GRID_W] = jnp.where(lo, even, odd)


def _state_k_kernel(x_ref, o_ref):
    for h in range(NA_HEADS):
        o_ref[:, h * NA_HEAD_DIM:(h + 1) * NA_HEAD_DIM] = x_ref[0, 0, :, h, :].astype(o_ref.dtype)


def _state_vt_kernel(x_ref, o_ref, flat_ref):
    for h in range(NA_HEADS):
        flat_ref[:, h * NA_HEAD_DIM:(h + 1) * NA_HEAD_DIM] = x_ref[0, 0, :, h, :]
    _store_head_slots(flat_ref[...].T, o_ref)


def _state_k(x, layer):
    b, _, t, h, dh = x.shape
    return pl.pallas_call(
        _state_k_kernel,
        grid=(b,),
        in_specs=[pl.BlockSpec((1, 1, t, h, dh), lambda i: (i, layer, 0, 0, 0))],
        out_specs=pl.BlockSpec((t, h * dh), lambda i: (i, 0)),
        out_shape=jax.ShapeDtypeStruct((b * t, h * dh), BF16),
        compiler_params=_cparams(("arbitrary",)),
        name="state_k",
    )(x)


def _state_vt(x, layer):
    b, _, t, h, dh = x.shape
    return pl.pallas_call(
        _state_vt_kernel,
        grid=(b,),
        in_specs=[pl.BlockSpec((1, 1, t, h, dh), lambda i: (i, layer, 0, 0, 0))],
        out_specs=pl.BlockSpec((h * LANES, t), lambda i: (0, i)),
        out_shape=jax.ShapeDtypeStruct((h * LANES, b * t), BF16),
        scratch_shapes=[pltpu.VMEM((t, h * dh), F32)],
        compiler_params=_cparams(("arbitrary",)),
        name="state_vt",
    )(x)


def _na_latent_kernel(q_ref, k_ref, vt_ref, kc_ref, vct_ref, tab_ref, o_ref, *, bases, types):
    _, m0, m1 = _head_masks()
    kctx = kc_ref[...]
    gq = NA_GROUP_ROWS * GRID_W
    gk = NA_WIN_ROWS * GRID_W
    blocks = [(gi, base, ty, j) for gi, (base, ty) in enumerate(zip(bases, types)) for j in range(2)]

    def scores(gi, base, ty, j):
        qj = q_ref[gi * gq:(gi + 1) * gq, :] * (m0 if j == 0 else m1)
        kwin = k_ref[base * GRID_W:base * GRID_W + gk, :]
        return _qk(kwin, qj) + tab_ref[j, ty], _qk(kctx, qj)

    nxt = scores(*blocks[0])
    outs = []
    for n, (gi, base, ty, j) in enumerate(blocks):
        s_win, s_ctx = nxt
        if n + 1 < len(blocks):
            nxt = scores(*blocks[n + 1])
        m = jnp.maximum(_col_max(s_win), _col_max(s_ctx))
        p_win = jnp.exp2(s_win - m).astype(BF16)
        p_ctx = jnp.exp2(s_ctx - m).astype(BF16)
        sl = slice(j * LANES, (j + 1) * LANES)
        ot = (jnp.dot(vt_ref[sl, base * GRID_W:base * GRID_W + gk], p_win, preferred_element_type=F32)
              + jnp.dot(vct_ref[sl, :], p_ctx, preferred_element_type=F32))
        outs.append(ot[:NA_HEAD_DIM] * (1.0 / ot[NA_HEAD_DIM:NA_HEAD_DIM + 1]))
        if j == 1:
            o_ref[gi * gq:(gi + 1) * gq, :] = jnp.concatenate(outs, axis=0).T.astype(o_ref.dtype)
            outs = []


def _na_latent(q, k, vt, k_ctx, vt_ctx, tables, n_batch):
    t = q.shape[0] // n_batch
    rows = t // GRID_W
    t_ctx = k_ctx.shape[0] // n_batch
    bases, types, type_keys = _na_group_layout(rows)
    n_types = len(type_keys)
    gq = NA_GROUP_ROWS * GRID_W
    gk = NA_WIN_ROWS * GRID_W
    tok = pl.BlockSpec((t, LANES), lambda p, b: (b, p))
    ctx = pl.BlockSpec((t_ctx, LANES), lambda p, b: (b, p))
    return pl.pallas_call(
        functools.partial(_na_latent_kernel, bases=bases, types=types),
        grid=(D_MODEL // LANES, n_batch),
        in_specs=[tok, tok, pl.BlockSpec((2 * LANES, t), lambda p, b: (p, b)),
                  ctx, pl.BlockSpec((2 * LANES, t_ctx), lambda p, b: (p, b)),
                  pl.BlockSpec((2, n_types, gk, gq), lambda p, b: (p, 0, 0, 0))],
        out_specs=tok,
        out_shape=jax.ShapeDtypeStruct(q.shape, BF16),
        compiler_params=_cparams(("arbitrary", "arbitrary")),
        name="na_latent",
    )(q, k, vt, k_ctx, vt_ctx, tables)


def _rope_lane_tables(n_tokens):
    f32 = np.float32
    n_freq = MLA_ROPE_DIM // 4
    inv = f32(1.0) / (f32(ROPE_THETA) ** (np.arange(n_freq, dtype=f32) / f32(n_freq)))
    tt = np.arange(n_tokens)
    row = (tt // GRID_W).astype(f32)
    col = (tt % GRID_W).astype(f32)
    ang = np.concatenate([row[:, None] * inv, col[:, None] * inv], axis=-1).astype(f32)
    cos, sin = np.cos(ang).astype(f32), np.sin(ang).astype(f32)
    ones = np.ones((n_tokens, MLA_NOPE_DIM), f32)
    z16 = np.zeros((n_tokens, MLA_ROPE_DIM // 2), f32)
    z32 = np.zeros((n_tokens, LANES - MLA_QK_DIM), f32)
    z64 = np.zeros((n_tokens, MLA_NOPE_DIM), f32)
    c_tab = np.concatenate([ones, cos, cos, z32], axis=-1)
    s_dn = np.concatenate([z64, -sin, z16, z32], axis=-1)
    s_up = np.concatenate([z64, z16, sin, z32], axis=-1)
    return (jnp.asarray(c_tab), jnp.asarray(s_dn), jnp.asarray(s_up),
            jnp.asarray(np.ascontiguousarray(cos.T)), jnp.asarray(np.ascontiguousarray(sin.T)))


def _rope_slot(x, c_tab, s_dn, s_up):
    half = MLA_ROPE_DIM // 2
    return x * c_tab + pltpu.roll(x, LANES - half, 1) * s_dn + pltpu.roll(x, half, 1) * s_up


def _mla_kv(ckv_n, kr_slot, wuk_ref, wuv_ref, k_ref, v_ref, v_transposed, rows=slice(None)):
    c = ckv_n.astype(BF16)
    k_nope = jnp.dot(c, wuk_ref[...], preferred_element_type=F32)
    for h in range(MLA_HEADS):
        k_ref[rows, h * LANES:(h + 1) * LANES] = (k_nope[:, h * LANES:(h + 1) * LANES] + kr_slot).astype(k_ref.dtype)
    if v_transposed:
        _store_head_slots(_qk(wuv_ref[...], c), v_ref.at[:, rows])
    else:
        v_ref[rows, :] = jnp.dot(c, wuv_ref[...], preferred_element_type=F32).astype(v_ref.dtype)


def _mla_proj_kernel(*refs, rope, emit_state, q_scale):
    (x_ref, g_ref, sh_ref, sc_ref, win_ref, qg_ref, kvg_ref, wuq_ref, wuk_ref, wuv_ref) = refs[:10]
    pos = 10
    if rope:
        ct_ref, sd_ref, su_ref, cos_ref, sin_ref = refs[pos:pos + 5]
        pos += 5
    q_ref, k_ref, v_ref = refs[pos:pos + 3]
    pos += 3

    def front(rows):
        h = _norm_mod(x_ref[rows, :], g_ref[...], sh_ref[0], sc_ref[0]).astype(BF16)
        c = jnp.dot(h, win_ref[...], preferred_element_type=F32)
        cq = _rms(c[:, :MLA_Q_RANK], qg_ref[...]).astype(BF16)
        if rope:
            q = _qk(wuq_ref[...], cq) * q_scale
        else:
            q = jnp.dot(cq, wuq_ref[...], preferred_element_type=F32) * q_scale
        ckv = _rms(c[:, MLA_Q_RANK:MLA_Q_RANK + MLA_KV_RANK], kvg_ref[...])
        kr_chunk = c[:, MLA_Q_RANK + MLA_KV_RANK:]
        if emit_state:
            ckv_out, kr_out = refs[pos:pos + 2]
            ckv_out[rows, :] = ckv
            kr_out[rows, :] = kr_chunk[:, :MLA_ROPE_DIM]
        return q, ckv, pltpu.roll(kr_chunk, MLA_NOPE_DIM, 1)

    def back(rows, q, ckv, kr_slot):
        if rope:
            ct, sd, su = ct_ref[rows, :], sd_ref[rows, :], su_ref[rows, :]
            kr_slot = _rope_slot(kr_slot, ct, sd, su)
        _mla_kv(ckv, kr_slot, wuk_ref, wuv_ref, k_ref, v_ref, v_transposed=rope, rows=rows)
        if not rope:
            q_ref[rows, :] = q.astype(q_ref.dtype)
            return
        cs, sn = cos_ref[:, rows], sin_ref[:, rows]
        half = MLA_ROPE_DIM // 2
        for hh in range(MLA_HEADS):
            r0 = hh * LANES
            r1 = r0 + MLA_NOPE_DIM
            x1, x2 = q[r1:r1 + half], q[r1 + half:r1 + 2 * half]
            q_ref[r0:r1, rows] = q[r0:r1].astype(q_ref.dtype)
            q_ref[r1:r1 + half, rows] = (x1 * cs - x2 * sn).astype(q_ref.dtype)
            q_ref[r1 + half:r1 + 2 * half, rows] = (x2 * cs + x1 * sn).astype(q_ref.dtype)
            q_ref[r1 + 2 * half:r0 + LANES, rows] = q[r1 + 2 * half:r0 + LANES].astype(q_ref.dtype)

    tiles = [slice(r0, r0 + TM_SUB_MLA) for r0 in range(0, x_ref.shape[0], TM_SUB_MLA)]
    nxt = front(tiles[0])
    for n, rows in enumerate(tiles):
        cur = nxt
        if n + 1 < len(tiles):
            nxt = front(tiles[n + 1])
        back(rows, *cur)


def _mla_proj(x, g, mod, layer, row0, tokens_per_row, w, rope_tabs, emit_state):
    t = x.shape[0]
    win, qg, kvg, wuq, wuk, wuv = w
    tm = TM_MLA
    tok = lambda n: pl.BlockSpec((tm, n), lambda i: (i, 0))
    wq = MLA_HEADS * LANES
    in_specs = [tok(D_MODEL), _const_spec((1, D_MODEL)),
                _mod_spec(layer, 0, row0, tokens_per_row, tm), _mod_spec(layer, 1, row0, tokens_per_row, tm),
                _const_spec(win.shape), _const_spec(qg.shape), _const_spec(kvg.shape),
                _const_spec(wuq.shape), _const_spec(wuk.shape), _const_spec(wuv.shape)]
    args = [x, g, mod, mod, win, qg, kvg, wuq, wuk, wuv]
    rope = rope_tabs is not None
    if rope:
        n_tab = rope_tabs[0].shape[0] // tm
        in_specs += [pl.BlockSpec((tm, LANES), lambda i: (i % n_tab, 0)) for _ in range(3)]
        in_specs += [pl.BlockSpec((MLA_ROPE_DIM // 2, tm), lambda i: (0, i % n_tab)) for _ in range(2)]
        args += list(rope_tabs)
    if rope:
        q_spec, q_shape = pl.BlockSpec((wq, tm), lambda i: (0, i)), jax.ShapeDtypeStruct((wq, t), BF16)
        v_spec, v_shape = q_spec, q_shape
        q_scale = MLA_QK_DIM ** -0.5 * LOG2_E
    else:
        q_spec, q_shape = tok(wq), jax.ShapeDtypeStruct((t, wq), BF16)
        v_spec, v_shape = tok(D_MODEL), jax.ShapeDtypeStruct((t, D_MODEL), BF16)
        q_scale = MLA_QK_DIM ** -0.5
    out_specs = [q_spec, tok(wq), v_spec]
    out_shape = [q_shape, jax.ShapeDtypeStruct((t, wq), BF16), v_shape]
    if emit_state:
        out_specs += [tok(MLA_KV_RANK), tok(MLA_ROPE_DIM)]
        out_shape += [jax.ShapeDtypeStruct((t, MLA_KV_RANK), F32), jax.ShapeDtypeStruct((t, MLA_ROPE_DIM), F32)]
    return pl.pallas_call(
        functools.partial(_mla_proj_kernel, rope=rope, emit_state=emit_state, q_scale=q_scale),
        grid=(t // tm,),
        in_specs=in_specs, out_specs=out_specs, out_shape=out_shape,
        compiler_params=_cparams(("arbitrary",)),
        name="mla_proj",
    )(*args)


def _mla_state_kv_kernel(ckv_ref, kr_ref, wuk_ref, wuv_ref, k_ref, v_ref):
    kr_slot = pltpu.roll(kr_ref[...], MLA_NOPE_DIM, 1)
    _mla_kv(ckv_ref[...], kr_slot, wuk_ref, wuv_ref, k_ref, v_ref, v_transposed=True)


def _mla_state_kv(ckv, kr_pad, wuk, wuv_t):
    t = ckv.shape[0]
    wq = MLA_HEADS * LANES
    tok = lambda n: pl.BlockSpec((TM, n), lambda i: (i, 0))
    return pl.pallas_call(
        _mla_state_kv_kernel,
        grid=(t // TM,),
        in_specs=[tok(MLA_KV_RANK), tok(LANES), _const_spec(wuk.shape), _const_spec(wuv_t.shape)],
        out_specs=[tok(wq), pl.BlockSpec((wq, TM), lambda i: (0, i))],
        out_shape=[jax.ShapeDtypeStruct((t, wq), BF16), jax.ShapeDtypeStruct((wq, t), BF16)],
        compiler_params=_cparams(("arbitrary",)),
        name="mla_state_kv",
    )(ckv, kr_pad, wuk, wuv_t)


def _first_argmax_rows(v, row):
    top = v.max(axis=0, keepdims=True)
    idx = jnp.where(v == top, row, float(v.shape[0])).min(axis=0, keepdims=True)
    return top, idx


ROUTER_ROWS = 32
ROUTER_EXPERT_ROW = 8


def _route_t(lt):
    n = lt.shape[1]
    row8 = lax.broadcasted_iota(jnp.int32, (8, n), 0).astype(F32)
    row16 = lax.broadcasted_iota(jnp.int32, (N_EXPERTS, n), 0).astype(F32)
    is_g = row8 < N_GROUPS
    gl = jnp.where(is_g, lt[:8], -jnp.inf)
    gmax, gsel = _first_argmax_rows(gl, row8)
    p_group = 1.0 / jnp.where(is_g, jnp.exp(gl - gmax), 0.0).sum(axis=0, keepdims=True)
    first = gsel * EXPERTS_PER_GROUP
    e_all = lt[ROUTER_EXPERT_ROW:ROUTER_EXPERT_ROW + N_EXPERTS]
    el = jnp.where((row16 >= first) & (row16 < first + EXPERTS_PER_GROUP), e_all, -jnp.inf)
    t1, i1 = _first_argmax_rows(el, row16)
    t2, i2 = _first_argmax_rows(jnp.where(row16 == i1, -jnp.inf, el), row16)
    e2 = jnp.exp(t2 - t1)
    w1 = 1.0 / (1.0 + e2)
    gates = jnp.where(row16 == i1, w1 * p_group, jnp.where(row16 == i2, e2 * w1 * p_group, 0.0))
    rel = gates[:8] + gates[8:]
    rel = rel + pltpu.roll(rel, EXPERTS_PER_GROUP, 0)
    meta_t = jnp.where(row8 < EXPERTS_PER_GROUP, rel, jnp.where(row8 == META_GROUP_LANE, gsel, 0.0))
    onehot_t = jnp.where(row8 == gsel, 1.0, 0.0)
    return meta_t, onehot_t, row8, gsel


def _pack_halves(x):
    half = x.shape[1] // 2
    lo = lax.bitcast_convert_type(x[:, :half].astype(BF16).astype(F32), jnp.uint32)
    hi = lax.bitcast_convert_type(x[:, half:].astype(BF16).astype(F32), jnp.uint32)
    return (lo >> 16) | hi


def _unpack_halves(p):
    lo = lax.bitcast_convert_type(p << 16, F32)
    hi = lax.bitcast_convert_type(p & jnp.uint32(0xFFFF0000), F32)
    return lo, hi


ROW_WORDS = D_MODEL // 2 + LANES
META_GROUP_LANE = EXPERTS_PER_GROUP
META_RANK_LANE = EXPERTS_PER_GROUP + 1


def _proj_res_kernel(o_ref, x_ref, w_ref, gate_ref, g_ref, sh_ref, sc_ref, wrt_ref, brt_ref, tri_ref,
                     xn_ref, hpk_ref, metat_ref, cnt_ref):
    @pl.when(pl.program_id(0) == 0)
    def _():
        cnt_ref[...] = jnp.zeros(cnt_ref.shape, cnt_ref.dtype)

    sub = tri_ref.shape[0]
    cnt = cnt_ref[:, :1]
    tiles = [slice(r0, r0 + sub) for r0 in range(0, x_ref.shape[0], sub)]

    def out_proj(rows):
        return jnp.dot(o_ref[rows, :], w_ref[...], preferred_element_type=F32)

    y_next = out_proj(tiles[0])
    for n, rows in enumerate(tiles):
        y = y_next
        if n + 1 < len(tiles):
            y_next = out_proj(tiles[n + 1])
        xn = x_ref[rows, :] + gate_ref[0] * y
        xn_ref[rows, :] = xn
        h = _norm_mod(xn, g_ref[...], sh_ref[0], sc_ref[0])
        lt = _qk(wrt_ref[...], h.astype(BF16)) + brt_ref[...]
        meta_t, onehot_t, row8, gsel = _route_t(lt)
        prefix = jnp.dot(onehot_t.astype(BF16), tri_ref[...], preferred_element_type=F32)
        rank = jnp.where(row8 == gsel, prefix + cnt - 1.0, 0.0).sum(axis=0, keepdims=True)
        cnt = cnt + onehot_t.sum(axis=1, keepdims=True)
        meta_t = jnp.where(row8 == META_RANK_LANE, rank, meta_t)
        meta = jnp.concatenate([meta_t, jnp.zeros((LANES - 8, sub), F32)], axis=0).T
        metat_ref[:, rows] = meta_t
        hpk_ref[rows, :] = jnp.concatenate([_pack_halves(h), lax.bitcast_convert_type(meta, jnp.uint32)], axis=1)
    cnt_ref[...] = jnp.broadcast_to(cnt, cnt_ref.shape)


def _proj_res(o, x, w_o, g_ffn, mod, layer, row0, tokens_per_row, w_router_t, b_router_t):
    t = x.shape[0]
    tm = TM_PROJ
    tok = lambda n: pl.BlockSpec((tm, n), lambda i: (i, 0))
    ms = lambda k: _mod_spec(layer, k, row0, tokens_per_row, tm)
    tri = jnp.asarray(np.triu(np.ones((TM_SUB, TM_SUB), np.float32)), BF16)
    return pl.pallas_call(
        _proj_res_kernel,
        grid=(t // tm,),
        in_specs=[tok(D_MODEL), tok(D_MODEL), _const_spec(w_o.shape), ms(2), _const_spec((1, D_MODEL)), ms(3), ms(4),
                  _const_spec(w_router_t.shape), _const_spec(b_router_t.shape), _const_spec((TM_SUB, TM_SUB))],
        out_specs=[tok(D_MODEL), tok(ROW_WORDS), pl.BlockSpec((8, tm), lambda i: (0, i)), _const_spec((8, LANES))],
        out_shape=[jax.ShapeDtypeStruct((t, D_MODEL), F32), jax.ShapeDtypeStruct((t, ROW_WORDS), jnp.uint32),
                   jax.ShapeDtypeStruct((8, t), F32), jax.ShapeDtypeStruct((8, LANES), F32)],
        compiler_params=_cparams(("arbitrary",)),
        name="proj_res",
    )(o, x, w_o, mod, g_ffn, mod, mod, w_router_t, b_router_t, tri)


def _moe_plan(meta_t, cnt):
    t = meta_t.shape[1]
    n_tiles = t // TM + N_GROUPS
    group = meta_t[META_GROUP_LANE].astype(jnp.int32)
    rank = meta_t[META_RANK_LANE].astype(jnp.int32)
    counts = cnt[:N_GROUPS, 0].astype(jnp.int32)
    tiles = (counts + TM - 1) // TM
    tile_end = jnp.cumsum(tiles)
    tile_start = tile_end - tiles
    pos = rank + TM * jnp.sum(jnp.where(group[:, None] == jnp.arange(N_GROUPS)[None], tile_start[None], 0), axis=1)
    tile_group = jnp.minimum(jnp.sum(jnp.arange(n_tiles)[:, None] >= tile_end[None], axis=1), N_GROUPS - 1)
    pad_bounds = jnp.stack([tile_start * TM + counts, tile_end * TM], axis=1).reshape(-1)
    return (pos.astype(jnp.int32), tile_group.astype(jnp.int32), tile_end[-1:].astype(jnp.int32),
            pad_bounds.astype(jnp.int32))


def _gather_rows(srcs, idx_ref, base, dsts, n_rows):
    for r in range(n_rows):
        s = idx_ref[base + r]
        for src_ref, dst_ref in zip(srcs, dsts):
            dst_ref[r:r + 1, :] = src_ref[pl.ds(s, 1), :]


def _moe_kernel(pos_ref, tg_ref, nt_ref, pad_ref, hpk_hbm, wg_ref, wu_ref, wd_ref, ypk_ref,
                src_ref, hpk_ref, hbuf, hid_ref, load_sem):
    i = pl.program_id(0)
    n_tok = hpk_ref.shape[0]

    @pl.when(i == 0)
    def _():
        load = pltpu.make_async_copy(hpk_hbm, hpk_ref, load_sem)
        load.start()
        def clear(p, c):
            src_ref[p] = 0
            return c
        for g in range(N_GROUPS):
            lax.fori_loop(pad_ref[2 * g], pad_ref[2 * g + 1], clear, 0)

        def fill(t, c):
            src_ref[pos_ref[t]] = t
            return c
        lax.fori_loop(0, n_tok, fill, 0, unroll=8)
        load.wait()
        _gather_rows((hpk_ref,), src_ref, 0, (hbuf.at[0],), TM)

    @pl.when(i < nt_ref[0])
    def _():
        slot = i % 2
        row = hbuf[slot]
        lo, hi = _unpack_halves(row[:, :D_MODEL // 2])
        h = jnp.concatenate([lo.astype(BF16), hi.astype(BF16)], axis=1)
        gates = lax.bitcast_convert_type(row[:, D_MODEL // 2:], F32)
        nxt = jnp.minimum(i + 1, nt_ref[0] - 1) * TM
        _gather_rows((hpk_ref,), src_ref, nxt, (hbuf.at[1 - slot],), TM)
        f = D_FF_EXPERT
        for j in range(EXPERTS_PER_GROUP):
            gate_act = _silu(jnp.dot(h, wg_ref[0, j], preferred_element_type=F32))
            hid = gate_act * jnp.dot(h, wu_ref[0, j], preferred_element_type=F32) * gates[:, j:j + 1]
            hid_ref[:, j * f:(j + 1) * f] = hid.astype(BF16)
        ypk_ref[...] = _pack_halves(jnp.dot(hid_ref[...], wd_ref[0], preferred_element_type=F32))

    @pl.when(i >= nt_ref[0])
    def _():
        ypk_ref[...] = jnp.zeros(ypk_ref.shape, ypk_ref.dtype)


def _moe(hpk, pos, tile_group, n_used, pad_bounds, wg, wu, wd, layer):
    t = hpk.shape[0]
    n_tiles = t // TM + N_GROUPS
    f = D_FF_EXPERT
    g0 = layer * N_GROUPS
    grid_spec = pltpu.PrefetchScalarGridSpec(
        num_scalar_prefetch=4,
        grid=(n_tiles,),
        in_specs=[
            pl.BlockSpec(memory_space=pl.ANY),
            pl.BlockSpec((1, EXPERTS_PER_GROUP, D_MODEL, f), lambda i, pos, tg, nt, pad: (g0 + tg[i], 0, 0, 0)),
            pl.BlockSpec((1, EXPERTS_PER_GROUP, D_MODEL, f), lambda i, pos, tg, nt, pad: (g0 + tg[i], 0, 0, 0)),
            pl.BlockSpec((1, EXPERTS_PER_GROUP * f, D_MODEL), lambda i, pos, tg, nt, pad: (g0 + tg[i], 0, 0)),
        ],
        out_specs=pl.BlockSpec((TM, D_MODEL // 2), lambda i, *_: (i, 0)),
        scratch_shapes=[pltpu.SMEM((n_tiles * TM,), jnp.int32),
                        pltpu.VMEM(hpk.shape, hpk.dtype),
                        pltpu.VMEM((2, TM, ROW_WORDS), jnp.uint32),
                        pltpu.VMEM((TM, EXPERTS_PER_GROUP * f), BF16),
                        pltpu.SemaphoreType.DMA(())],
    )
    return pl.pallas_call(
        _moe_kernel,
        grid_spec=grid_spec,
        out_shape=jax.ShapeDtypeStruct((n_tiles * TM, D_MODEL // 2), jnp.uint32),
        compiler_params=_cparams(("arbitrary",)),
        name="moe",
    )(pos, tile_group, n_used, pad_bounds, hpk, wg, wu, wd)


def _combine_kernel(*refs, final_norm):
    pos_ref, ypk_ref, x_ref, gate_ref = refs[:4]
    o_ref, ybuf = refs[-2], refs[-1]
    _gather_rows((ypk_ref,), pos_ref, pl.program_id(0) * TM, (ybuf,), TM)
    lo, hi = _unpack_halves(ybuf[...])
    out = x_ref[...] + gate_ref[0] * jnp.concatenate([lo, hi], axis=1)
    if final_norm:
        out = _rms(out, refs[4][...])
    o_ref[...] = out


def _combine(ypk, pos, x, mod, layer, row0, tokens_per_row, final_g):
    t = x.shape[0]

    def mod_map(i, pos):
        row = row0 + (i * TM) // tokens_per_row
        return ((layer * MOD_ROWS + row) * N_MOD + 5, 0, 0)

    in_specs = [pl.BlockSpec(ypk.shape, lambda i, pos: (0, 0), pipeline_mode=pl.Buffered(1)),
                pl.BlockSpec((TM, D_MODEL), lambda i, pos: (i, 0)),
                pl.BlockSpec((1, 1, D_MODEL), mod_map)]
    args = [ypk, x, mod]
    if final_g is not None:
        in_specs.append(pl.BlockSpec((1, D_MODEL), lambda i, pos: (0, 0)))
        args.append(final_g)
    grid_spec = pltpu.PrefetchScalarGridSpec(
        num_scalar_prefetch=1,
        grid=(t // TM,),
        in_specs=in_specs,
        out_specs=pl.BlockSpec((TM, D_MODEL), lambda i, pos: (i, 0)),
        scratch_shapes=[pltpu.VMEM((TM, D_MODEL // 2), jnp.uint32)],
    )
    return pl.pallas_call(
        functools.partial(_combine_kernel, final_norm=final_g is not None),
        grid_spec=grid_spec,
        out_shape=jax.ShapeDtypeStruct((t, D_MODEL), F32),
        compiler_params=_cparams(("arbitrary",)),
        name="moe_combine",
    )(pos, *args)


def _pad_heads(w, used):
    k = w.shape[0]
    w = w.reshape(k, MLA_HEADS, used)
    return jnp.pad(w, ((0, 0), (0, 0), (0, LANES - used))).reshape(k, MLA_HEADS * LANES)


def kernel(x_prompt, x_sample, state_na_k, state_na_v, state_mla_ckv, state_mla_kr, c, c_ctx, ada_w, ada_b, norm_mix_g, norm_ffn_g, final_norm_g, na_w_qkv, na_w_o, na_rpb, mla_w_in, mla_q_norm_g, mla_kv_norm_g, mla_w_uq, mla_w_uk, mla_w_uv, mla_w_o, moe_w_group, moe_b_group, moe_w_expert, moe_b_expert, moe_w_gate, moe_w_up, moe_w_down):
    n_b, seq, d = x_prompt.shape
    dec_b, n_lat, _ = x_sample.shape
    past = state_na_k.shape[2]
    rows = n_lat // GRID_W
    assert d == D_MODEL and dec_b + 1 <= MOD_ROWS and n_lat % TM_PROJ == 0 and (n_b * seq) % TM_PROJ == 0

    cond8 = jnp.concatenate([c_ctx[None], c, jnp.zeros((MOD_ROWS - 1 - dec_b, d), F32)], axis=0)
    mod = _ada_mod(cond8, ada_w, ada_b)

    xp = x_prompt.reshape(n_b * seq, d)
    xs = x_sample.reshape(dec_b * n_lat, d)
    p_rows = (0, n_b * seq)
    s_rows = (1, n_lat)
    outs = {}

    f = D_FF_EXPERT
    wg = _to_bf16(moe_w_gate).reshape(DEPTH * N_GROUPS, EXPERTS_PER_GROUP, d, f)
    wu = _to_bf16(moe_w_up).reshape(DEPTH * N_GROUPS, EXPERTS_PER_GROUP, d, f)
    wd = _to_bf16(moe_w_down).reshape(DEPTH * N_GROUPS, EXPERTS_PER_GROUP * f, d)

    for i in range(DEPTH):
        j = i // 2
        g_mix = norm_mix_g[i][None]
        g_ffn = norm_ffn_g[i][None]
        if i % 2 == 0:
            w_qkv = _to_bf16(na_w_qkv[j])
            w_vt = _transposed_bf16(na_w_qkv[j], 2 * d)
            qp, kp, vp, outs['na_k'], outs['na_v'] = _na_qkv(xp, g_mix, mod, i, *p_rows, w_qkv, n_b)
            qs, ks, vs_t = _na_qkv_t(xs, g_mix, mod, i, *s_rows, w_qkv, w_vt)
            op = _attention(qp, [(kp, vp)], n_b, wide_qk=False, pairs=8, nb=4)
            tables = _na_bias_tables(na_rpb[j], rows)
            os_ = _na_latent(qs, ks, vs_t, _state_k(state_na_k, j), _state_vt(state_na_v, j), tables, dec_b)
            w_o = _to_bf16(na_w_o[j])
        else:
            pad_in = MLA_IN_PAD - mla_w_in.shape[-1]
            w_mla = (jnp.pad(mla_w_in[j], ((0, 0), (0, pad_in))).astype(BF16),
                     mla_q_norm_g[j][None], mla_kv_norm_g[j][None],
                     _pad_heads(mla_w_uq[j], MLA_QK_DIM).astype(BF16),
                     _pad_heads(mla_w_uk[j], MLA_NOPE_DIM).astype(BF16),
                     mla_w_uv[j].astype(BF16))
            wuv_t = _transposed_bf16(mla_w_uv[j])
            wuq_t = _transposed_bf16(_pad_heads(mla_w_uq[j], MLA_QK_DIM))
            w_mla_t = w_mla[:3] + (wuq_t, w_mla[4], wuv_t)
            qp, kp, vp, ckv_p, kr_p = _mla_proj(xp, g_mix, mod, i, *p_rows, w_mla, None, True)
            qs, ks, vs_t = _mla_proj(xs, g_mix, mod, i, *s_rows, w_mla_t, _rope_lane_tables(n_lat), False)
            outs['ckv'] = ckv_p.reshape(n_b, 1, seq, MLA_KV_RANK)
            outs['kr'] = kr_p.reshape(n_b, 1, seq, MLA_ROPE_DIM)
            kr_state = jnp.pad(state_mla_kr[:, j].reshape(dec_b * past, MLA_ROPE_DIM),
                               ((0, 0), (0, LANES - MLA_ROPE_DIM)))
            kc, vc_t = _mla_state_kv(state_mla_ckv[:, j].reshape(dec_b * past, MLA_KV_RANK), kr_state,
                                     w_mla[4], wuv_t)
            op = _attention(qp, [(kp, vp)], n_b, wide_qk=True, pairs=8, nb=4)
            os_ = _attention_t(qs, ks, vs_t, kc, vc_t, dec_b, 1024, heads=8)
            w_o = _to_bf16(mla_w_o[j])

        pad_g = ((0, ROUTER_EXPERT_ROW - N_GROUPS), (0, 0))
        pad_e = ((0, ROUTER_ROWS - ROUTER_EXPERT_ROW - N_EXPERTS), (0, 0))
        w_router_t = jnp.concatenate([jnp.pad(moe_w_group[i].T, pad_g), jnp.pad(moe_w_expert[i].T, pad_e)],
                                     axis=0).astype(BF16)
        b_router_t = jnp.concatenate([jnp.pad(moe_b_group[i][:, None], pad_g),
                                      jnp.pad(moe_b_expert[i][:, None], pad_e)], axis=0)
        b_router_t = jnp.broadcast_to(b_router_t, (ROUTER_ROWS, TM_SUB))
        final_g = final_norm_g[None] if i == DEPTH - 1 else None
        new_x = []
        for o_att, x, rows_ in ((op, xp, p_rows), (os_, xs, s_rows)):
            xn, hpk, meta, cnt = _proj_res(o_att, x, w_o, g_ffn, mod, i, *rows_, w_router_t, b_router_t)
            pos, tile_group, n_used, pad_bounds = _moe_plan(meta, cnt)
            ypk = _moe(hpk, pos, tile_group, n_used, pad_bounds, wg, wu, wd, i)
            new_x.append(_combine(ypk, pos, xn, mod, i, *rows_, final_g))
        xp, xs = new_x

    return (xp.reshape(n_b, seq, d), xs.reshape(dec_b, n_lat, d),
            outs['na_k'], outs['na_v'], outs['ckv'], outs['kr'])
```
